```python
import math
import jax, jax.numpy as jnp
from jax import lax
import numpy as np

D_MODEL = 1024
BATCH = 8
SEQ = 8192
DEPTH = 1

ATTN_HEADS = 8
ATTN_HEAD_DIM = 64
ATTN_WIDTH = ATTN_HEADS * ATTN_HEAD_DIM
DILATED_PATTERNS = ((128, 1), (512, 4), (2048, 16))
ATTN_BLOCK = 128
N_BUCKETS = 32
MAX_DISTANCE = 2048
NEG_INF = -1e30
HGRN_HEADS = 8
HGRN_KEY_DIM = 128
HGRN_VAL_DIM = 128
HGRN_FDIM = HGRN_HEADS * HGRN_KEY_DIM
HGRN_WIDTH = HGRN_HEADS * HGRN_VAL_DIM
HGRN_CHUNK = 64
IN_WIDTH = 4 * ATTN_WIDTH + 2 * HGRN_FDIM + 2 * HGRN_WIDTH + 2 * D_MODEL
EPS = 1e-6

kernel_name = "hybrid_dilated_attn_hgrn2_gated_merge"


def rmsnorm(x, g):
    xf = x.astype(jnp.float32)
    y = xf * lax.rsqrt(jnp.mean(xf * xf, axis=-1, keepdims=True) + EPS)
    return (y * g.astype(jnp.float32)).astype(x.dtype)


def t5_bucket(dist):
    max_exact = N_BUCKETS // 2
    n = dist.astype(jnp.float32)
    large = max_exact + (jnp.log(jnp.maximum(n, 1.0) / max_exact)
                         / math.log(MAX_DISTANCE / max_exact)
                         * (N_BUCKETS - max_exact)).astype(jnp.int32)
    large = jnp.minimum(large, N_BUCKETS - 1)
    return jnp.where(dist < max_exact, dist, large)


def dilated_pattern(q, k, v, rel_bias, window, dilation):
    B, S, H, E = q.shape
    L = S // dilation
    span = window // dilation
    nb = -(-L // ATTN_BLOCK)
    Lp = nb * ATTN_BLOCK

    def to_sub(t):
        t = t.reshape(B, L, dilation, H, E).transpose(0, 2, 3, 1, 4)
        return jnp.pad(t, ((0, 0), (0, 0), (0, 0), (0, Lp - L), (0, 0)))

    def kv_blocks(t):
        tp = jnp.pad(to_sub(t), ((0, 0), (0, 0), (0, 0), (ATTN_BLOCK, 0), (0, 0)))
        prev = tp[:, :, :, :Lp].reshape(B, dilation, H, nb, ATTN_BLOCK, E)
        cur = tp[:, :, :, ATTN_BLOCK:].reshape(B, dilation, H, nb, ATTN_BLOCK, E)
        return jnp.concatenate([prev, cur], axis=4)

    qs = to_sub(q).reshape(B, dilation, H, nb, ATTN_BLOCK, E)
    ks, vs = kv_blocks(k), kv_blocks(v)

    qi = jnp.arange(ATTN_BLOCK)[:, None]
    kj = jnp.arange(2 * ATTN_BLOCK)[None, :]
    delta = qi + ATTN_BLOCK - kj
    band = (delta >= 0) & (delta <= span)
    key_pos = jnp.arange(nb)[:, None, None] * ATTN_BLOCK + kj[None] - ATTN_BLOCK
    mask = band[None] & (key_pos >= 0)
    bucket = t5_bucket(jnp.clip(delta, 0, None) * dilation)
    bias = rel_bias.astype(jnp.float32)[bucket].transpose(2, 0, 1)

    s = jnp.einsum('bdhnqe,bdhnke->bdhnqk', qs, ks) * (E ** -0.5) + bias[None, None, :, None]
    s = jnp.where(mask, s, NEG_INF)
    m = jnp.max(s, axis=-1, keepdims=True)
    p = jnp.exp(s - m)
    den = jnp.sum(p, axis=-1, keepdims=True)
    o = jnp.einsum('bdhnqk,bdhnke->bdhnqe', p, vs) / den
    lse = (m + jnp.log(den))[..., 0]

    o = o.reshape(B, dilation, H, Lp, E)[:, :, :, :L].transpose(0, 3, 1, 2, 4).reshape(B, S, H, E)
    lse = lse.reshape(B, dilation, H, Lp)[..., :L].transpose(0, 3, 1, 2).reshape(B, S, H)
    return o, lse


def dilated_attention(q, k, v, rel_bias):
    outs, lses = [], []
    for window, dilation in DILATED_PATTERNS:
        o, lse = dilated_pattern(q, k, v, rel_bias, window, dilation)
        outs.append(o)
        lses.append(lse)
    w = jax.nn.softmax(jnp.stack(lses, 0), axis=0)
    return jnp.einsum('gbsh,gbshe->bshe', w, jnp.stack(outs, 0))


def hgrn2_recurrence(q, f_raw, i, lb):
    B, S, H, DK = q.shape
    DV = i.shape[-1]
    C = HGRN_CHUNK
    nc = S // C
    f = lb + (1.0 - lb) * jax.nn.sigmoid(f_raw)
    g = jnp.log(f)
    k = 1.0 - f

    def chunks(t):
        return t.reshape(B, nc, C, H, t.shape[-1]).transpose(1, 0, 3, 2, 4)

    causal = jnp.tril(jnp.ones((C, C), dtype=bool))

    def step(state, inp):
        qc, kc, vc, gc = inp
        b = jnp.cumsum(gc, axis=2)
        o_inter = jnp.einsum('bhtk,bhkv->bhtv', qc * jnp.exp(b), state)
        diff = b[:, :, :, None, :] - b[:, :, None, :, :]
        decay = jnp.exp(jnp.where(causal[:, :, None], diff, -jnp.inf))
        a = jnp.einsum('bhtk,bhsk,bhtsk->bhts', qc, kc, decay)
        o_intra = jnp.einsum('bhts,bhsv->bhtv', a, vc)
        b_last = b[:, :, -1:, :]
        new_state = (jnp.exp(b_last[:, :, 0, :])[..., None] * state
                     + jnp.einsum('bhsk,bhsv->bhkv', kc * jnp.exp(b_last - b), vc))
        return new_state, o_inter + o_intra

    s0 = jnp.zeros((B, H, DK, DV), jnp.float32)
    _, o = lax.scan(step, s0, (chunks(q), chunks(k), chunks(i), chunks(g)))
    return o.transpose(1, 0, 3, 2, 4).reshape(B, S, H, DV)


def _fwd_setup_inputs(seed: int = 0) -> dict:
    key = jax.random.key(seed)
    ks = jax.random.split(key, 13)
    D = D_MODEL
    nrm = lambda k, shape, fan_in: jax.random.normal(k, shape, jnp.float32) * fan_in ** -0.5
    return {
        "x": jax.random.normal(ks[0], (BATCH, SEQ, D), jnp.float32),
        "c": jax.random.normal(ks[1], (BATCH, D), jnp.float32),
        "w_ada": nrm(ks[2], (DEPTH, D, 3 * D), D),
        "b_ada": 0.02 * jax.random.normal(ks[3], (DEPTH, 3 * D), jnp.float32),
        "norm_g": 1.0 + 0.05 * jax.random.normal(ks[4], (DEPTH, D), jnp.float32),
        "w_in": nrm(ks[5], (DEPTH, D, IN_WIDTH), D),
        "hgrn_onorm_g": 1.0 + 0.05 * jax.random.normal(ks[6], (DEPTH, HGRN_VAL_DIM), jnp.float32),
        "w_branch_a": nrm(ks[7], (DEPTH, ATTN_WIDTH, D), ATTN_WIDTH),
        "w_branch_b": nrm(ks[8], (DEPTH, HGRN_WIDTH, D), HGRN_WIDTH),
        "w_out": nrm(ks[9], (DEPTH, D, D), D),
        "rel_bias": 0.5 * jax.random.normal(ks[10], (N_BUCKETS, ATTN_HEADS), jnp.float32),
        "hgrn_lb": 0.5 * jax.random.normal(ks[11], (DEPTH + 1, HGRN_FDIM), jnp.float32),
        "final_g": 1.0 + 0.05 * jax.random.normal(ks[12], (D,), jnp.float32),
    }


def _fwd_reference(x, c, w_ada, b_ada, norm_g, w_in, hgrn_onorm_g, w_branch_a, w_branch_b,
              w_out, rel_bias, hgrn_lb, final_g):
    B, S, D = x.shape
    sizes = [ATTN_WIDTH] * 4 + [HGRN_FDIM, HGRN_FDIM, HGRN_WIDTH, HGRN_WIDTH, D_MODEL, D_MODEL]
    cuts = [int(v) for v in np.cumsum(sizes)[:-1]]
    lower_bounds = jnp.cumsum(jax.nn.softmax(hgrn_lb.astype(jnp.float32), axis=0), axis=0)
    for l in range(DEPTH):
        mod = jax.nn.silu(c) @ w_ada[l] + b_ada[l]
        shift, scale, gate = jnp.split(mod, 3, axis=-1)
        h = rmsnorm(x, norm_g[l]) * (1.0 + scale[:, None]) + shift[:, None]
        proj = h @ w_in[l]
        q_a, k_a, v_a, z_a, q_b, f_b, i_b, z_b, g_a, g_b = jnp.split(proj, cuts, axis=-1)

        heads_a = lambda t: t.astype(jnp.float32).reshape(B, S, ATTN_HEADS, ATTN_HEAD_DIM)
        o_a = dilated_attention(heads_a(q_a), heads_a(k_a), heads_a(v_a), rel_bias)
        o_a = o_a.reshape(B, S, ATTN_WIDTH).astype(x.dtype) * jax.nn.silu(z_a)

        heads_b = lambda t, e: t.astype(jnp.float32).reshape(B, S, HGRN_HEADS, e)
        lb = lower_bounds[l].reshape(HGRN_HEADS, HGRN_KEY_DIM)
        o_b = hgrn2_recurrence(jax.nn.silu(heads_b(q_b, HGRN_KEY_DIM)), heads_b(f_b, HGRN_KEY_DIM),
                               heads_b(i_b, HGRN_VAL_DIM), lb)
        o_b = rmsnorm(o_b, hgrn_onorm_g[l]).reshape(B, S, HGRN_WIDTH).astype(x.dtype) * jax.nn.silu(z_b)

        y = jax.nn.sigmoid(g_a) * (o_a @ w_branch_a[l]) + jax.nn.sigmoid(g_b) * (o_b @ w_branch_b[l])
        x = x + gate[:, None] * (y @ w_out[l])
    return rmsnorm(x, final_g)


import jax as _jax
import jax.numpy as _jnp

TWIN_FORMAT = 'train_step'
FWD_PARAMS = ['x', 'c', 'w_ada', 'b_ada', 'norm_g', 'w_in', 'hgrn_onorm_g', 'w_branch_a', 'w_branch_b', 'w_out', 'rel_bias', 'hgrn_lb', 'final_g']
TWIN_WEIGHTS = ['w_ada', 'b_ada', 'norm_g', 'w_in', 'hgrn_onorm_g', 'w_branch_a', 'w_branch_b', 'w_out', 'rel_bias', 'hgrn_lb', 'final_g']
TWIN_DIFF_INPUT = 'x'
TWIN_INPUTS = ['x', 'c', 'w_ada', 'b_ada', 'norm_g', 'w_in', 'hgrn_onorm_g', 'w_branch_a', 'w_branch_b', 'w_out', 'rel_bias', 'hgrn_lb', 'final_g', 'loss_target', 'm_w_ada', 'm_b_ada', 'm_norm_g', 'm_w_in', 'm_hgrn_onorm_g', 'm_w_branch_a', 'm_w_branch_b', 'm_w_out', 'm_rel_bias', 'm_hgrn_lb', 'm_final_g', 'v_w_ada', 'v_b_ada', 'v_norm_g', 'v_w_in', 'v_hgrn_onorm_g', 'v_w_branch_a', 'v_w_branch_b', 'v_w_out', 'v_rel_bias', 'v_hgrn_lb', 'v_final_g']
TWIN_OUTPUTS = ['loss', 'grad_x', 'grad_w_ada', 'grad_b_ada', 'grad_norm_g', 'grad_w_in', 'grad_hgrn_onorm_g', 'grad_w_branch_a', 'grad_w_branch_b', 'grad_w_out', 'grad_rel_bias', 'grad_hgrn_lb', 'grad_final_g', 'delta_w_ada', 'delta_b_ada', 'delta_norm_g', 'delta_w_in', 'delta_hgrn_onorm_g', 'delta_w_branch_a', 'delta_w_branch_b', 'delta_w_out', 'delta_rel_bias', 'delta_hgrn_lb', 'delta_final_g', 'new_m_w_ada', 'new_m_b_ada', 'new_m_norm_g', 'new_m_w_in', 'new_m_hgrn_onorm_g', 'new_m_w_branch_a', 'new_m_w_branch_b', 'new_m_w_out', 'new_m_rel_bias', 'new_m_hgrn_lb', 'new_m_final_g', 'new_v_w_ada', 'new_v_b_ada', 'new_v_norm_g', 'new_v_w_in', 'new_v_hgrn_onorm_g', 'new_v_w_branch_a', 'new_v_w_branch_b', 'new_v_w_out', 'new_v_rel_bias', 'new_v_hgrn_lb', 'new_v_final_g']
TWIN_LEAF_KINDS = {'loss': 'loss', 'grad_x': 'grad_x', 'grad_w_ada': 'grad_w', 'grad_b_ada': 'grad_w', 'grad_norm_g': 'grad_w', 'grad_w_in': 'grad_w', 'grad_hgrn_onorm_g': 'grad_w', 'grad_w_branch_a': 'grad_w', 'grad_w_branch_b': 'grad_w', 'grad_w_out': 'grad_w', 'grad_rel_bias': 'grad_w', 'grad_hgrn_lb': 'grad_w', 'grad_final_g': 'grad_w', 'delta_w_ada': 'delta_w', 'delta_b_ada': 'delta_w', 'delta_norm_g': 'delta_w', 'delta_w_in': 'delta_w', 'delta_hgrn_onorm_g': 'delta_w', 'delta_w_branch_a': 'delta_w', 'delta_w_branch_b': 'delta_w', 'delta_w_out': 'delta_w', 'delta_rel_bias': 'delta_w', 'delta_hgrn_lb': 'delta_w', 'delta_final_g': 'delta_w', 'new_m_w_ada': 'new_m', 'new_m_b_ada': 'new_m', 'new_m_norm_g': 'new_m', 'new_m_w_in': 'new_m', 'new_m_hgrn_onorm_g': 'new_m', 'new_m_w_branch_a': 'new_m', 'new_m_w_branch_b': 'new_m', 'new_m_w_out': 'new_m', 'new_m_rel_bias': 'new_m', 'new_m_hgrn_lb': 'new_m', 'new_m_final_g': 'new_m', 'new_v_w_ada': 'new_v', 'new_v_b_ada': 'new_v', 'new_v_norm_g': 'new_v', 'new_v_w_in': 'new_v', 'new_v_hgrn_onorm_g': 'new_v', 'new_v_w_branch_a': 'new_v', 'new_v_w_branch_b': 'new_v', 'new_v_w_out': 'new_v', 'new_v_rel_bias': 'new_v', 'new_v_hgrn_lb': 'new_v', 'new_v_final_g': 'new_v'}


def _forward(args):
    return _fwd_reference(*[args[k] for k in FWD_PARAMS])


def _output_shape():
    def fwd():
        inp = _fwd_setup_inputs(0)
        return _fwd_reference(*[inp[k] for k in FWD_PARAMS])
    out = _jax.eval_shape(fwd)
    return out.shape, out.dtype

N_MICROBATCH = 1
ADAM_LR = 0.001
ADAM_B1 = 0.9
ADAM_B2 = 0.999
ADAM_EPS = 1e-08
ADAM_WD = 0.01
ADAM_STEP = 10
PER_EXAMPLE_BATCH_AXIS = {'x': 0, 'c': 0, 'loss_target': 0}
SHARED_INPUTS = []
_WEIGHT_DTYPES = {'w_ada': _jnp.float32, 'b_ada': _jnp.float32, 'norm_g': _jnp.float32, 'w_in': _jnp.float32, 'hgrn_onorm_g': _jnp.float32, 'w_branch_a': _jnp.float32, 'w_branch_b': _jnp.float32, 'w_out': _jnp.float32, 'rel_bias': _jnp.float32, 'hgrn_lb': _jnp.float32, 'final_g': _jnp.float32}
MOMENT_SCALE = {'w_ada': 1.107501e-01, 'b_ada': 2.252262e-01, 'norm_g': 1.009967e-01, 'w_in': 4.736295e-02, 'hgrn_onorm_g': 2.020526e-01, 'w_branch_a': 4.962047e-02, 'w_branch_b': 7.444263e-02, 'w_out': 8.851070e-02, 'rel_bias': 3.369410e-02, 'hgrn_lb': 3.793836e-03, 'final_g': 6.428724e+01}


def _to_microbatches(a, axis):
    t = _jnp.moveaxis(a, axis, 0)
    t = t.reshape((N_MICROBATCH, t.shape[0] // N_MICROBATCH) + t.shape[1:])
    return _jnp.moveaxis(t, 1, axis + 1)


def setup_inputs(seed: int = 0) -> dict:
    inp = _fwd_setup_inputs(seed)
    key = _jax.random.fold_in(_jax.random.key(seed), 7919)
    shape, _ = _output_shape()
    out = dict(inp)
    out["loss_target"] = _jax.random.normal(_jax.random.fold_in(key, 0), shape, _jnp.float32)
    for i, name in enumerate(TWIN_WEIGHTS):
        w = inp[name].astype(_jnp.float32)
        if MOMENT_SCALE is None:
            s = _jnp.sqrt(_jnp.mean(_jnp.square(w)) + 1e-30)
        else:
            s = MOMENT_SCALE[name]
        km, kv = _jax.random.split(_jax.random.fold_in(key, i + 1))
        out[name] = w
        out["m_" + name] = s * _jax.random.normal(km, w.shape, _jnp.float32)
        out["v_" + name] = (s * s) * _jax.random.uniform(kv, w.shape, _jnp.float32, 0.5, 1.5)
    if N_MICROBATCH > 1:
        for name, axis in PER_EXAMPLE_BATCH_AXIS.items():
            out[name] = _to_microbatches(out[name], axis)
    return {'x': out['x'], 'c': out['c'], 'w_ada': out['w_ada'], 'b_ada': out['b_ada'], 'norm_g': out['norm_g'], 'w_in': out['w_in'], 'hgrn_onorm_g': out['hgrn_onorm_g'], 'w_branch_a': out['w_branch_a'], 'w_branch_b': out['w_branch_b'], 'w_out': out['w_out'], 'rel_bias': out['rel_bias'], 'hgrn_lb': out['hgrn_lb'], 'final_g': out['final_g'], 'loss_target': out['loss_target'], 'm_w_ada': out['m_w_ada'], 'm_b_ada': out['m_b_ada'], 'm_norm_g': out['m_norm_g'], 'm_w_in': out['m_w_in'], 'm_hgrn_onorm_g': out['m_hgrn_onorm_g'], 'm_w_branch_a': out['m_w_branch_a'], 'm_w_branch_b': out['m_w_branch_b'], 'm_w_out': out['m_w_out'], 'm_rel_bias': out['m_rel_bias'], 'm_hgrn_lb': out['m_hgrn_lb'], 'm_final_g': out['m_final_g'], 'v_w_ada': out['v_w_ada'], 'v_b_ada': out['v_b_ada'], 'v_norm_g': out['v_norm_g'], 'v_w_in': out['v_w_in'], 'v_hgrn_onorm_g': out['v_hgrn_onorm_g'], 'v_w_branch_a': out['v_w_branch_a'], 'v_w_branch_b': out['v_w_branch_b'], 'v_w_out': out['v_w_out'], 'v_rel_bias': out['v_rel_bias'], 'v_hgrn_lb': out['v_hgrn_lb'], 'v_final_g': out['v_final_g']}


def _loss(weights, diff, rest, loss_target):
    with _jax.named_scope("forward"):
        args = {**rest, TWIN_DIFF_INPUT: diff, **{k: w.astype(_WEIGHT_DTYPES[k]) for k, w in weights.items()}}
        y = _forward(args)
    with _jax.named_scope("loss_head"):
        err = _jnp.square(y.astype(_jnp.float32) - loss_target)
        return 0.5 * _jnp.sum(_jnp.mean(err, axis=-1)) if err.ndim else 0.5 * err


def _adamw(w, g, m, v):
    m = ADAM_B1 * m + (1.0 - ADAM_B1) * g
    v = ADAM_B2 * v + (1.0 - ADAM_B2) * _jnp.square(g)
    m_hat = m / (1.0 - ADAM_B1 ** ADAM_STEP)
    v_hat = v / (1.0 - ADAM_B2 ** ADAM_STEP)
    delta = -ADAM_LR * (m_hat / (_jnp.sqrt(v_hat) + ADAM_EPS) + ADAM_WD * w)
    return delta, m, v


def reference(x, c, w_ada, b_ada, norm_g, w_in, hgrn_onorm_g, w_branch_a, w_branch_b, w_out, rel_bias, hgrn_lb, final_g, loss_target, m_w_ada, m_b_ada, m_norm_g, m_w_in, m_hgrn_onorm_g, m_w_branch_a, m_w_branch_b, m_w_out, m_rel_bias, m_hgrn_lb, m_final_g, v_w_ada, v_b_ada, v_norm_g, v_w_in, v_hgrn_onorm_g, v_w_branch_a, v_w_branch_b, v_w_out, v_rel_bias, v_hgrn_lb, v_final_g):
    given = dict(x=x, c=c, w_ada=w_ada, b_ada=b_ada, norm_g=norm_g, w_in=w_in, hgrn_onorm_g=hgrn_onorm_g, w_branch_a=w_branch_a, w_branch_b=w_branch_b, w_out=w_out, rel_bias=rel_bias, hgrn_lb=hgrn_lb, final_g=final_g, loss_target=loss_target, m_w_ada=m_w_ada, m_b_ada=m_b_ada, m_norm_g=m_norm_g, m_w_in=m_w_in, m_hgrn_onorm_g=m_hgrn_onorm_g, m_w_branch_a=m_w_branch_a, m_w_branch_b=m_w_branch_b, m_w_out=m_w_out, m_rel_bias=m_rel_bias, m_hgrn_lb=m_hgrn_lb, m_final_g=m_final_g, v_w_ada=v_w_ada, v_b_ada=v_b_ada, v_norm_g=v_norm_g, v_w_in=v_w_in, v_hgrn_onorm_g=v_hgrn_onorm_g, v_w_branch_a=v_w_branch_a, v_w_branch_b=v_w_branch_b, v_w_out=v_w_out, v_rel_bias=v_rel_bias, v_hgrn_lb=v_hgrn_lb, v_final_g=v_final_g)
    weights = {n: given[n] for n in TWIN_WEIGHTS}
    shared = {n: given[n] for n in SHARED_INPUTS}
    per_example = {n: given[n] for n in ['x', 'c']}
    grad_fn = _jax.value_and_grad(_loss, argnums=(0, 1))

    def one_microbatch(ex, loss_target):
        ex = dict(ex)
        diff = ex.pop(TWIN_DIFF_INPUT)
        return grad_fn(weights, diff, {**shared, **ex}, loss_target)

    if N_MICROBATCH == 1:
        loss, (grad_w, grad_x) = one_microbatch(per_example, given["loss_target"])
    else:
        def body(carry, xs):
            loss_sum, grad_sum = carry
            l_k, (gw_k, gx_k) = one_microbatch(xs[0], xs[1])
            with _jax.named_scope("update"):
                return (loss_sum + l_k, _jax.tree.map(_jnp.add, grad_sum, gw_k)), gx_k

        init = (_jnp.zeros((), _jnp.float32), _jax.tree.map(_jnp.zeros_like, weights))
        (loss, grad_w), grad_x = _jax.lax.scan(body, init, (per_example, given["loss_target"]))
    with _jax.named_scope("update"):
        delta_w, new_m, new_v = {}, {}, {}
        for n in TWIN_WEIGHTS:
            delta_w[n], new_m[n], new_v[n] = _adamw(weights[n], grad_w[n], given["m_" + n], given["v_" + n])
    return (loss, grad_x, *[grad_w[n] for n in TWIN_WEIGHTS], *[delta_w[n] for n in TWIN_WEIGHTS],
            *[new_m[n] for n in TWIN_WEIGHTS], *[new_v[n] for n in TWIN_WEIGHTS])
```

```python
import functools
import math

import numpy as np
import jax
import jax.numpy as jnp
from jax import lax
from jax.experimental import pallas as pl
from jax.experimental.pallas import tpu as pltpu

D = 1024
AW = 512
NH = 8
HE = 64
HK = 128
NPROJ = 8192
ABLK = 128
PATTERNS = (1, 4, 16)
NBUCKETS = 32
MAXDIST = 2048
NEG = -1e30
EPS = 1e-6
CH = 64
LR, B1, B2, AEPS, WD, STEP = 0.001, 0.9, 0.999, 1e-08, 0.01, 10

F32 = jnp.float32
BF16 = jnp.bfloat16
MESH = pl.DeviceIdType.MESH
VMEM_LIMIT = 56 * 1024 * 1024


def _cp(**kw):
    return pltpu.CompilerParams(vmem_limit_bytes=VMEM_LIMIT, **kw)


def _sig(x):
    return 1.0 / (1.0 + jnp.exp(-x))


def _nt(a, b):
    return lax.dot_general(a, b, (((1,), (1,)), ((), ())), preferred_element_type=F32)


def _tn(a, b):
    return lax.dot_general(a, b, (((0,), (0,)), ((), ())), preferred_element_type=F32)


def _nn(a, b):
    return jnp.dot(a, b, preferred_element_type=F32)


def _split3(x):
    h = x.astype(BF16)
    r = x - h.astype(F32)
    m = r.astype(BF16)
    l = (r - m.astype(F32)).astype(BF16)
    return h, m, l


def _exact_mm(tri_bf16, x):
    h, m, l = _split3(x)
    return _nn(tri_bf16, h) + _nn(tri_bf16, m) + _nn(tri_bf16, l)


def _h_call(x, avec):
    S = x.shape[0]
    tm = 512

    def body(x_ref, a_ref, h_ref):
        xv = x_ref[...]
        r = lax.rsqrt(jnp.mean(xv * xv, axis=-1, keepdims=True) + EPS)
        h_ref[...] = (xv * r * a_ref[0:1, :] + a_ref[1:2, :]).astype(BF16)

    return pl.pallas_call(
        body, name="h_norm", grid=(S // tm,),
        in_specs=[pl.BlockSpec((tm, D), lambda i: (i, 0)), pl.BlockSpec((8, D), lambda i: (0, 0))],
        out_specs=pl.BlockSpec((tm, D), lambda i: (i, 0)),
        out_shape=jax.ShapeDtypeStruct((S, D), BF16), compiler_params=_cp(),
    )(x, avec)


def _proj_call(h, w_in):
    S = h.shape[0]
    tm, tn = 512, 2048

    def body(h_ref, w_ref, o_ref):
        o_ref[...] = _nn(h_ref[...], w_ref[...])

    return pl.pallas_call(
        body, name="in_proj", grid=(NPROJ // tn, S // tm),
        in_specs=[pl.BlockSpec((tm, D), lambda j, i: (i, 0)), pl.BlockSpec((D, tn), lambda j, i: (0, j))],
        out_specs=pl.BlockSpec((tm, tn), lambda j, i: (i, j)),
        out_shape=jax.ShapeDtypeStruct((S, NPROJ), F32), compiler_params=_cp(),
    )(h, w_in)


def _t5_bucket_np(dist):
    max_exact = NBUCKETS // 2
    n = dist.astype(np.float32)
    large = max_exact + (np.log(np.maximum(n, np.float32(1.0)) / np.float32(max_exact))
                         / np.float32(math.log(MAXDIST / max_exact))
                         * np.float32(NBUCKETS - max_exact)).astype(np.int32)
    large = np.minimum(large, NBUCKETS - 1)
    return np.where(dist < max_exact, dist, large)


def _band_bucket(d):
    qi = np.arange(ABLK)[:, None]
    kj = np.arange(2 * ABLK)[None, :]
    delta = qi + ABLK - kj
    band = (delta >= 0) & (delta <= ABLK)
    bucket = _t5_bucket_np(np.clip(delta, 0, None) * d)
    return band, bucket


def _bias_tile(rel_bias, d):
    band, bucket = _band_bucket(d)
    bias = jnp.transpose(rel_bias[bucket], (2, 0, 1))
    return jnp.where(band[None], bias, NEG)


def _attn_fwd_call(proj, bias, d):
    S = proj.shape[0]
    L = S // d
    nb = L // ABLK
    pv = proj.reshape(L, d * NPROJ)
    cpb = NPROJ // AW

    def body(q_ref, kc_ref, kp_ref, vc_ref, vp_ref, b_ref, o_ref, l_ref):
        n = pl.program_id(1)
        q = q_ref[...].astype(BF16)
        k = jnp.concatenate([kp_ref[...], kc_ref[...]], axis=0).astype(BF16)
        v = jnp.concatenate([vp_ref[...], vc_ref[...]], axis=0).astype(BF16)
        col = lax.broadcasted_iota(jnp.int32, (ABLK, 2 * ABLK), 1)
        dead = jnp.logical_and(n == 0, col < ABLK)
        for hh in range(NH):
            sl = slice(hh * HE, (hh + 1) * HE)
            s = _nt(q[:, sl], k[:, sl]) * (HE ** -0.5) + b_ref[hh]
            s = jnp.where(dead, NEG, s)
            m = jnp.max(s, axis=-1, keepdims=True)
            p = jnp.exp(s - m)
            den = jnp.sum(p, axis=-1, keepdims=True)
            o = _nn(p.astype(BF16), v[:, sl]) / den
            o_ref[:, sl] = o
            l_ref[:, sl] = jnp.broadcast_to(m + jnp.log(den), (ABLK, HE))

    def cur(c):
        return pl.BlockSpec((ABLK, AW), lambda r, n: (n, r * cpb + c))

    def prev(c):
        return pl.BlockSpec((ABLK, AW), lambda r, n: (jnp.maximum(n - 1, 0), r * cpb + c))

    o, l = pl.pallas_call(
        body, name="attn_fwd_d%d" % d, grid=(d, nb),
        in_specs=[cur(0), cur(1), prev(1), cur(2), prev(2),
                  pl.BlockSpec((NH, ABLK, 2 * ABLK), lambda r, n: (0, 0, 0))],
        out_specs=[pl.BlockSpec((ABLK, AW), lambda r, n: (n, r)),
                   pl.BlockSpec((ABLK, AW), lambda r, n: (n, r))],
        out_shape=[jax.ShapeDtypeStruct((L, d * AW), F32), jax.ShapeDtypeStruct((L, d * AW), F32)],
        compiler_params=_cp(),
    )(pv, pv, pv, pv, pv, bias)
    return o.reshape(S, AW), l.reshape(S, AW)


def _attn_merge_call(os_, ls_, proj):
    S = proj.shape[0]
    tm = 512

    def body(o1, o2, o3, l1, l2, l3, z_ref, a_ref, lt_ref, oa_ref):
        la, lb_, lc = l1[...], l2[...], l3[...]
        m = jnp.maximum(jnp.maximum(la, lb_), lc)
        ea, eb, ec = jnp.exp(la - m), jnp.exp(lb_ - m), jnp.exp(lc - m)
        den = ea + eb + ec
        att = (ea * o1[...] + eb * o2[...] + ec * o3[...]) / den
        a_ref[...] = att
        lt_ref[...] = m + jnp.log(den)
        z = z_ref[...]
        oa_ref[...] = (att * (z * _sig(z))).astype(BF16)

    blk = pl.BlockSpec((tm, AW), lambda i: (i, 0))
    return pl.pallas_call(
        body, name="attn_merge", grid=(S // tm,),
        in_specs=[blk] * 6 + [pl.BlockSpec((tm, AW), lambda i: (i, 3))],
        out_specs=[blk, blk, blk],
        out_shape=[jax.ShapeDtypeStruct((S, AW), F32), jax.ShapeDtypeStruct((S, AW), F32),
                   jax.ShapeDtypeStruct((S, AW), BF16)],
        compiler_params=_cp(),
    )(*os_, *ls_, proj)


def _attn_bwd_call(proj, dattn, lse, dsum, bias, d):
    S = proj.shape[0]
    L = S // d
    nb = L // ABLK
    pv = proj.reshape(L, d * NPROJ)
    cpb = NPROJ // AW
    dov, lv, dsv = (t.reshape(L, d * AW) for t in (dattn, lse, dsum))

    def body(q_ref, kc_ref, kp_ref, vc_ref, vp_ref, do_ref, l_ref, ds_ref, b_ref,
             dq_ref, dk_ref, dv_ref, db_ref, dkc, dvc):
        r = pl.program_id(0)
        n = pl.program_id(1)

        @pl.when(jnp.logical_and(r == 0, n == 0))
        def _():
            db_ref[...] = jnp.zeros_like(db_ref)

        @pl.when(n == 0)
        def _():
            dkc[...] = jnp.zeros_like(dkc)
            dvc[...] = jnp.zeros_like(dvc)

        @pl.when(n < nb)
        def _():
            q = q_ref[...].astype(BF16)
            k = jnp.concatenate([kp_ref[...], kc_ref[...]], axis=0).astype(BF16)
            v = jnp.concatenate([vp_ref[...], vc_ref[...]], axis=0).astype(BF16)
            do = do_ref[...].astype(BF16)
            col = lax.broadcasted_iota(jnp.int32, (ABLK, 2 * ABLK), 1)
            dead = jnp.logical_and(n == 0, col < ABLK)
            for hh in range(NH):
                sl = slice(hh * HE, (hh + 1) * HE)
                s = _nt(q[:, sl], k[:, sl]) * (HE ** -0.5) + b_ref[hh]
                s = jnp.where(dead, NEG, s)
                p = jnp.exp(s - l_ref[:, hh * HE:hh * HE + 1])
                dp = _nt(do[:, sl], v[:, sl])
                dsc = p * (dp - ds_ref[:, hh * HE:hh * HE + 1])
                db_ref[hh] += dsc
                dsb = (dsc * (HE ** -0.5)).astype(BF16)
                dq_ref[:, sl] = _nn(dsb, k[:, sl])
                dkk = _tn(dsb, q[:, sl])
                dvv = _tn(p.astype(BF16), do[:, sl])
                dk_ref[:, sl] = dkc[:, sl] + dkk[:ABLK]
                dv_ref[:, sl] = dvc[:, sl] + dvv[:ABLK]
                dkc[:, sl] = dkk[ABLK:]
                dvc[:, sl] = dvv[ABLK:]

        @pl.when(n == nb)
        def _():
            dk_ref[...] = dkc[...]
            dv_ref[...] = dvc[...]

    def cur(c):
        return pl.BlockSpec((ABLK, AW), lambda r, n: (jnp.minimum(n, nb - 1), r * cpb + c))

    def prev(c):
        return pl.BlockSpec((ABLK, AW), lambda r, n: (jnp.clip(n - 1, 0, nb - 1), r * cpb + c))

    qrow = pl.BlockSpec((ABLK, AW), lambda r, n: (jnp.minimum(n, nb - 1), r))
    krow = pl.BlockSpec((ABLK, AW), lambda r, n: (jnp.maximum(n - 1, 0), r))
    dq, dk, dv, db = pl.pallas_call(
        body, name="attn_bwd_d%d" % d, grid=(d, nb + 1),
        in_specs=[cur(0), cur(1), prev(1), cur(2), prev(2), qrow, qrow, qrow,
                  pl.BlockSpec((NH, ABLK, 2 * ABLK), lambda r, n: (0, 0, 0))],
        out_specs=[qrow, krow, krow, pl.BlockSpec((NH, ABLK, 2 * ABLK), lambda r, n: (0, 0, 0))],
        out_shape=[jax.ShapeDtypeStruct((L, d * AW), F32)] * 3
                  + [jax.ShapeDtypeStruct((NH, ABLK, 2 * ABLK), F32)],
        scratch_shapes=[pltpu.VMEM((ABLK, AW), F32), pltpu.VMEM((ABLK, AW), F32)],
        compiler_params=_cp(),
    )(pv, pv, pv, pv, pv, dov, lv, dsv, bias)
    return dq.reshape(S, AW), dk.reshape(S, AW), dv.reshape(S, AW), db


def _attn_sum_call(dqs, dks, dvs, dza):
    S = dza.shape[0]
    tm = 512

    def body(q1, q2, q3, k1, k2, k3, v1, v2, v3, z_ref, o_ref):
        o_ref[:, 0:AW] = (q1[...] + q2[...] + q3[...]).astype(BF16)
        o_ref[:, AW:2 * AW] = (k1[...] + k2[...] + k3[...]).astype(BF16)
        o_ref[:, 2 * AW:3 * AW] = (v1[...] + v2[...] + v3[...]).astype(BF16)
        o_ref[:, 3 * AW:4 * AW] = z_ref[...]

    blk = pl.BlockSpec((tm, AW), lambda i: (i, 0))
    return pl.pallas_call(
        body, name="attn_dsum", grid=(S // tm,), in_specs=[blk] * 10,
        out_specs=pl.BlockSpec((tm, 4 * AW), lambda i: (i, 0)),
        out_shape=jax.ShapeDtypeStruct((S, 4 * AW), BF16), compiler_params=_cp(),
    )(*dqs, *dks, *dvs, dza)


HRB = 256


def _hgrn_gates(q_ref, f_ref, rows, lbv, tri):
    qraw = q_ref[rows, :]
    sq = _sig(qraw)
    q = qraw * sq
    sf = _sig(f_ref[rows, :])
    f = lbv + (1.0 - lbv) * sf
    k = 1.0 - f
    b = _exact_mm(tri, jnp.log(f))
    bl = b[CH - 1:CH, :]
    bm = b[CH // 2 - 1:CH // 2, :]
    eb = jnp.exp(b)
    qe = q * eb
    qs = q * jnp.exp(b - bm)
    ks = k * jnp.exp(bm - b)
    kd = k * jnp.exp(bl - b)
    return dict(qraw=qraw, sq=sq, q=q, sf=sf, f=f, k=k, b=b, bl=bl, bm=bm, eb=eb, qe=qe, qs=qs, ks=ks, kd=kd)


def _tri_masks():
    row = lax.broadcasted_iota(jnp.int32, (CH, CH), 0)
    col = lax.broadcasted_iota(jnp.int32, (CH, CH), 1)
    return row >= col


def _hgrn_fwd_call(proj, lb, gn):
    S = proj.shape[0]
    nc = S // CH
    cps = HRB // CH

    def body(q_ref, f_ref, i_ref, z_ref, lb_ref, gn_ref, or_ref, ob_ref, st_ref, st):
        @pl.when(pl.program_id(0) == 0)
        def _():
            st[...] = jnp.zeros_like(st)

        low = _tri_masks()
        tri = low.astype(BF16)
        lbv = lb_ref[...]
        for ci in range(cps):
            rows = slice(ci * CH, (ci + 1) * CH)
            g = _hgrn_gates(q_ref, f_ref, rows, lbv, tri)
            v = i_ref[rows, :]
            ebl = jnp.exp(g["bl"])
            st_ref[ci] = st[...]
            for hh in range(NH):
                sl = slice(hh * HK, (hh + 1) * HK)
                s0 = st[sl, :]
                vb = v[:, sl].astype(BF16)
                a = jnp.where(low, _nt(g["qs"][:, sl].astype(BF16), g["ks"][:, sl].astype(BF16)), 0.0)
                o = _nt(g["qe"][:, sl].astype(BF16), s0.astype(BF16)) + _nn(a.astype(BF16), vb)
                st[sl, :] = s0 * ebl[:, sl] + _tn(vb, g["kd"][:, sl].astype(BF16))
                or_ref[rows, sl] = o
                r = lax.rsqrt(jnp.mean(o * o, axis=-1, keepdims=True) + EPS)
                z = z_ref[rows, sl]
                ob_ref[rows, sl] = (o * r * gn_ref[:, sl] * (z * _sig(z))).astype(BF16)

    def pcol(c):
        return pl.BlockSpec((HRB, D), lambda i: (i, c))

    vec = pl.BlockSpec((1, D), lambda i: (0, 0))
    row = pl.BlockSpec((HRB, D), lambda i: (i, 0))
    return pl.pallas_call(
        body, name="hgrn_fwd", grid=(S // HRB,),
        in_specs=[pcol(2), pcol(3), pcol(4), pcol(5), vec, vec],
        out_specs=[row, row, pl.BlockSpec((cps, NH * HK, HK), lambda i: (i, 0, 0))],
        out_shape=[jax.ShapeDtypeStruct((S, D), F32), jax.ShapeDtypeStruct((S, D), BF16),
                   jax.ShapeDtypeStruct((nc, NH * HK, HK), F32)],
        scratch_shapes=[pltpu.VMEM((NH * HK, HK), F32)],
        compiler_params=_cp(),
    )(proj, proj, proj, proj, lb, gn)


def _hgrn_bwd_call(proj, oraw, dob, states, lb, gn):
    S = proj.shape[0]
    nblk = S // HRB
    cps = HRB // CH

    def body(q_ref, f_ref, i_ref, z_ref, or_ref, dob_ref, st_ref, lb_ref, gn_ref, dh_ref, acc_ref, dst):
        @pl.when(pl.program_id(0) == 0)
        def _():
            dst[...] = jnp.zeros_like(dst)
            acc_ref[...] = jnp.zeros_like(acc_ref)

        low = _tri_masks()
        tri = low.astype(BF16)
        triu = jnp.logical_not(_tri_masks()) | (lax.broadcasted_iota(jnp.int32, (CH, CH), 0)
                                               == lax.broadcasted_iota(jnp.int32, (CH, CH), 1))
        triu = triu.astype(BF16)
        lbv = lb_ref[...]
        for ci in reversed(range(cps)):
            rows = slice(ci * CH, (ci + 1) * CH)
            g = _hgrn_gates(q_ref, f_ref, rows, lbv, tri)
            v = i_ref[rows, :]
            ebl = jnp.exp(g["bl"])
            dqs_, dks_, dbs_, dvs_, dzs_, exs_, dgn_ = [], [], [], [], [], [], []
            for hh in reversed(range(NH)):
                sl = slice(hh * HK, (hh + 1) * HK)
                o = or_ref[rows, sl]
                z = z_ref[rows, sl]
                sz = _sig(z)
                gnv = gn_ref[:, sl]
                r = lax.rsqrt(jnp.mean(o * o, axis=-1, keepdims=True) + EPS)
                dobv = dob_ref[rows, sl]
                don = dobv * (z * sz)
                dzs_.append(dobv * (o * r * gnv) * (sz * (1.0 + z * (1.0 - sz))))
                dgn_.append(jnp.sum(don * o * r, axis=0, keepdims=True))
                gh = don * gnv
                do = r * gh - o * (r * r * r * jnp.mean(gh * o, axis=-1, keepdims=True))
                dob16 = do.astype(BF16)

                st0 = st_ref[ci, sl, :]
                dst1 = dst[sl, :]
                vb = v[:, sl].astype(BF16)
                qsb = g["qs"][:, sl].astype(BF16)
                ksb = g["ks"][:, sl].astype(BF16)
                qeb = g["qe"][:, sl].astype(BF16)
                kdb = g["kd"][:, sl].astype(BF16)
                dst1b = dst1.astype(BF16)
                a = jnp.where(low, _nt(qsb, ksb), 0.0).astype(BF16)
                da = jnp.where(low, _nt(dob16, vb), 0.0).astype(BF16)
                dqe = _nn(dob16, st0.astype(BF16))
                dqs = _nn(da, ksb)
                dks = _tn(da, qsb)
                dv = _tn(a, dob16) + _nt(kdb, dst1b)
                dkd = _nn(vb, dst1b)
                eb = g["eb"][:, sl]
                b = g["b"][:, sl]
                dq = dqe * eb + dqs * jnp.exp(b - g["bm"][:, sl])
                dk = dks * jnp.exp(g["bm"][:, sl] - b) + dkd * jnp.exp(g["bl"][:, sl] - b)
                db = (dqe * g["qe"][:, sl] + dqs * qsb.astype(F32)
                      - dks * ksb.astype(F32) - dkd * g["kd"][:, sl])
                ex = (jnp.sum(dkd * g["kd"][:, sl], axis=0, keepdims=True)
                      + jnp.sum(dst1 * st0, axis=0, keepdims=True) * ebl[:, sl])
                dst[sl, :] = _tn(dob16, qeb) + dst1 * ebl[:, sl]
                dqs_.append(dq); dks_.append(dk); dbs_.append(db); dvs_.append(dv); exs_.append(ex)

            cat = lambda lst: jnp.concatenate(lst[::-1], axis=1)
            dq, dk, db, dv, dz, ex, dgn = (cat(t) for t in (dqs_, dks_, dbs_, dvs_, dzs_, exs_, dgn_))
            dg = _exact_mm(triu, db) + ex
            df = dg / g["f"] - dk
            sf = g["sf"]
            dfr = df * (1.0 - lbv) * sf * (1.0 - sf)
            sq = g["sq"]
            dqr = dq * (sq * (1.0 + g["qraw"] * (1.0 - sq)))
            acc_ref[0:1, :] += jnp.sum(df * (1.0 - sf), axis=0, keepdims=True)
            acc_ref[1:2, :] += dgn
            dh_ref[rows, 0:D] = dqr.astype(BF16)
            dh_ref[rows, D:2 * D] = dfr.astype(BF16)
            dh_ref[rows, 2 * D:3 * D] = dv.astype(BF16)
            dh_ref[rows, 3 * D:4 * D] = dz.astype(BF16)

    def pcol(c):
        return pl.BlockSpec((HRB, D), lambda i: (nblk - 1 - i, c))

    vec = pl.BlockSpec((1, D), lambda i: (0, 0))
    row = pl.BlockSpec((HRB, D), lambda i: (nblk - 1 - i, 0))
    return pl.pallas_call(
        body, name="hgrn_bwd", grid=(nblk,),
        in_specs=[pcol(2), pcol(3), pcol(4), pcol(5), row, row,
                  pl.BlockSpec((cps, NH * HK, HK), lambda i: (nblk - 1 - i, 0, 0)), vec, vec],
        out_specs=[pl.BlockSpec((HRB, 4 * D), lambda i: (nblk - 1 - i, 0)),
                   pl.BlockSpec((8, D), lambda i: (0, 0))],
        out_shape=[jax.ShapeDtypeStruct((S, 4 * D), BF16), jax.ShapeDtypeStruct((8, D), F32)],
        scratch_shapes=[pltpu.VMEM((NH * HK, HK), F32)],
        compiler_params=_cp(),
    )(proj, proj, proj, proj, oraw, dob, states, lb, gn)


def _fwd2_call(oa, ob, proj, x, tgt, vecs, wa, wb, wo):
    S = x.shape[0]
    tm = 256

    def body(oa_ref, ob_ref, ga_ref, gb_ref, x_ref, t_ref, v_ref, wa_ref, wb_ref, wo_ref,
             ya_ref, yb_ref, y_ref, u_ref, x2_ref, ls_ref):
        @pl.when(pl.program_id(0) == 0)
        def _():
            ls_ref[...] = jnp.zeros_like(ls_ref)

        ya = _nn(oa_ref[...], wa_ref[...])
        yb = _nn(ob_ref[...], wb_ref[...])
        y = _sig(ga_ref[...]) * ya + _sig(gb_ref[...]) * yb
        u = _nn(y.astype(BF16), wo_ref[...])
        x2 = x_ref[...] + v_ref[0:1, :] * u
        r = lax.rsqrt(jnp.mean(x2 * x2, axis=-1, keepdims=True) + EPS)
        err = x2 * r * v_ref[1:2, :] - t_ref[...]
        ls_ref[...] += jnp.sum(err * err)
        ya_ref[...] = ya.astype(BF16)
        yb_ref[...] = yb.astype(BF16)
        y_ref[...] = y.astype(BF16)
        u_ref[...] = u.astype(BF16)
        x2_ref[...] = x2

    row = pl.BlockSpec((tm, D), lambda i: (i, 0))
    full = lambda a: pl.BlockSpec(a.shape, lambda i: (0, 0))
    return pl.pallas_call(
        body, name="fwd_merge_out", grid=(S // tm,),
        in_specs=[pl.BlockSpec((tm, AW), lambda i: (i, 0)), row,
                  pl.BlockSpec((tm, D), lambda i: (i, 6)), pl.BlockSpec((tm, D), lambda i: (i, 7)),
                  row, row, pl.BlockSpec((8, D), lambda i: (0, 0)), full(wa), full(wb), full(wo)],
        out_specs=[row, row, row, row, row, pl.BlockSpec((8, 128), lambda i: (0, 0))],
        out_shape=[jax.ShapeDtypeStruct((S, D), BF16)] * 4
                  + [jax.ShapeDtypeStruct((S, D), F32), jax.ShapeDtypeStruct((8, 128), F32)],
        compiler_params=_cp(),
    )(oa, ob, proj, proj, x, tgt, vecs, wa, wb, wo)


def _bwd2_call(x2, tgt, vecs, ya, yb, u, proj, attn, wa, wb, wo, hsum):
    S = x2.shape[0]
    tm = 256

    def body(x2_ref, t_ref, v_ref, ya_ref, yb_ref, u_ref, ga_ref, gb_ref, at_ref, za_ref,
             wa_ref, wb_ref, wo_ref, hs_ref,
             dx2_ref, du_ref, dya_ref, dyb_ref, dg_ref, dat_ref, dsum_ref, dza_ref, dob_ref, acc_ref):
        @pl.when(pl.program_id(0) == 0)
        def _():
            acc_ref[...] = jnp.zeros_like(acc_ref)

        x2v = x2_ref[...]
        gate = v_ref[0:1, :]
        fg = v_ref[1:2, :]
        r = lax.rsqrt(jnp.mean(x2v * x2v, axis=-1, keepdims=True) + EPS)
        dout = (x2v * r * fg - t_ref[...]) * (1.0 / D)
        gh = dout * fg
        dx2 = r * gh - x2v * (r * r * r * jnp.mean(gh * x2v, axis=-1, keepdims=True))
        acc_ref[0:1, :] += jnp.sum(dx2 * u_ref[...].astype(F32), axis=0, keepdims=True)
        acc_ref[1:2, :] += jnp.sum(dout * x2v * r, axis=0, keepdims=True)
        dx2_ref[...] = dx2
        du = (dx2 * gate).astype(BF16)
        du_ref[...] = du
        dy = _nt(du, wo_ref[...])
        sa = _sig(ga_ref[...])
        sb = _sig(gb_ref[...])
        dya = (dy * sa).astype(BF16)
        dyb = (dy * sb).astype(BF16)
        dya_ref[...] = dya
        dyb_ref[...] = dyb
        dg_ref[:, 0:D] = (dy * ya_ref[...].astype(F32) * sa * (1.0 - sa)).astype(BF16)
        dg_ref[:, D:2 * D] = (dy * yb_ref[...].astype(F32) * sb * (1.0 - sb)).astype(BF16)
        doa = _nt(dya, wa_ref[...])
        dob_ref[...] = _nt(dyb, wb_ref[...])
        za = za_ref[...]
        sz = _sig(za)
        att = at_ref[...]
        dat = doa * (za * sz)
        dat_ref[...] = dat
        dza_ref[...] = (doa * att * (sz * (1.0 + za * (1.0 - sz)))).astype(BF16)
        dsum_ref[...] = _exact_mm_r(dat * att, hs_ref[...])

    row = pl.BlockSpec((tm, D), lambda i: (i, 0))
    arow = pl.BlockSpec((tm, AW), lambda i: (i, 0))
    full = lambda a: pl.BlockSpec(a.shape, lambda i: (0, 0))
    return pl.pallas_call(
        body, name="bwd_merge_out", grid=(S // tm,),
        in_specs=[row, row, pl.BlockSpec((8, D), lambda i: (0, 0)), row, row, row,
                  pl.BlockSpec((tm, D), lambda i: (i, 6)), pl.BlockSpec((tm, D), lambda i: (i, 7)),
                  arow, pl.BlockSpec((tm, AW), lambda i: (i, 3)), full(wa), full(wb), full(wo), full(hsum)],
        out_specs=[row, row, row, row, pl.BlockSpec((tm, 2 * D), lambda i: (i, 0)),
                   arow, arow, arow, row, pl.BlockSpec((8, D), lambda i: (0, 0))],
        out_shape=[jax.ShapeDtypeStruct((S, D), F32), jax.ShapeDtypeStruct((S, D), BF16),
                   jax.ShapeDtypeStruct((S, D), BF16), jax.ShapeDtypeStruct((S, D), BF16),
                   jax.ShapeDtypeStruct((S, 2 * D), BF16), jax.ShapeDtypeStruct((S, AW), F32),
                   jax.ShapeDtypeStruct((S, AW), F32), jax.ShapeDtypeStruct((S, AW), BF16),
                   jax.ShapeDtypeStruct((S, D), F32), jax.ShapeDtypeStruct((8, D), F32)],
        compiler_params=_cp(),
    )(x2, tgt, vecs, ya, yb, u, proj, proj, attn, proj, wa, wb, wo, hsum)


def _exact_mm_r(x, ones_bf16):
    h, m, l = _split3(x)
    return _nn(h, ones_bf16) + _nn(m, ones_bf16) + _nn(l, ones_bf16)


def _atb_call(a, b, name):
    S, K = a.shape
    N = b.shape[1]
    tm = 512

    def body(a_ref, b_ref, o_ref):
        @pl.when(pl.program_id(0) == 0)
        def _():
            o_ref[...] = jnp.zeros_like(o_ref)

        o_ref[...] += _tn(a_ref[...], b_ref[...])

    return pl.pallas_call(
        body, name=name, grid=(S // tm,),
        in_specs=[pl.BlockSpec((tm, K), lambda i: (i, 0)), pl.BlockSpec((tm, N), lambda i: (i, 0))],
        out_specs=pl.BlockSpec((K, N), lambda i: (0, 0)),
        out_shape=jax.ShapeDtypeStruct((K, N), F32), compiler_params=_cp(),
    )(a, b)


def _dwin_call(h, d_attn, d_hgrn, d_gates):
    S = h.shape[0]
    tm = 512
    tn = 2048

    def body(h_ref, a_ref, m_ref, g_ref, o_ref):
        j = pl.program_id(0)

        @pl.when(pl.program_id(1) == 0)
        def _():
            o_ref[...] = jnp.zeros_like(o_ref)

        hv = h_ref[...]

        @pl.when(j == 0)
        def _():
            o_ref[0] += _tn(hv, a_ref[...])

        @pl.when(jnp.logical_or(j == 1, j == 2))
        def _():
            o_ref[0] += _tn(hv, m_ref[...])

        @pl.when(j == 3)
        def _():
            o_ref[0] += _tn(hv, g_ref[...])

    return pl.pallas_call(
        body, name="dw_in", grid=(4, S // tm),
        in_specs=[pl.BlockSpec((tm, D), lambda j, i: (i, 0)),
                  pl.BlockSpec((tm, tn), lambda j, i: (jnp.where(j == 0, i, 0), 0)),
                  pl.BlockSpec((tm, tn), lambda j, i: (jnp.where(jnp.logical_or(j == 1, j == 2), i, 0),
                                                       jnp.where(j == 2, 1, 0))),
                  pl.BlockSpec((tm, tn), lambda j, i: (jnp.where(j == 3, i, 0), 0))],
        out_specs=pl.BlockSpec((1, D, tn), lambda j, i: (j, 0, 0)),
        out_shape=jax.ShapeDtypeStruct((4, D, tn), F32), compiler_params=_cp(),
    )(h, d_attn, d_hgrn, d_gates)


def _dh_call(d_attn, d_hgrn, d_gates, w_in, x, dx2, vecs):
    S = x.shape[0]
    tm = 512
    tk = 2048

    def body(a_ref, m_ref, g_ref, w_ref, x_ref, dx2_ref, v_ref, gx_ref, acc_ref, dh):
        j = pl.program_id(1)

        @pl.when(jnp.logical_and(pl.program_id(0) == 0, j == 0))
        def _():
            acc_ref[...] = jnp.zeros_like(acc_ref)

        @pl.when(j == 0)
        def _():
            dh[...] = _nt(a_ref[...], w_ref[...])

        @pl.when(jnp.logical_or(j == 1, j == 2))
        def _():
            dh[...] += _nt(m_ref[...], w_ref[...])

        @pl.when(j == 3)
        def _():
            dhv = dh[...] + _nt(g_ref[...], w_ref[...])
            xv = x_ref[...]
            r = lax.rsqrt(jnp.mean(xv * xv, axis=-1, keepdims=True) + EPS)
            xn = xv * r
            acc_ref[0:1, :] += jnp.sum(dhv, axis=0, keepdims=True)
            acc_ref[1:2, :] += jnp.sum(dhv * xn * v_ref[1:2, :], axis=0, keepdims=True)
            acc_ref[2:3, :] += jnp.sum(dhv * xn * v_ref[2:3, :], axis=0, keepdims=True)
            dxn = dhv * v_ref[0:1, :]
            gx_ref[...] = dx2_ref[...] + r * dxn - xv * (r * r * r * jnp.mean(dxn * xv, axis=-1, keepdims=True))

    row = pl.BlockSpec((tm, D), lambda i, j: (i, 0))
    return pl.pallas_call(
        body, name="dh_gradx", grid=(S // tm, 4),
        in_specs=[pl.BlockSpec((tm, tk), lambda i, j: (i, 0)),
                  pl.BlockSpec((tm, tk), lambda i, j: (i, jnp.where(j == 2, 1, 0))),
                  pl.BlockSpec((tm, tk), lambda i, j: (i, 0)),
                  pl.BlockSpec((D, tk), lambda i, j: (0, j)),
                  row, row, pl.BlockSpec((8, D), lambda i, j: (0, 0))],
        out_specs=[row, pl.BlockSpec((8, D), lambda i, j: (0, 0))],
        out_shape=[jax.ShapeDtypeStruct((S, D), F32), jax.ShapeDtypeStruct((8, D), F32)],
        scratch_shapes=[pltpu.VMEM((tm, D), F32)],
        compiler_params=_cp(),
    )(d_attn, d_hgrn, d_gates, w_in, x, dx2, vecs)


def _adamw_math(w, g, m, v):
    m = B1 * m + (1.0 - B1) * g
    v = B2 * v + (1.0 - B2) * (g * g)
    m_hat = m / (1.0 - B1 ** STEP)
    v_hat = v / (1.0 - B2 ** STEP)
    delta = -LR * (m_hat / (jnp.sqrt(v_hat) + AEPS) + WD * w)
    return delta, m, v


def _adamw_call(w, g, m, v, name):
    R, C = w.shape
    tr = R if R * C * 4 <= (1 << 20) else max(8, (1 << 20) // (C * 4))
    assert R % tr == 0

    def body(w_ref, g_ref, m_ref, v_ref, d_ref, nm_ref, nv_ref):
        d_ref[...], nm_ref[...], nv_ref[...] = _adamw_math(w_ref[...], g_ref[...], m_ref[...], v_ref[...])

    blk = pl.BlockSpec((tr, C), lambda i: (i, 0))
    return pl.pallas_call(
        body, name=name, grid=(R // tr,), in_specs=[blk] * 4, out_specs=[blk] * 3,
        out_shape=[jax.ShapeDtypeStruct((R, C), F32)] * 3, compiler_params=_cp(),
    )(w, g, m, v)


def _mod_call(c_all, w_ada_s, b_s):
    def body(c_ref, w_ref, b_ref, o_ref):
        cv = c_ref[...]
        sc = cv * _sig(cv)
        o_ref[...] = jnp.dot(sc, w_ref[...], preferred_element_type=F32,
                             precision=lax.Precision.HIGHEST) + b_ref[...]

    return pl.pallas_call(
        body, name="ada_mod", out_shape=jax.ShapeDtypeStruct((8, w_ada_s.shape[1]), F32),
        compiler_params=_cp(),
    )(c_all, w_ada_s, b_s)


def _ada_update_call(sct, dm, w, m, v):
    R, C = w.shape
    tr = 256

    def body(s_ref, d_ref, w_ref, m_ref, v_ref, g_ref, dl_ref, nm_ref, nv_ref):
        g = s_ref[:, 0:1] * d_ref[0:1, :]
        for b in range(1, 8):
            g = g + s_ref[:, b:b + 1] * d_ref[b:b + 1, :]
        g_ref[...] = g
        dl_ref[...], nm_ref[...], nv_ref[...] = _adamw_math(w_ref[...], g, m_ref[...], v_ref[...])

    blk = pl.BlockSpec((tr, C), lambda i: (i, 0))
    return pl.pallas_call(
        body, name="ada_update", grid=(R // tr,),
        in_specs=[pl.BlockSpec((tr, 8), lambda i: (i, 0)), pl.BlockSpec((8, C), lambda i: (0, 0)), blk, blk, blk],
        out_specs=[blk] * 4, out_shape=[jax.ShapeDtypeStruct((R, C), F32)] * 4, compiler_params=_cp(),
    )(sct, dm, w, m, v)


def _sum8_call(packs):
    def body(p_ref, o_ref):
        acc = p_ref[0]
        for k in range(1, 8):
            acc = acc + p_ref[k]
        o_ref[...] = acc

    return pl.pallas_call(
        body, name="sum_small", out_shape=jax.ShapeDtypeStruct(packs.shape[1:], F32), compiler_params=_cp(),
    )(packs)


def _local_step(x, tgt, shift, scale, gate, norm_g, hgrn_onorm_g, rel_bias, lb, final_g, w_in, wa, wb, wo):
    a = norm_g * (1.0 + scale)
    z6 = jnp.zeros((6, D), F32)
    h = _h_call(x, jnp.concatenate([a, shift, z6], 0))
    proj = _proj_call(h, w_in)

    biases = [_bias_tile(rel_bias, d) for d in PATTERNS]
    outs = [_attn_fwd_call(proj, biases[p], d) for p, d in enumerate(PATTERNS)]
    attn, lse, oa = _attn_merge_call([o for o, _ in outs], [l for _, l in outs], proj)

    gn = jnp.tile(hgrn_onorm_g, (1, NH))
    oraw, ob, states = _hgrn_fwd_call(proj, lb, gn)

    vecs2 = jnp.concatenate([gate, final_g, z6], 0)
    ya, yb, y, u, x2, lsq = _fwd2_call(oa, ob, proj, x, tgt, vecs2, wa, wb, wo)
    loss = 0.5 * lsq[0, 0] / D

    hsum = jnp.asarray(np.kron(np.eye(NH), np.ones((HE, HE))), BF16)
    dx2, du, dya, dyb, d_gates, dattn, dsum, dza, dob, acc2 = _bwd2_call(
        x2, tgt, vecs2, ya, yb, u, proj, attn, wa, wb, wo, hsum)
    d_wo = _atb_call(y, du, "dw_out")
    d_wa = _atb_call(oa, dya, "dw_branch_a")
    d_wb = _atb_call(ob, dyb, "dw_branch_b")

    d_hgrn, acch = _hgrn_bwd_call(proj, oraw, dob, states, lb, gn)

    dqs, dks, dvs, dbs = [], [], [], []
    for p, d in enumerate(PATTERNS):
        dq, dk, dv, db = _attn_bwd_call(proj, dattn, lse, dsum, biases[p], d)
        dqs.append(dq); dks.append(dk); dvs.append(dv); dbs.append(db)
    d_attn = _attn_sum_call(dqs, dks, dvs, dza)

    d_win = _dwin_call(h, d_attn, d_hgrn, d_gates)
    one_scale = 1.0 + scale
    grad_x, acc1 = _dh_call(d_attn, d_hgrn, d_gates, w_in, x, dx2,
                            jnp.concatenate([a, norm_g, one_scale, jnp.zeros((5, D), F32)], 0))

    d_rel = jnp.zeros((NBUCKETS, NH), F32)
    for p, d in enumerate(PATTERNS):
        band, bucket = _band_bucket(d)
        onehot = (bucket[None] == np.arange(NBUCKETS)[:, None, None]) & band[None]
        d_rel = d_rel + jnp.einsum("hqk,bqk->bh", dbs[p], jnp.asarray(onehot, F32),
                                   precision=lax.Precision.HIGHEST)
    d_onorm = jnp.sum(acch[1].reshape(NH, HK), axis=0)

    pack = jnp.zeros((16, D), F32)
    pack = pack.at[0].set(acc1[0])
    pack = pack.at[1].set(acc1[1])
    pack = pack.at[2].set(acc2[0])
    pack = pack.at[3].set(acc1[2])
    pack = pack.at[4].set(acc2[1])
    pack = pack.at[5].set(acch[0])
    pack = pack.at[6, :HK].set(d_onorm)
    pack = pack.at[7, 0].set(loss)
    pack = pack.at[8, :NBUCKETS * NH].set(d_rel.reshape(-1))
    return grad_x, d_win, d_wa, d_wb, d_wo, pack


def _me():
    return lax.axis_index("x"), lax.axis_index("y"), lax.axis_index("c")


def _peers(x, y):
    return [(1 - x, y), (x, 1 - y), (1 - x, 1 - y)]


def _allgather_small(blk, name):
    m_per, n = blk.shape

    def body(x_ref, out_ref, send_sems, recv_sems, local_sem):
        x, y, c = _me()
        me, sibling = (x, y, c), (x, y, 1 - c)
        chips = _peers(x, y)

        def rows(px, py, pc):
            return out_ref.at[pl.ds((4 * px + 2 * py + pc) * m_per, m_per), :]

        def copy(k, block, to, src=None):
            return pltpu.make_async_remote_copy(
                src_ref=rows(*block) if src is None else src, dst_ref=rows(*block),
                send_sem=send_sems.at[k], recv_sem=recv_sems.at[k], device_id=to, device_id_type=MESH)

        mine = pltpu.make_async_copy(x_ref, rows(*me), local_sem)
        mine.start()
        first = [copy(0, me, sibling, src=x_ref)]
        first += [copy(1 + j, me, (*chip, c), src=x_ref) for j, chip in enumerate(chips)]
        for cp in first:
            cp.start()
        passed = [copy(4 + j, (*chip, c), sibling) for j, chip in enumerate(chips)]
        for j, chip in enumerate(chips):
            copy(1 + j, (*chip, c), me).wait_recv()
            passed[j].start()
        copy(0, sibling, me).wait_recv()
        for j, chip in enumerate(chips):
            copy(4 + j, (*chip, 1 - c), me).wait_recv()
        for cp in first + passed:
            cp.wait_send()
        mine.wait()

    return pl.pallas_call(
        body, name=name, out_shape=jax.ShapeDtypeStruct((8 * m_per, n), blk.dtype),
        in_specs=[pl.BlockSpec(memory_space=pltpu.VMEM)], out_specs=pl.BlockSpec(memory_space=pltpu.VMEM),
        scratch_shapes=[pltpu.SemaphoreType.DMA((7,)), pltpu.SemaphoreType.DMA((7,)), pltpu.SemaphoreType.DMA],
    )(blk)


ANY = pl.BlockSpec(memory_space=pl.ANY)


def _gather_weights(win_s, wa_s, wb_s, wo_s):
    shapes = [(D, NPROJ), (AW, D), (D, D), (D, D)]

    def body(win_ref, wa_ref, wb_ref, wo_ref, owin, owa, owb, owo, send_sems, recv_sems, local_sems):
        x, y, c = _me()
        sibling = (x, y, 1 - c)
        chips = _peers(x, y)

        def shard(t, ref, j, half=None):
            if t == 0:
                r0, nr = (0, D) if half is None else (half * (D // 2), D // 2)
                return ref.at[pl.ds(r0, nr), pl.ds(j * 2048, 2048)]
            if t == 1:
                r0, nr = (0, AW) if half is None else (half * (AW // 2), AW // 2)
                return ref.at[pl.ds(r0, nr), pl.ds(j * 256, 256)]
            r0, nr = (0, 256) if half is None else (half * 128, 128)
            return ref.at[pl.ds(j * 256 + r0, nr), :]

        def src_half(t, ref, half):
            if t == 0:
                return ref.at[pl.ds(half * (D // 2), D // 2), :]
            if t == 1:
                return ref.at[pl.ds(half * (AW // 2), AW // 2), :]
            return ref.at[pl.ds(half * 128, 128), :]

        ins = [win_ref, wa_ref, wb_ref, wo_ref]
        outs = [owin, owa, owb, owo]
        me_chip = 2 * x + y
        local, sends, fwds = [], [], []
        for t in range(4):
            lc = pltpu.make_async_copy(ins[t], shard(t, outs[t], me_chip), local_sems.at[t])
            lc.start()
            local.append(lc)
            for k, chip in enumerate(chips):
                cp = pltpu.make_async_remote_copy(
                    src_ref=src_half(t, ins[t], c), dst_ref=shard(t, outs[t], me_chip, c),
                    send_sem=send_sems.at[t, k], recv_sem=recv_sems.at[t, k],
                    device_id=(*chip, c), device_id_type=MESH)
                cp.start()
                sends.append(cp)
        for t in range(4):
            for k, chip in enumerate(chips):
                pj = 2 * chip[0] + chip[1]
                blk = shard(t, outs[t], pj, c)
                pltpu.make_async_remote_copy(
                    src_ref=blk, dst_ref=blk, send_sem=send_sems.at[t, k], recv_sem=recv_sems.at[t, k],
                    device_id=(*chip, c), device_id_type=MESH).wait_recv()
                fw = pltpu.make_async_remote_copy(
                    src_ref=blk, dst_ref=blk, send_sem=send_sems.at[t, 3 + k], recv_sem=recv_sems.at[t, 3 + k],
                    device_id=sibling, device_id_type=MESH)
                fw.start()
                fwds.append(fw)
        for t in range(4):
            for k, chip in enumerate(chips):
                pj = 2 * chip[0] + chip[1]
                blk = shard(t, outs[t], pj, 1 - c)
                pltpu.make_async_remote_copy(
                    src_ref=blk, dst_ref=blk, send_sem=send_sems.at[t, 3 + k], recv_sem=recv_sems.at[t, 3 + k],
                    device_id=sibling, device_id_type=MESH).wait_recv()
        for cp in sends + fwds:
            cp.wait_send()
        for lc in local:
            lc.wait()

    return pl.pallas_call(
        body, name="gather_weights",
        out_shape=[jax.ShapeDtypeStruct(s, BF16) for s in shapes],
        in_specs=[ANY] * 4, out_specs=[ANY] * 4,
        scratch_shapes=[pltpu.SemaphoreType.DMA((4, 6)), pltpu.SemaphoreType.DMA((4, 6)),
                        pltpu.SemaphoreType.DMA((4,))],
    )(win_s, wa_s, wb_s, wo_s)


def _half_of(t, ref, half):
    if t == 0:
        return ref.at[:, pl.ds(half * 512, 512), :]
    if t == 1:
        return ref.at[pl.ds(half * 256, 256), :]
    return ref.at[:, pl.ds(half * 512, 512)]


HALF_SHAPES = [(4, 512, 2048), (256, D), (D, 512), (D, 512)]
PIECE_SHAPES = [(512, 2048), (256, 256), (256, 512), (256, 512)]
SHARD_SHAPES = [(D, 2048), (AW, 256), (256, D), (256, D)]


def _chip_piece(t, ref, j):
    if t == 0:
        return ref.at[j]
    if t == 1:
        return ref.at[:, pl.ds(j * 256, 256)]
    return ref.at[pl.ds(j * 256, 256), :]


def _reduce_sibling_send(gs):
    def body(g0, g1, g2, g3, r0, r1, r2, r3, send_sems, recv_sems):
        x, y, c = _me()
        ins, outs = [g0, g1, g2, g3], [r0, r1, r2, r3]
        cps = []
        for t in range(4):
            cp = pltpu.make_async_remote_copy(
                src_ref=_half_of(t, ins[t], 1 - c), dst_ref=outs[t],
                send_sem=send_sems.at[t], recv_sem=recv_sems.at[t], device_id=(x, y, 1 - c), device_id_type=MESH)
            cp.start()
            cps.append(cp)
        for cp in cps:
            cp.wait_recv()
        for cp in cps:
            cp.wait_send()

    return pl.pallas_call(
        body, name="reduce_sibling", out_shape=[jax.ShapeDtypeStruct(s, BF16) for s in HALF_SHAPES],
        in_specs=[ANY] * 4, out_specs=[ANY] * 4,
        scratch_shapes=[pltpu.SemaphoreType.DMA((4,)), pltpu.SemaphoreType.DMA((4,))],
    )(*gs)


def _reduce_chips_send(hs):
    def body(h0, h1, h2, h3, r0, r1, r2, r3, send_sems, recv_sems):
        x, y, c = _me()
        ins, outs = [h0, h1, h2, h3], [r0, r1, r2, r3]
        cps = []
        for t in range(4):
            for k, chip in enumerate(_peers(x, y)):
                pj = 2 * chip[0] + chip[1]
                cp = pltpu.make_async_remote_copy(
                    src_ref=_chip_piece(t, ins[t], pj), dst_ref=outs[t].at[k],
                    send_sem=send_sems.at[t, k], recv_sem=recv_sems.at[t, k],
                    device_id=(*chip, c), device_id_type=MESH)
                cp.start()
                cps.append(cp)
        for cp in cps:
            cp.wait_recv()
        for cp in cps:
            cp.wait_send()

    return pl.pallas_call(
        body, name="reduce_chips", out_shape=[jax.ShapeDtypeStruct((3,) + s, BF16) for s in PIECE_SHAPES],
        in_specs=[ANY] * 4, out_specs=[ANY] * 4,
        scratch_shapes=[pltpu.SemaphoreType.DMA((4, 3)), pltpu.SemaphoreType.DMA((4, 3))],
    )(*hs)


def _share_sibling(ps):
    def body(p0, p1, p2, p3, o0, o1, o2, o3, send_sems, recv_sems, local_sems):
        x, y, c = _me()
        ins, outs = [p0, p1, p2, p3], [o0, o1, o2, o3]

        def half(t, ref, hf):
            if t == 0:
                return ref.at[pl.ds(hf * 512, 512), :]
            if t == 1:
                return ref.at[pl.ds(hf * 256, 256), :]
            return ref.at[:, pl.ds(hf * 512, 512)]

        cps, lcs = [], []
        for t in range(4):
            lc = pltpu.make_async_copy(ins[t], half(t, outs[t], c), local_sems.at[t])
            lc.start()
            lcs.append(lc)
            cp = pltpu.make_async_remote_copy(
                src_ref=ins[t], dst_ref=half(t, outs[t], c), send_sem=send_sems.at[t], recv_sem=recv_sems.at[t],
                device_id=(x, y, 1 - c), device_id_type=MESH)
            cp.start()
            cps.append(cp)
        for t in range(4):
            pltpu.make_async_remote_copy(
                src_ref=ins[t], dst_ref=half(t, outs[t], 1 - c), send_sem=send_sems.at[t],
                recv_sem=recv_sems.at[t], device_id=(x, y, 1 - c), device_id_type=MESH).wait_recv()
        for cp in cps:
            cp.wait_send()
        for lc in lcs:
            lc.wait()

    return pl.pallas_call(
        body, name="share_sibling", out_shape=[jax.ShapeDtypeStruct(s, F32) for s in SHARD_SHAPES],
        in_specs=[ANY] * 4, out_specs=[ANY] * 4,
        scratch_shapes=[pltpu.SemaphoreType.DMA((4,)), pltpu.SemaphoreType.DMA((4,)),
                        pltpu.SemaphoreType.DMA((4,))],
    )(*ps)


def _cast_call(a, name):
    lead = a.shape[0]
    rest = a.shape[1:]
    per = int(np.prod(rest)) * 4
    tl = lead
    while tl * per > (4 << 20) and tl % 2 == 0:
        tl //= 2
    if len(rest) == 1 and tl % 16 != 0:
        tl = lead
    zeros = (0,) * len(rest)

    def body(a_ref, o_ref):
        o_ref[...] = a_ref[...].astype(BF16)

    blk = pl.BlockSpec((tl,) + rest, lambda i: (i,) + zeros)
    return pl.pallas_call(
        body, name=name, grid=(lead // tl,), in_specs=[blk], out_specs=blk,
        out_shape=jax.ShapeDtypeStruct(a.shape, BF16), compiler_params=_cp(),
    )(a)


def _half_blockspec(t, idx_pos):
    if t == 0:
        return pl.BlockSpec((1, 512, 2048), lambda i, s: (i, s[idx_pos], 0)), 4
    if t == 1:
        return pl.BlockSpec((256, D), lambda i, s: (s[idx_pos], 0)), 1
    return pl.BlockSpec((256, 512), lambda i, s: (i, s[idx_pos])), 4


def _half_out_blockspec(t):
    if t == 0:
        return pl.BlockSpec((1, 512, 2048), lambda i, s: (i, 0, 0))
    if t == 1:
        return pl.BlockSpec((256, D), lambda i, s: (0, 0))
    return pl.BlockSpec((256, 512), lambda i, s: (i, 0))


def _add_half_call(t, own, recv, sc, name):
    in_blk, steps = _half_blockspec(t, 0)
    out_blk = _half_out_blockspec(t)

    def body(s_ref, a_ref, b_ref, o_ref, ob_ref):
        v = a_ref[...] + b_ref[...].astype(F32)
        o_ref[...] = v
        ob_ref[...] = v.astype(BF16)

    return pl.pallas_call(
        body, name=name,
        grid_spec=pltpu.PrefetchScalarGridSpec(
            num_scalar_prefetch=1, grid=(steps,), in_specs=[in_blk, out_blk], out_specs=[out_blk, out_blk]),
        out_shape=[jax.ShapeDtypeStruct(HALF_SHAPES[t], F32), jax.ShapeDtypeStruct(HALF_SHAPES[t], BF16)],
        compiler_params=_cp(),
    )(sc, own, recv)


def _final_piece_call(t, chipsum, recv3, sc, name):
    ps = PIECE_SHAPES[t]
    if t == 0:
        own_blk = pl.BlockSpec((1,) + ps, lambda i, s: (s[1], 0, 0))
    elif t == 1:
        own_blk = pl.BlockSpec(ps, lambda i, s: (0, s[1]))
    else:
        own_blk = pl.BlockSpec(ps, lambda i, s: (s[1], 0))
    r_blk = pl.BlockSpec((3,) + ps, lambda i, s: (0, 0, 0))
    o_blk = pl.BlockSpec(ps, lambda i, s: (0, 0))

    def body(s_ref, a_ref, r_ref, o_ref):
        a = a_ref[0] if t == 0 else a_ref[...]
        o_ref[...] = ((a + r_ref[0].astype(F32)) + r_ref[1].astype(F32)) + r_ref[2].astype(F32)

    return pl.pallas_call(
        body, name=name,
        grid_spec=pltpu.PrefetchScalarGridSpec(
            num_scalar_prefetch=1, grid=(1,), in_specs=[own_blk, r_blk], out_specs=o_blk),
        out_shape=jax.ShapeDtypeStruct(ps, F32), compiler_params=_cp(),
    )(sc, chipsum, recv3)


def _lower_bound_fn(hgrn_lb):
    return jnp.cumsum(jax.nn.softmax(hgrn_lb.astype(F32), axis=0), axis=0)[0]


def kernel(x, c, w_ada, b_ada, norm_g, w_in, hgrn_onorm_g, w_branch_a, w_branch_b, w_out, rel_bias, hgrn_lb, final_g, loss_target, m_w_ada, m_b_ada, m_norm_g, m_w_in, m_hgrn_onorm_g, m_w_branch_a, m_w_branch_b, m_w_out, m_rel_bias, m_hgrn_lb, m_final_g, v_w_ada, v_b_ada, v_norm_g, v_w_in, v_hgrn_onorm_g, v_w_branch_a, v_w_branch_b, v_w_out, v_rel_bias, v_hgrn_lb, v_final_g):
    ax, ay, ac = _me()
    chip = 2 * ax + ay
    dev = 4 * ax + 2 * ay + ac
    sc_idx = jnp.stack([ac, chip]).astype(jnp.int32)

    c_all = _allgather_small(jnp.pad(c, ((0, 7), (0, 0))), "gather_c").reshape(8, 8, D)[:, 0]
    b_s = lax.dynamic_slice(b_ada, (0, chip * 768), (1, 768))
    mod_part = _mod_call(c_all, w_ada[0], b_s)
    mod_all = _allgather_small(mod_part, "gather_mod").reshape(8, 8, 768)
    mod_mine = lax.dynamic_index_in_dim(mod_all, dev, axis=1, keepdims=False)
    mod = mod_mine[0::2].reshape(1, 3 * D)
    shift, scale, gate = mod[:, :D], mod[:, D:2 * D], mod[:, 2 * D:]

    win_f, wa_f, wb_f, wo_f = _gather_weights(
        _cast_call(w_in[0], "cast_w_in"), _cast_call(w_branch_a[0], "cast_w_a"),
        _cast_call(w_branch_b[0], "cast_w_b"), _cast_call(w_out[0], "cast_w_o"))

    lb, lb_vjp = jax.vjp(_lower_bound_fn, hgrn_lb)
    grad_x, d_win, d_wa, d_wb, d_wo, pack = _local_step(
        x[0], loss_target[0], shift, scale, gate, norm_g, hgrn_onorm_g, rel_bias, lb[None, :],
        final_g[None, :], win_f, wa_f, wb_f, wo_f)

    own = [d_win, d_wa, d_wb, d_wo]
    names = ["w_in", "w_a", "w_b", "w_o"]
    sib = _reduce_sibling_send([_cast_call(g, "cast_g_" + n) for g, n in zip(own, names)])
    halves = [_add_half_call(t, own[t], sib[t], sc_idx, "chipsum_" + names[t]) for t in range(4)]
    rec = _reduce_chips_send([hb for _, hb in halves])
    pieces = [_final_piece_call(t, halves[t][0], rec[t], sc_idx, "piece_" + names[t]) for t in range(4)]
    g_win, g_wa, g_wb, g_wo = _share_sibling(pieces)

    packs = _allgather_small(pack, "gather_small").reshape(8, 16, D)
    tot = _sum8_call(packs)
    loss = tot[7, 0]
    g_b_ada = tot[0:3].reshape(1, 3 * D)
    g_norm_g = tot[3:4]
    g_final_g = tot[4]
    (g_hgrn_lb,) = lb_vjp(tot[5])
    g_onorm = tot[6:7, :HK]
    g_rel = tot[8, :NBUCKETS * NH].reshape(NBUCKETS, NH)

    def rows_of(a):
        flat = a.reshape(-1)
        n = -(-flat.shape[0] // D)
        return jnp.pad(flat, (0, n * D - flat.shape[0])).reshape(n, D)

    smalls = [(b_ada, g_b_ada, m_b_ada, v_b_ada), (norm_g, g_norm_g, m_norm_g, v_norm_g),
              (hgrn_onorm_g, g_onorm, m_hgrn_onorm_g, v_hgrn_onorm_g), (rel_bias, g_rel, m_rel_bias, v_rel_bias),
              (hgrn_lb, g_hgrn_lb, m_hgrn_lb, v_hgrn_lb), (final_g, g_final_g, m_final_g, v_final_g)]
    cat = [jnp.concatenate([rows_of(s[k]) for s in smalls], 0) for k in range(4)]
    cat = [jnp.pad(a, ((0, 16 - a.shape[0]), (0, 0))) for a in cat]
    sd, sm, sv = _adamw_call(*cat, "adamw_small")

    def unpack(packed):
        res, r = [], 0
        for s in smalls:
            n = -(-s[0].size // D)
            res.append(packed[r:r + n].reshape(-1)[:s[0].size].reshape(s[0].shape))
            r += n
        return res

    d_small, m_small, v_small = unpack(sd), unpack(sm), unpack(sv)

    sc_all = c_all * jax.nn.sigmoid(c_all)
    dmod_all = packs[:, 0:3].reshape(8, 3 * D)
    dm_s = lax.dynamic_slice(dmod_all, (0, chip * 768), (8, 768))
    g_w_ada, d_w_ada, nm_w_ada, nv_w_ada = _ada_update_call(sc_all.T, dm_s, w_ada[0], m_w_ada[0], v_w_ada[0])

    big = []
    for w, g, m, v, n in [(w_in, g_win, m_w_in, v_w_in, "w_in"), (w_branch_a, g_wa, m_w_branch_a, v_w_branch_a, "w_a"),
                          (w_branch_b, g_wb, m_w_branch_b, v_w_branch_b, "w_b"), (w_out, g_wo, m_w_out, v_w_out, "w_o")]:
        big.append(_adamw_call(w[0], g, m[0], v[0], "adamw_" + n))

    e = lambda a: a[None]
    grads = [e(g_w_ada), g_b_ada, g_norm_g, e(g_win), g_onorm, e(g_wa), e(g_wb), e(g_wo), g_rel, g_hgrn_lb, g_final_g]
    deltas = [e(d_w_ada), d_small[0], d_small[1], e(big[0][0]), d_small[2], e(big[1][0]), e(big[2][0]), e(big[3][0]),
              d_small[3], d_small[4], d_small[5]]
    new_m = [e(nm_w_ada), m_small[0], m_small[1], e(big[0][1]), m_small[2], e(big[1][1]), e(big[2][1]), e(big[3][1]),
             m_small[3], m_small[4], m_small[5]]
    new_v = [e(nv_w_ada), v_small[0], v_small[1], e(big[0][2]), v_small[2], e(big[1][2]), e(big[2][2]), e(big[3][2]),
             v_small[3], v_small[4], v_small[5]]
    return (loss, grad_x[None], *grads, *deltas, *new_m, *new_v)
```

```python
import functools
import math

import numpy as np
import jax
import jax.numpy as jnp
from jax import lax
from jax.experimental import pallas as pl
from jax.experimental.pallas import tpu as pltpu

D = 1024
AW = 512
NH = 8
HE = 64
HK = 128
NPROJ = 8192
ABLK = 128
PATTERNS = (1, 4, 16)
NBUCKETS = 32
MAXDIST = 2048
NEG = -1e30
EPS = 1e-6
CH = 64
LR, B1, B2, AEPS, WD, STEP = 0.001, 0.9, 0.999, 1e-08, 0.01, 10

F32 = jnp.float32
BF16 = jnp.bfloat16
MESH = pl.DeviceIdType.MESH
VMEM_LIMIT = 56 * 1024 * 1024


def _cp(**kw):
    return pltpu.CompilerParams(vmem_limit_bytes=VMEM_LIMIT, **kw)


def _sig(x):
    return 1.0 / (1.0 + jnp.exp(-x))


def _nt(a, b):
    return lax.dot_general(a, b, (((1,), (1,)), ((), ())), preferred_element_type=F32)


def _tn(a, b):
    return lax.dot_general(a, b, (((0,), (0,)), ((), ())), preferred_element_type=F32)


def _nn(a, b):
    return jnp.dot(a, b, preferred_element_type=F32)


def _split3(x):
    h = x.astype(BF16)
    r = x - h.astype(F32)
    m = r.astype(BF16)
    l = (r - m.astype(F32)).astype(BF16)
    return h, m, l


def _exact_mm(tri_bf16, x):
    h, m, l = _split3(x)
    return _nn(tri_bf16, h) + _nn(tri_bf16, m) + _nn(tri_bf16, l)


def _exact_mm_r(x, ones_bf16):
    h, m, l = _split3(x)
    return _nn(h, ones_bf16) + _nn(m, ones_bf16) + _nn(l, ones_bf16)


def _h_call(x, avec):
    S = x.shape[0]
    tm = 512

    def body(x_ref, a_ref, h_ref):
        xv = x_ref[...]
        r = lax.rsqrt(jnp.mean(xv * xv, axis=-1, keepdims=True) + EPS)
        h_ref[...] = (xv * r * a_ref[0:1, :] + a_ref[1:2, :]).astype(BF16)

    return pl.pallas_call(
        body, name="h_norm", grid=(S // tm,),
        in_specs=[pl.BlockSpec((tm, D), lambda i: (i, 0)), pl.BlockSpec((8, D), lambda i: (0, 0))],
        out_specs=pl.BlockSpec((tm, D), lambda i: (i, 0)),
        out_shape=jax.ShapeDtypeStruct((S, D), BF16), compiler_params=_cp(),
    )(x, avec)


def _proj_call(h, w_in):
    S = h.shape[0]
    tm, tn = 512, 2048

    def body(h_ref, w_ref, o_ref):
        o_ref[...] = _nn(h_ref[...], w_ref[...])

    return pl.pallas_call(
        body, name="in_proj", grid=(NPROJ // tn, S // tm),
        in_specs=[pl.BlockSpec((tm, D), lambda j, i: (i, 0)), pl.BlockSpec((D, tn), lambda j, i: (0, j))],
        out_specs=pl.BlockSpec((tm, tn), lambda j, i: (i, j)),
        out_shape=jax.ShapeDtypeStruct((S, NPROJ), F32), compiler_params=_cp(),
    )(h, w_in)


def _t5_bucket_np(dist):
    max_exact = NBUCKETS // 2
    n = dist.astype(np.float32)
    large = max_exact + (np.log(np.maximum(n, np.float32(1.0)) / np.float32(max_exact))
                         / np.float32(math.log(MAXDIST / max_exact))
                         * np.float32(NBUCKETS - max_exact)).astype(np.int32)
    large = np.minimum(large, NBUCKETS - 1)
    return np.where(dist < max_exact, dist, large)


def _band_bucket(d):
    qi = np.arange(ABLK)[:, None]
    kj = np.arange(2 * ABLK)[None, :]
    delta = qi + ABLK - kj
    band = (delta >= 0) & (delta <= ABLK)
    bucket = _t5_bucket_np(np.clip(delta, 0, None) * d)
    return band, bucket


def _bias_tiles(rel_bias):
    tiles = []
    for d in PATTERNS:
        band, bucket = _band_bucket(d)
        onehot = (jnp.asarray(bucket, jnp.int32)[None] == jnp.arange(NBUCKETS, dtype=jnp.int32)[:, None, None])
        bias = jnp.einsum("bqk,bh->hqk", onehot.astype(F32), rel_bias, precision=lax.Precision.HIGHEST)
        tiles.append(jnp.where(jnp.asarray(band)[None], bias, NEG))
    return jnp.stack(tiles, 0)


ATT = 2048
HP = 2 * HE


def _attn_blocks():
    out = []
    for p, d in enumerate(PATTERNS):
        for r in range(d):
            for n in range(ATT // (d * ABLK)):
                out.append((p, d, r, n))
    return out


def _attn_fwd_call(proj, biases):
    S = proj.shape[0]
    nt = S // ATT

    def body(q_ref, k_ref, v_ref, z_ref, b_ref, a_ref, l_ref, oa_ref, kc, vc, acc, mm, ll):
        i = pl.program_id(1)

        @pl.when(i == 0)
        def _():
            kc[0:ATT] = jnp.zeros((ATT, HP), F32)
            vc[0:ATT] = jnp.zeros((ATT, HP), F32)

        @pl.when(i > 0)
        def _():
            kc[0:ATT] = kc[ATT:2 * ATT]
            vc[0:ATT] = vc[ATT:2 * ATT]

        kc[ATT:2 * ATT] = k_ref[...]
        vc[ATT:2 * ATT] = v_ref[...]
        col = lax.broadcasted_iota(jnp.int32, (ABLK, 2 * ABLK), 1)
        dead = jnp.logical_and(i == 0, col < ABLK)
        for p, d, r, n in _attn_blocks():
            qrows = pl.ds(n * ABLK * d + r, ABLK, stride=d)
            krows = pl.ds(ATT + (n - 1) * ABLK * d + r, 2 * ABLK, stride=d)
            q = q_ref[qrows, :].astype(BF16)
            k = kc[krows, :].astype(BF16)
            v = vc[krows, :].astype(BF16)
            if p > 0:
                mo, lo, ao = mm[qrows, :], ll[qrows, :], acc[qrows, :]
            ms, ls, as_ = [], [], []
            for e in range(2):
                sl = slice(e * HE, (e + 1) * HE)
                s = _nt(q[:, sl], k[:, sl]) * (HE ** -0.5) + b_ref[p, e]
                if n == 0:
                    s = jnp.where(dead, NEG, s)
                mx = jnp.max(s, axis=-1, keepdims=True)
                if p == 0:
                    pe = jnp.exp(s - mx)
                    ln = jnp.sum(pe, axis=-1, keepdims=True)
                    an = _nn(pe.astype(BF16), v[:, sl])
                else:
                    moe = mo[:, e * HE:e * HE + 1]
                    mx = jnp.maximum(moe, mx)
                    alpha = jnp.exp(moe - mx)
                    pe = jnp.exp(s - mx)
                    ln = alpha * lo[:, e * HE:e * HE + 1] + jnp.sum(pe, axis=-1, keepdims=True)
                    an = alpha * ao[:, sl] + _nn(pe.astype(BF16), v[:, sl])
                ms.append(jnp.broadcast_to(mx, (ABLK, HE)))
                ls.append(jnp.broadcast_to(ln, (ABLK, HE)))
                as_.append(an)
            mm[qrows, :] = jnp.concatenate(ms, axis=1)
            ll[qrows, :] = jnp.concatenate(ls, axis=1)
            acc[qrows, :] = jnp.concatenate(as_, axis=1)
        lv = ll[...]
        att = acc[...] / lv
        a_ref[...] = att
        l_ref[...] = mm[...] + jnp.log(lv)
        z = z_ref[...]
        oa_ref[...] = (att * (z * _sig(z))).astype(BF16)

    def pcol(c):
        return pl.BlockSpec((ATT, HP), lambda h, i: (i, c * 4 + h))

    out = pl.BlockSpec((ATT, HP), lambda h, i: (i, h))
    return pl.pallas_call(
        body, name="attn_fwd", grid=(4, nt),
        in_specs=[pcol(0), pcol(1), pcol(2), pcol(3),
                  pl.BlockSpec((3, 2, ABLK, 2 * ABLK), lambda h, i: (0, h, 0, 0))],
        out_specs=[out, out, out],
        out_shape=[jax.ShapeDtypeStruct((S, AW), F32), jax.ShapeDtypeStruct((S, AW), F32),
                   jax.ShapeDtypeStruct((S, AW), BF16)],
        scratch_shapes=[pltpu.VMEM((2 * ATT, HP), F32), pltpu.VMEM((2 * ATT, HP), F32),
                        pltpu.VMEM((ATT, HP), F32), pltpu.VMEM((ATT, HP), F32), pltpu.VMEM((ATT, HP), F32)],
        compiler_params=_cp(),
    )(proj, proj, proj, proj, biases)


def _attn_bwd_call(proj, dattn, lse, dsum, biases):
    S = proj.shape[0]
    nt = S // ATT

    def body(q_ref, k_ref, v_ref, do_ref, l_ref, ds_ref, b_ref, dq_ref, dk_ref, dv_ref, db_ref,
             kc, vc, dkc, dvc, dqa):
        i = pl.program_id(1)

        @pl.when(i == 0)
        def _():
            kc[ATT:2 * ATT] = jnp.zeros((ATT, HP), F32)
            vc[ATT:2 * ATT] = jnp.zeros((ATT, HP), F32)
            dkc[ATT:2 * ATT] = jnp.zeros((ATT, HP), F32)
            dvc[ATT:2 * ATT] = jnp.zeros((ATT, HP), F32)
            db_ref[...] = jnp.zeros_like(db_ref)

        @pl.when(i < nt)
        def _():
            kc[0:ATT] = kc[ATT:2 * ATT]
            vc[0:ATT] = vc[ATT:2 * ATT]
            dkc[0:ATT] = dkc[ATT:2 * ATT]
            dvc[0:ATT] = dvc[ATT:2 * ATT]
            kc[ATT:2 * ATT] = k_ref[...]
            vc[ATT:2 * ATT] = v_ref[...]
            dkc[ATT:2 * ATT] = jnp.zeros((ATT, HP), F32)
            dvc[ATT:2 * ATT] = jnp.zeros((ATT, HP), F32)
            col = lax.broadcasted_iota(jnp.int32, (ABLK, 2 * ABLK), 1)
            dead = jnp.logical_and(i == 0, col < ABLK)
            for p, d, r, n in _attn_blocks():
                qrows = pl.ds(n * ABLK * d + r, ABLK, stride=d)
                krows = pl.ds(ATT + (n - 1) * ABLK * d + r, 2 * ABLK, stride=d)
                q = q_ref[qrows, :].astype(BF16)
                k = kc[krows, :].astype(BF16)
                v = vc[krows, :].astype(BF16)
                do = do_ref[qrows, :].astype(BF16)
                lv = l_ref[qrows, :]
                dsv = ds_ref[qrows, :]
                dqs, dks, dvs = [], [], []
                for e in range(2):
                    sl = slice(e * HE, (e + 1) * HE)
                    s = _nt(q[:, sl], k[:, sl]) * (HE ** -0.5) + b_ref[p, e]
                    if n == 0:
                        s = jnp.where(dead, NEG, s)
                    pe = jnp.exp(s - lv[:, e * HE:e * HE + 1])
                    dp = _nt(do[:, sl], v[:, sl])
                    dsc = pe * (dp - dsv[:, e * HE:e * HE + 1])
                    db_ref[p, e] += dsc
                    dsb = (dsc * (HE ** -0.5)).astype(BF16)
                    dqs.append(_nn(dsb, k[:, sl]))
                    dks.append(_tn(dsb, q[:, sl]))
                    dvs.append(_tn(pe.astype(BF16), do[:, sl]))
                dq = jnp.concatenate(dqs, axis=1)
                if p == 0:
                    dqa[qrows, :] = dq
                else:
                    dqa[qrows, :] += dq
                dkc[krows, :] += jnp.concatenate(dks, axis=1)
                dvc[krows, :] += jnp.concatenate(dvs, axis=1)
            dq_ref[...] = dqa[...].astype(BF16)
            dk_ref[...] = dkc[0:ATT].astype(BF16)
            dv_ref[...] = dvc[0:ATT].astype(BF16)

        @pl.when(i == nt)
        def _():
            dk_ref[...] = dkc[ATT:2 * ATT].astype(BF16)
            dv_ref[...] = dvc[ATT:2 * ATT].astype(BF16)

    def pcol(c):
        return pl.BlockSpec((ATT, HP), lambda h, i: (jnp.minimum(i, nt - 1), c * 4 + h))

    qrow = pl.BlockSpec((ATT, HP), lambda h, i: (jnp.minimum(i, nt - 1), h))
    krow = pl.BlockSpec((ATT, HP), lambda h, i: (jnp.maximum(i - 1, 0), h))
    bspec = pl.BlockSpec((3, 2, ABLK, 2 * ABLK), lambda h, i: (0, h, 0, 0))
    return pl.pallas_call(
        body, name="attn_bwd", grid=(4, nt + 1),
        in_specs=[pcol(0), pcol(1), pcol(2), qrow, qrow, qrow, bspec],
        out_specs=[qrow, krow, krow, bspec],
        out_shape=[jax.ShapeDtypeStruct((S, AW), BF16)] * 3
                  + [jax.ShapeDtypeStruct((3, NH, ABLK, 2 * ABLK), F32)],
        scratch_shapes=[pltpu.VMEM((2 * ATT, HP), F32)] * 4 + [pltpu.VMEM((ATT, HP), F32)],
        compiler_params=_cp(),
    )(proj, proj, proj, dattn, lse, dsum, biases)


HRB = 256


def _hgrn_gates(q_ref, f_ref, rows, lbv, tri):
    qraw = q_ref[rows, :]
    sq = _sig(qraw)
    q = qraw * sq
    sf = _sig(f_ref[rows, :])
    f = lbv + (1.0 - lbv) * sf
    k = 1.0 - f
    b = _exact_mm(tri, jnp.log(f))
    bl = b[CH - 1:CH, :]
    bm = b[CH // 2 - 1:CH // 2, :]
    eb = jnp.exp(b)
    qe = q * eb
    qs = q * jnp.exp(b - bm)
    ks = k * jnp.exp(bm - b)
    kd = k * jnp.exp(bl - b)
    return dict(qraw=qraw, sq=sq, q=q, sf=sf, f=f, k=k, b=b, bl=bl, bm=bm, eb=eb, qe=qe, qs=qs, ks=ks, kd=kd)


def _tri_masks():
    row = lax.broadcasted_iota(jnp.int32, (CH, CH), 0)
    col = lax.broadcasted_iota(jnp.int32, (CH, CH), 1)
    return row >= col


def _hgrn_fwd_call(proj, lb, gn):
    S = proj.shape[0]
    nc = S // CH
    cps = HRB // CH

    def body(q_ref, f_ref, i_ref, z_ref, lb_ref, gn_ref, or_ref, ob_ref, st_ref, st):
        @pl.when(pl.program_id(0) == 0)
        def _():
            st[...] = jnp.zeros_like(st)

        low = _tri_masks()
        tri = low.astype(BF16)
        lbv = lb_ref[...]
        for ci in range(cps):
            rows = slice(ci * CH, (ci + 1) * CH)
            g = _hgrn_gates(q_ref, f_ref, rows, lbv, tri)
            v = i_ref[rows, :]
            ebl = jnp.exp(g["bl"])
            st_ref[ci] = st[...]
            for hh in range(NH):
                sl = slice(hh * HK, (hh + 1) * HK)
                s0 = st[sl, :]
                vb = v[:, sl].astype(BF16)
                a = jnp.where(low, _nt(g["qs"][:, sl].astype(BF16), g["ks"][:, sl].astype(BF16)), 0.0)
                o = _nt(g["qe"][:, sl].astype(BF16), s0.astype(BF16)) + _nn(a.astype(BF16), vb)
                st[sl, :] = s0 * ebl[:, sl] + _tn(vb, g["kd"][:, sl].astype(BF16))
                or_ref[rows, sl] = o
                r = lax.rsqrt(jnp.mean(o * o, axis=-1, keepdims=True) + EPS)
                z = z_ref[rows, sl]
                ob_ref[rows, sl] = (o * r * gn_ref[:, sl] * (z * _sig(z))).astype(BF16)

    def pcol(c):
        return pl.BlockSpec((HRB, D), lambda i: (i, c))

    vec = pl.BlockSpec((1, D), lambda i: (0, 0))
    row = pl.BlockSpec((HRB, D), lambda i: (i, 0))
    return pl.pallas_call(
        body, name="hgrn_fwd", grid=(S // HRB,),
        in_specs=[pcol(2), pcol(3), pcol(4), pcol(5), vec, vec],
        out_specs=[row, row, pl.BlockSpec((cps, NH * HK, HK), lambda i: (i, 0, 0))],
        out_shape=[jax.ShapeDtypeStruct((S, D), F32), jax.ShapeDtypeStruct((S, D), BF16),
                   jax.ShapeDtypeStruct((nc, NH * HK, HK), F32)],
        scratch_shapes=[pltpu.VMEM((NH * HK, HK), F32)],
        compiler_params=_cp(),
    )(proj, proj, proj, proj, lb, gn)


def _hgrn_bwd_call(proj, oraw, dob, states, lb, gn):
    S = proj.shape[0]
    nblk = S // HRB
    cps = HRB // CH

    def body(q_ref, f_ref, i_ref, z_ref, or_ref, dob_ref, st_ref, lb_ref, gn_ref, dh_ref, acc_ref, dst):
        @pl.when(pl.program_id(0) == 0)
        def _():
            dst[...] = jnp.zeros_like(dst)
            acc_ref[...] = jnp.zeros_like(acc_ref)

        low = _tri_masks()
        tri = low.astype(BF16)
        triu = jnp.logical_not(_tri_masks()) | (lax.broadcasted_iota(jnp.int32, (CH, CH), 0)
                                               == lax.broadcasted_iota(jnp.int32, (CH, CH), 1))
        triu = triu.astype(BF16)
        lbv = lb_ref[...]
        for ci in reversed(range(cps)):
            rows = slice(ci * CH, (ci + 1) * CH)
            g = _hgrn_gates(q_ref, f_ref, rows, lbv, tri)
            v = i_ref[rows, :]
            ebl = jnp.exp(g["bl"])
            dqs_, dks_, dbs_, dvs_, dzs_, exs_, dgn_ = [], [], [], [], [], [], []
            for hh in reversed(range(NH)):
                sl = slice(hh * HK, (hh + 1) * HK)
                o = or_ref[rows, sl]
                z = z_ref[rows, sl]
                sz = _sig(z)
                gnv = gn_ref[:, sl]
                r = lax.rsqrt(jnp.mean(o * o, axis=-1, keepdims=True) + EPS)
                dobv = dob_ref[rows, sl]
                don = dobv * (z * sz)
                dzs_.append(dobv * (o * r * gnv) * (sz * (1.0 + z * (1.0 - sz))))
                dgn_.append(jnp.sum(don * o * r, axis=0, keepdims=True))
                gh = don * gnv
                do = r * gh - o * (r * r * r * jnp.mean(gh * o, axis=-1, keepdims=True))
                dob16 = do.astype(BF16)

                st0 = st_ref[ci, sl, :]
                dst1 = dst[sl, :]
                vb = v[:, sl].astype(BF16)
                qsb = g["qs"][:, sl].astype(BF16)
                ksb = g["ks"][:, sl].astype(BF16)
                qeb = g["qe"][:, sl].astype(BF16)
                kdb = g["kd"][:, sl].astype(BF16)
                dst1b = dst1.astype(BF16)
                a = jnp.where(low, _nt(qsb, ksb), 0.0).astype(BF16)
                da = jnp.where(low, _nt(dob16, vb), 0.0).astype(BF16)
                dqe = _nn(dob16, st0.astype(BF16))
                dqs = _nn(da, ksb)
                dks = _tn(da, qsb)
                dv = _tn(a, dob16) + _nt(kdb, dst1b)
                dkd = _nn(vb, dst1b)
                eb = g["eb"][:, sl]
                b = g["b"][:, sl]
                dq = dqe * eb + dqs * jnp.exp(b - g["bm"][:, sl])
                dk = dks * jnp.exp(g["bm"][:, sl] - b) + dkd * jnp.exp(g["bl"][:, sl] - b)
                db = (dqe * g["qe"][:, sl] + dqs * qsb.astype(F32)
                      - dks * ksb.astype(F32) - dkd * g["kd"][:, sl])
                ex = (jnp.sum(dkd * g["kd"][:, sl], axis=0, keepdims=True)
                      + jnp.sum(dst1 * st0, axis=0, keepdims=True) * ebl[:, sl])
                dst[sl, :] = _tn(dob16, qeb) + dst1 * ebl[:, sl]
                dqs_.append(dq); dks_.append(dk); dbs_.append(db); dvs_.append(dv); exs_.append(ex)

            cat = lambda lst: jnp.concatenate(lst[::-1], axis=1)
            dq, dk, db, dv, dz, ex, dgn = (cat(t) for t in (dqs_, dks_, dbs_, dvs_, dzs_, exs_, dgn_))
            dg = _exact_mm(triu, db) + ex
            df = dg / g["f"] - dk
            sf = g["sf"]
            dfr = df * (1.0 - lbv) * sf * (1.0 - sf)
            sq = g["sq"]
            dqr = dq * (sq * (1.0 + g["qraw"] * (1.0 - sq)))
            acc_ref[0:1, :] += jnp.sum(df * (1.0 - sf), axis=0, keepdims=True)
            acc_ref[1:2, :] += dgn
            dh_ref[rows, 0:D] = dqr.astype(BF16)
            dh_ref[rows, D:2 * D] = dfr.astype(BF16)
            dh_ref[rows, 2 * D:3 * D] = dv.astype(BF16)
            dh_ref[rows, 3 * D:4 * D] = dz.astype(BF16)

    def pcol(c):
        return pl.BlockSpec((HRB, D), lambda i: (nblk - 1 - i, c))

    vec = pl.BlockSpec((1, D), lambda i: (0, 0))
    row = pl.BlockSpec((HRB, D), lambda i: (nblk - 1 - i, 0))
    return pl.pallas_call(
        body, name="hgrn_bwd", grid=(nblk,),
        in_specs=[pcol(2), pcol(3), pcol(4), pcol(5), row, row,
                  pl.BlockSpec((cps, NH * HK, HK), lambda i: (nblk - 1 - i, 0, 0)), vec, vec],
        out_specs=[pl.BlockSpec((HRB, 4 * D), lambda i: (nblk - 1 - i, 0)),
                   pl.BlockSpec((8, D), lambda i: (0, 0))],
        out_shape=[jax.ShapeDtypeStruct((S, 4 * D), BF16), jax.ShapeDtypeStruct((8, D), F32)],
        scratch_shapes=[pltpu.VMEM((NH * HK, HK), F32)],
        compiler_params=_cp(),
    )(proj, proj, proj, proj, oraw, dob, states, lb, gn)


def _fwd2_call(oa, ob, proj, x, tgt, vecs, wa, wb, wo):
    S = x.shape[0]
    tm = 256

    def body(oa_ref, ob_ref, ga_ref, gb_ref, x_ref, t_ref, v_ref, wa_ref, wb_ref, wo_ref,
             ya_ref, yb_ref, y_ref, u_ref, x2_ref, ls_ref):
        @pl.when(pl.program_id(0) == 0)
        def _():
            ls_ref[...] = jnp.zeros_like(ls_ref)

        ya = _nn(oa_ref[...], wa_ref[...])
        yb = _nn(ob_ref[...], wb_ref[...])
        y = _sig(ga_ref[...]) * ya + _sig(gb_ref[...]) * yb
        u = _nn(y.astype(BF16), wo_ref[...])
        x2 = x_ref[...] + v_ref[0:1, :] * u
        r = lax.rsqrt(jnp.mean(x2 * x2, axis=-1, keepdims=True) + EPS)
        err = x2 * r * v_ref[1:2, :] - t_ref[...]
        ls_ref[...] += jnp.sum(err * err)
        ya_ref[...] = ya.astype(BF16)
        yb_ref[...] = yb.astype(BF16)
        y_ref[...] = y.astype(BF16)
        u_ref[...] = u.astype(BF16)
        x2_ref[...] = x2

    row = pl.BlockSpec((tm, D), lambda i: (i, 0))
    full = lambda a: pl.BlockSpec(a.shape, lambda i: (0, 0))
    return pl.pallas_call(
        body, name="fwd_merge_out", grid=(S // tm,),
        in_specs=[pl.BlockSpec((tm, AW), lambda i: (i, 0)), row,
                  pl.BlockSpec((tm, D), lambda i: (i, 6)), pl.BlockSpec((tm, D), lambda i: (i, 7)),
                  row, row, pl.BlockSpec((8, D), lambda i: (0, 0)), full(wa), full(wb), full(wo)],
        out_specs=[row, row, row, row, row, pl.BlockSpec((8, 128), lambda i: (0, 0))],
        out_shape=[jax.ShapeDtypeStruct((S, D), BF16)] * 4
                  + [jax.ShapeDtypeStruct((S, D), F32), jax.ShapeDtypeStruct((8, 128), F32)],
        compiler_params=_cp(),
    )(oa, ob, proj, proj, x, tgt, vecs, wa, wb, wo)


def _bwd2_call(x2, tgt, vecs, ya, yb, u, proj, attn, wa, wb, wo, hsum):
    S = x2.shape[0]
    tm = 256

    def body(x2_ref, t_ref, v_ref, ya_ref, yb_ref, u_ref, ga_ref, gb_ref, at_ref, za_ref,
             wa_ref, wb_ref, wo_ref, hs_ref,
             dx2_ref, du_ref, dya_ref, dyb_ref, dg_ref, dat_ref, dsum_ref, dza_ref, dob_ref, acc_ref):
        @pl.when(pl.program_id(0) == 0)
        def _():
            acc_ref[...] = jnp.zeros_like(acc_ref)

        x2v = x2_ref[...]
        gate = v_ref[0:1, :]
        fg = v_ref[1:2, :]
        r = lax.rsqrt(jnp.mean(x2v * x2v, axis=-1, keepdims=True) + EPS)
        dout = (x2v * r * fg - t_ref[...]) * (1.0 / D)
        gh = dout * fg
        dx2 = r * gh - x2v * (r * r * r * jnp.mean(gh * x2v, axis=-1, keepdims=True))
        acc_ref[0:1, :] += jnp.sum(dx2 * u_ref[...].astype(F32), axis=0, keepdims=True)
        acc_ref[1:2, :] += jnp.sum(dout * x2v * r, axis=0, keepdims=True)
        dx2_ref[...] = dx2
        du = (dx2 * gate).astype(BF16)
        du_ref[...] = du
        dy = _nt(du, wo_ref[...])
        sa = _sig(ga_ref[...])
        sb = _sig(gb_ref[...])
        dya = (dy * sa).astype(BF16)
        dyb = (dy * sb).astype(BF16)
        dya_ref[...] = dya
        dyb_ref[...] = dyb
        dg_ref[:, 0:D] = (dy * ya_ref[...].astype(F32) * sa * (1.0 - sa)).astype(BF16)
        dg_ref[:, D:2 * D] = (dy * yb_ref[...].astype(F32) * sb * (1.0 - sb)).astype(BF16)
        doa = _nt(dya, wa_ref[...])
        dob_ref[...] = _nt(dyb, wb_ref[...])
        za = za_ref[...]
        sz = _sig(za)
        att = at_ref[...]
        dat = doa * (za * sz)
        dat_ref[...] = dat
        dza_ref[...] = (doa * att * (sz * (1.0 + za * (1.0 - sz)))).astype(BF16)
        dsum_ref[...] = _exact_mm_r(dat * att, hs_ref[...])

    row = pl.BlockSpec((tm, D), lambda i: (i, 0))
    arow = pl.BlockSpec((tm, AW), lambda i: (i, 0))
    full = lambda a: pl.BlockSpec(a.shape, lambda i: (0, 0))
    return pl.pallas_call(
        body, name="bwd_merge_out", grid=(S // tm,),
        in_specs=[row, row, pl.BlockSpec((8, D), lambda i: (0, 0)), row, row, row,
                  pl.BlockSpec((tm, D), lambda i: (i, 6)), pl.BlockSpec((tm, D), lambda i: (i, 7)),
                  arow, pl.BlockSpec((tm, AW), lambda i: (i, 3)), full(wa), full(wb), full(wo), full(hsum)],
        out_specs=[row, row, row, row, pl.BlockSpec((tm, 2 * D), lambda i: (i, 0)),
                   arow, arow, arow, row, pl.BlockSpec((8, D), lambda i: (0, 0))],
        out_shape=[jax.ShapeDtypeStruct((S, D), F32), jax.ShapeDtypeStruct((S, D), BF16),
                   jax.ShapeDtypeStruct((S, D), BF16), jax.ShapeDtypeStruct((S, D), BF16),
                   jax.ShapeDtypeStruct((S, 2 * D), BF16), jax.ShapeDtypeStruct((S, AW), F32),
                   jax.ShapeDtypeStruct((S, AW), F32), jax.ShapeDtypeStruct((S, AW), BF16),
                   jax.ShapeDtypeStruct((S, D), F32), jax.ShapeDtypeStruct((8, D), F32)],
        compiler_params=_cp(),
    )(x2, tgt, vecs, ya, yb, u, proj, proj, attn, proj, wa, wb, wo, hsum)


def _atb_call(a, b, name):
    S, K = a.shape
    N = b.shape[1]
    tm = 512

    def body(a_ref, b_ref, o_ref):
        @pl.when(pl.program_id(0) == 0)
        def _():
            o_ref[...] = jnp.zeros_like(o_ref)

        o_ref[...] += _tn(a_ref[...], b_ref[...])

    return pl.pallas_call(
        body, name=name, grid=(S // tm,),
        in_specs=[pl.BlockSpec((tm, K), lambda i: (i, 0)), pl.BlockSpec((tm, N), lambda i: (i, 0))],
        out_specs=pl.BlockSpec((K, N), lambda i: (0, 0)),
        out_shape=jax.ShapeDtypeStruct((K, N), F32), compiler_params=_cp(),
    )(a, b)


def _dwin_call(h, dqkvz, d_hgrn, d_gates):
    S = h.shape[0]
    tm = 512
    tn = 2048

    def body(h_ref, q_ref, k_ref, v_ref, z_ref, m_ref, g_ref, o_ref):
        j = pl.program_id(0)

        @pl.when(pl.program_id(1) == 0)
        def _():
            o_ref[...] = jnp.zeros_like(o_ref)

        hv = h_ref[...]

        @pl.when(j == 0)
        def _():
            for cidx, r in enumerate((q_ref, k_ref, v_ref, z_ref)):
                o_ref[0, :, cidx * AW:(cidx + 1) * AW] += _tn(hv, r[...])

        @pl.when(jnp.logical_or(j == 1, j == 2))
        def _():
            o_ref[0] += _tn(hv, m_ref[...])

        @pl.when(j == 3)
        def _():
            o_ref[0] += _tn(hv, g_ref[...])

    aspec = pl.BlockSpec((tm, AW), lambda j, i: (jnp.where(j == 0, i, 0), 0))
    return pl.pallas_call(
        body, name="dw_in", grid=(4, S // tm),
        in_specs=[pl.BlockSpec((tm, D), lambda j, i: (i, 0)), aspec, aspec, aspec, aspec,
                  pl.BlockSpec((tm, tn), lambda j, i: (jnp.where(jnp.logical_or(j == 1, j == 2), i, 0),
                                                       jnp.where(j == 2, 1, 0))),
                  pl.BlockSpec((tm, tn), lambda j, i: (jnp.where(j == 3, i, 0), 0))],
        out_specs=pl.BlockSpec((1, D, tn), lambda j, i: (j, 0, 0)),
        out_shape=jax.ShapeDtypeStruct((4, D, tn), F32), compiler_params=_cp(),
    )(h, *dqkvz, d_hgrn, d_gates)


def _dh_call(dqkvz, d_hgrn, d_gates, w_in, x, dx2, vecs):
    S = x.shape[0]
    tm = 512
    tk = 2048

    def body(q_ref, k_ref, v_ref, z_ref, m_ref, g_ref, w_ref, x_ref, dx2_ref, p_ref, gx_ref, acc_ref, dh):
        j = pl.program_id(1)

        @pl.when(jnp.logical_and(pl.program_id(0) == 0, j == 0))
        def _():
            acc_ref[...] = jnp.zeros_like(acc_ref)

        @pl.when(j == 0)
        def _():
            t = _nt(q_ref[...], w_ref[:, 0:AW])
            for cidx, r in enumerate((k_ref, v_ref, z_ref)):
                t += _nt(r[...], w_ref[:, (cidx + 1) * AW:(cidx + 2) * AW])
            dh[...] = t

        @pl.when(jnp.logical_or(j == 1, j == 2))
        def _():
            dh[...] += _nt(m_ref[...], w_ref[...])

        @pl.when(j == 3)
        def _():
            dhv = dh[...] + _nt(g_ref[...], w_ref[...])
            xv = x_ref[...]
            r = lax.rsqrt(jnp.mean(xv * xv, axis=-1, keepdims=True) + EPS)
            xn = xv * r
            acc_ref[0:1, :] += jnp.sum(dhv, axis=0, keepdims=True)
            acc_ref[1:2, :] += jnp.sum(dhv * xn * p_ref[1:2, :], axis=0, keepdims=True)
            acc_ref[2:3, :] += jnp.sum(dhv * xn * p_ref[2:3, :], axis=0, keepdims=True)
            dxn = dhv * p_ref[0:1, :]
            gx_ref[...] = dx2_ref[...] + r * dxn - xv * (r * r * r * jnp.mean(dxn * xv, axis=-1, keepdims=True))

    row = pl.BlockSpec((tm, D), lambda i, j: (i, 0))
    aspec = pl.BlockSpec((tm, AW), lambda i, j: (i, 0))
    return pl.pallas_call(
        body, name="dh_gradx", grid=(S // tm, 4),
        in_specs=[aspec, aspec, aspec, aspec,
                  pl.BlockSpec((tm, tk), lambda i, j: (i, jnp.where(j == 2, 1, 0))),
                  pl.BlockSpec((tm, tk), lambda i, j: (i, 0)),
                  pl.BlockSpec((D, tk), lambda i, j: (0, j)),
                  row, row, pl.BlockSpec((8, D), lambda i, j: (0, 0))],
        out_specs=[row, pl.BlockSpec((8, D), lambda i, j: (0, 0))],
        out_shape=[jax.ShapeDtypeStruct((S, D), F32), jax.ShapeDtypeStruct((8, D), F32)],
        scratch_shapes=[pltpu.VMEM((tm, D), F32)],
        compiler_params=_cp(),
    )(*dqkvz, d_hgrn, d_gates, w_in, x, dx2, vecs)


def _adamw_math(w, g, m, v):
    m = B1 * m + (1.0 - B1) * g
    v = B2 * v + (1.0 - B2) * (g * g)
    m_hat = m / (1.0 - B1 ** STEP)
    v_hat = v / (1.0 - B2 ** STEP)
    delta = -LR * (m_hat / (jnp.sqrt(v_hat) + AEPS) + WD * w)
    return delta, m, v


def _adamw_call(w, g, m, v, name):
    R, C = w.shape
    tr = R if R * C * 4 <= (1 << 20) else max(8, (1 << 20) // (C * 4))
    assert R % tr == 0

    def body(w_ref, g_ref, m_ref, v_ref, d_ref, nm_ref, nv_ref):
        d_ref[...], nm_ref[...], nv_ref[...] = _adamw_math(w_ref[...], g_ref[...], m_ref[...], v_ref[...])

    blk = pl.BlockSpec((tr, C), lambda i: (i, 0))
    return pl.pallas_call(
        body, name=name, grid=(R // tr,), in_specs=[blk] * 4, out_specs=[blk] * 3,
        out_shape=[jax.ShapeDtypeStruct((R, C), F32)] * 3, compiler_params=_cp(),
    )(w, g, m, v)


def _mod_call(c_all, w_ada_s, b_s):
    def body(c_ref, w_ref, b_ref, o_ref):
        cv = c_ref[...]
        sc = cv * _sig(cv)
        o_ref[...] = jnp.dot(sc, w_ref[...], preferred_element_type=F32,
                             precision=lax.Precision.HIGHEST) + b_ref[...]

    return pl.pallas_call(
        body, name="ada_mod", out_shape=jax.ShapeDtypeStruct((8, w_ada_s.shape[1]), F32),
        compiler_params=_cp(),
    )(c_all, w_ada_s, b_s)


def _ada_update_call(sct, dm, w, m, v):
    R, C = w.shape
    tr = 256

    def body(s_ref, d_ref, w_ref, m_ref, v_ref, g_ref, dl_ref, nm_ref, nv_ref):
        g = s_ref[:, 0:1] * d_ref[0:1, :]
        for b in range(1, 8):
            g = g + s_ref[:, b:b + 1] * d_ref[b:b + 1, :]
        g_ref[...] = g
        dl_ref[...], nm_ref[...], nv_ref[...] = _adamw_math(w_ref[...], g, m_ref[...], v_ref[...])

    blk = pl.BlockSpec((tr, C), lambda i: (i, 0))
    return pl.pallas_call(
        body, name="ada_update", grid=(R // tr,),
        in_specs=[pl.BlockSpec((tr, 8), lambda i: (i, 0)), pl.BlockSpec((8, C), lambda i: (0, 0)), blk, blk, blk],
        out_specs=[blk] * 4, out_shape=[jax.ShapeDtypeStruct((R, C), F32)] * 4, compiler_params=_cp(),
    )(sct, dm, w, m, v)


def _sum8_call(packs):
    def body(p_ref, o_ref):
        acc = p_ref[0]
        for k in range(1, 8):
            acc = acc + p_ref[k]
        o_ref[...] = acc

    return pl.pallas_call(
        body, name="sum_small", out_shape=jax.ShapeDtypeStruct(packs.shape[1:], F32), compiler_params=_cp(),
    )(packs)


def _local_step(x, tgt, shift, scale, gate, norm_g, hgrn_onorm_g, rel_bias, lb, final_g, w_in, wa, wb, wo):
    a = norm_g * (1.0 + scale)
    z6 = jnp.zeros((6, D), F32)
    h = _h_call(x, jnp.concatenate([a, shift, z6], 0))
    proj = _proj_call(h, w_in)

    biases = _bias_tiles(rel_bias)
    attn, lse, oa = _attn_fwd_call(proj, biases)

    gn = jnp.tile(hgrn_onorm_g, (1, NH))
    oraw, ob, states = _hgrn_fwd_call(proj, lb, gn)

    vecs2 = jnp.concatenate([gate, final_g, z6], 0)
    ya, yb, y, u, x2, lsq = _fwd2_call(oa, ob, proj, x, tgt, vecs2, wa, wb, wo)
    loss = 0.5 * lsq[0, 0] / D

    hsum = jnp.asarray(np.kron(np.eye(NH), np.ones((HE, HE))), BF16)
    dx2, du, dya, dyb, d_gates, dattn, dsum, dza, dob, acc2 = _bwd2_call(
        x2, tgt, vecs2, ya, yb, u, proj, attn, wa, wb, wo, hsum)
    d_wo = _atb_call(y, du, "dw_out")
    d_wa = _atb_call(oa, dya, "dw_branch_a")
    d_wb = _atb_call(ob, dyb, "dw_branch_b")

    d_hgrn, acch = _hgrn_bwd_call(proj, oraw, dob, states, lb, gn)

    dq, dk, dv, dbs = _attn_bwd_call(proj, dattn, lse, dsum, biases)
    dqkvz = (dq, dk, dv, dza)

    d_win = _dwin_call(h, dqkvz, d_hgrn, d_gates)
    one_scale = 1.0 + scale
    grad_x, acc1 = _dh_call(dqkvz, d_hgrn, d_gates, w_in, x, dx2,
                            jnp.concatenate([a, norm_g, one_scale, jnp.zeros((5, D), F32)], 0))

    d_rel = jnp.zeros((NBUCKETS, NH), F32)
    for p, d in enumerate(PATTERNS):
        band, bucket = _band_bucket(d)
        onehot = (bucket[None] == np.arange(NBUCKETS)[:, None, None]) & band[None]
        d_rel = d_rel + jnp.einsum("hqk,bqk->bh", dbs[p], jnp.asarray(onehot, F32),
                                   precision=lax.Precision.HIGHEST)
    d_onorm = jnp.sum(acch[1].reshape(NH, HK), axis=0)

    zrow = jnp.zeros((D,), F32)
    pack = jnp.stack([acc1[0], acc1[1], acc2[0], acc1[2], acc2[1], acch[0],
                      zrow.at[:HK].set(d_onorm), zrow.at[0].set(loss),
                      zrow.at[:NBUCKETS * NH].set(d_rel.reshape(-1))] + [zrow] * 7, 0)
    return grad_x, d_win, d_wa, d_wb, d_wo, pack


def _me():
    return lax.axis_index("x"), lax.axis_index("y"), lax.axis_index("c")


def _peers(x, y):
    return [(1 - x, y), (x, 1 - y), (1 - x, 1 - y)]


def _allgather_small(blk, name):
    m_per, n = blk.shape

    def body(x_ref, out_ref, send_sems, recv_sems, local_sem):
        x, y, c = _me()
        me, sibling = (x, y, c), (x, y, 1 - c)
        chips = _peers(x, y)

        def rows(px, py, pc):
            return out_ref.at[pl.ds((4 * px + 2 * py + pc) * m_per, m_per), :]

        def copy(k, block, to, src=None):
            return pltpu.make_async_remote_copy(
                src_ref=rows(*block) if src is None else src, dst_ref=rows(*block),
                send_sem=send_sems.at[k], recv_sem=recv_sems.at[k], device_id=to, device_id_type=MESH)

        mine = pltpu.make_async_copy(x_ref, rows(*me), local_sem)
        mine.start()
        first = [copy(0, me, sibling, src=x_ref)]
        first += [copy(1 + j, me, (*chip, c), src=x_ref) for j, chip in enumerate(chips)]
        for cp in first:
            cp.start()
        passed = [copy(4 + j, (*chip, c), sibling) for j, chip in enumerate(chips)]
        for j, chip in enumerate(chips):
            copy(1 + j, (*chip, c), me).wait_recv()
            passed[j].start()
        copy(0, sibling, me).wait_recv()
        for j, chip in enumerate(chips):
            copy(4 + j, (*chip, 1 - c), me).wait_recv()
        for cp in first + passed:
            cp.wait_send()
        mine.wait()

    return pl.pallas_call(
        body, name=name, out_shape=jax.ShapeDtypeStruct((8 * m_per, n), blk.dtype),
        in_specs=[pl.BlockSpec(memory_space=pltpu.VMEM)], out_specs=pl.BlockSpec(memory_space=pltpu.VMEM),
        scratch_shapes=[pltpu.SemaphoreType.DMA((7,)), pltpu.SemaphoreType.DMA((7,)), pltpu.SemaphoreType.DMA],
    )(blk)


ANY = pl.BlockSpec(memory_space=pl.ANY)


def _gather_weights(win_s, wa_s, wb_s, wo_s):
    shapes = [(D, NPROJ), (AW, D), (D, D), (D, D)]

    def body(win_ref, wa_ref, wb_ref, wo_ref, owin, owa, owb, owo, send_sems, recv_sems, local_sems):
        x, y, c = _me()
        sibling = (x, y, 1 - c)
        chips = _peers(x, y)

        def shard(t, ref, j, half=None):
            if t == 0:
                r0, nr = (0, D) if half is None else (half * (D // 2), D // 2)
                return ref.at[pl.ds(r0, nr), pl.ds(j * 2048, 2048)]
            if t == 1:
                r0, nr = (0, AW) if half is None else (half * (AW // 2), AW // 2)
                return ref.at[pl.ds(r0, nr), pl.ds(j * 256, 256)]
            r0, nr = (0, 256) if half is None else (half * 128, 128)
            return ref.at[pl.ds(j * 256 + r0, nr), :]

        def src_half(t, ref, half):
            if t == 0:
                return ref.at[pl.ds(half * (D // 2), D // 2), :]
            if t == 1:
                return ref.at[pl.ds(half * (AW // 2), AW // 2), :]
            return ref.at[pl.ds(half * 128, 128), :]

        ins = [win_ref, wa_ref, wb_ref, wo_ref]
        outs = [owin, owa, owb, owo]
        me_chip = 2 * x + y
        local, sends, fwds = [], [], []
        for t in range(4):
            lc = pltpu.make_async_copy(ins[t], shard(t, outs[t], me_chip), local_sems.at[t])
            lc.start()
            local.append(lc)
            for k, chip in enumerate(chips):
                cp = pltpu.make_async_remote_copy(
                    src_ref=src_half(t, ins[t], c), dst_ref=shard(t, outs[t], me_chip, c),
                    send_sem=send_sems.at[t, k], recv_sem=recv_sems.at[t, k],
                    device_id=(*chip, c), device_id_type=MESH)
                cp.start()
                sends.append(cp)
        for t in range(4):
            for k, chip in enumerate(chips):
                pj = 2 * chip[0] + chip[1]
                blk = shard(t, outs[t], pj, c)
                pltpu.make_async_remote_copy(
                    src_ref=blk, dst_ref=blk, send_sem=send_sems.at[t, k], recv_sem=recv_sems.at[t, k],
                    device_id=(*chip, c), device_id_type=MESH).wait_recv()
                fw = pltpu.make_async_remote_copy(
                    src_ref=blk, dst_ref=blk, send_sem=send_sems.at[t, 3 + k], recv_sem=recv_sems.at[t, 3 + k],
                    device_id=sibling, device_id_type=MESH)
                fw.start()
                fwds.append(fw)
        for t in range(4):
            for k, chip in enumerate(chips):
                pj = 2 * chip[0] + chip[1]
                blk = shard(t, outs[t], pj, 1 - c)
                pltpu.make_async_remote_copy(
                    src_ref=blk, dst_ref=blk, send_sem=send_sems.at[t, 3 + k], recv_sem=recv_sems.at[t, 3 + k],
                    device_id=sibling, device_id_type=MESH).wait_recv()
        for cp in sends + fwds:
            cp.wait_send()
        for lc in local:
            lc.wait()

    return pl.pallas_call(
        body, name="gather_weights",
        out_shape=[jax.ShapeDtypeStruct(s, BF16) for s in shapes],
        in_specs=[ANY] * 4, out_specs=[ANY] * 4,
        scratch_shapes=[pltpu.SemaphoreType.DMA((4, 6)), pltpu.SemaphoreType.DMA((4, 6)),
                        pltpu.SemaphoreType.DMA((4,))],
    )(win_s, wa_s, wb_s, wo_s)


def _half_of(t, ref, half):
    if t == 0:
        return ref.at[:, pl.ds(half * 512, 512), :]
    if t == 1:
        return ref.at[pl.ds(half * 256, 256), :]
    return ref.at[:, pl.ds(half * 512, 512)]


HALF_SHAPES = [(4, 512, 2048), (256, D), (D, 512), (D, 512)]
PIECE_SHAPES = [(512, 2048), (256, 256), (256, 512), (256, 512)]
SHARD_SHAPES = [(D, 2048), (AW, 256), (256, D), (256, D)]


def _chip_piece(t, ref, j):
    if t == 0:
        return ref.at[j]
    if t == 1:
        return ref.at[:, pl.ds(j * 256, 256)]
    return ref.at[pl.ds(j * 256, 256), :]


def _reduce_sibling_send(gs):
    def body(g0, g1, g2, g3, r0, r1, r2, r3, send_sems, recv_sems):
        x, y, c = _me()
        ins, outs = [g0, g1, g2, g3], [r0, r1, r2, r3]
        cps = []
        for t in range(4):
            cp = pltpu.make_async_remote_copy(
                src_ref=_half_of(t, ins[t], 1 - c), dst_ref=outs[t],
                send_sem=send_sems.at[t], recv_sem=recv_sems.at[t], device_id=(x, y, 1 - c), device_id_type=MESH)
            cp.start()
            cps.append(cp)
        for cp in cps:
            cp.wait_recv()
        for cp in cps:
            cp.wait_send()

    return pl.pallas_call(
        body, name="reduce_sibling", out_shape=[jax.ShapeDtypeStruct(s, BF16) for s in HALF_SHAPES],
        in_specs=[ANY] * 4, out_specs=[ANY] * 4,
        scratch_shapes=[pltpu.SemaphoreType.DMA((4,)), pltpu.SemaphoreType.DMA((4,))],
    )(*gs)


def _reduce_chips_send(hs):
    def body(h0, h1, h2, h3, r0, r1, r2, r3, send_sems, recv_sems):
        x, y, c = _me()
        ins, outs = [h0, h1, h2, h3], [r0, r1, r2, r3]
        cps = []
        for t in range(4):
            for k, chip in enumerate(_peers(x, y)):
                pj = 2 * chip[0] + chip[1]
                cp = pltpu.make_async_remote_copy(
                    src_ref=_chip_piece(t, ins[t], pj), dst_ref=outs[t].at[k],
                    send_sem=send_sems.at[t, k], recv_sem=recv_sems.at[t, k],
                    device_id=(*chip, c), device_id_type=MESH)
                cp.start()
                cps.append(cp)
        for cp in cps:
            cp.wait_recv()
        for cp in cps:
            cp.wait_send()

    return pl.pallas_call(
        body, name="reduce_chips", out_shape=[jax.ShapeDtypeStruct((3,) + s, BF16) for s in PIECE_SHAPES],
        in_specs=[ANY] * 4, out_specs=[ANY] * 4,
        scratch_shapes=[pltpu.SemaphoreType.DMA((4, 3)), pltpu.SemaphoreType.DMA((4, 3))],
    )(*hs)


def _share_sibling(ps):
    def body(p0, p1, p2, p3, o0, o1, o2, o3, send_sems, recv_sems, local_sems):
        x, y, c = _me()
        ins, outs = [p0, p1, p2, p3], [o0, o1, o2, o3]

        def half(t, ref, hf):
            if t == 0:
                return ref.at[pl.ds(hf * 512, 512), :]
            if t == 1:
                return ref.at[pl.ds(hf * 256, 256), :]
            return ref.at[:, pl.ds(hf * 512, 512)]

        cps, lcs = [], []
        for t in range(4):
            lc = pltpu.make_async_copy(ins[t], half(t, outs[t], c), local_sems.at[t])
            lc.start()
            lcs.append(lc)
            cp = pltpu.make_async_remote_copy(
                src_ref=ins[t], dst_ref=half(t, outs[t], c), send_sem=send_sems.at[t], recv_sem=recv_sems.at[t],
                device_id=(x, y, 1 - c), device_id_type=MESH)
            cp.start()
            cps.append(cp)
        for t in range(4):
            pltpu.make_async_remote_copy(
                src_ref=ins[t], dst_ref=half(t, outs[t], 1 - c), send_sem=send_sems.at[t],
                recv_sem=recv_sems.at[t], device_id=(x, y, 1 - c), device_id_type=MESH).wait_recv()
        for cp in cps:
            cp.wait_send()
        for lc in lcs:
            lc.wait()

    return pl.pallas_call(
        body, name="share_sibling", out_shape=[jax.ShapeDtypeStruct(s, F32) for s in SHARD_SHAPES],
        in_specs=[ANY] * 4, out_specs=[ANY] * 4,
        scratch_shapes=[pltpu.SemaphoreType.DMA((4,)), pltpu.SemaphoreType.DMA((4,)),
                        pltpu.SemaphoreType.DMA((4,))],
    )(*ps)


def _cast_call(a, name):
    lead = a.shape[0]
    rest = a.shape[1:]
    per = int(np.prod(rest)) * 4
    tl = lead
    while tl * per > (4 << 20) and tl % 2 == 0:
        tl //= 2
    if len(rest) == 1 and tl % 16 != 0:
        tl = lead
    zeros = (0,) * len(rest)

    def body(a_ref, o_ref):
        o_ref[...] = a_ref[...].astype(BF16)

    blk = pl.BlockSpec((tl,) + rest, lambda i: (i,) + zeros)
    return pl.pallas_call(
        body, name=name, grid=(lead // tl,), in_specs=[blk], out_specs=blk,
        out_shape=jax.ShapeDtypeStruct(a.shape, BF16), compiler_params=_cp(),
    )(a)


def _half_blockspec(t, idx_pos):
    if t == 0:
        return pl.BlockSpec((1, 512, 2048), lambda i, s: (i, s[idx_pos], 0)), 4
    if t == 1:
        return pl.BlockSpec((256, D), lambda i, s: (s[idx_pos], 0)), 1
    return pl.BlockSpec((256, 512), lambda i, s: (i, s[idx_pos])), 4


def _half_out_blockspec(t):
    if t == 0:
        return pl.BlockSpec((1, 512, 2048), lambda i, s: (i, 0, 0))
    if t == 1:
        return pl.BlockSpec((256, D), lambda i, s: (0, 0))
    return pl.BlockSpec((256, 512), lambda i, s: (i, 0))


def _add_half_call(t, own, recv, sc, name):
    in_blk, steps = _half_blockspec(t, 0)
    out_blk = _half_out_blockspec(t)

    def body(s_ref, a_ref, b_ref, o_ref, ob_ref):
        v = a_ref[...] + b_ref[...].astype(F32)
        o_ref[...] = v
        ob_ref[...] = v.astype(BF16)

    return pl.pallas_call(
        body, name=name,
        grid_spec=pltpu.PrefetchScalarGridSpec(
            num_scalar_prefetch=1, grid=(steps,), in_specs=[in_blk, out_blk], out_specs=[out_blk, out_blk]),
        out_shape=[jax.ShapeDtypeStruct(HALF_SHAPES[t], F32), jax.ShapeDtypeStruct(HALF_SHAPES[t], BF16)],
        compiler_params=_cp(),
    )(sc, own, recv)


def _final_piece_call(t, chipsum, recv3, sc, name):
    ps = PIECE_SHAPES[t]
    if t == 0:
        own_blk = pl.BlockSpec((1,) + ps, lambda i, s: (s[1], 0, 0))
    elif t == 1:
        own_blk = pl.BlockSpec(ps, lambda i, s: (0, s[1]))
    else:
        own_blk = pl.BlockSpec(ps, lambda i, s: (s[1], 0))
    r_blk = pl.BlockSpec((3,) + ps, lambda i, s: (0, 0, 0))
    o_blk = pl.BlockSpec(ps, lambda i, s: (0, 0))

    def body(s_ref, a_ref, r_ref, o_ref):
        a = a_ref[0] if t == 0 else a_ref[...]
        o_ref[...] = ((a + r_ref[0].astype(F32)) + r_ref[1].astype(F32)) + r_ref[2].astype(F32)

    return pl.pallas_call(
        body, name=name,
        grid_spec=pltpu.PrefetchScalarGridSpec(
            num_scalar_prefetch=1, grid=(1,), in_specs=[own_blk, r_blk], out_specs=o_blk),
        out_shape=jax.ShapeDtypeStruct(ps, F32), compiler_params=_cp(),
    )(sc, chipsum, recv3)


def _lower_bound_fn(hgrn_lb):
    return jnp.cumsum(jax.nn.softmax(hgrn_lb.astype(F32), axis=0), axis=0)[0]


def kernel(x, c, w_ada, b_ada, norm_g, w_in, hgrn_onorm_g, w_branch_a, w_branch_b, w_out, rel_bias, hgrn_lb, final_g, loss_target, m_w_ada, m_b_ada, m_norm_g, m_w_in, m_hgrn_onorm_g, m_w_branch_a, m_w_branch_b, m_w_out, m_rel_bias, m_hgrn_lb, m_final_g, v_w_ada, v_b_ada, v_norm_g, v_w_in, v_hgrn_onorm_g, v_w_branch_a, v_w_branch_b, v_w_out, v_rel_bias, v_hgrn_lb, v_final_g):
    ax, ay, ac = _me()
    chip = 2 * ax + ay
    dev = 4 * ax + 2 * ay + ac
    sc_idx = jnp.stack([ac, chip]).astype(jnp.int32)

    c_all = _allgather_small(jnp.pad(c, ((0, 7), (0, 0))), "gather_c").reshape(8, 8, D)[:, 0]
    b_s = lax.dynamic_slice(b_ada, (0, chip * 768), (1, 768))
    mod_part = _mod_call(c_all, w_ada[0], b_s)
    mod_all = _allgather_small(mod_part, "gather_mod").reshape(8, 8, 768)
    mod_mine = lax.dynamic_index_in_dim(mod_all, dev, axis=1, keepdims=False)
    mod = mod_mine[0::2].reshape(1, 3 * D)
    shift, scale, gate = mod[:, :D], mod[:, D:2 * D], mod[:, 2 * D:]

    win_f, wa_f, wb_f, wo_f = _gather_weights(
        _cast_call(w_in[0], "cast_w_in"), _cast_call(w_branch_a[0], "cast_w_a"),
        _cast_call(w_branch_b[0], "cast_w_b"), _cast_call(w_out[0], "cast_w_o"))

    lb, lb_vjp = jax.vjp(_lower_bound_fn, hgrn_lb)
    grad_x, d_win, d_wa, d_wb, d_wo, pack = _local_step(
        x[0], loss_target[0], shift, scale, gate, norm_g, hgrn_onorm_g, rel_bias, lb[None, :],
        final_g[None, :], win_f, wa_f, wb_f, wo_f)

    own = [d_win, d_wa, d_wb, d_wo]
    names = ["w_in", "w_a", "w_b", "w_o"]
    sib = _reduce_sibling_send([_cast_call(g, "cast_g_" + n) for g, n in zip(own, names)])
    halves = [_add_half_call(t, own[t], sib[t], sc_idx, "chipsum_" + names[t]) for t in range(4)]
    rec = _reduce_chips_send([hb for _, hb in halves])
    pieces = [_final_piece_call(t, halves[t][0], rec[t], sc_idx, "piece_" + names[t]) for t in range(4)]
    g_win, g_wa, g_wb, g_wo = _share_sibling(pieces)

    packs = _allgather_small(pack, "gather_small").reshape(8, 16, D)
    tot = _sum8_call(packs)
    loss = tot[7, 0]
    g_b_ada = tot[0:3].reshape(1, 3 * D)
    g_norm_g = tot[3:4]
    g_final_g = tot[4]
    (g_hgrn_lb,) = lb_vjp(tot[5])
    g_onorm = tot[6:7, :HK]
    g_rel = tot[8, :NBUCKETS * NH].reshape(NBUCKETS, NH)

    def rows_of(a):
        flat = a.reshape(-1)
        n = -(-flat.shape[0] // D)
        return jnp.pad(flat, (0, n * D - flat.shape[0])).reshape(n, D)

    smalls = [(b_ada, g_b_ada, m_b_ada, v_b_ada), (norm_g, g_norm_g, m_norm_g, v_norm_g),
              (hgrn_onorm_g, g_onorm, m_hgrn_onorm_g, v_hgrn_onorm_g), (rel_bias, g_rel, m_rel_bias, v_rel_bias),
              (hgrn_lb, g_hgrn_lb, m_hgrn_lb, v_hgrn_lb), (final_g, g_final_g, m_final_g, v_final_g)]
    cat = [jnp.concatenate([rows_of(s[k]) for s in smalls], 0) for k in range(4)]
    cat = [jnp.pad(a, ((0, 16 - a.shape[0]), (0, 0))) for a in cat]
    sd, sm, sv = _adamw_call(*cat, "adamw_small")

    def unpack(packed):
        res, r = [], 0
        for s in smalls:
            n = -(-s[0].size // D)
            res.append(packed[r:r + n].reshape(-1)[:s[0].size].reshape(s[0].shape))
            r += n
        return res

    d_small, m_small, v_small = unpack(sd), unpack(sm), unpack(sv)

    sc_all = c_all * jax.nn.sigmoid(c_all)
    dmod_all = packs[:, 0:3].reshape(8, 3 * D)
    dm_s = lax.dynamic_slice(dmod_all, (0, chip * 768), (8, 768))
    g_w_ada, d_w_ada, nm_w_ada, nv_w_ada = _ada_update_call(sc_all.T, dm_s, w_ada[0], m_w_ada[0], v_w_ada[0])

    big = []
    for w, g, m, v, n in [(w_in, g_win, m_w_in, v_w_in, "w_in"), (w_branch_a, g_wa, m_w_branch_a, v_w_branch_a, "w_a"),
                          (w_branch_b, g_wb, m_w_branch_b, v_w_branch_b, "w_b"), (w_out, g_wo, m_w_out, v_w_out, "w_o")]:
        big.append(_adamw_call(w[0], g, m[0], v[0], "adamw_" + n))

    e = lambda a: a[None]
    grads = [e(g_w_ada), g_b_ada, g_norm_g, e(g_win), g_onorm, e(g_wa), e(g_wb), e(g_wo), g_rel, g_hgrn_lb, g_final_g]
    deltas = [e(d_w_ada), d_small[0], d_small[1], e(big[0][0]), d_small[2], e(big[1][0]), e(big[2][0]), e(big[3][0]),
              d_small[3], d_small[4], d_small[5]]
    new_m = [e(nm_w_ada), m_small[0], m_small[1], e(big[0][1]), m_small[2], e(big[1][1]), e(big[2][1]), e(big[3][1]),
             m_small[3], m_small[4], m_small[5]]
    new_v = [e(nv_w_ada), v_small[0], v_small[1], e(big[0][2]), v_small[2], e(big[1][2]), e(big[2][2]), e(big[3][2]),
             v_small[3], v_small[4], v_small[5]]
    return (loss, grad_x[None], *grads, *deltas, *new_m, *new_v)
```

```python
import functools
import math

import numpy as np
import jax
import jax.numpy as jnp
from jax import lax
from jax.experimental import pallas as pl
from jax.experimental.pallas import tpu as pltpu

D = 1024
AW = 512
NH = 8
HE = 64
HK = 128
NPROJ = 8192
ABLK = 128
PATTERNS = (1, 4, 16)
NBUCKETS = 32
MAXDIST = 2048
NEG = -1e30
EPS = 1e-6
CH = 64
LR, B1, B2, AEPS, WD, STEP = 0.001, 0.9, 0.999, 1e-08, 0.01, 10

F32 = jnp.float32
BF16 = jnp.bfloat16
MESH = pl.DeviceIdType.MESH
VMEM_LIMIT = 56 * 1024 * 1024


def _cp(**kw):
    return pltpu.CompilerParams(vmem_limit_bytes=VMEM_LIMIT, **kw)


def _sig(x):
    return 0.5 * jnp.tanh(0.5 * x) + 0.5


def _nt(a, b):
    return lax.dot_general(a, b, (((1,), (1,)), ((), ())), preferred_element_type=F32)


def _tn(a, b):
    return lax.dot_general(a, b, (((0,), (0,)), ((), ())), preferred_element_type=F32)


def _nn(a, b):
    return jnp.dot(a, b, preferred_element_type=F32)


def _split3(x):
    h = x.astype(BF16)
    r = x - h.astype(F32)
    m = r.astype(BF16)
    l = (r - m.astype(F32)).astype(BF16)
    return h, m, l


def _exact_mm(tri_bf16, x):
    h, m, l = _split3(x)
    return _nn(tri_bf16, h) + _nn(tri_bf16, m) + _nn(tri_bf16, l)


def _exact_mm_r(x, ones_bf16):
    h, m, l = _split3(x)
    return _nn(h, ones_bf16) + _nn(m, ones_bf16) + _nn(l, ones_bf16)


def _h_call(x, avec):
    S = x.shape[0]
    tm = 512

    def body(x_ref, a_ref, h_ref):
        xv = x_ref[...]
        r = lax.rsqrt(jnp.mean(xv * xv, axis=-1, keepdims=True) + EPS)
        h_ref[...] = (xv * r * a_ref[0:1, :] + a_ref[1:2, :]).astype(BF16)

    return pl.pallas_call(
        body, name="h_norm", grid=(S // tm,),
        in_specs=[pl.BlockSpec((tm, D), lambda i: (i, 0)), pl.BlockSpec((8, D), lambda i: (0, 0))],
        out_specs=pl.BlockSpec((tm, D), lambda i: (i, 0)),
        out_shape=jax.ShapeDtypeStruct((S, D), BF16), compiler_params=_cp(),
    )(x, avec)


def _proj_call(h, w_in):
    S = h.shape[0]
    tm, tn = 512, 2048

    def body(h_ref, w_ref, o_ref):
        o_ref[...] = _nn(h_ref[...], w_ref[...])

    return pl.pallas_call(
        body, name="in_proj", grid=(NPROJ // tn, S // tm),
        in_specs=[pl.BlockSpec((tm, D), lambda j, i: (i, 0)), pl.BlockSpec((D, tn), lambda j, i: (0, j))],
        out_specs=pl.BlockSpec((tm, tn), lambda j, i: (i, j)),
        out_shape=jax.ShapeDtypeStruct((S, NPROJ), F32), compiler_params=_cp(),
    )(h, w_in)


def _t5_bucket_np(dist):
    max_exact = NBUCKETS // 2
    n = dist.astype(np.float32)
    large = max_exact + (np.log(np.maximum(n, np.float32(1.0)) / np.float32(max_exact))
                         / np.float32(math.log(MAXDIST / max_exact))
                         * np.float32(NBUCKETS - max_exact)).astype(np.int32)
    large = np.minimum(large, NBUCKETS - 1)
    return np.where(dist < max_exact, dist, large)


def _band_bucket(d):
    qi = np.arange(ABLK)[:, None]
    kj = np.arange(2 * ABLK)[None, :]
    delta = qi + ABLK - kj
    band = (delta >= 0) & (delta <= ABLK)
    bucket = _t5_bucket_np(np.clip(delta, 0, None) * d)
    return band, bucket


def _bias_tiles(rel_bias):
    tiles = []
    for d in PATTERNS:
        band, bucket = _band_bucket(d)
        onehot = (jnp.asarray(bucket, jnp.int32)[None] == jnp.arange(NBUCKETS, dtype=jnp.int32)[:, None, None])
        bias = jnp.einsum("bqk,bh->hqk", onehot.astype(F32), rel_bias, precision=lax.Precision.HIGHEST)
        tiles.append(jnp.where(jnp.asarray(band)[None], bias, NEG))
    return jnp.stack(tiles, 0)


ATT = 2048
HP = 2 * HE
AGRP = 4
AGRP_B = 2


def _attn_blocks():
    out = []
    for p, d in enumerate(PATTERNS):
        for r in range(d):
            for n in range(ATT // (d * ABLK)):
                out.append((p, d, r, n))
    return out


def _attn_fwd_call(proj, biases):
    S = proj.shape[0]
    nt = S // ATT

    def body(q_ref, k_ref, v_ref, z_ref, b_ref, a_ref, l_ref, oa_ref, kc, vc, op, lp):
        i = pl.program_id(1)

        @pl.when(i == 0)
        def _():
            kc[0:ATT] = jnp.zeros((ATT, HP), F32)
            vc[0:ATT] = jnp.zeros((ATT, HP), F32)

        @pl.when(i > 0)
        def _():
            kc[0:ATT] = kc[ATT:2 * ATT]
            vc[0:ATT] = vc[ATT:2 * ATT]

        kc[ATT:2 * ATT] = k_ref[...]
        vc[ATT:2 * ATT] = v_ref[...]
        col = lax.broadcasted_iota(jnp.int32, (ABLK, 2 * ABLK), 1)
        dead = jnp.logical_and(i == 0, col < ABLK)
        blocks = _attn_blocks()
        hs = (slice(0, HE), slice(HE, 2 * HE))
        for g0 in range(0, len(blocks), AGRP):
            grp = blocks[g0:g0 + AGRP]
            qrows = [pl.ds(n * ABLK * d + r, ABLK, stride=d) for p, d, r, n in grp]
            krows = [pl.ds(ATT + (n - 1) * ABLK * d + r, 2 * ABLK, stride=d) for p, d, r, n in grp]
            qs = [(q_ref[qr, :] * (HE ** -0.5)).astype(BF16) for qr in qrows]
            ks = [kc[kr, :].astype(BF16) for kr in krows]
            vs = [vc[kr, :].astype(BF16) for kr in krows]
            ss = [[_nt(qs[b][:, sl], ks[b][:, sl]) + b_ref[grp[b][0], e] for e, sl in enumerate(hs)]
                  for b in range(len(grp))]
            ss = [[jnp.where(dead, NEG, s) if grp[b][3] == 0 else s for s in ss[b]] for b in range(len(grp))]
            mxs = [[jnp.max(s, axis=-1, keepdims=True) for s in sb] for sb in ss]
            pes = [[jnp.exp(s - mx) for s, mx in zip(sb, mb)] for sb, mb in zip(ss, mxs)]
            dens = [[jnp.sum(pe, axis=-1, keepdims=True) for pe in pb] for pb in pes]
            pvs = [[_nn(pe.astype(BF16), vs[b][:, sl]) for pe, sl in zip(pes[b], hs)] for b in range(len(grp))]
            for b in range(len(grp)):
                p, d, r, n = grp[b]
                prow = pl.ds(p * ATT + n * ABLK * d + r, ABLK, stride=d)
                lp[prow, :] = jnp.concatenate(
                    [jnp.broadcast_to(mx + jnp.log(dn), (ABLK, HE)) for mx, dn in zip(mxs[b], dens[b])], axis=1)
                op[prow, :] = jnp.concatenate([pv / dn for pv, dn in zip(pvs[b], dens[b])], axis=1)
        rt = 256
        for t in range(ATT // rt):
            rows = slice(t * rt, (t + 1) * rt)
            pr = [slice(p * ATT + t * rt, p * ATT + (t + 1) * rt) for p in range(3)]
            la, lb_, lc = lp[pr[0], :], lp[pr[1], :], lp[pr[2], :]
            m = jnp.maximum(jnp.maximum(la, lb_), lc)
            ea, eb, ec = jnp.exp(la - m), jnp.exp(lb_ - m), jnp.exp(lc - m)
            den = ea + eb + ec
            att = (ea * op[pr[0], :] + eb * op[pr[1], :] + ec * op[pr[2], :]) / den
            a_ref[rows, :] = att
            l_ref[rows, :] = m + jnp.log(den)
            z = z_ref[rows, :]
            oa_ref[rows, :] = (att * (z * _sig(z))).astype(BF16)

    def pcol(c):
        return pl.BlockSpec((ATT, HP), lambda h, i: (i, c * 4 + h))

    out = pl.BlockSpec((ATT, HP), lambda h, i: (i, h))
    return pl.pallas_call(
        body, name="attn_fwd", grid=(4, nt),
        in_specs=[pcol(0), pcol(1), pcol(2), pcol(3),
                  pl.BlockSpec((3, 2, ABLK, 2 * ABLK), lambda h, i: (0, h, 0, 0))],
        out_specs=[out, out, out],
        out_shape=[jax.ShapeDtypeStruct((S, AW), F32), jax.ShapeDtypeStruct((S, AW), F32),
                   jax.ShapeDtypeStruct((S, AW), BF16)],
        scratch_shapes=[pltpu.VMEM((2 * ATT, HP), F32), pltpu.VMEM((2 * ATT, HP), F32),
                        pltpu.VMEM((3 * ATT, HP), F32), pltpu.VMEM((3 * ATT, HP), F32)],
        compiler_params=_cp(),
    )(proj, proj, proj, proj, biases)


def _attn_bwd_call(proj, dattn, lse, dsum, biases):
    S = proj.shape[0]
    nt = S // ATT

    def body(q_ref, k_ref, v_ref, do_ref, l_ref, ds_ref, b_ref, dq_ref, dk_ref, dv_ref, db_ref,
             kc, vc, dkc, dvc, dqa):
        i = pl.program_id(1)

        @pl.when(i == 0)
        def _():
            kc[ATT:2 * ATT] = jnp.zeros((ATT, HP), F32)
            vc[ATT:2 * ATT] = jnp.zeros((ATT, HP), F32)
            dkc[ATT:2 * ATT] = jnp.zeros((ATT, HP), F32)
            dvc[ATT:2 * ATT] = jnp.zeros((ATT, HP), F32)
            db_ref[...] = jnp.zeros_like(db_ref)

        @pl.when(i < nt)
        def _():
            kc[0:ATT] = kc[ATT:2 * ATT]
            vc[0:ATT] = vc[ATT:2 * ATT]
            dkc[0:ATT] = dkc[ATT:2 * ATT]
            dvc[0:ATT] = dvc[ATT:2 * ATT]
            kc[ATT:2 * ATT] = k_ref[...]
            vc[ATT:2 * ATT] = v_ref[...]
            dkc[ATT:2 * ATT] = jnp.zeros((ATT, HP), F32)
            dvc[ATT:2 * ATT] = jnp.zeros((ATT, HP), F32)
            col = lax.broadcasted_iota(jnp.int32, (ABLK, 2 * ABLK), 1)
            dead = jnp.logical_and(i == 0, col < ABLK)
            blocks = _attn_blocks()
            hs = (slice(0, HE), slice(HE, 2 * HE))
            for g0 in range(0, len(blocks), AGRP_B):
                grp = blocks[g0:g0 + AGRP_B]
                nb_ = range(len(grp))
                qrows = [pl.ds(n * ABLK * d + r, ABLK, stride=d) for p, d, r, n in grp]
                krows = [pl.ds(ATT + (n - 1) * ABLK * d + r, 2 * ABLK, stride=d) for p, d, r, n in grp]
                qs = [(q_ref[qr, :] * (HE ** -0.5)).astype(BF16) for qr in qrows]
                ks = [kc[kr, :].astype(BF16) for kr in krows]
                vs = [vc[kr, :].astype(BF16) for kr in krows]
                dos = [do_ref[qr, :].astype(BF16) for qr in qrows]
                lvs = [l_ref[qr, :] for qr in qrows]
                dsvs = [ds_ref[qr, :] for qr in qrows]
                ss = [[_nt(qs[b][:, sl], ks[b][:, sl]) + b_ref[grp[b][0], e] for e, sl in enumerate(hs)] for b in nb_]
                ss = [[jnp.where(dead, NEG, s) if grp[b][3] == 0 else s for s in ss[b]] for b in nb_]
                dps = [[_nt(dos[b][:, sl], vs[b][:, sl]) for sl in hs] for b in nb_]
                pes = [[jnp.exp(ss[b][e] - lvs[b][:, e * HE:e * HE + 1]) for e in range(2)] for b in nb_]
                dscs = [[pes[b][e] * (dps[b][e] - dsvs[b][:, e * HE:e * HE + 1]) for e in range(2)] for b in nb_]
                for b in nb_:
                    for e in range(2):
                        db_ref[grp[b][0], e] += dscs[b][e]
                dsbs = [[t.astype(BF16) for t in tb] for tb in dscs]
                dqs = [[_nn(dsbs[b][e], ks[b][:, sl]) * (HE ** -0.5) for e, sl in enumerate(hs)] for b in nb_]
                dks = [[_tn(dsbs[b][e], qs[b][:, sl]) for e, sl in enumerate(hs)] for b in nb_]
                dvs = [[_tn(pes[b][e].astype(BF16), dos[b][:, sl]) for e, sl in enumerate(hs)] for b in nb_]
                for b in nb_:
                    dq = jnp.concatenate(dqs[b], axis=1)
                    if grp[b][0] == 0:
                        dqa[qrows[b], :] = dq
                    else:
                        dqa[qrows[b], :] += dq
                    dkc[krows[b], :] += jnp.concatenate(dks[b], axis=1)
                    dvc[krows[b], :] += jnp.concatenate(dvs[b], axis=1)
            dq_ref[...] = dqa[...].astype(BF16)
            dk_ref[...] = dkc[0:ATT].astype(BF16)
            dv_ref[...] = dvc[0:ATT].astype(BF16)

        @pl.when(i == nt)
        def _():
            dk_ref[...] = dkc[ATT:2 * ATT].astype(BF16)
            dv_ref[...] = dvc[ATT:2 * ATT].astype(BF16)

    def pcol(c):
        return pl.BlockSpec((ATT, HP), lambda h, i: (jnp.minimum(i, nt - 1), c * 4 + h))

    qrow = pl.BlockSpec((ATT, HP), lambda h, i: (jnp.minimum(i, nt - 1), h))
    krow = pl.BlockSpec((ATT, HP), lambda h, i: (jnp.maximum(i - 1, 0), h))
    bspec = pl.BlockSpec((3, 2, ABLK, 2 * ABLK), lambda h, i: (0, h, 0, 0))
    return pl.pallas_call(
        body, name="attn_bwd", grid=(4, nt + 1),
        in_specs=[pcol(0), pcol(1), pcol(2), qrow, qrow, qrow, bspec],
        out_specs=[qrow, krow, krow, bspec],
        out_shape=[jax.ShapeDtypeStruct((S, AW), BF16)] * 3
                  + [jax.ShapeDtypeStruct((3, NH, ABLK, 2 * ABLK), F32)],
        scratch_shapes=[pltpu.VMEM((2 * ATT, HP), F32)] * 4 + [pltpu.VMEM((ATT, HP), F32)],
        compiler_params=_cp(),
    )(proj, proj, proj, dattn, lse, dsum, biases)


HRB = 256


def _hgrn_gates(q_ref, f_ref, rows, lbv, tri):
    qraw = q_ref[rows, :]
    sq = _sig(qraw)
    q = qraw * sq
    sf = _sig(f_ref[rows, :])
    f = lbv + (1.0 - lbv) * sf
    k = 1.0 - f
    b = _exact_mm(tri, jnp.log(f))
    bl = b[CH - 1:CH, :]
    bm = b[CH // 2 - 1:CH // 2, :]
    e1 = jnp.exp(b - bm)
    e2 = jnp.exp(bm - b)
    ebm = jnp.exp(bm)
    eblm = jnp.exp(bl - bm)
    qs = q * e1
    ks = k * e2
    qe = qs * ebm
    kd = ks * eblm
    return dict(qraw=qraw, sq=sq, sf=sf, f=f, bl=bl, e1=e1, e2=e2, ebm=ebm, eblm=eblm, qe=qe, qs=qs, ks=ks, kd=kd)


def _tri_masks():
    row = lax.broadcasted_iota(jnp.int32, (CH, CH), 0)
    col = lax.broadcasted_iota(jnp.int32, (CH, CH), 1)
    return row >= col


def _hgrn_fwd_call(proj, lb, gn):
    S = proj.shape[0]
    nc = S // CH
    cps = HRB // CH

    def body(q_ref, f_ref, i_ref, z_ref, lb_ref, gn_ref, or_ref, ob_ref, st_ref, st):
        @pl.when(pl.program_id(0) == 0)
        def _():
            st[...] = jnp.zeros_like(st)

        low = _tri_masks()
        tri = low.astype(BF16)
        lbv = lb_ref[...]
        for ci in range(cps):
            rows = slice(ci * CH, (ci + 1) * CH)
            g = _hgrn_gates(q_ref, f_ref, rows, lbv, tri)
            v = i_ref[rows, :]
            ebl = jnp.exp(g["bl"])
            st_ref[ci] = st[...]
            hs = [slice(hh * HK, (hh + 1) * HK) for hh in range(NH)]
            vb = v.astype(BF16)
            qsb, ksb, qeb, kdb = (g[n_].astype(BF16) for n_ in ("qs", "ks", "qe", "kd"))
            s0s = [st[sl, :] for sl in hs]
            as_ = [_nt(qsb[:, sl], ksb[:, sl]) for sl in hs]
            ois = [_nt(qeb[:, sl], s0.astype(BF16)) for sl, s0 in zip(hs, s0s)]
            sts = [_tn(vb[:, sl], kdb[:, sl]) for sl in hs]
            abs_ = [jnp.where(low, a, 0.0).astype(BF16) for a in as_]
            os_ = [oi + _nn(a, vb[:, sl]) for oi, a, sl in zip(ois, abs_, hs)]
            for sl, s0, sn in zip(hs, s0s, sts):
                st[sl, :] = s0 * ebl[:, sl] + sn
            o = jnp.concatenate(os_, axis=1)
            or_ref[rows, :] = o
            rs = [lax.rsqrt(jnp.mean(oh * oh, axis=-1, keepdims=True) + EPS) for oh in os_]
            on = jnp.concatenate([oh * r for oh, r in zip(os_, rs)], axis=1)
            z = z_ref[rows, :]
            ob_ref[rows, :] = (on * gn_ref[...] * (z * _sig(z))).astype(BF16)

    def pcol(c):
        return pl.BlockSpec((HRB, D), lambda i: (i, c))

    vec = pl.BlockSpec((1, D), lambda i: (0, 0))
    row = pl.BlockSpec((HRB, D), lambda i: (i, 0))
    return pl.pallas_call(
        body, name="hgrn_fwd", grid=(S // HRB,),
        in_specs=[pcol(2), pcol(3), pcol(4), pcol(5), vec, vec],
        out_specs=[row, row, pl.BlockSpec((cps, NH * HK, HK), lambda i: (i, 0, 0))],
        out_shape=[jax.ShapeDtypeStruct((S, D), F32), jax.ShapeDtypeStruct((S, D), BF16),
                   jax.ShapeDtypeStruct((nc, NH * HK, HK), F32)],
        scratch_shapes=[pltpu.VMEM((NH * HK, HK), F32)],
        compiler_params=_cp(),
    )(proj, proj, proj, proj, lb, gn)


def _hgrn_bwd_call(proj, oraw, dob, states, lb, gn):
    S = proj.shape[0]
    nblk = S // HRB
    cps = HRB // CH

    def body(q_ref, f_ref, i_ref, z_ref, or_ref, dob_ref, st_ref, lb_ref, gn_ref, dh_ref, acc_ref, dst):
        @pl.when(pl.program_id(0) == 0)
        def _():
            dst[...] = jnp.zeros_like(dst)
            acc_ref[...] = jnp.zeros_like(acc_ref)

        low = _tri_masks()
        tri = low.astype(BF16)
        triu = jnp.logical_not(_tri_masks()) | (lax.broadcasted_iota(jnp.int32, (CH, CH), 0)
                                               == lax.broadcasted_iota(jnp.int32, (CH, CH), 1))
        triu = triu.astype(BF16)
        lbv = lb_ref[...]
        for ci in reversed(range(cps)):
            rows = slice(ci * CH, (ci + 1) * CH)
            g = _hgrn_gates(q_ref, f_ref, rows, lbv, tri)
            v = i_ref[rows, :]
            ebl = jnp.exp(g["bl"])
            hs = [slice(hh * HK, (hh + 1) * HK) for hh in range(NH)]
            cat = lambda lst: jnp.concatenate(lst, axis=1)
            o = or_ref[rows, :]
            z = z_ref[rows, :]
            sz = _sig(z)
            gnv = gn_ref[...]
            dobv = dob_ref[rows, :]
            r = cat([jnp.broadcast_to(lax.rsqrt(jnp.mean(o[:, sl] * o[:, sl], axis=-1, keepdims=True) + EPS),
                                      (CH, HK)) for sl in hs])
            don = dobv * (z * sz)
            dz = dobv * (o * r * gnv) * (sz * (1.0 + z * (1.0 - sz)))
            dgn = jnp.sum(don * o * r, axis=0, keepdims=True)
            gh = don * gnv
            gho = gh * o
            mg = cat([jnp.broadcast_to(jnp.mean(gho[:, sl], axis=-1, keepdims=True), (CH, HK)) for sl in hs])
            dob16 = (r * gh - o * (r * r * r * mg)).astype(BF16)

            vb = v.astype(BF16)
            qsb, ksb, qeb, kdb = (g[n_].astype(BF16) for n_ in ("qs", "ks", "qe", "kd"))
            st0s = [st_ref[ci, sl, :] for sl in hs]
            dst1s = [dst[sl, :] for sl in hs]
            dst1bs = [t.astype(BF16) for t in dst1s]
            as_ = [_nt(qsb[:, sl], ksb[:, sl]) for sl in hs]
            das_ = [_nt(dob16[:, sl], vb[:, sl]) for sl in hs]
            dqes = [_nn(dob16[:, sl], s0.astype(BF16)) for sl, s0 in zip(hs, st0s)]
            dkds = [_nn(vb[:, sl], d1) for sl, d1 in zip(hs, dst1bs)]
            dvis = [_nt(kdb[:, sl], d1) for sl, d1 in zip(hs, dst1bs)]
            dsts = [_tn(dob16[:, sl], qeb[:, sl]) for sl in hs]
            abs_ = [jnp.where(low, a, 0.0).astype(BF16) for a in as_]
            dabs_ = [jnp.where(low, a, 0.0).astype(BF16) for a in das_]
            dqss = [_nn(da, ksb[:, sl]) for da, sl in zip(dabs_, hs)]
            dkss = [_tn(da, qsb[:, sl]) for da, sl in zip(dabs_, hs)]
            dvs_ = [_tn(a, dob16[:, sl]) + dvi for a, sl, dvi in zip(abs_, hs, dvis)]
            exs_ = [jnp.sum(d1 * s0, axis=0, keepdims=True) for d1, s0 in zip(dst1s, st0s)]
            for sl, d1, dn in zip(hs, dst1s, dsts):
                dst[sl, :] = dn + d1 * ebl[:, sl]
            dqe, dqs, dks, dkd, dv = cat(dqes), cat(dqss), cat(dkss), cat(dkds), cat(dvs_)
            dq = (dqe * g["ebm"] + dqs) * g["e1"]
            dk = (dks + dkd * g["eblm"]) * g["e2"]
            dkdkd = dkd * g["kd"]
            db = dqe * g["qe"] + dqs * qsb.astype(F32) - dks * ksb.astype(F32) - dkdkd
            ex = jnp.sum(dkdkd, axis=0, keepdims=True) + cat(exs_) * ebl
            dg = _exact_mm(triu, db) + ex
            df = dg / g["f"] - dk
            sf = g["sf"]
            dfr = df * (1.0 - lbv) * sf * (1.0 - sf)
            sq = g["sq"]
            dqr = dq * (sq * (1.0 + g["qraw"] * (1.0 - sq)))
            acc_ref[0:1, :] += jnp.sum(df * (1.0 - sf), axis=0, keepdims=True)
            acc_ref[1:2, :] += dgn
            dh_ref[rows, 0:D] = dqr.astype(BF16)
            dh_ref[rows, D:2 * D] = dfr.astype(BF16)
            dh_ref[rows, 2 * D:3 * D] = dv.astype(BF16)
            dh_ref[rows, 3 * D:4 * D] = dz.astype(BF16)

    def pcol(c):
        return pl.BlockSpec((HRB, D), lambda i: (nblk - 1 - i, c))

    vec = pl.BlockSpec((1, D), lambda i: (0, 0))
    row = pl.BlockSpec((HRB, D), lambda i: (nblk - 1 - i, 0))
    return pl.pallas_call(
        body, name="hgrn_bwd", grid=(nblk,),
        in_specs=[pcol(2), pcol(3), pcol(4), pcol(5), row, row,
                  pl.BlockSpec((cps, NH * HK, HK), lambda i: (nblk - 1 - i, 0, 0)), vec, vec],
        out_specs=[pl.BlockSpec((HRB, 4 * D), lambda i: (nblk - 1 - i, 0)),
                   pl.BlockSpec((8, D), lambda i: (0, 0))],
        out_shape=[jax.ShapeDtypeStruct((S, 4 * D), BF16), jax.ShapeDtypeStruct((8, D), F32)],
        scratch_shapes=[pltpu.VMEM((NH * HK, HK), F32)],
        compiler_params=_cp(),
    )(proj, proj, proj, proj, oraw, dob, states, lb, gn)


def _fwd2_call(oa, ob, proj, x, tgt, vecs, wa, wb, wo):
    S = x.shape[0]
    tm = 256

    def body(oa_ref, ob_ref, ga_ref, gb_ref, x_ref, t_ref, v_ref, wa_ref, wb_ref, wo_ref,
             ya_ref, yb_ref, y_ref, u_ref, x2_ref, ls_ref):
        @pl.when(pl.program_id(0) == 0)
        def _():
            ls_ref[...] = jnp.zeros_like(ls_ref)

        ya = _nn(oa_ref[...], wa_ref[...])
        yb = _nn(ob_ref[...], wb_ref[...])
        y = _sig(ga_ref[...]) * ya + _sig(gb_ref[...]) * yb
        u = _nn(y.astype(BF16), wo_ref[...])
        x2 = x_ref[...] + v_ref[0:1, :] * u
        r = lax.rsqrt(jnp.mean(x2 * x2, axis=-1, keepdims=True) + EPS)
        err = x2 * r * v_ref[1:2, :] - t_ref[...]
        ls_ref[...] += jnp.sum(err * err)
        ya_ref[...] = ya.astype(BF16)
        yb_ref[...] = yb.astype(BF16)
        y_ref[...] = y.astype(BF16)
        u_ref[...] = u.astype(BF16)
        x2_ref[...] = x2

    row = pl.BlockSpec((tm, D), lambda i: (i, 0))
    full = lambda a: pl.BlockSpec(a.shape, lambda i: (0, 0))
    return pl.pallas_call(
        body, name="fwd_merge_out", grid=(S // tm,),
        in_specs=[pl.BlockSpec((tm, AW), lambda i: (i, 0)), row,
                  pl.BlockSpec((tm, D), lambda i: (i, 6)), pl.BlockSpec((tm, D), lambda i: (i, 7)),
                  row, row, pl.BlockSpec((8, D), lambda i: (0, 0)), full(wa), full(wb), full(wo)],
        out_specs=[row, row, row, row, row, pl.BlockSpec((8, 128), lambda i: (0, 0))],
        out_shape=[jax.ShapeDtypeStruct((S, D), BF16)] * 4
                  + [jax.ShapeDtypeStruct((S, D), F32), jax.ShapeDtypeStruct((8, 128), F32)],
        compiler_params=_cp(),
    )(oa, ob, proj, proj, x, tgt, vecs, wa, wb, wo)


def _bwd2_call(x2, tgt, vecs, ya, yb, u, proj, attn, wa, wb, wo, hsum):
    S = x2.shape[0]
    tm = 256

    def body(x2_ref, t_ref, v_ref, ya_ref, yb_ref, u_ref, ga_ref, gb_ref, at_ref, za_ref,
             wa_ref, wb_ref, wo_ref, hs_ref,
             dx2_ref, du_ref, dya_ref, dyb_ref, dg_ref, dat_ref, dsum_ref, dza_ref, dob_ref, acc_ref):
        @pl.when(pl.program_id(0) == 0)
        def _():
            acc_ref[...] = jnp.zeros_like(acc_ref)

        x2v = x2_ref[...]
        gate = v_ref[0:1, :]
        fg = v_ref[1:2, :]
        r = lax.rsqrt(jnp.mean(x2v * x2v, axis=-1, keepdims=True) + EPS)
        dout = (x2v * r * fg - t_ref[...]) * (1.0 / D)
        gh = dout * fg
        dx2 = r * gh - x2v * (r * r * r * jnp.mean(gh * x2v, axis=-1, keepdims=True))
        acc_ref[0:1, :] += jnp.sum(dx2 * u_ref[...].astype(F32), axis=0, keepdims=True)
        acc_ref[1:2, :] += jnp.sum(dout * x2v * r, axis=0, keepdims=True)
        dx2_ref[...] = dx2
        du = (dx2 * gate).astype(BF16)
        du_ref[...] = du
        dy = _nt(du, wo_ref[...])
        sa = _sig(ga_ref[...])
        sb = _sig(gb_ref[...])
        dya = (dy * sa).astype(BF16)
        dyb = (dy * sb).astype(BF16)
        dya_ref[...] = dya
        dyb_ref[...] = dyb
        dg_ref[:, 0:D] = (dy * ya_ref[...].astype(F32) * sa * (1.0 - sa)).astype(BF16)
        dg_ref[:, D:2 * D] = (dy * yb_ref[...].astype(F32) * sb * (1.0 - sb)).astype(BF16)
        doa = _nt(dya, wa_ref[...])
        dob_ref[...] = _nt(dyb, wb_ref[...])
        za = za_ref[...]
        sz = _sig(za)
        att = at_ref[...]
        dat = doa * (za * sz)
        dat_ref[...] = dat
        dza_ref[...] = (doa * att * (sz * (1.0 + za * (1.0 - sz)))).astype(BF16)
        dsum_ref[...] = _exact_mm_r(dat * att, hs_ref[...])

    row = pl.BlockSpec((tm, D), lambda i: (i, 0))
    arow = pl.BlockSpec((tm, AW), lambda i: (i, 0))
    full = lambda a: pl.BlockSpec(a.shape, lambda i: (0, 0))
    return pl.pallas_call(
        body, name="bwd_merge_out", grid=(S // tm,),
        in_specs=[row, row, pl.BlockSpec((8, D), lambda i: (0, 0)), row, row, row,
                  pl.BlockSpec((tm, D), lambda i: (i, 6)), pl.BlockSpec((tm, D), lambda i: (i, 7)),
                  arow, pl.BlockSpec((tm, AW), lambda i: (i, 3)), full(wa), full(wb), full(wo), full(hsum)],
        out_specs=[row, row, row, row, pl.BlockSpec((tm, 2 * D), lambda i: (i, 0)),
                   arow, arow, arow, row, pl.BlockSpec((8, D), lambda i: (0, 0))],
        out_shape=[jax.ShapeDtypeStruct((S, D), F32), jax.ShapeDtypeStruct((S, D), BF16),
                   jax.ShapeDtypeStruct((S, D), BF16), jax.ShapeDtypeStruct((S, D), BF16),
                   jax.ShapeDtypeStruct((S, 2 * D), BF16), jax.ShapeDtypeStruct((S, AW), F32),
                   jax.ShapeDtypeStruct((S, AW), F32), jax.ShapeDtypeStruct((S, AW), BF16),
                   jax.ShapeDtypeStruct((S, D), F32), jax.ShapeDtypeStruct((8, D), F32)],
        compiler_params=_cp(),
    )(x2, tgt, vecs, ya, yb, u, proj, proj, attn, proj, wa, wb, wo, hsum)


def _atb_call(a, b, name):
    S, K = a.shape
    N = b.shape[1]
    tm = 512

    def body(a_ref, b_ref, o_ref):
        @pl.when(pl.program_id(0) == 0)
        def _():
            o_ref[...] = jnp.zeros_like(o_ref)

        o_ref[...] += _tn(a_ref[...], b_ref[...])

    return pl.pallas_call(
        body, name=name, grid=(S // tm,),
        in_specs=[pl.BlockSpec((tm, K), lambda i: (i, 0)), pl.BlockSpec((tm, N), lambda i: (i, 0))],
        out_specs=pl.BlockSpec((K, N), lambda i: (0, 0)),
        out_shape=jax.ShapeDtypeStruct((K, N), F32), compiler_params=_cp(),
    )(a, b)


def _dwin_call(h, dqkvz, d_hgrn, d_gates):
    S = h.shape[0]
    tm = 512
    tn = 2048

    def body(h_ref, q_ref, k_ref, v_ref, z_ref, m_ref, g_ref, o_ref):
        j = pl.program_id(0)

        @pl.when(pl.program_id(1) == 0)
        def _():
            o_ref[...] = jnp.zeros_like(o_ref)

        hv = h_ref[...]

        @pl.when(j == 0)
        def _():
            for cidx, r in enumerate((q_ref, k_ref, v_ref, z_ref)):
                o_ref[0, :, cidx * AW:(cidx + 1) * AW] += _tn(hv, r[...])

        @pl.when(jnp.logical_or(j == 1, j == 2))
        def _():
            o_ref[0] += _tn(hv, m_ref[...])

        @pl.when(j == 3)
        def _():
            o_ref[0] += _tn(hv, g_ref[...])

    aspec = pl.BlockSpec((tm, AW), lambda j, i: (jnp.where(j == 0, i, 0), 0))
    return pl.pallas_call(
        body, name="dw_in", grid=(4, S // tm),
        in_specs=[pl.BlockSpec((tm, D), lambda j, i: (i, 0)), aspec, aspec, aspec, aspec,
                  pl.BlockSpec((tm, tn), lambda j, i: (jnp.where(jnp.logical_or(j == 1, j == 2), i, 0),
                                                       jnp.where(j == 2, 1, 0))),
                  pl.BlockSpec((tm, tn), lambda j, i: (jnp.where(j == 3, i, 0), 0))],
        out_specs=pl.BlockSpec((1, D, tn), lambda j, i: (j, 0, 0)),
        out_shape=jax.ShapeDtypeStruct((4, D, tn), F32), compiler_params=_cp(),
    )(h, *dqkvz, d_hgrn, d_gates)


def _dh_call(dqkvz, d_hgrn, d_gates, w_in, x, dx2, vecs):
    S = x.shape[0]
    tm = 512
    tk = 2048

    def body(q_ref, k_ref, v_ref, z_ref, m_ref, g_ref, w_ref, x_ref, dx2_ref, p_ref, gx_ref, acc_ref, dh):
        j = pl.program_id(1)

        @pl.when(jnp.logical_and(pl.program_id(0) == 0, j == 0))
        def _():
            acc_ref[...] = jnp.zeros_like(acc_ref)

        @pl.when(j == 0)
        def _():
            t = _nt(q_ref[...], w_ref[:, 0:AW])
            for cidx, r in enumerate((k_ref, v_ref, z_ref)):
                t += _nt(r[...], w_ref[:, (cidx + 1) * AW:(cidx + 2) * AW])
            dh[...] = t

        @pl.when(jnp.logical_or(j == 1, j == 2))
        def _():
            dh[...] += _nt(m_ref[...], w_ref[...])

        @pl.when(j == 3)
        def _():
            dhv = dh[...] + _nt(g_ref[...], w_ref[...])
            xv = x_ref[...]
            r = lax.rsqrt(jnp.mean(xv * xv, axis=-1, keepdims=True) + EPS)
            xn = xv * r
            acc_ref[0:1, :] += jnp.sum(dhv, axis=0, keepdims=True)
            acc_ref[1:2, :] += jnp.sum(dhv * xn * p_ref[1:2, :], axis=0, keepdims=True)
            acc_ref[2:3, :] += jnp.sum(dhv * xn * p_ref[2:3, :], axis=0, keepdims=True)
            dxn = dhv * p_ref[0:1, :]
            gx_ref[...] = dx2_ref[...] + r * dxn - xv * (r * r * r * jnp.mean(dxn * xv, axis=-1, keepdims=True))

    row = pl.BlockSpec((tm, D), lambda i, j: (i, 0))
    aspec = pl.BlockSpec((tm, AW), lambda i, j: (i, 0))
    return pl.pallas_call(
        body, name="dh_gradx", grid=(S // tm, 4),
        in_specs=[aspec, aspec, aspec, aspec,
                  pl.BlockSpec((tm, tk), lambda i, j: (i, jnp.where(j == 2, 1, 0))),
                  pl.BlockSpec((tm, tk), lambda i, j: (i, 0)),
                  pl.BlockSpec((D, tk), lambda i, j: (0, j)),
                  row, row, pl.BlockSpec((8, D), lambda i, j: (0, 0))],
        out_specs=[row, pl.BlockSpec((8, D), lambda i, j: (0, 0))],
        out_shape=[jax.ShapeDtypeStruct((S, D), F32), jax.ShapeDtypeStruct((8, D), F32)],
        scratch_shapes=[pltpu.VMEM((tm, D), F32)],
        compiler_params=_cp(),
    )(*dqkvz, d_hgrn, d_gates, w_in, x, dx2, vecs)


def _adamw_math(w, g, m, v):
    m = B1 * m + (1.0 - B1) * g
    v = B2 * v + (1.0 - B2) * (g * g)
    m_hat = m / (1.0 - B1 ** STEP)
    v_hat = v / (1.0 - B2 ** STEP)
    delta = -LR * (m_hat / (jnp.sqrt(v_hat) + AEPS) + WD * w)
    return delta, m, v


def _adamw_call(w, g, m, v, name):
    R, C = w.shape
    tr = R if R * C * 4 <= (1 << 20) else max(8, (1 << 20) // (C * 4))
    assert R % tr == 0

    def body(w_ref, g_ref, m_ref, v_ref, d_ref, nm_ref, nv_ref):
        d_ref[...], nm_ref[...], nv_ref[...] = _adamw_math(w_ref[...], g_ref[...], m_ref[...], v_ref[...])

    blk = pl.BlockSpec((tr, C), lambda i: (i, 0))
    return pl.pallas_call(
        body, name=name, grid=(R // tr,), in_specs=[blk] * 4, out_specs=[blk] * 3,
        out_shape=[jax.ShapeDtypeStruct((R, C), F32)] * 3, compiler_params=_cp(),
    )(w, g, m, v)


def _mod_call(c_all, w_ada_s, b_s):
    def body(c_ref, w_ref, b_ref, o_ref):
        cv = c_ref[...]
        sc = cv * _sig(cv)
        o_ref[...] = jnp.dot(sc, w_ref[...], preferred_element_type=F32,
                             precision=lax.Precision.HIGHEST) + b_ref[...]

    return pl.pallas_call(
        body, name="ada_mod", out_shape=jax.ShapeDtypeStruct((8, w_ada_s.shape[1]), F32),
        compiler_params=_cp(),
    )(c_all, w_ada_s, b_s)


def _ada_update_call(sct, dm, w, m, v):
    R, C = w.shape
    tr = 256

    def body(s_ref, d_ref, w_ref, m_ref, v_ref, g_ref, dl_ref, nm_ref, nv_ref):
        g = s_ref[:, 0:1] * d_ref[0:1, :]
        for b in range(1, 8):
            g = g + s_ref[:, b:b + 1] * d_ref[b:b + 1, :]
        g_ref[...] = g
        dl_ref[...], nm_ref[...], nv_ref[...] = _adamw_math(w_ref[...], g, m_ref[...], v_ref[...])

    blk = pl.BlockSpec((tr, C), lambda i: (i, 0))
    return pl.pallas_call(
        body, name="ada_update", grid=(R // tr,),
        in_specs=[pl.BlockSpec((tr, 8), lambda i: (i, 0)), pl.BlockSpec((8, C), lambda i: (0, 0)), blk, blk, blk],
        out_specs=[blk] * 4, out_shape=[jax.ShapeDtypeStruct((R, C), F32)] * 4, compiler_params=_cp(),
    )(sct, dm, w, m, v)


def _sum8_call(packs):
    def body(p_ref, o_ref):
        acc = p_ref[0]
        for k in range(1, 8):
            acc = acc + p_ref[k]
        o_ref[...] = acc

    return pl.pallas_call(
        body, name="sum_small", out_shape=jax.ShapeDtypeStruct(packs.shape[1:], F32), compiler_params=_cp(),
    )(packs)


def _local_step(x, tgt, shift, scale, gate, norm_g, hgrn_onorm_g, rel_bias, lb, final_g, w_in, wa, wb, wo):
    a = norm_g * (1.0 + scale)
    z6 = jnp.zeros((6, D), F32)
    h = _h_call(x, jnp.concatenate([a, shift, z6], 0))
    proj = _proj_call(h, w_in)

    biases = _bias_tiles(rel_bias)
    attn, lse, oa = _attn_fwd_call(proj, biases)

    gn = jnp.tile(hgrn_onorm_g, (1, NH))
    oraw, ob, states = _hgrn_fwd_call(proj, lb, gn)

    vecs2 = jnp.concatenate([gate, final_g, z6], 0)
    ya, yb, y, u, x2, lsq = _fwd2_call(oa, ob, proj, x, tgt, vecs2, wa, wb, wo)
    loss = 0.5 * lsq[0, 0] / D

    hsum = jnp.asarray(np.kron(np.eye(NH), np.ones((HE, HE))), BF16)
    dx2, du, dya, dyb, d_gates, dattn, dsum, dza, dob, acc2 = _bwd2_call(
        x2, tgt, vecs2, ya, yb, u, proj, attn, wa, wb, wo, hsum)
    d_wo = _atb_call(y, du, "dw_out")
    d_wa = _atb_call(oa, dya, "dw_branch_a")
    d_wb = _atb_call(ob, dyb, "dw_branch_b")

    d_hgrn, acch = _hgrn_bwd_call(proj, oraw, dob, states, lb, gn)

    dq, dk, dv, dbs = _attn_bwd_call(proj, dattn, lse, dsum, biases)
    dqkvz = (dq, dk, dv, dza)

    d_win = _dwin_call(h, dqkvz, d_hgrn, d_gates)
    one_scale = 1.0 + scale
    grad_x, acc1 = _dh_call(dqkvz, d_hgrn, d_gates, w_in, x, dx2,
                            jnp.concatenate([a, norm_g, one_scale, jnp.zeros((5, D), F32)], 0))

    d_rel = jnp.zeros((NBUCKETS, NH), F32)
    for p, d in enumerate(PATTERNS):
        band, bucket = _band_bucket(d)
        onehot = (bucket[None] == np.arange(NBUCKETS)[:, None, None]) & band[None]
        d_rel = d_rel + jnp.einsum("hqk,bqk->bh", dbs[p], jnp.asarray(onehot, F32),
                                   precision=lax.Precision.HIGHEST)
    d_onorm = jnp.sum(acch[1].reshape(NH, HK), axis=0)

    zrow = jnp.zeros((D,), F32)
    pack = jnp.stack([acc1[0], acc1[1], acc2[0], acc1[2], acc2[1], acch[0],
                      zrow.at[:HK].set(d_onorm), zrow.at[0].set(loss),
                      zrow.at[:NBUCKETS * NH].set(d_rel.reshape(-1))] + [zrow] * 7, 0)
    return grad_x, d_win, d_wa, d_wb, d_wo, pack


def _me():
    return lax.axis_index("x"), lax.axis_index("y"), lax.axis_index("c")


def _peers(x, y):
    return [(1 - x, y), (x, 1 - y), (1 - x, 1 - y)]


def _allgather_small(blk, name):
    m_per, n = blk.shape

    def body(x_ref, out_ref, send_sems, recv_sems, local_sem):
        x, y, c = _me()
        me, sibling = (x, y, c), (x, y, 1 - c)
        chips = _peers(x, y)

        def rows(px, py, pc):
            return out_ref.at[pl.ds((4 * px + 2 * py + pc) * m_per, m_per), :]

        def copy(k, block, to, src=None):
            return pltpu.make_async_remote_copy(
                src_ref=rows(*block) if src is None else src, dst_ref=rows(*block),
                send_sem=send_sems.at[k], recv_sem=recv_sems.at[k], device_id=to, device_id_type=MESH)

        mine = pltpu.make_async_copy(x_ref, rows(*me), local_sem)
        mine.start()
        first = [copy(0, me, sibling, src=x_ref)]
        first += [copy(1 + j, me, (*chip, c), src=x_ref) for j, chip in enumerate(chips)]
        for cp in first:
            cp.start()
        passed = [copy(4 + j, (*chip, c), sibling) for j, chip in enumerate(chips)]
        for j, chip in enumerate(chips):
            copy(1 + j, (*chip, c), me).wait_recv()
            passed[j].start()
        copy(0, sibling, me).wait_recv()
        for j, chip in enumerate(chips):
            copy(4 + j, (*chip, 1 - c), me).wait_recv()
        for cp in first + passed:
            cp.wait_send()
        mine.wait()

    return pl.pallas_call(
        body, name=name, out_shape=jax.ShapeDtypeStruct((8 * m_per, n), blk.dtype),
        in_specs=[pl.BlockSpec(memory_space=pltpu.VMEM)], out_specs=pl.BlockSpec(memory_space=pltpu.VMEM),
        scratch_shapes=[pltpu.SemaphoreType.DMA((7,)), pltpu.SemaphoreType.DMA((7,)), pltpu.SemaphoreType.DMA],
    )(blk)


ANY = pl.BlockSpec(memory_space=pl.ANY)


def _gather_weights(win_s, wa_s, wb_s, wo_s):
    shapes = [(D, NPROJ), (AW, D), (D, D), (D, D)]

    def body(win_ref, wa_ref, wb_ref, wo_ref, owin, owa, owb, owo, send_sems, recv_sems, local_sems):
        x, y, c = _me()
        sibling = (x, y, 1 - c)
        chips = _peers(x, y)

        def shard(t, ref, j, half=None):
            if t == 0:
                r0, nr = (0, D) if half is None else (half * (D // 2), D // 2)
                return ref.at[pl.ds(r0, nr), pl.ds(j * 2048, 2048)]
            if t == 1:
                r0, nr = (0, AW) if half is None else (half * (AW // 2), AW // 2)
                return ref.at[pl.ds(r0, nr), pl.ds(j * 256, 256)]
            r0, nr = (0, 256) if half is None else (half * 128, 128)
            return ref.at[pl.ds(j * 256 + r0, nr), :]

        def src_half(t, ref, half):
            if t == 0:
                return ref.at[pl.ds(half * (D // 2), D // 2), :]
            if t == 1:
                return ref.at[pl.ds(half * (AW // 2), AW // 2), :]
            return ref.at[pl.ds(half * 128, 128), :]

        ins = [win_ref, wa_ref, wb_ref, wo_ref]
        outs = [owin, owa, owb, owo]
        me_chip = 2 * x + y
        local, sends, fwds = [], [], []
        for t in range(4):
            lc = pltpu.make_async_copy(ins[t], shard(t, outs[t], me_chip), local_sems.at[t])
            lc.start()
            local.append(lc)
            for k, chip in enumerate(chips):
                cp = pltpu.make_async_remote_copy(
                    src_ref=src_half(t, ins[t], c), dst_ref=shard(t, outs[t], me_chip, c),
                    send_sem=send_sems.at[t, k], recv_sem=recv_sems.at[t, k],
                    device_id=(*chip, c), device_id_type=MESH)
                cp.start()
                sends.append(cp)
        for t in range(4):
            for k, chip in enumerate(chips):
                pj = 2 * chip[0] + chip[1]
                blk = shard(t, outs[t], pj, c)
                pltpu.make_async_remote_copy(
                    src_ref=blk, dst_ref=blk, send_sem=send_sems.at[t, k], recv_sem=recv_sems.at[t, k],
                    device_id=(*chip, c), device_id_type=MESH).wait_recv()
                fw = pltpu.make_async_remote_copy(
                    src_ref=blk, dst_ref=blk, send_sem=send_sems.at[t, 3 + k], recv_sem=recv_sems.at[t, 3 + k],
                    device_id=sibling, device_id_type=MESH)
                fw.start()
                fwds.append(fw)
        for t in range(4):
            for k, chip in enumerate(chips):
                pj = 2 * chip[0] + chip[1]
                blk = shard(t, outs[t], pj, 1 - c)
                pltpu.make_async_remote_copy(
                    src_ref=blk, dst_ref=blk, send_sem=send_sems.at[t, 3 + k], recv_sem=recv_sems.at[t, 3 + k],
                    device_id=sibling, device_id_type=MESH).wait_recv()
        for cp in sends + fwds:
            cp.wait_send()
        for lc in local:
            lc.wait()

    return pl.pallas_call(
        body, name="gather_weights",
        out_shape=[jax.ShapeDtypeStruct(s, BF16) for s in shapes],
        in_specs=[ANY] * 4, out_specs=[ANY] * 4,
        scratch_shapes=[pltpu.SemaphoreType.DMA((4, 6)), pltpu.SemaphoreType.DMA((4, 6)),
                        pltpu.SemaphoreType.DMA((4,))],
    )(win_s, wa_s, wb_s, wo_s)


def _half_of(t, ref, half):
    if t == 0:
        return ref.at[:, pl.ds(half * 512, 512), :]
    if t == 1:
        return ref.at[pl.ds(half * 256, 256), :]
    return ref.at[:, pl.ds(half * 512, 512)]


HALF_SHAPES = [(4, 512, 2048), (256, D), (D, 512), (D, 512)]
PIECE_SHAPES = [(512, 2048), (256, 256), (256, 512), (256, 512)]
SHARD_SHAPES = [(D, 2048), (AW, 256), (256, D), (256, D)]


def _chip_piece(t, ref, j):
    if t == 0:
        return ref.at[j]
    if t == 1:
        return ref.at[:, pl.ds(j * 256, 256)]
    return ref.at[pl.ds(j * 256, 256), :]


def _reduce_sibling_send(gs):
    def body(g0, g1, g2, g3, r0, r1, r2, r3, send_sems, recv_sems):
        x, y, c = _me()
        ins, outs = [g0, g1, g2, g3], [r0, r1, r2, r3]
        cps = []
        for t in range(4):
            cp = pltpu.make_async_remote_copy(
                src_ref=_half_of(t, ins[t], 1 - c), dst_ref=outs[t],
                send_sem=send_sems.at[t], recv_sem=recv_sems.at[t], device_id=(x, y, 1 - c), device_id_type=MESH)
            cp.start()
            cps.append(cp)
        for cp in cps:
            cp.wait_recv()
        for cp in cps:
            cp.wait_send()

    return pl.pallas_call(
        body, name="reduce_sibling", out_shape=[jax.ShapeDtypeStruct(s, BF16) for s in HALF_SHAPES],
        in_specs=[ANY] * 4, out_specs=[ANY] * 4,
        scratch_shapes=[pltpu.SemaphoreType.DMA((4,)), pltpu.SemaphoreType.DMA((4,))],
    )(*gs)


def _reduce_chips_send(hs):
    def body(h0, h1, h2, h3, r0, r1, r2, r3, send_sems, recv_sems):
        x, y, c = _me()
        ins, outs = [h0, h1, h2, h3], [r0, r1, r2, r3]
        cps = []
        for t in range(4):
            for k, chip in enumerate(_peers(x, y)):
                pj = 2 * chip[0] + chip[1]
                cp = pltpu.make_async_remote_copy(
                    src_ref=_chip_piece(t, ins[t], pj), dst_ref=outs[t].at[k],
                    send_sem=send_sems.at[t, k], recv_sem=recv_sems.at[t, k],
                    device_id=(*chip, c), device_id_type=MESH)
                cp.start()
                cps.append(cp)
        for cp in cps:
            cp.wait_recv()
        for cp in cps:
            cp.wait_send()

    return pl.pallas_call(
        body, name="reduce_chips", out_shape=[jax.ShapeDtypeStruct((3,) + s, BF16) for s in PIECE_SHAPES],
        in_specs=[ANY] * 4, out_specs=[ANY] * 4,
        scratch_shapes=[pltpu.SemaphoreType.DMA((4, 3)), pltpu.SemaphoreType.DMA((4, 3))],
    )(*hs)


def _share_sibling(ps):
    def body(p0, p1, p2, p3, o0, o1, o2, o3, send_sems, recv_sems, local_sems):
        x, y, c = _me()
        ins, outs = [p0, p1, p2, p3], [o0, o1, o2, o3]

        def half(t, ref, hf):
            if t == 0:
                return ref.at[pl.ds(hf * 512, 512), :]
            if t == 1:
                return ref.at[pl.ds(hf * 256, 256), :]
            return ref.at[:, pl.ds(hf * 512, 512)]

        cps, lcs = [], []
        for t in range(4):
            lc = pltpu.make_async_copy(ins[t], half(t, outs[t], c), local_sems.at[t])
            lc.start()
            lcs.append(lc)
            cp = pltpu.make_async_remote_copy(
                src_ref=ins[t], dst_ref=half(t, outs[t], c), send_sem=send_sems.at[t], recv_sem=recv_sems.at[t],
                device_id=(x, y, 1 - c), device_id_type=MESH)
            cp.start()
            cps.append(cp)
        for t in range(4):
            pltpu.make_async_remote_copy(
                src_ref=ins[t], dst_ref=half(t, outs[t], 1 - c), send_sem=send_sems.at[t],
                recv_sem=recv_sems.at[t], device_id=(x, y, 1 - c), device_id_type=MESH).wait_recv()
        for cp in cps:
            cp.wait_send()
        for lc in lcs:
            lc.wait()

    return pl.pallas_call(
        body, name="share_sibling", out_shape=[jax.ShapeDtypeStruct(s, F32) for s in SHARD_SHAPES],
        in_specs=[ANY] * 4, out_specs=[ANY] * 4,
        scratch_shapes=[pltpu.SemaphoreType.DMA((4,)), pltpu.SemaphoreType.DMA((4,)),
                        pltpu.SemaphoreType.DMA((4,))],
    )(*ps)


def _cast_call(a, name):
    lead = a.shape[0]
    rest = a.shape[1:]
    per = int(np.prod(rest)) * 4
    tl = lead
    while tl * per > (4 << 20) and tl % 2 == 0:
        tl //= 2
    if len(rest) == 1 and tl % 16 != 0:
        tl = lead
    zeros = (0,) * len(rest)

    def body(a_ref, o_ref):
        o_ref[...] = a_ref[...].astype(BF16)

    blk = pl.BlockSpec((tl,) + rest, lambda i: (i,) + zeros)
    return pl.pallas_call(
        body, name=name, grid=(lead // tl,), in_specs=[blk], out_specs=blk,
        out_shape=jax.ShapeDtypeStruct(a.shape, BF16), compiler_params=_cp(),
    )(a)


def _half_blockspec(t, idx_pos):
    if t == 0:
        return pl.BlockSpec((1, 512, 2048), lambda i, s: (i, s[idx_pos], 0)), 4
    if t == 1:
        return pl.BlockSpec((256, D), lambda i, s: (s[idx_pos], 0)), 1
    return pl.BlockSpec((256, 512), lambda i, s: (i, s[idx_pos])), 4


def _half_out_blockspec(t):
    if t == 0:
        return pl.BlockSpec((1, 512, 2048), lambda i, s: (i, 0, 0))
    if t == 1:
        return pl.BlockSpec((256, D), lambda i, s: (0, 0))
    return pl.BlockSpec((256, 512), lambda i, s: (i, 0))


def _add_half_call(t, own, recv, sc, name):
    in_blk, steps = _half_blockspec(t, 0)
    out_blk = _half_out_blockspec(t)

    def body(s_ref, a_ref, b_ref, o_ref, ob_ref):
        v = a_ref[...] + b_ref[...].astype(F32)
        o_ref[...] = v
        ob_ref[...] = v.astype(BF16)

    return pl.pallas_call(
        body, name=name,
        grid_spec=pltpu.PrefetchScalarGridSpec(
            num_scalar_prefetch=1, grid=(steps,), in_specs=[in_blk, out_blk], out_specs=[out_blk, out_blk]),
        out_shape=[jax.ShapeDtypeStruct(HALF_SHAPES[t], F32), jax.ShapeDtypeStruct(HALF_SHAPES[t], BF16)],
        compiler_params=_cp(),
    )(sc, own, recv)


def _final_piece_call(t, chipsum, recv3, sc, name):
    ps = PIECE_SHAPES[t]
    if t == 0:
        own_blk = pl.BlockSpec((1,) + ps, lambda i, s: (s[1], 0, 0))
    elif t == 1:
        own_blk = pl.BlockSpec(ps, lambda i, s: (0, s[1]))
    else:
        own_blk = pl.BlockSpec(ps, lambda i, s: (s[1], 0))
    r_blk = pl.BlockSpec((3,) + ps, lambda i, s: (0, 0, 0))
    o_blk = pl.BlockSpec(ps, lambda i, s: (0, 0))

    def body(s_ref, a_ref, r_ref, o_ref):
        a = a_ref[0] if t == 0 else a_ref[...]
        o_ref[...] = ((a + r_ref[0].astype(F32)) + r_ref[1].astype(F32)) + r_ref[2].astype(F32)

    return pl.pallas_call(
        body, name=name,
        grid_spec=pltpu.PrefetchScalarGridSpec(
            num_scalar_prefetch=1, grid=(1,), in_specs=[own_blk, r_blk], out_specs=o_blk),
        out_shape=jax.ShapeDtypeStruct(ps, F32), compiler_params=_cp(),
    )(sc, chipsum, recv3)


def _lower_bound_fn(hgrn_lb):
    return jnp.cumsum(jax.nn.softmax(hgrn_lb.astype(F32), axis=0), axis=0)[0]


def kernel(x, c, w_ada, b_ada, norm_g, w_in, hgrn_onorm_g, w_branch_a, w_branch_b, w_out, rel_bias, hgrn_lb, final_g, loss_target, m_w_ada, m_b_ada, m_norm_g, m_w_in, m_hgrn_onorm_g, m_w_branch_a, m_w_branch_b, m_w_out, m_rel_bias, m_hgrn_lb, m_final_g, v_w_ada, v_b_ada, v_norm_g, v_w_in, v_hgrn_onorm_g, v_w_branch_a, v_w_branch_b, v_w_out, v_rel_bias, v_hgrn_lb, v_final_g):
    ax, ay, ac = _me()
    chip = 2 * ax + ay
    dev = 4 * ax + 2 * ay + ac
    sc_idx = jnp.stack([ac, chip]).astype(jnp.int32)

    c_all = _allgather_small(jnp.pad(c, ((0, 7), (0, 0))), "gather_c").reshape(8, 8, D)[:, 0]
    b_s = lax.dynamic_slice(b_ada, (0, chip * 768), (1, 768))
    mod_part = _mod_call(c_all, w_ada[0], b_s)
    mod_all = _allgather_small(mod_part, "gather_mod").reshape(8, 8, 768)
    mod_mine = lax.dynamic_index_in_dim(mod_all, dev, axis=1, keepdims=False)
    mod = mod_mine[0::2].reshape(1, 3 * D)
    shift, scale, gate = mod[:, :D], mod[:, D:2 * D], mod[:, 2 * D:]

    win_f, wa_f, wb_f, wo_f = _gather_weights(
        _cast_call(w_in[0], "cast_w_in"), _cast_call(w_branch_a[0], "cast_w_a"),
        _cast_call(w_branch_b[0], "cast_w_b"), _cast_call(w_out[0], "cast_w_o"))

    lb, lb_vjp = jax.vjp(_lower_bound_fn, hgrn_lb)
    grad_x, d_win, d_wa, d_wb, d_wo, pack = _local_step(
        x[0], loss_target[0], shift, scale, gate, norm_g, hgrn_onorm_g, rel_bias, lb[None, :],
        final_g[None, :], win_f, wa_f, wb_f, wo_f)

    own = [d_win, d_wa, d_wb, d_wo]
    names = ["w_in", "w_a", "w_b", "w_o"]
    sib = _reduce_sibling_send([_cast_call(g, "cast_g_" + n) for g, n in zip(own, names)])
    halves = [_add_half_call(t, own[t], sib[t], sc_idx, "chipsum_" + names[t]) for t in range(4)]
    rec = _reduce_chips_send([hb for _, hb in halves])
    pieces = [_final_piece_call(t, halves[t][0], rec[t], sc_idx, "piece_" + names[t]) for t in range(4)]
    g_win, g_wa, g_wb, g_wo = _share_sibling(pieces)

    packs = _allgather_small(pack, "gather_small").reshape(8, 16, D)
    tot = _sum8_call(packs)
    loss = tot[7, 0]
    g_b_ada = tot[0:3].reshape(1, 3 * D)
    g_norm_g = tot[3:4]
    g_final_g = tot[4]
    (g_hgrn_lb,) = lb_vjp(tot[5])
    g_onorm = tot[6:7, :HK]
    g_rel = tot[8, :NBUCKETS * NH].reshape(NBUCKETS, NH)

    def rows_of(a):
        flat = a.reshape(-1)
        n = -(-flat.shape[0] // D)
        return jnp.pad(flat, (0, n * D - flat.shape[0])).reshape(n, D)

    smalls = [(b_ada, g_b_ada, m_b_ada, v_b_ada), (norm_g, g_norm_g, m_norm_g, v_norm_g),
              (hgrn_onorm_g, g_onorm, m_hgrn_onorm_g, v_hgrn_onorm_g), (rel_bias, g_rel, m_rel_bias, v_rel_bias),
              (hgrn_lb, g_hgrn_lb, m_hgrn_lb, v_hgrn_lb), (final_g, g_final_g, m_final_g, v_final_g)]
    cat = [jnp.concatenate([rows_of(s[k]) for s in smalls], 0) for k in range(4)]
    cat = [jnp.pad(a, ((0, 16 - a.shape[0]), (0, 0))) for a in cat]
    sd, sm, sv = _adamw_call(*cat, "adamw_small")

    def unpack(packed):
        res, r = [], 0
        for s in smalls:
            n = -(-s[0].size // D)
            res.append(packed[r:r + n].reshape(-1)[:s[0].size].reshape(s[0].shape))
            r += n
        return res

    d_small, m_small, v_small = unpack(sd), unpack(sm), unpack(sv)

    sc_all = c_all * jax.nn.sigmoid(c_all)
    dmod_all = packs[:, 0:3].reshape(8, 3 * D)
    dm_s = lax.dynamic_slice(dmod_all, (0, chip * 768), (8, 768))
    g_w_ada, d_w_ada, nm_w_ada, nv_w_ada = _ada_update_call(sc_all.T, dm_s, w_ada[0], m_w_ada[0], v_w_ada[0])

    big = []
    for w, g, m, v, n in [(w_in, g_win, m_w_in, v_w_in, "w_in"), (w_branch_a, g_wa, m_w_branch_a, v_w_branch_a, "w_a"),
                          (w_branch_b, g_wb, m_w_branch_b, v_w_branch_b, "w_b"), (w_out, g_wo, m_w_out, v_w_out, "w_o")]:
        big.append(_adamw_call(w[0], g, m[0], v[0], "adamw_" + n))

    e = lambda a: a[None]
    grads = [e(g_w_ada), g_b_ada, g_norm_g, e(g_win), g_onorm, e(g_wa), e(g_wb), e(g_wo), g_rel, g_hgrn_lb, g_final_g]
    deltas = [e(d_w_ada), d_small[0], d_small[1], e(big[0][0]), d_small[2], e(big[1][0]), e(big[2][0]), e(big[3][0]),
              d_small[3], d_small[4], d_small[5]]
    new_m = [e(nm_w_ada), m_small[0], m_small[1], e(big[0][1]), m_small[2], e(big[1][1]), e(big[2][1]), e(big[3][1]),
             m_small[3], m_small[4], m_small[5]]
    new_v = [e(nv_w_ada), v_small[0], v_small[1], e(big[0][2]), v_small[2], e(big[1][2]), e(big[2][2]), e(big[3][2]),
             v_small[3], v_small[4], v_small[5]]
    return (loss, grad_x[None], *grads, *deltas, *new_m, *new_v)
```

```python
import functools
import math

import numpy as np
import jax
import jax.numpy as jnp
from jax import lax
from jax.experimental import pallas as pl
from jax.experimental.pallas import tpu as pltpu

D = 1024
AW = 512
NH = 8
HE = 64
HK = 128
NPROJ = 8192
ABLK = 128
PATTERNS = (1, 4, 16)
NBUCKETS = 32
MAXDIST = 2048
NEG = -1e30
EPS = 1e-6
CH = 64
LR, B1, B2, AEPS, WD, STEP = 0.001, 0.9, 0.999, 1e-08, 0.01, 10

F32 = jnp.float32
BF16 = jnp.bfloat16
MESH = pl.DeviceIdType.MESH
VMEM_LIMIT = 56 * 1024 * 1024


def _cp(**kw):
    return pltpu.CompilerParams(vmem_limit_bytes=VMEM_LIMIT, **kw)


def _sig(x):
    return 0.5 * jnp.tanh(0.5 * x) + 0.5


def _nt(a, b):
    return lax.dot_general(a, b, (((1,), (1,)), ((), ())), preferred_element_type=F32)


def _tn(a, b):
    return lax.dot_general(a, b, (((0,), (0,)), ((), ())), preferred_element_type=F32)


def _nn(a, b):
    return jnp.dot(a, b, preferred_element_type=F32)


def _split3(x):
    h = x.astype(BF16)
    r = x - h.astype(F32)
    m = r.astype(BF16)
    l = (r - m.astype(F32)).astype(BF16)
    return h, m, l


def _exact_mm(tri_bf16, x):
    h, m, l = _split3(x)
    return _nn(tri_bf16, h) + _nn(tri_bf16, m) + _nn(tri_bf16, l)


def _exact_mm_r(x, ones_bf16):
    h, m, l = _split3(x)
    return _nn(h, ones_bf16) + _nn(m, ones_bf16) + _nn(l, ones_bf16)


def _h_call(x, avec):
    S = x.shape[0]
    tm = 512

    def body(x_ref, a_ref, h_ref):
        xv = x_ref[...]
        r = lax.rsqrt(jnp.mean(xv * xv, axis=-1, keepdims=True) + EPS)
        h_ref[...] = (xv * r * a_ref[0:1, :] + a_ref[1:2, :]).astype(BF16)

    return pl.pallas_call(
        body, name="h_norm", grid=(S // tm,),
        in_specs=[pl.BlockSpec((tm, D), lambda i: (i, 0)), pl.BlockSpec((8, D), lambda i: (0, 0))],
        out_specs=pl.BlockSpec((tm, D), lambda i: (i, 0)),
        out_shape=jax.ShapeDtypeStruct((S, D), BF16), compiler_params=_cp(),
    )(x, avec)


def _proj_call(h, w_in):
    S = h.shape[0]
    tm, tn = 512, 2048

    def body(h_ref, w_ref, o_ref):
        o_ref[...] = _nn(h_ref[...], w_ref[...])

    return pl.pallas_call(
        body, name="in_proj", grid=(NPROJ // tn, S // tm),
        in_specs=[pl.BlockSpec((tm, D), lambda j, i: (i, 0)), pl.BlockSpec((D, tn), lambda j, i: (0, j))],
        out_specs=pl.BlockSpec((tm, tn), lambda j, i: (i, j)),
        out_shape=jax.ShapeDtypeStruct((S, NPROJ), F32), compiler_params=_cp(),
    )(h, w_in)


def _t5_bucket_np(dist):
    max_exact = NBUCKETS // 2
    n = dist.astype(np.float32)
    large = max_exact + (np.log(np.maximum(n, np.float32(1.0)) / np.float32(max_exact))
                         / np.float32(math.log(MAXDIST / max_exact))
                         * np.float32(NBUCKETS - max_exact)).astype(np.int32)
    large = np.minimum(large, NBUCKETS - 1)
    return np.where(dist < max_exact, dist, large)


def _band_bucket(d):
    qi = np.arange(ABLK)[:, None]
    kj = np.arange(2 * ABLK)[None, :]
    delta = qi + ABLK - kj
    band = (delta >= 0) & (delta <= ABLK)
    bucket = _t5_bucket_np(np.clip(delta, 0, None) * d)
    return band, bucket


def _bias_tiles(rel_bias):
    tiles = []
    for d in PATTERNS:
        band, bucket = _band_bucket(d)
        onehot = (jnp.asarray(bucket, jnp.int32)[None] == jnp.arange(NBUCKETS, dtype=jnp.int32)[:, None, None])
        bias = jnp.einsum("bqk,bh->hqk", onehot.astype(F32), rel_bias, precision=lax.Precision.HIGHEST)
        tiles.append(jnp.where(jnp.asarray(band)[None], bias, NEG))
    return jnp.stack(tiles, 0)


ATT = 2048
HP = 2 * HE
AGRP = 4
AGRP_B = 2


def _attn_blocks():
    out = []
    for p, d in enumerate(PATTERNS):
        for r in range(d):
            for n in range(ATT // (d * ABLK)):
                out.append((p, d, r, n))
    return out


def _attn_fwd_call(proj, biases):
    S = proj.shape[0]
    nt = S // ATT

    def body(q_ref, k_ref, v_ref, z_ref, b_ref, a_ref, l_ref, oa_ref, kc, vc, op, lp):
        i = pl.program_id(1)

        @pl.when(i == 0)
        def _():
            kc[0:ATT] = jnp.zeros((ATT, HP), F32)
            vc[0:ATT] = jnp.zeros((ATT, HP), F32)

        @pl.when(i > 0)
        def _():
            kc[0:ATT] = kc[ATT:2 * ATT]
            vc[0:ATT] = vc[ATT:2 * ATT]

        kc[ATT:2 * ATT] = k_ref[...]
        vc[ATT:2 * ATT] = v_ref[...]
        col = lax.broadcasted_iota(jnp.int32, (ABLK, 2 * ABLK), 1)
        dead = jnp.logical_and(i == 0, col < ABLK)
        blocks = _attn_blocks()
        hs = (slice(0, HE), slice(HE, 2 * HE))
        for g0 in range(0, len(blocks), AGRP):
            grp = blocks[g0:g0 + AGRP]
            qrows = [pl.ds(n * ABLK * d + r, ABLK, stride=d) for p, d, r, n in grp]
            krows = [pl.ds(ATT + (n - 1) * ABLK * d + r, 2 * ABLK, stride=d) for p, d, r, n in grp]
            qs = [(q_ref[qr, :] * (HE ** -0.5)).astype(BF16) for qr in qrows]
            ks = [kc[kr, :].astype(BF16) for kr in krows]
            vs = [vc[kr, :].astype(BF16) for kr in krows]
            ss = [[_nt(qs[b][:, sl], ks[b][:, sl]) + b_ref[grp[b][0], e] for e, sl in enumerate(hs)]
                  for b in range(len(grp))]
            ss = [[jnp.where(dead, NEG, s) if grp[b][3] == 0 else s for s in ss[b]] for b in range(len(grp))]
            mxs = [[jnp.max(s, axis=-1, keepdims=True) for s in sb] for sb in ss]
            pes = [[jnp.exp(s - mx) for s, mx in zip(sb, mb)] for sb, mb in zip(ss, mxs)]
            dens = [[jnp.sum(pe, axis=-1, keepdims=True) for pe in pb] for pb in pes]
            pvs = [[_nn(pe.astype(BF16), vs[b][:, sl]) for pe, sl in zip(pes[b], hs)] for b in range(len(grp))]
            for b in range(len(grp)):
                p, d, r, n = grp[b]
                prow = pl.ds(p * ATT + n * ABLK * d + r, ABLK, stride=d)
                lp[prow, :] = jnp.concatenate(
                    [jnp.broadcast_to(mx + jnp.log(dn), (ABLK, HE)) for mx, dn in zip(mxs[b], dens[b])], axis=1)
                op[prow, :] = jnp.concatenate([pv / dn for pv, dn in zip(pvs[b], dens[b])], axis=1)
        rt = 256
        for t in range(ATT // rt):
            rows = slice(t * rt, (t + 1) * rt)
            pr = [slice(p * ATT + t * rt, p * ATT + (t + 1) * rt) for p in range(3)]
            la, lb_, lc = lp[pr[0], :], lp[pr[1], :], lp[pr[2], :]
            m = jnp.maximum(jnp.maximum(la, lb_), lc)
            ea, eb, ec = jnp.exp(la - m), jnp.exp(lb_ - m), jnp.exp(lc - m)
            den = ea + eb + ec
            att = (ea * op[pr[0], :] + eb * op[pr[1], :] + ec * op[pr[2], :]) / den
            a_ref[rows, :] = att
            l_ref[rows, :] = m + jnp.log(den)
            z = z_ref[rows, :]
            oa_ref[rows, :] = (att * (z * _sig(z))).astype(BF16)

    def pcol(c):
        return pl.BlockSpec((ATT, HP), lambda h, i: (i, c * 4 + h))

    out = pl.BlockSpec((ATT, HP), lambda h, i: (i, h))
    return pl.pallas_call(
        body, name="attn_fwd", grid=(4, nt),
        in_specs=[pcol(0), pcol(1), pcol(2), pcol(3),
                  pl.BlockSpec((3, 2, ABLK, 2 * ABLK), lambda h, i: (0, h, 0, 0))],
        out_specs=[out, out, out],
        out_shape=[jax.ShapeDtypeStruct((S, AW), F32), jax.ShapeDtypeStruct((S, AW), F32),
                   jax.ShapeDtypeStruct((S, AW), BF16)],
        scratch_shapes=[pltpu.VMEM((2 * ATT, HP), F32), pltpu.VMEM((2 * ATT, HP), F32),
                        pltpu.VMEM((3 * ATT, HP), F32), pltpu.VMEM((3 * ATT, HP), F32)],
        compiler_params=_cp(),
    )(proj, proj, proj, proj, biases)


def _attn_bwd_call(proj, dattn, lse, dsum, biases):
    S = proj.shape[0]
    nt = S // ATT

    def body(q_ref, k_ref, v_ref, do_ref, l_ref, ds_ref, b_ref, dq_ref, dk_ref, dv_ref, db_ref,
             kc, vc, dkc, dvc, dqa):
        i = pl.program_id(1)

        @pl.when(i == 0)
        def _():
            kc[ATT:2 * ATT] = jnp.zeros((ATT, HP), F32)
            vc[ATT:2 * ATT] = jnp.zeros((ATT, HP), F32)
            dkc[ATT:2 * ATT] = jnp.zeros((ATT, HP), F32)
            dvc[ATT:2 * ATT] = jnp.zeros((ATT, HP), F32)
            db_ref[...] = jnp.zeros_like(db_ref)

        @pl.when(i < nt)
        def _():
            kc[0:ATT] = kc[ATT:2 * ATT]
            vc[0:ATT] = vc[ATT:2 * ATT]
            dkc[0:ATT] = dkc[ATT:2 * ATT]
            dvc[0:ATT] = dvc[ATT:2 * ATT]
            kc[ATT:2 * ATT] = k_ref[...]
            vc[ATT:2 * ATT] = v_ref[...]
            dkc[ATT:2 * ATT] = jnp.zeros((ATT, HP), F32)
            dvc[ATT:2 * ATT] = jnp.zeros((ATT, HP), F32)
            col = lax.broadcasted_iota(jnp.int32, (ABLK, 2 * ABLK), 1)
            dead = jnp.logical_and(i == 0, col < ABLK)
            blocks = _attn_blocks()
            hs = (slice(0, HE), slice(HE, 2 * HE))
            for g0 in range(0, len(blocks), AGRP_B):
                grp = blocks[g0:g0 + AGRP_B]
                nb_ = range(len(grp))
                qrows = [pl.ds(n * ABLK * d + r, ABLK, stride=d) for p, d, r, n in grp]
                krows = [pl.ds(ATT + (n - 1) * ABLK * d + r, 2 * ABLK, stride=d) for p, d, r, n in grp]
                qs = [(q_ref[qr, :] * (HE ** -0.5)).astype(BF16) for qr in qrows]
                ks = [kc[kr, :].astype(BF16) for kr in krows]
                vs = [vc[kr, :].astype(BF16) for kr in krows]
                dos = [do_ref[qr, :].astype(BF16) for qr in qrows]
                lvs = [l_ref[qr, :] for qr in qrows]
                dsvs = [ds_ref[qr, :] for qr in qrows]
                ss = [[_nt(qs[b][:, sl], ks[b][:, sl]) + b_ref[grp[b][0], e] for e, sl in enumerate(hs)] for b in nb_]
                ss = [[jnp.where(dead, NEG, s) if grp[b][3] == 0 else s for s in ss[b]] for b in nb_]
                dps = [[_nt(dos[b][:, sl], vs[b][:, sl]) for sl in hs] for b in nb_]
                pes = [[jnp.exp(ss[b][e] - lvs[b][:, e * HE:e * HE + 1]) for e in range(2)] for b in nb_]
                dscs = [[pes[b][e] * (dps[b][e] - dsvs[b][:, e * HE:e * HE + 1]) for e in range(2)] for b in nb_]
                for b in nb_:
                    for e in range(2):
                        db_ref[grp[b][0], e] += dscs[b][e]
                dsbs = [[t.astype(BF16) for t in tb] for tb in dscs]
                dqs = [[_nn(dsbs[b][e], ks[b][:, sl]) * (HE ** -0.5) for e, sl in enumerate(hs)] for b in nb_]
                dks = [[_tn(dsbs[b][e], qs[b][:, sl]) for e, sl in enumerate(hs)] for b in nb_]
                dvs = [[_tn(pes[b][e].astype(BF16), dos[b][:, sl]) for e, sl in enumerate(hs)] for b in nb_]
                for b in nb_:
                    dq = jnp.concatenate(dqs[b], axis=1)
                    if grp[b][0] == 0:
                        dqa[qrows[b], :] = dq
                    else:
                        dqa[qrows[b], :] += dq
                    dkc[krows[b], :] += jnp.concatenate(dks[b], axis=1)
                    dvc[krows[b], :] += jnp.concatenate(dvs[b], axis=1)
            dq_ref[...] = dqa[...].astype(BF16)
            dk_ref[...] = dkc[0:ATT].astype(BF16)
            dv_ref[...] = dvc[0:ATT].astype(BF16)

        @pl.when(i == nt)
        def _():
            dk_ref[...] = dkc[ATT:2 * ATT].astype(BF16)
            dv_ref[...] = dvc[ATT:2 * ATT].astype(BF16)

    def pcol(c):
        return pl.BlockSpec((ATT, HP), lambda h, i: (jnp.minimum(i, nt - 1), c * 4 + h))

    qrow = pl.BlockSpec((ATT, HP), lambda h, i: (jnp.minimum(i, nt - 1), h))
    krow = pl.BlockSpec((ATT, HP), lambda h, i: (jnp.maximum(i - 1, 0), h))
    bspec = pl.BlockSpec((3, 2, ABLK, 2 * ABLK), lambda h, i: (0, h, 0, 0))
    return pl.pallas_call(
        body, name="attn_bwd", grid=(4, nt + 1),
        in_specs=[pcol(0), pcol(1), pcol(2), qrow, qrow, qrow, bspec],
        out_specs=[qrow, krow, krow, bspec],
        out_shape=[jax.ShapeDtypeStruct((S, AW), BF16)] * 3
                  + [jax.ShapeDtypeStruct((3, NH, ABLK, 2 * ABLK), F32)],
        scratch_shapes=[pltpu.VMEM((2 * ATT, HP), F32)] * 4 + [pltpu.VMEM((ATT, HP), F32)],
        compiler_params=_cp(),
    )(proj, proj, proj, dattn, lse, dsum, biases)


HRB = 256


def _hgrn_gates(q_ref, f_ref, rows, lbv, tri):
    qraw = q_ref[rows, :]
    sq = _sig(qraw)
    q = qraw * sq
    sf = _sig(f_ref[rows, :])
    f = lbv + (1.0 - lbv) * sf
    k = 1.0 - f
    b = _exact_mm(tri, jnp.log(f))
    bl = b[CH - 1:CH, :]
    bm = b[CH // 2 - 1:CH // 2, :]
    e1 = jnp.exp(b - bm)
    e2 = jnp.exp(bm - b)
    ebm = jnp.exp(bm)
    eblm = jnp.exp(bl - bm)
    qs = q * e1
    ks = k * e2
    qe = qs * ebm
    kd = ks * eblm
    return dict(qraw=qraw, sq=sq, sf=sf, f=f, bl=bl, e1=e1, e2=e2, ebm=ebm, eblm=eblm, qe=qe, qs=qs, ks=ks, kd=kd)


def _tri_masks():
    row = lax.broadcasted_iota(jnp.int32, (CH, CH), 0)
    col = lax.broadcasted_iota(jnp.int32, (CH, CH), 1)
    return row >= col


def _hgrn_fwd_call(proj, lb, gn):
    S = proj.shape[0]
    nc = S // CH
    cps = HRB // CH

    def body(q_ref, f_ref, i_ref, z_ref, lb_ref, gn_ref, or_ref, ob_ref, st_ref, st):
        @pl.when(pl.program_id(0) == 0)
        def _():
            st[...] = jnp.zeros_like(st)

        low = _tri_masks()
        tri = low.astype(BF16)
        lbv = lb_ref[...]
        for ci in range(cps):
            rows = slice(ci * CH, (ci + 1) * CH)
            g = _hgrn_gates(q_ref, f_ref, rows, lbv, tri)
            v = i_ref[rows, :]
            ebl = jnp.exp(g["bl"])
            st_ref[ci] = st[...]
            hs = [slice(hh * HK, (hh + 1) * HK) for hh in range(NH)]
            vb = v.astype(BF16)
            qsb, ksb, qeb, kdb = (g[n_].astype(BF16) for n_ in ("qs", "ks", "qe", "kd"))
            s0s = [st[sl, :] for sl in hs]
            as_ = [_nt(qsb[:, sl], ksb[:, sl]) for sl in hs]
            ois = [_nt(qeb[:, sl], s0.astype(BF16)) for sl, s0 in zip(hs, s0s)]
            sts = [_tn(vb[:, sl], kdb[:, sl]) for sl in hs]
            abs_ = [jnp.where(low, a, 0.0).astype(BF16) for a in as_]
            os_ = [oi + _nn(a, vb[:, sl]) for oi, a, sl in zip(ois, abs_, hs)]
            for sl, s0, sn in zip(hs, s0s, sts):
                st[sl, :] = s0 * ebl[:, sl] + sn
            o = jnp.concatenate(os_, axis=1)
            or_ref[rows, :] = o
            rs = [lax.rsqrt(jnp.mean(oh * oh, axis=-1, keepdims=True) + EPS) for oh in os_]
            on = jnp.concatenate([oh * r for oh, r in zip(os_, rs)], axis=1)
            z = z_ref[rows, :]
            ob_ref[rows, :] = (on * gn_ref[...] * (z * _sig(z))).astype(BF16)

    def pcol(c):
        return pl.BlockSpec((HRB, D), lambda i: (i, c))

    vec = pl.BlockSpec((1, D), lambda i: (0, 0))
    row = pl.BlockSpec((HRB, D), lambda i: (i, 0))
    return pl.pallas_call(
        body, name="hgrn_fwd", grid=(S // HRB,),
        in_specs=[pcol(2), pcol(3), pcol(4), pcol(5), vec, vec],
        out_specs=[row, row, pl.BlockSpec((cps, NH * HK, HK), lambda i: (i, 0, 0))],
        out_shape=[jax.ShapeDtypeStruct((S, D), F32), jax.ShapeDtypeStruct((S, D), BF16),
                   jax.ShapeDtypeStruct((nc, NH * HK, HK), F32)],
        scratch_shapes=[pltpu.VMEM((NH * HK, HK), F32)],
        compiler_params=_cp(),
    )(proj, proj, proj, proj, lb, gn)


def _hgrn_bwd_call(proj, oraw, dob, states, lb, gn):
    S = proj.shape[0]
    nblk = S // HRB
    cps = HRB // CH

    def body(q_ref, f_ref, i_ref, z_ref, or_ref, dob_ref, st_ref, lb_ref, gn_ref, dh_ref, acc_ref, dst):
        @pl.when(pl.program_id(0) == 0)
        def _():
            dst[...] = jnp.zeros_like(dst)
            acc_ref[...] = jnp.zeros_like(acc_ref)

        low = _tri_masks()
        tri = low.astype(BF16)
        triu = jnp.logical_not(_tri_masks()) | (lax.broadcasted_iota(jnp.int32, (CH, CH), 0)
                                               == lax.broadcasted_iota(jnp.int32, (CH, CH), 1))
        triu = triu.astype(BF16)
        lbv = lb_ref[...]
        for ci in reversed(range(cps)):
            rows = slice(ci * CH, (ci + 1) * CH)
            g = _hgrn_gates(q_ref, f_ref, rows, lbv, tri)
            v = i_ref[rows, :]
            ebl = jnp.exp(g["bl"])
            hs = [slice(hh * HK, (hh + 1) * HK) for hh in range(NH)]
            cat = lambda lst: jnp.concatenate(lst, axis=1)
            o = or_ref[rows, :]
            z = z_ref[rows, :]
            sz = _sig(z)
            gnv = gn_ref[...]
            dobv = dob_ref[rows, :]
            r = cat([jnp.broadcast_to(lax.rsqrt(jnp.mean(o[:, sl] * o[:, sl], axis=-1, keepdims=True) + EPS),
                                      (CH, HK)) for sl in hs])
            don = dobv * (z * sz)
            dz = dobv * (o * r * gnv) * (sz * (1.0 + z * (1.0 - sz)))
            dgn = jnp.sum(don * o * r, axis=0, keepdims=True)
            gh = don * gnv
            gho = gh * o
            mg = cat([jnp.broadcast_to(jnp.mean(gho[:, sl], axis=-1, keepdims=True), (CH, HK)) for sl in hs])
            dob16 = (r * gh - o * (r * r * r * mg)).astype(BF16)

            vb = v.astype(BF16)
            qsb, ksb, qeb, kdb = (g[n_].astype(BF16) for n_ in ("qs", "ks", "qe", "kd"))
            st0s = [st_ref[ci, sl, :] for sl in hs]
            dst1s = [dst[sl, :] for sl in hs]
            dst1bs = [t.astype(BF16) for t in dst1s]
            as_ = [_nt(qsb[:, sl], ksb[:, sl]) for sl in hs]
            das_ = [_nt(dob16[:, sl], vb[:, sl]) for sl in hs]
            dqes = [_nn(dob16[:, sl], s0.astype(BF16)) for sl, s0 in zip(hs, st0s)]
            dkds = [_nn(vb[:, sl], d1) for sl, d1 in zip(hs, dst1bs)]
            dvis = [_nt(kdb[:, sl], d1) for sl, d1 in zip(hs, dst1bs)]
            dsts = [_tn(dob16[:, sl], qeb[:, sl]) for sl in hs]
            abs_ = [jnp.where(low, a, 0.0).astype(BF16) for a in as_]
            dabs_ = [jnp.where(low, a, 0.0).astype(BF16) for a in das_]
            dqss = [_nn(da, ksb[:, sl]) for da, sl in zip(dabs_, hs)]
            dkss = [_tn(da, qsb[:, sl]) for da, sl in zip(dabs_, hs)]
            dvs_ = [_tn(a, dob16[:, sl]) + dvi for a, sl, dvi in zip(abs_, hs, dvis)]
            exs_ = [jnp.sum(d1 * s0, axis=0, keepdims=True) for d1, s0 in zip(dst1s, st0s)]
            for sl, d1, dn in zip(hs, dst1s, dsts):
                dst[sl, :] = dn + d1 * ebl[:, sl]
            dqe, dqs, dks, dkd, dv = cat(dqes), cat(dqss), cat(dkss), cat(dkds), cat(dvs_)
            dq = (dqe * g["ebm"] + dqs) * g["e1"]
            dk = (dks + dkd * g["eblm"]) * g["e2"]
            dkdkd = dkd * g["kd"]
            db = dqe * g["qe"] + dqs * qsb.astype(F32) - dks * ksb.astype(F32) - dkdkd
            ex = jnp.sum(dkdkd, axis=0, keepdims=True) + cat(exs_) * ebl
            dg = _exact_mm(triu, db) + ex
            df = dg / g["f"] - dk
            sf = g["sf"]
            dfr = df * (1.0 - lbv) * sf * (1.0 - sf)
            sq = g["sq"]
            dqr = dq * (sq * (1.0 + g["qraw"] * (1.0 - sq)))
            acc_ref[0:1, :] += jnp.sum(df * (1.0 - sf), axis=0, keepdims=True)
            acc_ref[1:2, :] += dgn
            dh_ref[rows, 0:D] = dqr.astype(BF16)
            dh_ref[rows, D:2 * D] = dfr.astype(BF16)
            dh_ref[rows, 2 * D:3 * D] = dv.astype(BF16)
            dh_ref[rows, 3 * D:4 * D] = dz.astype(BF16)

    def pcol(c):
        return pl.BlockSpec((HRB, D), lambda i: (nblk - 1 - i, c))

    vec = pl.BlockSpec((1, D), lambda i: (0, 0))
    row = pl.BlockSpec((HRB, D), lambda i: (nblk - 1 - i, 0))
    return pl.pallas_call(
        body, name="hgrn_bwd", grid=(nblk,),
        in_specs=[pcol(2), pcol(3), pcol(4), pcol(5), row, row,
                  pl.BlockSpec((cps, NH * HK, HK), lambda i: (nblk - 1 - i, 0, 0)), vec, vec],
        out_specs=[pl.BlockSpec((HRB, 4 * D), lambda i: (nblk - 1 - i, 0)),
                   pl.BlockSpec((8, D), lambda i: (0, 0))],
        out_shape=[jax.ShapeDtypeStruct((S, 4 * D), BF16), jax.ShapeDtypeStruct((8, D), F32)],
        scratch_shapes=[pltpu.VMEM((NH * HK, HK), F32)],
        compiler_params=_cp(),
    )(proj, proj, proj, proj, oraw, dob, states, lb, gn)


def _fwd2_call(oa, ob, proj, x, tgt, vecs, wa, wb, wo):
    S = x.shape[0]
    tm = 256

    def body(oa_ref, ob_ref, ga_ref, gb_ref, x_ref, t_ref, v_ref, wa_ref, wb_ref, wo_ref,
             ya_ref, yb_ref, y_ref, u_ref, x2_ref, ls_ref):
        @pl.when(pl.program_id(0) == 0)
        def _():
            ls_ref[...] = jnp.zeros_like(ls_ref)

        ya = _nn(oa_ref[...], wa_ref[...])
        yb = _nn(ob_ref[...], wb_ref[...])
        y = _sig(ga_ref[...]) * ya + _sig(gb_ref[...]) * yb
        u = _nn(y.astype(BF16), wo_ref[...])
        x2 = x_ref[...] + v_ref[0:1, :] * u
        r = lax.rsqrt(jnp.mean(x2 * x2, axis=-1, keepdims=True) + EPS)
        err = x2 * r * v_ref[1:2, :] - t_ref[...]
        ls_ref[...] += jnp.sum(err * err)
        ya_ref[...] = ya.astype(BF16)
        yb_ref[...] = yb.astype(BF16)
        y_ref[...] = y.astype(BF16)
        u_ref[...] = u.astype(BF16)
        x2_ref[...] = x2

    row = pl.BlockSpec((tm, D), lambda i: (i, 0))
    full = lambda a: pl.BlockSpec(a.shape, lambda i: (0, 0))
    return pl.pallas_call(
        body, name="fwd_merge_out", grid=(S // tm,),
        in_specs=[pl.BlockSpec((tm, AW), lambda i: (i, 0)), row,
                  pl.BlockSpec((tm, D), lambda i: (i, 6)), pl.BlockSpec((tm, D), lambda i: (i, 7)),
                  row, row, pl.BlockSpec((8, D), lambda i: (0, 0)), full(wa), full(wb), full(wo)],
        out_specs=[row, row, row, row, row, pl.BlockSpec((8, 128), lambda i: (0, 0))],
        out_shape=[jax.ShapeDtypeStruct((S, D), BF16)] * 4
                  + [jax.ShapeDtypeStruct((S, D), F32), jax.ShapeDtypeStruct((8, 128), F32)],
        compiler_params=_cp(),
    )(oa, ob, proj, proj, x, tgt, vecs, wa, wb, wo)


def _bwd2_call(x2, tgt, vecs, ya, yb, u, proj, attn, wa, wb, wo, hsum):
    S = x2.shape[0]
    tm = 256

    def body(x2_ref, t_ref, v_ref, ya_ref, yb_ref, u_ref, ga_ref, gb_ref, at_ref, za_ref,
             wa_ref, wb_ref, wo_ref, hs_ref,
             dx2_ref, du_ref, dya_ref, dyb_ref, dg_ref, dat_ref, dsum_ref, dza_ref, dob_ref, acc_ref):
        @pl.when(pl.program_id(0) == 0)
        def _():
            acc_ref[...] = jnp.zeros_like(acc_ref)

        x2v = x2_ref[...]
        gate = v_ref[0:1, :]
        fg = v_ref[1:2, :]
        r = lax.rsqrt(jnp.mean(x2v * x2v, axis=-1, keepdims=True) + EPS)
        dout = (x2v * r * fg - t_ref[...]) * (1.0 / D)
        gh = dout * fg
        dx2 = r * gh - x2v * (r * r * r * jnp.mean(gh * x2v, axis=-1, keepdims=True))
        acc_ref[0:1, :] += jnp.sum(dx2 * u_ref[...].astype(F32), axis=0, keepdims=True)
        acc_ref[1:2, :] += jnp.sum(dout * x2v * r, axis=0, keepdims=True)
        dx2_ref[...] = dx2
        du = (dx2 * gate).astype(BF16)
        du_ref[...] = du
        dy = _nt(du, wo_ref[...])
        sa = _sig(ga_ref[...])
        sb = _sig(gb_ref[...])
        dya = (dy * sa).astype(BF16)
        dyb = (dy * sb).astype(BF16)
        dya_ref[...] = dya
        dyb_ref[...] = dyb
        dg_ref[:, 0:D] = (dy * ya_ref[...].astype(F32) * sa * (1.0 - sa)).astype(BF16)
        dg_ref[:, D:2 * D] = (dy * yb_ref[...].astype(F32) * sb * (1.0 - sb)).astype(BF16)
        doa = _nt(dya, wa_ref[...])
        dob_ref[...] = _nt(dyb, wb_ref[...])
        za = za_ref[...]
        sz = _sig(za)
        att = at_ref[...]
        dat = doa * (za * sz)
        dat_ref[...] = dat
        dza_ref[...] = (doa * att * (sz * (1.0 + za * (1.0 - sz)))).astype(BF16)
        dsum_ref[...] = _exact_mm_r(dat * att, hs_ref[...])

    row = pl.BlockSpec((tm, D), lambda i: (i, 0))
    arow = pl.BlockSpec((tm, AW), lambda i: (i, 0))
    full = lambda a: pl.BlockSpec(a.shape, lambda i: (0, 0))
    return pl.pallas_call(
        body, name="bwd_merge_out", grid=(S // tm,),
        in_specs=[row, row, pl.BlockSpec((8, D), lambda i: (0, 0)), row, row, row,
                  pl.BlockSpec((tm, D), lambda i: (i, 6)), pl.BlockSpec((tm, D), lambda i: (i, 7)),
                  arow, pl.BlockSpec((tm, AW), lambda i: (i, 3)), full(wa), full(wb), full(wo), full(hsum)],
        out_specs=[row, row, row, row, pl.BlockSpec((tm, 2 * D), lambda i: (i, 0)),
                   arow, arow, arow, row, pl.BlockSpec((8, D), lambda i: (0, 0))],
        out_shape=[jax.ShapeDtypeStruct((S, D), F32), jax.ShapeDtypeStruct((S, D), BF16),
                   jax.ShapeDtypeStruct((S, D), BF16), jax.ShapeDtypeStruct((S, D), BF16),
                   jax.ShapeDtypeStruct((S, 2 * D), BF16), jax.ShapeDtypeStruct((S, AW), F32),
                   jax.ShapeDtypeStruct((S, AW), F32), jax.ShapeDtypeStruct((S, AW), BF16),
                   jax.ShapeDtypeStruct((S, D), F32), jax.ShapeDtypeStruct((8, D), F32)],
        compiler_params=_cp(),
    )(x2, tgt, vecs, ya, yb, u, proj, proj, attn, proj, wa, wb, wo, hsum)


def _atb_call(a, b, name):
    S, K = a.shape
    N = b.shape[1]
    tm = 512

    def body(a_ref, b_ref, o_ref):
        @pl.when(pl.program_id(0) == 0)
        def _():
            o_ref[...] = jnp.zeros_like(o_ref)

        o_ref[...] += _tn(a_ref[...], b_ref[...])

    return pl.pallas_call(
        body, name=name, grid=(S // tm,),
        in_specs=[pl.BlockSpec((tm, K), lambda i: (i, 0)), pl.BlockSpec((tm, N), lambda i: (i, 0))],
        out_specs=pl.BlockSpec((K, N), lambda i: (0, 0)),
        out_shape=jax.ShapeDtypeStruct((K, N), F32), compiler_params=_cp(),
    )(a, b)


def _dwin_call(h, dqkvz, d_hgrn, d_gates):
    S = h.shape[0]
    tm = 512
    tn = 2048

    def body(h_ref, q_ref, k_ref, v_ref, z_ref, m_ref, g_ref, o_ref):
        j = pl.program_id(0)

        @pl.when(pl.program_id(1) == 0)
        def _():
            o_ref[...] = jnp.zeros_like(o_ref)

        hv = h_ref[...]

        @pl.when(j == 0)
        def _():
            for cidx, r in enumerate((q_ref, k_ref, v_ref, z_ref)):
                o_ref[0, :, cidx * AW:(cidx + 1) * AW] += _tn(hv, r[...])

        @pl.when(jnp.logical_or(j == 1, j == 2))
        def _():
            o_ref[0] += _tn(hv, m_ref[...])

        @pl.when(j == 3)
        def _():
            o_ref[0] += _tn(hv, g_ref[...])

    aspec = pl.BlockSpec((tm, AW), lambda j, i: (jnp.where(j == 0, i, 0), 0))
    return pl.pallas_call(
        body, name="dw_in", grid=(4, S // tm),
        in_specs=[pl.BlockSpec((tm, D), lambda j, i: (i, 0)), aspec, aspec, aspec, aspec,
                  pl.BlockSpec((tm, tn), lambda j, i: (jnp.where(jnp.logical_or(j == 1, j == 2), i, 0),
                                                       jnp.where(j == 2, 1, 0))),
                  pl.BlockSpec((tm, tn), lambda j, i: (jnp.where(j == 3, i, 0), 0))],
        out_specs=pl.BlockSpec((1, D, tn), lambda j, i: (j, 0, 0)),
        out_shape=jax.ShapeDtypeStruct((4, D, tn), F32), compiler_params=_cp(),
    )(h, *dqkvz, d_hgrn, d_gates)


def _dh_call(dqkvz, d_hgrn, d_gates, w_in, x, dx2, vecs):
    S = x.shape[0]
    tm = 512
    tk = 2048

    def body(q_ref, k_ref, v_ref, z_ref, m_ref, g_ref, w_ref, x_ref, dx2_ref, p_ref, gx_ref, acc_ref, dh):
        j = pl.program_id(1)

        @pl.when(jnp.logical_and(pl.program_id(0) == 0, j == 0))
        def _():
            acc_ref[...] = jnp.zeros_like(acc_ref)

        @pl.when(j == 0)
        def _():
            t = _nt(q_ref[...], w_ref[:, 0:AW])
            for cidx, r in enumerate((k_ref, v_ref, z_ref)):
                t += _nt(r[...], w_ref[:, (cidx + 1) * AW:(cidx + 2) * AW])
            dh[...] = t

        @pl.when(jnp.logical_or(j == 1, j == 2))
        def _():
            dh[...] += _nt(m_ref[...], w_ref[...])

        @pl.when(j == 3)
        def _():
            dhv = dh[...] + _nt(g_ref[...], w_ref[...])
            xv = x_ref[...]
            r = lax.rsqrt(jnp.mean(xv * xv, axis=-1, keepdims=True) + EPS)
            xn = xv * r
            acc_ref[0:1, :] += jnp.sum(dhv, axis=0, keepdims=True)
            acc_ref[1:2, :] += jnp.sum(dhv * xn * p_ref[1:2, :], axis=0, keepdims=True)
            acc_ref[2:3, :] += jnp.sum(dhv * xn * p_ref[2:3, :], axis=0, keepdims=True)
            dxn = dhv * p_ref[0:1, :]
            gx_ref[...] = dx2_ref[...] + r * dxn - xv * (r * r * r * jnp.mean(dxn * xv, axis=-1, keepdims=True))

    row = pl.BlockSpec((tm, D), lambda i, j: (i, 0))
    aspec = pl.BlockSpec((tm, AW), lambda i, j: (i, 0))
    return pl.pallas_call(
        body, name="dh_gradx", grid=(S // tm, 4),
        in_specs=[aspec, aspec, aspec, aspec,
                  pl.BlockSpec((tm, tk), lambda i, j: (i, jnp.where(j == 2, 1, 0))),
                  pl.BlockSpec((tm, tk), lambda i, j: (i, 0)),
                  pl.BlockSpec((D, tk), lambda i, j: (0, j)),
                  row, row, pl.BlockSpec((8, D), lambda i, j: (0, 0))],
        out_specs=[row, pl.BlockSpec((8, D), lambda i, j: (0, 0))],
        out_shape=[jax.ShapeDtypeStruct((S, D), F32), jax.ShapeDtypeStruct((8, D), F32)],
        scratch_shapes=[pltpu.VMEM((tm, D), F32)],
        compiler_params=_cp(),
    )(*dqkvz, d_hgrn, d_gates, w_in, x, dx2, vecs)


def _adamw_math(w, g, m, v):
    m = B1 * m + (1.0 - B1) * g
    v = B2 * v + (1.0 - B2) * (g * g)
    m_hat = m / (1.0 - B1 ** STEP)
    v_hat = v / (1.0 - B2 ** STEP)
    delta = -LR * (m_hat / (jnp.sqrt(v_hat) + AEPS) + WD * w)
    return delta, m, v


def _adamw_call(w, g, m, v, name):
    R, C = w.shape
    tr = R if R * C * 4 <= (1 << 20) else max(8, (1 << 20) // (C * 4))
    assert R % tr == 0

    def body(w_ref, g_ref, m_ref, v_ref, d_ref, nm_ref, nv_ref):
        d_ref[...], nm_ref[...], nv_ref[...] = _adamw_math(w_ref[...], g_ref[...], m_ref[...], v_ref[...])

    blk = pl.BlockSpec((tr, C), lambda i: (i, 0))
    return pl.pallas_call(
        body, name=name, grid=(R // tr,), in_specs=[blk] * 4, out_specs=[blk] * 3,
        out_shape=[jax.ShapeDtypeStruct((R, C), F32)] * 3, compiler_params=_cp(),
    )(w, g, m, v)


def _mod_call(c_all, w_ada_s, b_s):
    def body(c_ref, w_ref, b_ref, o_ref):
        cv = c_ref[...]
        sc = cv * _sig(cv)
        o_ref[...] = jnp.dot(sc, w_ref[...], preferred_element_type=F32,
                             precision=lax.Precision.HIGHEST) + b_ref[...]

    return pl.pallas_call(
        body, name="ada_mod", out_shape=jax.ShapeDtypeStruct((8, w_ada_s.shape[1]), F32),
        compiler_params=_cp(),
    )(c_all, w_ada_s, b_s)


def _ada_update_call(sct, dm, w, m, v):
    R, C = w.shape
    tr = 256

    def body(s_ref, d_ref, w_ref, m_ref, v_ref, g_ref, dl_ref, nm_ref, nv_ref):
        g = s_ref[:, 0:1] * d_ref[0:1, :]
        for b in range(1, 8):
            g = g + s_ref[:, b:b + 1] * d_ref[b:b + 1, :]
        g_ref[...] = g
        dl_ref[...], nm_ref[...], nv_ref[...] = _adamw_math(w_ref[...], g, m_ref[...], v_ref[...])

    blk = pl.BlockSpec((tr, C), lambda i: (i, 0))
    return pl.pallas_call(
        body, name="ada_update", grid=(R // tr,),
        in_specs=[pl.BlockSpec((tr, 8), lambda i: (i, 0)), pl.BlockSpec((8, C), lambda i: (0, 0)), blk, blk, blk],
        out_specs=[blk] * 4, out_shape=[jax.ShapeDtypeStruct((R, C), F32)] * 4, compiler_params=_cp(),
    )(sct, dm, w, m, v)


def _sum8_call(packs):
    def body(p_ref, o_ref):
        acc = p_ref[0]
        for k in range(1, 8):
            acc = acc + p_ref[k]
        o_ref[...] = acc

    return pl.pallas_call(
        body, name="sum_small", out_shape=jax.ShapeDtypeStruct(packs.shape[1:], F32), compiler_params=_cp(),
    )(packs)


def _local_step(x, tgt, shift, scale, gate, norm_g, hgrn_onorm_g, rel_bias, lb, final_g, w_in, wa, wb, wo,
                hook=None):
    a = norm_g * (1.0 + scale)
    z6 = jnp.zeros((6, D), F32)
    h = _h_call(x, jnp.concatenate([a, shift, z6], 0))
    proj = _proj_call(h, w_in)

    biases = _bias_tiles(rel_bias)
    attn, lse, oa = _attn_fwd_call(proj, biases)

    gn = jnp.tile(hgrn_onorm_g, (1, NH))
    oraw, ob, states = _hgrn_fwd_call(proj, lb, gn)

    vecs2 = jnp.concatenate([gate, final_g, z6], 0)
    ya, yb, y, u, x2, lsq = _fwd2_call(oa, ob, proj, x, tgt, vecs2, wa, wb, wo)
    loss = 0.5 * lsq[0, 0] / D

    hsum = jnp.asarray(np.kron(np.eye(NH), np.ones((HE, HE))), BF16)
    dx2, du, dya, dyb, d_gates, dattn, dsum, dza, dob, acc2 = _bwd2_call(
        x2, tgt, vecs2, ya, yb, u, proj, attn, wa, wb, wo, hsum)
    d_wo = _atb_call(y, du, "dw_out")
    d_wa = _atb_call(oa, dya, "dw_branch_a")
    d_wb = _atb_call(ob, dyb, "dw_branch_b")

    d_hgrn, acch = _hgrn_bwd_call(proj, oraw, dob, states, lb, gn)

    dq, dk, dv, dbs = _attn_bwd_call(proj, dattn, lse, dsum, biases)
    dqkvz = (dq, dk, dv, dza)

    d_win = _dwin_call(h, dqkvz, d_hgrn, d_gates)
    tok = hook(d_win, d_wa, d_wb, d_wo) if hook is not None else 0.0
    one_scale = 1.0 + scale
    grad_x, acc1 = _dh_call(dqkvz, d_hgrn, d_gates, w_in, x, dx2,
                            jnp.concatenate([a + tok, norm_g, one_scale, jnp.zeros((5, D), F32)], 0))

    d_rel = jnp.zeros((NBUCKETS, NH), F32)
    for p, d in enumerate(PATTERNS):
        band, bucket = _band_bucket(d)
        onehot = (bucket[None] == np.arange(NBUCKETS)[:, None, None]) & band[None]
        d_rel = d_rel + jnp.einsum("hqk,bqk->bh", dbs[p], jnp.asarray(onehot, F32),
                                   precision=lax.Precision.HIGHEST)
    d_onorm = jnp.sum(acch[1].reshape(NH, HK), axis=0)

    zrow = jnp.zeros((D,), F32)
    pack = jnp.stack([acc1[0], acc1[1], acc2[0], acc1[2], acc2[1], acch[0],
                      zrow.at[:HK].set(d_onorm), zrow.at[0].set(loss),
                      zrow.at[:NBUCKETS * NH].set(d_rel.reshape(-1))] + [zrow] * 7, 0)
    return grad_x, d_win, d_wa, d_wb, d_wo, pack


def _me():
    return lax.axis_index("x"), lax.axis_index("y"), lax.axis_index("c")


def _peers(x, y):
    return [(1 - x, y), (x, 1 - y), (1 - x, 1 - y)]


def _allgather_small(blk, name):
    m_per, n = blk.shape

    def body(x_ref, out_ref, send_sems, recv_sems, local_sem):
        x, y, c = _me()
        me, sibling = (x, y, c), (x, y, 1 - c)
        chips = _peers(x, y)

        def rows(px, py, pc):
            return out_ref.at[pl.ds((4 * px + 2 * py + pc) * m_per, m_per), :]

        def copy(k, block, to, src=None):
            return pltpu.make_async_remote_copy(
                src_ref=rows(*block) if src is None else src, dst_ref=rows(*block),
                send_sem=send_sems.at[k], recv_sem=recv_sems.at[k], device_id=to, device_id_type=MESH)

        mine = pltpu.make_async_copy(x_ref, rows(*me), local_sem)
        mine.start()
        first = [copy(0, me, sibling, src=x_ref)]
        first += [copy(1 + j, me, (*chip, c), src=x_ref) for j, chip in enumerate(chips)]
        for cp in first:
            cp.start()
        passed = [copy(4 + j, (*chip, c), sibling) for j, chip in enumerate(chips)]
        for j, chip in enumerate(chips):
            copy(1 + j, (*chip, c), me).wait_recv()
            passed[j].start()
        copy(0, sibling, me).wait_recv()
        for j, chip in enumerate(chips):
            copy(4 + j, (*chip, 1 - c), me).wait_recv()
        for cp in first + passed:
            cp.wait_send()
        mine.wait()

    return pl.pallas_call(
        body, name=name, out_shape=jax.ShapeDtypeStruct((8 * m_per, n), blk.dtype),
        in_specs=[pl.BlockSpec(memory_space=pltpu.VMEM)], out_specs=pl.BlockSpec(memory_space=pltpu.VMEM),
        scratch_shapes=[pltpu.SemaphoreType.DMA((7,)), pltpu.SemaphoreType.DMA((7,)), pltpu.SemaphoreType.DMA],
    )(blk)


ANY = pl.BlockSpec(memory_space=pl.ANY)


def _gather_weights(fulls):
    def body(iwin, iwa, iwb, iwo, owin, owa, owb, owo, send_sems, recv_sems):
        x, y, c = _me()
        sibling = (x, y, 1 - c)
        chips = _peers(x, y)

        def shard(t, ref, j, half):
            if t == 0:
                return ref.at[pl.ds(half * (D // 2), D // 2), pl.ds(j * 2048, 2048)]
            if t == 1:
                return ref.at[pl.ds(half * (AW // 2), AW // 2), pl.ds(j * 256, 256)]
            return ref.at[pl.ds(j * 256 + half * 128, 128), :]

        outs = [owin, owa, owb, owo]
        me_chip = 2 * x + y
        sends, fwds = [], []
        for t in range(4):
            for k, chip in enumerate(chips):
                mine = shard(t, outs[t], me_chip, c)
                cp = pltpu.make_async_remote_copy(
                    src_ref=mine, dst_ref=mine, send_sem=send_sems.at[t, k], recv_sem=recv_sems.at[t, k],
                    device_id=(*chip, c), device_id_type=MESH)
                cp.start()
                sends.append(cp)
        for t in range(4):
            for k, chip in enumerate(chips):
                pj = 2 * chip[0] + chip[1]
                blk = shard(t, outs[t], pj, c)
                pltpu.make_async_remote_copy(
                    src_ref=blk, dst_ref=blk, send_sem=send_sems.at[t, k], recv_sem=recv_sems.at[t, k],
                    device_id=(*chip, c), device_id_type=MESH).wait_recv()
                fw = pltpu.make_async_remote_copy(
                    src_ref=blk, dst_ref=blk, send_sem=send_sems.at[t, 3 + k], recv_sem=recv_sems.at[t, 3 + k],
                    device_id=sibling, device_id_type=MESH)
                fw.start()
                fwds.append(fw)
        for t in range(4):
            for k, chip in enumerate(chips):
                pj = 2 * chip[0] + chip[1]
                blk = shard(t, outs[t], pj, 1 - c)
                pltpu.make_async_remote_copy(
                    src_ref=blk, dst_ref=blk, send_sem=send_sems.at[t, 3 + k], recv_sem=recv_sems.at[t, 3 + k],
                    device_id=sibling, device_id_type=MESH).wait_recv()
        for cp in sends + fwds:
            cp.wait_send()

    return pl.pallas_call(
        body, name="gather_weights",
        out_shape=[jax.ShapeDtypeStruct(s, BF16) for s in FULL_W_SHAPES],
        in_specs=[ANY] * 4, out_specs=[ANY] * 4, input_output_aliases={0: 0, 1: 1, 2: 2, 3: 3},
        scratch_shapes=[pltpu.SemaphoreType.DMA((4, 6)), pltpu.SemaphoreType.DMA((4, 6))],
    )(*fulls)


def _half_of(t, ref, half):
    if t == 0:
        return ref.at[:, pl.ds(half * 512, 512), :]
    if t == 1:
        return ref.at[pl.ds(half * 256, 256), :]
    return ref.at[:, pl.ds(half * 512, 512)]


HALF_SHAPES = [(4, 512, 2048), (256, D), (D, 512), (D, 512)]
PIECE_SHAPES = [(512, 2048), (256, 256), (256, 512), (256, 512)]
SHARD_SHAPES = [(D, 2048), (AW, 256), (256, D), (256, D)]


def _chip_piece(t, ref, j):
    if t == 0:
        return ref.at[j]
    if t == 1:
        return ref.at[:, pl.ds(j * 256, 256)]
    return ref.at[pl.ds(j * 256, 256), :]


def _reduce_sibling_send(gs):
    def body(g0, g1, g2, g3, r0, r1, r2, r3, send_sems, recv_sems):
        x, y, c = _me()
        ins, outs = [g0, g1, g2, g3], [r0, r1, r2, r3]
        cps = []
        for t in range(4):
            cp = pltpu.make_async_remote_copy(
                src_ref=_half_of(t, ins[t], 1 - c), dst_ref=outs[t],
                send_sem=send_sems.at[t], recv_sem=recv_sems.at[t], device_id=(x, y, 1 - c), device_id_type=MESH)
            cp.start()
            cps.append(cp)
        for cp in cps:
            cp.wait_recv()
        for cp in cps:
            cp.wait_send()

    return pl.pallas_call(
        body, name="reduce_sibling", out_shape=[jax.ShapeDtypeStruct(s, BF16) for s in HALF_SHAPES],
        in_specs=[ANY] * 4, out_specs=[ANY] * 4,
        scratch_shapes=[pltpu.SemaphoreType.DMA((4,)), pltpu.SemaphoreType.DMA((4,))],
    )(*gs)


HBM = pl.BlockSpec(memory_space=pltpu.HBM)
SEM = pl.BlockSpec(memory_space=pltpu.SEMAPHORE)
EFFECT = pltpu.SideEffectType.DATAFLOW_SIDE_EFFECTING


def _chip_copies(hs, lands, send_sems, recv_sems):
    x, y, c = _me()
    cps = []
    for t in range(4):
        for k, chip in enumerate(_peers(x, y)):
            pj = 2 * chip[0] + chip[1]
            cps.append(pltpu.make_async_remote_copy(
                src_ref=_chip_piece(t, hs[t], pj), dst_ref=lands[t].at[k],
                send_sem=send_sems.at[3 * t + k], recv_sem=recv_sems.at[3 * t + k],
                device_id=(*chip, c), device_id_type=MESH))
    return cps


def _reduce_chips_start(hs):
    lands = [lax.empty((3,) + s, BF16) for s in PIECE_SHAPES]

    def body(h0, h1, h2, h3, l0, l1, l2, l3, send_sems, recv_sems, t0, t1, t2, t3, t4, t5, t6, t7, token):
        for cp in _chip_copies([h0, h1, h2, h3], [l0, l1, l2, l3], send_sems, recv_sems):
            cp.start()
        token[...] = jnp.zeros_like(token)

    bufs = list(hs) + lands
    res = pl.pallas_call(
        body, name="reduce_chips_start",
        out_shape=(pltpu.SemaphoreType.DMA((12,)), pltpu.SemaphoreType.DMA((12,)),
                   *[pltpu.HBM(a.shape, a.dtype) for a in bufs], jax.ShapeDtypeStruct((8, 128), F32)),
        in_specs=[HBM] * 8, out_specs=(SEM, SEM, *[HBM] * 8, pl.BlockSpec(memory_space=pltpu.VMEM)),
        input_output_aliases={i: 2 + i for i in range(8)},
        compiler_params=pltpu.CompilerParams(has_side_effects=EFFECT),
    )(*[pltpu.with_memory_space_constraint(a, pltpu.HBM) for a in bufs])
    return res[0], res[1], list(res[2:10]), res[10]


def _reduce_chips_wait(send_sems, recv_sems, thru, after):
    def body(h0, h1, h2, h3, l0, l1, l2, l3, send_sems, recv_sems, after_ref, d0, d1, d2, d3, g0, g1, g2, g3):
        cps = _chip_copies([h0, h1, h2, h3], [l0, l1, l2, l3], send_sems, recv_sems)
        for cp in cps:
            cp.wait_send()
        for cp in cps:
            cp.wait_recv()

    res = pl.pallas_call(
        body, name="reduce_chips_wait", out_shape=tuple(pltpu.HBM(a.shape, a.dtype) for a in thru),
        in_specs=[HBM] * 8 + [SEM, SEM, ANY], out_specs=[HBM] * 8,
        input_output_aliases={i: i for i in range(8)},
        compiler_params=pltpu.CompilerParams(has_side_effects=EFFECT),
    )(*thru, send_sems, recv_sems, after)
    return list(res[4:8])


def _share_sibling(shards):
    def body(i0, i1, i2, i3, o0, o1, o2, o3, send_sems, recv_sems):
        x, y, c = _me()
        outs = [o0, o1, o2, o3]

        def half(t, ref, hf):
            if t == 0:
                return ref.at[pl.ds(hf * 512, 512), :]
            if t == 1:
                return ref.at[pl.ds(hf * 256, 256), :]
            return ref.at[:, pl.ds(hf * 512, 512)]

        cps = []
        for t in range(4):
            mine = half(t, outs[t], c)
            cp = pltpu.make_async_remote_copy(
                src_ref=mine, dst_ref=mine, send_sem=send_sems.at[t], recv_sem=recv_sems.at[t],
                device_id=(x, y, 1 - c), device_id_type=MESH)
            cp.start()
            cps.append(cp)
        for t in range(4):
            theirs = half(t, outs[t], 1 - c)
            pltpu.make_async_remote_copy(
                src_ref=theirs, dst_ref=theirs, send_sem=send_sems.at[t],
                recv_sem=recv_sems.at[t], device_id=(x, y, 1 - c), device_id_type=MESH).wait_recv()
        for cp in cps:
            cp.wait_send()

    return pl.pallas_call(
        body, name="share_sibling", out_shape=[jax.ShapeDtypeStruct(s, F32) for s in SHARD_SHAPES],
        in_specs=[ANY] * 4, out_specs=[ANY] * 4, input_output_aliases={0: 0, 1: 1, 2: 2, 3: 3},
        scratch_shapes=[pltpu.SemaphoreType.DMA((4,)), pltpu.SemaphoreType.DMA((4,))],
    )(*shards)


def _cast_call(a, name):
    lead = a.shape[0]
    rest = a.shape[1:]
    per = int(np.prod(rest)) * 4
    tl = lead
    while tl * per > (4 << 20) and tl % 2 == 0:
        tl //= 2
    if len(rest) == 1 and tl % 16 != 0:
        tl = lead
    zeros = (0,) * len(rest)

    def body(a_ref, o_ref):
        o_ref[...] = a_ref[...].astype(BF16)

    blk = pl.BlockSpec((tl,) + rest, lambda i: (i,) + zeros)
    return pl.pallas_call(
        body, name=name, grid=(lead // tl,), in_specs=[blk], out_specs=blk,
        out_shape=jax.ShapeDtypeStruct(a.shape, BF16), compiler_params=_cp(),
    )(a)


def _half_blockspec(t, idx_pos):
    if t == 0:
        return pl.BlockSpec((1, 512, 2048), lambda i, s: (i, s[idx_pos], 0)), 4
    if t == 1:
        return pl.BlockSpec((256, D), lambda i, s: (s[idx_pos], 0)), 1
    return pl.BlockSpec((256, 512), lambda i, s: (i, s[idx_pos])), 4


def _half_out_blockspec(t):
    if t == 0:
        return pl.BlockSpec((1, 512, 2048), lambda i, s: (i, 0, 0))
    if t == 1:
        return pl.BlockSpec((256, D), lambda i, s: (0, 0))
    return pl.BlockSpec((256, 512), lambda i, s: (i, 0))


def _add_half_call(t, own, recv, sc, name):
    in_blk, steps = _half_blockspec(t, 0)
    out_blk = _half_out_blockspec(t)

    def body(s_ref, a_ref, b_ref, o_ref, ob_ref):
        v = a_ref[...] + b_ref[...].astype(F32)
        o_ref[...] = v
        ob_ref[...] = v.astype(BF16)

    return pl.pallas_call(
        body, name=name,
        grid_spec=pltpu.PrefetchScalarGridSpec(
            num_scalar_prefetch=1, grid=(steps,), in_specs=[in_blk, out_blk], out_specs=[out_blk, out_blk]),
        out_shape=[jax.ShapeDtypeStruct(HALF_SHAPES[t], F32), jax.ShapeDtypeStruct(HALF_SHAPES[t], BF16)],
        compiler_params=_cp(),
    )(sc, own, recv)


def _final_piece_call(t, chipsum, recv3, sc, name):
    ps = PIECE_SHAPES[t]
    if t == 0:
        own_blk = pl.BlockSpec((1,) + ps, lambda i, s: (s[1], 0, 0))
        o_blk = pl.BlockSpec(ps, lambda i, s: (s[0], 0))
    elif t == 1:
        own_blk = pl.BlockSpec(ps, lambda i, s: (0, s[1]))
        o_blk = pl.BlockSpec(ps, lambda i, s: (s[0], 0))
    else:
        own_blk = pl.BlockSpec(ps, lambda i, s: (s[1], 0))
        o_blk = pl.BlockSpec(ps, lambda i, s: (0, s[0]))
    r_blk = pl.BlockSpec((3,) + ps, lambda i, s: (0, 0, 0))

    def body(s_ref, a_ref, r_ref, o_ref):
        a = a_ref[0] if t == 0 else a_ref[...]
        o_ref[...] = ((a + r_ref[0].astype(F32)) + r_ref[1].astype(F32)) + r_ref[2].astype(F32)

    return pl.pallas_call(
        body, name=name,
        grid_spec=pltpu.PrefetchScalarGridSpec(
            num_scalar_prefetch=1, grid=(1,), in_specs=[own_blk, r_blk], out_specs=o_blk),
        out_shape=jax.ShapeDtypeStruct(SHARD_SHAPES[t], F32), compiler_params=_cp(),
    )(sc, chipsum, recv3)


FULL_W_SHAPES = [(D, NPROJ), (AW, D), (D, D), (D, D)]


def _cast_place_call(t, shard, sc, name):
    if t == 0:
        blk, steps = (512, 2048), 2
        in_blk = pl.BlockSpec(blk, lambda i, s: (i, 0))
        o_blk = pl.BlockSpec(blk, lambda i, s: (i, s[1]))
    elif t == 1:
        blk, steps = (AW, 256), 1
        in_blk = pl.BlockSpec(blk, lambda i, s: (0, 0))
        o_blk = pl.BlockSpec(blk, lambda i, s: (0, s[1]))
    else:
        blk, steps = (256, D), 1
        in_blk = pl.BlockSpec(blk, lambda i, s: (0, 0))
        o_blk = pl.BlockSpec(blk, lambda i, s: (s[1], 0))

    def body(s_ref, a_ref, o_ref):
        o_ref[...] = a_ref[...].astype(BF16)

    return pl.pallas_call(
        body, name=name,
        grid_spec=pltpu.PrefetchScalarGridSpec(
            num_scalar_prefetch=1, grid=(steps,), in_specs=[in_blk], out_specs=o_blk),
        out_shape=jax.ShapeDtypeStruct(FULL_W_SHAPES[t], BF16), compiler_params=_cp(),
    )(sc, shard)


def _lower_bound_fn(hgrn_lb):
    return jnp.cumsum(jax.nn.softmax(hgrn_lb.astype(F32), axis=0), axis=0)[0]


def kernel(x, c, w_ada, b_ada, norm_g, w_in, hgrn_onorm_g, w_branch_a, w_branch_b, w_out, rel_bias, hgrn_lb, final_g, loss_target, m_w_ada, m_b_ada, m_norm_g, m_w_in, m_hgrn_onorm_g, m_w_branch_a, m_w_branch_b, m_w_out, m_rel_bias, m_hgrn_lb, m_final_g, v_w_ada, v_b_ada, v_norm_g, v_w_in, v_hgrn_onorm_g, v_w_branch_a, v_w_branch_b, v_w_out, v_rel_bias, v_hgrn_lb, v_final_g):
    ax, ay, ac = _me()
    chip = 2 * ax + ay
    dev = 4 * ax + 2 * ay + ac
    sc_idx = jnp.stack([ac, chip]).astype(jnp.int32)

    c_all = _allgather_small(jnp.pad(c, ((0, 7), (0, 0))), "gather_c").reshape(8, 8, D)[:, 0]
    b_s = lax.dynamic_slice(b_ada, (0, chip * 768), (1, 768))
    mod_part = _mod_call(c_all, w_ada[0], b_s)
    mod_all = _allgather_small(mod_part, "gather_mod").reshape(8, 8, 768)
    mod_mine = lax.dynamic_index_in_dim(mod_all, dev, axis=1, keepdims=False)
    mod = mod_mine[0::2].reshape(1, 3 * D)
    shift, scale, gate = mod[:, :D], mod[:, D:2 * D], mod[:, 2 * D:]

    names = ["w_in", "w_a", "w_b", "w_o"]
    shards = [w_in[0], w_branch_a[0], w_branch_b[0], w_out[0]]
    win_f, wa_f, wb_f, wo_f = _gather_weights(
        [_cast_place_call(t, shards[t], sc_idx, "cast_" + names[t]) for t in range(4)])

    flight = {}

    def start_reduction(*own):
        sib = _reduce_sibling_send([_cast_call(g, "cast_g_" + n) for g, n in zip(own, names)])
        halves = [_add_half_call(t, own[t], sib[t], sc_idx, "chipsum_" + names[t]) for t in range(4)]
        send_sems, recv_sems, thru, token = _reduce_chips_start([hb for _, hb in halves])
        flight.update(sems=(send_sems, recv_sems), thru=thru, sums=[hf for hf, _ in halves])
        return token[0, 0]

    lb, lb_vjp = jax.vjp(_lower_bound_fn, hgrn_lb)
    grad_x, d_win, d_wa, d_wb, d_wo, pack = _local_step(
        x[0], loss_target[0], shift, scale, gate, norm_g, hgrn_onorm_g, rel_bias, lb[None, :],
        final_g[None, :], win_f, wa_f, wb_f, wo_f, hook=start_reduction)

    rec = _reduce_chips_wait(*flight["sems"], flight["thru"], pack)
    pieces = [_final_piece_call(t, flight["sums"][t], rec[t], sc_idx, "piece_" + names[t]) for t in range(4)]
    g_win, g_wa, g_wb, g_wo = _share_sibling(pieces)

    packs = _allgather_small(pack, "gather_small").reshape(8, 16, D)
    tot = _sum8_call(packs)
    loss = tot[7, 0]
    g_b_ada = tot[0:3].reshape(1, 3 * D)
    g_norm_g = tot[3:4]
    g_final_g = tot[4]
    (g_hgrn_lb,) = lb_vjp(tot[5])
    g_onorm = tot[6:7, :HK]
    g_rel = tot[8, :NBUCKETS * NH].reshape(NBUCKETS, NH)

    def rows_of(a):
        flat = a.reshape(-1)
        n = -(-flat.shape[0] // D)
        return jnp.pad(flat, (0, n * D - flat.shape[0])).reshape(n, D)

    smalls = [(b_ada, g_b_ada, m_b_ada, v_b_ada), (norm_g, g_norm_g, m_norm_g, v_norm_g),
              (hgrn_onorm_g, g_onorm, m_hgrn_onorm_g, v_hgrn_onorm_g), (rel_bias, g_rel, m_rel_bias, v_rel_bias),
              (hgrn_lb, g_hgrn_lb, m_hgrn_lb, v_hgrn_lb), (final_g, g_final_g, m_final_g, v_final_g)]
    cat = [jnp.concatenate([rows_of(s[k]) for s in smalls], 0) for k in range(4)]
    cat = [jnp.pad(a, ((0, 16 - a.shape[0]), (0, 0))) for a in cat]
    sd, sm, sv = _adamw_call(*cat, "adamw_small")

    def unpack(packed):
        res, r = [], 0
        for s in smalls:
            n = -(-s[0].size // D)
            res.append(packed[r:r + n].reshape(-1)[:s[0].size].reshape(s[0].shape))
            r += n
        return res

    d_small, m_small, v_small = unpack(sd), unpack(sm), unpack(sv)

    sc_all = c_all * jax.nn.sigmoid(c_all)
    dmod_all = packs[:, 0:3].reshape(8, 3 * D)
    dm_s = lax.dynamic_slice(dmod_all, (0, chip * 768), (8, 768))
    g_w_ada, d_w_ada, nm_w_ada, nv_w_ada = _ada_update_call(sc_all.T, dm_s, w_ada[0], m_w_ada[0], v_w_ada[0])

    big = []
    for w, g, m, v, n in [(w_in, g_win, m_w_in, v_w_in, "w_in"), (w_branch_a, g_wa, m_w_branch_a, v_w_branch_a, "w_a"),
                          (w_branch_b, g_wb, m_w_branch_b, v_w_branch_b, "w_b"), (w_out, g_wo, m_w_out, v_w_out, "w_o")]:
        big.append(_adamw_call(w[0], g, m[0], v[0], "adamw_" + n))

    e = lambda a: a[None]
    grads = [e(g_w_ada), g_b_ada, g_norm_g, e(g_win), g_onorm, e(g_wa), e(g_wb), e(g_wo), g_rel, g_hgrn_lb, g_final_g]
    deltas = [e(d_w_ada), d_small[0], d_small[1], e(big[0][0]), d_small[2], e(big[1][0]), e(big[2][0]), e(big[3][0]),
              d_small[3], d_small[4], d_small[5]]
    new_m = [e(nm_w_ada), m_small[0], m_small[1], e(big[0][1]), m_small[2], e(big[1][1]), e(big[2][1]), e(big[3][1]),
             m_small[3], m_small[4], m_small[5]]
    new_v = [e(nv_w_ada), v_small[0], v_small[1], e(big[0][2]), v_small[2], e(big[1][2]), e(big[2][2]), e(big[3][2]),
             v_small[3], v_small[4], v_small[5]]
    return (loss, grad_x[None], *grads, *deltas, *new_m, *new_v)
```

```python
import functools
import math

import numpy as np
import jax
import jax.numpy as jnp
from jax import lax
from jax.experimental import pallas as pl
from jax.experimental.pallas import tpu as pltpu

D = 1024
AW = 512
NH = 8
HE = 64
HK = 128
NPROJ = 8192
ABLK = 128
PATTERNS = (1, 4, 16)
NBUCKETS = 32
MAXDIST = 2048
NEG = -1e30
EPS = 1e-6
CH = 64
LR, B1, B2, AEPS, WD, STEP = 0.001, 0.9, 0.999, 1e-08, 0.01, 10

F32 = jnp.float32
BF16 = jnp.bfloat16
MESH = pl.DeviceIdType.MESH
VMEM_LIMIT = 56 * 1024 * 1024


def _cp(**kw):
    return pltpu.CompilerParams(vmem_limit_bytes=VMEM_LIMIT, **kw)


def _sig(x):
    return 0.5 * jnp.tanh(0.5 * x) + 0.5


def _nt(a, b):
    return lax.dot_general(a, b, (((1,), (1,)), ((), ())), preferred_element_type=F32)


def _tn(a, b):
    return lax.dot_general(a, b, (((0,), (0,)), ((), ())), preferred_element_type=F32)


def _nn(a, b):
    return jnp.dot(a, b, preferred_element_type=F32)


def _split3(x):
    h = x.astype(BF16)
    r = x - h.astype(F32)
    m = r.astype(BF16)
    l = (r - m.astype(F32)).astype(BF16)
    return h, m, l


def _exact_mm(tri_bf16, x):
    h, m, l = _split3(x)
    return _nn(tri_bf16, h) + _nn(tri_bf16, m) + _nn(tri_bf16, l)


def _exact_mm_r(x, ones_bf16):
    h, m, l = _split3(x)
    return _nn(h, ones_bf16) + _nn(m, ones_bf16) + _nn(l, ones_bf16)


def _h_call(x, avec):
    S = x.shape[0]
    tm = 512

    def body(x_ref, a_ref, h_ref):
        xv = x_ref[...]
        r = lax.rsqrt(jnp.mean(xv * xv, axis=-1, keepdims=True) + EPS)
        h_ref[...] = (xv * r * a_ref[0:1, :] + a_ref[1:2, :]).astype(BF16)

    return pl.pallas_call(
        body, name="h_norm", grid=(S // tm,),
        in_specs=[pl.BlockSpec((tm, D), lambda i: (i, 0)), pl.BlockSpec((8, D), lambda i: (0, 0))],
        out_specs=pl.BlockSpec((tm, D), lambda i: (i, 0)),
        out_shape=jax.ShapeDtypeStruct((S, D), BF16), compiler_params=_cp(),
    )(x, avec)


def _proj_call(h, w_in):
    S = h.shape[0]
    tm, tn = 512, 2048

    def body(h_ref, w_ref, o_ref):
        o_ref[...] = _nn(h_ref[...], w_ref[...])

    return pl.pallas_call(
        body, name="in_proj", grid=(NPROJ // tn, S // tm),
        in_specs=[pl.BlockSpec((tm, D), lambda j, i: (i, 0)), pl.BlockSpec((D, tn), lambda j, i: (0, j))],
        out_specs=pl.BlockSpec((tm, tn), lambda j, i: (i, j)),
        out_shape=jax.ShapeDtypeStruct((S, NPROJ), F32), compiler_params=_cp(),
    )(h, w_in)


def _proj_own_call(h, w_own, sc):
    S = h.shape[0]
    tm, tn = 512, 2048

    def body(s_ref, h_ref, w_ref, o_ref):
        o_ref[...] = _nn(h_ref[...], w_ref[...])

    return pl.pallas_call(
        body, name="in_proj_own",
        grid_spec=pltpu.PrefetchScalarGridSpec(
            num_scalar_prefetch=1, grid=(S // tm,),
            in_specs=[pl.BlockSpec((tm, D), lambda i, s: (i, 0)), pl.BlockSpec((D, tn), lambda i, s: (0, 0))],
            out_specs=pl.BlockSpec((tm, tn), lambda i, s: (i, s[1]))),
        out_shape=jax.ShapeDtypeStruct((S, NPROJ), F32), compiler_params=_cp(),
    )(sc, h, w_own)


def _proj_rest_call(h, w_in, proj, sc):
    S = h.shape[0]
    tm, tn = 512, 2048

    def body(s_ref, h_ref, w_ref, p_ref, o_ref):
        o_ref[...] = _nn(h_ref[...], w_ref[...])

    col = lambda j, s: (s[1] + 1 + j) % 4
    return pl.pallas_call(
        body, name="in_proj_rest",
        grid_spec=pltpu.PrefetchScalarGridSpec(
            num_scalar_prefetch=1, grid=(3, S // tm),
            in_specs=[pl.BlockSpec((tm, D), lambda j, i, s: (i, 0)),
                      pl.BlockSpec((D, tn), lambda j, i, s: (0, col(j, s))),
                      pl.BlockSpec(memory_space=pl.ANY)],
            out_specs=pl.BlockSpec((tm, tn), lambda j, i, s: (i, col(j, s)))),
        out_shape=jax.ShapeDtypeStruct((S, NPROJ), F32), input_output_aliases={3: 0}, compiler_params=_cp(),
    )(sc, h, w_in, proj)


def _t5_bucket_np(dist):
    max_exact = NBUCKETS // 2
    n = dist.astype(np.float32)
    large = max_exact + (np.log(np.maximum(n, np.float32(1.0)) / np.float32(max_exact))
                         / np.float32(math.log(MAXDIST / max_exact))
                         * np.float32(NBUCKETS - max_exact)).astype(np.int32)
    large = np.minimum(large, NBUCKETS - 1)
    return np.where(dist < max_exact, dist, large)


def _band_bucket(d):
    qi = np.arange(ABLK)[:, None]
    kj = np.arange(2 * ABLK)[None, :]
    delta = qi + ABLK - kj
    band = (delta >= 0) & (delta <= ABLK)
    bucket = _t5_bucket_np(np.clip(delta, 0, None) * d)
    return band, bucket


def _bias_tiles(rel_bias):
    tiles = []
    for d in PATTERNS:
        band, bucket = _band_bucket(d)
        onehot = (jnp.asarray(bucket, jnp.int32)[None] == jnp.arange(NBUCKETS, dtype=jnp.int32)[:, None, None])
        bias = jnp.einsum("bqk,bh->hqk", onehot.astype(F32), rel_bias, precision=lax.Precision.HIGHEST)
        tiles.append(jnp.where(jnp.asarray(band)[None], bias, NEG))
    return jnp.stack(tiles, 0)


ATT = 2048
HP = 2 * HE
AGRP = 4
AGRP_B = 2


def _attn_blocks():
    out = []
    for p, d in enumerate(PATTERNS):
        for r in range(d):
            for n in range(ATT // (d * ABLK)):
                out.append((p, d, r, n))
    return out


def _attn_fwd_call(proj, biases):
    S = proj.shape[0]
    nt = S // ATT

    def body(q_ref, k_ref, v_ref, z_ref, b_ref, a_ref, l_ref, oa_ref, kc, vc, op, lp):
        i = pl.program_id(1)

        @pl.when(i == 0)
        def _():
            kc[0:ATT] = jnp.zeros((ATT, HP), F32)
            vc[0:ATT] = jnp.zeros((ATT, HP), F32)

        @pl.when(i > 0)
        def _():
            kc[0:ATT] = kc[ATT:2 * ATT]
            vc[0:ATT] = vc[ATT:2 * ATT]

        kc[ATT:2 * ATT] = k_ref[...]
        vc[ATT:2 * ATT] = v_ref[...]
        col = lax.broadcasted_iota(jnp.int32, (ABLK, 2 * ABLK), 1)
        dead = jnp.logical_and(i == 0, col < ABLK)
        blocks = _attn_blocks()
        hs = (slice(0, HE), slice(HE, 2 * HE))
        for g0 in range(0, len(blocks), AGRP):
            grp = blocks[g0:g0 + AGRP]
            qrows = [pl.ds(n * ABLK * d + r, ABLK, stride=d) for p, d, r, n in grp]
            krows = [pl.ds(ATT + (n - 1) * ABLK * d + r, 2 * ABLK, stride=d) for p, d, r, n in grp]
            qs = [(q_ref[qr, :] * (HE ** -0.5)).astype(BF16) for qr in qrows]
            ks = [kc[kr, :].astype(BF16) for kr in krows]
            vs = [vc[kr, :].astype(BF16) for kr in krows]
            ss = [[_nt(qs[b][:, sl], ks[b][:, sl]) + b_ref[grp[b][0], e] for e, sl in enumerate(hs)]
                  for b in range(len(grp))]
            ss = [[jnp.where(dead, NEG, s) if grp[b][3] == 0 else s for s in ss[b]] for b in range(len(grp))]
            mxs = [[jnp.max(s, axis=-1, keepdims=True) for s in sb] for sb in ss]
            pes = [[jnp.exp(s - mx) for s, mx in zip(sb, mb)] for sb, mb in zip(ss, mxs)]
            dens = [[jnp.sum(pe, axis=-1, keepdims=True) for pe in pb] for pb in pes]
            pvs = [[_nn(pe.astype(BF16), vs[b][:, sl]) for pe, sl in zip(pes[b], hs)] for b in range(len(grp))]
            for b in range(len(grp)):
                p, d, r, n = grp[b]
                prow = pl.ds(p * ATT + n * ABLK * d + r, ABLK, stride=d)
                lp[prow, :] = jnp.concatenate(
                    [jnp.broadcast_to(mx + jnp.log(dn), (ABLK, HE)) for mx, dn in zip(mxs[b], dens[b])], axis=1)
                op[prow, :] = jnp.concatenate([pv / dn for pv, dn in zip(pvs[b], dens[b])], axis=1)
        rt = 256
        for t in range(ATT // rt):
            rows = slice(t * rt, (t + 1) * rt)
            pr = [slice(p * ATT + t * rt, p * ATT + (t + 1) * rt) for p in range(3)]
            la, lb_, lc = lp[pr[0], :], lp[pr[1], :], lp[pr[2], :]
            m = jnp.maximum(jnp.maximum(la, lb_), lc)
            ea, eb, ec = jnp.exp(la - m), jnp.exp(lb_ - m), jnp.exp(lc - m)
            den = ea + eb + ec
            att = (ea * op[pr[0], :] + eb * op[pr[1], :] + ec * op[pr[2], :]) / den
            a_ref[rows, :] = att
            l_ref[rows, :] = m + jnp.log(den)
            z = z_ref[rows, :]
            oa_ref[rows, :] = (att * (z * _sig(z))).astype(BF16)

    def pcol(c):
        return pl.BlockSpec((ATT, HP), lambda h, i: (i, c * 4 + h))

    out = pl.BlockSpec((ATT, HP), lambda h, i: (i, h))
    return pl.pallas_call(
        body, name="attn_fwd", grid=(4, nt),
        in_specs=[pcol(0), pcol(1), pcol(2), pcol(3),
                  pl.BlockSpec((3, 2, ABLK, 2 * ABLK), lambda h, i: (0, h, 0, 0))],
        out_specs=[out, out, out],
        out_shape=[jax.ShapeDtypeStruct((S, AW), F32), jax.ShapeDtypeStruct((S, AW), F32),
                   jax.ShapeDtypeStruct((S, AW), BF16)],
        scratch_shapes=[pltpu.VMEM((2 * ATT, HP), F32), pltpu.VMEM((2 * ATT, HP), F32),
                        pltpu.VMEM((3 * ATT, HP), F32), pltpu.VMEM((3 * ATT, HP), F32)],
        compiler_params=_cp(),
    )(proj, proj, proj, proj, biases)


def _attn_bwd_call(proj, dattn, lse, dsum, biases):
    S = proj.shape[0]
    nt = S // ATT

    def body(q_ref, k_ref, v_ref, do_ref, l_ref, ds_ref, b_ref, dq_ref, dk_ref, dv_ref, db_ref,
             kc, vc, dkc, dvc, dqa):
        i = pl.program_id(1)

        @pl.when(i == 0)
        def _():
            kc[ATT:2 * ATT] = jnp.zeros((ATT, HP), F32)
            vc[ATT:2 * ATT] = jnp.zeros((ATT, HP), F32)
            dkc[ATT:2 * ATT] = jnp.zeros((ATT, HP), F32)
            dvc[ATT:2 * ATT] = jnp.zeros((ATT, HP), F32)
            db_ref[...] = jnp.zeros_like(db_ref)

        @pl.when(i < nt)
        def _():
            kc[0:ATT] = kc[ATT:2 * ATT]
            vc[0:ATT] = vc[ATT:2 * ATT]
            dkc[0:ATT] = dkc[ATT:2 * ATT]
            dvc[0:ATT] = dvc[ATT:2 * ATT]
            kc[ATT:2 * ATT] = k_ref[...]
            vc[ATT:2 * ATT] = v_ref[...]
            dkc[ATT:2 * ATT] = jnp.zeros((ATT, HP), F32)
            dvc[ATT:2 * ATT] = jnp.zeros((ATT, HP), F32)
            col = lax.broadcasted_iota(jnp.int32, (ABLK, 2 * ABLK), 1)
            dead = jnp.logical_and(i == 0, col < ABLK)
            blocks = _attn_blocks()
            hs = (slice(0, HE), slice(HE, 2 * HE))
            for g0 in range(0, len(blocks), AGRP_B):
                grp = blocks[g0:g0 + AGRP_B]
                nb_ = range(len(grp))
                qrows = [pl.ds(n * ABLK * d + r, ABLK, stride=d) for p, d, r, n in grp]
                krows = [pl.ds(ATT + (n - 1) * ABLK * d + r, 2 * ABLK, stride=d) for p, d, r, n in grp]
                qs = [(q_ref[qr, :] * (HE ** -0.5)).astype(BF16) for qr in qrows]
                ks = [kc[kr, :].astype(BF16) for kr in krows]
                vs = [vc[kr, :].astype(BF16) for kr in krows]
                dos = [do_ref[qr, :].astype(BF16) for qr in qrows]
                lvs = [l_ref[qr, :] for qr in qrows]
                dsvs = [ds_ref[qr, :] for qr in qrows]
                ss = [[_nt(qs[b][:, sl], ks[b][:, sl]) + b_ref[grp[b][0], e] for e, sl in enumerate(hs)] for b in nb_]
                ss = [[jnp.where(dead, NEG, s) if grp[b][3] == 0 else s for s in ss[b]] for b in nb_]
                dps = [[_nt(dos[b][:, sl], vs[b][:, sl]) for sl in hs] for b in nb_]
                pes = [[jnp.exp(ss[b][e] - lvs[b][:, e * HE:e * HE + 1]) for e in range(2)] for b in nb_]
                dscs = [[pes[b][e] * (dps[b][e] - dsvs[b][:, e * HE:e * HE + 1]) for e in range(2)] for b in nb_]
                for b in nb_:
                    for e in range(2):
                        db_ref[grp[b][0], e] += dscs[b][e]
                dsbs = [[t.astype(BF16) for t in tb] for tb in dscs]
                dqs = [[_nn(dsbs[b][e], ks[b][:, sl]) * (HE ** -0.5) for e, sl in enumerate(hs)] for b in nb_]
                dks = [[_tn(dsbs[b][e], qs[b][:, sl]) for e, sl in enumerate(hs)] for b in nb_]
                dvs = [[_tn(pes[b][e].astype(BF16), dos[b][:, sl]) for e, sl in enumerate(hs)] for b in nb_]
                for b in nb_:
                    dq = jnp.concatenate(dqs[b], axis=1)
                    if grp[b][0] == 0:
                        dqa[qrows[b], :] = dq
                    else:
                        dqa[qrows[b], :] += dq
                    dkc[krows[b], :] += jnp.concatenate(dks[b], axis=1)
                    dvc[krows[b], :] += jnp.concatenate(dvs[b], axis=1)
            dq_ref[...] = dqa[...].astype(BF16)
            dk_ref[...] = dkc[0:ATT].astype(BF16)
            dv_ref[...] = dvc[0:ATT].astype(BF16)

        @pl.when(i == nt)
        def _():
            dk_ref[...] = dkc[ATT:2 * ATT].astype(BF16)
            dv_ref[...] = dvc[ATT:2 * ATT].astype(BF16)

    def pcol(c):
        return pl.BlockSpec((ATT, HP), lambda h, i: (jnp.minimum(i, nt - 1), c * 4 + h))

    qrow = pl.BlockSpec((ATT, HP), lambda h, i: (jnp.minimum(i, nt - 1), h))
    krow = pl.BlockSpec((ATT, HP), lambda h, i: (jnp.maximum(i - 1, 0), h))
    bspec = pl.BlockSpec((3, 2, ABLK, 2 * ABLK), lambda h, i: (0, h, 0, 0))
    return pl.pallas_call(
        body, name="attn_bwd", grid=(4, nt + 1),
        in_specs=[pcol(0), pcol(1), pcol(2), qrow, qrow, qrow, bspec],
        out_specs=[qrow, krow, krow, bspec],
        out_shape=[jax.ShapeDtypeStruct((S, AW), BF16)] * 3
                  + [jax.ShapeDtypeStruct((3, NH, ABLK, 2 * ABLK), F32)],
        scratch_shapes=[pltpu.VMEM((2 * ATT, HP), F32)] * 4 + [pltpu.VMEM((ATT, HP), F32)],
        compiler_params=_cp(),
    )(proj, proj, proj, dattn, lse, dsum, biases)


HRB = 256


def _hgrn_gates(q_ref, f_ref, rows, lbv, tri):
    qraw = q_ref[rows, :]
    sq = _sig(qraw)
    q = qraw * sq
    sf = _sig(f_ref[rows, :])
    f = lbv + (1.0 - lbv) * sf
    k = 1.0 - f
    b = _exact_mm(tri, jnp.log(f))
    bl = b[CH - 1:CH, :]
    bm = b[CH // 2 - 1:CH // 2, :]
    e1 = jnp.exp(b - bm)
    e2 = jnp.exp(bm - b)
    ebm = jnp.exp(bm)
    eblm = jnp.exp(bl - bm)
    qs = q * e1
    ks = k * e2
    qe = qs * ebm
    kd = ks * eblm
    return dict(qraw=qraw, sq=sq, sf=sf, f=f, bl=bl, e1=e1, e2=e2, ebm=ebm, eblm=eblm, qe=qe, qs=qs, ks=ks, kd=kd)


def _tri_masks():
    row = lax.broadcasted_iota(jnp.int32, (CH, CH), 0)
    col = lax.broadcasted_iota(jnp.int32, (CH, CH), 1)
    return row >= col


def _hgrn_fwd_call(proj, lb, gn):
    S = proj.shape[0]
    nc = S // CH
    cps = HRB // CH

    def body(q_ref, f_ref, i_ref, z_ref, lb_ref, gn_ref, or_ref, ob_ref, st_ref, st):
        @pl.when(pl.program_id(0) == 0)
        def _():
            st[...] = jnp.zeros_like(st)

        low = _tri_masks()
        tri = low.astype(BF16)
        lbv = lb_ref[...]
        for ci in range(cps):
            rows = slice(ci * CH, (ci + 1) * CH)
            g = _hgrn_gates(q_ref, f_ref, rows, lbv, tri)
            v = i_ref[rows, :]
            ebl = jnp.exp(g["bl"])
            st_ref[ci] = st[...]
            hs = [slice(hh * HK, (hh + 1) * HK) for hh in range(NH)]
            vb = v.astype(BF16)
            qsb, ksb, qeb, kdb = (g[n_].astype(BF16) for n_ in ("qs", "ks", "qe", "kd"))
            s0s = [st[sl, :] for sl in hs]
            as_ = [_nt(qsb[:, sl], ksb[:, sl]) for sl in hs]
            ois = [_nt(qeb[:, sl], s0.astype(BF16)) for sl, s0 in zip(hs, s0s)]
            sts = [_tn(vb[:, sl], kdb[:, sl]) for sl in hs]
            abs_ = [jnp.where(low, a, 0.0).astype(BF16) for a in as_]
            os_ = [oi + _nn(a, vb[:, sl]) for oi, a, sl in zip(ois, abs_, hs)]
            for sl, s0, sn in zip(hs, s0s, sts):
                st[sl, :] = s0 * ebl[:, sl] + sn
            o = jnp.concatenate(os_, axis=1)
            or_ref[rows, :] = o
            rs = [lax.rsqrt(jnp.mean(oh * oh, axis=-1, keepdims=True) + EPS) for oh in os_]
            on = jnp.concatenate([oh * r for oh, r in zip(os_, rs)], axis=1)
            z = z_ref[rows, :]
            ob_ref[rows, :] = (on * gn_ref[...] * (z * _sig(z))).astype(BF16)

    def pcol(c):
        return pl.BlockSpec((HRB, D), lambda i: (i, c))

    vec = pl.BlockSpec((1, D), lambda i: (0, 0))
    row = pl.BlockSpec((HRB, D), lambda i: (i, 0))
    return pl.pallas_call(
        body, name="hgrn_fwd", grid=(S // HRB,),
        in_specs=[pcol(2), pcol(3), pcol(4), pcol(5), vec, vec],
        out_specs=[row, row, pl.BlockSpec((cps, NH * HK, HK), lambda i: (i, 0, 0))],
        out_shape=[jax.ShapeDtypeStruct((S, D), F32), jax.ShapeDtypeStruct((S, D), BF16),
                   jax.ShapeDtypeStruct((nc, NH * HK, HK), F32)],
        scratch_shapes=[pltpu.VMEM((NH * HK, HK), F32)],
        compiler_params=_cp(),
    )(proj, proj, proj, proj, lb, gn)


def _hgrn_bwd_call(proj, oraw, dob, states, lb, gn):
    S = proj.shape[0]
    nblk = S // HRB
    cps = HRB // CH

    def body(q_ref, f_ref, i_ref, z_ref, or_ref, dob_ref, st_ref, lb_ref, gn_ref, dh_ref, acc_ref, dst):
        @pl.when(pl.program_id(0) == 0)
        def _():
            dst[...] = jnp.zeros_like(dst)
            acc_ref[...] = jnp.zeros_like(acc_ref)

        low = _tri_masks()
        tri = low.astype(BF16)
        triu = jnp.logical_not(_tri_masks()) | (lax.broadcasted_iota(jnp.int32, (CH, CH), 0)
                                               == lax.broadcasted_iota(jnp.int32, (CH, CH), 1))
        triu = triu.astype(BF16)
        lbv = lb_ref[...]
        for ci in reversed(range(cps)):
            rows = slice(ci * CH, (ci + 1) * CH)
            g = _hgrn_gates(q_ref, f_ref, rows, lbv, tri)
            v = i_ref[rows, :]
            ebl = jnp.exp(g["bl"])
            hs = [slice(hh * HK, (hh + 1) * HK) for hh in range(NH)]
            cat = lambda lst: jnp.concatenate(lst, axis=1)
            o = or_ref[rows, :]
            z = z_ref[rows, :]
            sz = _sig(z)
            gnv = gn_ref[...]
            dobv = dob_ref[rows, :]
            r = cat([jnp.broadcast_to(lax.rsqrt(jnp.mean(o[:, sl] * o[:, sl], axis=-1, keepdims=True) + EPS),
                                      (CH, HK)) for sl in hs])
            don = dobv * (z * sz)
            dz = dobv * (o * r * gnv) * (sz * (1.0 + z * (1.0 - sz)))
            dgn = jnp.sum(don * o * r, axis=0, keepdims=True)
            gh = don * gnv
            gho = gh * o
            mg = cat([jnp.broadcast_to(jnp.mean(gho[:, sl], axis=-1, keepdims=True), (CH, HK)) for sl in hs])
            dob16 = (r * gh - o * (r * r * r * mg)).astype(BF16)

            vb = v.astype(BF16)
            qsb, ksb, qeb, kdb = (g[n_].astype(BF16) for n_ in ("qs", "ks", "qe", "kd"))
            st0s = [st_ref[ci, sl, :] for sl in hs]
            dst1s = [dst[sl, :] for sl in hs]
            dst1bs = [t.astype(BF16) for t in dst1s]
            as_ = [_nt(qsb[:, sl], ksb[:, sl]) for sl in hs]
            das_ = [_nt(dob16[:, sl], vb[:, sl]) for sl in hs]
            dqes = [_nn(dob16[:, sl], s0.astype(BF16)) for sl, s0 in zip(hs, st0s)]
            dkds = [_nn(vb[:, sl], d1) for sl, d1 in zip(hs, dst1bs)]
            dvis = [_nt(kdb[:, sl], d1) for sl, d1 in zip(hs, dst1bs)]
            dsts = [_tn(dob16[:, sl], qeb[:, sl]) for sl in hs]
            abs_ = [jnp.where(low, a, 0.0).astype(BF16) for a in as_]
            dabs_ = [jnp.where(low, a, 0.0).astype(BF16) for a in das_]
            dqss = [_nn(da, ksb[:, sl]) for da, sl in zip(dabs_, hs)]
            dkss = [_tn(da, qsb[:, sl]) for da, sl in zip(dabs_, hs)]
            dvs_ = [_tn(a, dob16[:, sl]) + dvi for a, sl, dvi in zip(abs_, hs, dvis)]
            exs_ = [jnp.sum(d1 * s0, axis=0, keepdims=True) for d1, s0 in zip(dst1s, st0s)]
            for sl, d1, dn in zip(hs, dst1s, dsts):
                dst[sl, :] = dn + d1 * ebl[:, sl]
            dqe, dqs, dks, dkd, dv = cat(dqes), cat(dqss), cat(dkss), cat(dkds), cat(dvs_)
            dq = (dqe * g["ebm"] + dqs) * g["e1"]
            dk = (dks + dkd * g["eblm"]) * g["e2"]
            dkdkd = dkd * g["kd"]
            db = dqe * g["qe"] + dqs * qsb.astype(F32) - dks * ksb.astype(F32) - dkdkd
            ex = jnp.sum(dkdkd, axis=0, keepdims=True) + cat(exs_) * ebl
            dg = _exact_mm(triu, db) + ex
            df = dg / g["f"] - dk
            sf = g["sf"]
            dfr = df * (1.0 - lbv) * sf * (1.0 - sf)
            sq = g["sq"]
            dqr = dq * (sq * (1.0 + g["qraw"] * (1.0 - sq)))
            acc_ref[0:1, :] += jnp.sum(df * (1.0 - sf), axis=0, keepdims=True)
            acc_ref[1:2, :] += dgn
            dh_ref[rows, 0:D] = dqr.astype(BF16)
            dh_ref[rows, D:2 * D] = dfr.astype(BF16)
            dh_ref[rows, 2 * D:3 * D] = dv.astype(BF16)
            dh_ref[rows, 3 * D:4 * D] = dz.astype(BF16)

    def pcol(c):
        return pl.BlockSpec((HRB, D), lambda i: (nblk - 1 - i, c))

    vec = pl.BlockSpec((1, D), lambda i: (0, 0))
    row = pl.BlockSpec((HRB, D), lambda i: (nblk - 1 - i, 0))
    return pl.pallas_call(
        body, name="hgrn_bwd", grid=(nblk,),
        in_specs=[pcol(2), pcol(3), pcol(4), pcol(5), row, row,
                  pl.BlockSpec((cps, NH * HK, HK), lambda i: (nblk - 1 - i, 0, 0)), vec, vec],
        out_specs=[pl.BlockSpec((HRB, 4 * D), lambda i: (nblk - 1 - i, 0)),
                   pl.BlockSpec((8, D), lambda i: (0, 0))],
        out_shape=[jax.ShapeDtypeStruct((S, 4 * D), BF16), jax.ShapeDtypeStruct((8, D), F32)],
        scratch_shapes=[pltpu.VMEM((NH * HK, HK), F32)],
        compiler_params=_cp(),
    )(proj, proj, proj, proj, oraw, dob, states, lb, gn)


def _fwd2_call(oa, ob, proj, x, tgt, vecs, wa, wb, wo):
    S = x.shape[0]
    tm = 256

    def body(oa_ref, ob_ref, ga_ref, gb_ref, x_ref, t_ref, v_ref, wa_ref, wb_ref, wo_ref,
             ya_ref, yb_ref, y_ref, u_ref, x2_ref, ls_ref):
        @pl.when(pl.program_id(0) == 0)
        def _():
            ls_ref[...] = jnp.zeros_like(ls_ref)

        ya = _nn(oa_ref[...], wa_ref[...])
        yb = _nn(ob_ref[...], wb_ref[...])
        y = _sig(ga_ref[...]) * ya + _sig(gb_ref[...]) * yb
        u = _nn(y.astype(BF16), wo_ref[...])
        x2 = x_ref[...] + v_ref[0:1, :] * u
        r = lax.rsqrt(jnp.mean(x2 * x2, axis=-1, keepdims=True) + EPS)
        err = x2 * r * v_ref[1:2, :] - t_ref[...]
        ls_ref[...] += jnp.sum(err * err)
        ya_ref[...] = ya.astype(BF16)
        yb_ref[...] = yb.astype(BF16)
        y_ref[...] = y.astype(BF16)
        u_ref[...] = u.astype(BF16)
        x2_ref[...] = x2

    row = pl.BlockSpec((tm, D), lambda i: (i, 0))
    full = lambda a: pl.BlockSpec(a.shape, lambda i: (0, 0))
    return pl.pallas_call(
        body, name="fwd_merge_out", grid=(S // tm,),
        in_specs=[pl.BlockSpec((tm, AW), lambda i: (i, 0)), row,
                  pl.BlockSpec((tm, D), lambda i: (i, 6)), pl.BlockSpec((tm, D), lambda i: (i, 7)),
                  row, row, pl.BlockSpec((8, D), lambda i: (0, 0)), full(wa), full(wb), full(wo)],
        out_specs=[row, row, row, row, row, pl.BlockSpec((8, 128), lambda i: (0, 0))],
        out_shape=[jax.ShapeDtypeStruct((S, D), BF16)] * 4
                  + [jax.ShapeDtypeStruct((S, D), F32), jax.ShapeDtypeStruct((8, 128), F32)],
        compiler_params=_cp(),
    )(oa, ob, proj, proj, x, tgt, vecs, wa, wb, wo)


def _bwd2_call(x2, tgt, vecs, ya, yb, u, proj, attn, wa, wb, wo, hsum):
    S = x2.shape[0]
    tm = 256

    def body(x2_ref, t_ref, v_ref, ya_ref, yb_ref, u_ref, ga_ref, gb_ref, at_ref, za_ref,
             wa_ref, wb_ref, wo_ref, hs_ref,
             dx2_ref, du_ref, dya_ref, dyb_ref, dg_ref, dat_ref, dsum_ref, dza_ref, dob_ref, acc_ref):
        @pl.when(pl.program_id(0) == 0)
        def _():
            acc_ref[...] = jnp.zeros_like(acc_ref)

        x2v = x2_ref[...]
        gate = v_ref[0:1, :]
        fg = v_ref[1:2, :]
        r = lax.rsqrt(jnp.mean(x2v * x2v, axis=-1, keepdims=True) + EPS)
        dout = (x2v * r * fg - t_ref[...]) * (1.0 / D)
        gh = dout * fg
        dx2 = r * gh - x2v * (r * r * r * jnp.mean(gh * x2v, axis=-1, keepdims=True))
        acc_ref[0:1, :] += jnp.sum(dx2 * u_ref[...].astype(F32), axis=0, keepdims=True)
        acc_ref[1:2, :] += jnp.sum(dout * x2v * r, axis=0, keepdims=True)
        dx2_ref[...] = dx2
        du = (dx2 * gate).astype(BF16)
        du_ref[...] = du
        dy = _nt(du, wo_ref[...])
        sa = _sig(ga_ref[...])
        sb = _sig(gb_ref[...])
        dya = (dy * sa).astype(BF16)
        dyb = (dy * sb).astype(BF16)
        dya_ref[...] = dya
        dyb_ref[...] = dyb
        dg_ref[:, 0:D] = (dy * ya_ref[...].astype(F32) * sa * (1.0 - sa)).astype(BF16)
        dg_ref[:, D:2 * D] = (dy * yb_ref[...].astype(F32) * sb * (1.0 - sb)).astype(BF16)
        doa = _nt(dya, wa_ref[...])
        dob_ref[...] = _nt(dyb, wb_ref[...])
        za = za_ref[...]
        sz = _sig(za)
        att = at_ref[...]
        dat = doa * (za * sz)
        dat_ref[...] = dat
        dza_ref[...] = (doa * att * (sz * (1.0 + za * (1.0 - sz)))).astype(BF16)
        dsum_ref[...] = _exact_mm_r(dat * att, hs_ref[...])

    row = pl.BlockSpec((tm, D), lambda i: (i, 0))
    arow = pl.BlockSpec((tm, AW), lambda i: (i, 0))
    full = lambda a: pl.BlockSpec(a.shape, lambda i: (0, 0))
    return pl.pallas_call(
        body, name="bwd_merge_out", grid=(S // tm,),
        in_specs=[row, row, pl.BlockSpec((8, D), lambda i: (0, 0)), row, row, row,
                  pl.BlockSpec((tm, D), lambda i: (i, 6)), pl.BlockSpec((tm, D), lambda i: (i, 7)),
                  arow, pl.BlockSpec((tm, AW), lambda i: (i, 3)), full(wa), full(wb), full(wo), full(hsum)],
        out_specs=[row, row, row, row, pl.BlockSpec((tm, 2 * D), lambda i: (i, 0)),
                   arow, arow, arow, row, pl.BlockSpec((8, D), lambda i: (0, 0))],
        out_shape=[jax.ShapeDtypeStruct((S, D), F32), jax.ShapeDtypeStruct((S, D), BF16),
                   jax.ShapeDtypeStruct((S, D), BF16), jax.ShapeDtypeStruct((S, D), BF16),
                   jax.ShapeDtypeStruct((S, 2 * D), BF16), jax.ShapeDtypeStruct((S, AW), F32),
                   jax.ShapeDtypeStruct((S, AW), F32), jax.ShapeDtypeStruct((S, AW), BF16),
                   jax.ShapeDtypeStruct((S, D), F32), jax.ShapeDtypeStruct((8, D), F32)],
        compiler_params=_cp(),
    )(x2, tgt, vecs, ya, yb, u, proj, proj, attn, proj, wa, wb, wo, hsum)


def _atb_call(a, b, name):
    S, K = a.shape
    N = b.shape[1]
    tm = 512

    def body(a_ref, b_ref, o_ref, ob_ref):
        @pl.when(pl.program_id(0) == 0)
        def _():
            o_ref[...] = jnp.zeros_like(o_ref)

        o_ref[...] += _tn(a_ref[...], b_ref[...])

        @pl.when(pl.program_id(0) == S // tm - 1)
        def _():
            ob_ref[...] = o_ref[...].astype(BF16)

    ospec = pl.BlockSpec((K, N), lambda i: (0, 0))
    return pl.pallas_call(
        body, name=name, grid=(S // tm,),
        in_specs=[pl.BlockSpec((tm, K), lambda i: (i, 0)), pl.BlockSpec((tm, N), lambda i: (i, 0))],
        out_specs=[ospec, ospec],
        out_shape=[jax.ShapeDtypeStruct((K, N), F32), jax.ShapeDtypeStruct((K, N), BF16)], compiler_params=_cp(),
    )(a, b)


def _dwin_call(h, dqkvz, d_hgrn, d_gates):
    S = h.shape[0]
    tm = 512
    tn = 2048

    def body(h_ref, q_ref, k_ref, v_ref, z_ref, m_ref, g_ref, o_ref, ob_ref):
        j = pl.program_id(0)

        @pl.when(pl.program_id(1) == 0)
        def _():
            o_ref[...] = jnp.zeros_like(o_ref)

        hv = h_ref[...]

        @pl.when(j == 0)
        def _():
            for cidx, r in enumerate((q_ref, k_ref, v_ref, z_ref)):
                o_ref[0, :, cidx * AW:(cidx + 1) * AW] += _tn(hv, r[...])

        @pl.when(jnp.logical_or(j == 1, j == 2))
        def _():
            o_ref[0] += _tn(hv, m_ref[...])

        @pl.when(j == 3)
        def _():
            o_ref[0] += _tn(hv, g_ref[...])

        @pl.when(pl.program_id(1) == S // tm - 1)
        def _():
            ob_ref[...] = o_ref[...].astype(BF16)

    aspec = pl.BlockSpec((tm, AW), lambda j, i: (jnp.where(j == 0, i, 0), 0))
    ospec = pl.BlockSpec((1, D, tn), lambda j, i: (j, 0, 0))
    return pl.pallas_call(
        body, name="dw_in", grid=(4, S // tm),
        in_specs=[pl.BlockSpec((tm, D), lambda j, i: (i, 0)), aspec, aspec, aspec, aspec,
                  pl.BlockSpec((tm, tn), lambda j, i: (jnp.where(jnp.logical_or(j == 1, j == 2), i, 0),
                                                       jnp.where(j == 2, 1, 0))),
                  pl.BlockSpec((tm, tn), lambda j, i: (jnp.where(j == 3, i, 0), 0))],
        out_specs=[ospec, ospec],
        out_shape=[jax.ShapeDtypeStruct((4, D, tn), F32), jax.ShapeDtypeStruct((4, D, tn), BF16)],
        compiler_params=_cp(),
    )(h, *dqkvz, d_hgrn, d_gates)


def _dh_call(dqkvz, d_hgrn, d_gates, w_in, x, dx2, vecs):
    S = x.shape[0]
    tm = 256

    def body(q_ref, k_ref, v_ref, z_ref, m_ref, g_ref, w_ref, x_ref, dx2_ref, p_ref, gx_ref, acc_ref):
        @pl.when(pl.program_id(0) == 0)
        def _():
            acc_ref[...] = jnp.zeros_like(acc_ref)

        dhv = _nt(q_ref[...], w_ref[:, 0:AW])
        for cidx, r in enumerate((k_ref, v_ref, z_ref)):
            dhv += _nt(r[...], w_ref[:, (cidx + 1) * AW:(cidx + 2) * AW])
        dhv += _nt(m_ref[...], w_ref[:, 4 * AW:4 * AW + 4 * D])
        dhv += _nt(g_ref[...], w_ref[:, 4 * AW + 4 * D:NPROJ])
        xv = x_ref[...]
        r = lax.rsqrt(jnp.mean(xv * xv, axis=-1, keepdims=True) + EPS)
        xn = xv * r
        acc_ref[0:1, :] += jnp.sum(dhv, axis=0, keepdims=True)
        acc_ref[1:2, :] += jnp.sum(dhv * xn * p_ref[1:2, :], axis=0, keepdims=True)
        acc_ref[2:3, :] += jnp.sum(dhv * xn * p_ref[2:3, :], axis=0, keepdims=True)
        dxn = dhv * p_ref[0:1, :]
        gx_ref[...] = dx2_ref[...] + r * dxn - xv * (r * r * r * jnp.mean(dxn * xv, axis=-1, keepdims=True))

    row = pl.BlockSpec((tm, D), lambda i: (i, 0))
    aspec = pl.BlockSpec((tm, AW), lambda i: (i, 0))
    const = lambda shape: pl.BlockSpec(shape, lambda i: (0, 0))
    return pl.pallas_call(
        body, name="dh_gradx", grid=(S // tm,),
        in_specs=[aspec, aspec, aspec, aspec,
                  pl.BlockSpec((tm, 4 * D), lambda i: (i, 0)), pl.BlockSpec((tm, 2 * D), lambda i: (i, 0)),
                  pl.BlockSpec((D, NPROJ), lambda i: (0, 0), pipeline_mode=pl.Buffered(1)),
                  row, row, const((8, D))],
        out_specs=[row, const((8, D))],
        out_shape=[jax.ShapeDtypeStruct((S, D), F32), jax.ShapeDtypeStruct((8, D), F32)],
        compiler_params=_cp(),
    )(*dqkvz, d_hgrn, d_gates, w_in, x, dx2, vecs)


def _adamw_math(w, g, m, v):
    m = B1 * m + (1.0 - B1) * g
    v = B2 * v + (1.0 - B2) * (g * g)
    m_hat = m / (1.0 - B1 ** STEP)
    v_hat = v / (1.0 - B2 ** STEP)
    delta = -LR * (m_hat / (jnp.sqrt(v_hat) + AEPS) + WD * w)
    return delta, m, v


def _adamw_call(w, g, m, v, name):
    R, C = w.shape
    tr = R if R * C * 4 <= (1 << 20) else max(8, (1 << 20) // (C * 4))
    assert R % tr == 0

    def body(w_ref, g_ref, m_ref, v_ref, d_ref, nm_ref, nv_ref):
        d_ref[...], nm_ref[...], nv_ref[...] = _adamw_math(w_ref[...], g_ref[...], m_ref[...], v_ref[...])

    blk = pl.BlockSpec((tr, C), lambda i: (i, 0))
    return pl.pallas_call(
        body, name=name, grid=(R // tr,), in_specs=[blk] * 4, out_specs=[blk] * 3,
        out_shape=[jax.ShapeDtypeStruct((R, C), F32)] * 3, compiler_params=_cp(),
    )(w, g, m, v)


def _mod_call(c_all, w_ada_s, b_s):
    def body(c_ref, w_ref, b_ref, o_ref):
        cv = c_ref[...]
        sc = cv * _sig(cv)
        o_ref[...] = jnp.dot(sc, w_ref[...], preferred_element_type=F32,
                             precision=lax.Precision.HIGHEST) + b_ref[...]

    return pl.pallas_call(
        body, name="ada_mod", out_shape=jax.ShapeDtypeStruct((8, w_ada_s.shape[1]), F32),
        compiler_params=_cp(),
    )(c_all, w_ada_s, b_s)


def _ada_update_call(sct, dm, w, m, v):
    R, C = w.shape
    tr = 256

    def body(s_ref, d_ref, w_ref, m_ref, v_ref, g_ref, dl_ref, nm_ref, nv_ref):
        g = s_ref[:, 0:1] * d_ref[0:1, :]
        for b in range(1, 8):
            g = g + s_ref[:, b:b + 1] * d_ref[b:b + 1, :]
        g_ref[...] = g
        dl_ref[...], nm_ref[...], nv_ref[...] = _adamw_math(w_ref[...], g, m_ref[...], v_ref[...])

    blk = pl.BlockSpec((tr, C), lambda i: (i, 0))
    return pl.pallas_call(
        body, name="ada_update", grid=(R // tr,),
        in_specs=[pl.BlockSpec((tr, 8), lambda i: (i, 0)), pl.BlockSpec((8, C), lambda i: (0, 0)), blk, blk, blk],
        out_specs=[blk] * 4, out_shape=[jax.ShapeDtypeStruct((R, C), F32)] * 4, compiler_params=_cp(),
    )(sct, dm, w, m, v)


def _sum8_call(packs):
    def body(p_ref, o_ref):
        acc = p_ref[0]
        for k in range(1, 8):
            acc = acc + p_ref[k]
        o_ref[...] = acc

    return pl.pallas_call(
        body, name="sum_small", out_shape=jax.ShapeDtypeStruct(packs.shape[1:], F32), compiler_params=_cp(),
    )(packs)


def _local_step(x, tgt, shift, scale, gate, norm_g, hgrn_onorm_g, rel_bias, lb, final_g, weights_fn, hook=None):
    a = norm_g * (1.0 + scale)
    z6 = jnp.zeros((6, D), F32)
    h = _h_call(x, jnp.concatenate([a, shift, z6], 0))
    proj, w_in, wa, wb, wo = weights_fn(h)

    biases = _bias_tiles(rel_bias)
    attn, lse, oa = _attn_fwd_call(proj, biases)

    gn = jnp.tile(hgrn_onorm_g, (1, NH))
    oraw, ob, states = _hgrn_fwd_call(proj, lb, gn)

    vecs2 = jnp.concatenate([gate, final_g, z6], 0)
    ya, yb, y, u, x2, lsq = _fwd2_call(oa, ob, proj, x, tgt, vecs2, wa, wb, wo)
    loss = 0.5 * lsq[0, 0] / D

    hsum = jnp.asarray(np.kron(np.eye(NH), np.ones((HE, HE))), BF16)
    dx2, du, dya, dyb, d_gates, dattn, dsum, dza, dob, acc2 = _bwd2_call(
        x2, tgt, vecs2, ya, yb, u, proj, attn, wa, wb, wo, hsum)
    d_wo, d_wo16 = _atb_call(y, du, "dw_out")
    d_wa, d_wa16 = _atb_call(oa, dya, "dw_branch_a")
    d_wb, d_wb16 = _atb_call(ob, dyb, "dw_branch_b")

    d_hgrn, acch = _hgrn_bwd_call(proj, oraw, dob, states, lb, gn)

    dq, dk, dv, dbs = _attn_bwd_call(proj, dattn, lse, dsum, biases)
    dqkvz = (dq, dk, dv, dza)

    d_win, d_win16 = _dwin_call(h, dqkvz, d_hgrn, d_gates)
    tok = hook((d_win, d_wa, d_wb, d_wo), (d_win16, d_wa16, d_wb16, d_wo16)) if hook is not None else 0.0
    one_scale = 1.0 + scale
    grad_x, acc1 = _dh_call(dqkvz, d_hgrn, d_gates, w_in, x, dx2,
                            jnp.concatenate([a + tok, norm_g, one_scale, jnp.zeros((5, D), F32)], 0))

    d_rel = jnp.zeros((NBUCKETS, NH), F32)
    for p, d in enumerate(PATTERNS):
        band, bucket = _band_bucket(d)
        onehot = (bucket[None] == np.arange(NBUCKETS)[:, None, None]) & band[None]
        d_rel = d_rel + jnp.einsum("hqk,bqk->bh", dbs[p], jnp.asarray(onehot, F32),
                                   precision=lax.Precision.HIGHEST)
    d_onorm = jnp.sum(acch[1].reshape(NH, HK), axis=0)

    zrow = jnp.zeros((D,), F32)
    pack = jnp.stack([acc1[0], acc1[1], acc2[0], acc1[2], acc2[1], acch[0],
                      zrow.at[:HK].set(d_onorm), zrow.at[0].set(loss),
                      zrow.at[:NBUCKETS * NH].set(d_rel.reshape(-1))] + [zrow] * 7, 0)
    return grad_x, d_win, d_wa, d_wb, d_wo, pack


def _me():
    return lax.axis_index("x"), lax.axis_index("y"), lax.axis_index("c")


def _peers(x, y):
    return [(1 - x, y), (x, 1 - y), (1 - x, 1 - y)]


def _allgather_small(blk, name):
    m_per, n = blk.shape

    def body(x_ref, out_ref, send_sems, recv_sems, local_sem):
        x, y, c = _me()
        me, sibling = (x, y, c), (x, y, 1 - c)
        chips = _peers(x, y)

        def rows(px, py, pc):
            return out_ref.at[pl.ds((4 * px + 2 * py + pc) * m_per, m_per), :]

        def copy(k, block, to, src=None):
            return pltpu.make_async_remote_copy(
                src_ref=rows(*block) if src is None else src, dst_ref=rows(*block),
                send_sem=send_sems.at[k], recv_sem=recv_sems.at[k], device_id=to, device_id_type=MESH)

        mine = pltpu.make_async_copy(x_ref, rows(*me), local_sem)
        mine.start()
        first = [copy(0, me, sibling, src=x_ref)]
        first += [copy(1 + j, me, (*chip, c), src=x_ref) for j, chip in enumerate(chips)]
        for cp in first:
            cp.start()
        passed = [copy(4 + j, (*chip, c), sibling) for j, chip in enumerate(chips)]
        for j, chip in enumerate(chips):
            copy(1 + j, (*chip, c), me).wait_recv()
            passed[j].start()
        copy(0, sibling, me).wait_recv()
        for j, chip in enumerate(chips):
            copy(4 + j, (*chip, 1 - c), me).wait_recv()
        for cp in first + passed:
            cp.wait_send()
        mine.wait()

    return pl.pallas_call(
        body, name=name, out_shape=jax.ShapeDtypeStruct((8 * m_per, n), blk.dtype),
        in_specs=[pl.BlockSpec(memory_space=pltpu.VMEM)], out_specs=pl.BlockSpec(memory_space=pltpu.VMEM),
        scratch_shapes=[pltpu.SemaphoreType.DMA((7,)), pltpu.SemaphoreType.DMA((7,)), pltpu.SemaphoreType.DMA],
    )(blk)


ANY = pl.BlockSpec(memory_space=pl.ANY)


HBM = pl.BlockSpec(memory_space=pltpu.HBM)
SEM = pl.BlockSpec(memory_space=pltpu.SEMAPHORE)
EFFECT = pltpu.SideEffectType.DATAFLOW_SIDE_EFFECTING


def _w_part(t, ref, j, half):
    if t == 0:
        return ref.at[pl.ds(half * (D // 2), D // 2), pl.ds(j * 2048, 2048)]
    if t == 1:
        return ref.at[pl.ds(half * (AW // 2), AW // 2), pl.ds(j * 256, 256)]
    return ref.at[pl.ds(j * 256 + half * 128, 128), :]


def _w_ici_copies(fulls, send_sems, recv_sems):
    x, y, c = _me()
    outs, ins = [], []
    for t in range(4):
        for k, chip in enumerate(_peers(x, y)):
            mine = _w_part(t, fulls[t], 2 * x + y, c)
            theirs = _w_part(t, fulls[t], 2 * chip[0] + chip[1], c)
            kw = dict(send_sem=send_sems.at[3 * t + k], recv_sem=recv_sems.at[3 * t + k],
                      device_id=(*chip, c), device_id_type=MESH)
            outs.append(pltpu.make_async_remote_copy(src_ref=mine, dst_ref=mine, **kw))
            ins.append(pltpu.make_async_remote_copy(src_ref=theirs, dst_ref=theirs, **kw))
    return outs, ins


def _gather_ici_start(fulls):
    def body(f0, f1, f2, f3, send_sems, recv_sems, t0, t1, t2, t3, token):
        for cp in _w_ici_copies([f0, f1, f2, f3], send_sems, recv_sems)[0]:
            cp.start()
        token[...] = jnp.zeros_like(token)

    res = pl.pallas_call(
        body, name="gather_ici_start",
        out_shape=(pltpu.SemaphoreType.DMA((12,)), pltpu.SemaphoreType.DMA((12,)),
                   *[pltpu.HBM(a.shape, a.dtype) for a in fulls], jax.ShapeDtypeStruct((8, 128), F32)),
        in_specs=[HBM] * 4, out_specs=(SEM, SEM, *[HBM] * 4, pl.BlockSpec(memory_space=pltpu.VMEM)),
        input_output_aliases={i: 2 + i for i in range(4)},
        compiler_params=pltpu.CompilerParams(has_side_effects=EFFECT),
    )(*[pltpu.with_memory_space_constraint(a, pltpu.HBM) for a in fulls])
    return res[0], res[1], list(res[2:6]), res[6]


def _gather_ici_wait(send_sems, recv_sems, thru, after):
    def body(f0, f1, f2, f3, send_sems, recv_sems, after_ref, g0, g1, g2, g3):
        outs, ins = _w_ici_copies([f0, f1, f2, f3], send_sems, recv_sems)
        for cp in outs:
            cp.wait_send()
        for cp in ins:
            cp.wait_recv()

    return pl.pallas_call(
        body, name="gather_ici_wait", out_shape=tuple(pltpu.HBM(a.shape, a.dtype) for a in thru),
        in_specs=[HBM] * 4 + [SEM, SEM, ANY], out_specs=[HBM] * 4,
        input_output_aliases={i: i for i in range(4)},
        compiler_params=pltpu.CompilerParams(has_side_effects=EFFECT),
    )(*thru, send_sems, recv_sems, after)


def _gather_sibling(fulls):
    def body(i0, i1, i2, i3, o0, o1, o2, o3, send_sems, recv_sems):
        x, y, c = _me()
        outs = [o0, o1, o2, o3]
        cps = []
        for t in range(4):
            for k, chip in enumerate(_peers(x, y)):
                blk = _w_part(t, outs[t], 2 * chip[0] + chip[1], c)
                cp = pltpu.make_async_remote_copy(
                    src_ref=blk, dst_ref=blk, send_sem=send_sems.at[t, k], recv_sem=recv_sems.at[t, k],
                    device_id=(x, y, 1 - c), device_id_type=MESH)
                cp.start()
                cps.append(cp)
        for t in range(4):
            for k, chip in enumerate(_peers(x, y)):
                blk = _w_part(t, outs[t], 2 * chip[0] + chip[1], 1 - c)
                pltpu.make_async_remote_copy(
                    src_ref=blk, dst_ref=blk, send_sem=send_sems.at[t, k], recv_sem=recv_sems.at[t, k],
                    device_id=(x, y, 1 - c), device_id_type=MESH).wait_recv()
        for cp in cps:
            cp.wait_send()

    return pl.pallas_call(
        body, name="gather_sibling",
        out_shape=[jax.ShapeDtypeStruct(s, BF16) for s in FULL_W_SHAPES],
        in_specs=[ANY] * 4, out_specs=[ANY] * 4, input_output_aliases={0: 0, 1: 1, 2: 2, 3: 3},
        scratch_shapes=[pltpu.SemaphoreType.DMA((4, 3)), pltpu.SemaphoreType.DMA((4, 3))],
    )(*fulls)


def _half_of(t, ref, half):
    if t == 0:
        return ref.at[:, pl.ds(half * 512, 512), :]
    if t == 1:
        return ref.at[pl.ds(half * 256, 256), :]
    return ref.at[:, pl.ds(half * 512, 512)]


HALF_SHAPES = [(4, 512, 2048), (256, D), (D, 512), (D, 512)]
PIECE_SHAPES = [(512, 2048), (256, 256), (256, 512), (256, 512)]
SHARD_SHAPES = [(D, 2048), (AW, 256), (256, D), (256, D)]


def _chip_piece(t, ref, j):
    if t == 0:
        return ref.at[j]
    if t == 1:
        return ref.at[:, pl.ds(j * 256, 256)]
    return ref.at[pl.ds(j * 256, 256), :]


def _reduce_sibling_send(gs):
    def body(g0, g1, g2, g3, r0, r1, r2, r3, send_sems, recv_sems):
        x, y, c = _me()
        ins, outs = [g0, g1, g2, g3], [r0, r1, r2, r3]
        cps = []
        for t in range(4):
            cp = pltpu.make_async_remote_copy(
                src_ref=_half_of(t, ins[t], 1 - c), dst_ref=outs[t],
                send_sem=send_sems.at[t], recv_sem=recv_sems.at[t], device_id=(x, y, 1 - c), device_id_type=MESH)
            cp.start()
            cps.append(cp)
        for cp in cps:
            cp.wait_recv()
        for cp in cps:
            cp.wait_send()

    return pl.pallas_call(
        body, name="reduce_sibling", out_shape=[jax.ShapeDtypeStruct(s, BF16) for s in HALF_SHAPES],
        in_specs=[ANY] * 4, out_specs=[ANY] * 4,
        scratch_shapes=[pltpu.SemaphoreType.DMA((4,)), pltpu.SemaphoreType.DMA((4,))],
    )(*gs)


def _chip_copies(hs, lands, send_sems, recv_sems):
    x, y, c = _me()
    cps = []
    for t in range(4):
        for k, chip in enumerate(_peers(x, y)):
            pj = 2 * chip[0] + chip[1]
            cps.append(pltpu.make_async_remote_copy(
                src_ref=_chip_piece(t, hs[t], pj), dst_ref=lands[t].at[k],
                send_sem=send_sems.at[3 * t + k], recv_sem=recv_sems.at[3 * t + k],
                device_id=(*chip, c), device_id_type=MESH))
    return cps


def _reduce_chips_start(hs):
    lands = [lax.empty((3,) + s, BF16) for s in PIECE_SHAPES]

    def body(h0, h1, h2, h3, l0, l1, l2, l3, send_sems, recv_sems, t0, t1, t2, t3, t4, t5, t6, t7, token):
        for cp in _chip_copies([h0, h1, h2, h3], [l0, l1, l2, l3], send_sems, recv_sems):
            cp.start()
        token[...] = jnp.zeros_like(token)

    bufs = list(hs) + lands
    res = pl.pallas_call(
        body, name="reduce_chips_start",
        out_shape=(pltpu.SemaphoreType.DMA((12,)), pltpu.SemaphoreType.DMA((12,)),
                   *[pltpu.HBM(a.shape, a.dtype) for a in bufs], jax.ShapeDtypeStruct((8, 128), F32)),
        in_specs=[HBM] * 8, out_specs=(SEM, SEM, *[HBM] * 8, pl.BlockSpec(memory_space=pltpu.VMEM)),
        input_output_aliases={i: 2 + i for i in range(8)},
        compiler_params=pltpu.CompilerParams(has_side_effects=EFFECT),
    )(*[pltpu.with_memory_space_constraint(a, pltpu.HBM) for a in bufs])
    return res[0], res[1], list(res[2:10]), res[10]


def _reduce_chips_wait(send_sems, recv_sems, thru, after):
    def body(h0, h1, h2, h3, l0, l1, l2, l3, send_sems, recv_sems, after_ref, d0, d1, d2, d3, g0, g1, g2, g3):
        cps = _chip_copies([h0, h1, h2, h3], [l0, l1, l2, l3], send_sems, recv_sems)
        for cp in cps:
            cp.wait_send()
        for cp in cps:
            cp.wait_recv()

    res = pl.pallas_call(
        body, name="reduce_chips_wait", out_shape=tuple(pltpu.HBM(a.shape, a.dtype) for a in thru),
        in_specs=[HBM] * 8 + [SEM, SEM, ANY], out_specs=[HBM] * 8,
        input_output_aliases={i: i for i in range(8)},
        compiler_params=pltpu.CompilerParams(has_side_effects=EFFECT),
    )(*thru, send_sems, recv_sems, after)
    return list(res[4:8])


def _share_sibling(shards):
    def body(i0, i1, i2, i3, o0, o1, o2, o3, send_sems, recv_sems):
        x, y, c = _me()
        outs = [o0, o1, o2, o3]

        def half(t, ref, hf):
            if t == 0:
                return ref.at[pl.ds(hf * 512, 512), :]
            if t == 1:
                return ref.at[pl.ds(hf * 256, 256), :]
            return ref.at[:, pl.ds(hf * 512, 512)]

        cps = []
        for t in range(4):
            mine = half(t, outs[t], c)
            cp = pltpu.make_async_remote_copy(
                src_ref=mine, dst_ref=mine, send_sem=send_sems.at[t], recv_sem=recv_sems.at[t],
                device_id=(x, y, 1 - c), device_id_type=MESH)
            cp.start()
            cps.append(cp)
        for t in range(4):
            theirs = half(t, outs[t], 1 - c)
            pltpu.make_async_remote_copy(
                src_ref=theirs, dst_ref=theirs, send_sem=send_sems.at[t],
                recv_sem=recv_sems.at[t], device_id=(x, y, 1 - c), device_id_type=MESH).wait_recv()
        for cp in cps:
            cp.wait_send()

    return pl.pallas_call(
        body, name="share_sibling", out_shape=[jax.ShapeDtypeStruct(s, F32) for s in SHARD_SHAPES],
        in_specs=[ANY] * 4, out_specs=[ANY] * 4, input_output_aliases={0: 0, 1: 1, 2: 2, 3: 3},
        scratch_shapes=[pltpu.SemaphoreType.DMA((4,)), pltpu.SemaphoreType.DMA((4,))],
    )(*shards)


def _cast_call(a, name):
    lead = a.shape[0]
    rest = a.shape[1:]
    per = int(np.prod(rest)) * 4
    tl = lead
    while tl * per > (4 << 20) and tl % 2 == 0:
        tl //= 2
    if len(rest) == 1 and tl % 16 != 0:
        tl = lead
    zeros = (0,) * len(rest)

    def body(a_ref, o_ref):
        o_ref[...] = a_ref[...].astype(BF16)

    blk = pl.BlockSpec((tl,) + rest, lambda i: (i,) + zeros)
    return pl.pallas_call(
        body, name=name, grid=(lead // tl,), in_specs=[blk], out_specs=blk,
        out_shape=jax.ShapeDtypeStruct(a.shape, BF16), compiler_params=_cp(),
    )(a)


def _half_blockspec(t, idx_pos):
    if t == 0:
        return pl.BlockSpec((1, 512, 2048), lambda i, s: (i, s[idx_pos], 0)), 4
    if t == 1:
        return pl.BlockSpec((256, D), lambda i, s: (s[idx_pos], 0)), 1
    return pl.BlockSpec((256, 512), lambda i, s: (i, s[idx_pos])), 4


def _half_out_blockspec(t):
    if t == 0:
        return pl.BlockSpec((1, 512, 2048), lambda i, s: (i, 0, 0))
    if t == 1:
        return pl.BlockSpec((256, D), lambda i, s: (0, 0))
    return pl.BlockSpec((256, 512), lambda i, s: (i, 0))


def _add_half_call(t, own, recv, sc, name):
    in_blk, steps = _half_blockspec(t, 0)
    out_blk = _half_out_blockspec(t)

    def body(s_ref, a_ref, b_ref, o_ref, ob_ref):
        v = a_ref[...] + b_ref[...].astype(F32)
        o_ref[...] = v
        ob_ref[...] = v.astype(BF16)

    return pl.pallas_call(
        body, name=name,
        grid_spec=pltpu.PrefetchScalarGridSpec(
            num_scalar_prefetch=1, grid=(steps,), in_specs=[in_blk, out_blk], out_specs=[out_blk, out_blk]),
        out_shape=[jax.ShapeDtypeStruct(HALF_SHAPES[t], F32), jax.ShapeDtypeStruct(HALF_SHAPES[t], BF16)],
        compiler_params=_cp(),
    )(sc, own, recv)


def _final_piece_call(t, chipsum, recv3, sc, name):
    ps = PIECE_SHAPES[t]
    if t == 0:
        own_blk = pl.BlockSpec((1,) + ps, lambda i, s: (s[1], 0, 0))
        o_blk = pl.BlockSpec(ps, lambda i, s: (s[0], 0))
    elif t == 1:
        own_blk = pl.BlockSpec(ps, lambda i, s: (0, s[1]))
        o_blk = pl.BlockSpec(ps, lambda i, s: (s[0], 0))
    else:
        own_blk = pl.BlockSpec(ps, lambda i, s: (s[1], 0))
        o_blk = pl.BlockSpec(ps, lambda i, s: (0, s[0]))
    r_blk = pl.BlockSpec((3,) + ps, lambda i, s: (0, 0, 0))

    def body(s_ref, a_ref, r_ref, o_ref):
        a = a_ref[0] if t == 0 else a_ref[...]
        o_ref[...] = ((a + r_ref[0].astype(F32)) + r_ref[1].astype(F32)) + r_ref[2].astype(F32)

    return pl.pallas_call(
        body, name=name,
        grid_spec=pltpu.PrefetchScalarGridSpec(
            num_scalar_prefetch=1, grid=(1,), in_specs=[own_blk, r_blk], out_specs=o_blk),
        out_shape=jax.ShapeDtypeStruct(SHARD_SHAPES[t], F32), compiler_params=_cp(),
    )(sc, chipsum, recv3)


FULL_W_SHAPES = [(D, NPROJ), (AW, D), (D, D), (D, D)]


def _cast_place_call(t, shard, sc, name):
    if t == 0:
        blk, steps = (512, 2048), 2
        in_blk = pl.BlockSpec(blk, lambda i, s: (i, 0))
        o_blk = pl.BlockSpec(blk, lambda i, s: (i, s[1]))
    elif t == 1:
        blk, steps = (AW, 256), 1
        in_blk = pl.BlockSpec(blk, lambda i, s: (0, 0))
        o_blk = pl.BlockSpec(blk, lambda i, s: (0, s[1]))
    else:
        blk, steps = (256, D), 1
        in_blk = pl.BlockSpec(blk, lambda i, s: (0, 0))
        o_blk = pl.BlockSpec(blk, lambda i, s: (s[1], 0))

    def body(s_ref, a_ref, o_ref, own_ref):
        v = a_ref[...].astype(BF16)
        o_ref[...] = v
        own_ref[...] = v

    return pl.pallas_call(
        body, name=name,
        grid_spec=pltpu.PrefetchScalarGridSpec(
            num_scalar_prefetch=1, grid=(steps,), in_specs=[in_blk], out_specs=[o_blk, in_blk]),
        out_shape=[jax.ShapeDtypeStruct(FULL_W_SHAPES[t], BF16), jax.ShapeDtypeStruct(shard.shape, BF16)],
        compiler_params=_cp(),
    )(sc, shard)


def _lower_bound_fn(hgrn_lb):
    return jnp.cumsum(jax.nn.softmax(hgrn_lb.astype(F32), axis=0), axis=0)[0]


def kernel(x, c, w_ada, b_ada, norm_g, w_in, hgrn_onorm_g, w_branch_a, w_branch_b, w_out, rel_bias, hgrn_lb, final_g, loss_target, m_w_ada, m_b_ada, m_norm_g, m_w_in, m_hgrn_onorm_g, m_w_branch_a, m_w_branch_b, m_w_out, m_rel_bias, m_hgrn_lb, m_final_g, v_w_ada, v_b_ada, v_norm_g, v_w_in, v_hgrn_onorm_g, v_w_branch_a, v_w_branch_b, v_w_out, v_rel_bias, v_hgrn_lb, v_final_g):
    ax, ay, ac = _me()
    chip = 2 * ax + ay
    dev = 4 * ax + 2 * ay + ac
    sc_idx = jnp.stack([ac, chip]).astype(jnp.int32)

    names = ["w_in", "w_a", "w_b", "w_o"]
    shards = [w_in[0], w_branch_a[0], w_branch_b[0], w_out[0]]
    placed = [_cast_place_call(t, shards[t], sc_idx, "cast_" + names[t]) for t in range(4)]
    w_send_sems, w_recv_sems, w_thru, w_token = _gather_ici_start([p_[0] for p_ in placed])

    c_all = _allgather_small(jnp.pad(c + w_token[0, 0], ((0, 7), (0, 0))), "gather_c").reshape(8, 8, D)[:, 0]
    b_s = lax.dynamic_slice(b_ada, (0, chip * 768), (1, 768))
    mod_part = _mod_call(c_all, w_ada[0], b_s)
    mod_all = _allgather_small(mod_part, "gather_mod").reshape(8, 8, 768)
    mod_mine = lax.dynamic_index_in_dim(mod_all, dev, axis=1, keepdims=False)
    mod = mod_mine[0::2].reshape(1, 3 * D)
    shift, scale, gate = mod[:, :D], mod[:, D:2 * D], mod[:, 2 * D:]

    def weights_fn(h):
        proj = _proj_own_call(h, placed[0][1], sc_idx)
        arrived = _gather_ici_wait(w_send_sems, w_recv_sems, w_thru, proj)
        win_f, wa_f, wb_f, wo_f = _gather_sibling(arrived)
        return _proj_rest_call(h, win_f, proj, sc_idx), win_f, wa_f, wb_f, wo_f

    flight = {}

    def start_reduction(own, own16):
        sib = _reduce_sibling_send(own16)
        halves = [_add_half_call(t, own[t], sib[t], sc_idx, "chipsum_" + names[t]) for t in range(4)]
        send_sems, recv_sems, thru, token = _reduce_chips_start([hb for _, hb in halves])
        flight.update(sems=(send_sems, recv_sems), thru=thru, sums=[hf for hf, _ in halves])
        return token[0, 0]

    lb, lb_vjp = jax.vjp(_lower_bound_fn, hgrn_lb)
    grad_x, d_win, d_wa, d_wb, d_wo, pack = _local_step(
        x[0], loss_target[0], shift, scale, gate, norm_g, hgrn_onorm_g, rel_bias, lb[None, :],
        final_g[None, :], weights_fn, hook=start_reduction)

    rec = _reduce_chips_wait(*flight["sems"], flight["thru"], pack)
    pieces = [_final_piece_call(t, flight["sums"][t], rec[t], sc_idx, "piece_" + names[t]) for t in range(4)]
    g_win, g_wa, g_wb, g_wo = _share_sibling(pieces)

    packs = _allgather_small(pack, "gather_small").reshape(8, 16, D)
    tot = _sum8_call(packs)
    loss = tot[7, 0]
    g_b_ada = tot[0:3].reshape(1, 3 * D)
    g_norm_g = tot[3:4]
    g_final_g = tot[4]
    (g_hgrn_lb,) = lb_vjp(tot[5])
    g_onorm = tot[6:7, :HK]
    g_rel = tot[8, :NBUCKETS * NH].reshape(NBUCKETS, NH)

    def rows_of(a):
        flat = a.reshape(-1)
        n = -(-flat.shape[0] // D)
        return jnp.pad(flat, (0, n * D - flat.shape[0])).reshape(n, D)

    smalls = [(b_ada, g_b_ada, m_b_ada, v_b_ada), (norm_g, g_norm_g, m_norm_g, v_norm_g),
              (hgrn_onorm_g, g_onorm, m_hgrn_onorm_g, v_hgrn_onorm_g), (rel_bias, g_rel, m_rel_bias, v_rel_bias),
              (hgrn_lb, g_hgrn_lb, m_hgrn_lb, v_hgrn_lb), (final_g, g_final_g, m_final_g, v_final_g)]
    cat = [jnp.concatenate([rows_of(s[k]) for s in smalls], 0) for k in range(4)]
    cat = [jnp.pad(a, ((0, 16 - a.shape[0]), (0, 0))) for a in cat]
    sd, sm, sv = _adamw_call(*cat, "adamw_small")

    def unpack(packed):
        res, r = [], 0
        for s in smalls:
            n = -(-s[0].size // D)
            res.append(packed[r:r + n].reshape(-1)[:s[0].size].reshape(s[0].shape))
            r += n
        return res

    d_small, m_small, v_small = unpack(sd), unpack(sm), unpack(sv)

    sc_all = c_all * jax.nn.sigmoid(c_all)
    dmod_all = packs[:, 0:3].reshape(8, 3 * D)
    dm_s = lax.dynamic_slice(dmod_all, (0, chip * 768), (8, 768))
    g_w_ada, d_w_ada, nm_w_ada, nv_w_ada = _ada_update_call(sc_all.T, dm_s, w_ada[0], m_w_ada[0], v_w_ada[0])

    big = []
    for w, g, m, v, n in [(w_in, g_win, m_w_in, v_w_in, "w_in"), (w_branch_a, g_wa, m_w_branch_a, v_w_branch_a, "w_a"),
                          (w_branch_b, g_wb, m_w_branch_b, v_w_branch_b, "w_b"), (w_out, g_wo, m_w_out, v_w_out, "w_o")]:
        big.append(_adamw_call(w[0], g, m[0], v[0], "adamw_" + n))

    e = lambda a: a[None]
    grads = [e(g_w_ada), g_b_ada, g_norm_g, e(g_win), g_onorm, e(g_wa), e(g_wb), e(g_wo), g_rel, g_hgrn_lb, g_final_g]
    deltas = [e(d_w_ada), d_small[0], d_small[1], e(big[0][0]), d_small[2], e(big[1][0]), e(big[2][0]), e(big[3][0]),
              d_small[3], d_small[4], d_small[5]]
    new_m = [e(nm_w_ada), m_small[0], m_small[1], e(big[0][1]), m_small[2], e(big[1][1]), e(big[2][1]), e(big[3][1]),
             m_small[3], m_small[4], m_small[5]]
    new_v = [e(nv_w_ada), v_small[0], v_small[1], e(big[0][2]), v_small[2], e(big[1][2]), e(big[2][2]), e(big[3][2]),
             v_small[3], v_small[4], v_small[5]]
    return (loss, grad_x[None], *grads, *deltas, *new_m, *new_v)
```

```python
import functools
import math

import numpy as np
import jax
import jax.numpy as jnp
from jax import lax
from jax.experimental import pallas as pl
from jax.experimental.pallas import tpu as pltpu

D = 1024
AW = 512
NH = 8
HE = 64
HK = 128
NPROJ = 8192
ABLK = 128
PATTERNS = (1, 4, 16)
NBUCKETS = 32
MAXDIST = 2048
NEG = -1e30
EPS = 1e-6
CH = 64
LR, B1, B2, AEPS, WD, STEP = 0.001, 0.9, 0.999, 1e-08, 0.01, 10

F32 = jnp.float32
BF16 = jnp.bfloat16
MESH = pl.DeviceIdType.MESH
VMEM_LIMIT = 56 * 1024 * 1024


def _cp(**kw):
    return pltpu.CompilerParams(vmem_limit_bytes=VMEM_LIMIT, **kw)


def _sig(x):
    return 0.5 * jnp.tanh(0.5 * x) + 0.5


def _nt(a, b):
    return lax.dot_general(a, b, (((1,), (1,)), ((), ())), preferred_element_type=F32)


def _tn(a, b):
    return lax.dot_general(a, b, (((0,), (0,)), ((), ())), preferred_element_type=F32)


def _nn(a, b):
    return jnp.dot(a, b, preferred_element_type=F32)


def _split2(x):
    h = x.astype(BF16)
    return h, (x - h.astype(F32)).astype(BF16)


def _exact_mm(tri_bf16, x):
    h, l = _split2(x)
    return _nn(tri_bf16, h) + _nn(tri_bf16, l)


def _exact_mm_r(x, ones_bf16):
    h, l = _split2(x)
    return _nn(h, ones_bf16) + _nn(l, ones_bf16)


def _h_call(x, avec):
    S = x.shape[0]
    tm = 512

    def body(x_ref, a_ref, h_ref):
        xv = x_ref[...]
        r = lax.rsqrt(jnp.mean(xv * xv, axis=-1, keepdims=True) + EPS)
        h_ref[...] = (xv * r * a_ref[0:1, :] + a_ref[1:2, :]).astype(BF16)

    return pl.pallas_call(
        body, name="h_norm", grid=(S // tm,),
        in_specs=[pl.BlockSpec((tm, D), lambda i: (i, 0)), pl.BlockSpec((8, D), lambda i: (0, 0))],
        out_specs=pl.BlockSpec((tm, D), lambda i: (i, 0)),
        out_shape=jax.ShapeDtypeStruct((S, D), BF16), compiler_params=_cp(),
    )(x, avec)


def _proj_call(h, w_in):
    S = h.shape[0]
    tm, tn = 512, 2048

    def body(h_ref, w_ref, o_ref):
        o_ref[...] = _nn(h_ref[...], w_ref[...])

    return pl.pallas_call(
        body, name="in_proj", grid=(NPROJ // tn, S // tm),
        in_specs=[pl.BlockSpec((tm, D), lambda j, i: (i, 0)), pl.BlockSpec((D, tn), lambda j, i: (0, j))],
        out_specs=pl.BlockSpec((tm, tn), lambda j, i: (i, j)),
        out_shape=jax.ShapeDtypeStruct((S, NPROJ), F32), compiler_params=_cp(),
    )(h, w_in)


def _proj_own_call(h, w_own, sc):
    S = h.shape[0]
    tm, tn = 512, 2048

    def body(s_ref, h_ref, w_ref, o_ref):
        o_ref[...] = _nn(h_ref[...], w_ref[...])

    return pl.pallas_call(
        body, name="in_proj_own",
        grid_spec=pltpu.PrefetchScalarGridSpec(
            num_scalar_prefetch=1, grid=(S // tm,),
            in_specs=[pl.BlockSpec((tm, D), lambda i, s: (i, 0)), pl.BlockSpec((D, tn), lambda i, s: (0, 0))],
            out_specs=pl.BlockSpec((tm, tn), lambda i, s: (i, s[1]))),
        out_shape=jax.ShapeDtypeStruct((S, NPROJ), F32), compiler_params=_cp(),
    )(sc, h, w_own)


def _proj_rest_call(h, w_in, proj, sc):
    S = h.shape[0]
    tm, tn = 512, 2048

    def body(s_ref, h_ref, w_ref, p_ref, o_ref):
        o_ref[...] = _nn(h_ref[...], w_ref[...])

    col = lambda j, s: (s[1] + 1 + j) % 4
    return pl.pallas_call(
        body, name="in_proj_rest",
        grid_spec=pltpu.PrefetchScalarGridSpec(
            num_scalar_prefetch=1, grid=(3, S // tm),
            in_specs=[pl.BlockSpec((tm, D), lambda j, i, s: (i, 0)),
                      pl.BlockSpec((D, tn), lambda j, i, s: (0, col(j, s))),
                      pl.BlockSpec(memory_space=pl.ANY)],
            out_specs=pl.BlockSpec((tm, tn), lambda j, i, s: (i, col(j, s)))),
        out_shape=jax.ShapeDtypeStruct((S, NPROJ), F32), input_output_aliases={3: 0}, compiler_params=_cp(),
    )(sc, h, w_in, proj)


def _t5_bucket_np(dist):
    max_exact = NBUCKETS // 2
    n = dist.astype(np.float32)
    large = max_exact + (np.log(np.maximum(n, np.float32(1.0)) / np.float32(max_exact))
                         / np.float32(math.log(MAXDIST / max_exact))
                         * np.float32(NBUCKETS - max_exact)).astype(np.int32)
    large = np.minimum(large, NBUCKETS - 1)
    return np.where(dist < max_exact, dist, large)


def _band_bucket(d):
    qi = np.arange(ABLK)[:, None]
    kj = np.arange(2 * ABLK)[None, :]
    delta = qi + ABLK - kj
    band = (delta >= 0) & (delta <= ABLK)
    bucket = _t5_bucket_np(np.clip(delta, 0, None) * d)
    return band, bucket


def _bias_tiles(rel_bias):
    tiles = []
    for d in PATTERNS:
        band, bucket = _band_bucket(d)
        onehot = (jnp.asarray(bucket, jnp.int32)[None] == jnp.arange(NBUCKETS, dtype=jnp.int32)[:, None, None])
        bias = jnp.einsum("bqk,bh->hqk", onehot.astype(F32), rel_bias, precision=lax.Precision.HIGHEST)
        tiles.append(jnp.where(jnp.asarray(band)[None], bias, NEG))
    return jnp.stack(tiles, 0)


ATT = 2048
HP = 2 * HE
AGRP = 4
AGRP_B = 2


def _attn_blocks():
    out = []
    for p, d in enumerate(PATTERNS):
        for r in range(d):
            for n in range(ATT // (d * ABLK)):
                out.append((p, d, r, n))
    return out


def _attn_fwd_call(proj, biases):
    S = proj.shape[0]
    nt = S // ATT

    def body(q_ref, k_ref, v_ref, z_ref, b_ref, a_ref, l_ref, oa_ref, kc, vc, op, lp):
        i = pl.program_id(1)

        @pl.when(i == 0)
        def _():
            kc[0:ATT] = jnp.zeros((ATT, HP), F32)
            vc[0:ATT] = jnp.zeros((ATT, HP), F32)

        @pl.when(i > 0)
        def _():
            kc[0:ATT] = kc[ATT:2 * ATT]
            vc[0:ATT] = vc[ATT:2 * ATT]

        kc[ATT:2 * ATT] = k_ref[...]
        vc[ATT:2 * ATT] = v_ref[...]
        col = lax.broadcasted_iota(jnp.int32, (ABLK, 2 * ABLK), 1)
        dead = jnp.logical_and(i == 0, col < ABLK)
        blocks = _attn_blocks()
        hs = (slice(0, HE), slice(HE, 2 * HE))
        for g0 in range(0, len(blocks), AGRP):
            grp = blocks[g0:g0 + AGRP]
            qrows = [pl.ds(n * ABLK * d + r, ABLK, stride=d) for p, d, r, n in grp]
            krows = [pl.ds(ATT + (n - 1) * ABLK * d + r, 2 * ABLK, stride=d) for p, d, r, n in grp]
            qs = [(q_ref[qr, :] * (HE ** -0.5)).astype(BF16) for qr in qrows]
            ks = [kc[kr, :].astype(BF16) for kr in krows]
            vs = [vc[kr, :].astype(BF16) for kr in krows]
            ss = [[_nt(qs[b][:, sl], ks[b][:, sl]) + b_ref[grp[b][0], e] for e, sl in enumerate(hs)]
                  for b in range(len(grp))]
            ss = [[jnp.where(dead, NEG, s) if grp[b][3] == 0 else s for s in ss[b]] for b in range(len(grp))]
            mxs = [[jnp.max(s, axis=-1, keepdims=True) for s in sb] for sb in ss]
            pes = [[jnp.exp(s - mx) for s, mx in zip(sb, mb)] for sb, mb in zip(ss, mxs)]
            dens = [[jnp.sum(pe, axis=-1, keepdims=True) for pe in pb] for pb in pes]
            pvs = [[_nn(pe.astype(BF16), vs[b][:, sl]) for pe, sl in zip(pes[b], hs)] for b in range(len(grp))]
            for b in range(len(grp)):
                p, d, r, n = grp[b]
                prow = pl.ds(p * ATT + n * ABLK * d + r, ABLK, stride=d)
                lp[prow, :] = jnp.concatenate(
                    [jnp.broadcast_to(mx + jnp.log(dn), (ABLK, HE)) for mx, dn in zip(mxs[b], dens[b])], axis=1)
                op[prow, :] = jnp.concatenate([pv / dn for pv, dn in zip(pvs[b], dens[b])], axis=1)
        rt = 256
        for t in range(ATT // rt):
            rows = slice(t * rt, (t + 1) * rt)
            pr = [slice(p * ATT + t * rt, p * ATT + (t + 1) * rt) for p in range(3)]
            la, lb_, lc = lp[pr[0], :], lp[pr[1], :], lp[pr[2], :]
            m = jnp.maximum(jnp.maximum(la, lb_), lc)
            ea, eb, ec = jnp.exp(la - m), jnp.exp(lb_ - m), jnp.exp(lc - m)
            den = ea + eb + ec
            att = (ea * op[pr[0], :] + eb * op[pr[1], :] + ec * op[pr[2], :]) / den
            a_ref[rows, :] = att
            l_ref[rows, :] = m + jnp.log(den)
            z = z_ref[rows, :]
            oa_ref[rows, :] = (att * (z * _sig(z))).astype(BF16)

    def pcol(c):
        return pl.BlockSpec((ATT, HP), lambda h, i: (i, c * 4 + h))

    out = pl.BlockSpec((ATT, HP), lambda h, i: (i, h))
    return pl.pallas_call(
        body, name="attn_fwd", grid=(4, nt),
        in_specs=[pcol(0), pcol(1), pcol(2), pcol(3),
                  pl.BlockSpec((3, 2, ABLK, 2 * ABLK), lambda h, i: (0, h, 0, 0))],
        out_specs=[out, out, out],
        out_shape=[jax.ShapeDtypeStruct((S, AW), F32), jax.ShapeDtypeStruct((S, AW), F32),
                   jax.ShapeDtypeStruct((S, AW), BF16)],
        scratch_shapes=[pltpu.VMEM((2 * ATT, HP), F32), pltpu.VMEM((2 * ATT, HP), F32),
                        pltpu.VMEM((3 * ATT, HP), F32), pltpu.VMEM((3 * ATT, HP), F32)],
        compiler_params=_cp(),
    )(proj, proj, proj, proj, biases)


def _attn_bwd_call(proj, dattn, lse, dsum, biases):
    S = proj.shape[0]
    nt = S // ATT

    def body(q_ref, k_ref, v_ref, do_ref, l_ref, ds_ref, b_ref, dq_ref, dk_ref, dv_ref, db_ref,
             kc, vc, dkc, dvc, dqa):
        i = pl.program_id(1)

        @pl.when(i == 0)
        def _():
            kc[ATT:2 * ATT] = jnp.zeros((ATT, HP), F32)
            vc[ATT:2 * ATT] = jnp.zeros((ATT, HP), F32)
            dkc[ATT:2 * ATT] = jnp.zeros((ATT, HP), F32)
            dvc[ATT:2 * ATT] = jnp.zeros((ATT, HP), F32)
            db_ref[...] = jnp.zeros_like(db_ref)

        @pl.when(i < nt)
        def _():
            kc[0:ATT] = kc[ATT:2 * ATT]
            vc[0:ATT] = vc[ATT:2 * ATT]
            dkc[0:ATT] = dkc[ATT:2 * ATT]
            dvc[0:ATT] = dvc[ATT:2 * ATT]
            kc[ATT:2 * ATT] = k_ref[...]
            vc[ATT:2 * ATT] = v_ref[...]
            dkc[ATT:2 * ATT] = jnp.zeros((ATT, HP), F32)
            dvc[ATT:2 * ATT] = jnp.zeros((ATT, HP), F32)
            col = lax.broadcasted_iota(jnp.int32, (ABLK, 2 * ABLK), 1)
            dead = jnp.logical_and(i == 0, col < ABLK)
            blocks = _attn_blocks()
            hs = (slice(0, HE), slice(HE, 2 * HE))
            for g0 in range(0, len(blocks), AGRP_B):
                grp = blocks[g0:g0 + AGRP_B]
                nb_ = range(len(grp))
                qrows = [pl.ds(n * ABLK * d + r, ABLK, stride=d) for p, d, r, n in grp]
                krows = [pl.ds(ATT + (n - 1) * ABLK * d + r, 2 * ABLK, stride=d) for p, d, r, n in grp]
                qs = [(q_ref[qr, :] * (HE ** -0.5)).astype(BF16) for qr in qrows]
                ks = [kc[kr, :].astype(BF16) for kr in krows]
                vs = [vc[kr, :].astype(BF16) for kr in krows]
                dos = [do_ref[qr, :].astype(BF16) for qr in qrows]
                lvs = [l_ref[qr, :] for qr in qrows]
                dsvs = [ds_ref[qr, :] for qr in qrows]
                ss = [[_nt(qs[b][:, sl], ks[b][:, sl]) + b_ref[grp[b][0], e] for e, sl in enumerate(hs)] for b in nb_]
                ss = [[jnp.where(dead, NEG, s) if grp[b][3] == 0 else s for s in ss[b]] for b in nb_]
                dps = [[_nt(dos[b][:, sl], vs[b][:, sl]) for sl in hs] for b in nb_]
                pes = [[jnp.exp(ss[b][e] - lvs[b][:, e * HE:e * HE + 1]) for e in range(2)] for b in nb_]
                dscs = [[pes[b][e] * (dps[b][e] - dsvs[b][:, e * HE:e * HE + 1]) for e in range(2)] for b in nb_]
                for b in nb_:
                    for e in range(2):
                        db_ref[grp[b][0], e] += dscs[b][e]
                dsbs = [[t.astype(BF16) for t in tb] for tb in dscs]
                dqs = [[_nn(dsbs[b][e], ks[b][:, sl]) * (HE ** -0.5) for e, sl in enumerate(hs)] for b in nb_]
                dks = [[_tn(dsbs[b][e], qs[b][:, sl]) for e, sl in enumerate(hs)] for b in nb_]
                dvs = [[_tn(pes[b][e].astype(BF16), dos[b][:, sl]) for e, sl in enumerate(hs)] for b in nb_]
                for b in nb_:
                    dq = jnp.concatenate(dqs[b], axis=1)
                    if grp[b][0] == 0:
                        dqa[qrows[b], :] = dq
                    else:
                        dqa[qrows[b], :] += dq
                    dkc[krows[b], :] += jnp.concatenate(dks[b], axis=1)
                    dvc[krows[b], :] += jnp.concatenate(dvs[b], axis=1)
            dq_ref[...] = dqa[...].astype(BF16)
            dk_ref[...] = dkc[0:ATT].astype(BF16)
            dv_ref[...] = dvc[0:ATT].astype(BF16)

        @pl.when(i == nt)
        def _():
            dk_ref[...] = dkc[ATT:2 * ATT].astype(BF16)
            dv_ref[...] = dvc[ATT:2 * ATT].astype(BF16)

    def pcol(c):
        return pl.BlockSpec((ATT, HP), lambda h, i: (jnp.minimum(i, nt - 1), c * 4 + h))

    qrow = pl.BlockSpec((ATT, HP), lambda h, i: (jnp.minimum(i, nt - 1), h))
    krow = pl.BlockSpec((ATT, HP), lambda h, i: (jnp.maximum(i - 1, 0), h))
    bspec = pl.BlockSpec((3, 2, ABLK, 2 * ABLK), lambda h, i: (0, h, 0, 0))
    return pl.pallas_call(
        body, name="attn_bwd", grid=(4, nt + 1),
        in_specs=[pcol(0), pcol(1), pcol(2), qrow, qrow, qrow, bspec],
        out_specs=[qrow, krow, krow, bspec],
        out_shape=[jax.ShapeDtypeStruct((S, AW), BF16)] * 3
                  + [jax.ShapeDtypeStruct((3, NH, ABLK, 2 * ABLK), F32)],
        scratch_shapes=[pltpu.VMEM((2 * ATT, HP), F32)] * 4 + [pltpu.VMEM((ATT, HP), F32)],
        compiler_params=_cp(),
    )(proj, proj, proj, dattn, lse, dsum, biases)


HRB = 256


def _hgrn_gates(q_ref, f_ref, rows, lbv, tri):
    qraw = q_ref[rows, :]
    sq = _sig(qraw)
    q = qraw * sq
    sf = _sig(f_ref[rows, :])
    f = lbv + (1.0 - lbv) * sf
    k = 1.0 - f
    b = _exact_mm(tri, jnp.log(f))
    bl = b[CH - 1:CH, :]
    bm = b[CH // 2 - 1:CH // 2, :]
    e1 = jnp.exp(b - bm)
    e2 = jnp.exp(bm - b)
    ebm = jnp.exp(bm)
    eblm = jnp.exp(bl - bm)
    qs = q * e1
    ks = k * e2
    qe = qs * ebm
    kd = ks * eblm
    return dict(qraw=qraw, sq=sq, sf=sf, f=f, bl=bl, e1=e1, e2=e2, ebm=ebm, eblm=eblm, qe=qe, qs=qs, ks=ks, kd=kd)


def _tri_masks():
    row = lax.broadcasted_iota(jnp.int32, (CH, CH), 0)
    col = lax.broadcasted_iota(jnp.int32, (CH, CH), 1)
    return row >= col


def _hgrn_fwd_call(proj, lb, gn):
    S = proj.shape[0]
    nc = S // CH
    cps = HRB // CH

    def body(q_ref, f_ref, i_ref, z_ref, lb_ref, gn_ref, or_ref, ob_ref, st_ref, st):
        @pl.when(pl.program_id(0) == 0)
        def _():
            st[...] = jnp.zeros_like(st)

        low = _tri_masks()
        tri = low.astype(BF16)
        lbv = lb_ref[...]
        for ci in range(cps):
            rows = slice(ci * CH, (ci + 1) * CH)
            g = _hgrn_gates(q_ref, f_ref, rows, lbv, tri)
            v = i_ref[rows, :]
            ebl = jnp.exp(g["bl"])
            st_ref[ci] = st[...]
            hs = [slice(hh * HK, (hh + 1) * HK) for hh in range(NH)]
            vb = v.astype(BF16)
            qsb, ksb, qeb, kdb = (g[n_].astype(BF16) for n_ in ("qs", "ks", "qe", "kd"))
            s0s = [st[sl, :] for sl in hs]
            as_ = [_nt(qsb[:, sl], ksb[:, sl]) for sl in hs]
            ois = [_nt(qeb[:, sl], s0.astype(BF16)) for sl, s0 in zip(hs, s0s)]
            sts = [_tn(vb[:, sl], kdb[:, sl]) for sl in hs]
            abs_ = [jnp.where(low, a, 0.0).astype(BF16) for a in as_]
            os_ = [oi + _nn(a, vb[:, sl]) for oi, a, sl in zip(ois, abs_, hs)]
            for sl, s0, sn in zip(hs, s0s, sts):
                st[sl, :] = s0 * ebl[:, sl] + sn
            o = jnp.concatenate(os_, axis=1)
            or_ref[rows, :] = o
            rs = [lax.rsqrt(jnp.mean(oh * oh, axis=-1, keepdims=True) + EPS) for oh in os_]
            on = jnp.concatenate([oh * r for oh, r in zip(os_, rs)], axis=1)
            z = z_ref[rows, :]
            ob_ref[rows, :] = (on * gn_ref[...] * (z * _sig(z))).astype(BF16)

    def pcol(c):
        return pl.BlockSpec((HRB, D), lambda i: (i, c))

    vec = pl.BlockSpec((1, D), lambda i: (0, 0))
    row = pl.BlockSpec((HRB, D), lambda i: (i, 0))
    return pl.pallas_call(
        body, name="hgrn_fwd", grid=(S // HRB,),
        in_specs=[pcol(2), pcol(3), pcol(4), pcol(5), vec, vec],
        out_specs=[row, row, pl.BlockSpec((cps, NH * HK, HK), lambda i: (i, 0, 0))],
        out_shape=[jax.ShapeDtypeStruct((S, D), F32), jax.ShapeDtypeStruct((S, D), BF16),
                   jax.ShapeDtypeStruct((nc, NH * HK, HK), F32)],
        scratch_shapes=[pltpu.VMEM((NH * HK, HK), F32)],
        compiler_params=_cp(),
    )(proj, proj, proj, proj, lb, gn)


def _hgrn_bwd_call(proj, oraw, dob, states, lb, gn):
    S = proj.shape[0]
    nblk = S // HRB
    cps = HRB // CH

    def body(q_ref, f_ref, i_ref, z_ref, or_ref, dob_ref, st_ref, lb_ref, gn_ref, dh_ref, acc_ref, dst):
        @pl.when(pl.program_id(0) == 0)
        def _():
            dst[...] = jnp.zeros_like(dst)
            acc_ref[...] = jnp.zeros_like(acc_ref)

        low = _tri_masks()
        tri = low.astype(BF16)
        triu = jnp.logical_not(_tri_masks()) | (lax.broadcasted_iota(jnp.int32, (CH, CH), 0)
                                               == lax.broadcasted_iota(jnp.int32, (CH, CH), 1))
        triu = triu.astype(BF16)
        lbv = lb_ref[...]
        for ci in reversed(range(cps)):
            rows = slice(ci * CH, (ci + 1) * CH)
            g = _hgrn_gates(q_ref, f_ref, rows, lbv, tri)
            v = i_ref[rows, :]
            ebl = jnp.exp(g["bl"])
            hs = [slice(hh * HK, (hh + 1) * HK) for hh in range(NH)]
            cat = lambda lst: jnp.concatenate(lst, axis=1)
            o = or_ref[rows, :]
            z = z_ref[rows, :]
            sz = _sig(z)
            gnv = gn_ref[...]
            dobv = dob_ref[rows, :]
            r = cat([jnp.broadcast_to(lax.rsqrt(jnp.mean(o[:, sl] * o[:, sl], axis=-1, keepdims=True) + EPS),
                                      (CH, HK)) for sl in hs])
            don = dobv * (z * sz)
            dz = dobv * (o * r * gnv) * (sz * (1.0 + z * (1.0 - sz)))
            dgn = jnp.sum(don * o * r, axis=0, keepdims=True)
            gh = don * gnv
            gho = gh * o
            mg = cat([jnp.broadcast_to(jnp.mean(gho[:, sl], axis=-1, keepdims=True), (CH, HK)) for sl in hs])
            dob16 = (r * gh - o * (r * r * r * mg)).astype(BF16)

            vb = v.astype(BF16)
            qsb, ksb, qeb, kdb = (g[n_].astype(BF16) for n_ in ("qs", "ks", "qe", "kd"))
            st0s = [st_ref[ci, sl, :] for sl in hs]
            dst1s = [dst[sl, :] for sl in hs]
            dst1bs = [t.astype(BF16) for t in dst1s]
            as_ = [_nt(qsb[:, sl], ksb[:, sl]) for sl in hs]
            das_ = [_nt(dob16[:, sl], vb[:, sl]) for sl in hs]
            dqes = [_nn(dob16[:, sl], s0.astype(BF16)) for sl, s0 in zip(hs, st0s)]
            dkds = [_nn(vb[:, sl], d1) for sl, d1 in zip(hs, dst1bs)]
            dvis = [_nt(kdb[:, sl], d1) for sl, d1 in zip(hs, dst1bs)]
            dsts = [_tn(dob16[:, sl], qeb[:, sl]) for sl in hs]
            abs_ = [jnp.where(low, a, 0.0).astype(BF16) for a in as_]
            dabs_ = [jnp.where(low, a, 0.0).astype(BF16) for a in das_]
            dqss = [_nn(da, ksb[:, sl]) for da, sl in zip(dabs_, hs)]
            dkss = [_tn(da, qsb[:, sl]) for da, sl in zip(dabs_, hs)]
            dvs_ = [_tn(a, dob16[:, sl]) + dvi for a, sl, dvi in zip(abs_, hs, dvis)]
            exs_ = [jnp.sum(d1 * s0, axis=0, keepdims=True) for d1, s0 in zip(dst1s, st0s)]
            for sl, d1, dn in zip(hs, dst1s, dsts):
                dst[sl, :] = dn + d1 * ebl[:, sl]
            dqe, dqs, dks, dkd, dv = cat(dqes), cat(dqss), cat(dkss), cat(dkds), cat(dvs_)
            dq = (dqe * g["ebm"] + dqs) * g["e1"]
            dk = (dks + dkd * g["eblm"]) * g["e2"]
            dkdkd = dkd * g["kd"]
            db = dqe * g["qe"] + dqs * qsb.astype(F32) - dks * ksb.astype(F32) - dkdkd
            ex = jnp.sum(dkdkd, axis=0, keepdims=True) + cat(exs_) * ebl
            dg = _exact_mm(triu, db) + ex
            df = dg / g["f"] - dk
            sf = g["sf"]
            dfr = df * (1.0 - lbv) * sf * (1.0 - sf)
            sq = g["sq"]
            dqr = dq * (sq * (1.0 + g["qraw"] * (1.0 - sq)))
            acc_ref[0:1, :] += jnp.sum(df * (1.0 - sf), axis=0, keepdims=True)
            acc_ref[1:2, :] += dgn
            dh_ref[rows, 0:D] = dqr.astype(BF16)
            dh_ref[rows, D:2 * D] = dfr.astype(BF16)
            dh_ref[rows, 2 * D:3 * D] = dv.astype(BF16)
            dh_ref[rows, 3 * D:4 * D] = dz.astype(BF16)

    def pcol(c):
        return pl.BlockSpec((HRB, D), lambda i: (nblk - 1 - i, c))

    vec = pl.BlockSpec((1, D), lambda i: (0, 0))
    row = pl.BlockSpec((HRB, D), lambda i: (nblk - 1 - i, 0))
    return pl.pallas_call(
        body, name="hgrn_bwd", grid=(nblk,),
        in_specs=[pcol(2), pcol(3), pcol(4), pcol(5), row, row,
                  pl.BlockSpec((cps, NH * HK, HK), lambda i: (nblk - 1 - i, 0, 0)), vec, vec],
        out_specs=[pl.BlockSpec((HRB, 4 * D), lambda i: (nblk - 1 - i, 0)),
                   pl.BlockSpec((8, D), lambda i: (0, 0))],
        out_shape=[jax.ShapeDtypeStruct((S, 4 * D), BF16), jax.ShapeDtypeStruct((8, D), F32)],
        scratch_shapes=[pltpu.VMEM((NH * HK, HK), F32)],
        compiler_params=_cp(),
    )(proj, proj, proj, proj, oraw, dob, states, lb, gn)


def _fwd2_call(oa, ob, proj, x, tgt, vecs, wa, wb, wo):
    S = x.shape[0]
    tm = 256

    def body(oa_ref, ob_ref, ga_ref, gb_ref, x_ref, t_ref, v_ref, wa_ref, wb_ref, wo_ref,
             ya_ref, yb_ref, y_ref, u_ref, x2_ref, ls_ref):
        @pl.when(pl.program_id(0) == 0)
        def _():
            ls_ref[...] = jnp.zeros_like(ls_ref)

        ya = _nn(oa_ref[...], wa_ref[...])
        yb = _nn(ob_ref[...], wb_ref[...])
        y = _sig(ga_ref[...]) * ya + _sig(gb_ref[...]) * yb
        u = _nn(y.astype(BF16), wo_ref[...])
        x2 = x_ref[...] + v_ref[0:1, :] * u
        r = lax.rsqrt(jnp.mean(x2 * x2, axis=-1, keepdims=True) + EPS)
        err = x2 * r * v_ref[1:2, :] - t_ref[...]
        ls_ref[...] += jnp.sum(err * err)
        ya_ref[...] = ya.astype(BF16)
        yb_ref[...] = yb.astype(BF16)
        y_ref[...] = y.astype(BF16)
        u_ref[...] = u.astype(BF16)
        x2_ref[...] = x2

    row = pl.BlockSpec((tm, D), lambda i: (i, 0))
    full = lambda a: pl.BlockSpec(a.shape, lambda i: (0, 0))
    return pl.pallas_call(
        body, name="fwd_merge_out", grid=(S // tm,),
        in_specs=[pl.BlockSpec((tm, AW), lambda i: (i, 0)), row,
                  pl.BlockSpec((tm, D), lambda i: (i, 6)), pl.BlockSpec((tm, D), lambda i: (i, 7)),
                  row, row, pl.BlockSpec((8, D), lambda i: (0, 0)), full(wa), full(wb), full(wo)],
        out_specs=[row, row, row, row, row, pl.BlockSpec((8, 128), lambda i: (0, 0))],
        out_shape=[jax.ShapeDtypeStruct((S, D), BF16)] * 4
                  + [jax.ShapeDtypeStruct((S, D), F32), jax.ShapeDtypeStruct((8, 128), F32)],
        compiler_params=_cp(),
    )(oa, ob, proj, proj, x, tgt, vecs, wa, wb, wo)


def _bwd2_call(x2, tgt, vecs, ya, yb, u, proj, attn, wa, wb, wo, hsum):
    S = x2.shape[0]
    tm = 256

    def body(x2_ref, t_ref, v_ref, ya_ref, yb_ref, u_ref, ga_ref, gb_ref, at_ref, za_ref,
             wa_ref, wb_ref, wo_ref, hs_ref,
             dx2_ref, du_ref, dya_ref, dyb_ref, dg_ref, dat_ref, dsum_ref, dza_ref, dob_ref, acc_ref):
        @pl.when(pl.program_id(0) == 0)
        def _():
            acc_ref[...] = jnp.zeros_like(acc_ref)

        x2v = x2_ref[...]
        gate = v_ref[0:1, :]
        fg = v_ref[1:2, :]
        r = lax.rsqrt(jnp.mean(x2v * x2v, axis=-1, keepdims=True) + EPS)
        dout = (x2v * r * fg - t_ref[...]) * (1.0 / D)
        gh = dout * fg
        dx2 = r * gh - x2v * (r * r * r * jnp.mean(gh * x2v, axis=-1, keepdims=True))
        acc_ref[0:1, :] += jnp.sum(dx2 * u_ref[...].astype(F32), axis=0, keepdims=True)
        acc_ref[1:2, :] += jnp.sum(dout * x2v * r, axis=0, keepdims=True)
        dx2_ref[...] = dx2
        du = (dx2 * gate).astype(BF16)
        du_ref[...] = du
        dy = _nt(du, wo_ref[...])
        sa = _sig(ga_ref[...])
        sb = _sig(gb_ref[...])
        dya = (dy * sa).astype(BF16)
        dyb = (dy * sb).astype(BF16)
        dya_ref[...] = dya
        dyb_ref[...] = dyb
        dg_ref[:, 0:D] = (dy * ya_ref[...].astype(F32) * sa * (1.0 - sa)).astype(BF16)
        dg_ref[:, D:2 * D] = (dy * yb_ref[...].astype(F32) * sb * (1.0 - sb)).astype(BF16)
        doa = _nt(dya, wa_ref[...])
        dob_ref[...] = _nt(dyb, wb_ref[...])
        za = za_ref[...]
        sz = _sig(za)
        att = at_ref[...]
        dat = doa * (za * sz)
        dat_ref[...] = dat
        dza_ref[...] = (doa * att * (sz * (1.0 + za * (1.0 - sz)))).astype(BF16)
        dsum_ref[...] = _exact_mm_r(dat * att, hs_ref[...])

    row = pl.BlockSpec((tm, D), lambda i: (i, 0))
    arow = pl.BlockSpec((tm, AW), lambda i: (i, 0))
    full = lambda a: pl.BlockSpec(a.shape, lambda i: (0, 0))
    return pl.pallas_call(
        body, name="bwd_merge_out", grid=(S // tm,),
        in_specs=[row, row, pl.BlockSpec((8, D), lambda i: (0, 0)), row, row, row,
                  pl.BlockSpec((tm, D), lambda i: (i, 6)), pl.BlockSpec((tm, D), lambda i: (i, 7)),
                  arow, pl.BlockSpec((tm, AW), lambda i: (i, 3)), full(wa), full(wb), full(wo), full(hsum)],
        out_specs=[row, row, row, row, pl.BlockSpec((tm, 2 * D), lambda i: (i, 0)),
                   arow, arow, arow, row, pl.BlockSpec((8, D), lambda i: (0, 0))],
        out_shape=[jax.ShapeDtypeStruct((S, D), F32), jax.ShapeDtypeStruct((S, D), BF16),
                   jax.ShapeDtypeStruct((S, D), BF16), jax.ShapeDtypeStruct((S, D), BF16),
                   jax.ShapeDtypeStruct((S, 2 * D), BF16), jax.ShapeDtypeStruct((S, AW), F32),
                   jax.ShapeDtypeStruct((S, AW), F32), jax.ShapeDtypeStruct((S, AW), BF16),
                   jax.ShapeDtypeStruct((S, D), F32), jax.ShapeDtypeStruct((8, D), F32)],
        compiler_params=_cp(),
    )(x2, tgt, vecs, ya, yb, u, proj, proj, attn, proj, wa, wb, wo, hsum)


def _atb_call(a, b, name):
    S, K = a.shape
    N = b.shape[1]
    tm = 512

    def body(a_ref, b_ref, o_ref, ob_ref):
        @pl.when(pl.program_id(0) == 0)
        def _():
            o_ref[...] = jnp.zeros_like(o_ref)

        o_ref[...] += _tn(a_ref[...], b_ref[...])

        @pl.when(pl.program_id(0) == S // tm - 1)
        def _():
            ob_ref[...] = o_ref[...].astype(BF16)

    ospec = pl.BlockSpec((K, N), lambda i: (0, 0))
    return pl.pallas_call(
        body, name=name, grid=(S // tm,),
        in_specs=[pl.BlockSpec((tm, K), lambda i: (i, 0)), pl.BlockSpec((tm, N), lambda i: (i, 0))],
        out_specs=[ospec, ospec],
        out_shape=[jax.ShapeDtypeStruct((K, N), F32), jax.ShapeDtypeStruct((K, N), BF16)], compiler_params=_cp(),
    )(a, b)


def _dwin_call(h, dqkvz, d_hgrn, d_gates):
    S = h.shape[0]
    tm = 512
    tn = 2048

    def body(h_ref, q_ref, k_ref, v_ref, z_ref, m_ref, g_ref, o_ref, ob_ref):
        j = pl.program_id(0)

        @pl.when(pl.program_id(1) == 0)
        def _():
            o_ref[...] = jnp.zeros_like(o_ref)

        hv = h_ref[...]

        @pl.when(j == 0)
        def _():
            for cidx, r in enumerate((q_ref, k_ref, v_ref, z_ref)):
                o_ref[0, :, cidx * AW:(cidx + 1) * AW] += _tn(hv, r[...])

        @pl.when(jnp.logical_or(j == 1, j == 2))
        def _():
            o_ref[0] += _tn(hv, m_ref[...])

        @pl.when(j == 3)
        def _():
            o_ref[0] += _tn(hv, g_ref[...])

        @pl.when(pl.program_id(1) == S // tm - 1)
        def _():
            ob_ref[...] = o_ref[...].astype(BF16)

    aspec = pl.BlockSpec((tm, AW), lambda j, i: (jnp.where(j == 0, i, 0), 0))
    ospec = pl.BlockSpec((1, D, tn), lambda j, i: (j, 0, 0))
    return pl.pallas_call(
        body, name="dw_in", grid=(4, S // tm),
        in_specs=[pl.BlockSpec((tm, D), lambda j, i: (i, 0)), aspec, aspec, aspec, aspec,
                  pl.BlockSpec((tm, tn), lambda j, i: (jnp.where(jnp.logical_or(j == 1, j == 2), i, 0),
                                                       jnp.where(j == 2, 1, 0))),
                  pl.BlockSpec((tm, tn), lambda j, i: (jnp.where(j == 3, i, 0), 0))],
        out_specs=[ospec, ospec],
        out_shape=[jax.ShapeDtypeStruct((4, D, tn), F32), jax.ShapeDtypeStruct((4, D, tn), BF16)],
        compiler_params=_cp(),
    )(h, *dqkvz, d_hgrn, d_gates)


def _dh_call(dqkvz, d_hgrn, d_gates, w_in, x, dx2, vecs):
    S = x.shape[0]
    tm = 256

    def body(q_ref, k_ref, v_ref, z_ref, m_ref, g_ref, w_ref, x_ref, dx2_ref, p_ref, gx_ref, acc_ref):
        @pl.when(pl.program_id(0) == 0)
        def _():
            acc_ref[...] = jnp.zeros_like(acc_ref)

        dhv = _nt(q_ref[...], w_ref[:, 0:AW])
        for cidx, r in enumerate((k_ref, v_ref, z_ref)):
            dhv += _nt(r[...], w_ref[:, (cidx + 1) * AW:(cidx + 2) * AW])
        dhv += _nt(m_ref[...], w_ref[:, 4 * AW:4 * AW + 4 * D])
        dhv += _nt(g_ref[...], w_ref[:, 4 * AW + 4 * D:NPROJ])
        xv = x_ref[...]
        r = lax.rsqrt(jnp.mean(xv * xv, axis=-1, keepdims=True) + EPS)
        xn = xv * r
        acc_ref[0:1, :] += jnp.sum(dhv, axis=0, keepdims=True)
        acc_ref[1:2, :] += jnp.sum(dhv * xn * p_ref[1:2, :], axis=0, keepdims=True)
        acc_ref[2:3, :] += jnp.sum(dhv * xn * p_ref[2:3, :], axis=0, keepdims=True)
        dxn = dhv * p_ref[0:1, :]
        gx_ref[...] = dx2_ref[...] + r * dxn - xv * (r * r * r * jnp.mean(dxn * xv, axis=-1, keepdims=True))

    row = pl.BlockSpec((tm, D), lambda i: (i, 0))
    aspec = pl.BlockSpec((tm, AW), lambda i: (i, 0))
    const = lambda shape: pl.BlockSpec(shape, lambda i: (0, 0))
    return pl.pallas_call(
        body, name="dh_gradx", grid=(S // tm,),
        in_specs=[aspec, aspec, aspec, aspec,
                  pl.BlockSpec((tm, 4 * D), lambda i: (i, 0)), pl.BlockSpec((tm, 2 * D), lambda i: (i, 0)),
                  pl.BlockSpec((D, NPROJ), lambda i: (0, 0), pipeline_mode=pl.Buffered(1)),
                  row, row, const((8, D))],
        out_specs=[row, const((8, D))],
        out_shape=[jax.ShapeDtypeStruct((S, D), F32), jax.ShapeDtypeStruct((8, D), F32)],
        compiler_params=_cp(),
    )(*dqkvz, d_hgrn, d_gates, w_in, x, dx2, vecs)


def _adamw_math(w, g, m, v):
    m = B1 * m + (1.0 - B1) * g
    v = B2 * v + (1.0 - B2) * (g * g)
    m_hat = m / (1.0 - B1 ** STEP)
    v_hat = v / (1.0 - B2 ** STEP)
    delta = -LR * (m_hat / (jnp.sqrt(v_hat) + AEPS) + WD * w)
    return delta, m, v


def _adamw_call(w, g, m, v, name):
    R, C = w.shape
    tr = R if R * C * 4 <= (1 << 20) else max(8, (1 << 20) // (C * 4))
    assert R % tr == 0

    def body(w_ref, g_ref, m_ref, v_ref, d_ref, nm_ref, nv_ref):
        d_ref[...], nm_ref[...], nv_ref[...] = _adamw_math(w_ref[...], g_ref[...], m_ref[...], v_ref[...])

    blk = pl.BlockSpec((tr, C), lambda i: (i, 0))
    return pl.pallas_call(
        body, name=name, grid=(R // tr,), in_specs=[blk] * 4, out_specs=[blk] * 3,
        out_shape=[jax.ShapeDtypeStruct((R, C), F32)] * 3, compiler_params=_cp(),
    )(w, g, m, v)


def _mod_call(c_all, w_ada_s, b_s):
    def body(c_ref, w_ref, b_ref, o_ref):
        cv = c_ref[...]
        sc = cv * _sig(cv)
        o_ref[...] = jnp.dot(sc, w_ref[...], preferred_element_type=F32,
                             precision=lax.Precision.HIGHEST) + b_ref[...]

    return pl.pallas_call(
        body, name="ada_mod", out_shape=jax.ShapeDtypeStruct((8, w_ada_s.shape[1]), F32),
        compiler_params=_cp(),
    )(c_all, w_ada_s, b_s)


def _ada_update_call(sct, dm, w, m, v):
    R, C = w.shape
    tr = 256

    def body(s_ref, d_ref, w_ref, m_ref, v_ref, g_ref, dl_ref, nm_ref, nv_ref):
        g = s_ref[:, 0:1] * d_ref[0:1, :]
        for b in range(1, 8):
            g = g + s_ref[:, b:b + 1] * d_ref[b:b + 1, :]
        g_ref[...] = g
        dl_ref[...], nm_ref[...], nv_ref[...] = _adamw_math(w_ref[...], g, m_ref[...], v_ref[...])

    blk = pl.BlockSpec((tr, C), lambda i: (i, 0))
    return pl.pallas_call(
        body, name="ada_update", grid=(R // tr,),
        in_specs=[pl.BlockSpec((tr, 8), lambda i: (i, 0)), pl.BlockSpec((8, C), lambda i: (0, 0)), blk, blk, blk],
        out_specs=[blk] * 4, out_shape=[jax.ShapeDtypeStruct((R, C), F32)] * 4, compiler_params=_cp(),
    )(sct, dm, w, m, v)


def _sum8_call(packs):
    def body(p_ref, o_ref):
        acc = p_ref[0]
        for k in range(1, 8):
            acc = acc + p_ref[k]
        o_ref[...] = acc

    return pl.pallas_call(
        body, name="sum_small", out_shape=jax.ShapeDtypeStruct(packs.shape[1:], F32), compiler_params=_cp(),
    )(packs)


def _local_step(x, tgt, shift, scale, gate, norm_g, hgrn_onorm_g, rel_bias, lb, final_g, weights_fn, hook=None):
    a = norm_g * (1.0 + scale)
    z6 = jnp.zeros((6, D), F32)
    h = _h_call(x, jnp.concatenate([a, shift, z6], 0))
    proj, w_in, wa, wb, wo = weights_fn(h)

    biases = _bias_tiles(rel_bias)
    attn, lse, oa = _attn_fwd_call(proj, biases)

    gn = jnp.tile(hgrn_onorm_g, (1, NH))
    oraw, ob, states = _hgrn_fwd_call(proj, lb, gn)

    vecs2 = jnp.concatenate([gate, final_g, z6], 0)
    ya, yb, y, u, x2, lsq = _fwd2_call(oa, ob, proj, x, tgt, vecs2, wa, wb, wo)
    loss = 0.5 * lsq[0, 0] / D

    hsum = jnp.asarray(np.kron(np.eye(NH), np.ones((HE, HE))), BF16)
    dx2, du, dya, dyb, d_gates, dattn, dsum, dza, dob, acc2 = _bwd2_call(
        x2, tgt, vecs2, ya, yb, u, proj, attn, wa, wb, wo, hsum)
    d_wo, d_wo16 = _atb_call(y, du, "dw_out")
    d_wa, d_wa16 = _atb_call(oa, dya, "dw_branch_a")
    d_wb, d_wb16 = _atb_call(ob, dyb, "dw_branch_b")

    d_hgrn, acch = _hgrn_bwd_call(proj, oraw, dob, states, lb, gn)

    dq, dk, dv, dbs = _attn_bwd_call(proj, dattn, lse, dsum, biases)
    dqkvz = (dq, dk, dv, dza)

    d_win, d_win16 = _dwin_call(h, dqkvz, d_hgrn, d_gates)
    tok = hook((d_win, d_wa, d_wb, d_wo), (d_win16, d_wa16, d_wb16, d_wo16)) if hook is not None else 0.0
    one_scale = 1.0 + scale
    grad_x, acc1 = _dh_call(dqkvz, d_hgrn, d_gates, w_in, x, dx2,
                            jnp.concatenate([a + tok, norm_g, one_scale, jnp.zeros((5, D), F32)], 0))

    d_rel = jnp.zeros((NBUCKETS, NH), F32)
    for p, d in enumerate(PATTERNS):
        band, bucket = _band_bucket(d)
        onehot = (bucket[None] == np.arange(NBUCKETS)[:, None, None]) & band[None]
        d_rel = d_rel + jnp.einsum("hqk,bqk->bh", dbs[p], jnp.asarray(onehot, F32),
                                   precision=lax.Precision.HIGHEST)
    d_onorm = jnp.sum(acch[1].reshape(NH, HK), axis=0)

    zrow = jnp.zeros((D,), F32)
    pack = jnp.stack([acc1[0], acc1[1], acc2[0], acc1[2], acc2[1], acch[0],
                      zrow.at[:HK].set(d_onorm), zrow.at[0].set(loss),
                      zrow.at[:NBUCKETS * NH].set(d_rel.reshape(-1))] + [zrow] * 7, 0)
    return grad_x, d_win, d_wa, d_wb, d_wo, pack


def _me():
    return lax.axis_index("x"), lax.axis_index("y"), lax.axis_index("c")


def _peers(x, y):
    return [(1 - x, y), (x, 1 - y), (1 - x, 1 - y)]


def _allgather_small(blk, name):
    m_per, n = blk.shape

    def body(x_ref, out_ref, send_sems, recv_sems, local_sem):
        x, y, c = _me()
        me, sibling = (x, y, c), (x, y, 1 - c)
        chips = _peers(x, y)

        def rows(px, py, pc):
            return out_ref.at[pl.ds((4 * px + 2 * py + pc) * m_per, m_per), :]

        def copy(k, block, to, src=None):
            return pltpu.make_async_remote_copy(
                src_ref=rows(*block) if src is None else src, dst_ref=rows(*block),
                send_sem=send_sems.at[k], recv_sem=recv_sems.at[k], device_id=to, device_id_type=MESH)

        mine = pltpu.make_async_copy(x_ref, rows(*me), local_sem)
        mine.start()
        first = [copy(0, me, sibling, src=x_ref)]
        first += [copy(1 + j, me, (*chip, c), src=x_ref) for j, chip in enumerate(chips)]
        for cp in first:
            cp.start()
        passed = [copy(4 + j, (*chip, c), sibling) for j, chip in enumerate(chips)]
        for j, chip in enumerate(chips):
            copy(1 + j, (*chip, c), me).wait_recv()
            passed[j].start()
        copy(0, sibling, me).wait_recv()
        for j, chip in enumerate(chips):
            copy(4 + j, (*chip, 1 - c), me).wait_recv()
        for cp in first + passed:
            cp.wait_send()
        mine.wait()

    return pl.pallas_call(
        body, name=name, out_shape=jax.ShapeDtypeStruct((8 * m_per, n), blk.dtype),
        in_specs=[pl.BlockSpec(memory_space=pltpu.VMEM)], out_specs=pl.BlockSpec(memory_space=pltpu.VMEM),
        scratch_shapes=[pltpu.SemaphoreType.DMA((7,)), pltpu.SemaphoreType.DMA((7,)), pltpu.SemaphoreType.DMA],
    )(blk)


ANY = pl.BlockSpec(memory_space=pl.ANY)


HBM = pl.BlockSpec(memory_space=pltpu.HBM)
SEM = pl.BlockSpec(memory_space=pltpu.SEMAPHORE)
EFFECT = pltpu.SideEffectType.DATAFLOW_SIDE_EFFECTING


def _w_part(t, ref, j, half):
    if t == 0:
        return ref.at[pl.ds(half * (D // 2), D // 2), pl.ds(j * 2048, 2048)]
    if t == 1:
        return ref.at[pl.ds(half * (AW // 2), AW // 2), pl.ds(j * 256, 256)]
    return ref.at[pl.ds(j * 256 + half * 128, 128), :]


def _w_ici_copies(fulls, send_sems, recv_sems):
    x, y, c = _me()
    outs, ins = [], []
    for t in range(4):
        for k, chip in enumerate(_peers(x, y)):
            mine = _w_part(t, fulls[t], 2 * x + y, c)
            theirs = _w_part(t, fulls[t], 2 * chip[0] + chip[1], c)
            kw = dict(send_sem=send_sems.at[3 * t + k], recv_sem=recv_sems.at[3 * t + k],
                      device_id=(*chip, c), device_id_type=MESH)
            outs.append(pltpu.make_async_remote_copy(src_ref=mine, dst_ref=mine, **kw))
            ins.append(pltpu.make_async_remote_copy(src_ref=theirs, dst_ref=theirs, **kw))
    return outs, ins


def _gather_ici_start(fulls):
    def body(f0, f1, f2, f3, send_sems, recv_sems, t0, t1, t2, t3, token):
        for cp in _w_ici_copies([f0, f1, f2, f3], send_sems, recv_sems)[0]:
            cp.start()
        token[...] = jnp.zeros_like(token)

    res = pl.pallas_call(
        body, name="gather_ici_start",
        out_shape=(pltpu.SemaphoreType.DMA((12,)), pltpu.SemaphoreType.DMA((12,)),
                   *[pltpu.HBM(a.shape, a.dtype) for a in fulls], jax.ShapeDtypeStruct((8, 128), F32)),
        in_specs=[HBM] * 4, out_specs=(SEM, SEM, *[HBM] * 4, pl.BlockSpec(memory_space=pltpu.VMEM)),
        input_output_aliases={i: 2 + i for i in range(4)},
        compiler_params=pltpu.CompilerParams(has_side_effects=EFFECT),
    )(*[pltpu.with_memory_space_constraint(a, pltpu.HBM) for a in fulls])
    return res[0], res[1], list(res[2:6]), res[6]


def _gather_ici_wait(send_sems, recv_sems, thru, after):
    def body(f0, f1, f2, f3, send_sems, recv_sems, after_ref, g0, g1, g2, g3):
        outs, ins = _w_ici_copies([f0, f1, f2, f3], send_sems, recv_sems)
        for cp in outs:
            cp.wait_send()
        for cp in ins:
            cp.wait_recv()

    return pl.pallas_call(
        body, name="gather_ici_wait", out_shape=tuple(pltpu.HBM(a.shape, a.dtype) for a in thru),
        in_specs=[HBM] * 4 + [SEM, SEM, ANY], out_specs=[HBM] * 4,
        input_output_aliases={i: i for i in range(4)},
        compiler_params=pltpu.CompilerParams(has_side_effects=EFFECT),
    )(*thru, send_sems, recv_sems, after)


def _gather_sibling(fulls):
    def body(i0, i1, i2, i3, o0, o1, o2, o3, send_sems, recv_sems):
        x, y, c = _me()
        outs = [o0, o1, o2, o3]
        cps = []
        for t in range(4):
            for k, chip in enumerate(_peers(x, y)):
                blk = _w_part(t, outs[t], 2 * chip[0] + chip[1], c)
                cp = pltpu.make_async_remote_copy(
                    src_ref=blk, dst_ref=blk, send_sem=send_sems.at[t, k], recv_sem=recv_sems.at[t, k],
                    device_id=(x, y, 1 - c), device_id_type=MESH)
                cp.start()
                cps.append(cp)
        for t in range(4):
            for k, chip in enumerate(_peers(x, y)):
                blk = _w_part(t, outs[t], 2 * chip[0] + chip[1], 1 - c)
                pltpu.make_async_remote_copy(
                    src_ref=blk, dst_ref=blk, send_sem=send_sems.at[t, k], recv_sem=recv_sems.at[t, k],
                    device_id=(x, y, 1 - c), device_id_type=MESH).wait_recv()
        for cp in cps:
            cp.wait_send()

    return pl.pallas_call(
        body, name="gather_sibling",
        out_shape=[jax.ShapeDtypeStruct(s, BF16) for s in FULL_W_SHAPES],
        in_specs=[ANY] * 4, out_specs=[ANY] * 4, input_output_aliases={0: 0, 1: 1, 2: 2, 3: 3},
        scratch_shapes=[pltpu.SemaphoreType.DMA((4, 3)), pltpu.SemaphoreType.DMA((4, 3))],
    )(*fulls)


def _half_of(t, ref, half):
    if t == 0:
        return ref.at[:, pl.ds(half * 512, 512), :]
    if t == 1:
        return ref.at[pl.ds(half * 256, 256), :]
    return ref.at[:, pl.ds(half * 512, 512)]


HALF_SHAPES = [(4, 512, 2048), (256, D), (D, 512), (D, 512)]
PIECE_SHAPES = [(512, 2048), (256, 256), (256, 512), (256, 512)]
SHARD_SHAPES = [(D, 2048), (AW, 256), (256, D), (256, D)]


def _chip_piece(t, ref, j):
    if t == 0:
        return ref.at[j]
    if t == 1:
        return ref.at[:, pl.ds(j * 256, 256)]
    return ref.at[pl.ds(j * 256, 256), :]


def _reduce_sibling_send(gs):
    def body(g0, g1, g2, g3, r0, r1, r2, r3, send_sems, recv_sems):
        x, y, c = _me()
        ins, outs = [g0, g1, g2, g3], [r0, r1, r2, r3]
        cps = []
        for t in range(4):
            cp = pltpu.make_async_remote_copy(
                src_ref=_half_of(t, ins[t], 1 - c), dst_ref=outs[t],
                send_sem=send_sems.at[t], recv_sem=recv_sems.at[t], device_id=(x, y, 1 - c), device_id_type=MESH)
            cp.start()
            cps.append(cp)
        for cp in cps:
            cp.wait_recv()
        for cp in cps:
            cp.wait_send()

    return pl.pallas_call(
        body, name="reduce_sibling", out_shape=[jax.ShapeDtypeStruct(s, BF16) for s in HALF_SHAPES],
        in_specs=[ANY] * 4, out_specs=[ANY] * 4,
        scratch_shapes=[pltpu.SemaphoreType.DMA((4,)), pltpu.SemaphoreType.DMA((4,))],
    )(*gs)


def _chip_copies(hs, lands, send_sems, recv_sems):
    x, y, c = _me()
    cps = []
    for t in range(4):
        for k, chip in enumerate(_peers(x, y)):
            pj = 2 * chip[0] + chip[1]
            cps.append(pltpu.make_async_remote_copy(
                src_ref=_chip_piece(t, hs[t], pj), dst_ref=lands[t].at[k],
                send_sem=send_sems.at[3 * t + k], recv_sem=recv_sems.at[3 * t + k],
                device_id=(*chip, c), device_id_type=MESH))
    return cps


def _reduce_chips_start(hs):
    lands = [lax.empty((3,) + s, BF16) for s in PIECE_SHAPES]

    def body(h0, h1, h2, h3, l0, l1, l2, l3, send_sems, recv_sems, t0, t1, t2, t3, t4, t5, t6, t7, token):
        for cp in _chip_copies([h0, h1, h2, h3], [l0, l1, l2, l3], send_sems, recv_sems):
            cp.start()
        token[...] = jnp.zeros_like(token)

    bufs = list(hs) + lands
    res = pl.pallas_call(
        body, name="reduce_chips_start",
        out_shape=(pltpu.SemaphoreType.DMA((12,)), pltpu.SemaphoreType.DMA((12,)),
                   *[pltpu.HBM(a.shape, a.dtype) for a in bufs], jax.ShapeDtypeStruct((8, 128), F32)),
        in_specs=[HBM] * 8, out_specs=(SEM, SEM, *[HBM] * 8, pl.BlockSpec(memory_space=pltpu.VMEM)),
        input_output_aliases={i: 2 + i for i in range(8)},
        compiler_params=pltpu.CompilerParams(has_side_effects=EFFECT),
    )(*[pltpu.with_memory_space_constraint(a, pltpu.HBM) for a in bufs])
    return res[0], res[1], list(res[2:10]), res[10]


def _reduce_chips_wait(send_sems, recv_sems, thru, after):
    def body(h0, h1, h2, h3, l0, l1, l2, l3, send_sems, recv_sems, after_ref, d0, d1, d2, d3, g0, g1, g2, g3):
        cps = _chip_copies([h0, h1, h2, h3], [l0, l1, l2, l3], send_sems, recv_sems)
        for cp in cps:
            cp.wait_send()
        for cp in cps:
            cp.wait_recv()

    res = pl.pallas_call(
        body, name="reduce_chips_wait", out_shape=tuple(pltpu.HBM(a.shape, a.dtype) for a in thru),
        in_specs=[HBM] * 8 + [SEM, SEM, ANY], out_specs=[HBM] * 8,
        input_output_aliases={i: i for i in range(8)},
        compiler_params=pltpu.CompilerParams(has_side_effects=EFFECT),
    )(*thru, send_sems, recv_sems, after)
    return list(res[4:8])


def _share_sibling(shards):
    def body(i0, i1, i2, i3, o0, o1, o2, o3, send_sems, recv_sems):
        x, y, c = _me()
        outs = [o0, o1, o2, o3]

        def half(t, ref, hf):
            if t == 0:
                return ref.at[pl.ds(hf * 512, 512), :]
            if t == 1:
                return ref.at[pl.ds(hf * 256, 256), :]
            return ref.at[:, pl.ds(hf * 512, 512)]

        cps = []
        for t in range(4):
            mine = half(t, outs[t], c)
            cp = pltpu.make_async_remote_copy(
                src_ref=mine, dst_ref=mine, send_sem=send_sems.at[t], recv_sem=recv_sems.at[t],
                device_id=(x, y, 1 - c), device_id_type=MESH)
            cp.start()
            cps.append(cp)
        for t in range(4):
            theirs = half(t, outs[t], 1 - c)
            pltpu.make_async_remote_copy(
                src_ref=theirs, dst_ref=theirs, send_sem=send_sems.at[t],
                recv_sem=recv_sems.at[t], device_id=(x, y, 1 - c), device_id_type=MESH).wait_recv()
        for cp in cps:
            cp.wait_send()

    return pl.pallas_call(
        body, name="share_sibling", out_shape=[jax.ShapeDtypeStruct(s, F32) for s in SHARD_SHAPES],
        in_specs=[ANY] * 4, out_specs=[ANY] * 4, input_output_aliases={0: 0, 1: 1, 2: 2, 3: 3},
        scratch_shapes=[pltpu.SemaphoreType.DMA((4,)), pltpu.SemaphoreType.DMA((4,))],
    )(*shards)


def _half_blockspec(t, idx_pos):
    if t == 0:
        return pl.BlockSpec((1, 512, 2048), lambda i, s: (i, s[idx_pos], 0)), 4
    if t == 1:
        return pl.BlockSpec((256, D), lambda i, s: (s[idx_pos], 0)), 1
    return pl.BlockSpec((256, 512), lambda i, s: (i, s[idx_pos])), 4


def _half_out_blockspec(t):
    if t == 0:
        return pl.BlockSpec((1, 512, 2048), lambda i, s: (i, 0, 0))
    if t == 1:
        return pl.BlockSpec((256, D), lambda i, s: (0, 0))
    return pl.BlockSpec((256, 512), lambda i, s: (i, 0))


def _add_half_call(t, own, recv, sc, name):
    in_blk, steps = _half_blockspec(t, 0)
    out_blk = _half_out_blockspec(t)

    def body(s_ref, a_ref, b_ref, o_ref, ob_ref):
        v = a_ref[...] + b_ref[...].astype(F32)
        o_ref[...] = v
        ob_ref[...] = v.astype(BF16)

    return pl.pallas_call(
        body, name=name,
        grid_spec=pltpu.PrefetchScalarGridSpec(
            num_scalar_prefetch=1, grid=(steps,), in_specs=[in_blk, out_blk], out_specs=[out_blk, out_blk]),
        out_shape=[jax.ShapeDtypeStruct(HALF_SHAPES[t], F32), jax.ShapeDtypeStruct(HALF_SHAPES[t], BF16)],
        compiler_params=_cp(),
    )(sc, own, recv)


def _final_piece_call(t, chipsum, recv3, sc, name):
    ps = PIECE_SHAPES[t]
    if t == 0:
        own_blk = pl.BlockSpec((1,) + ps, lambda i, s: (s[1], 0, 0))
        o_blk = pl.BlockSpec(ps, lambda i, s: (s[0], 0))
    elif t == 1:
        own_blk = pl.BlockSpec(ps, lambda i, s: (0, s[1]))
        o_blk = pl.BlockSpec(ps, lambda i, s: (s[0], 0))
    else:
        own_blk = pl.BlockSpec(ps, lambda i, s: (s[1], 0))
        o_blk = pl.BlockSpec(ps, lambda i, s: (0, s[0]))
    r_blk = pl.BlockSpec((3,) + ps, lambda i, s: (0, 0, 0))

    def body(s_ref, a_ref, r_ref, o_ref):
        a = a_ref[0] if t == 0 else a_ref[...]
        o_ref[...] = ((a + r_ref[0].astype(F32)) + r_ref[1].astype(F32)) + r_ref[2].astype(F32)

    return pl.pallas_call(
        body, name=name,
        grid_spec=pltpu.PrefetchScalarGridSpec(
            num_scalar_prefetch=1, grid=(1,), in_specs=[own_blk, r_blk], out_specs=o_blk),
        out_shape=jax.ShapeDtypeStruct(SHARD_SHAPES[t], F32), compiler_params=_cp(),
    )(sc, chipsum, recv3)


FULL_W_SHAPES = [(D, NPROJ), (AW, D), (D, D), (D, D)]


def _cast_place_call(t, shard, sc, name):
    if t == 0:
        blk, steps = (512, 2048), 2
        in_blk = pl.BlockSpec(blk, lambda i, s: (i, 0))
        o_blk = pl.BlockSpec(blk, lambda i, s: (i, s[1]))
    elif t == 1:
        blk, steps = (AW, 256), 1
        in_blk = pl.BlockSpec(blk, lambda i, s: (0, 0))
        o_blk = pl.BlockSpec(blk, lambda i, s: (0, s[1]))
    else:
        blk, steps = (256, D), 1
        in_blk = pl.BlockSpec(blk, lambda i, s: (0, 0))
        o_blk = pl.BlockSpec(blk, lambda i, s: (s[1], 0))

    def body(s_ref, a_ref, o_ref, own_ref):
        v = a_ref[...].astype(BF16)
        o_ref[...] = v
        own_ref[...] = v

    return pl.pallas_call(
        body, name=name,
        grid_spec=pltpu.PrefetchScalarGridSpec(
            num_scalar_prefetch=1, grid=(steps,), in_specs=[in_blk], out_specs=[o_blk, in_blk]),
        out_shape=[jax.ShapeDtypeStruct(FULL_W_SHAPES[t], BF16), jax.ShapeDtypeStruct(shard.shape, BF16)],
        compiler_params=_cp(),
    )(sc, shard)


def _lower_bound_fn(hgrn_lb):
    return jnp.cumsum(jax.nn.softmax(hgrn_lb.astype(F32), axis=0), axis=0)[0]


def kernel(x, c, w_ada, b_ada, norm_g, w_in, hgrn_onorm_g, w_branch_a, w_branch_b, w_out, rel_bias, hgrn_lb, final_g, loss_target, m_w_ada, m_b_ada, m_norm_g, m_w_in, m_hgrn_onorm_g, m_w_branch_a, m_w_branch_b, m_w_out, m_rel_bias, m_hgrn_lb, m_final_g, v_w_ada, v_b_ada, v_norm_g, v_w_in, v_hgrn_onorm_g, v_w_branch_a, v_w_branch_b, v_w_out, v_rel_bias, v_hgrn_lb, v_final_g):
    ax, ay, ac = _me()
    chip = 2 * ax + ay
    dev = 4 * ax + 2 * ay + ac
    sc_idx = jnp.stack([ac, chip]).astype(jnp.int32)

    c_all = _allgather_small(jnp.pad(c, ((0, 7), (0, 0))), "gather_c").reshape(8, 8, D)[:, 0]
    b_s = lax.dynamic_slice(b_ada, (0, chip * 768), (1, 768))
    mod_part = _mod_call(c_all, w_ada[0], b_s)
    mod_all = _allgather_small(mod_part, "gather_mod").reshape(8, 8, 768)
    mod_mine = lax.dynamic_index_in_dim(mod_all, dev, axis=1, keepdims=False)
    mod = mod_mine[0::2].reshape(1, 3 * D)

    names = ["w_in", "w_a", "w_b", "w_o"]
    shards, mod = lax.optimization_barrier(([w_in[0], w_branch_a[0], w_branch_b[0], w_out[0]], mod))
    placed = [_cast_place_call(t, shards[t], sc_idx, "cast_" + names[t]) for t in range(4)]
    w_send_sems, w_recv_sems, w_thru, w_token = _gather_ici_start([p_[0] for p_ in placed])
    mod = mod + w_token[0, 0]
    shift, scale, gate = mod[:, :D], mod[:, D:2 * D], mod[:, 2 * D:]

    def weights_fn(h):
        proj = _proj_own_call(h, placed[0][1], sc_idx)
        arrived = _gather_ici_wait(w_send_sems, w_recv_sems, w_thru, proj)
        win_f, wa_f, wb_f, wo_f = _gather_sibling(arrived)
        return _proj_rest_call(h, win_f, proj, sc_idx), win_f, wa_f, wb_f, wo_f

    flight = {}

    def start_reduction(own, own16):
        sib = _reduce_sibling_send(own16)
        halves = [_add_half_call(t, own[t], sib[t], sc_idx, "chipsum_" + names[t]) for t in range(4)]
        send_sems, recv_sems, thru, token = _reduce_chips_start([hb for _, hb in halves])
        flight.update(sems=(send_sems, recv_sems), thru=thru, sums=[hf for hf, _ in halves])
        return token[0, 0]

    lb, lb_vjp = jax.vjp(_lower_bound_fn, hgrn_lb)
    grad_x, d_win, d_wa, d_wb, d_wo, pack = _local_step(
        x[0], loss_target[0], shift, scale, gate, norm_g, hgrn_onorm_g, rel_bias, lb[None, :],
        final_g[None, :], weights_fn, hook=start_reduction)

    rec = _reduce_chips_wait(*flight["sems"], flight["thru"], pack)
    pieces = [_final_piece_call(t, flight["sums"][t], rec[t], sc_idx, "piece_" + names[t]) for t in range(4)]
    g_win, g_wa, g_wb, g_wo = _share_sibling(pieces)

    packs = _allgather_small(pack, "gather_small").reshape(8, 16, D)
    tot = _sum8_call(packs)
    loss = tot[7, 0]
    g_b_ada = tot[0:3].reshape(1, 3 * D)
    g_norm_g = tot[3:4]
    g_final_g = tot[4]
    (g_hgrn_lb,) = lb_vjp(tot[5])
    g_onorm = tot[6:7, :HK]
    g_rel = tot[8, :NBUCKETS * NH].reshape(NBUCKETS, NH)

    def rows_of(a):
        flat = a.reshape(-1)
        n = -(-flat.shape[0] // D)
        return jnp.pad(flat, (0, n * D - flat.shape[0])).reshape(n, D)

    smalls = [(b_ada, g_b_ada, m_b_ada, v_b_ada), (norm_g, g_norm_g, m_norm_g, v_norm_g),
              (hgrn_onorm_g, g_onorm, m_hgrn_onorm_g, v_hgrn_onorm_g), (rel_bias, g_rel, m_rel_bias, v_rel_bias),
              (hgrn_lb, g_hgrn_lb, m_hgrn_lb, v_hgrn_lb), (final_g, g_final_g, m_final_g, v_final_g)]
    cat = [jnp.concatenate([rows_of(s[k]) for s in smalls], 0) for k in range(4)]
    cat = [jnp.pad(a, ((0, 16 - a.shape[0]), (0, 0))) for a in cat]
    sd, sm, sv = _adamw_call(*cat, "adamw_small")

    def unpack(packed):
        res, r = [], 0
        for s in smalls:
            n = -(-s[0].size // D)
            res.append(packed[r:r + n].reshape(-1)[:s[0].size].reshape(s[0].shape))
            r += n
        return res

    d_small, m_small, v_small = unpack(sd), unpack(sm), unpack(sv)

    sc_all = c_all * jax.nn.sigmoid(c_all)
    dmod_all = packs[:, 0:3].reshape(8, 3 * D)
    dm_s = lax.dynamic_slice(dmod_all, (0, chip * 768), (8, 768))
    g_w_ada, d_w_ada, nm_w_ada, nv_w_ada = _ada_update_call(sc_all.T, dm_s, w_ada[0], m_w_ada[0], v_w_ada[0])

    big = []
    for w, g, m, v, n in [(w_in, g_win, m_w_in, v_w_in, "w_in"), (w_branch_a, g_wa, m_w_branch_a, v_w_branch_a, "w_a"),
                          (w_branch_b, g_wb, m_w_branch_b, v_w_branch_b, "w_b"), (w_out, g_wo, m_w_out, v_w_out, "w_o")]:
        big.append(_adamw_call(w[0], g, m[0], v[0], "adamw_" + n))

    e = lambda a: a[None]
    grads = [e(g_w_ada), g_b_ada, g_norm_g, e(g_win), g_onorm, e(g_wa), e(g_wb), e(g_wo), g_rel, g_hgrn_lb, g_final_g]
    deltas = [e(d_w_ada), d_small[0], d_small[1], e(big[0][0]), d_small[2], e(big[1][0]), e(big[2][0]), e(big[3][0]),
              d_small[3], d_small[4], d_small[5]]
    new_m = [e(nm_w_ada), m_small[0], m_small[1], e(big[0][1]), m_small[2], e(big[1][1]), e(big[2][1]), e(big[3][1]),
             m_small[3], m_small[4], m_small[5]]
    new_v = [e(nv_w_ada), v_small[0], v_small[1], e(big[0][2]), v_small[2], e(big[1][2]), e(big[2][2]), e(big[3][2]),
             v_small[3], v_small[4], v_small[5]]
    return (loss, grad_x[None], *grads, *deltas, *new_m, *new_v)
```

```python
import functools
import math

import numpy as np
import jax
import jax.numpy as jnp
from jax import lax
from jax.experimental import pallas as pl
from jax.experimental.pallas import tpu as pltpu

D = 1024
AW = 512
NH = 8
HE = 64
HK = 128
NPROJ = 8192
ABLK = 128
PATTERNS = (1, 4, 16)
NBUCKETS = 32
MAXDIST = 2048
NEG = -1e30
EPS = 1e-6
CH = 64
LR, B1, B2, AEPS, WD, STEP = 0.001, 0.9, 0.999, 1e-08, 0.01, 10

F32 = jnp.float32
BF16 = jnp.bfloat16
MESH = pl.DeviceIdType.MESH
VMEM_LIMIT = 56 * 1024 * 1024


def _cp(**kw):
    return pltpu.CompilerParams(vmem_limit_bytes=VMEM_LIMIT, **kw)


def _sig(x):
    return 0.5 * jnp.tanh(0.5 * x) + 0.5


def _nt(a, b):
    return lax.dot_general(a, b, (((1,), (1,)), ((), ())), preferred_element_type=F32)


def _tn(a, b):
    return lax.dot_general(a, b, (((0,), (0,)), ((), ())), preferred_element_type=F32)


def _nn(a, b):
    return jnp.dot(a, b, preferred_element_type=F32)


def _split2(x):
    h = x.astype(BF16)
    return h, (x - h.astype(F32)).astype(BF16)


def _exact_mm(tri_bf16, x):
    h, l = _split2(x)
    return _nn(tri_bf16, h) + _nn(tri_bf16, l)


def _exact_mm_r(x, ones_bf16):
    h, l = _split2(x)
    return _nn(h, ones_bf16) + _nn(l, ones_bf16)


def _h_call(x, avec):
    S = x.shape[0]
    tm = 512

    def body(x_ref, a_ref, h_ref, ht_ref):
        xv = x_ref[...]
        r = lax.rsqrt(jnp.mean(xv * xv, axis=-1, keepdims=True) + EPS)
        hv = xv * r * a_ref[0:1, :] + a_ref[1:2, :]
        h_ref[...] = hv.astype(BF16)
        ht_ref[...] = hv.T.astype(BF16)

    return pl.pallas_call(
        body, name="h_norm", grid=(S // tm,),
        in_specs=[pl.BlockSpec((tm, D), lambda i: (i, 0)), pl.BlockSpec((8, D), lambda i: (0, 0))],
        out_specs=[pl.BlockSpec((tm, D), lambda i: (i, 0)), pl.BlockSpec((D, tm), lambda i: (0, i))],
        out_shape=[jax.ShapeDtypeStruct((S, D), BF16), jax.ShapeDtypeStruct((D, S), BF16)], compiler_params=_cp(),
    )(x, avec)


def _proj_call(h, w_in):
    S = h.shape[0]
    tm, tn = 512, 2048

    def body(h_ref, w_ref, o_ref):
        o_ref[...] = _nn(h_ref[...], w_ref[...])

    return pl.pallas_call(
        body, name="in_proj", grid=(NPROJ // tn, S // tm),
        in_specs=[pl.BlockSpec((tm, D), lambda j, i: (i, 0)), pl.BlockSpec((D, tn), lambda j, i: (0, j))],
        out_specs=pl.BlockSpec((tm, tn), lambda j, i: (i, j)),
        out_shape=jax.ShapeDtypeStruct((S, NPROJ), F32), compiler_params=_cp(),
    )(h, w_in)


def _proj_own_call(x, avec, w_own, sc):
    S = x.shape[0]
    tm, tn = 512, 2048

    def body(s_ref, x_ref, a_ref, w_ref, o_ref, h_ref, ht_ref):
        xv = x_ref[...]
        r = lax.rsqrt(jnp.mean(xv * xv, axis=-1, keepdims=True) + EPS)
        hv = xv * r * a_ref[0:1, :] + a_ref[1:2, :]
        hb = hv.astype(BF16)
        h_ref[...] = hb
        ht_ref[...] = hv.T.astype(BF16)
        o_ref[...] = _nn(hb, w_ref[...])

    return pl.pallas_call(
        body, name="in_proj_own",
        grid_spec=pltpu.PrefetchScalarGridSpec(
            num_scalar_prefetch=1, grid=(S // tm,),
            in_specs=[pl.BlockSpec((tm, D), lambda i, s: (i, 0)), pl.BlockSpec((8, D), lambda i, s: (0, 0)),
                      pl.BlockSpec((D, tn), lambda i, s: (0, 0))],
            out_specs=[pl.BlockSpec((tm, tn), lambda i, s: (i, s[1])), pl.BlockSpec((tm, D), lambda i, s: (i, 0)),
                       pl.BlockSpec((D, tm), lambda i, s: (0, i))]),
        out_shape=[jax.ShapeDtypeStruct((S, NPROJ), F32), jax.ShapeDtypeStruct((S, D), BF16),
                   jax.ShapeDtypeStruct((D, S), BF16)],
        compiler_params=_cp(),
    )(sc, x, avec, w_own)


def _proj_rest_call(h, w_in, proj, sc):
    S = h.shape[0]
    tm, tn = 512, 2048

    def body(s_ref, h_ref, w_ref, p_ref, o_ref):
        o_ref[...] = _nn(h_ref[...], w_ref[...])

    col = lambda j, s: (s[1] + 1 + j) % 4
    return pl.pallas_call(
        body, name="in_proj_rest",
        grid_spec=pltpu.PrefetchScalarGridSpec(
            num_scalar_prefetch=1, grid=(3, S // tm),
            in_specs=[pl.BlockSpec((tm, D), lambda j, i, s: (i, 0)),
                      pl.BlockSpec((D, tn), lambda j, i, s: (0, col(j, s))),
                      pl.BlockSpec(memory_space=pl.ANY)],
            out_specs=pl.BlockSpec((tm, tn), lambda j, i, s: (i, col(j, s)))),
        out_shape=jax.ShapeDtypeStruct((S, NPROJ), F32), input_output_aliases={3: 0}, compiler_params=_cp(),
    )(sc, h, w_in, proj)


def _t5_bucket_np(dist):
    max_exact = NBUCKETS // 2
    n = dist.astype(np.float32)
    large = max_exact + (np.log(np.maximum(n, np.float32(1.0)) / np.float32(max_exact))
                         / np.float32(math.log(MAXDIST / max_exact))
                         * np.float32(NBUCKETS - max_exact)).astype(np.int32)
    large = np.minimum(large, NBUCKETS - 1)
    return np.where(dist < max_exact, dist, large)


def _band_bucket(d):
    qi = np.arange(ABLK)[:, None]
    kj = np.arange(2 * ABLK)[None, :]
    delta = qi + ABLK - kj
    band = (delta >= 0) & (delta <= ABLK)
    bucket = _t5_bucket_np(np.clip(delta, 0, None) * d)
    return band, bucket


def _bias_tiles(rel_bias):
    tiles = []
    for d in PATTERNS:
        band, bucket = _band_bucket(d)
        onehot = (jnp.asarray(bucket, jnp.int32)[None] == jnp.arange(NBUCKETS, dtype=jnp.int32)[:, None, None])
        bias = jnp.einsum("bqk,bh->hqk", onehot.astype(F32), rel_bias, precision=lax.Precision.HIGHEST)
        tiles.append(jnp.where(jnp.asarray(band)[None], bias, NEG))
    return jnp.stack(tiles, 0)


ATT = 2048
HP = 2 * HE
AGRP = 4
AGRP_B = 2


def _attn_blocks():
    out = []
    for p, d in enumerate(PATTERNS):
        for r in range(d):
            for n in range(ATT // (d * ABLK)):
                out.append((p, d, r, n))
    return out


def _attn_fwd_call(proj, biases):
    S = proj.shape[0]
    nt = S // ATT

    def body(q_ref, k_ref, v_ref, z_ref, b_ref, a_ref, l_ref, oa_ref, kc, vc, op, lp):
        i = pl.program_id(1)

        @pl.when(i == 0)
        def _():
            kc[0:ATT] = jnp.zeros((ATT, HP), F32)
            vc[0:ATT] = jnp.zeros((ATT, HP), F32)

        @pl.when(i > 0)
        def _():
            kc[0:ATT] = kc[ATT:2 * ATT]
            vc[0:ATT] = vc[ATT:2 * ATT]

        kc[ATT:2 * ATT] = k_ref[...]
        vc[ATT:2 * ATT] = v_ref[...]
        col = lax.broadcasted_iota(jnp.int32, (ABLK, 2 * ABLK), 1)
        dead = jnp.logical_and(i == 0, col < ABLK)
        blocks = _attn_blocks()
        hs = (slice(0, HE), slice(HE, 2 * HE))
        for g0 in range(0, len(blocks), AGRP):
            grp = blocks[g0:g0 + AGRP]
            qrows = [pl.ds(n * ABLK * d + r, ABLK, stride=d) for p, d, r, n in grp]
            krows = [pl.ds(ATT + (n - 1) * ABLK * d + r, 2 * ABLK, stride=d) for p, d, r, n in grp]
            qs = [(q_ref[qr, :] * (HE ** -0.5)).astype(BF16) for qr in qrows]
            ks = [kc[kr, :].astype(BF16) for kr in krows]
            vs = [vc[kr, :].astype(BF16) for kr in krows]
            ss = [[_nt(qs[b][:, sl], ks[b][:, sl]) + b_ref[grp[b][0], e] for e, sl in enumerate(hs)]
                  for b in range(len(grp))]
            ss = [[jnp.where(dead, NEG, s) if grp[b][3] == 0 else s for s in ss[b]] for b in range(len(grp))]
            mxs = [[jnp.max(s, axis=-1, keepdims=True) for s in sb] for sb in ss]
            pes = [[jnp.exp(s - mx) for s, mx in zip(sb, mb)] for sb, mb in zip(ss, mxs)]
            dens = [[jnp.sum(pe, axis=-1, keepdims=True) for pe in pb] for pb in pes]
            pvs = [[_nn(pe.astype(BF16), vs[b][:, sl]) for pe, sl in zip(pes[b], hs)] for b in range(len(grp))]
            for b in range(len(grp)):
                p, d, r, n = grp[b]
                prow = pl.ds(p * ATT + n * ABLK * d + r, ABLK, stride=d)
                lp[prow, :] = jnp.concatenate(
                    [jnp.broadcast_to(mx + jnp.log(dn), (ABLK, HE)) for mx, dn in zip(mxs[b], dens[b])], axis=1)
                op[prow, :] = jnp.concatenate([pv / dn for pv, dn in zip(pvs[b], dens[b])], axis=1)
        rt = 256
        for t in range(ATT // rt):
            rows = slice(t * rt, (t + 1) * rt)
            pr = [slice(p * ATT + t * rt, p * ATT + (t + 1) * rt) for p in range(3)]
            la, lb_, lc = lp[pr[0], :], lp[pr[1], :], lp[pr[2], :]
            m = jnp.maximum(jnp.maximum(la, lb_), lc)
            ea, eb, ec = jnp.exp(la - m), jnp.exp(lb_ - m), jnp.exp(lc - m)
            den = ea + eb + ec
            att = (ea * op[pr[0], :] + eb * op[pr[1], :] + ec * op[pr[2], :]) / den
            a_ref[rows, :] = att
            l_ref[rows, :] = m + jnp.log(den)
            z = z_ref[rows, :]
            oa_ref[rows, :] = (att * (z * _sig(z))).astype(BF16)

    def pcol(c):
        return pl.BlockSpec((ATT, HP), lambda h, i: (i, c * 4 + h))

    out = pl.BlockSpec((ATT, HP), lambda h, i: (i, h))
    return pl.pallas_call(
        body, name="attn_fwd", grid=(4, nt),
        in_specs=[pcol(0), pcol(1), pcol(2), pcol(3),
                  pl.BlockSpec((3, 2, ABLK, 2 * ABLK), lambda h, i: (0, h, 0, 0))],
        out_specs=[out, out, out],
        out_shape=[jax.ShapeDtypeStruct((S, AW), F32), jax.ShapeDtypeStruct((S, AW), F32),
                   jax.ShapeDtypeStruct((S, AW), BF16)],
        scratch_shapes=[pltpu.VMEM((2 * ATT, HP), F32), pltpu.VMEM((2 * ATT, HP), F32),
                        pltpu.VMEM((3 * ATT, HP), F32), pltpu.VMEM((3 * ATT, HP), F32)],
        compiler_params=_cp(),
    )(proj, proj, proj, proj, biases)


def _attn_bwd_call(proj, dattn, lse, dsum, biases):
    S = proj.shape[0]
    nt = S // ATT

    def body(q_ref, k_ref, v_ref, do_ref, l_ref, ds_ref, b_ref, dq_ref, dk_ref, dv_ref, db_ref,
             kc, vc, dkc, dvc, dqa):
        i = pl.program_id(1)

        @pl.when(i == 0)
        def _():
            kc[ATT:2 * ATT] = jnp.zeros((ATT, HP), F32)
            vc[ATT:2 * ATT] = jnp.zeros((ATT, HP), F32)
            dkc[ATT:2 * ATT] = jnp.zeros((ATT, HP), F32)
            dvc[ATT:2 * ATT] = jnp.zeros((ATT, HP), F32)
            db_ref[...] = jnp.zeros_like(db_ref)

        @pl.when(i < nt)
        def _():
            kc[0:ATT] = kc[ATT:2 * ATT]
            vc[0:ATT] = vc[ATT:2 * ATT]
            dkc[0:ATT] = dkc[ATT:2 * ATT]
            dvc[0:ATT] = dvc[ATT:2 * ATT]
            kc[ATT:2 * ATT] = k_ref[...]
            vc[ATT:2 * ATT] = v_ref[...]
            dkc[ATT:2 * ATT] = jnp.zeros((ATT, HP), F32)
            dvc[ATT:2 * ATT] = jnp.zeros((ATT, HP), F32)
            col = lax.broadcasted_iota(jnp.int32, (ABLK, 2 * ABLK), 1)
            dead = jnp.logical_and(i == 0, col < ABLK)
            blocks = _attn_blocks()
            hs = (slice(0, HE), slice(HE, 2 * HE))
            for g0 in range(0, len(blocks), AGRP_B):
                grp = blocks[g0:g0 + AGRP_B]
                nb_ = range(len(grp))
                qrows = [pl.ds(n * ABLK * d + r, ABLK, stride=d) for p, d, r, n in grp]
                krows = [pl.ds(ATT + (n - 1) * ABLK * d + r, 2 * ABLK, stride=d) for p, d, r, n in grp]
                qs = [(q_ref[qr, :] * (HE ** -0.5)).astype(BF16) for qr in qrows]
                ks = [kc[kr, :].astype(BF16) for kr in krows]
                vs = [vc[kr, :].astype(BF16) for kr in krows]
                dos = [do_ref[qr, :].astype(BF16) for qr in qrows]
                lvs = [l_ref[qr, :] for qr in qrows]
                dsvs = [ds_ref[qr, :] for qr in qrows]
                ss = [[_nt(qs[b][:, sl], ks[b][:, sl]) + b_ref[grp[b][0], e] for e, sl in enumerate(hs)] for b in nb_]
                ss = [[jnp.where(dead, NEG, s) if grp[b][3] == 0 else s for s in ss[b]] for b in nb_]
                dps = [[_nt(dos[b][:, sl], vs[b][:, sl]) for sl in hs] for b in nb_]
                pes = [[jnp.exp(ss[b][e] - lvs[b][:, e * HE:e * HE + 1]) for e in range(2)] for b in nb_]
                dscs = [[pes[b][e] * (dps[b][e] - dsvs[b][:, e * HE:e * HE + 1]) for e in range(2)] for b in nb_]
                for b in nb_:
                    for e in range(2):
                        db_ref[grp[b][0], e] += dscs[b][e]
                dsbs = [[t.astype(BF16) for t in tb] for tb in dscs]
                dqs = [[_nn(dsbs[b][e], ks[b][:, sl]) * (HE ** -0.5) for e, sl in enumerate(hs)] for b in nb_]
                dks = [[_tn(dsbs[b][e], qs[b][:, sl]) for e, sl in enumerate(hs)] for b in nb_]
                dvs = [[_tn(pes[b][e].astype(BF16), dos[b][:, sl]) for e, sl in enumerate(hs)] for b in nb_]
                for b in nb_:
                    dq = jnp.concatenate(dqs[b], axis=1)
                    if grp[b][0] == 0:
                        dqa[qrows[b], :] = dq
                    else:
                        dqa[qrows[b], :] += dq
                    dkc[krows[b], :] += jnp.concatenate(dks[b], axis=1)
                    dvc[krows[b], :] += jnp.concatenate(dvs[b], axis=1)
            dq_ref[...] = dqa[...].astype(BF16)
            dk_ref[...] = dkc[0:ATT].astype(BF16)
            dv_ref[...] = dvc[0:ATT].astype(BF16)

        @pl.when(i == nt)
        def _():
            dk_ref[...] = dkc[ATT:2 * ATT].astype(BF16)
            dv_ref[...] = dvc[ATT:2 * ATT].astype(BF16)

    def pcol(c):
        return pl.BlockSpec((ATT, HP), lambda h, i: (jnp.minimum(i, nt - 1), c * 4 + h))

    qrow = pl.BlockSpec((ATT, HP), lambda h, i: (jnp.minimum(i, nt - 1), h))
    krow = pl.BlockSpec((ATT, HP), lambda h, i: (jnp.maximum(i - 1, 0), h))
    bspec = pl.BlockSpec((3, 2, ABLK, 2 * ABLK), lambda h, i: (0, h, 0, 0))
    return pl.pallas_call(
        body, name="attn_bwd", grid=(4, nt + 1),
        in_specs=[pcol(0), pcol(1), pcol(2), qrow, qrow, qrow, bspec],
        out_specs=[qrow, krow, krow, bspec],
        out_shape=[jax.ShapeDtypeStruct((S, AW), BF16)] * 3
                  + [jax.ShapeDtypeStruct((3, NH, ABLK, 2 * ABLK), F32)],
        scratch_shapes=[pltpu.VMEM((2 * ATT, HP), F32)] * 4 + [pltpu.VMEM((ATT, HP), F32)],
        compiler_params=_cp(),
    )(proj, proj, proj, dattn, lse, dsum, biases)


HRB = 256


def _hgrn_gates(q_ref, f_ref, rows, lbv, tri):
    qraw = q_ref[rows, :]
    sq = _sig(qraw)
    q = qraw * sq
    sf = _sig(f_ref[rows, :])
    f = lbv + (1.0 - lbv) * sf
    k = 1.0 - f
    b = _exact_mm(tri, jnp.log(f))
    bl = b[CH - 1:CH, :]
    bm = b[CH // 2 - 1:CH // 2, :]
    e1 = jnp.exp(b - bm)
    e2 = jnp.exp(bm - b)
    ebm = jnp.exp(bm)
    eblm = jnp.exp(bl - bm)
    qs = q * e1
    ks = k * e2
    qe = qs * ebm
    kd = ks * eblm
    return dict(qraw=qraw, sq=sq, sf=sf, f=f, bl=bl, e1=e1, e2=e2, ebm=ebm, eblm=eblm, qe=qe, qs=qs, ks=ks, kd=kd)


def _tri_masks():
    row = lax.broadcasted_iota(jnp.int32, (CH, CH), 0)
    col = lax.broadcasted_iota(jnp.int32, (CH, CH), 1)
    return row >= col


def _hgrn_fwd_call(proj, lb, gn):
    S = proj.shape[0]
    nc = S // CH
    cps = HRB // CH

    def body(q_ref, f_ref, i_ref, z_ref, lb_ref, gn_ref, or_ref, ob_ref, st_ref, st):
        @pl.when(pl.program_id(0) == 0)
        def _():
            st[...] = jnp.zeros_like(st)

        low = _tri_masks()
        tri = low.astype(BF16)
        lbv = lb_ref[...]
        for ci in range(cps):
            rows = slice(ci * CH, (ci + 1) * CH)
            g = _hgrn_gates(q_ref, f_ref, rows, lbv, tri)
            v = i_ref[rows, :]
            ebl = jnp.exp(g["bl"])
            st_ref[ci] = st[...]
            hs = [slice(hh * HK, (hh + 1) * HK) for hh in range(NH)]
            vb = v.astype(BF16)
            qsb, ksb, qeb, kdb = (g[n_].astype(BF16) for n_ in ("qs", "ks", "qe", "kd"))
            s0s = [st[sl, :] for sl in hs]
            as_ = [_nt(qsb[:, sl], ksb[:, sl]) for sl in hs]
            ois = [_nt(qeb[:, sl], s0.astype(BF16)) for sl, s0 in zip(hs, s0s)]
            sts = [_tn(vb[:, sl], kdb[:, sl]) for sl in hs]
            abs_ = [jnp.where(low, a, 0.0).astype(BF16) for a in as_]
            os_ = [oi + _nn(a, vb[:, sl]) for oi, a, sl in zip(ois, abs_, hs)]
            for sl, s0, sn in zip(hs, s0s, sts):
                st[sl, :] = s0 * ebl[:, sl] + sn
            o = jnp.concatenate(os_, axis=1)
            or_ref[rows, :] = o
            rs = [lax.rsqrt(jnp.mean(oh * oh, axis=-1, keepdims=True) + EPS) for oh in os_]
            on = jnp.concatenate([oh * r for oh, r in zip(os_, rs)], axis=1)
            z = z_ref[rows, :]
            ob_ref[rows, :] = (on * gn_ref[...] * (z * _sig(z))).astype(BF16)

    def pcol(c):
        return pl.BlockSpec((HRB, D), lambda i: (i, c))

    vec = pl.BlockSpec((1, D), lambda i: (0, 0))
    row = pl.BlockSpec((HRB, D), lambda i: (i, 0))
    return pl.pallas_call(
        body, name="hgrn_fwd", grid=(S // HRB,),
        in_specs=[pcol(2), pcol(3), pcol(4), pcol(5), vec, vec],
        out_specs=[row, row, pl.BlockSpec((cps, NH * HK, HK), lambda i: (i, 0, 0))],
        out_shape=[jax.ShapeDtypeStruct((S, D), F32), jax.ShapeDtypeStruct((S, D), BF16),
                   jax.ShapeDtypeStruct((nc, NH * HK, HK), F32)],
        scratch_shapes=[pltpu.VMEM((NH * HK, HK), F32)],
        compiler_params=_cp(),
    )(proj, proj, proj, proj, lb, gn)


def _hgrn_bwd_call(proj, oraw, dob, states, lb, gn):
    S = proj.shape[0]
    nblk = S // HRB
    cps = HRB // CH

    def body(q_ref, f_ref, i_ref, z_ref, or_ref, dob_ref, st_ref, lb_ref, gn_ref, dh_ref, acc_ref, dst):
        @pl.when(pl.program_id(0) == 0)
        def _():
            dst[...] = jnp.zeros_like(dst)
            acc_ref[...] = jnp.zeros_like(acc_ref)

        low = _tri_masks()
        tri = low.astype(BF16)
        triu = jnp.logical_not(_tri_masks()) | (lax.broadcasted_iota(jnp.int32, (CH, CH), 0)
                                               == lax.broadcasted_iota(jnp.int32, (CH, CH), 1))
        triu = triu.astype(BF16)
        lbv = lb_ref[...]
        for ci in reversed(range(cps)):
            rows = slice(ci * CH, (ci + 1) * CH)
            g = _hgrn_gates(q_ref, f_ref, rows, lbv, tri)
            v = i_ref[rows, :]
            ebl = jnp.exp(g["bl"])
            hs = [slice(hh * HK, (hh + 1) * HK) for hh in range(NH)]
            cat = lambda lst: jnp.concatenate(lst, axis=1)
            o = or_ref[rows, :]
            z = z_ref[rows, :]
            sz = _sig(z)
            gnv = gn_ref[...]
            dobv = dob_ref[rows, :]
            r = cat([jnp.broadcast_to(lax.rsqrt(jnp.mean(o[:, sl] * o[:, sl], axis=-1, keepdims=True) + EPS),
                                      (CH, HK)) for sl in hs])
            don = dobv * (z * sz)
            dz = dobv * (o * r * gnv) * (sz * (1.0 + z * (1.0 - sz)))
            dgn = jnp.sum(don * o * r, axis=0, keepdims=True)
            gh = don * gnv
            gho = gh * o
            mg = cat([jnp.broadcast_to(jnp.mean(gho[:, sl], axis=-1, keepdims=True), (CH, HK)) for sl in hs])
            dob16 = (r * gh - o * (r * r * r * mg)).astype(BF16)

            vb = v.astype(BF16)
            qsb, ksb, qeb, kdb = (g[n_].astype(BF16) for n_ in ("qs", "ks", "qe", "kd"))
            st0s = [st_ref[ci, sl, :] for sl in hs]
            dst1s = [dst[sl, :] for sl in hs]
            dst1bs = [t.astype(BF16) for t in dst1s]
            as_ = [_nt(qsb[:, sl], ksb[:, sl]) for sl in hs]
            das_ = [_nt(dob16[:, sl], vb[:, sl]) for sl in hs]
            dqes = [_nn(dob16[:, sl], s0.astype(BF16)) for sl, s0 in zip(hs, st0s)]
            dkds = [_nn(vb[:, sl], d1) for sl, d1 in zip(hs, dst1bs)]
            dvis = [_nt(kdb[:, sl], d1) for sl, d1 in zip(hs, dst1bs)]
            dsts = [_tn(dob16[:, sl], qeb[:, sl]) for sl in hs]
            abs_ = [jnp.where(low, a, 0.0).astype(BF16) for a in as_]
            dabs_ = [jnp.where(low, a, 0.0).astype(BF16) for a in das_]
            dqss = [_nn(da, ksb[:, sl]) for da, sl in zip(dabs_, hs)]
            dkss = [_tn(da, qsb[:, sl]) for da, sl in zip(dabs_, hs)]
            dvs_ = [_tn(a, dob16[:, sl]) + dvi for a, sl, dvi in zip(abs_, hs, dvis)]
            exs_ = [jnp.sum(d1 * s0, axis=0, keepdims=True) for d1, s0 in zip(dst1s, st0s)]
            for sl, d1, dn in zip(hs, dst1s, dsts):
                dst[sl, :] = dn + d1 * ebl[:, sl]
            dqe, dqs, dks, dkd, dv = cat(dqes), cat(dqss), cat(dkss), cat(dkds), cat(dvs_)
            dq = (dqe * g["ebm"] + dqs) * g["e1"]
            dk = (dks + dkd * g["eblm"]) * g["e2"]
            dkdkd = dkd * g["kd"]
            db = dqe * g["qe"] + dqs * qsb.astype(F32) - dks * ksb.astype(F32) - dkdkd
            ex = jnp.sum(dkdkd, axis=0, keepdims=True) + cat(exs_) * ebl
            dg = _exact_mm(triu, db) + ex
            df = dg / g["f"] - dk
            sf = g["sf"]
            dfr = df * (1.0 - lbv) * sf * (1.0 - sf)
            sq = g["sq"]
            dqr = dq * (sq * (1.0 + g["qraw"] * (1.0 - sq)))
            acc_ref[0:1, :] += jnp.sum(df * (1.0 - sf), axis=0, keepdims=True)
            acc_ref[1:2, :] += dgn
            dh_ref[rows, 0:D] = dqr.astype(BF16)
            dh_ref[rows, D:2 * D] = dfr.astype(BF16)
            dh_ref[rows, 2 * D:3 * D] = dv.astype(BF16)
            dh_ref[rows, 3 * D:4 * D] = dz.astype(BF16)

    def pcol(c):
        return pl.BlockSpec((HRB, D), lambda i: (nblk - 1 - i, c))

    vec = pl.BlockSpec((1, D), lambda i: (0, 0))
    row = pl.BlockSpec((HRB, D), lambda i: (nblk - 1 - i, 0))
    return pl.pallas_call(
        body, name="hgrn_bwd", grid=(nblk,),
        in_specs=[pcol(2), pcol(3), pcol(4), pcol(5), row, row,
                  pl.BlockSpec((cps, NH * HK, HK), lambda i: (nblk - 1 - i, 0, 0)), vec, vec],
        out_specs=[pl.BlockSpec((HRB, 4 * D), lambda i: (nblk - 1 - i, 0)),
                   pl.BlockSpec((8, D), lambda i: (0, 0))],
        out_shape=[jax.ShapeDtypeStruct((S, 4 * D), BF16), jax.ShapeDtypeStruct((8, D), F32)],
        scratch_shapes=[pltpu.VMEM((NH * HK, HK), F32)],
        compiler_params=_cp(),
    )(proj, proj, proj, proj, oraw, dob, states, lb, gn)


def _fwd2_call(oa, ob, proj, x, tgt, vecs, wa, wb, wo):
    S = x.shape[0]
    tm = 256

    def body(oa_ref, ob_ref, ga_ref, gb_ref, x_ref, t_ref, v_ref, wa_ref, wb_ref, wo_ref,
             ya_ref, yb_ref, y_ref, u_ref, x2_ref, ls_ref):
        @pl.when(pl.program_id(0) == 0)
        def _():
            ls_ref[...] = jnp.zeros_like(ls_ref)

        ya = _nn(oa_ref[...], wa_ref[...])
        yb = _nn(ob_ref[...], wb_ref[...])
        y = _sig(ga_ref[...]) * ya + _sig(gb_ref[...]) * yb
        u = _nn(y.astype(BF16), wo_ref[...])
        x2 = x_ref[...] + v_ref[0:1, :] * u
        r = lax.rsqrt(jnp.mean(x2 * x2, axis=-1, keepdims=True) + EPS)
        err = x2 * r * v_ref[1:2, :] - t_ref[...]
        ls_ref[...] += jnp.sum(err * err)
        ya_ref[...] = ya.astype(BF16)
        yb_ref[...] = yb.astype(BF16)
        y_ref[...] = y.astype(BF16)
        u_ref[...] = u.astype(BF16)
        x2_ref[...] = x2

    row = pl.BlockSpec((tm, D), lambda i: (i, 0))
    full = lambda a: pl.BlockSpec(a.shape, lambda i: (0, 0))
    return pl.pallas_call(
        body, name="fwd_merge_out", grid=(S // tm,),
        in_specs=[pl.BlockSpec((tm, AW), lambda i: (i, 0)), row,
                  pl.BlockSpec((tm, D), lambda i: (i, 6)), pl.BlockSpec((tm, D), lambda i: (i, 7)),
                  row, row, pl.BlockSpec((8, D), lambda i: (0, 0)), full(wa), full(wb), full(wo)],
        out_specs=[row, row, row, row, row, pl.BlockSpec((8, 128), lambda i: (0, 0))],
        out_shape=[jax.ShapeDtypeStruct((S, D), BF16)] * 4
                  + [jax.ShapeDtypeStruct((S, D), F32), jax.ShapeDtypeStruct((8, 128), F32)],
        compiler_params=_cp(),
    )(oa, ob, proj, proj, x, tgt, vecs, wa, wb, wo)


def _bwd2_call(x2, tgt, vecs, ya, yb, u, proj, attn, wa, wb, wo, hsum):
    S = x2.shape[0]
    tm = 256

    def body(x2_ref, t_ref, v_ref, ya_ref, yb_ref, u_ref, ga_ref, gb_ref, at_ref, za_ref,
             wa_ref, wb_ref, wo_ref, hs_ref,
             dx2_ref, du_ref, dya_ref, dyb_ref, dg_ref, dat_ref, dsum_ref, dza_ref, dob_ref, acc_ref):
        @pl.when(pl.program_id(0) == 0)
        def _():
            acc_ref[...] = jnp.zeros_like(acc_ref)

        x2v = x2_ref[...]
        gate = v_ref[0:1, :]
        fg = v_ref[1:2, :]
        r = lax.rsqrt(jnp.mean(x2v * x2v, axis=-1, keepdims=True) + EPS)
        dout = (x2v * r * fg - t_ref[...]) * (1.0 / D)
        gh = dout * fg
        dx2 = r * gh - x2v * (r * r * r * jnp.mean(gh * x2v, axis=-1, keepdims=True))
        acc_ref[0:1, :] += jnp.sum(dx2 * u_ref[...].astype(F32), axis=0, keepdims=True)
        acc_ref[1:2, :] += jnp.sum(dout * x2v * r, axis=0, keepdims=True)
        dx2_ref[...] = dx2
        du = (dx2 * gate).astype(BF16)
        du_ref[...] = du
        dy = _nt(du, wo_ref[...])
        sa = _sig(ga_ref[...])
        sb = _sig(gb_ref[...])
        dya = (dy * sa).astype(BF16)
        dyb = (dy * sb).astype(BF16)
        dya_ref[...] = dya
        dyb_ref[...] = dyb
        dg_ref[:, 0:D] = (dy * ya_ref[...].astype(F32) * sa * (1.0 - sa)).astype(BF16)
        dg_ref[:, D:2 * D] = (dy * yb_ref[...].astype(F32) * sb * (1.0 - sb)).astype(BF16)
        doa = _nt(dya, wa_ref[...])
        dob_ref[...] = _nt(dyb, wb_ref[...])
        za = za_ref[...]
        sz = _sig(za)
        att = at_ref[...]
        dat = doa * (za * sz)
        dat_ref[...] = dat
        dza_ref[...] = (doa * att * (sz * (1.0 + za * (1.0 - sz)))).astype(BF16)
        dsum_ref[...] = _exact_mm_r(dat * att, hs_ref[...])

    row = pl.BlockSpec((tm, D), lambda i: (i, 0))
    arow = pl.BlockSpec((tm, AW), lambda i: (i, 0))
    full = lambda a: pl.BlockSpec(a.shape, lambda i: (0, 0))
    return pl.pallas_call(
        body, name="bwd_merge_out", grid=(S // tm,),
        in_specs=[row, row, pl.BlockSpec((8, D), lambda i: (0, 0)), row, row, row,
                  pl.BlockSpec((tm, D), lambda i: (i, 6)), pl.BlockSpec((tm, D), lambda i: (i, 7)),
                  arow, pl.BlockSpec((tm, AW), lambda i: (i, 3)), full(wa), full(wb), full(wo), full(hsum)],
        out_specs=[row, row, row, row, pl.BlockSpec((tm, 2 * D), lambda i: (i, 0)),
                   arow, arow, arow, row, pl.BlockSpec((8, D), lambda i: (0, 0))],
        out_shape=[jax.ShapeDtypeStruct((S, D), F32), jax.ShapeDtypeStruct((S, D), BF16),
                   jax.ShapeDtypeStruct((S, D), BF16), jax.ShapeDtypeStruct((S, D), BF16),
                   jax.ShapeDtypeStruct((S, 2 * D), BF16), jax.ShapeDtypeStruct((S, AW), F32),
                   jax.ShapeDtypeStruct((S, AW), F32), jax.ShapeDtypeStruct((S, AW), BF16),
                   jax.ShapeDtypeStruct((S, D), F32), jax.ShapeDtypeStruct((8, D), F32)],
        compiler_params=_cp(),
    )(x2, tgt, vecs, ya, yb, u, proj, proj, attn, proj, wa, wb, wo, hsum)


def _atb_call(a, b, name):
    S, K = a.shape
    N = b.shape[1]
    tm = min(1024, S)

    def body(a_ref, b_ref, o_ref, ob_ref):
        @pl.when(pl.program_id(0) == 0)
        def _():
            o_ref[...] = jnp.zeros_like(o_ref)

        o_ref[...] += _tn(a_ref[...], b_ref[...])

        @pl.when(pl.program_id(0) == S // tm - 1)
        def _():
            ob_ref[...] = o_ref[...].astype(BF16)

    ospec = pl.BlockSpec((K, N), lambda i: (0, 0))
    return pl.pallas_call(
        body, name=name, grid=(S // tm,),
        in_specs=[pl.BlockSpec((tm, K), lambda i: (i, 0)), pl.BlockSpec((tm, N), lambda i: (i, 0))],
        out_specs=[ospec, ospec],
        out_shape=[jax.ShapeDtypeStruct((K, N), F32), jax.ShapeDtypeStruct((K, N), BF16)], compiler_params=_cp(),
    )(a, b)


def _dwin_call(h_t, dqkvz, d_hgrn, d_gates):
    S = h_t.shape[1]
    tm = min(1024, S)
    tn = 1024

    def body(h_ref, q_ref, k_ref, v_ref, z_ref, m_ref, g_ref, o_ref, ob_ref):
        j = pl.program_id(0)

        @pl.when(pl.program_id(1) == 0)
        def _():
            o_ref[...] = jnp.zeros_like(o_ref)

        hv = h_ref[...]

        @pl.when(j == 0)
        def _():
            o_ref[0, :, 0:AW] += _nn(hv, q_ref[...])
            o_ref[0, :, AW:2 * AW] += _nn(hv, k_ref[...])

        @pl.when(j == 1)
        def _():
            o_ref[0, :, 0:AW] += _nn(hv, v_ref[...])
            o_ref[0, :, AW:2 * AW] += _nn(hv, z_ref[...])

        @pl.when(jnp.logical_and(j >= 2, j < 6))
        def _():
            o_ref[0] += _nn(hv, m_ref[...])

        @pl.when(j >= 6)
        def _():
            o_ref[0] += _nn(hv, g_ref[...])

        @pl.when(pl.program_id(1) == S // tm - 1)
        def _():
            ob_ref[...] = o_ref[...].astype(BF16)

    def aspec(jb):
        return pl.BlockSpec((tm, AW), lambda j, i: (jnp.where(j == jb, i, 0), 0))

    ospec = pl.BlockSpec((1, D, tn), lambda j, i: (j // 2, 0, j % 2))
    return pl.pallas_call(
        body, name="dw_in", grid=(8, S // tm),
        in_specs=[pl.BlockSpec((D, tm), lambda j, i: (0, i)), aspec(0), aspec(0), aspec(1), aspec(1),
                  pl.BlockSpec((tm, tn), lambda j, i: (jnp.where(jnp.logical_and(j >= 2, j < 6), i, 0),
                                                       jnp.clip(j - 2, 0, 3))),
                  pl.BlockSpec((tm, tn), lambda j, i: (jnp.where(j >= 6, i, 0), jnp.clip(j - 6, 0, 1)))],
        out_specs=[ospec, ospec],
        out_shape=[jax.ShapeDtypeStruct((4, D, 2 * tn), F32), jax.ShapeDtypeStruct((4, D, 2 * tn), BF16)],
        compiler_params=_cp(),
    )(h_t, *dqkvz, d_hgrn, d_gates)


def _dh_call(dqkvz, d_hgrn, d_gates, w_in, x, dx2, vecs):
    S = x.shape[0]
    tm = 256

    def body(q_ref, k_ref, v_ref, z_ref, m_ref, g_ref, w_ref, x_ref, dx2_ref, p_ref, gx_ref, acc_ref):
        @pl.when(pl.program_id(0) == 0)
        def _():
            acc_ref[...] = jnp.zeros_like(acc_ref)

        dhv = _nt(q_ref[...], w_ref[:, 0:AW])
        for cidx, r in enumerate((k_ref, v_ref, z_ref)):
            dhv += _nt(r[...], w_ref[:, (cidx + 1) * AW:(cidx + 2) * AW])
        dhv += _nt(m_ref[...], w_ref[:, 4 * AW:4 * AW + 4 * D])
        dhv += _nt(g_ref[...], w_ref[:, 4 * AW + 4 * D:NPROJ])
        xv = x_ref[...]
        r = lax.rsqrt(jnp.mean(xv * xv, axis=-1, keepdims=True) + EPS)
        xn = xv * r
        acc_ref[0:1, :] += jnp.sum(dhv, axis=0, keepdims=True)
        acc_ref[1:2, :] += jnp.sum(dhv * xn * p_ref[1:2, :], axis=0, keepdims=True)
        acc_ref[2:3, :] += jnp.sum(dhv * xn * p_ref[2:3, :], axis=0, keepdims=True)
        dxn = dhv * p_ref[0:1, :]
        gx_ref[...] = dx2_ref[...] + r * dxn - xv * (r * r * r * jnp.mean(dxn * xv, axis=-1, keepdims=True))

    row = pl.BlockSpec((tm, D), lambda i: (i, 0))
    aspec = pl.BlockSpec((tm, AW), lambda i: (i, 0))
    const = lambda shape: pl.BlockSpec(shape, lambda i: (0, 0))
    return pl.pallas_call(
        body, name="dh_gradx", grid=(S // tm,),
        in_specs=[aspec, aspec, aspec, aspec,
                  pl.BlockSpec((tm, 4 * D), lambda i: (i, 0)), pl.BlockSpec((tm, 2 * D), lambda i: (i, 0)),
                  pl.BlockSpec((D, NPROJ), lambda i: (0, 0), pipeline_mode=pl.Buffered(1)),
                  row, row, const((8, D))],
        out_specs=[row, const((8, D))],
        out_shape=[jax.ShapeDtypeStruct((S, D), F32), jax.ShapeDtypeStruct((8, D), F32)],
        compiler_params=_cp(),
    )(*dqkvz, d_hgrn, d_gates, w_in, x, dx2, vecs)


def _adamw_math(w, g, m, v):
    m = B1 * m + (1.0 - B1) * g
    v = B2 * v + (1.0 - B2) * (g * g)
    m_hat = m / (1.0 - B1 ** STEP)
    v_hat = v / (1.0 - B2 ** STEP)
    delta = -LR * (m_hat / (jnp.sqrt(v_hat) + AEPS) + WD * w)
    return delta, m, v


def _adamw_call(w, g, m, v, name):
    R, C = w.shape
    tr = R if R * C * 4 <= (1 << 20) else max(8, (1 << 20) // (C * 4))
    assert R % tr == 0

    def body(w_ref, g_ref, m_ref, v_ref, go_ref, d_ref, nm_ref, nv_ref):
        g = g_ref[...]
        go_ref[...] = g
        d_ref[...], nm_ref[...], nv_ref[...] = _adamw_math(w_ref[...], g, m_ref[...], v_ref[...])

    blk = pl.BlockSpec((tr, C), lambda i: (i, 0))
    return pl.pallas_call(
        body, name=name, grid=(R // tr,), in_specs=[blk] * 4, out_specs=[blk] * 4,
        out_shape=[jax.ShapeDtypeStruct((R, C), F32)] * 4, compiler_params=_cp(),
    )(w, g, m, v)


def _mod_call(c_all, w_ada_s, b_s):
    def body(c_ref, w_ref, b_ref, o_ref):
        cv = c_ref[...]
        sc = cv * _sig(cv)
        o_ref[...] = jnp.dot(sc, w_ref[...], preferred_element_type=F32,
                             precision=lax.Precision.HIGHEST) + b_ref[...]

    return pl.pallas_call(
        body, name="ada_mod", out_shape=jax.ShapeDtypeStruct((8, w_ada_s.shape[1]), F32),
        compiler_params=_cp(),
    )(c_all, w_ada_s, b_s)


def _ada_update_call(sct, dm, w, m, v):
    R, C = w.shape
    tr = 256

    def body(s_ref, d_ref, w_ref, m_ref, v_ref, g_ref, dl_ref, nm_ref, nv_ref):
        g = s_ref[:, 0:1] * d_ref[0:1, :]
        for b in range(1, 8):
            g = g + s_ref[:, b:b + 1] * d_ref[b:b + 1, :]
        g_ref[...] = g
        dl_ref[...], nm_ref[...], nv_ref[...] = _adamw_math(w_ref[...], g, m_ref[...], v_ref[...])

    blk = pl.BlockSpec((tr, C), lambda i: (i, 0))
    return pl.pallas_call(
        body, name="ada_update", grid=(R // tr,),
        in_specs=[pl.BlockSpec((tr, 8), lambda i: (i, 0)), pl.BlockSpec((8, C), lambda i: (0, 0)), blk, blk, blk],
        out_specs=[blk] * 4, out_shape=[jax.ShapeDtypeStruct((R, C), F32)] * 4, compiler_params=_cp(),
    )(sct, dm, w, m, v)


def _sum8_call(packs):
    def body(p_ref, o_ref):
        acc = p_ref[0]
        for k in range(1, 8):
            acc = acc + p_ref[k]
        o_ref[...] = acc

    return pl.pallas_call(
        body, name="sum_small", out_shape=jax.ShapeDtypeStruct(packs.shape[1:], F32), compiler_params=_cp(),
    )(packs)


def _local_step(x, tgt, shift, scale, gate, norm_g, hgrn_onorm_g, rel_bias, lb, final_g, weights_fn, hook=None):
    a = norm_g * (1.0 + scale)
    z6 = jnp.zeros((6, D), F32)
    h, h_t, proj, w_in, wa, wb, wo = weights_fn(x, jnp.concatenate([a, shift, z6], 0))

    biases = _bias_tiles(rel_bias)
    attn, lse, oa = _attn_fwd_call(proj, biases)

    gn = jnp.tile(hgrn_onorm_g, (1, NH))
    oraw, ob, states = _hgrn_fwd_call(proj, lb, gn)

    vecs2 = jnp.concatenate([gate, final_g, z6], 0)
    ya, yb, y, u, x2, lsq = _fwd2_call(oa, ob, proj, x, tgt, vecs2, wa, wb, wo)
    loss = 0.5 * lsq[0, 0] / D

    hsum = jnp.asarray(np.kron(np.eye(NH), np.ones((HE, HE))), BF16)
    dx2, du, dya, dyb, d_gates, dattn, dsum, dza, dob, acc2 = _bwd2_call(
        x2, tgt, vecs2, ya, yb, u, proj, attn, wa, wb, wo, hsum)
    d_wo, d_wo16 = _atb_call(y, du, "dw_out")
    d_wa, d_wa16 = _atb_call(oa, dya, "dw_branch_a")
    d_wb, d_wb16 = _atb_call(ob, dyb, "dw_branch_b")

    d_hgrn, acch = _hgrn_bwd_call(proj, oraw, dob, states, lb, gn)

    dq, dk, dv, dbs = _attn_bwd_call(proj, dattn, lse, dsum, biases)
    dqkvz = (dq, dk, dv, dza)

    d_win, d_win16 = _dwin_call(h_t, dqkvz, d_hgrn, d_gates)
    tok = hook((d_win, d_wa, d_wb, d_wo), (d_win16, d_wa16, d_wb16, d_wo16)) if hook is not None else 0.0
    one_scale = 1.0 + scale
    grad_x, acc1 = _dh_call(dqkvz, d_hgrn, d_gates, w_in, x, dx2,
                            jnp.concatenate([a + tok, norm_g, one_scale, jnp.zeros((5, D), F32)], 0))

    d_rel = jnp.zeros((NBUCKETS, NH), F32)
    for p, d in enumerate(PATTERNS):
        band, bucket = _band_bucket(d)
        onehot = (bucket[None] == np.arange(NBUCKETS)[:, None, None]) & band[None]
        d_rel = d_rel + jnp.einsum("hqk,bqk->bh", dbs[p], jnp.asarray(onehot, F32),
                                   precision=lax.Precision.HIGHEST)
    d_onorm = jnp.sum(acch[1].reshape(NH, HK), axis=0)

    zrow = jnp.zeros((D,), F32)
    pack = jnp.stack([acc1[0], acc1[1], acc2[0], acc1[2], acc2[1], acch[0],
                      zrow.at[:HK].set(d_onorm), zrow.at[0].set(loss),
                      zrow.at[:NBUCKETS * NH].set(d_rel.reshape(-1))] + [zrow] * 7, 0)
    return grad_x, d_win, d_wa, d_wb, d_wo, pack


def _me():
    return lax.axis_index("x"), lax.axis_index("y"), lax.axis_index("c")


def _peers(x, y):
    return [(1 - x, y), (x, 1 - y), (1 - x, 1 - y)]


def _allgather_small(blk, name):
    m_per, n = blk.shape

    def body(x_ref, out_ref, send_sems, recv_sems, local_sem):
        x, y, c = _me()
        me, sibling = (x, y, c), (x, y, 1 - c)
        chips = _peers(x, y)

        def rows(px, py, pc):
            return out_ref.at[pl.ds((4 * px + 2 * py + pc) * m_per, m_per), :]

        def copy(k, block, to, src=None):
            return pltpu.make_async_remote_copy(
                src_ref=rows(*block) if src is None else src, dst_ref=rows(*block),
                send_sem=send_sems.at[k], recv_sem=recv_sems.at[k], device_id=to, device_id_type=MESH)

        mine = pltpu.make_async_copy(x_ref, rows(*me), local_sem)
        mine.start()
        first = [copy(0, me, sibling, src=x_ref)]
        first += [copy(1 + j, me, (*chip, c), src=x_ref) for j, chip in enumerate(chips)]
        for cp in first:
            cp.start()
        passed = [copy(4 + j, (*chip, c), sibling) for j, chip in enumerate(chips)]
        for j, chip in enumerate(chips):
            copy(1 + j, (*chip, c), me).wait_recv()
            passed[j].start()
        copy(0, sibling, me).wait_recv()
        for j, chip in enumerate(chips):
            copy(4 + j, (*chip, 1 - c), me).wait_recv()
        for cp in first + passed:
            cp.wait_send()
        mine.wait()

    return pl.pallas_call(
        body, name=name, out_shape=jax.ShapeDtypeStruct((8 * m_per, n), blk.dtype),
        in_specs=[pl.BlockSpec(memory_space=pltpu.VMEM)], out_specs=pl.BlockSpec(memory_space=pltpu.VMEM),
        scratch_shapes=[pltpu.SemaphoreType.DMA((7,)), pltpu.SemaphoreType.DMA((7,)), pltpu.SemaphoreType.DMA],
    )(blk)


ANY = pl.BlockSpec(memory_space=pl.ANY)


HBM = pl.BlockSpec(memory_space=pltpu.HBM)
SEM = pl.BlockSpec(memory_space=pltpu.SEMAPHORE)
EFFECT = pltpu.SideEffectType.DATAFLOW_SIDE_EFFECTING


def _w_part(t, ref, j, half):
    if t == 0:
        return ref.at[pl.ds(half * (D // 2), D // 2), pl.ds(j * 2048, 2048)]
    if t == 1:
        return ref.at[pl.ds(half * (AW // 2), AW // 2), pl.ds(j * 256, 256)]
    return ref.at[pl.ds(j * 256 + half * 128, 128), :]


def _w_ici_copies(fulls, send_sems, recv_sems):
    x, y, c = _me()
    outs, ins = [], []
    for t in range(4):
        for k, chip in enumerate(_peers(x, y)):
            mine = _w_part(t, fulls[t], 2 * x + y, c)
            theirs = _w_part(t, fulls[t], 2 * chip[0] + chip[1], c)
            kw = dict(send_sem=send_sems.at[3 * t + k], recv_sem=recv_sems.at[3 * t + k],
                      device_id=(*chip, c), device_id_type=MESH)
            outs.append(pltpu.make_async_remote_copy(src_ref=mine, dst_ref=mine, **kw))
            ins.append(pltpu.make_async_remote_copy(src_ref=theirs, dst_ref=theirs, **kw))
    return outs, ins


def _gather_ici_start(fulls):
    def body(f0, f1, f2, f3, send_sems, recv_sems, t0, t1, t2, t3, token):
        for cp in _w_ici_copies([f0, f1, f2, f3], send_sems, recv_sems)[0]:
            cp.start()
        token[...] = jnp.zeros_like(token)

    res = pl.pallas_call(
        body, name="gather_ici_start",
        out_shape=(pltpu.SemaphoreType.DMA((12,)), pltpu.SemaphoreType.DMA((12,)),
                   *[pltpu.HBM(a.shape, a.dtype) for a in fulls], jax.ShapeDtypeStruct((8, 128), F32)),
        in_specs=[HBM] * 4, out_specs=(SEM, SEM, *[HBM] * 4, pl.BlockSpec(memory_space=pltpu.VMEM)),
        input_output_aliases={i: 2 + i for i in range(4)},
        compiler_params=pltpu.CompilerParams(has_side_effects=EFFECT),
    )(*[pltpu.with_memory_space_constraint(a, pltpu.HBM) for a in fulls])
    return res[0], res[1], list(res[2:6]), res[6]


def _gather_ici_wait(send_sems, recv_sems, thru, after):
    def body(f0, f1, f2, f3, send_sems, recv_sems, after_ref, g0, g1, g2, g3):
        outs, ins = _w_ici_copies([f0, f1, f2, f3], send_sems, recv_sems)
        for cp in outs:
            cp.wait_send()
        for cp in ins:
            cp.wait_recv()

    return pl.pallas_call(
        body, name="gather_ici_wait", out_shape=tuple(pltpu.HBM(a.shape, a.dtype) for a in thru),
        in_specs=[HBM] * 4 + [SEM, SEM, ANY], out_specs=[HBM] * 4,
        input_output_aliases={i: i for i in range(4)},
        compiler_params=pltpu.CompilerParams(has_side_effects=EFFECT),
    )(*thru, send_sems, recv_sems, after)


def _gather_sibling(fulls):
    def body(i0, i1, i2, i3, o0, o1, o2, o3, send_sems, recv_sems):
        x, y, c = _me()
        outs = [o0, o1, o2, o3]
        cps = []
        for t in range(4):
            for k, chip in enumerate(_peers(x, y)):
                blk = _w_part(t, outs[t], 2 * chip[0] + chip[1], c)
                cp = pltpu.make_async_remote_copy(
                    src_ref=blk, dst_ref=blk, send_sem=send_sems.at[t, k], recv_sem=recv_sems.at[t, k],
                    device_id=(x, y, 1 - c), device_id_type=MESH)
                cp.start()
                cps.append(cp)
        for t in range(4):
            for k, chip in enumerate(_peers(x, y)):
                blk = _w_part(t, outs[t], 2 * chip[0] + chip[1], 1 - c)
                pltpu.make_async_remote_copy(
                    src_ref=blk, dst_ref=blk, send_sem=send_sems.at[t, k], recv_sem=recv_sems.at[t, k],
                    device_id=(x, y, 1 - c), device_id_type=MESH).wait_recv()
        for cp in cps:
            cp.wait_send()

    return pl.pallas_call(
        body, name="gather_sibling",
        out_shape=[jax.ShapeDtypeStruct(s, BF16) for s in FULL_W_SHAPES],
        in_specs=[ANY] * 4, out_specs=[ANY] * 4, input_output_aliases={0: 0, 1: 1, 2: 2, 3: 3},
        scratch_shapes=[pltpu.SemaphoreType.DMA((4, 3)), pltpu.SemaphoreType.DMA((4, 3))],
    )(*fulls)


def _half_of(t, ref, half):
    if t == 0:
        return ref.at[:, pl.ds(half * 512, 512), :]
    if t == 1:
        return ref.at[pl.ds(half * 256, 256), :]
    return ref.at[:, pl.ds(half * 512, 512)]


HALF_SHAPES = [(4, 512, 2048), (256, D), (D, 512), (D, 512)]
PIECE_SHAPES = [(512, 2048), (256, 256), (256, 512), (256, 512)]
SHARD_SHAPES = [(D, 2048), (AW, 256), (256, D), (256, D)]


def _chip_piece(t, ref, j):
    if t == 0:
        return ref.at[j]
    if t == 1:
        return ref.at[:, pl.ds(j * 256, 256)]
    return ref.at[pl.ds(j * 256, 256), :]


def _reduce_sibling_send(gs):
    def body(g0, g1, g2, g3, r0, r1, r2, r3, send_sems, recv_sems):
        x, y, c = _me()
        ins, outs = [g0, g1, g2, g3], [r0, r1, r2, r3]
        cps = []
        for t in range(4):
            cp = pltpu.make_async_remote_copy(
                src_ref=_half_of(t, ins[t], 1 - c), dst_ref=outs[t],
                send_sem=send_sems.at[t], recv_sem=recv_sems.at[t], device_id=(x, y, 1 - c), device_id_type=MESH)
            cp.start()
            cps.append(cp)
        for cp in cps:
            cp.wait_recv()
        for cp in cps:
            cp.wait_send()

    return pl.pallas_call(
        body, name="reduce_sibling", out_shape=[jax.ShapeDtypeStruct(s, BF16) for s in HALF_SHAPES],
        in_specs=[ANY] * 4, out_specs=[ANY] * 4,
        scratch_shapes=[pltpu.SemaphoreType.DMA((4,)), pltpu.SemaphoreType.DMA((4,))],
    )(*gs)


def _chip_copies(hs, lands, send_sems, recv_sems):
    x, y, c = _me()
    cps = []
    for t in range(4):
        for k, chip in enumerate(_peers(x, y)):
            pj = 2 * chip[0] + chip[1]
            cps.append(pltpu.make_async_remote_copy(
                src_ref=_chip_piece(t, hs[t], pj), dst_ref=lands[t].at[k],
                send_sem=send_sems.at[3 * t + k], recv_sem=recv_sems.at[3 * t + k],
                device_id=(*chip, c), device_id_type=MESH))
    return cps


def _reduce_chips_start(hs):
    lands = [lax.empty((3,) + s, BF16) for s in PIECE_SHAPES]

    def body(h0, h1, h2, h3, l0, l1, l2, l3, send_sems, recv_sems, t0, t1, t2, t3, t4, t5, t6, t7, token):
        for cp in _chip_copies([h0, h1, h2, h3], [l0, l1, l2, l3], send_sems, recv_sems):
            cp.start()
        token[...] = jnp.zeros_like(token)

    bufs = list(hs) + lands
    res = pl.pallas_call(
        body, name="reduce_chips_start",
        out_shape=(pltpu.SemaphoreType.DMA((12,)), pltpu.SemaphoreType.DMA((12,)),
                   *[pltpu.HBM(a.shape, a.dtype) for a in bufs], jax.ShapeDtypeStruct((8, 128), F32)),
        in_specs=[HBM] * 8, out_specs=(SEM, SEM, *[HBM] * 8, pl.BlockSpec(memory_space=pltpu.VMEM)),
        input_output_aliases={i: 2 + i for i in range(8)},
        compiler_params=pltpu.CompilerParams(has_side_effects=EFFECT),
    )(*[pltpu.with_memory_space_constraint(a, pltpu.HBM) for a in bufs])
    return res[0], res[1], list(res[2:10]), res[10]


def _reduce_chips_wait(send_sems, recv_sems, thru, after):
    def body(h0, h1, h2, h3, l0, l1, l2, l3, send_sems, recv_sems, after_ref, d0, d1, d2, d3, g0, g1, g2, g3):
        cps = _chip_copies([h0, h1, h2, h3], [l0, l1, l2, l3], send_sems, recv_sems)
        for cp in cps:
            cp.wait_send()
        for cp in cps:
            cp.wait_recv()

    res = pl.pallas_call(
        body, name="reduce_chips_wait", out_shape=tuple(pltpu.HBM(a.shape, a.dtype) for a in thru),
        in_specs=[HBM] * 8 + [SEM, SEM, ANY], out_specs=[HBM] * 8,
        input_output_aliases={i: i for i in range(8)},
        compiler_params=pltpu.CompilerParams(has_side_effects=EFFECT),
    )(*thru, send_sems, recv_sems, after)
    return list(res[4:8])


def _share_sibling(shards):
    def body(i0, i1, i2, i3, o0, o1, o2, o3, send_sems, recv_sems):
        x, y, c = _me()
        outs = [o0, o1, o2, o3]

        def half(t, ref, hf):
            if t == 0:
                return ref.at[pl.ds(hf * 512, 512), :]
            if t == 1:
                return ref.at[pl.ds(hf * 256, 256), :]
            return ref.at[:, pl.ds(hf * 512, 512)]

        cps = []
        for t in range(4):
            mine = half(t, outs[t], c)
            cp = pltpu.make_async_remote_copy(
                src_ref=mine, dst_ref=mine, send_sem=send_sems.at[t], recv_sem=recv_sems.at[t],
                device_id=(x, y, 1 - c), device_id_type=MESH)
            cp.start()
            cps.append(cp)
        for t in range(4):
            theirs = half(t, outs[t], 1 - c)
            pltpu.make_async_remote_copy(
                src_ref=theirs, dst_ref=theirs, send_sem=send_sems.at[t],
                recv_sem=recv_sems.at[t], device_id=(x, y, 1 - c), device_id_type=MESH).wait_recv()
        for cp in cps:
            cp.wait_send()

    return pl.pallas_call(
        body, name="share_sibling", out_shape=[jax.ShapeDtypeStruct(s, F32) for s in SHARD_SHAPES],
        in_specs=[ANY] * 4, out_specs=[ANY] * 4, input_output_aliases={0: 0, 1: 1, 2: 2, 3: 3},
        scratch_shapes=[pltpu.SemaphoreType.DMA((4,)), pltpu.SemaphoreType.DMA((4,))],
    )(*shards)


def _half_blockspec(t, idx_pos):
    if t == 0:
        return pl.BlockSpec((1, 512, 2048), lambda i, s: (i, s[idx_pos], 0)), 4
    if t == 1:
        return pl.BlockSpec((256, D), lambda i, s: (s[idx_pos], 0)), 1
    return pl.BlockSpec((256, 512), lambda i, s: (i, s[idx_pos])), 4


def _half_out_blockspec(t):
    if t == 0:
        return pl.BlockSpec((1, 512, 2048), lambda i, s: (i, 0, 0))
    if t == 1:
        return pl.BlockSpec((256, D), lambda i, s: (0, 0))
    return pl.BlockSpec((256, 512), lambda i, s: (i, 0))


def _add_half_call(t, own, recv, sc, name):
    in_blk, steps = _half_blockspec(t, 0)
    out_blk = _half_out_blockspec(t)

    def body(s_ref, a_ref, b_ref, o_ref, ob_ref):
        v = a_ref[...] + b_ref[...].astype(F32)
        o_ref[...] = v
        ob_ref[...] = v.astype(BF16)

    return pl.pallas_call(
        body, name=name,
        grid_spec=pltpu.PrefetchScalarGridSpec(
            num_scalar_prefetch=1, grid=(steps,), in_specs=[in_blk, out_blk], out_specs=[out_blk, out_blk]),
        out_shape=[jax.ShapeDtypeStruct(HALF_SHAPES[t], F32), jax.ShapeDtypeStruct(HALF_SHAPES[t], BF16)],
        compiler_params=_cp(),
    )(sc, own, recv)


def _final_piece_call(t, chipsum, recv3, sc, name):
    ps = PIECE_SHAPES[t]
    if t == 0:
        own_blk = pl.BlockSpec((1,) + ps, lambda i, s: (s[1], 0, 0))
        o_blk = pl.BlockSpec(ps, lambda i, s: (s[0], 0))
    elif t == 1:
        own_blk = pl.BlockSpec(ps, lambda i, s: (0, s[1]))
        o_blk = pl.BlockSpec(ps, lambda i, s: (s[0], 0))
    else:
        own_blk = pl.BlockSpec(ps, lambda i, s: (s[1], 0))
        o_blk = pl.BlockSpec(ps, lambda i, s: (0, s[0]))
    r_blk = pl.BlockSpec((3,) + ps, lambda i, s: (0, 0, 0))

    def body(s_ref, a_ref, r_ref, o_ref):
        a = a_ref[0] if t == 0 else a_ref[...]
        o_ref[...] = ((a + r_ref[0].astype(F32)) + r_ref[1].astype(F32)) + r_ref[2].astype(F32)

    return pl.pallas_call(
        body, name=name,
        grid_spec=pltpu.PrefetchScalarGridSpec(
            num_scalar_prefetch=1, grid=(1,), in_specs=[own_blk, r_blk], out_specs=o_blk),
        out_shape=jax.ShapeDtypeStruct(SHARD_SHAPES[t], F32), compiler_params=_cp(),
    )(sc, chipsum, recv3)


FULL_W_SHAPES = [(D, NPROJ), (AW, D), (D, D), (D, D)]


def _cast_place_call(t, shard, sc, name):
    if t == 0:
        blk, steps = (512, 2048), 2
        in_blk = pl.BlockSpec(blk, lambda i, s: (i, 0))
        o_blk = pl.BlockSpec(blk, lambda i, s: (i, s[1]))
    elif t == 1:
        blk, steps = (AW, 256), 1
        in_blk = pl.BlockSpec(blk, lambda i, s: (0, 0))
        o_blk = pl.BlockSpec(blk, lambda i, s: (0, s[1]))
    else:
        blk, steps = (256, D), 1
        in_blk = pl.BlockSpec(blk, lambda i, s: (0, 0))
        o_blk = pl.BlockSpec(blk, lambda i, s: (s[1], 0))

    def body(s_ref, a_ref, o_ref, own_ref):
        v = a_ref[...].astype(BF16)
        o_ref[...] = v
        own_ref[...] = v

    return pl.pallas_call(
        body, name=name,
        grid_spec=pltpu.PrefetchScalarGridSpec(
            num_scalar_prefetch=1, grid=(steps,), in_specs=[in_blk], out_specs=[o_blk, in_blk]),
        out_shape=[jax.ShapeDtypeStruct(FULL_W_SHAPES[t], BF16), jax.ShapeDtypeStruct(shard.shape, BF16)],
        compiler_params=_cp(),
    )(sc, shard)


def _lower_bound_fn(hgrn_lb):
    return jnp.cumsum(jax.nn.softmax(hgrn_lb.astype(F32), axis=0), axis=0)[0]


def kernel(x, c, w_ada, b_ada, norm_g, w_in, hgrn_onorm_g, w_branch_a, w_branch_b, w_out, rel_bias, hgrn_lb, final_g, loss_target, m_w_ada, m_b_ada, m_norm_g, m_w_in, m_hgrn_onorm_g, m_w_branch_a, m_w_branch_b, m_w_out, m_rel_bias, m_hgrn_lb, m_final_g, v_w_ada, v_b_ada, v_norm_g, v_w_in, v_hgrn_onorm_g, v_w_branch_a, v_w_branch_b, v_w_out, v_rel_bias, v_hgrn_lb, v_final_g):
    ax, ay, ac = _me()
    chip = 2 * ax + ay
    dev = 4 * ax + 2 * ay + ac
    sc_idx = jnp.stack([ac, chip]).astype(jnp.int32)

    c_all = _allgather_small(jnp.pad(c, ((0, 7), (0, 0))), "gather_c").reshape(8, 8, D)[:, 0]
    b_s = lax.dynamic_slice(b_ada, (0, chip * 768), (1, 768))
    mod_part = _mod_call(c_all, w_ada[0], b_s)
    mod_all = _allgather_small(mod_part, "gather_mod").reshape(8, 8, 768)
    mod_mine = lax.dynamic_index_in_dim(mod_all, dev, axis=1, keepdims=False)
    mod = mod_mine[0::2].reshape(1, 3 * D)

    names = ["w_in", "w_a", "w_b", "w_o"]
    shards, mod = lax.optimization_barrier(([w_in[0], w_branch_a[0], w_branch_b[0], w_out[0]], mod))
    placed = [_cast_place_call(t, shards[t], sc_idx, "cast_" + names[t]) for t in range(4)]
    w_send_sems, w_recv_sems, w_thru, w_token = _gather_ici_start([p_[0] for p_ in placed])
    mod = mod + w_token[0, 0]
    shift, scale, gate = mod[:, :D], mod[:, D:2 * D], mod[:, 2 * D:]

    def weights_fn(xs, avec):
        proj, h, h_t = _proj_own_call(xs, avec, placed[0][1], sc_idx)
        arrived = _gather_ici_wait(w_send_sems, w_recv_sems, w_thru, proj)
        win_f, wa_f, wb_f, wo_f = _gather_sibling(arrived)
        return h, h_t, _proj_rest_call(h, win_f, proj, sc_idx), win_f, wa_f, wb_f, wo_f

    flight = {}

    def start_reduction(own, own16):
        sib = _reduce_sibling_send(own16)
        halves = [_add_half_call(t, own[t], sib[t], sc_idx, "chipsum_" + names[t]) for t in range(4)]
        send_sems, recv_sems, thru, token = _reduce_chips_start([hb for _, hb in halves])
        flight.update(sems=(send_sems, recv_sems), thru=thru, sums=[hf for hf, _ in halves])
        return token[0, 0]

    lb, lb_vjp = jax.vjp(_lower_bound_fn, hgrn_lb)
    grad_x, d_win, d_wa, d_wb, d_wo, pack = _local_step(
        x[0], loss_target[0], shift, scale, gate, norm_g, hgrn_onorm_g, rel_bias, lb[None, :],
        final_g[None, :], weights_fn, hook=start_reduction)

    rec = _reduce_chips_wait(*flight["sems"], flight["thru"], pack)
    pieces = [_final_piece_call(t, flight["sums"][t], rec[t], sc_idx, "piece_" + names[t]) for t in range(4)]
    g_win, g_wa, g_wb, g_wo = _share_sibling(pieces)

    packs = _allgather_small(pack, "gather_small").reshape(8, 16, D)
    tot = _sum8_call(packs)
    loss = tot[7, 0]
    g_b_ada = tot[0:3].reshape(1, 3 * D)
    g_norm_g = tot[3:4]
    g_final_g = tot[4]
    (g_hgrn_lb,) = lb_vjp(tot[5])
    g_onorm = tot[6:7, :HK]
    g_rel = tot[8, :NBUCKETS * NH].reshape(NBUCKETS, NH)

    def rows_of(a):
        flat = a.reshape(-1)
        n = -(-flat.shape[0] // D)
        return jnp.pad(flat, (0, n * D - flat.shape[0])).reshape(n, D)

    smalls = [(b_ada, g_b_ada, m_b_ada, v_b_ada), (norm_g, g_norm_g, m_norm_g, v_norm_g),
              (hgrn_onorm_g, g_onorm, m_hgrn_onorm_g, v_hgrn_onorm_g), (rel_bias, g_rel, m_rel_bias, v_rel_bias),
              (hgrn_lb, g_hgrn_lb, m_hgrn_lb, v_hgrn_lb), (final_g, g_final_g, m_final_g, v_final_g)]
    cat = [jnp.concatenate([rows_of(s[k]) for s in smalls], 0) for k in range(4)]
    cat = [jnp.pad(a, ((0, 16 - a.shape[0]), (0, 0))) for a in cat]
    _, sd, sm, sv = _adamw_call(*cat, "adamw_small")

    def unpack(packed):
        res, r = [], 0
        for s in smalls:
            n = -(-s[0].size // D)
            res.append(packed[r:r + n].reshape(-1)[:s[0].size].reshape(s[0].shape))
            r += n
        return res

    d_small, m_small, v_small = unpack(sd), unpack(sm), unpack(sv)

    sc_all = c_all * jax.nn.sigmoid(c_all)
    dmod_all = packs[:, 0:3].reshape(8, 3 * D)
    dm_s = lax.dynamic_slice(dmod_all, (0, chip * 768), (8, 768))
    g_w_ada, d_w_ada, nm_w_ada, nv_w_ada = _ada_update_call(sc_all.T, dm_s, w_ada[0], m_w_ada[0], v_w_ada[0])

    big = []
    for w, g, m, v, n in [(w_in, g_win, m_w_in, v_w_in, "w_in"), (w_branch_a, g_wa, m_w_branch_a, v_w_branch_a, "w_a"),
                          (w_branch_b, g_wb, m_w_branch_b, v_w_branch_b, "w_b"), (w_out, g_wo, m_w_out, v_w_out, "w_o")]:
        big.append(_adamw_call(w[0], g, m[0], v[0], "adamw_" + n))

    e = lambda a: a[None]
    grads = [e(g_w_ada), g_b_ada, g_norm_g, e(big[0][0]), g_onorm, e(big[1][0]), e(big[2][0]), e(big[3][0]),
             g_rel, g_hgrn_lb, g_final_g]
    deltas = [e(d_w_ada), d_small[0], d_small[1], e(big[0][1]), d_small[2], e(big[1][1]), e(big[2][1]), e(big[3][1]),
              d_small[3], d_small[4], d_small[5]]
    new_m = [e(nm_w_ada), m_small[0], m_small[1], e(big[0][2]), m_small[2], e(big[1][2]), e(big[2][2]), e(big[3][2]),
             m_small[3], m_small[4], m_small[5]]
    new_v = [e(nv_w_ada), v_small[0], v_small[1], e(big[0][3]), v_small[2], e(big[1][3]), e(big[2][3]), e(big[3][3]),
             v_small[3], v_small[4], v_small[5]]
    return (loss, grad_x[None], *grads, *deltas, *new_m, *new_v)
```

```python
import functools
import math

import numpy as np
import jax
import jax.numpy as jnp
from jax import lax
from jax.experimental import pallas as pl
from jax.experimental.pallas import tpu as pltpu

D = 1024
AW = 512
NH = 8
HE = 64
HK = 128
NPROJ = 8192
ABLK = 128
PATTERNS = (1, 4, 16)
NBUCKETS = 32
MAXDIST = 2048
NEG = -1e30
EPS = 1e-6
CH = 64
LR, B1, B2, AEPS, WD, STEP = 0.001, 0.9, 0.999, 1e-08, 0.01, 10

F32 = jnp.float32
BF16 = jnp.bfloat16
MESH = pl.DeviceIdType.MESH
VMEM_LIMIT = 56 * 1024 * 1024


def _cp(**kw):
    return pltpu.CompilerParams(vmem_limit_bytes=VMEM_LIMIT, **kw)


def _sig(x):
    return 0.5 * jnp.tanh(0.5 * x) + 0.5


def _nt(a, b):
    return lax.dot_general(a, b, (((1,), (1,)), ((), ())), preferred_element_type=F32)


def _tn(a, b):
    return lax.dot_general(a, b, (((0,), (0,)), ((), ())), preferred_element_type=F32)


def _nn(a, b):
    return jnp.dot(a, b, preferred_element_type=F32)


def _split2(x):
    h = x.astype(BF16)
    return h, (x - h.astype(F32)).astype(BF16)


def _exact_mm(tri_bf16, x):
    h, l = _split2(x)
    return _nn(tri_bf16, h) + _nn(tri_bf16, l)


def _exact_mm_r(x, ones_bf16):
    h, l = _split2(x)
    return _nn(h, ones_bf16) + _nn(l, ones_bf16)


def _h_call(x, avec):
    S = x.shape[0]
    tm = 512

    def body(x_ref, a_ref, h_ref, ht_ref):
        xv = x_ref[...]
        r = lax.rsqrt(jnp.mean(xv * xv, axis=-1, keepdims=True) + EPS)
        hv = xv * r * a_ref[0:1, :] + a_ref[1:2, :]
        h_ref[...] = hv.astype(BF16)
        ht_ref[...] = hv.T.astype(BF16)

    return pl.pallas_call(
        body, name="h_norm", grid=(S // tm,),
        in_specs=[pl.BlockSpec((tm, D), lambda i: (i, 0)), pl.BlockSpec((8, D), lambda i: (0, 0))],
        out_specs=[pl.BlockSpec((tm, D), lambda i: (i, 0)), pl.BlockSpec((D, tm), lambda i: (0, i))],
        out_shape=[jax.ShapeDtypeStruct((S, D), BF16), jax.ShapeDtypeStruct((D, S), BF16)], compiler_params=_cp(),
    )(x, avec)


def _proj_call(h, w_in):
    S = h.shape[0]
    tm, tn = 512, 2048

    def body(h_ref, w_ref, o_ref):
        o_ref[...] = _nn(h_ref[...], w_ref[...])

    return pl.pallas_call(
        body, name="in_proj", grid=(NPROJ // tn, S // tm),
        in_specs=[pl.BlockSpec((tm, D), lambda j, i: (i, 0)), pl.BlockSpec((D, tn), lambda j, i: (0, j))],
        out_specs=pl.BlockSpec((tm, tn), lambda j, i: (i, j)),
        out_shape=jax.ShapeDtypeStruct((S, NPROJ), F32), compiler_params=_cp(),
    )(h, w_in)


def _proj_own_call(x, avec, w_own, sc):
    S = x.shape[0]
    tm, tn = 512, 2048

    def body(s_ref, x_ref, a_ref, w_ref, o_ref, h_ref, ht_ref):
        xv = x_ref[...]
        r = lax.rsqrt(jnp.mean(xv * xv, axis=-1, keepdims=True) + EPS)
        hv = xv * r * a_ref[0:1, :] + a_ref[1:2, :]
        hb = hv.astype(BF16)
        h_ref[...] = hb
        ht_ref[...] = hv.T.astype(BF16)
        o_ref[...] = _nn(hb, w_ref[...])

    return pl.pallas_call(
        body, name="in_proj_own",
        grid_spec=pltpu.PrefetchScalarGridSpec(
            num_scalar_prefetch=1, grid=(S // tm,),
            in_specs=[pl.BlockSpec((tm, D), lambda i, s: (i, 0)), pl.BlockSpec((8, D), lambda i, s: (0, 0)),
                      pl.BlockSpec((D, tn), lambda i, s: (0, 0))],
            out_specs=[pl.BlockSpec((tm, tn), lambda i, s: (i, s[1])), pl.BlockSpec((tm, D), lambda i, s: (i, 0)),
                       pl.BlockSpec((D, tm), lambda i, s: (0, i))]),
        out_shape=[jax.ShapeDtypeStruct((S, NPROJ), F32), jax.ShapeDtypeStruct((S, D), BF16),
                   jax.ShapeDtypeStruct((D, S), BF16)],
        compiler_params=_cp(),
    )(sc, x, avec, w_own)


def _proj_rest_call(h, w_in, proj, sc):
    S = h.shape[0]
    tm, tn = 512, 2048

    def body(s_ref, h_ref, w_ref, p_ref, o_ref):
        o_ref[...] = _nn(h_ref[...], w_ref[...])

    col = lambda j, s: (s[1] + 1 + j) % 4
    return pl.pallas_call(
        body, name="in_proj_rest",
        grid_spec=pltpu.PrefetchScalarGridSpec(
            num_scalar_prefetch=1, grid=(3, S // tm),
            in_specs=[pl.BlockSpec((tm, D), lambda j, i, s: (i, 0)),
                      pl.BlockSpec((D, tn), lambda j, i, s: (0, col(j, s))),
                      pl.BlockSpec(memory_space=pl.ANY)],
            out_specs=pl.BlockSpec((tm, tn), lambda j, i, s: (i, col(j, s)))),
        out_shape=jax.ShapeDtypeStruct((S, NPROJ), F32), input_output_aliases={3: 0}, compiler_params=_cp(),
    )(sc, h, w_in, proj)


def _t5_bucket_np(dist):
    max_exact = NBUCKETS // 2
    n = dist.astype(np.float32)
    large = max_exact + (np.log(np.maximum(n, np.float32(1.0)) / np.float32(max_exact))
                         / np.float32(math.log(MAXDIST / max_exact))
                         * np.float32(NBUCKETS - max_exact)).astype(np.int32)
    large = np.minimum(large, NBUCKETS - 1)
    return np.where(dist < max_exact, dist, large)


def _band_bucket(d):
    qi = np.arange(ABLK)[:, None]
    kj = np.arange(2 * ABLK)[None, :]
    delta = qi + ABLK - kj
    band = (delta >= 0) & (delta <= ABLK)
    bucket = _t5_bucket_np(np.clip(delta, 0, None) * d)
    return band, bucket


def _bias_tiles(rel_bias):
    tiles = []
    for d in PATTERNS:
        band, bucket = _band_bucket(d)
        onehot = (jnp.asarray(bucket, jnp.int32)[None] == jnp.arange(NBUCKETS, dtype=jnp.int32)[:, None, None])
        bias = jnp.einsum("bqk,bh->hqk", onehot.astype(F32), rel_bias, precision=lax.Precision.HIGHEST)
        tiles.append(jnp.where(jnp.asarray(band)[None], bias, NEG))
    return jnp.stack(tiles, 0)


ATT = 2048
HP = 2 * HE
AGRP = 4
AGRP_B = 2


def _attn_blocks():
    out = []
    for p, d in enumerate(PATTERNS):
        for r in range(d):
            for n in range(ATT // (d * ABLK)):
                out.append((p, d, r, n))
    return out


def _attn_fwd_call(proj, biases):
    S = proj.shape[0]
    nt = S // ATT

    def body(q_ref, k_ref, v_ref, z_ref, b_ref, a_ref, l_ref, oa_ref, kc, vc, op, lp):
        i = pl.program_id(1)

        @pl.when(i == 0)
        def _():
            kc[0:ATT] = jnp.zeros((ATT, HP), F32)
            vc[0:ATT] = jnp.zeros((ATT, HP), F32)

        @pl.when(i > 0)
        def _():
            kc[0:ATT] = kc[ATT:2 * ATT]
            vc[0:ATT] = vc[ATT:2 * ATT]

        kc[ATT:2 * ATT] = k_ref[...]
        vc[ATT:2 * ATT] = v_ref[...]
        col = lax.broadcasted_iota(jnp.int32, (ABLK, 2 * ABLK), 1)
        dead = jnp.logical_and(i == 0, col < ABLK)
        blocks = _attn_blocks()
        hs = (slice(0, HE), slice(HE, 2 * HE))
        for g0 in range(0, len(blocks), AGRP):
            grp = blocks[g0:g0 + AGRP]
            qrows = [pl.ds(n * ABLK * d + r, ABLK, stride=d) for p, d, r, n in grp]
            krows = [pl.ds(ATT + (n - 1) * ABLK * d + r, 2 * ABLK, stride=d) for p, d, r, n in grp]
            qs = [(q_ref[qr, :] * (HE ** -0.5)).astype(BF16) for qr in qrows]
            ks = [kc[kr, :].astype(BF16) for kr in krows]
            vs = [vc[kr, :].astype(BF16) for kr in krows]
            ss = [[_nt(qs[b][:, sl], ks[b][:, sl]) + b_ref[grp[b][0], e] for e, sl in enumerate(hs)]
                  for b in range(len(grp))]
            ss = [[jnp.where(dead, NEG, s) if grp[b][3] == 0 else s for s in ss[b]] for b in range(len(grp))]
            mxs = [[jnp.max(s, axis=-1, keepdims=True) for s in sb] for sb in ss]
            pes = [[jnp.exp(s - mx) for s, mx in zip(sb, mb)] for sb, mb in zip(ss, mxs)]
            dens = [[jnp.sum(pe, axis=-1, keepdims=True) for pe in pb] for pb in pes]
            pvs = [[_nn(pe.astype(BF16), vs[b][:, sl]) for pe, sl in zip(pes[b], hs)] for b in range(len(grp))]
            for b in range(len(grp)):
                p, d, r, n = grp[b]
                prow = pl.ds(p * ATT + n * ABLK * d + r, ABLK, stride=d)
                lp[prow, :] = jnp.concatenate(
                    [jnp.broadcast_to(mx + jnp.log(dn), (ABLK, HE)) for mx, dn in zip(mxs[b], dens[b])], axis=1)
                op[prow, :] = jnp.concatenate([pv / dn for pv, dn in zip(pvs[b], dens[b])], axis=1)
        rt = 256
        for t in range(ATT // rt):
            rows = slice(t * rt, (t + 1) * rt)
            pr = [slice(p * ATT + t * rt, p * ATT + (t + 1) * rt) for p in range(3)]
            la, lb_, lc = lp[pr[0], :], lp[pr[1], :], lp[pr[2], :]
            m = jnp.maximum(jnp.maximum(la, lb_), lc)
            ea, eb, ec = jnp.exp(la - m), jnp.exp(lb_ - m), jnp.exp(lc - m)
            den = ea + eb + ec
            att = (ea * op[pr[0], :] + eb * op[pr[1], :] + ec * op[pr[2], :]) / den
            a_ref[rows, :] = att
            l_ref[rows, :] = m + jnp.log(den)
            z = z_ref[rows, :]
            oa_ref[rows, :] = (att * (z * _sig(z))).astype(BF16)

    def pcol(c):
        return pl.BlockSpec((ATT, HP), lambda h, i: (i, c * 4 + h))

    out = pl.BlockSpec((ATT, HP), lambda h, i: (i, h))
    return pl.pallas_call(
        body, name="attn_fwd", grid=(4, nt),
        in_specs=[pcol(0), pcol(1), pcol(2), pcol(3),
                  pl.BlockSpec((3, 2, ABLK, 2 * ABLK), lambda h, i: (0, h, 0, 0))],
        out_specs=[out, out, out],
        out_shape=[jax.ShapeDtypeStruct((S, AW), F32), jax.ShapeDtypeStruct((S, AW), F32),
                   jax.ShapeDtypeStruct((S, AW), BF16)],
        scratch_shapes=[pltpu.VMEM((2 * ATT, HP), F32), pltpu.VMEM((2 * ATT, HP), F32),
                        pltpu.VMEM((3 * ATT, HP), F32), pltpu.VMEM((3 * ATT, HP), F32)],
        compiler_params=_cp(),
    )(proj, proj, proj, proj, biases)


def _attn_bwd_call(proj, dattn, lse, dsum, biases):
    S = proj.shape[0]
    nt = S // ATT

    def body(q_ref, k_ref, v_ref, do_ref, l_ref, ds_ref, b_ref, dq_ref, dk_ref, dv_ref, db_ref,
             kc, vc, dkc, dvc, dqa):
        i = pl.program_id(1)

        @pl.when(i == 0)
        def _():
            kc[ATT:2 * ATT] = jnp.zeros((ATT, HP), F32)
            vc[ATT:2 * ATT] = jnp.zeros((ATT, HP), F32)
            dkc[ATT:2 * ATT] = jnp.zeros((ATT, HP), F32)
            dvc[ATT:2 * ATT] = jnp.zeros((ATT, HP), F32)
            db_ref[...] = jnp.zeros_like(db_ref)

        @pl.when(i < nt)
        def _():
            kc[0:ATT] = kc[ATT:2 * ATT]
            vc[0:ATT] = vc[ATT:2 * ATT]
            dkc[0:ATT] = dkc[ATT:2 * ATT]
            dvc[0:ATT] = dvc[ATT:2 * ATT]
            kc[ATT:2 * ATT] = k_ref[...]
            vc[ATT:2 * ATT] = v_ref[...]
            dkc[ATT:2 * ATT] = jnp.zeros((ATT, HP), F32)
            dvc[ATT:2 * ATT] = jnp.zeros((ATT, HP), F32)
            col = lax.broadcasted_iota(jnp.int32, (ABLK, 2 * ABLK), 1)
            dead = jnp.logical_and(i == 0, col < ABLK)
            blocks = _attn_blocks()
            hs = (slice(0, HE), slice(HE, 2 * HE))
            for g0 in range(0, len(blocks), AGRP_B):
                grp = blocks[g0:g0 + AGRP_B]
                nb_ = range(len(grp))
                qrows = [pl.ds(n * ABLK * d + r, ABLK, stride=d) for p, d, r, n in grp]
                krows = [pl.ds(ATT + (n - 1) * ABLK * d + r, 2 * ABLK, stride=d) for p, d, r, n in grp]
                qs = [(q_ref[qr, :] * (HE ** -0.5)).astype(BF16) for qr in qrows]
                ks = [kc[kr, :].astype(BF16) for kr in krows]
                vs = [vc[kr, :].astype(BF16) for kr in krows]
                dos = [do_ref[qr, :].astype(BF16) for qr in qrows]
                lvs = [l_ref[qr, :] for qr in qrows]
                dsvs = [ds_ref[qr, :] for qr in qrows]
                ss = [[_nt(qs[b][:, sl], ks[b][:, sl]) + b_ref[grp[b][0], e] for e, sl in enumerate(hs)] for b in nb_]
                ss = [[jnp.where(dead, NEG, s) if grp[b][3] == 0 else s for s in ss[b]] for b in nb_]
                dps = [[_nt(dos[b][:, sl], vs[b][:, sl]) for sl in hs] for b in nb_]
                pes = [[jnp.exp(ss[b][e] - lvs[b][:, e * HE:e * HE + 1]) for e in range(2)] for b in nb_]
                dscs = [[pes[b][e] * (dps[b][e] - dsvs[b][:, e * HE:e * HE + 1]) for e in range(2)] for b in nb_]
                for b in nb_:
                    for e in range(2):
                        db_ref[grp[b][0], e] += dscs[b][e]
                dsbs = [[t.astype(BF16) for t in tb] for tb in dscs]
                dqs = [[_nn(dsbs[b][e], ks[b][:, sl]) * (HE ** -0.5) for e, sl in enumerate(hs)] for b in nb_]
                dks = [[_tn(dsbs[b][e], qs[b][:, sl]) for e, sl in enumerate(hs)] for b in nb_]
                dvs = [[_tn(pes[b][e].astype(BF16), dos[b][:, sl]) for e, sl in enumerate(hs)] for b in nb_]
                for b in nb_:
                    dq = jnp.concatenate(dqs[b], axis=1)
                    if grp[b][0] == 0:
                        dqa[qrows[b], :] = dq
                    else:
                        dqa[qrows[b], :] += dq
                    dkc[krows[b], :] += jnp.concatenate(dks[b], axis=1)
                    dvc[krows[b], :] += jnp.concatenate(dvs[b], axis=1)
            dq_ref[...] = dqa[...].astype(BF16)
            dk_ref[...] = dkc[0:ATT].astype(BF16)
            dv_ref[...] = dvc[0:ATT].astype(BF16)

        @pl.when(i == nt)
        def _():
            dk_ref[...] = dkc[ATT:2 * ATT].astype(BF16)
            dv_ref[...] = dvc[ATT:2 * ATT].astype(BF16)

    def pcol(c):
        return pl.BlockSpec((ATT, HP), lambda h, i: (jnp.minimum(i, nt - 1), c * 4 + h))

    qrow = pl.BlockSpec((ATT, HP), lambda h, i: (jnp.minimum(i, nt - 1), h))
    krow = pl.BlockSpec((ATT, HP), lambda h, i: (jnp.maximum(i - 1, 0), h))
    bspec = pl.BlockSpec((3, 2, ABLK, 2 * ABLK), lambda h, i: (0, h, 0, 0))
    return pl.pallas_call(
        body, name="attn_bwd", grid=(4, nt + 1),
        in_specs=[pcol(0), pcol(1), pcol(2), qrow, qrow, qrow, bspec],
        out_specs=[qrow, krow, krow, bspec],
        out_shape=[jax.ShapeDtypeStruct((S, AW), BF16)] * 3
                  + [jax.ShapeDtypeStruct((3, NH, ABLK, 2 * ABLK), F32)],
        scratch_shapes=[pltpu.VMEM((2 * ATT, HP), F32)] * 4 + [pltpu.VMEM((ATT, HP), F32)],
        compiler_params=_cp(),
    )(proj, proj, proj, dattn, lse, dsum, biases)


HRB = 256


def _hgrn_gates(q_ref, f_ref, rows, lbv, tri):
    qraw = q_ref[rows, :]
    sq = _sig(qraw)
    q = qraw * sq
    sf = _sig(f_ref[rows, :])
    f = lbv + (1.0 - lbv) * sf
    k = 1.0 - f
    b = _exact_mm(tri, jnp.log(f))
    bl = b[CH - 1:CH, :]
    bm = b[CH // 2 - 1:CH // 2, :]
    e1 = jnp.exp(b - bm)
    e2 = jnp.exp(bm - b)
    ebm = jnp.exp(bm)
    eblm = jnp.exp(bl - bm)
    qs = q * e1
    ks = k * e2
    qe = qs * ebm
    kd = ks * eblm
    return dict(qraw=qraw, sq=sq, sf=sf, f=f, bl=bl, e1=e1, e2=e2, ebm=ebm, eblm=eblm, qe=qe, qs=qs, ks=ks, kd=kd)


def _unused_tri_masks():
    row = lax.broadcasted_iota(jnp.int32, (CH, CH), 0)
    col = lax.broadcasted_iota(jnp.int32, (CH, CH), 1)
    return row >= col


def _unused_hgrn_fwd_call(proj, lb, gn):
    S = proj.shape[0]
    nc = S // CH
    cps = HRB // CH

    def body(q_ref, f_ref, i_ref, z_ref, lb_ref, gn_ref, or_ref, ob_ref, st_ref, st):
        @pl.when(pl.program_id(0) == 0)
        def _():
            st[...] = jnp.zeros_like(st)

        low = _tri_masks()
        tri = low.astype(BF16)
        lbv = lb_ref[...]
        for ci in range(cps):
            rows = slice(ci * CH, (ci + 1) * CH)
            g = _hgrn_gates(q_ref, f_ref, rows, lbv, tri)
            v = i_ref[rows, :]
            ebl = jnp.exp(g["bl"])
            st_ref[ci] = st[...]
            hs = [slice(hh * HK, (hh + 1) * HK) for hh in range(NH)]
            vb = v.astype(BF16)
            qsb, ksb, qeb, kdb = (g[n_].astype(BF16) for n_ in ("qs", "ks", "qe", "kd"))
            s0s = [st[sl, :] for sl in hs]
            as_ = [_nt(qsb[:, sl], ksb[:, sl]) for sl in hs]
            ois = [_nt(qeb[:, sl], s0.astype(BF16)) for sl, s0 in zip(hs, s0s)]
            sts = [_tn(vb[:, sl], kdb[:, sl]) for sl in hs]
            abs_ = [jnp.where(low, a, 0.0).astype(BF16) for a in as_]
            os_ = [oi + _nn(a, vb[:, sl]) for oi, a, sl in zip(ois, abs_, hs)]
            for sl, s0, sn in zip(hs, s0s, sts):
                st[sl, :] = s0 * ebl[:, sl] + sn
            o = jnp.concatenate(os_, axis=1)
            or_ref[rows, :] = o
            rs = [lax.rsqrt(jnp.mean(oh * oh, axis=-1, keepdims=True) + EPS) for oh in os_]
            on = jnp.concatenate([oh * r for oh, r in zip(os_, rs)], axis=1)
            z = z_ref[rows, :]
            ob_ref[rows, :] = (on * gn_ref[...] * (z * _sig(z))).astype(BF16)

    def pcol(c):
        return pl.BlockSpec((HRB, D), lambda i: (i, c))

    vec = pl.BlockSpec((1, D), lambda i: (0, 0))
    row = pl.BlockSpec((HRB, D), lambda i: (i, 0))
    return pl.pallas_call(
        body, name="hgrn_fwd", grid=(S // HRB,),
        in_specs=[pcol(2), pcol(3), pcol(4), pcol(5), vec, vec],
        out_specs=[row, row, pl.BlockSpec((cps, NH * HK, HK), lambda i: (i, 0, 0))],
        out_shape=[jax.ShapeDtypeStruct((S, D), F32), jax.ShapeDtypeStruct((S, D), BF16),
                   jax.ShapeDtypeStruct((nc, NH * HK, HK), F32)],
        scratch_shapes=[pltpu.VMEM((NH * HK, HK), F32)],
        compiler_params=_cp(),
    )(proj, proj, proj, proj, lb, gn)


def _unused_hgrn_bwd_call(proj, oraw, dob, states, lb, gn):
    S = proj.shape[0]
    nblk = S // HRB
    cps = HRB // CH

    def body(q_ref, f_ref, i_ref, z_ref, or_ref, dob_ref, st_ref, lb_ref, gn_ref, dh_ref, acc_ref, dst):
        @pl.when(pl.program_id(0) == 0)
        def _():
            dst[...] = jnp.zeros_like(dst)
            acc_ref[...] = jnp.zeros_like(acc_ref)

        low = _tri_masks()
        tri = low.astype(BF16)
        triu = jnp.logical_not(_tri_masks()) | (lax.broadcasted_iota(jnp.int32, (CH, CH), 0)
                                               == lax.broadcasted_iota(jnp.int32, (CH, CH), 1))
        triu = triu.astype(BF16)
        lbv = lb_ref[...]
        for ci in reversed(range(cps)):
            rows = slice(ci * CH, (ci + 1) * CH)
            g = _hgrn_gates(q_ref, f_ref, rows, lbv, tri)
            v = i_ref[rows, :]
            ebl = jnp.exp(g["bl"])
            hs = [slice(hh * HK, (hh + 1) * HK) for hh in range(NH)]
            cat = lambda lst: jnp.concatenate(lst, axis=1)
            o = or_ref[rows, :]
            z = z_ref[rows, :]
            sz = _sig(z)
            gnv = gn_ref[...]
            dobv = dob_ref[rows, :]
            r = cat([jnp.broadcast_to(lax.rsqrt(jnp.mean(o[:, sl] * o[:, sl], axis=-1, keepdims=True) + EPS),
                                      (CH, HK)) for sl in hs])
            don = dobv * (z * sz)
            dz = dobv * (o * r * gnv) * (sz * (1.0 + z * (1.0 - sz)))
            dgn = jnp.sum(don * o * r, axis=0, keepdims=True)
            gh = don * gnv
            gho = gh * o
            mg = cat([jnp.broadcast_to(jnp.mean(gho[:, sl], axis=-1, keepdims=True), (CH, HK)) for sl in hs])
            dob16 = (r * gh - o * (r * r * r * mg)).astype(BF16)

            vb = v.astype(BF16)
            qsb, ksb, qeb, kdb = (g[n_].astype(BF16) for n_ in ("qs", "ks", "qe", "kd"))
            st0s = [st_ref[ci, sl, :] for sl in hs]
            dst1s = [dst[sl, :] for sl in hs]
            dst1bs = [t.astype(BF16) for t in dst1s]
            as_ = [_nt(qsb[:, sl], ksb[:, sl]) for sl in hs]
            das_ = [_nt(dob16[:, sl], vb[:, sl]) for sl in hs]
            dqes = [_nn(dob16[:, sl], s0.astype(BF16)) for sl, s0 in zip(hs, st0s)]
            dkds = [_nn(vb[:, sl], d1) for sl, d1 in zip(hs, dst1bs)]
            dvis = [_nt(kdb[:, sl], d1) for sl, d1 in zip(hs, dst1bs)]
            dsts = [_tn(dob16[:, sl], qeb[:, sl]) for sl in hs]
            abs_ = [jnp.where(low, a, 0.0).astype(BF16) for a in as_]
            dabs_ = [jnp.where(low, a, 0.0).astype(BF16) for a in das_]
            dqss = [_nn(da, ksb[:, sl]) for da, sl in zip(dabs_, hs)]
            dkss = [_tn(da, qsb[:, sl]) for da, sl in zip(dabs_, hs)]
            dvs_ = [_tn(a, dob16[:, sl]) + dvi for a, sl, dvi in zip(abs_, hs, dvis)]
            exs_ = [jnp.sum(d1 * s0, axis=0, keepdims=True) for d1, s0 in zip(dst1s, st0s)]
            for sl, d1, dn in zip(hs, dst1s, dsts):
                dst[sl, :] = dn + d1 * ebl[:, sl]
            dqe, dqs, dks, dkd, dv = cat(dqes), cat(dqss), cat(dkss), cat(dkds), cat(dvs_)
            dq = (dqe * g["ebm"] + dqs) * g["e1"]
            dk = (dks + dkd * g["eblm"]) * g["e2"]
            dkdkd = dkd * g["kd"]
            db = dqe * g["qe"] + dqs * qsb.astype(F32) - dks * ksb.astype(F32) - dkdkd
            ex = jnp.sum(dkdkd, axis=0, keepdims=True) + cat(exs_) * ebl
            dg = _exact_mm(triu, db) + ex
            df = dg / g["f"] - dk
            sf = g["sf"]
            dfr = df * (1.0 - lbv) * sf * (1.0 - sf)
            sq = g["sq"]
            dqr = dq * (sq * (1.0 + g["qraw"] * (1.0 - sq)))
            acc_ref[0:1, :] += jnp.sum(df * (1.0 - sf), axis=0, keepdims=True)
            acc_ref[1:2, :] += dgn
            dh_ref[rows, 0:D] = dqr.astype(BF16)
            dh_ref[rows, D:2 * D] = dfr.astype(BF16)
            dh_ref[rows, 2 * D:3 * D] = dv.astype(BF16)
            dh_ref[rows, 3 * D:4 * D] = dz.astype(BF16)

    def pcol(c):
        return pl.BlockSpec((HRB, D), lambda i: (nblk - 1 - i, c))

    vec = pl.BlockSpec((1, D), lambda i: (0, 0))
    row = pl.BlockSpec((HRB, D), lambda i: (nblk - 1 - i, 0))
    return pl.pallas_call(
        body, name="hgrn_bwd", grid=(nblk,),
        in_specs=[pcol(2), pcol(3), pcol(4), pcol(5), row, row,
                  pl.BlockSpec((cps, NH * HK, HK), lambda i: (nblk - 1 - i, 0, 0)), vec, vec],
        out_specs=[pl.BlockSpec((HRB, 4 * D), lambda i: (nblk - 1 - i, 0)),
                   pl.BlockSpec((8, D), lambda i: (0, 0))],
        out_shape=[jax.ShapeDtypeStruct((S, 4 * D), BF16), jax.ShapeDtypeStruct((8, D), F32)],
        scratch_shapes=[pltpu.VMEM((NH * HK, HK), F32)],
        compiler_params=_cp(),
    )(proj, proj, proj, proj, oraw, dob, states, lb, gn)


def _tri_masks():
    row = lax.broadcasted_iota(jnp.int32, (CH, CH), 0)
    col = lax.broadcasted_iota(jnp.int32, (CH, CH), 1)
    return row >= col


def _heads():
    return [slice(hh * HK, (hh + 1) * HK) for hh in range(NH)]


def _hgrn_gate_heads(q_ref, f_ref, rows, lbv):
    out = []
    for sl in _heads():
        qraw = q_ref[rows, sl]
        sq = _sig(qraw)
        sf = _sig(f_ref[rows, sl])
        f = lbv[:, sl] + (1.0 - lbv[:, sl]) * sf
        out.append(dict(qraw=qraw, sq=sq, q=qraw * sq, sf=sf, f=f, k=1.0 - f, lg=jnp.log(f)))
    return out


def _hgrn_decay_heads(gh, b):
    bl = b[CH - 1:CH, :]
    bm = b[CH // 2 - 1:CH // 2, :]
    ebm = jnp.exp(bm)
    eblm = jnp.exp(bl - bm)
    ebl = jnp.exp(bl)
    out = []
    for g, sl in zip(gh, _heads()):
        d = b[:, sl] - bm[:, sl]
        e1 = jnp.exp(d)
        e2 = jnp.exp(-d)
        qs = g["q"] * e1
        ks = g["k"] * e2
        qe = qs * ebm[:, sl]
        kd = ks * eblm[:, sl]
        out.append(dict(e1=e1, e2=e2, qe=qe, kd=kd, ebm=ebm[:, sl], eblm=eblm[:, sl], ebl=ebl[:, sl],
                        qsb=qs.astype(BF16), ksb=ks.astype(BF16), qeb=qe.astype(BF16), kdb=kd.astype(BF16)))
    return out


def _hgrn_fwd_call(proj, lb, gn):
    S = proj.shape[0]
    nc = S // CH
    cps = HRB // CH

    def body(q_ref, f_ref, i_ref, z_ref, lb_ref, gn_ref, or_ref, ob_ref, st_ref, st):
        @pl.when(pl.program_id(0) == 0)
        def _():
            st[...] = jnp.zeros_like(st)

        low = _tri_masks()
        tri = low.astype(BF16)
        lbv = lb_ref[...]
        hs = _heads()
        for ci in range(cps):
            rows = slice(ci * CH, (ci + 1) * CH)
            gh = _hgrn_gate_heads(q_ref, f_ref, rows, lbv)
            b = _exact_mm(tri, jnp.concatenate([g["lg"] for g in gh], axis=1))
            dh = _hgrn_decay_heads(gh, b)
            vbs = [i_ref[rows, sl].astype(BF16) for sl in hs]
            st_ref[ci] = st[...]
            s0s = [st[sl, :] for sl in hs]
            as_ = [_nt(d["qsb"], d["ksb"]) for d in dh]
            ois = [_nt(d["qeb"], s0.astype(BF16)) for d, s0 in zip(dh, s0s)]
            sts = [_tn(vb, d["kdb"]) for vb, d in zip(vbs, dh)]
            abs_ = [jnp.where(low, a, 0.0).astype(BF16) for a in as_]
            os_ = [oi + _nn(a, vb) for oi, a, vb in zip(ois, abs_, vbs)]
            for sl, s0, sn, d, o in zip(hs, s0s, sts, dh, os_):
                st[sl, :] = s0 * d["ebl"] + sn
                or_ref[rows, sl] = o
                r = lax.rsqrt(jnp.mean(o * o, axis=-1, keepdims=True) + EPS)
                z = z_ref[rows, sl]
                ob_ref[rows, sl] = (o * r * gn_ref[:, sl] * (z * _sig(z))).astype(BF16)

    def pcol(c):
        return pl.BlockSpec((HRB, D), lambda i: (i, c))

    vec = pl.BlockSpec((1, D), lambda i: (0, 0))
    row = pl.BlockSpec((HRB, D), lambda i: (i, 0))
    return pl.pallas_call(
        body, name="hgrn_fwd", grid=(S // HRB,),
        in_specs=[pcol(2), pcol(3), pcol(4), pcol(5), vec, vec],
        out_specs=[row, row, pl.BlockSpec((cps, NH * HK, HK), lambda i: (i, 0, 0))],
        out_shape=[jax.ShapeDtypeStruct((S, D), F32), jax.ShapeDtypeStruct((S, D), BF16),
                   jax.ShapeDtypeStruct((nc, NH * HK, HK), F32)],
        scratch_shapes=[pltpu.VMEM((NH * HK, HK), F32)],
        compiler_params=_cp(),
    )(proj, proj, proj, proj, lb, gn)


def _hgrn_bwd_call(proj, oraw, dob, states, lb, gn):
    S = proj.shape[0]
    nblk = S // HRB
    cps = HRB // CH

    def body(q_ref, f_ref, i_ref, z_ref, or_ref, dob_ref, st_ref, lb_ref, gn_ref, dh_ref, acc_ref, dst):
        @pl.when(pl.program_id(0) == 0)
        def _():
            dst[...] = jnp.zeros_like(dst)
            acc_ref[...] = jnp.zeros_like(acc_ref)

        low = _tri_masks()
        tri = low.astype(BF16)
        triu = jnp.logical_not(_tri_masks()) | (lax.broadcasted_iota(jnp.int32, (CH, CH), 0)
                                               == lax.broadcasted_iota(jnp.int32, (CH, CH), 1))
        triu = triu.astype(BF16)
        lbv = lb_ref[...]
        hs = _heads()
        for ci in reversed(range(cps)):
            rows = slice(ci * CH, (ci + 1) * CH)
            dobs, dgns = [], []
            for sl in hs:
                o = or_ref[rows, sl]
                z = z_ref[rows, sl]
                sz = _sig(z)
                gnv = gn_ref[:, sl]
                dobv = dob_ref[rows, sl]
                r = lax.rsqrt(jnp.mean(o * o, axis=-1, keepdims=True) + EPS)
                onr = o * r
                don = dobv * (z * sz)
                dh_ref[rows, 3 * D + sl.start:3 * D + sl.stop] = (
                    dobv * (onr * gnv) * (sz * (1.0 + z * (1.0 - sz)))).astype(BF16)
                dgns.append(jnp.sum(don * onr, axis=0, keepdims=True))
                gh_ = don * gnv
                dobs.append((r * (gh_ - onr * jnp.mean(gh_ * onr, axis=-1, keepdims=True))).astype(BF16))
            acc_ref[1:2, :] += jnp.concatenate(dgns, axis=1)

            gh = _hgrn_gate_heads(q_ref, f_ref, rows, lbv)
            b = _exact_mm(tri, jnp.concatenate([g["lg"] for g in gh], axis=1))
            dh = _hgrn_decay_heads(gh, b)
            vbs = [i_ref[rows, sl].astype(BF16) for sl in hs]

            st0s = [st_ref[ci, sl, :] for sl in hs]
            dst1s = [dst[sl, :] for sl in hs]
            dst1bs = [t.astype(BF16) for t in dst1s]
            as_ = [_nt(d["qsb"], d["ksb"]) for d in dh]
            das_ = [_nt(do, vb) for do, vb in zip(dobs, vbs)]
            dqes = [_nn(do, s0.astype(BF16)) for do, s0 in zip(dobs, st0s)]
            dkds = [_nn(vb, d1) for vb, d1 in zip(vbs, dst1bs)]
            dvis = [_nt(d["kdb"], d1) for d, d1 in zip(dh, dst1bs)]
            dsts = [_tn(do, d["qeb"]) for do, d in zip(dobs, dh)]
            abs_ = [jnp.where(low, a, 0.0).astype(BF16) for a in as_]
            dabs_ = [jnp.where(low, a, 0.0).astype(BF16) for a in das_]
            dqss = [_nn(da, d["ksb"]) for da, d in zip(dabs_, dh)]
            dkss = [_tn(da, d["qsb"]) for da, d in zip(dabs_, dh)]
            dvs_ = [_tn(a, do) + dvi for a, do, dvi in zip(abs_, dobs, dvis)]

            dqs_, dks_, dbs_, exs_ = [], [], [], []
            for hh, sl in enumerate(hs):
                d, d1, s0 = dh[hh], dst1s[hh], st0s[hh]
                dqe, dqs, dks, dkd = dqes[hh], dqss[hh], dkss[hh], dkds[hh]
                dst[sl, :] = dsts[hh] + d1 * d["ebl"]
                dh_ref[rows, 2 * D + sl.start:2 * D + sl.stop] = dvs_[hh].astype(BF16)
                dqs_.append((dqe * d["ebm"] + dqs) * d["e1"])
                dks_.append((dks + dkd * d["eblm"]) * d["e2"])
                dkdkd = dkd * d["kd"]
                dbs_.append(dqe * d["qe"] + dqs * d["qsb"].astype(F32) - dks * d["ksb"].astype(F32) - dkdkd)
                exs_.append(jnp.sum(dkdkd, axis=0, keepdims=True)
                            + jnp.sum(d1 * s0, axis=0, keepdims=True) * d["ebl"])
            dg = _exact_mm(triu, jnp.concatenate(dbs_, axis=1)) + jnp.concatenate(exs_, axis=1)

            dlbs = []
            for hh, sl in enumerate(hs):
                g = gh[hh]
                df = dg[:, sl] / g["f"] - dks_[hh]
                sf = g["sf"]
                omsf = 1.0 - sf
                sq = g["sq"]
                dlbs.append(jnp.sum(df * omsf, axis=0, keepdims=True))
                dh_ref[rows, sl] = (dqs_[hh] * (sq * (1.0 + g["qraw"] * (1.0 - sq)))).astype(BF16)
                dh_ref[rows, D + sl.start:D + sl.stop] = (df * (1.0 - lbv[:, sl]) * sf * omsf).astype(BF16)
            acc_ref[0:1, :] += jnp.concatenate(dlbs, axis=1)

    def pcol(c):
        return pl.BlockSpec((HRB, D), lambda i: (nblk - 1 - i, c))

    vec = pl.BlockSpec((1, D), lambda i: (0, 0))
    row = pl.BlockSpec((HRB, D), lambda i: (nblk - 1 - i, 0))
    return pl.pallas_call(
        body, name="hgrn_bwd", grid=(nblk,),
        in_specs=[pcol(2), pcol(3), pcol(4), pcol(5), row, row,
                  pl.BlockSpec((cps, NH * HK, HK), lambda i: (nblk - 1 - i, 0, 0)), vec, vec],
        out_specs=[pl.BlockSpec((HRB, 4 * D), lambda i: (nblk - 1 - i, 0)),
                   pl.BlockSpec((8, D), lambda i: (0, 0))],
        out_shape=[jax.ShapeDtypeStruct((S, 4 * D), BF16), jax.ShapeDtypeStruct((8, D), F32)],
        scratch_shapes=[pltpu.VMEM((NH * HK, HK), F32)],
        compiler_params=_cp(),
    )(proj, proj, proj, proj, oraw, dob, states, lb, gn)


def _merge_call(oa, ob, proj, x, tgt, vecs, attn, wa, wb, wo, hsum):
    S = x.shape[0]
    tm = 256

    def body(oa_ref, ob_ref, ga_ref, gb_ref, x_ref, t_ref, v_ref, at_ref, za_ref, wa_ref, wb_ref, wo_ref, hs_ref,
             y_ref, dx2_ref, du_ref, dya_ref, dyb_ref, dg_ref, dat_ref, dsum_ref, dza_ref, dob_ref, acc_ref, ls_ref):
        @pl.when(pl.program_id(0) == 0)
        def _():
            acc_ref[...] = jnp.zeros_like(acc_ref)
            ls_ref[...] = jnp.zeros_like(ls_ref)

        gate = v_ref[0:1, :]
        fg = v_ref[1:2, :]
        ya = _nn(oa_ref[...], wa_ref[...])
        yb = _nn(ob_ref[...], wb_ref[...])
        sa = _sig(ga_ref[...])
        sb = _sig(gb_ref[...])
        y = (sa * ya + sb * yb).astype(BF16)
        y_ref[...] = y
        u = _nn(y, wo_ref[...])
        x2v = x_ref[...] + gate * u
        r = lax.rsqrt(jnp.mean(x2v * x2v, axis=-1, keepdims=True) + EPS)
        err = x2v * r * fg - t_ref[...]
        ls_ref[...] += jnp.sum(err * err)
        dout = err * (1.0 / D)
        gh = dout * fg
        dx2 = r * gh - x2v * (r * r * r * jnp.mean(gh * x2v, axis=-1, keepdims=True))
        acc_ref[0:1, :] += jnp.sum(dx2 * u, axis=0, keepdims=True)
        acc_ref[1:2, :] += jnp.sum(dout * x2v * r, axis=0, keepdims=True)
        dx2_ref[...] = dx2
        du = (dx2 * gate).astype(BF16)
        du_ref[...] = du
        dy = _nt(du, wo_ref[...])
        dya = (dy * sa).astype(BF16)
        dyb = (dy * sb).astype(BF16)
        dya_ref[...] = dya
        dyb_ref[...] = dyb
        dg_ref[:, 0:D] = (dy * ya * sa * (1.0 - sa)).astype(BF16)
        dg_ref[:, D:2 * D] = (dy * yb * sb * (1.0 - sb)).astype(BF16)
        doa = _nt(dya, wa_ref[...])
        dob_ref[...] = _nt(dyb, wb_ref[...])
        za = za_ref[...]
        sz = _sig(za)
        att = at_ref[...]
        dat = doa * (za * sz)
        dat_ref[...] = dat
        dza_ref[...] = (doa * att * (sz * (1.0 + za * (1.0 - sz)))).astype(BF16)
        dsum_ref[...] = _exact_mm_r(dat * att, hs_ref[...])

    row = pl.BlockSpec((tm, D), lambda i: (i, 0))
    arow = pl.BlockSpec((tm, AW), lambda i: (i, 0))
    full = lambda a: pl.BlockSpec(a.shape, lambda i: (0, 0))
    return pl.pallas_call(
        body, name="merge_fwd_bwd", grid=(S // tm,),
        in_specs=[arow, row, pl.BlockSpec((tm, D), lambda i: (i, 6)), pl.BlockSpec((tm, D), lambda i: (i, 7)),
                  row, row, pl.BlockSpec((8, D), lambda i: (0, 0)), arow, pl.BlockSpec((tm, AW), lambda i: (i, 3)),
                  full(wa), full(wb), full(wo), full(hsum)],
        out_specs=[row, row, row, row, row, pl.BlockSpec((tm, 2 * D), lambda i: (i, 0)),
                   arow, arow, arow, row, pl.BlockSpec((8, D), lambda i: (0, 0)),
                   pl.BlockSpec((8, 128), lambda i: (0, 0))],
        out_shape=[jax.ShapeDtypeStruct((S, D), BF16), jax.ShapeDtypeStruct((S, D), F32),
                   jax.ShapeDtypeStruct((S, D), BF16), jax.ShapeDtypeStruct((S, D), BF16),
                   jax.ShapeDtypeStruct((S, D), BF16), jax.ShapeDtypeStruct((S, 2 * D), BF16),
                   jax.ShapeDtypeStruct((S, AW), F32), jax.ShapeDtypeStruct((S, AW), F32),
                   jax.ShapeDtypeStruct((S, AW), BF16), jax.ShapeDtypeStruct((S, D), F32),
                   jax.ShapeDtypeStruct((8, D), F32), jax.ShapeDtypeStruct((8, 128), F32)],
        compiler_params=_cp(),
    )(oa, ob, proj, proj, x, tgt, vecs, attn, proj, wa, wb, wo, hsum)


def _unused_fwd2_call(oa, ob, proj, x, tgt, vecs, wa, wb, wo):
    S = x.shape[0]
    tm = 256

    def body(oa_ref, ob_ref, ga_ref, gb_ref, x_ref, t_ref, v_ref, wa_ref, wb_ref, wo_ref,
             ya_ref, yb_ref, y_ref, u_ref, x2_ref, ls_ref):
        @pl.when(pl.program_id(0) == 0)
        def _():
            ls_ref[...] = jnp.zeros_like(ls_ref)

        ya = _nn(oa_ref[...], wa_ref[...])
        yb = _nn(ob_ref[...], wb_ref[...])
        y = _sig(ga_ref[...]) * ya + _sig(gb_ref[...]) * yb
        u = _nn(y.astype(BF16), wo_ref[...])
        x2 = x_ref[...] + v_ref[0:1, :] * u
        r = lax.rsqrt(jnp.mean(x2 * x2, axis=-1, keepdims=True) + EPS)
        err = x2 * r * v_ref[1:2, :] - t_ref[...]
        ls_ref[...] += jnp.sum(err * err)
        ya_ref[...] = ya.astype(BF16)
        yb_ref[...] = yb.astype(BF16)
        y_ref[...] = y.astype(BF16)
        u_ref[...] = u.astype(BF16)
        x2_ref[...] = x2

    row = pl.BlockSpec((tm, D), lambda i: (i, 0))
    full = lambda a: pl.BlockSpec(a.shape, lambda i: (0, 0))
    return pl.pallas_call(
        body, name="fwd_merge_out", grid=(S // tm,),
        in_specs=[pl.BlockSpec((tm, AW), lambda i: (i, 0)), row,
                  pl.BlockSpec((tm, D), lambda i: (i, 6)), pl.BlockSpec((tm, D), lambda i: (i, 7)),
                  row, row, pl.BlockSpec((8, D), lambda i: (0, 0)), full(wa), full(wb), full(wo)],
        out_specs=[row, row, row, row, row, pl.BlockSpec((8, 128), lambda i: (0, 0))],
        out_shape=[jax.ShapeDtypeStruct((S, D), BF16)] * 4
                  + [jax.ShapeDtypeStruct((S, D), F32), jax.ShapeDtypeStruct((8, 128), F32)],
        compiler_params=_cp(),
    )(oa, ob, proj, proj, x, tgt, vecs, wa, wb, wo)


def _bwd2_call(x2, tgt, vecs, ya, yb, u, proj, attn, wa, wb, wo, hsum):
    S = x2.shape[0]
    tm = 256

    def body(x2_ref, t_ref, v_ref, ya_ref, yb_ref, u_ref, ga_ref, gb_ref, at_ref, za_ref,
             wa_ref, wb_ref, wo_ref, hs_ref,
             dx2_ref, du_ref, dya_ref, dyb_ref, dg_ref, dat_ref, dsum_ref, dza_ref, dob_ref, acc_ref):
        @pl.when(pl.program_id(0) == 0)
        def _():
            acc_ref[...] = jnp.zeros_like(acc_ref)

        x2v = x2_ref[...]
        gate = v_ref[0:1, :]
        fg = v_ref[1:2, :]
        r = lax.rsqrt(jnp.mean(x2v * x2v, axis=-1, keepdims=True) + EPS)
        dout = (x2v * r * fg - t_ref[...]) * (1.0 / D)
        gh = dout * fg
        dx2 = r * gh - x2v * (r * r * r * jnp.mean(gh * x2v, axis=-1, keepdims=True))
        acc_ref[0:1, :] += jnp.sum(dx2 * u_ref[...].astype(F32), axis=0, keepdims=True)
        acc_ref[1:2, :] += jnp.sum(dout * x2v * r, axis=0, keepdims=True)
        dx2_ref[...] = dx2
        du = (dx2 * gate).astype(BF16)
        du_ref[...] = du
        dy = _nt(du, wo_ref[...])
        sa = _sig(ga_ref[...])
        sb = _sig(gb_ref[...])
        dya = (dy * sa).astype(BF16)
        dyb = (dy * sb).astype(BF16)
        dya_ref[...] = dya
        dyb_ref[...] = dyb
        dg_ref[:, 0:D] = (dy * ya_ref[...].astype(F32) * sa * (1.0 - sa)).astype(BF16)
        dg_ref[:, D:2 * D] = (dy * yb_ref[...].astype(F32) * sb * (1.0 - sb)).astype(BF16)
        doa = _nt(dya, wa_ref[...])
        dob_ref[...] = _nt(dyb, wb_ref[...])
        za = za_ref[...]
        sz = _sig(za)
        att = at_ref[...]
        dat = doa * (za * sz)
        dat_ref[...] = dat
        dza_ref[...] = (doa * att * (sz * (1.0 + za * (1.0 - sz)))).astype(BF16)
        dsum_ref[...] = _exact_mm_r(dat * att, hs_ref[...])

    row = pl.BlockSpec((tm, D), lambda i: (i, 0))
    arow = pl.BlockSpec((tm, AW), lambda i: (i, 0))
    full = lambda a: pl.BlockSpec(a.shape, lambda i: (0, 0))
    return pl.pallas_call(
        body, name="bwd_merge_out", grid=(S // tm,),
        in_specs=[row, row, pl.BlockSpec((8, D), lambda i: (0, 0)), row, row, row,
                  pl.BlockSpec((tm, D), lambda i: (i, 6)), pl.BlockSpec((tm, D), lambda i: (i, 7)),
                  arow, pl.BlockSpec((tm, AW), lambda i: (i, 3)), full(wa), full(wb), full(wo), full(hsum)],
        out_specs=[row, row, row, row, pl.BlockSpec((tm, 2 * D), lambda i: (i, 0)),
                   arow, arow, arow, row, pl.BlockSpec((8, D), lambda i: (0, 0))],
        out_shape=[jax.ShapeDtypeStruct((S, D), F32), jax.ShapeDtypeStruct((S, D), BF16),
                   jax.ShapeDtypeStruct((S, D), BF16), jax.ShapeDtypeStruct((S, D), BF16),
                   jax.ShapeDtypeStruct((S, 2 * D), BF16), jax.ShapeDtypeStruct((S, AW), F32),
                   jax.ShapeDtypeStruct((S, AW), F32), jax.ShapeDtypeStruct((S, AW), BF16),
                   jax.ShapeDtypeStruct((S, D), F32), jax.ShapeDtypeStruct((8, D), F32)],
        compiler_params=_cp(),
    )(x2, tgt, vecs, ya, yb, u, proj, proj, attn, proj, wa, wb, wo, hsum)


def _atb_call(a, b, name):
    S, K = a.shape
    N = b.shape[1]
    tm = min(1024, S)

    def body(a_ref, b_ref, o_ref, ob_ref):
        @pl.when(pl.program_id(0) == 0)
        def _():
            o_ref[...] = jnp.zeros_like(o_ref)

        o_ref[...] += _tn(a_ref[...], b_ref[...])

        @pl.when(pl.program_id(0) == S // tm - 1)
        def _():
            ob_ref[...] = o_ref[...].astype(BF16)

    ospec = pl.BlockSpec((K, N), lambda i: (0, 0))
    return pl.pallas_call(
        body, name=name, grid=(S // tm,),
        in_specs=[pl.BlockSpec((tm, K), lambda i: (i, 0)), pl.BlockSpec((tm, N), lambda i: (i, 0))],
        out_specs=[ospec, ospec],
        out_shape=[jax.ShapeDtypeStruct((K, N), F32), jax.ShapeDtypeStruct((K, N), BF16)], compiler_params=_cp(),
    )(a, b)


def _dwin_call(h_t, dqkvz, d_hgrn, d_gates):
    S = h_t.shape[1]
    tm = min(1024, S)
    tn = 1024

    def body(h_ref, q_ref, k_ref, v_ref, z_ref, m_ref, g_ref, o_ref, ob_ref):
        j = pl.program_id(0)

        @pl.when(pl.program_id(1) == 0)
        def _():
            o_ref[...] = jnp.zeros_like(o_ref)

        hv = h_ref[...]

        @pl.when(j == 0)
        def _():
            o_ref[0, :, 0:AW] += _nn(hv, q_ref[...])
            o_ref[0, :, AW:2 * AW] += _nn(hv, k_ref[...])

        @pl.when(j == 1)
        def _():
            o_ref[0, :, 0:AW] += _nn(hv, v_ref[...])
            o_ref[0, :, AW:2 * AW] += _nn(hv, z_ref[...])

        @pl.when(jnp.logical_and(j >= 2, j < 6))
        def _():
            o_ref[0] += _nn(hv, m_ref[...])

        @pl.when(j >= 6)
        def _():
            o_ref[0] += _nn(hv, g_ref[...])

        @pl.when(pl.program_id(1) == S // tm - 1)
        def _():
            ob_ref[...] = o_ref[...].astype(BF16)

    def aspec(jb):
        return pl.BlockSpec((tm, AW), lambda j, i: (jnp.where(j == jb, i, 0), 0))

    ospec = pl.BlockSpec((1, D, tn), lambda j, i: (j // 2, 0, j % 2))
    return pl.pallas_call(
        body, name="dw_in", grid=(8, S // tm),
        in_specs=[pl.BlockSpec((D, tm), lambda j, i: (0, i)), aspec(0), aspec(0), aspec(1), aspec(1),
                  pl.BlockSpec((tm, tn), lambda j, i: (jnp.where(jnp.logical_and(j >= 2, j < 6), i, 0),
                                                       jnp.clip(j - 2, 0, 3))),
                  pl.BlockSpec((tm, tn), lambda j, i: (jnp.where(j >= 6, i, 0), jnp.clip(j - 6, 0, 1)))],
        out_specs=[ospec, ospec],
        out_shape=[jax.ShapeDtypeStruct((4, D, 2 * tn), F32), jax.ShapeDtypeStruct((4, D, 2 * tn), BF16)],
        compiler_params=_cp(),
    )(h_t, *dqkvz, d_hgrn, d_gates)


def _dh_call(dqkvz, d_hgrn, d_gates, w_in, x, dx2, vecs):
    S = x.shape[0]
    tm = 256

    def body(q_ref, k_ref, v_ref, z_ref, m_ref, g_ref, w_ref, x_ref, dx2_ref, p_ref, gx_ref, acc_ref):
        @pl.when(pl.program_id(0) == 0)
        def _():
            acc_ref[...] = jnp.zeros_like(acc_ref)

        dhv = _nt(q_ref[...], w_ref[:, 0:AW])
        for cidx, r in enumerate((k_ref, v_ref, z_ref)):
            dhv += _nt(r[...], w_ref[:, (cidx + 1) * AW:(cidx + 2) * AW])
        dhv += _nt(m_ref[...], w_ref[:, 4 * AW:4 * AW + 4 * D])
        dhv += _nt(g_ref[...], w_ref[:, 4 * AW + 4 * D:NPROJ])
        xv = x_ref[...]
        r = lax.rsqrt(jnp.mean(xv * xv, axis=-1, keepdims=True) + EPS)
        xn = xv * r
        acc_ref[0:1, :] += jnp.sum(dhv, axis=0, keepdims=True)
        acc_ref[1:2, :] += jnp.sum(dhv * xn * p_ref[1:2, :], axis=0, keepdims=True)
        acc_ref[2:3, :] += jnp.sum(dhv * xn * p_ref[2:3, :], axis=0, keepdims=True)
        dxn = dhv * p_ref[0:1, :]
        gx_ref[...] = dx2_ref[...] + r * dxn - xv * (r * r * r * jnp.mean(dxn * xv, axis=-1, keepdims=True))

    row = pl.BlockSpec((tm, D), lambda i: (i, 0))
    aspec = pl.BlockSpec((tm, AW), lambda i: (i, 0))
    const = lambda shape: pl.BlockSpec(shape, lambda i: (0, 0))
    return pl.pallas_call(
        body, name="dh_gradx", grid=(S // tm,),
        in_specs=[aspec, aspec, aspec, aspec,
                  pl.BlockSpec((tm, 4 * D), lambda i: (i, 0)), pl.BlockSpec((tm, 2 * D), lambda i: (i, 0)),
                  pl.BlockSpec((D, NPROJ), lambda i: (0, 0), pipeline_mode=pl.Buffered(1)),
                  row, row, const((8, D))],
        out_specs=[row, const((8, D))],
        out_shape=[jax.ShapeDtypeStruct((S, D), F32), jax.ShapeDtypeStruct((8, D), F32)],
        compiler_params=_cp(),
    )(*dqkvz, d_hgrn, d_gates, w_in, x, dx2, vecs)


def _adamw_math(w, g, m, v):
    m = B1 * m + (1.0 - B1) * g
    v = B2 * v + (1.0 - B2) * (g * g)
    m_hat = m / (1.0 - B1 ** STEP)
    v_hat = v / (1.0 - B2 ** STEP)
    delta = -LR * (m_hat / (jnp.sqrt(v_hat) + AEPS) + WD * w)
    return delta, m, v


def _adamw_call(w, g, m, v, name):
    R, C = w.shape
    tr = R if R * C * 4 <= (1 << 20) else max(8, (1 << 20) // (C * 4))
    assert R % tr == 0

    def body(w_ref, g_ref, m_ref, v_ref, go_ref, d_ref, nm_ref, nv_ref):
        g = g_ref[...]
        go_ref[...] = g
        d_ref[...], nm_ref[...], nv_ref[...] = _adamw_math(w_ref[...], g, m_ref[...], v_ref[...])

    blk = pl.BlockSpec((tr, C), lambda i: (i, 0))
    return pl.pallas_call(
        body, name=name, grid=(R // tr,), in_specs=[blk] * 4, out_specs=[blk] * 4,
        out_shape=[jax.ShapeDtypeStruct((R, C), F32)] * 4, compiler_params=_cp(),
    )(w, g, m, v)


def _mod_call(c_all, w_ada_s, b_s):
    def body(c_ref, w_ref, b_ref, o_ref):
        cv = c_ref[...]
        sc = cv * _sig(cv)
        o_ref[...] = jnp.dot(sc, w_ref[...], preferred_element_type=F32,
                             precision=lax.Precision.HIGHEST) + b_ref[...]

    return pl.pallas_call(
        body, name="ada_mod", out_shape=jax.ShapeDtypeStruct((8, w_ada_s.shape[1]), F32),
        compiler_params=_cp(),
    )(c_all, w_ada_s, b_s)


def _ada_update_call(sct, dm, w, m, v):
    R, C = w.shape
    tr = 256

    def body(s_ref, d_ref, w_ref, m_ref, v_ref, g_ref, dl_ref, nm_ref, nv_ref):
        g = s_ref[:, 0:1] * d_ref[0:1, :]
        for b in range(1, 8):
            g = g + s_ref[:, b:b + 1] * d_ref[b:b + 1, :]
        g_ref[...] = g
        dl_ref[...], nm_ref[...], nv_ref[...] = _adamw_math(w_ref[...], g, m_ref[...], v_ref[...])

    blk = pl.BlockSpec((tr, C), lambda i: (i, 0))
    return pl.pallas_call(
        body, name="ada_update", grid=(R // tr,),
        in_specs=[pl.BlockSpec((tr, 8), lambda i: (i, 0)), pl.BlockSpec((8, C), lambda i: (0, 0)), blk, blk, blk],
        out_specs=[blk] * 4, out_shape=[jax.ShapeDtypeStruct((R, C), F32)] * 4, compiler_params=_cp(),
    )(sct, dm, w, m, v)


def _sum8_call(packs):
    def body(p_ref, o_ref):
        acc = p_ref[0]
        for k in range(1, 8):
            acc = acc + p_ref[k]
        o_ref[...] = acc

    return pl.pallas_call(
        body, name="sum_small", out_shape=jax.ShapeDtypeStruct(packs.shape[1:], F32), compiler_params=_cp(),
    )(packs)


def _local_step(x, tgt, shift, scale, gate, norm_g, hgrn_onorm_g, rel_bias, lb, final_g, weights_fn, hook=None):
    a = norm_g * (1.0 + scale)
    z6 = jnp.zeros((6, D), F32)
    h, h_t, proj, w_in, wa, wb, wo = weights_fn(x, jnp.concatenate([a, shift, z6], 0))

    biases = _bias_tiles(rel_bias)
    attn, lse, oa = _attn_fwd_call(proj, biases)

    gn = jnp.tile(hgrn_onorm_g, (1, NH))
    oraw, ob, states = _hgrn_fwd_call(proj, lb, gn)

    vecs2 = jnp.concatenate([gate, final_g, z6], 0)
    hsum = jnp.asarray(np.kron(np.eye(NH), np.ones((HE, HE))), BF16)
    y, dx2, du, dya, dyb, d_gates, dattn, dsum, dza, dob, acc2, lsq = _merge_call(
        oa, ob, proj, x, tgt, vecs2, attn, wa, wb, wo, hsum)
    loss = 0.5 * lsq[0, 0] / D
    d_wo, d_wo16 = _atb_call(y, du, "dw_out")
    d_wa, d_wa16 = _atb_call(oa, dya, "dw_branch_a")
    d_wb, d_wb16 = _atb_call(ob, dyb, "dw_branch_b")

    d_hgrn, acch = _hgrn_bwd_call(proj, oraw, dob, states, lb, gn)

    dq, dk, dv, dbs = _attn_bwd_call(proj, dattn, lse, dsum, biases)
    dqkvz = (dq, dk, dv, dza)

    d_win, d_win16 = _dwin_call(h_t, dqkvz, d_hgrn, d_gates)
    tok = hook((d_win, d_wa, d_wb, d_wo), (d_win16, d_wa16, d_wb16, d_wo16)) if hook is not None else 0.0
    one_scale = 1.0 + scale
    grad_x, acc1 = _dh_call(dqkvz, d_hgrn, d_gates, w_in, x, dx2,
                            jnp.concatenate([a + tok, norm_g, one_scale, jnp.zeros((5, D), F32)], 0))

    d_rel = jnp.zeros((NBUCKETS, NH), F32)
    for p, d in enumerate(PATTERNS):
        band, bucket = _band_bucket(d)
        onehot = (bucket[None] == np.arange(NBUCKETS)[:, None, None]) & band[None]
        d_rel = d_rel + jnp.einsum("hqk,bqk->bh", dbs[p], jnp.asarray(onehot, F32),
                                   precision=lax.Precision.HIGHEST)
    d_onorm = jnp.sum(acch[1].reshape(NH, HK), axis=0)

    zrow = jnp.zeros((D,), F32)
    pack = jnp.stack([acc1[0], acc1[1], acc2[0], acc1[2], acc2[1], acch[0],
                      zrow.at[:HK].set(d_onorm), zrow.at[0].set(loss),
                      zrow.at[:NBUCKETS * NH].set(d_rel.reshape(-1))] + [zrow] * 7, 0)
    return grad_x, d_win, d_wa, d_wb, d_wo, pack


def _me():
    return lax.axis_index("x"), lax.axis_index("y"), lax.axis_index("c")


def _peers(x, y):
    return [(1 - x, y), (x, 1 - y), (1 - x, 1 - y)]


def _allgather_small(blk, name):
    m_per, n = blk.shape

    def body(x_ref, out_ref, send_sems, recv_sems, local_sem):
        x, y, c = _me()
        me, sibling = (x, y, c), (x, y, 1 - c)
        chips = _peers(x, y)

        def rows(px, py, pc):
            return out_ref.at[pl.ds((4 * px + 2 * py + pc) * m_per, m_per), :]

        def copy(k, block, to, src=None):
            return pltpu.make_async_remote_copy(
                src_ref=rows(*block) if src is None else src, dst_ref=rows(*block),
                send_sem=send_sems.at[k], recv_sem=recv_sems.at[k], device_id=to, device_id_type=MESH)

        mine = pltpu.make_async_copy(x_ref, rows(*me), local_sem)
        mine.start()
        first = [copy(0, me, sibling, src=x_ref)]
        first += [copy(1 + j, me, (*chip, c), src=x_ref) for j, chip in enumerate(chips)]
        for cp in first:
            cp.start()
        passed = [copy(4 + j, (*chip, c), sibling) for j, chip in enumerate(chips)]
        for j, chip in enumerate(chips):
            copy(1 + j, (*chip, c), me).wait_recv()
            passed[j].start()
        copy(0, sibling, me).wait_recv()
        for j, chip in enumerate(chips):
            copy(4 + j, (*chip, 1 - c), me).wait_recv()
        for cp in first + passed:
            cp.wait_send()
        mine.wait()

    return pl.pallas_call(
        body, name=name, out_shape=jax.ShapeDtypeStruct((8 * m_per, n), blk.dtype),
        in_specs=[pl.BlockSpec(memory_space=pltpu.VMEM)], out_specs=pl.BlockSpec(memory_space=pltpu.VMEM),
        scratch_shapes=[pltpu.SemaphoreType.DMA((7,)), pltpu.SemaphoreType.DMA((7,)), pltpu.SemaphoreType.DMA],
    )(blk)


ANY = pl.BlockSpec(memory_space=pl.ANY)


HBM = pl.BlockSpec(memory_space=pltpu.HBM)
SEM = pl.BlockSpec(memory_space=pltpu.SEMAPHORE)
EFFECT = pltpu.SideEffectType.DATAFLOW_SIDE_EFFECTING


def _w_part(t, ref, j, half):
    if t == 0:
        return ref.at[pl.ds(half * (D // 2), D // 2), pl.ds(j * 2048, 2048)]
    if t == 1:
        return ref.at[pl.ds(half * (AW // 2), AW // 2), pl.ds(j * 256, 256)]
    return ref.at[pl.ds(j * 256 + half * 128, 128), :]


def _w_ici_copies(fulls, send_sems, recv_sems):
    x, y, c = _me()
    outs, ins = [], []
    for t in range(4):
        for k, chip in enumerate(_peers(x, y)):
            mine = _w_part(t, fulls[t], 2 * x + y, c)
            theirs = _w_part(t, fulls[t], 2 * chip[0] + chip[1], c)
            kw = dict(send_sem=send_sems.at[3 * t + k], recv_sem=recv_sems.at[3 * t + k],
                      device_id=(*chip, c), device_id_type=MESH)
            outs.append(pltpu.make_async_remote_copy(src_ref=mine, dst_ref=mine, **kw))
            ins.append(pltpu.make_async_remote_copy(src_ref=theirs, dst_ref=theirs, **kw))
    return outs, ins


def _gather_ici_start(fulls):
    def body(f0, f1, f2, f3, send_sems, recv_sems, t0, t1, t2, t3, token):
        for cp in _w_ici_copies([f0, f1, f2, f3], send_sems, recv_sems)[0]:
            cp.start()
        token[...] = jnp.zeros_like(token)

    res = pl.pallas_call(
        body, name="gather_ici_start",
        out_shape=(pltpu.SemaphoreType.DMA((12,)), pltpu.SemaphoreType.DMA((12,)),
                   *[pltpu.HBM(a.shape, a.dtype) for a in fulls], jax.ShapeDtypeStruct((8, 128), F32)),
        in_specs=[HBM] * 4, out_specs=(SEM, SEM, *[HBM] * 4, pl.BlockSpec(memory_space=pltpu.VMEM)),
        input_output_aliases={i: 2 + i for i in range(4)},
        compiler_params=pltpu.CompilerParams(has_side_effects=EFFECT),
    )(*[pltpu.with_memory_space_constraint(a, pltpu.HBM) for a in fulls])
    return res[0], res[1], list(res[2:6]), res[6]


def _gather_ici_wait(send_sems, recv_sems, thru, after):
    def body(f0, f1, f2, f3, send_sems, recv_sems, after_ref, g0, g1, g2, g3):
        outs, ins = _w_ici_copies([f0, f1, f2, f3], send_sems, recv_sems)
        for cp in outs:
            cp.wait_send()
        for cp in ins:
            cp.wait_recv()

    return pl.pallas_call(
        body, name="gather_ici_wait", out_shape=tuple(pltpu.HBM(a.shape, a.dtype) for a in thru),
        in_specs=[HBM] * 4 + [SEM, SEM, ANY], out_specs=[HBM] * 4,
        input_output_aliases={i: i for i in range(4)},
        compiler_params=pltpu.CompilerParams(has_side_effects=EFFECT),
    )(*thru, send_sems, recv_sems, after)


def _gather_sibling(fulls):
    def body(i0, i1, i2, i3, o0, o1, o2, o3, send_sems, recv_sems):
        x, y, c = _me()
        outs = [o0, o1, o2, o3]
        cps = []
        for t in range(4):
            for k, chip in enumerate(_peers(x, y)):
                blk = _w_part(t, outs[t], 2 * chip[0] + chip[1], c)
                cp = pltpu.make_async_remote_copy(
                    src_ref=blk, dst_ref=blk, send_sem=send_sems.at[t, k], recv_sem=recv_sems.at[t, k],
                    device_id=(x, y, 1 - c), device_id_type=MESH)
                cp.start()
                cps.append(cp)
        for t in range(4):
            for k, chip in enumerate(_peers(x, y)):
                blk = _w_part(t, outs[t], 2 * chip[0] + chip[1], 1 - c)
                pltpu.make_async_remote_copy(
                    src_ref=blk, dst_ref=blk, send_sem=send_sems.at[t, k], recv_sem=recv_sems.at[t, k],
                    device_id=(x, y, 1 - c), device_id_type=MESH).wait_recv()
        for cp in cps:
            cp.wait_send()

    return pl.pallas_call(
        body, name="gather_sibling",
        out_shape=[jax.ShapeDtypeStruct(s, BF16) for s in FULL_W_SHAPES],
        in_specs=[ANY] * 4, out_specs=[ANY] * 4, input_output_aliases={0: 0, 1: 1, 2: 2, 3: 3},
        scratch_shapes=[pltpu.SemaphoreType.DMA((4, 3)), pltpu.SemaphoreType.DMA((4, 3))],
    )(*fulls)


def _half_of(t, ref, half):
    if t == 0:
        return ref.at[:, pl.ds(half * 512, 512), :]
    if t == 1:
        return ref.at[pl.ds(half * 256, 256), :]
    return ref.at[:, pl.ds(half * 512, 512)]


HALF_SHAPES = [(4, 512, 2048), (256, D), (D, 512), (D, 512)]
PIECE_SHAPES = [(512, 2048), (256, 256), (256, 512), (256, 512)]
SHARD_SHAPES = [(D, 2048), (AW, 256), (256, D), (256, D)]


def _chip_piece(t, ref, j):
    if t == 0:
        return ref.at[j]
    if t == 1:
        return ref.at[:, pl.ds(j * 256, 256)]
    return ref.at[pl.ds(j * 256, 256), :]


def _reduce_sibling_send(gs):
    def body(g0, g1, g2, g3, r0, r1, r2, r3, send_sems, recv_sems):
        x, y, c = _me()
        ins, outs = [g0, g1, g2, g3], [r0, r1, r2, r3]
        cps = []
        for t in range(4):
            cp = pltpu.make_async_remote_copy(
                src_ref=_half_of(t, ins[t], 1 - c), dst_ref=outs[t],
                send_sem=send_sems.at[t], recv_sem=recv_sems.at[t], device_id=(x, y, 1 - c), device_id_type=MESH)
            cp.start()
            cps.append(cp)
        for cp in cps:
            cp.wait_recv()
        for cp in cps:
            cp.wait_send()

    return pl.pallas_call(
        body, name="reduce_sibling", out_shape=[jax.ShapeDtypeStruct(s, BF16) for s in HALF_SHAPES],
        in_specs=[ANY] * 4, out_specs=[ANY] * 4,
        scratch_shapes=[pltpu.SemaphoreType.DMA((4,)), pltpu.SemaphoreType.DMA((4,))],
    )(*gs)


def _chip_copies(hs, lands, send_sems, recv_sems):
    x, y, c = _me()
    cps = []
    for t in range(4):
        for k, chip in enumerate(_peers(x, y)):
            pj = 2 * chip[0] + chip[1]
            cps.append(pltpu.make_async_remote_copy(
                src_ref=_chip_piece(t, hs[t], pj), dst_ref=lands[t].at[k],
                send_sem=send_sems.at[3 * t + k], recv_sem=recv_sems.at[3 * t + k],
                device_id=(*chip, c), device_id_type=MESH))
    return cps


def _reduce_chips_start(hs):
    lands = [lax.empty((3,) + s, BF16) for s in PIECE_SHAPES]

    def body(h0, h1, h2, h3, l0, l1, l2, l3, send_sems, recv_sems, t0, t1, t2, t3, t4, t5, t6, t7, token):
        for cp in _chip_copies([h0, h1, h2, h3], [l0, l1, l2, l3], send_sems, recv_sems):
            cp.start()
        token[...] = jnp.zeros_like(token)

    bufs = list(hs) + lands
    res = pl.pallas_call(
        body, name="reduce_chips_start",
        out_shape=(pltpu.SemaphoreType.DMA((12,)), pltpu.SemaphoreType.DMA((12,)),
                   *[pltpu.HBM(a.shape, a.dtype) for a in bufs], jax.ShapeDtypeStruct((8, 128), F32)),
        in_specs=[HBM] * 8, out_specs=(SEM, SEM, *[HBM] * 8, pl.BlockSpec(memory_space=pltpu.VMEM)),
        input_output_aliases={i: 2 + i for i in range(8)},
        compiler_params=pltpu.CompilerParams(has_side_effects=EFFECT),
    )(*[pltpu.with_memory_space_constraint(a, pltpu.HBM) for a in bufs])
    return res[0], res[1], list(res[2:10]), res[10]


def _reduce_chips_wait(send_sems, recv_sems, thru, after):
    def body(h0, h1, h2, h3, l0, l1, l2, l3, send_sems, recv_sems, after_ref, d0, d1, d2, d3, g0, g1, g2, g3):
        cps = _chip_copies([h0, h1, h2, h3], [l0, l1, l2, l3], send_sems, recv_sems)
        for cp in cps:
            cp.wait_send()
        for cp in cps:
            cp.wait_recv()

    res = pl.pallas_call(
        body, name="reduce_chips_wait", out_shape=tuple(pltpu.HBM(a.shape, a.dtype) for a in thru),
        in_specs=[HBM] * 8 + [SEM, SEM, ANY], out_specs=[HBM] * 8,
        input_output_aliases={i: i for i in range(8)},
        compiler_params=pltpu.CompilerParams(has_side_effects=EFFECT),
    )(*thru, send_sems, recv_sems, after)
    return list(res[4:8])


def _share_sibling(shards):
    def body(i0, i1, i2, i3, o0, o1, o2, o3, send_sems, recv_sems):
        x, y, c = _me()
        outs = [o0, o1, o2, o3]

        def half(t, ref, hf):
            if t == 0:
                return ref.at[pl.ds(hf * 512, 512), :]
            if t == 1:
                return ref.at[pl.ds(hf * 256, 256), :]
            return ref.at[:, pl.ds(hf * 512, 512)]

        cps = []
        for t in range(4):
            mine = half(t, outs[t], c)
            cp = pltpu.make_async_remote_copy(
                src_ref=mine, dst_ref=mine, send_sem=send_sems.at[t], recv_sem=recv_sems.at[t],
                device_id=(x, y, 1 - c), device_id_type=MESH)
            cp.start()
            cps.append(cp)
        for t in range(4):
            theirs = half(t, outs[t], 1 - c)
            pltpu.make_async_remote_copy(
                src_ref=theirs, dst_ref=theirs, send_sem=send_sems.at[t],
                recv_sem=recv_sems.at[t], device_id=(x, y, 1 - c), device_id_type=MESH).wait_recv()
        for cp in cps:
            cp.wait_send()

    return pl.pallas_call(
        body, name="share_sibling", out_shape=[jax.ShapeDtypeStruct(s, F32) for s in SHARD_SHAPES],
        in_specs=[ANY] * 4, out_specs=[ANY] * 4, input_output_aliases={0: 0, 1: 1, 2: 2, 3: 3},
        scratch_shapes=[pltpu.SemaphoreType.DMA((4,)), pltpu.SemaphoreType.DMA((4,))],
    )(*shards)


def _half_blockspec(t, idx_pos):
    if t == 0:
        return pl.BlockSpec((1, 512, 2048), lambda i, s: (i, s[idx_pos], 0)), 4
    if t == 1:
        return pl.BlockSpec((256, D), lambda i, s: (s[idx_pos], 0)), 1
    return pl.BlockSpec((256, 512), lambda i, s: (i, s[idx_pos])), 4


def _half_out_blockspec(t):
    if t == 0:
        return pl.BlockSpec((1, 512, 2048), lambda i, s: (i, 0, 0))
    if t == 1:
        return pl.BlockSpec((256, D), lambda i, s: (0, 0))
    return pl.BlockSpec((256, 512), lambda i, s: (i, 0))


def _add_half_call(t, own, recv, sc, name):
    in_blk, steps = _half_blockspec(t, 0)
    out_blk = _half_out_blockspec(t)

    def body(s_ref, a_ref, b_ref, o_ref, ob_ref):
        v = a_ref[...] + b_ref[...].astype(F32)
        o_ref[...] = v
        ob_ref[...] = v.astype(BF16)

    return pl.pallas_call(
        body, name=name,
        grid_spec=pltpu.PrefetchScalarGridSpec(
            num_scalar_prefetch=1, grid=(steps,), in_specs=[in_blk, out_blk], out_specs=[out_blk, out_blk]),
        out_shape=[jax.ShapeDtypeStruct(HALF_SHAPES[t], F32), jax.ShapeDtypeStruct(HALF_SHAPES[t], BF16)],
        compiler_params=_cp(),
    )(sc, own, recv)


def _final_piece_call(t, chipsum, recv3, sc, name):
    ps = PIECE_SHAPES[t]
    if t == 0:
        own_blk = pl.BlockSpec((1,) + ps, lambda i, s: (s[1], 0, 0))
        o_blk = pl.BlockSpec(ps, lambda i, s: (s[0], 0))
    elif t == 1:
        own_blk = pl.BlockSpec(ps, lambda i, s: (0, s[1]))
        o_blk = pl.BlockSpec(ps, lambda i, s: (s[0], 0))
    else:
        own_blk = pl.BlockSpec(ps, lambda i, s: (s[1], 0))
        o_blk = pl.BlockSpec(ps, lambda i, s: (0, s[0]))
    r_blk = pl.BlockSpec((3,) + ps, lambda i, s: (0, 0, 0))

    def body(s_ref, a_ref, r_ref, o_ref):
        a = a_ref[0] if t == 0 else a_ref[...]
        o_ref[...] = ((a + r_ref[0].astype(F32)) + r_ref[1].astype(F32)) + r_ref[2].astype(F32)

    return pl.pallas_call(
        body, name=name,
        grid_spec=pltpu.PrefetchScalarGridSpec(
            num_scalar_prefetch=1, grid=(1,), in_specs=[own_blk, r_blk], out_specs=o_blk),
        out_shape=jax.ShapeDtypeStruct(SHARD_SHAPES[t], F32), compiler_params=_cp(),
    )(sc, chipsum, recv3)


FULL_W_SHAPES = [(D, NPROJ), (AW, D), (D, D), (D, D)]


def _cast_place_call(t, shard, sc, name):
    if t == 0:
        blk, steps = (512, 2048), 2
        in_blk = pl.BlockSpec(blk, lambda i, s: (i, 0))
        o_blk = pl.BlockSpec(blk, lambda i, s: (i, s[1]))
    elif t == 1:
        blk, steps = (AW, 256), 1
        in_blk = pl.BlockSpec(blk, lambda i, s: (0, 0))
        o_blk = pl.BlockSpec(blk, lambda i, s: (0, s[1]))
    else:
        blk, steps = (256, D), 1
        in_blk = pl.BlockSpec(blk, lambda i, s: (0, 0))
        o_blk = pl.BlockSpec(blk, lambda i, s: (s[1], 0))

    def body(s_ref, a_ref, o_ref, own_ref):
        v = a_ref[...].astype(BF16)
        o_ref[...] = v
        own_ref[...] = v

    return pl.pallas_call(
        body, name=name,
        grid_spec=pltpu.PrefetchScalarGridSpec(
            num_scalar_prefetch=1, grid=(steps,), in_specs=[in_blk], out_specs=[o_blk, in_blk]),
        out_shape=[jax.ShapeDtypeStruct(FULL_W_SHAPES[t], BF16), jax.ShapeDtypeStruct(shard.shape, BF16)],
        compiler_params=_cp(),
    )(sc, shard)


def _lower_bound_fn(hgrn_lb):
    return jnp.cumsum(jax.nn.softmax(hgrn_lb.astype(F32), axis=0), axis=0)[0]


def kernel(x, c, w_ada, b_ada, norm_g, w_in, hgrn_onorm_g, w_branch_a, w_branch_b, w_out, rel_bias, hgrn_lb, final_g, loss_target, m_w_ada, m_b_ada, m_norm_g, m_w_in, m_hgrn_onorm_g, m_w_branch_a, m_w_branch_b, m_w_out, m_rel_bias, m_hgrn_lb, m_final_g, v_w_ada, v_b_ada, v_norm_g, v_w_in, v_hgrn_onorm_g, v_w_branch_a, v_w_branch_b, v_w_out, v_rel_bias, v_hgrn_lb, v_final_g):
    ax, ay, ac = _me()
    chip = 2 * ax + ay
    dev = 4 * ax + 2 * ay + ac
    sc_idx = jnp.stack([ac, chip]).astype(jnp.int32)

    c_all = _allgather_small(jnp.pad(c, ((0, 7), (0, 0))), "gather_c").reshape(8, 8, D)[:, 0]
    b_s = lax.dynamic_slice(b_ada, (0, chip * 768), (1, 768))
    mod_part = _mod_call(c_all, w_ada[0], b_s)
    mod_all = _allgather_small(mod_part, "gather_mod").reshape(8, 8, 768)
    mod_mine = lax.dynamic_index_in_dim(mod_all, dev, axis=1, keepdims=False)
    mod = mod_mine[0::2].reshape(1, 3 * D)

    names = ["w_in", "w_a", "w_b", "w_o"]
    shards, mod = lax.optimization_barrier(([w_in[0], w_branch_a[0], w_branch_b[0], w_out[0]], mod))
    placed = [_cast_place_call(t, shards[t], sc_idx, "cast_" + names[t]) for t in range(4)]
    w_send_sems, w_recv_sems, w_thru, w_token = _gather_ici_start([p_[0] for p_ in placed])
    mod = mod + w_token[0, 0]
    shift, scale, gate = mod[:, :D], mod[:, D:2 * D], mod[:, 2 * D:]

    def weights_fn(xs, avec):
        proj, h, h_t = _proj_own_call(xs, avec, placed[0][1], sc_idx)
        arrived = _gather_ici_wait(w_send_sems, w_recv_sems, w_thru, proj)
        win_f, wa_f, wb_f, wo_f = _gather_sibling(arrived)
        return h, h_t, _proj_rest_call(h, win_f, proj, sc_idx), win_f, wa_f, wb_f, wo_f

    flight = {}

    def start_reduction(own, own16):
        sib = _reduce_sibling_send(own16)
        halves = [_add_half_call(t, own[t], sib[t], sc_idx, "chipsum_" + names[t]) for t in range(4)]
        send_sems, recv_sems, thru, token = _reduce_chips_start([hb for _, hb in halves])
        flight.update(sems=(send_sems, recv_sems), thru=thru, sums=[hf for hf, _ in halves])
        return token[0, 0]

    lb, lb_vjp = jax.vjp(_lower_bound_fn, hgrn_lb)
    grad_x, d_win, d_wa, d_wb, d_wo, pack = _local_step(
        x[0], loss_target[0], shift, scale, gate, norm_g, hgrn_onorm_g, rel_bias, lb[None, :],
        final_g[None, :], weights_fn, hook=start_reduction)

    rec = _reduce_chips_wait(*flight["sems"], flight["thru"], pack)
    pieces = [_final_piece_call(t, flight["sums"][t], rec[t], sc_idx, "piece_" + names[t]) for t in range(4)]
    g_win, g_wa, g_wb, g_wo = _share_sibling(pieces)

    packs = _allgather_small(pack, "gather_small").reshape(8, 16, D)
    tot = _sum8_call(packs)
    loss = tot[7, 0]
    g_b_ada = tot[0:3].reshape(1, 3 * D)
    g_norm_g = tot[3:4]
    g_final_g = tot[4]
    (g_hgrn_lb,) = lb_vjp(tot[5])
    g_onorm = tot[6:7, :HK]
    g_rel = tot[8, :NBUCKETS * NH].reshape(NBUCKETS, NH)

    def rows_of(a):
        flat = a.reshape(-1)
        n = -(-flat.shape[0] // D)
        return jnp.pad(flat, (0, n * D - flat.shape[0])).reshape(n, D)

    smalls = [(b_ada, g_b_ada, m_b_ada, v_b_ada), (norm_g, g_norm_g, m_norm_g, v_norm_g),
              (hgrn_onorm_g, g_onorm, m_hgrn_onorm_g, v_hgrn_onorm_g), (rel_bias, g_rel, m_rel_bias, v_rel_bias),
              (hgrn_lb, g_hgrn_lb, m_hgrn_lb, v_hgrn_lb), (final_g, g_final_g, m_final_g, v_final_g)]
    cat = [jnp.concatenate([rows_of(s[k]) for s in smalls], 0) for k in range(4)]
    cat = [jnp.pad(a, ((0, 16 - a.shape[0]), (0, 0))) for a in cat]
    _, sd, sm, sv = _adamw_call(*cat, "adamw_small")

    def unpack(packed):
        res, r = [], 0
        for s in smalls:
            n = -(-s[0].size // D)
            res.append(packed[r:r + n].reshape(-1)[:s[0].size].reshape(s[0].shape))
            r += n
        return res

    d_small, m_small, v_small = unpack(sd), unpack(sm), unpack(sv)

    sc_all = c_all * jax.nn.sigmoid(c_all)
    dmod_all = packs[:, 0:3].reshape(8, 3 * D)
    dm_s = lax.dynamic_slice(dmod_all, (0, chip * 768), (8, 768))
    g_w_ada, d_w_ada, nm_w_ada, nv_w_ada = _ada_update_call(sc_all.T, dm_s, w_ada[0], m_w_ada[0], v_w_ada[0])

    big = []
    for w, g, m, v, n in [(w_in, g_win, m_w_in, v_w_in, "w_in"), (w_branch_a, g_wa, m_w_branch_a, v_w_branch_a, "w_a"),
                          (w_branch_b, g_wb, m_w_branch_b, v_w_branch_b, "w_b"), (w_out, g_wo, m_w_out, v_w_out, "w_o")]:
        big.append(_adamw_call(w[0], g, m[0], v[0], "adamw_" + n))

    e = lambda a: a[None]
    grads = [e(g_w_ada), g_b_ada, g_norm_g, e(big[0][0]), g_onorm, e(big[1][0]), e(big[2][0]), e(big[3][0]),
             g_rel, g_hgrn_lb, g_final_g]
    deltas = [e(d_w_ada), d_small[0], d_small[1], e(big[0][1]), d_small[2], e(big[1][1]), e(big[2][1]), e(big[3][1]),
              d_small[3], d_small[4], d_small[5]]
    new_m = [e(nm_w_ada), m_small[0], m_small[1], e(big[0][2]), m_small[2], e(big[1][2]), e(big[2][2]), e(big[3][2]),
             m_small[3], m_small[4], m_small[5]]
    new_v = [e(nv_w_ada), v_small[0], v_small[1], e(big[0][3]), v_small[2], e(big[1][3]), e(big[2][3]), e(big[3][3]),
             v_small[3], v_small[4], v_small[5]]
    return (loss, grad_x[None], *grads, *deltas, *new_m, *new_v)
```

```python
import functools
import math

import numpy as np
import jax
import jax.numpy as jnp
from jax import lax
from jax.experimental import pallas as pl
from jax.experimental.pallas import tpu as pltpu

D = 1024
AW = 512
NH = 8
HE = 64
HK = 128
NPROJ = 8192
ABLK = 128
PATTERNS = (1, 4, 16)
NBUCKETS = 32
MAXDIST = 2048
NEG = -1e30
EPS = 1e-6
CH = 64
LR, B1, B2, AEPS, WD, STEP = 0.001, 0.9, 0.999, 1e-08, 0.01, 10

F32 = jnp.float32
BF16 = jnp.bfloat16
MESH = pl.DeviceIdType.MESH
VMEM_LIMIT = 56 * 1024 * 1024


def _cp(**kw):
    return pltpu.CompilerParams(vmem_limit_bytes=VMEM_LIMIT, **kw)


def _sig(x):
    return 0.5 * jnp.tanh(0.5 * x) + 0.5


def _nt(a, b):
    return lax.dot_general(a, b, (((1,), (1,)), ((), ())), preferred_element_type=F32)


def _tn(a, b):
    return lax.dot_general(a, b, (((0,), (0,)), ((), ())), preferred_element_type=F32)


def _nn(a, b):
    return jnp.dot(a, b, preferred_element_type=F32)


def _split2(x):
    h = x.astype(BF16)
    return h, (x - h.astype(F32)).astype(BF16)


def _exact_mm(tri_bf16, x):
    h, l = _split2(x)
    return _nn(tri_bf16, h) + _nn(tri_bf16, l)


def _exact_mm_r(x, ones_bf16):
    h, l = _split2(x)
    return _nn(h, ones_bf16) + _nn(l, ones_bf16)


def _h_call(x, avec):
    S = x.shape[0]
    tm = 512

    def body(x_ref, a_ref, h_ref, ht_ref):
        xv = x_ref[...]
        r = lax.rsqrt(jnp.mean(xv * xv, axis=-1, keepdims=True) + EPS)
        hv = xv * r * a_ref[0:1, :] + a_ref[1:2, :]
        h_ref[...] = hv.astype(BF16)
        ht_ref[...] = hv.T.astype(BF16)

    return pl.pallas_call(
        body, name="h_norm", grid=(S // tm,),
        in_specs=[pl.BlockSpec((tm, D), lambda i: (i, 0)), pl.BlockSpec((8, D), lambda i: (0, 0))],
        out_specs=[pl.BlockSpec((tm, D), lambda i: (i, 0)), pl.BlockSpec((D, tm), lambda i: (0, i))],
        out_shape=[jax.ShapeDtypeStruct((S, D), BF16), jax.ShapeDtypeStruct((D, S), BF16)], compiler_params=_cp(),
    )(x, avec)


def _proj_call(h, w_in):
    S = h.shape[0]
    tm, tn = 512, 2048

    def body(h_ref, w_ref, o_ref):
        o_ref[...] = _nn(h_ref[...], w_ref[...])

    return pl.pallas_call(
        body, name="in_proj", grid=(NPROJ // tn, S // tm),
        in_specs=[pl.BlockSpec((tm, D), lambda j, i: (i, 0)), pl.BlockSpec((D, tn), lambda j, i: (0, j))],
        out_specs=pl.BlockSpec((tm, tn), lambda j, i: (i, j)),
        out_shape=jax.ShapeDtypeStruct((S, NPROJ), F32), compiler_params=_cp(),
    )(h, w_in)


def _proj_own_call(x, avec, w_own, sc):
    S = x.shape[0]
    tm, tn = 512, 2048

    def body(s_ref, x_ref, a_ref, w_ref, o_ref, h_ref, ht_ref):
        xv = x_ref[...]
        r = lax.rsqrt(jnp.mean(xv * xv, axis=-1, keepdims=True) + EPS)
        hv = xv * r * a_ref[0:1, :] + a_ref[1:2, :]
        hb = hv.astype(BF16)
        h_ref[...] = hb
        ht_ref[...] = hv.T.astype(BF16)
        o_ref[...] = _nn(hb, w_ref[...])

    return pl.pallas_call(
        body, name="in_proj_own",
        grid_spec=pltpu.PrefetchScalarGridSpec(
            num_scalar_prefetch=1, grid=(S // tm,),
            in_specs=[pl.BlockSpec((tm, D), lambda i, s: (i, 0)), pl.BlockSpec((8, D), lambda i, s: (0, 0)),
                      pl.BlockSpec((D, tn), lambda i, s: (0, 0))],
            out_specs=[pl.BlockSpec((tm, tn), lambda i, s: (i, s[1])), pl.BlockSpec((tm, D), lambda i, s: (i, 0)),
                       pl.BlockSpec((D, tm), lambda i, s: (0, i))]),
        out_shape=[jax.ShapeDtypeStruct((S, NPROJ), F32), jax.ShapeDtypeStruct((S, D), BF16),
                   jax.ShapeDtypeStruct((D, S), BF16)],
        compiler_params=_cp(),
    )(sc, x, avec, w_own)


def _proj_rest_call(h, w_in, proj, sc):
    S = h.shape[0]
    tm, tn = 512, 2048

    def body(s_ref, h_ref, w_ref, p_ref, o_ref):
        o_ref[...] = _nn(h_ref[...], w_ref[...])

    col = lambda j, s: (s[1] + 1 + j) % 4
    return pl.pallas_call(
        body, name="in_proj_rest",
        grid_spec=pltpu.PrefetchScalarGridSpec(
            num_scalar_prefetch=1, grid=(3, S // tm),
            in_specs=[pl.BlockSpec((tm, D), lambda j, i, s: (i, 0)),
                      pl.BlockSpec((D, tn), lambda j, i, s: (0, col(j, s))),
                      pl.BlockSpec(memory_space=pl.ANY)],
            out_specs=pl.BlockSpec((tm, tn), lambda j, i, s: (i, col(j, s)))),
        out_shape=jax.ShapeDtypeStruct((S, NPROJ), F32), input_output_aliases={3: 0}, compiler_params=_cp(),
    )(sc, h, w_in, proj)


def _t5_bucket_np(dist):
    max_exact = NBUCKETS // 2
    n = dist.astype(np.float32)
    large = max_exact + (np.log(np.maximum(n, np.float32(1.0)) / np.float32(max_exact))
                         / np.float32(math.log(MAXDIST / max_exact))
                         * np.float32(NBUCKETS - max_exact)).astype(np.int32)
    large = np.minimum(large, NBUCKETS - 1)
    return np.where(dist < max_exact, dist, large)


def _band_bucket(d):
    qi = np.arange(ABLK)[:, None]
    kj = np.arange(2 * ABLK)[None, :]
    delta = qi + ABLK - kj
    band = (delta >= 0) & (delta <= ABLK)
    bucket = _t5_bucket_np(np.clip(delta, 0, None) * d)
    return band, bucket


def _bias_tiles(rel_bias):
    tiles = []
    for d in PATTERNS:
        band, bucket = _band_bucket(d)
        onehot = (jnp.asarray(bucket, jnp.int32)[None] == jnp.arange(NBUCKETS, dtype=jnp.int32)[:, None, None])
        bias = jnp.einsum("bqk,bh->hqk", onehot.astype(F32), rel_bias, precision=lax.Precision.HIGHEST)
        tiles.append(jnp.where(jnp.asarray(band)[None], bias, NEG))
    return jnp.stack(tiles, 0)


ATT = 2048
HP = 2 * HE
LOG2E = 1.4426950408889634
LN2 = 0.6931471805599453
QSCALE2 = (HE ** -0.5) * LOG2E
AGRP = 8
AGRP_B = 4


def _attn_blocks():
    out = []
    for p, d in enumerate(PATTERNS):
        for r in range(d):
            for n in range(ATT // (d * ABLK)):
                out.append((p, d, r, n))
    return out


def _attn_fwd_call(proj, biases):
    S = proj.shape[0]
    nt = S // ATT

    def body(q_ref, k_ref, v_ref, z_ref, b_ref, a_ref, l_ref, oa_ref, kc, vc, op, lp):
        i = pl.program_id(1)

        @pl.when(i == 0)
        def _():
            kc[0:ATT] = jnp.zeros((ATT, HP), F32)
            vc[0:ATT] = jnp.zeros((ATT, HP), F32)

        @pl.when(i > 0)
        def _():
            kc[0:ATT] = kc[ATT:2 * ATT]
            vc[0:ATT] = vc[ATT:2 * ATT]

        kc[ATT:2 * ATT] = k_ref[...]
        vc[ATT:2 * ATT] = v_ref[...]
        col = lax.broadcasted_iota(jnp.int32, (ABLK, 2 * ABLK), 1)
        dead = jnp.logical_and(i == 0, col < ABLK)
        blocks = _attn_blocks()
        hs = (slice(0, HE), slice(HE, 2 * HE))
        for g0 in range(0, len(blocks), AGRP):
            grp = blocks[g0:g0 + AGRP]
            qrows = [pl.ds(n * ABLK * d + r, ABLK, stride=d) for p, d, r, n in grp]
            krows = [pl.ds(ATT + (n - 1) * ABLK * d + r, 2 * ABLK, stride=d) for p, d, r, n in grp]
            qs = [(q_ref[qr, :] * QSCALE2).astype(BF16) for qr in qrows]
            ks = [kc[kr, :].astype(BF16) for kr in krows]
            vs = [vc[kr, :].astype(BF16) for kr in krows]
            ss = [[_nt(qs[b][:, sl], ks[b][:, sl]) + b_ref[grp[b][0], e] for e, sl in enumerate(hs)]
                  for b in range(len(grp))]
            ss = [[jnp.where(dead, NEG, s) if grp[b][3] == 0 else s for s in ss[b]] for b in range(len(grp))]
            mxs = [[jnp.max(s, axis=-1, keepdims=True) for s in sb] for sb in ss]
            pes = [[jnp.exp2(s - mx) for s, mx in zip(sb, mb)] for sb, mb in zip(ss, mxs)]
            dens = [[jnp.sum(pe, axis=-1, keepdims=True) for pe in pb] for pb in pes]
            pvs = [[_nn(pe.astype(BF16), vs[b][:, sl]) for pe, sl in zip(pes[b], hs)] for b in range(len(grp))]
            for b in range(len(grp)):
                p, d, r, n = grp[b]
                prow = pl.ds(p * ATT + n * ABLK * d + r, ABLK, stride=d)
                lp[prow, :] = jnp.concatenate(
                    [jnp.broadcast_to(mx + jnp.log2(dn), (ABLK, HE)) for mx, dn in zip(mxs[b], dens[b])], axis=1)
                op[prow, :] = jnp.concatenate([pv / dn for pv, dn in zip(pvs[b], dens[b])], axis=1)
        rt = 256
        for t in range(ATT // rt):
            rows = slice(t * rt, (t + 1) * rt)
            pr = [slice(p * ATT + t * rt, p * ATT + (t + 1) * rt) for p in range(3)]
            la, lb_, lc = lp[pr[0], :], lp[pr[1], :], lp[pr[2], :]
            m = jnp.maximum(jnp.maximum(la, lb_), lc)
            ea, eb, ec = jnp.exp2(la - m), jnp.exp2(lb_ - m), jnp.exp2(lc - m)
            den = ea + eb + ec
            att = (ea * op[pr[0], :] + eb * op[pr[1], :] + ec * op[pr[2], :]) / den
            a_ref[rows, :] = att
            l_ref[rows, :] = m + jnp.log2(den)
            z = z_ref[rows, :]
            oa_ref[rows, :] = (att * (z * _sig(z))).astype(BF16)

    def pcol(c):
        return pl.BlockSpec((ATT, HP), lambda h, i: (i, c * 4 + h))

    out = pl.BlockSpec((ATT, HP), lambda h, i: (i, h))
    return pl.pallas_call(
        body, name="attn_fwd", grid=(4, nt),
        in_specs=[pcol(0), pcol(1), pcol(2), pcol(3),
                  pl.BlockSpec((3, 2, ABLK, 2 * ABLK), lambda h, i: (0, h, 0, 0))],
        out_specs=[out, out, out],
        out_shape=[jax.ShapeDtypeStruct((S, AW), F32), jax.ShapeDtypeStruct((S, AW), F32),
                   jax.ShapeDtypeStruct((S, AW), BF16)],
        scratch_shapes=[pltpu.VMEM((2 * ATT, HP), F32), pltpu.VMEM((2 * ATT, HP), F32),
                        pltpu.VMEM((3 * ATT, HP), F32), pltpu.VMEM((3 * ATT, HP), F32)],
        compiler_params=_cp(),
    )(proj, proj, proj, proj, biases)


def _attn_bwd_call(proj, dattn, lse, dsum, biases):
    S = proj.shape[0]
    nt = S // ATT

    def body(q_ref, k_ref, v_ref, do_ref, l_ref, ds_ref, b_ref, dq_ref, dk_ref, dv_ref, db_ref,
             kc, vc, dkc, dvc, dqa):
        i = pl.program_id(1)

        @pl.when(i == 0)
        def _():
            kc[ATT:2 * ATT] = jnp.zeros((ATT, HP), F32)
            vc[ATT:2 * ATT] = jnp.zeros((ATT, HP), F32)
            dkc[ATT:2 * ATT] = jnp.zeros((ATT, HP), F32)
            dvc[ATT:2 * ATT] = jnp.zeros((ATT, HP), F32)
            db_ref[...] = jnp.zeros_like(db_ref)

        @pl.when(i < nt)
        def _():
            kc[0:ATT] = kc[ATT:2 * ATT]
            vc[0:ATT] = vc[ATT:2 * ATT]
            dkc[0:ATT] = dkc[ATT:2 * ATT]
            dvc[0:ATT] = dvc[ATT:2 * ATT]
            kc[ATT:2 * ATT] = k_ref[...]
            vc[ATT:2 * ATT] = v_ref[...]
            dkc[ATT:2 * ATT] = jnp.zeros((ATT, HP), F32)
            dvc[ATT:2 * ATT] = jnp.zeros((ATT, HP), F32)
            col = lax.broadcasted_iota(jnp.int32, (ABLK, 2 * ABLK), 1)
            dead = jnp.logical_and(i == 0, col < ABLK)
            blocks = _attn_blocks()
            hs = (slice(0, HE), slice(HE, 2 * HE))
            for g0 in range(0, len(blocks), AGRP_B):
                grp = blocks[g0:g0 + AGRP_B]
                nb_ = range(len(grp))
                qrows = [pl.ds(n * ABLK * d + r, ABLK, stride=d) for p, d, r, n in grp]
                krows = [pl.ds(ATT + (n - 1) * ABLK * d + r, 2 * ABLK, stride=d) for p, d, r, n in grp]
                qs = [(q_ref[qr, :] * QSCALE2).astype(BF16) for qr in qrows]
                ks = [kc[kr, :].astype(BF16) for kr in krows]
                vs = [vc[kr, :].astype(BF16) for kr in krows]
                dos = [do_ref[qr, :].astype(BF16) for qr in qrows]
                lvs = [l_ref[qr, :] for qr in qrows]
                dsvs = [ds_ref[qr, :] for qr in qrows]
                ss = [[_nt(qs[b][:, sl], ks[b][:, sl]) + b_ref[grp[b][0], e] for e, sl in enumerate(hs)] for b in nb_]
                ss = [[jnp.where(dead, NEG, s) if grp[b][3] == 0 else s for s in ss[b]] for b in nb_]
                dps = [[_nt(dos[b][:, sl], vs[b][:, sl]) for sl in hs] for b in nb_]
                pes = [[jnp.exp2(ss[b][e] - lvs[b][:, e * HE:e * HE + 1]) for e in range(2)] for b in nb_]
                dscs = [[pes[b][e] * (dps[b][e] - dsvs[b][:, e * HE:e * HE + 1]) for e in range(2)] for b in nb_]
                for e in range(2):
                    tot = dscs[0][e]
                    for b in range(1, len(grp)):
                        tot = tot + dscs[b][e]
                    db_ref[grp[0][0], e] += tot
                dsbs = [[t.astype(BF16) for t in tb] for tb in dscs]
                dqs = [[_nn(dsbs[b][e], ks[b][:, sl]) * (HE ** -0.5) for e, sl in enumerate(hs)] for b in nb_]
                dks = [[_tn(dsbs[b][e], qs[b][:, sl]) for e, sl in enumerate(hs)] for b in nb_]
                dvs = [[_tn(pes[b][e].astype(BF16), dos[b][:, sl]) for e, sl in enumerate(hs)] for b in nb_]
                for b in nb_:
                    dq = jnp.concatenate(dqs[b], axis=1)
                    if grp[b][0] == 0:
                        dqa[qrows[b], :] = dq
                    else:
                        dqa[qrows[b], :] += dq
                    dkc[krows[b], :] += jnp.concatenate(dks[b], axis=1)
                    dvc[krows[b], :] += jnp.concatenate(dvs[b], axis=1)
            dq_ref[...] = dqa[...].astype(BF16)
            dk_ref[...] = (dkc[0:ATT] * LN2).astype(BF16)
            dv_ref[...] = dvc[0:ATT].astype(BF16)

        @pl.when(i == nt)
        def _():
            dk_ref[...] = (dkc[ATT:2 * ATT] * LN2).astype(BF16)
            dv_ref[...] = dvc[ATT:2 * ATT].astype(BF16)

    def pcol(c):
        return pl.BlockSpec((ATT, HP), lambda h, i: (jnp.minimum(i, nt - 1), c * 4 + h))

    qrow = pl.BlockSpec((ATT, HP), lambda h, i: (jnp.minimum(i, nt - 1), h))
    krow = pl.BlockSpec((ATT, HP), lambda h, i: (jnp.maximum(i - 1, 0), h))
    bspec = pl.BlockSpec((3, 2, ABLK, 2 * ABLK), lambda h, i: (0, h, 0, 0))
    return pl.pallas_call(
        body, name="attn_bwd", grid=(4, nt + 1),
        in_specs=[pcol(0), pcol(1), pcol(2), qrow, qrow, qrow, bspec],
        out_specs=[qrow, krow, krow, bspec],
        out_shape=[jax.ShapeDtypeStruct((S, AW), BF16)] * 3
                  + [jax.ShapeDtypeStruct((3, NH, ABLK, 2 * ABLK), F32)],
        scratch_shapes=[pltpu.VMEM((2 * ATT, HP), F32)] * 4 + [pltpu.VMEM((ATT, HP), F32)],
        compiler_params=_cp(),
    )(proj, proj, proj, dattn, lse, dsum, biases)


HRB = 256


def _hgrn_gates(q_ref, f_ref, rows, lbv, tri):
    qraw = q_ref[rows, :]
    sq = _sig(qraw)
    q = qraw * sq
    sf = _sig(f_ref[rows, :])
    f = lbv + (1.0 - lbv) * sf
    k = 1.0 - f
    b = _exact_mm(tri, jnp.log(f))
    bl = b[CH - 1:CH, :]
    bm = b[CH // 2 - 1:CH // 2, :]
    e1 = jnp.exp(b - bm)
    e2 = jnp.exp(bm - b)
    ebm = jnp.exp(bm)
    eblm = jnp.exp(bl - bm)
    qs = q * e1
    ks = k * e2
    qe = qs * ebm
    kd = ks * eblm
    return dict(qraw=qraw, sq=sq, sf=sf, f=f, bl=bl, e1=e1, e2=e2, ebm=ebm, eblm=eblm, qe=qe, qs=qs, ks=ks, kd=kd)


def _unused_tri_masks():
    row = lax.broadcasted_iota(jnp.int32, (CH, CH), 0)
    col = lax.broadcasted_iota(jnp.int32, (CH, CH), 1)
    return row >= col


def _unused_hgrn_fwd_call(proj, lb, gn):
    S = proj.shape[0]
    nc = S // CH
    cps = HRB // CH

    def body(q_ref, f_ref, i_ref, z_ref, lb_ref, gn_ref, or_ref, ob_ref, st_ref, st):
        @pl.when(pl.program_id(0) == 0)
        def _():
            st[...] = jnp.zeros_like(st)

        low = _tri_masks()
        tri = low.astype(BF16)
        lbv = lb_ref[...]
        for ci in range(cps):
            rows = slice(ci * CH, (ci + 1) * CH)
            g = _hgrn_gates(q_ref, f_ref, rows, lbv, tri)
            v = i_ref[rows, :]
            ebl = jnp.exp(g["bl"])
            st_ref[ci] = st[...]
            hs = [slice(hh * HK, (hh + 1) * HK) for hh in range(NH)]
            vb = v.astype(BF16)
            qsb, ksb, qeb, kdb = (g[n_].astype(BF16) for n_ in ("qs", "ks", "qe", "kd"))
            s0s = [st[sl, :] for sl in hs]
            as_ = [_nt(qsb[:, sl], ksb[:, sl]) for sl in hs]
            ois = [_nt(qeb[:, sl], s0.astype(BF16)) for sl, s0 in zip(hs, s0s)]
            sts = [_tn(vb[:, sl], kdb[:, sl]) for sl in hs]
            abs_ = [jnp.where(low, a, 0.0).astype(BF16) for a in as_]
            os_ = [oi + _nn(a, vb[:, sl]) for oi, a, sl in zip(ois, abs_, hs)]
            for sl, s0, sn in zip(hs, s0s, sts):
                st[sl, :] = s0 * ebl[:, sl] + sn
            o = jnp.concatenate(os_, axis=1)
            or_ref[rows, :] = o
            rs = [lax.rsqrt(jnp.mean(oh * oh, axis=-1, keepdims=True) + EPS) for oh in os_]
            on = jnp.concatenate([oh * r for oh, r in zip(os_, rs)], axis=1)
            z = z_ref[rows, :]
            ob_ref[rows, :] = (on * gn_ref[...] * (z * _sig(z))).astype(BF16)

    def pcol(c):
        return pl.BlockSpec((HRB, D), lambda i: (i, c))

    vec = pl.BlockSpec((1, D), lambda i: (0, 0))
    row = pl.BlockSpec((HRB, D), lambda i: (i, 0))
    return pl.pallas_call(
        body, name="hgrn_fwd", grid=(S // HRB,),
        in_specs=[pcol(2), pcol(3), pcol(4), pcol(5), vec, vec],
        out_specs=[row, row, pl.BlockSpec((cps, NH * HK, HK), lambda i: (i, 0, 0))],
        out_shape=[jax.ShapeDtypeStruct((S, D), F32), jax.ShapeDtypeStruct((S, D), BF16),
                   jax.ShapeDtypeStruct((nc, NH * HK, HK), F32)],
        scratch_shapes=[pltpu.VMEM((NH * HK, HK), F32)],
        compiler_params=_cp(),
    )(proj, proj, proj, proj, lb, gn)


def _unused_hgrn_bwd_call(proj, oraw, dob, states, lb, gn):
    S = proj.shape[0]
    nblk = S // HRB
    cps = HRB // CH

    def body(q_ref, f_ref, i_ref, z_ref, or_ref, dob_ref, st_ref, lb_ref, gn_ref, dh_ref, acc_ref, dst):
        @pl.when(pl.program_id(0) == 0)
        def _():
            dst[...] = jnp.zeros_like(dst)
            acc_ref[...] = jnp.zeros_like(acc_ref)

        low = _tri_masks()
        tri = low.astype(BF16)
        triu = jnp.logical_not(_tri_masks()) | (lax.broadcasted_iota(jnp.int32, (CH, CH), 0)
                                               == lax.broadcasted_iota(jnp.int32, (CH, CH), 1))
        triu = triu.astype(BF16)
        lbv = lb_ref[...]
        for ci in reversed(range(cps)):
            rows = slice(ci * CH, (ci + 1) * CH)
            g = _hgrn_gates(q_ref, f_ref, rows, lbv, tri)
            v = i_ref[rows, :]
            ebl = jnp.exp(g["bl"])
            hs = [slice(hh * HK, (hh + 1) * HK) for hh in range(NH)]
            cat = lambda lst: jnp.concatenate(lst, axis=1)
            o = or_ref[rows, :]
            z = z_ref[rows, :]
            sz = _sig(z)
            gnv = gn_ref[...]
            dobv = dob_ref[rows, :]
            r = cat([jnp.broadcast_to(lax.rsqrt(jnp.mean(o[:, sl] * o[:, sl], axis=-1, keepdims=True) + EPS),
                                      (CH, HK)) for sl in hs])
            don = dobv * (z * sz)
            dz = dobv * (o * r * gnv) * (sz * (1.0 + z * (1.0 - sz)))
            dgn = jnp.sum(don * o * r, axis=0, keepdims=True)
            gh = don * gnv
            gho = gh * o
            mg = cat([jnp.broadcast_to(jnp.mean(gho[:, sl], axis=-1, keepdims=True), (CH, HK)) for sl in hs])
            dob16 = (r * gh - o * (r * r * r * mg)).astype(BF16)

            vb = v.astype(BF16)
            qsb, ksb, qeb, kdb = (g[n_].astype(BF16) for n_ in ("qs", "ks", "qe", "kd"))
            st0s = [st_ref[ci, sl, :] for sl in hs]
            dst1s = [dst[sl, :] for sl in hs]
            dst1bs = [t.astype(BF16) for t in dst1s]
            as_ = [_nt(qsb[:, sl], ksb[:, sl]) for sl in hs]
            das_ = [_nt(dob16[:, sl], vb[:, sl]) for sl in hs]
            dqes = [_nn(dob16[:, sl], s0.astype(BF16)) for sl, s0 in zip(hs, st0s)]
            dkds = [_nn(vb[:, sl], d1) for sl, d1 in zip(hs, dst1bs)]
            dvis = [_nt(kdb[:, sl], d1) for sl, d1 in zip(hs, dst1bs)]
            dsts = [_tn(dob16[:, sl], qeb[:, sl]) for sl in hs]
            abs_ = [jnp.where(low, a, 0.0).astype(BF16) for a in as_]
            dabs_ = [jnp.where(low, a, 0.0).astype(BF16) for a in das_]
            dqss = [_nn(da, ksb[:, sl]) for da, sl in zip(dabs_, hs)]
            dkss = [_tn(da, qsb[:, sl]) for da, sl in zip(dabs_, hs)]
            dvs_ = [_tn(a, dob16[:, sl]) + dvi for a, sl, dvi in zip(abs_, hs, dvis)]
            exs_ = [jnp.sum(d1 * s0, axis=0, keepdims=True) for d1, s0 in zip(dst1s, st0s)]
            for sl, d1, dn in zip(hs, dst1s, dsts):
                dst[sl, :] = dn + d1 * ebl[:, sl]
            dqe, dqs, dks, dkd, dv = cat(dqes), cat(dqss), cat(dkss), cat(dkds), cat(dvs_)
            dq = (dqe * g["ebm"] + dqs) * g["e1"]
            dk = (dks + dkd * g["eblm"]) * g["e2"]
            dkdkd = dkd * g["kd"]
            db = dqe * g["qe"] + dqs * qsb.astype(F32) - dks * ksb.astype(F32) - dkdkd
            ex = jnp.sum(dkdkd, axis=0, keepdims=True) + cat(exs_) * ebl
            dg = _exact_mm(triu, db) + ex
            df = dg / g["f"] - dk
            sf = g["sf"]
            dfr = df * (1.0 - lbv) * sf * (1.0 - sf)
            sq = g["sq"]
            dqr = dq * (sq * (1.0 + g["qraw"] * (1.0 - sq)))
            acc_ref[0:1, :] += jnp.sum(df * (1.0 - sf), axis=0, keepdims=True)
            acc_ref[1:2, :] += dgn
            dh_ref[rows, 0:D] = dqr.astype(BF16)
            dh_ref[rows, D:2 * D] = dfr.astype(BF16)
            dh_ref[rows, 2 * D:3 * D] = dv.astype(BF16)
            dh_ref[rows, 3 * D:4 * D] = dz.astype(BF16)

    def pcol(c):
        return pl.BlockSpec((HRB, D), lambda i: (nblk - 1 - i, c))

    vec = pl.BlockSpec((1, D), lambda i: (0, 0))
    row = pl.BlockSpec((HRB, D), lambda i: (nblk - 1 - i, 0))
    return pl.pallas_call(
        body, name="hgrn_bwd", grid=(nblk,),
        in_specs=[pcol(2), pcol(3), pcol(4), pcol(5), row, row,
                  pl.BlockSpec((cps, NH * HK, HK), lambda i: (nblk - 1 - i, 0, 0)), vec, vec],
        out_specs=[pl.BlockSpec((HRB, 4 * D), lambda i: (nblk - 1 - i, 0)),
                   pl.BlockSpec((8, D), lambda i: (0, 0))],
        out_shape=[jax.ShapeDtypeStruct((S, 4 * D), BF16), jax.ShapeDtypeStruct((8, D), F32)],
        scratch_shapes=[pltpu.VMEM((NH * HK, HK), F32)],
        compiler_params=_cp(),
    )(proj, proj, proj, proj, oraw, dob, states, lb, gn)


def _tri_masks():
    row = lax.broadcasted_iota(jnp.int32, (CH, CH), 0)
    col = lax.broadcasted_iota(jnp.int32, (CH, CH), 1)
    return row >= col


def _heads():
    return [slice(hh * HK, (hh + 1) * HK) for hh in range(NH)]


def _hgrn_gate_heads(q_ref, f_ref, rows, lbv):
    out = []
    for sl in _heads():
        qraw = q_ref[rows, sl]
        sq = _sig(qraw)
        sf = _sig(f_ref[rows, sl])
        f = lbv[:, sl] + (1.0 - lbv[:, sl]) * sf
        out.append(dict(qraw=qraw, sq=sq, q=qraw * sq, sf=sf, f=f, k=1.0 - f, lg=jnp.log(f)))
    return out


def _hgrn_decay_heads(gh, b):
    bl = b[CH - 1:CH, :]
    bm = b[CH // 2 - 1:CH // 2, :]
    ebm = jnp.exp(bm)
    eblm = jnp.exp(bl - bm)
    ebl = jnp.exp(bl)
    out = []
    for g, sl in zip(gh, _heads()):
        d = b[:, sl] - bm[:, sl]
        e1 = jnp.exp(d)
        e2 = jnp.exp(-d)
        qs = g["q"] * e1
        ks = g["k"] * e2
        qe = qs * ebm[:, sl]
        kd = ks * eblm[:, sl]
        out.append(dict(e1=e1, e2=e2, qe=qe, kd=kd, ebm=ebm[:, sl], eblm=eblm[:, sl], ebl=ebl[:, sl],
                        qsb=qs.astype(BF16), ksb=ks.astype(BF16), qeb=qe.astype(BF16), kdb=kd.astype(BF16)))
    return out


def _hgrn_fwd_call(proj, lb, gn):
    S = proj.shape[0]
    nc = S // CH
    cps = HRB // CH

    def body(q_ref, f_ref, i_ref, z_ref, lb_ref, gn_ref, or_ref, ob_ref, st_ref, st):
        @pl.when(pl.program_id(0) == 0)
        def _():
            st[...] = jnp.zeros_like(st)

        low = _tri_masks()
        tri = low.astype(BF16)
        lbv = lb_ref[...]
        hs = _heads()
        for ci in range(cps):
            rows = slice(ci * CH, (ci + 1) * CH)
            gh = _hgrn_gate_heads(q_ref, f_ref, rows, lbv)
            b = _exact_mm(tri, jnp.concatenate([g["lg"] for g in gh], axis=1))
            dh = _hgrn_decay_heads(gh, b)
            vbs = [i_ref[rows, sl].astype(BF16) for sl in hs]
            st_ref[ci] = st[...]
            s0s = [st[sl, :] for sl in hs]
            as_ = [_nt(d["qsb"], d["ksb"]) for d in dh]
            ois = [_nt(d["qeb"], s0.astype(BF16)) for d, s0 in zip(dh, s0s)]
            sts = [_tn(vb, d["kdb"]) for vb, d in zip(vbs, dh)]
            abs_ = [jnp.where(low, a, 0.0).astype(BF16) for a in as_]
            os_ = [oi + _nn(a, vb) for oi, a, vb in zip(ois, abs_, vbs)]
            for sl, s0, sn, d, o in zip(hs, s0s, sts, dh, os_):
                st[sl, :] = s0 * d["ebl"] + sn
                or_ref[rows, sl] = o
                r = lax.rsqrt(jnp.mean(o * o, axis=-1, keepdims=True) + EPS)
                z = z_ref[rows, sl]
                ob_ref[rows, sl] = (o * r * gn_ref[:, sl] * (z * _sig(z))).astype(BF16)

    def pcol(c):
        return pl.BlockSpec((HRB, D), lambda i: (i, c))

    vec = pl.BlockSpec((1, D), lambda i: (0, 0))
    row = pl.BlockSpec((HRB, D), lambda i: (i, 0))
    return pl.pallas_call(
        body, name="hgrn_fwd", grid=(S // HRB,),
        in_specs=[pcol(2), pcol(3), pcol(4), pcol(5), vec, vec],
        out_specs=[row, row, pl.BlockSpec((cps, NH * HK, HK), lambda i: (i, 0, 0))],
        out_shape=[jax.ShapeDtypeStruct((S, D), F32), jax.ShapeDtypeStruct((S, D), BF16),
                   jax.ShapeDtypeStruct((nc, NH * HK, HK), F32)],
        scratch_shapes=[pltpu.VMEM((NH * HK, HK), F32)],
        compiler_params=_cp(),
    )(proj, proj, proj, proj, lb, gn)


def _hgrn_bwd_call(proj, oraw, dob, states, lb, gn):
    S = proj.shape[0]
    nblk = S // HRB
    cps = HRB // CH

    def body(q_ref, f_ref, i_ref, z_ref, or_ref, dob_ref, st_ref, lb_ref, gn_ref, dh_ref, acc_ref, dst):
        @pl.when(pl.program_id(0) == 0)
        def _():
            dst[...] = jnp.zeros_like(dst)
            acc_ref[...] = jnp.zeros_like(acc_ref)

        low = _tri_masks()
        tri = low.astype(BF16)
        triu = jnp.logical_not(_tri_masks()) | (lax.broadcasted_iota(jnp.int32, (CH, CH), 0)
                                               == lax.broadcasted_iota(jnp.int32, (CH, CH), 1))
        triu = triu.astype(BF16)
        lbv = lb_ref[...]
        hs = _heads()
        for ci in reversed(range(cps)):
            rows = slice(ci * CH, (ci + 1) * CH)
            dobs, dgns = [], []
            for sl in hs:
                o = or_ref[rows, sl]
                z = z_ref[rows, sl]
                sz = _sig(z)
                gnv = gn_ref[:, sl]
                dobv = dob_ref[rows, sl]
                r = lax.rsqrt(jnp.mean(o * o, axis=-1, keepdims=True) + EPS)
                onr = o * r
                don = dobv * (z * sz)
                dh_ref[rows, 3 * D + sl.start:3 * D + sl.stop] = (
                    dobv * (onr * gnv) * (sz * (1.0 + z * (1.0 - sz)))).astype(BF16)
                dgns.append(jnp.sum(don * onr, axis=0, keepdims=True))
                gh_ = don * gnv
                dobs.append((r * (gh_ - onr * jnp.mean(gh_ * onr, axis=-1, keepdims=True))).astype(BF16))
            acc_ref[1:2, :] += jnp.concatenate(dgns, axis=1)

            gh = _hgrn_gate_heads(q_ref, f_ref, rows, lbv)
            b = _exact_mm(tri, jnp.concatenate([g["lg"] for g in gh], axis=1))
            dh = _hgrn_decay_heads(gh, b)
            vbs = [i_ref[rows, sl].astype(BF16) for sl in hs]

            st0s = [st_ref[ci, sl, :] for sl in hs]
            dst1s = [dst[sl, :] for sl in hs]
            dst1bs = [t.astype(BF16) for t in dst1s]
            as_ = [_nt(d["qsb"], d["ksb"]) for d in dh]
            das_ = [_nt(do, vb) for do, vb in zip(dobs, vbs)]
            dqes = [_nn(do, s0.astype(BF16)) for do, s0 in zip(dobs, st0s)]
            dkds = [_nn(vb, d1) for vb, d1 in zip(vbs, dst1bs)]
            dvis = [_nt(d["kdb"], d1) for d, d1 in zip(dh, dst1bs)]
            dsts = [_tn(do, d["qeb"]) for do, d in zip(dobs, dh)]
            abs_ = [jnp.where(low, a, 0.0).astype(BF16) for a in as_]
            dabs_ = [jnp.where(low, a, 0.0).astype(BF16) for a in das_]
            dqss = [_nn(da, d["ksb"]) for da, d in zip(dabs_, dh)]
            dkss = [_tn(da, d["qsb"]) for da, d in zip(dabs_, dh)]
            dvs_ = [_tn(a, do) + dvi for a, do, dvi in zip(abs_, dobs, dvis)]

            dqs_, dks_, dbs_, exs_ = [], [], [], []
            for hh, sl in enumerate(hs):
                d, d1, s0 = dh[hh], dst1s[hh], st0s[hh]
                dqe, dqs, dks, dkd = dqes[hh], dqss[hh], dkss[hh], dkds[hh]
                dst[sl, :] = dsts[hh] + d1 * d["ebl"]
                dh_ref[rows, 2 * D + sl.start:2 * D + sl.stop] = dvs_[hh].astype(BF16)
                dqs_.append((dqe * d["ebm"] + dqs) * d["e1"])
                dks_.append((dks + dkd * d["eblm"]) * d["e2"])
                dkdkd = dkd * d["kd"]
                dbs_.append(dqe * d["qe"] + dqs * d["qsb"].astype(F32) - dks * d["ksb"].astype(F32) - dkdkd)
                exs_.append(jnp.sum(dkdkd, axis=0, keepdims=True)
                            + jnp.sum(d1 * s0, axis=0, keepdims=True) * d["ebl"])
            dg = _exact_mm(triu, jnp.concatenate(dbs_, axis=1)) + jnp.concatenate(exs_, axis=1)

            dlbs = []
            for hh, sl in enumerate(hs):
                g = gh[hh]
                df = dg[:, sl] / g["f"] - dks_[hh]
                sf = g["sf"]
                omsf = 1.0 - sf
                sq = g["sq"]
                dlbs.append(jnp.sum(df * omsf, axis=0, keepdims=True))
                dh_ref[rows, sl] = (dqs_[hh] * (sq * (1.0 + g["qraw"] * (1.0 - sq)))).astype(BF16)
                dh_ref[rows, D + sl.start:D + sl.stop] = (df * (1.0 - lbv[:, sl]) * sf * omsf).astype(BF16)
            acc_ref[0:1, :] += jnp.concatenate(dlbs, axis=1)

    def pcol(c):
        return pl.BlockSpec((HRB, D), lambda i: (nblk - 1 - i, c))

    vec = pl.BlockSpec((1, D), lambda i: (0, 0))
    row = pl.BlockSpec((HRB, D), lambda i: (nblk - 1 - i, 0))
    return pl.pallas_call(
        body, name="hgrn_bwd", grid=(nblk,),
        in_specs=[pcol(2), pcol(3), pcol(4), pcol(5), row, row,
                  pl.BlockSpec((cps, NH * HK, HK), lambda i: (nblk - 1 - i, 0, 0)), vec, vec],
        out_specs=[pl.BlockSpec((HRB, 4 * D), lambda i: (nblk - 1 - i, 0)),
                   pl.BlockSpec((8, D), lambda i: (0, 0))],
        out_shape=[jax.ShapeDtypeStruct((S, 4 * D), BF16), jax.ShapeDtypeStruct((8, D), F32)],
        scratch_shapes=[pltpu.VMEM((NH * HK, HK), F32)],
        compiler_params=_cp(),
    )(proj, proj, proj, proj, oraw, dob, states, lb, gn)


def _merge_call(oa, ob, proj, x, tgt, vecs, attn, wa, wb, wo, hsum):
    S = x.shape[0]
    tm = 256

    def body(oa_ref, ob_ref, ga_ref, gb_ref, x_ref, t_ref, v_ref, at_ref, za_ref, wa_ref, wb_ref, wo_ref, hs_ref,
             y_ref, dx2_ref, du_ref, dya_ref, dyb_ref, dg_ref, dat_ref, dsum_ref, dza_ref, dob_ref, acc_ref, ls_ref):
        @pl.when(pl.program_id(0) == 0)
        def _():
            acc_ref[...] = jnp.zeros_like(acc_ref)
            ls_ref[...] = jnp.zeros_like(ls_ref)

        gate = v_ref[0:1, :]
        fg = v_ref[1:2, :]
        ya = _nn(oa_ref[...], wa_ref[...])
        yb = _nn(ob_ref[...], wb_ref[...])
        sa = _sig(ga_ref[...])
        sb = _sig(gb_ref[...])
        y = (sa * ya + sb * yb).astype(BF16)
        y_ref[...] = y
        u = _nn(y, wo_ref[...])
        x2v = x_ref[...] + gate * u
        r = lax.rsqrt(jnp.mean(x2v * x2v, axis=-1, keepdims=True) + EPS)
        err = x2v * r * fg - t_ref[...]
        ls_ref[...] += jnp.sum(err * err)
        dout = err * (1.0 / D)
        gh = dout * fg
        dx2 = r * gh - x2v * (r * r * r * jnp.mean(gh * x2v, axis=-1, keepdims=True))
        acc_ref[0:1, :] += jnp.sum(dx2 * u, axis=0, keepdims=True)
        acc_ref[1:2, :] += jnp.sum(dout * x2v * r, axis=0, keepdims=True)
        dx2_ref[...] = dx2
        du = (dx2 * gate).astype(BF16)
        du_ref[...] = du
        dy = _nt(du, wo_ref[...])
        dya = (dy * sa).astype(BF16)
        dyb = (dy * sb).astype(BF16)
        dya_ref[...] = dya
        dyb_ref[...] = dyb
        dg_ref[:, 0:D] = (dy * ya * sa * (1.0 - sa)).astype(BF16)
        dg_ref[:, D:2 * D] = (dy * yb * sb * (1.0 - sb)).astype(BF16)
        doa = _nt(dya, wa_ref[...])
        dob_ref[...] = _nt(dyb, wb_ref[...])
        za = za_ref[...]
        sz = _sig(za)
        att = at_ref[...]
        dat = doa * (za * sz)
        dat_ref[...] = dat
        dza_ref[...] = (doa * att * (sz * (1.0 + za * (1.0 - sz)))).astype(BF16)
        dsum_ref[...] = _exact_mm_r(dat * att, hs_ref[...])

    row = pl.BlockSpec((tm, D), lambda i: (i, 0))
    arow = pl.BlockSpec((tm, AW), lambda i: (i, 0))
    full = lambda a: pl.BlockSpec(a.shape, lambda i: (0, 0))
    return pl.pallas_call(
        body, name="merge_fwd_bwd", grid=(S // tm,),
        in_specs=[arow, row, pl.BlockSpec((tm, D), lambda i: (i, 6)), pl.BlockSpec((tm, D), lambda i: (i, 7)),
                  row, row, pl.BlockSpec((8, D), lambda i: (0, 0)), arow, pl.BlockSpec((tm, AW), lambda i: (i, 3)),
                  full(wa), full(wb), full(wo), full(hsum)],
        out_specs=[row, row, row, row, row, pl.BlockSpec((tm, 2 * D), lambda i: (i, 0)),
                   arow, arow, arow, row, pl.BlockSpec((8, D), lambda i: (0, 0)),
                   pl.BlockSpec((8, 128), lambda i: (0, 0))],
        out_shape=[jax.ShapeDtypeStruct((S, D), BF16), jax.ShapeDtypeStruct((S, D), F32),
                   jax.ShapeDtypeStruct((S, D), BF16), jax.ShapeDtypeStruct((S, D), BF16),
                   jax.ShapeDtypeStruct((S, D), BF16), jax.ShapeDtypeStruct((S, 2 * D), BF16),
                   jax.ShapeDtypeStruct((S, AW), F32), jax.ShapeDtypeStruct((S, AW), F32),
                   jax.ShapeDtypeStruct((S, AW), BF16), jax.ShapeDtypeStruct((S, D), F32),
                   jax.ShapeDtypeStruct((8, D), F32), jax.ShapeDtypeStruct((8, 128), F32)],
        compiler_params=_cp(),
    )(oa, ob, proj, proj, x, tgt, vecs, attn, proj, wa, wb, wo, hsum)


def _unused_fwd2_call(oa, ob, proj, x, tgt, vecs, wa, wb, wo):
    S = x.shape[0]
    tm = 256

    def body(oa_ref, ob_ref, ga_ref, gb_ref, x_ref, t_ref, v_ref, wa_ref, wb_ref, wo_ref,
             ya_ref, yb_ref, y_ref, u_ref, x2_ref, ls_ref):
        @pl.when(pl.program_id(0) == 0)
        def _():
            ls_ref[...] = jnp.zeros_like(ls_ref)

        ya = _nn(oa_ref[...], wa_ref[...])
        yb = _nn(ob_ref[...], wb_ref[...])
        y = _sig(ga_ref[...]) * ya + _sig(gb_ref[...]) * yb
        u = _nn(y.astype(BF16), wo_ref[...])
        x2 = x_ref[...] + v_ref[0:1, :] * u
        r = lax.rsqrt(jnp.mean(x2 * x2, axis=-1, keepdims=True) + EPS)
        err = x2 * r * v_ref[1:2, :] - t_ref[...]
        ls_ref[...] += jnp.sum(err * err)
        ya_ref[...] = ya.astype(BF16)
        yb_ref[...] = yb.astype(BF16)
        y_ref[...] = y.astype(BF16)
        u_ref[...] = u.astype(BF16)
        x2_ref[...] = x2

    row = pl.BlockSpec((tm, D), lambda i: (i, 0))
    full = lambda a: pl.BlockSpec(a.shape, lambda i: (0, 0))
    return pl.pallas_call(
        body, name="fwd_merge_out", grid=(S // tm,),
        in_specs=[pl.BlockSpec((tm, AW), lambda i: (i, 0)), row,
                  pl.BlockSpec((tm, D), lambda i: (i, 6)), pl.BlockSpec((tm, D), lambda i: (i, 7)),
                  row, row, pl.BlockSpec((8, D), lambda i: (0, 0)), full(wa), full(wb), full(wo)],
        out_specs=[row, row, row, row, row, pl.BlockSpec((8, 128), lambda i: (0, 0))],
        out_shape=[jax.ShapeDtypeStruct((S, D), BF16)] * 4
                  + [jax.ShapeDtypeStruct((S, D), F32), jax.ShapeDtypeStruct((8, 128), F32)],
        compiler_params=_cp(),
    )(oa, ob, proj, proj, x, tgt, vecs, wa, wb, wo)


def _bwd2_call(x2, tgt, vecs, ya, yb, u, proj, attn, wa, wb, wo, hsum):
    S = x2.shape[0]
    tm = 256

    def body(x2_ref, t_ref, v_ref, ya_ref, yb_ref, u_ref, ga_ref, gb_ref, at_ref, za_ref,
             wa_ref, wb_ref, wo_ref, hs_ref,
             dx2_ref, du_ref, dya_ref, dyb_ref, dg_ref, dat_ref, dsum_ref, dza_ref, dob_ref, acc_ref):
        @pl.when(pl.program_id(0) == 0)
        def _():
            acc_ref[...] = jnp.zeros_like(acc_ref)

        x2v = x2_ref[...]
        gate = v_ref[0:1, :]
        fg = v_ref[1:2, :]
        r = lax.rsqrt(jnp.mean(x2v * x2v, axis=-1, keepdims=True) + EPS)
        dout = (x2v * r * fg - t_ref[...]) * (1.0 / D)
        gh = dout * fg
        dx2 = r * gh - x2v * (r * r * r * jnp.mean(gh * x2v, axis=-1, keepdims=True))
        acc_ref[0:1, :] += jnp.sum(dx2 * u_ref[...].astype(F32), axis=0, keepdims=True)
        acc_ref[1:2, :] += jnp.sum(dout * x2v * r, axis=0, keepdims=True)
        dx2_ref[...] = dx2
        du = (dx2 * gate).astype(BF16)
        du_ref[...] = du
        dy = _nt(du, wo_ref[...])
        sa = _sig(ga_ref[...])
        sb = _sig(gb_ref[...])
        dya = (dy * sa).astype(BF16)
        dyb = (dy * sb).astype(BF16)
        dya_ref[...] = dya
        dyb_ref[...] = dyb
        dg_ref[:, 0:D] = (dy * ya_ref[...].astype(F32) * sa * (1.0 - sa)).astype(BF16)
        dg_ref[:, D:2 * D] = (dy * yb_ref[...].astype(F32) * sb * (1.0 - sb)).astype(BF16)
        doa = _nt(dya, wa_ref[...])
        dob_ref[...] = _nt(dyb, wb_ref[...])
        za = za_ref[...]
        sz = _sig(za)
        att = at_ref[...]
        dat = doa * (za * sz)
        dat_ref[...] = dat
        dza_ref[...] = (doa * att * (sz * (1.0 + za * (1.0 - sz)))).astype(BF16)
        dsum_ref[...] = _exact_mm_r(dat * att, hs_ref[...])

    row = pl.BlockSpec((tm, D), lambda i: (i, 0))
    arow = pl.BlockSpec((tm, AW), lambda i: (i, 0))
    full = lambda a: pl.BlockSpec(a.shape, lambda i: (0, 0))
    return pl.pallas_call(
        body, name="bwd_merge_out", grid=(S // tm,),
        in_specs=[row, row, pl.BlockSpec((8, D), lambda i: (0, 0)), row, row, row,
                  pl.BlockSpec((tm, D), lambda i: (i, 6)), pl.BlockSpec((tm, D), lambda i: (i, 7)),
                  arow, pl.BlockSpec((tm, AW), lambda i: (i, 3)), full(wa), full(wb), full(wo), full(hsum)],
        out_specs=[row, row, row, row, pl.BlockSpec((tm, 2 * D), lambda i: (i, 0)),
                   arow, arow, arow, row, pl.BlockSpec((8, D), lambda i: (0, 0))],
        out_shape=[jax.ShapeDtypeStruct((S, D), F32), jax.ShapeDtypeStruct((S, D), BF16),
                   jax.ShapeDtypeStruct((S, D), BF16), jax.ShapeDtypeStruct((S, D), BF16),
                   jax.ShapeDtypeStruct((S, 2 * D), BF16), jax.ShapeDtypeStruct((S, AW), F32),
                   jax.ShapeDtypeStruct((S, AW), F32), jax.ShapeDtypeStruct((S, AW), BF16),
                   jax.ShapeDtypeStruct((S, D), F32), jax.ShapeDtypeStruct((8, D), F32)],
        compiler_params=_cp(),
    )(x2, tgt, vecs, ya, yb, u, proj, proj, attn, proj, wa, wb, wo, hsum)


def _atb_call(a, b, name):
    S, K = a.shape
    N = b.shape[1]
    tm = min(1024, S)

    def body(a_ref, b_ref, o_ref, ob_ref):
        @pl.when(pl.program_id(0) == 0)
        def _():
            o_ref[...] = jnp.zeros_like(o_ref)

        o_ref[...] += _tn(a_ref[...], b_ref[...])

        @pl.when(pl.program_id(0) == S // tm - 1)
        def _():
            ob_ref[...] = o_ref[...].astype(BF16)

    ospec = pl.BlockSpec((K, N), lambda i: (0, 0))
    return pl.pallas_call(
        body, name=name, grid=(S // tm,),
        in_specs=[pl.BlockSpec((tm, K), lambda i: (i, 0)), pl.BlockSpec((tm, N), lambda i: (i, 0))],
        out_specs=[ospec, ospec],
        out_shape=[jax.ShapeDtypeStruct((K, N), F32), jax.ShapeDtypeStruct((K, N), BF16)], compiler_params=_cp(),
    )(a, b)


def _dwin_call(h_t, dqkvz, d_hgrn, d_gates):
    S = h_t.shape[1]
    tm = min(1024, S)
    tn = 1024

    def body(h_ref, q_ref, k_ref, v_ref, z_ref, m_ref, g_ref, o_ref, ob_ref):
        j = pl.program_id(0)

        @pl.when(pl.program_id(1) == 0)
        def _():
            o_ref[...] = jnp.zeros_like(o_ref)

        hv = h_ref[...]

        @pl.when(j == 0)
        def _():
            o_ref[0, :, 0:AW] += _nn(hv, q_ref[...])
            o_ref[0, :, AW:2 * AW] += _nn(hv, k_ref[...])

        @pl.when(j == 1)
        def _():
            o_ref[0, :, 0:AW] += _nn(hv, v_ref[...])
            o_ref[0, :, AW:2 * AW] += _nn(hv, z_ref[...])

        @pl.when(jnp.logical_and(j >= 2, j < 6))
        def _():
            o_ref[0] += _nn(hv, m_ref[...])

        @pl.when(j >= 6)
        def _():
            o_ref[0] += _nn(hv, g_ref[...])

        @pl.when(pl.program_id(1) == S // tm - 1)
        def _():
            ob_ref[...] = o_ref[...].astype(BF16)

    def aspec(jb):
        return pl.BlockSpec((tm, AW), lambda j, i: (jnp.where(j == jb, i, 0), 0))

    ospec = pl.BlockSpec((1, D, tn), lambda j, i: (j // 2, 0, j % 2))
    return pl.pallas_call(
        body, name="dw_in", grid=(8, S // tm),
        in_specs=[pl.BlockSpec((D, tm), lambda j, i: (0, i)), aspec(0), aspec(0), aspec(1), aspec(1),
                  pl.BlockSpec((tm, tn), lambda j, i: (jnp.where(jnp.logical_and(j >= 2, j < 6), i, 0),
                                                       jnp.clip(j - 2, 0, 3))),
                  pl.BlockSpec((tm, tn), lambda j, i: (jnp.where(j >= 6, i, 0), jnp.clip(j - 6, 0, 1)))],
        out_specs=[ospec, ospec],
        out_shape=[jax.ShapeDtypeStruct((4, D, 2 * tn), F32), jax.ShapeDtypeStruct((4, D, 2 * tn), BF16)],
        compiler_params=_cp(),
    )(h_t, *dqkvz, d_hgrn, d_gates)


def _dh_call(dqkvz, d_hgrn, d_gates, w_in, x, dx2, vecs):
    S = x.shape[0]
    tm = 256

    def body(q_ref, k_ref, v_ref, z_ref, m_ref, g_ref, w_ref, x_ref, dx2_ref, p_ref, gx_ref, acc_ref):
        @pl.when(pl.program_id(0) == 0)
        def _():
            acc_ref[...] = jnp.zeros_like(acc_ref)

        dhv = _nt(q_ref[...], w_ref[:, 0:AW])
        for cidx, r in enumerate((k_ref, v_ref, z_ref)):
            dhv += _nt(r[...], w_ref[:, (cidx + 1) * AW:(cidx + 2) * AW])
        dhv += _nt(m_ref[...], w_ref[:, 4 * AW:4 * AW + 4 * D])
        dhv += _nt(g_ref[...], w_ref[:, 4 * AW + 4 * D:NPROJ])
        xv = x_ref[...]
        r = lax.rsqrt(jnp.mean(xv * xv, axis=-1, keepdims=True) + EPS)
        xn = xv * r
        acc_ref[0:1, :] += jnp.sum(dhv, axis=0, keepdims=True)
        acc_ref[1:2, :] += jnp.sum(dhv * xn * p_ref[1:2, :], axis=0, keepdims=True)
        acc_ref[2:3, :] += jnp.sum(dhv * xn * p_ref[2:3, :], axis=0, keepdims=True)
        dxn = dhv * p_ref[0:1, :]
        gx_ref[...] = dx2_ref[...] + r * dxn - xv * (r * r * r * jnp.mean(dxn * xv, axis=-1, keepdims=True))

    row = pl.BlockSpec((tm, D), lambda i: (i, 0))
    aspec = pl.BlockSpec((tm, AW), lambda i: (i, 0))
    const = lambda shape: pl.BlockSpec(shape, lambda i: (0, 0))
    return pl.pallas_call(
        body, name="dh_gradx", grid=(S // tm,),
        in_specs=[aspec, aspec, aspec, aspec,
                  pl.BlockSpec((tm, 4 * D), lambda i: (i, 0)), pl.BlockSpec((tm, 2 * D), lambda i: (i, 0)),
                  pl.BlockSpec((D, NPROJ), lambda i: (0, 0), pipeline_mode=pl.Buffered(1)),
                  row, row, const((8, D))],
        out_specs=[row, const((8, D))],
        out_shape=[jax.ShapeDtypeStruct((S, D), F32), jax.ShapeDtypeStruct((8, D), F32)],
        compiler_params=_cp(),
    )(*dqkvz, d_hgrn, d_gates, w_in, x, dx2, vecs)


def _adamw_math(w, g, m, v):
    m = B1 * m + (1.0 - B1) * g
    v = B2 * v + (1.0 - B2) * (g * g)
    m_hat = m / (1.0 - B1 ** STEP)
    v_hat = v / (1.0 - B2 ** STEP)
    delta = -LR * (m_hat / (jnp.sqrt(v_hat) + AEPS) + WD * w)
    return delta, m, v


def _adamw_call(w, g, m, v, name):
    R, C = w.shape
    tr = R if R * C * 4 <= (1 << 20) else max(8, (1 << 20) // (C * 4))
    assert R % tr == 0

    def body(w_ref, g_ref, m_ref, v_ref, go_ref, d_ref, nm_ref, nv_ref):
        g = g_ref[...]
        go_ref[...] = g
        d_ref[...], nm_ref[...], nv_ref[...] = _adamw_math(w_ref[...], g, m_ref[...], v_ref[...])

    blk = pl.BlockSpec((tr, C), lambda i: (i, 0))
    return pl.pallas_call(
        body, name=name, grid=(R // tr,), in_specs=[blk] * 4, out_specs=[blk] * 4,
        out_shape=[jax.ShapeDtypeStruct((R, C), F32)] * 4, compiler_params=_cp(),
    )(w, g, m, v)


def _mod_call(c_all, w_ada_s, b_s):
    def body(c_ref, w_ref, b_ref, o_ref):
        cv = c_ref[...]
        sc = cv * _sig(cv)
        o_ref[...] = jnp.dot(sc, w_ref[...], preferred_element_type=F32,
                             precision=lax.Precision.HIGHEST) + b_ref[...]

    return pl.pallas_call(
        body, name="ada_mod", out_shape=jax.ShapeDtypeStruct((8, w_ada_s.shape[1]), F32),
        compiler_params=_cp(),
    )(c_all, w_ada_s, b_s)


def _ada_update_call(sct, dm, w, m, v):
    R, C = w.shape
    tr = 256

    def body(s_ref, d_ref, w_ref, m_ref, v_ref, g_ref, dl_ref, nm_ref, nv_ref):
        g = s_ref[:, 0:1] * d_ref[0:1, :]
        for b in range(1, 8):
            g = g + s_ref[:, b:b + 1] * d_ref[b:b + 1, :]
        g_ref[...] = g
        dl_ref[...], nm_ref[...], nv_ref[...] = _adamw_math(w_ref[...], g, m_ref[...], v_ref[...])

    blk = pl.BlockSpec((tr, C), lambda i: (i, 0))
    return pl.pallas_call(
        body, name="ada_update", grid=(R // tr,),
        in_specs=[pl.BlockSpec((tr, 8), lambda i: (i, 0)), pl.BlockSpec((8, C), lambda i: (0, 0)), blk, blk, blk],
        out_specs=[blk] * 4, out_shape=[jax.ShapeDtypeStruct((R, C), F32)] * 4, compiler_params=_cp(),
    )(sct, dm, w, m, v)


def _sum8_call(packs):
    def body(p_ref, o_ref):
        acc = p_ref[0]
        for k in range(1, 8):
            acc = acc + p_ref[k]
        o_ref[...] = acc

    return pl.pallas_call(
        body, name="sum_small", out_shape=jax.ShapeDtypeStruct(packs.shape[1:], F32), compiler_params=_cp(),
    )(packs)


def _local_step(x, tgt, shift, scale, gate, norm_g, hgrn_onorm_g, rel_bias, lb, final_g, weights_fn, hook=None):
    a = norm_g * (1.0 + scale)
    z6 = jnp.zeros((6, D), F32)
    h, h_t, proj, w_in, wa, wb, wo = weights_fn(x, jnp.concatenate([a, shift, z6], 0))

    biases = _bias_tiles(rel_bias) * LOG2E
    attn, lse, oa = _attn_fwd_call(proj, biases)

    gn = jnp.tile(hgrn_onorm_g, (1, NH))
    oraw, ob, states = _hgrn_fwd_call(proj, lb, gn)

    vecs2 = jnp.concatenate([gate, final_g, z6], 0)
    hsum = jnp.asarray(np.kron(np.eye(NH), np.ones((HE, HE))), BF16)
    y, dx2, du, dya, dyb, d_gates, dattn, dsum, dza, dob, acc2, lsq = _merge_call(
        oa, ob, proj, x, tgt, vecs2, attn, wa, wb, wo, hsum)
    loss = 0.5 * lsq[0, 0] / D
    d_wo, d_wo16 = _atb_call(y, du, "dw_out")
    d_wa, d_wa16 = _atb_call(oa, dya, "dw_branch_a")
    d_wb, d_wb16 = _atb_call(ob, dyb, "dw_branch_b")

    d_hgrn, acch = _hgrn_bwd_call(proj, oraw, dob, states, lb, gn)

    dq, dk, dv, dbs = _attn_bwd_call(proj, dattn, lse, dsum, biases)
    dqkvz = (dq, dk, dv, dza)

    d_win, d_win16 = _dwin_call(h_t, dqkvz, d_hgrn, d_gates)
    tok = hook((d_win, d_wa, d_wb, d_wo), (d_win16, d_wa16, d_wb16, d_wo16)) if hook is not None else 0.0
    one_scale = 1.0 + scale
    grad_x, acc1 = _dh_call(dqkvz, d_hgrn, d_gates, w_in, x, dx2,
                            jnp.concatenate([a + tok, norm_g, one_scale, jnp.zeros((5, D), F32)], 0))

    d_rel = jnp.zeros((NBUCKETS, NH), F32)
    for p, d in enumerate(PATTERNS):
        band, bucket = _band_bucket(d)
        onehot = (bucket[None] == np.arange(NBUCKETS)[:, None, None]) & band[None]
        d_rel = d_rel + jnp.einsum("hqk,bqk->bh", dbs[p], jnp.asarray(onehot, F32),
                                   precision=lax.Precision.HIGHEST)
    d_onorm = jnp.sum(acch[1].reshape(NH, HK), axis=0)

    zrow = jnp.zeros((D,), F32)
    pack = jnp.stack([acc1[0], acc1[1], acc2[0], acc1[2], acc2[1], acch[0],
                      zrow.at[:HK].set(d_onorm), zrow.at[0].set(loss),
                      zrow.at[:NBUCKETS * NH].set(d_rel.reshape(-1))] + [zrow] * 7, 0)
    return grad_x, d_win, d_wa, d_wb, d_wo, pack


def _me():
    return lax.axis_index("x"), lax.axis_index("y"), lax.axis_index("c")


def _peers(x, y):
    return [(1 - x, y), (x, 1 - y), (1 - x, 1 - y)]


def _allgather_small(blk, name):
    m_per, n = blk.shape

    def body(x_ref, out_ref, send_sems, recv_sems, local_sem):
        x, y, c = _me()
        me, sibling = (x, y, c), (x, y, 1 - c)
        chips = _peers(x, y)

        def rows(px, py, pc):
            return out_ref.at[pl.ds((4 * px + 2 * py + pc) * m_per, m_per), :]

        def copy(k, block, to, src=None):
            return pltpu.make_async_remote_copy(
                src_ref=rows(*block) if src is None else src, dst_ref=rows(*block),
                send_sem=send_sems.at[k], recv_sem=recv_sems.at[k], device_id=to, device_id_type=MESH)

        mine = pltpu.make_async_copy(x_ref, rows(*me), local_sem)
        mine.start()
        first = [copy(0, me, sibling, src=x_ref)]
        first += [copy(1 + j, me, (*chip, c), src=x_ref) for j, chip in enumerate(chips)]
        for cp in first:
            cp.start()
        passed = [copy(4 + j, (*chip, c), sibling) for j, chip in enumerate(chips)]
        for j, chip in enumerate(chips):
            copy(1 + j, (*chip, c), me).wait_recv()
            passed[j].start()
        copy(0, sibling, me).wait_recv()
        for j, chip in enumerate(chips):
            copy(4 + j, (*chip, 1 - c), me).wait_recv()
        for cp in first + passed:
            cp.wait_send()
        mine.wait()

    return pl.pallas_call(
        body, name=name, out_shape=jax.ShapeDtypeStruct((8 * m_per, n), blk.dtype),
        in_specs=[pl.BlockSpec(memory_space=pltpu.VMEM)], out_specs=pl.BlockSpec(memory_space=pltpu.VMEM),
        scratch_shapes=[pltpu.SemaphoreType.DMA((7,)), pltpu.SemaphoreType.DMA((7,)), pltpu.SemaphoreType.DMA],
    )(blk)


ANY = pl.BlockSpec(memory_space=pl.ANY)


HBM = pl.BlockSpec(memory_space=pltpu.HBM)
SEM = pl.BlockSpec(memory_space=pltpu.SEMAPHORE)
EFFECT = pltpu.SideEffectType.DATAFLOW_SIDE_EFFECTING


def _w_part(t, ref, j, half):
    if t == 0:
        return ref.at[pl.ds(half * (D // 2), D // 2), pl.ds(j * 2048, 2048)]
    if t == 1:
        return ref.at[pl.ds(half * (AW // 2), AW // 2), pl.ds(j * 256, 256)]
    return ref.at[pl.ds(j * 256 + half * 128, 128), :]


def _w_ici_copies(fulls, send_sems, recv_sems):
    x, y, c = _me()
    outs, ins = [], []
    for t in range(4):
        for k, chip in enumerate(_peers(x, y)):
            mine = _w_part(t, fulls[t], 2 * x + y, c)
            theirs = _w_part(t, fulls[t], 2 * chip[0] + chip[1], c)
            kw = dict(send_sem=send_sems.at[3 * t + k], recv_sem=recv_sems.at[3 * t + k],
                      device_id=(*chip, c), device_id_type=MESH)
            outs.append(pltpu.make_async_remote_copy(src_ref=mine, dst_ref=mine, **kw))
            ins.append(pltpu.make_async_remote_copy(src_ref=theirs, dst_ref=theirs, **kw))
    return outs, ins


def _gather_ici_start(fulls):
    def body(f0, f1, f2, f3, send_sems, recv_sems, t0, t1, t2, t3, token):
        for cp in _w_ici_copies([f0, f1, f2, f3], send_sems, recv_sems)[0]:
            cp.start()
        token[...] = jnp.zeros_like(token)

    res = pl.pallas_call(
        body, name="gather_ici_start",
        out_shape=(pltpu.SemaphoreType.DMA((12,)), pltpu.SemaphoreType.DMA((12,)),
                   *[pltpu.HBM(a.shape, a.dtype) for a in fulls], jax.ShapeDtypeStruct((8, 128), F32)),
        in_specs=[HBM] * 4, out_specs=(SEM, SEM, *[HBM] * 4, pl.BlockSpec(memory_space=pltpu.VMEM)),
        input_output_aliases={i: 2 + i for i in range(4)},
        compiler_params=pltpu.CompilerParams(has_side_effects=EFFECT),
    )(*[pltpu.with_memory_space_constraint(a, pltpu.HBM) for a in fulls])
    return res[0], res[1], list(res[2:6]), res[6]


def _gather_ici_wait(send_sems, recv_sems, thru, after):
    def body(f0, f1, f2, f3, send_sems, recv_sems, after_ref, g0, g1, g2, g3):
        outs, ins = _w_ici_copies([f0, f1, f2, f3], send_sems, recv_sems)
        for cp in outs:
            cp.wait_send()
        for cp in ins:
            cp.wait_recv()

    return pl.pallas_call(
        body, name="gather_ici_wait", out_shape=tuple(pltpu.HBM(a.shape, a.dtype) for a in thru),
        in_specs=[HBM] * 4 + [SEM, SEM, ANY], out_specs=[HBM] * 4,
        input_output_aliases={i: i for i in range(4)},
        compiler_params=pltpu.CompilerParams(has_side_effects=EFFECT),
    )(*thru, send_sems, recv_sems, after)


def _gather_sibling(fulls):
    def body(i0, i1, i2, i3, o0, o1, o2, o3, send_sems, recv_sems):
        x, y, c = _me()
        outs = [o0, o1, o2, o3]
        cps = []
        for t in range(4):
            for k, chip in enumerate(_peers(x, y)):
                blk = _w_part(t, outs[t], 2 * chip[0] + chip[1], c)
                cp = pltpu.make_async_remote_copy(
                    src_ref=blk, dst_ref=blk, send_sem=send_sems.at[t, k], recv_sem=recv_sems.at[t, k],
                    device_id=(x, y, 1 - c), device_id_type=MESH)
                cp.start()
                cps.append(cp)
        for t in range(4):
            for k, chip in enumerate(_peers(x, y)):
                blk = _w_part(t, outs[t], 2 * chip[0] + chip[1], 1 - c)
                pltpu.make_async_remote_copy(
                    src_ref=blk, dst_ref=blk, send_sem=send_sems.at[t, k], recv_sem=recv_sems.at[t, k],
                    device_id=(x, y, 1 - c), device_id_type=MESH).wait_recv()
        for cp in cps:
            cp.wait_send()

    return pl.pallas_call(
        body, name="gather_sibling",
        out_shape=[jax.ShapeDtypeStruct(s, BF16) for s in FULL_W_SHAPES],
        in_specs=[ANY] * 4, out_specs=[ANY] * 4, input_output_aliases={0: 0, 1: 1, 2: 2, 3: 3},
        scratch_shapes=[pltpu.SemaphoreType.DMA((4, 3)), pltpu.SemaphoreType.DMA((4, 3))],
    )(*fulls)


def _half_of(t, ref, half):
    if t == 0:
        return ref.at[:, pl.ds(half * 512, 512), :]
    if t == 1:
        return ref.at[pl.ds(half * 256, 256), :]
    return ref.at[:, pl.ds(half * 512, 512)]


HALF_SHAPES = [(4, 512, 2048), (256, D), (D, 512), (D, 512)]
PIECE_SHAPES = [(512, 2048), (256, 256), (256, 512), (256, 512)]
SHARD_SHAPES = [(D, 2048), (AW, 256), (256, D), (256, D)]


def _chip_piece(t, ref, j):
    if t == 0:
        return ref.at[j]
    if t == 1:
        return ref.at[:, pl.ds(j * 256, 256)]
    return ref.at[pl.ds(j * 256, 256), :]


def _reduce_sibling_send(gs):
    def body(g0, g1, g2, g3, r0, r1, r2, r3, send_sems, recv_sems):
        x, y, c = _me()
        ins, outs = [g0, g1, g2, g3], [r0, r1, r2, r3]
        cps = []
        for t in range(4):
            cp = pltpu.make_async_remote_copy(
                src_ref=_half_of(t, ins[t], 1 - c), dst_ref=outs[t],
                send_sem=send_sems.at[t], recv_sem=recv_sems.at[t], device_id=(x, y, 1 - c), device_id_type=MESH)
            cp.start()
            cps.append(cp)
        for cp in cps:
            cp.wait_recv()
        for cp in cps:
            cp.wait_send()

    return pl.pallas_call(
        body, name="reduce_sibling", out_shape=[jax.ShapeDtypeStruct(s, BF16) for s in HALF_SHAPES],
        in_specs=[ANY] * 4, out_specs=[ANY] * 4,
        scratch_shapes=[pltpu.SemaphoreType.DMA((4,)), pltpu.SemaphoreType.DMA((4,))],
    )(*gs)


def _chip_copies(hs, lands, send_sems, recv_sems):
    x, y, c = _me()
    cps = []
    for t in range(4):
        for k, chip in enumerate(_peers(x, y)):
            pj = 2 * chip[0] + chip[1]
            cps.append(pltpu.make_async_remote_copy(
                src_ref=_chip_piece(t, hs[t], pj), dst_ref=lands[t].at[k],
                send_sem=send_sems.at[3 * t + k], recv_sem=recv_sems.at[3 * t + k],
                device_id=(*chip, c), device_id_type=MESH))
    return cps


def _reduce_chips_start(hs):
    lands = [lax.empty((3,) + s, BF16) for s in PIECE_SHAPES]

    def body(h0, h1, h2, h3, l0, l1, l2, l3, send_sems, recv_sems, t0, t1, t2, t3, t4, t5, t6, t7, token):
        for cp in _chip_copies([h0, h1, h2, h3], [l0, l1, l2, l3], send_sems, recv_sems):
            cp.start()
        token[...] = jnp.zeros_like(token)

    bufs = list(hs) + lands
    res = pl.pallas_call(
        body, name="reduce_chips_start",
        out_shape=(pltpu.SemaphoreType.DMA((12,)), pltpu.SemaphoreType.DMA((12,)),
                   *[pltpu.HBM(a.shape, a.dtype) for a in bufs], jax.ShapeDtypeStruct((8, 128), F32)),
        in_specs=[HBM] * 8, out_specs=(SEM, SEM, *[HBM] * 8, pl.BlockSpec(memory_space=pltpu.VMEM)),
        input_output_aliases={i: 2 + i for i in range(8)},
        compiler_params=pltpu.CompilerParams(has_side_effects=EFFECT),
    )(*[pltpu.with_memory_space_constraint(a, pltpu.HBM) for a in bufs])
    return res[0], res[1], list(res[2:10]), res[10]


def _reduce_chips_wait(send_sems, recv_sems, thru, after):
    def body(h0, h1, h2, h3, l0, l1, l2, l3, send_sems, recv_sems, after_ref, d0, d1, d2, d3, g0, g1, g2, g3):
        cps = _chip_copies([h0, h1, h2, h3], [l0, l1, l2, l3], send_sems, recv_sems)
        for cp in cps:
            cp.wait_send()
        for cp in cps:
            cp.wait_recv()

    res = pl.pallas_call(
        body, name="reduce_chips_wait", out_shape=tuple(pltpu.HBM(a.shape, a.dtype) for a in thru),
        in_specs=[HBM] * 8 + [SEM, SEM, ANY], out_specs=[HBM] * 8,
        input_output_aliases={i: i for i in range(8)},
        compiler_params=pltpu.CompilerParams(has_side_effects=EFFECT),
    )(*thru, send_sems, recv_sems, after)
    return list(res[4:8])


def _share_sibling(shards):
    def body(i0, i1, i2, i3, o0, o1, o2, o3, send_sems, recv_sems):
        x, y, c = _me()
        outs = [o0, o1, o2, o3]

        def half(t, ref, hf):
            if t == 0:
                return ref.at[pl.ds(hf * 512, 512), :]
            if t == 1:
                return ref.at[pl.ds(hf * 256, 256), :]
            return ref.at[:, pl.ds(hf * 512, 512)]

        cps = []
        for t in range(4):
            mine = half(t, outs[t], c)
            cp = pltpu.make_async_remote_copy(
                src_ref=mine, dst_ref=mine, send_sem=send_sems.at[t], recv_sem=recv_sems.at[t],
                device_id=(x, y, 1 - c), device_id_type=MESH)
            cp.start()
            cps.append(cp)
        for t in range(4):
            theirs = half(t, outs[t], 1 - c)
            pltpu.make_async_remote_copy(
                src_ref=theirs, dst_ref=theirs, send_sem=send_sems.at[t],
                recv_sem=recv_sems.at[t], device_id=(x, y, 1 - c), device_id_type=MESH).wait_recv()
        for cp in cps:
            cp.wait_send()

    return pl.pallas_call(
        body, name="share_sibling", out_shape=[jax.ShapeDtypeStruct(s, F32) for s in SHARD_SHAPES],
        in_specs=[ANY] * 4, out_specs=[ANY] * 4, input_output_aliases={0: 0, 1: 1, 2: 2, 3: 3},
        scratch_shapes=[pltpu.SemaphoreType.DMA((4,)), pltpu.SemaphoreType.DMA((4,))],
    )(*shards)


def _half_blockspec(t, idx_pos):
    if t == 0:
        return pl.BlockSpec((1, 512, 2048), lambda i, s: (i, s[idx_pos], 0)), 4
    if t == 1:
        return pl.BlockSpec((256, D), lambda i, s: (s[idx_pos], 0)), 1
    return pl.BlockSpec((256, 512), lambda i, s: (i, s[idx_pos])), 4


def _half_out_blockspec(t):
    if t == 0:
        return pl.BlockSpec((1, 512, 2048), lambda i, s: (i, 0, 0))
    if t == 1:
        return pl.BlockSpec((256, D), lambda i, s: (0, 0))
    return pl.BlockSpec((256, 512), lambda i, s: (i, 0))


def _add_half_call(t, own, recv, sc, name):
    in_blk, steps = _half_blockspec(t, 0)
    out_blk = _half_out_blockspec(t)

    def body(s_ref, a_ref, b_ref, o_ref, ob_ref):
        v = a_ref[...] + b_ref[...].astype(F32)
        o_ref[...] = v
        ob_ref[...] = v.astype(BF16)

    return pl.pallas_call(
        body, name=name,
        grid_spec=pltpu.PrefetchScalarGridSpec(
            num_scalar_prefetch=1, grid=(steps,), in_specs=[in_blk, out_blk], out_specs=[out_blk, out_blk]),
        out_shape=[jax.ShapeDtypeStruct(HALF_SHAPES[t], F32), jax.ShapeDtypeStruct(HALF_SHAPES[t], BF16)],
        compiler_params=_cp(),
    )(sc, own, recv)


def _final_piece_call(t, chipsum, recv3, sc, name):
    ps = PIECE_SHAPES[t]
    if t == 0:
        own_blk = pl.BlockSpec((1,) + ps, lambda i, s: (s[1], 0, 0))
        o_blk = pl.BlockSpec(ps, lambda i, s: (s[0], 0))
    elif t == 1:
        own_blk = pl.BlockSpec(ps, lambda i, s: (0, s[1]))
        o_blk = pl.BlockSpec(ps, lambda i, s: (s[0], 0))
    else:
        own_blk = pl.BlockSpec(ps, lambda i, s: (s[1], 0))
        o_blk = pl.BlockSpec(ps, lambda i, s: (0, s[0]))
    r_blk = pl.BlockSpec((3,) + ps, lambda i, s: (0, 0, 0))

    def body(s_ref, a_ref, r_ref, o_ref):
        a = a_ref[0] if t == 0 else a_ref[...]
        o_ref[...] = ((a + r_ref[0].astype(F32)) + r_ref[1].astype(F32)) + r_ref[2].astype(F32)

    return pl.pallas_call(
        body, name=name,
        grid_spec=pltpu.PrefetchScalarGridSpec(
            num_scalar_prefetch=1, grid=(1,), in_specs=[own_blk, r_blk], out_specs=o_blk),
        out_shape=jax.ShapeDtypeStruct(SHARD_SHAPES[t], F32), compiler_params=_cp(),
    )(sc, chipsum, recv3)


FULL_W_SHAPES = [(D, NPROJ), (AW, D), (D, D), (D, D)]


def _cast_place_call(t, shard, sc, name):
    if t == 0:
        blk, steps = (512, 2048), 2
        in_blk = pl.BlockSpec(blk, lambda i, s: (i, 0))
        o_blk = pl.BlockSpec(blk, lambda i, s: (i, s[1]))
    elif t == 1:
        blk, steps = (AW, 256), 1
        in_blk = pl.BlockSpec(blk, lambda i, s: (0, 0))
        o_blk = pl.BlockSpec(blk, lambda i, s: (0, s[1]))
    else:
        blk, steps = (256, D), 1
        in_blk = pl.BlockSpec(blk, lambda i, s: (0, 0))
        o_blk = pl.BlockSpec(blk, lambda i, s: (s[1], 0))

    def body(s_ref, a_ref, o_ref, own_ref):
        v = a_ref[...].astype(BF16)
        o_ref[...] = v
        own_ref[...] = v

    return pl.pallas_call(
        body, name=name,
        grid_spec=pltpu.PrefetchScalarGridSpec(
            num_scalar_prefetch=1, grid=(steps,), in_specs=[in_blk], out_specs=[o_blk, in_blk]),
        out_shape=[jax.ShapeDtypeStruct(FULL_W_SHAPES[t], BF16), jax.ShapeDtypeStruct(shard.shape, BF16)],
        compiler_params=_cp(),
    )(sc, shard)


def _lower_bound_fn(hgrn_lb):
    return jnp.cumsum(jax.nn.softmax(hgrn_lb.astype(F32), axis=0), axis=0)[0]


def kernel(x, c, w_ada, b_ada, norm_g, w_in, hgrn_onorm_g, w_branch_a, w_branch_b, w_out, rel_bias, hgrn_lb, final_g, loss_target, m_w_ada, m_b_ada, m_norm_g, m_w_in, m_hgrn_onorm_g, m_w_branch_a, m_w_branch_b, m_w_out, m_rel_bias, m_hgrn_lb, m_final_g, v_w_ada, v_b_ada, v_norm_g, v_w_in, v_hgrn_onorm_g, v_w_branch_a, v_w_branch_b, v_w_out, v_rel_bias, v_hgrn_lb, v_final_g):
    ax, ay, ac = _me()
    chip = 2 * ax + ay
    dev = 4 * ax + 2 * ay + ac
    sc_idx = jnp.stack([ac, chip]).astype(jnp.int32)

    c_all = _allgather_small(jnp.pad(c, ((0, 7), (0, 0))), "gather_c").reshape(8, 8, D)[:, 0]
    b_s = lax.dynamic_slice(b_ada, (0, chip * 768), (1, 768))
    mod_part = _mod_call(c_all, w_ada[0], b_s)
    mod_all = _allgather_small(mod_part, "gather_mod").reshape(8, 8, 768)
    mod_mine = lax.dynamic_index_in_dim(mod_all, dev, axis=1, keepdims=False)
    mod = mod_mine[0::2].reshape(1, 3 * D)

    names = ["w_in", "w_a", "w_b", "w_o"]
    shards, mod = lax.optimization_barrier(([w_in[0], w_branch_a[0], w_branch_b[0], w_out[0]], mod))
    placed = [_cast_place_call(t, shards[t], sc_idx, "cast_" + names[t]) for t in range(4)]
    w_send_sems, w_recv_sems, w_thru, w_token = _gather_ici_start([p_[0] for p_ in placed])
    mod = mod + w_token[0, 0]
    shift, scale, gate = mod[:, :D], mod[:, D:2 * D], mod[:, 2 * D:]

    def weights_fn(xs, avec):
        proj, h, h_t = _proj_own_call(xs, avec, placed[0][1], sc_idx)
        arrived = _gather_ici_wait(w_send_sems, w_recv_sems, w_thru, proj)
        win_f, wa_f, wb_f, wo_f = _gather_sibling(arrived)
        return h, h_t, _proj_rest_call(h, win_f, proj, sc_idx), win_f, wa_f, wb_f, wo_f

    flight = {}

    def start_reduction(own, own16):
        sib = _reduce_sibling_send(own16)
        halves = [_add_half_call(t, own[t], sib[t], sc_idx, "chipsum_" + names[t]) for t in range(4)]
        send_sems, recv_sems, thru, token = _reduce_chips_start([hb for _, hb in halves])
        flight.update(sems=(send_sems, recv_sems), thru=thru, sums=[hf for hf, _ in halves])
        return token[0, 0]

    lb, lb_vjp = jax.vjp(_lower_bound_fn, hgrn_lb)
    grad_x, d_win, d_wa, d_wb, d_wo, pack = _local_step(
        x[0], loss_target[0], shift, scale, gate, norm_g, hgrn_onorm_g, rel_bias, lb[None, :],
        final_g[None, :], weights_fn, hook=start_reduction)

    rec = _reduce_chips_wait(*flight["sems"], flight["thru"], pack)
    pieces = [_final_piece_call(t, flight["sums"][t], rec[t], sc_idx, "piece_" + names[t]) for t in range(4)]
    g_win, g_wa, g_wb, g_wo = _share_sibling(pieces)

    packs = _allgather_small(pack, "gather_small").reshape(8, 16, D)
    tot = _sum8_call(packs)
    loss = tot[7, 0]
    g_b_ada = tot[0:3].reshape(1, 3 * D)
    g_norm_g = tot[3:4]
    g_final_g = tot[4]
    (g_hgrn_lb,) = lb_vjp(tot[5])
    g_onorm = tot[6:7, :HK]
    g_rel = tot[8, :NBUCKETS * NH].reshape(NBUCKETS, NH)

    def rows_of(a):
        flat = a.reshape(-1)
        n = -(-flat.shape[0] // D)
        return jnp.pad(flat, (0, n * D - flat.shape[0])).reshape(n, D)

    smalls = [(b_ada, g_b_ada, m_b_ada, v_b_ada), (norm_g, g_norm_g, m_norm_g, v_norm_g),
              (hgrn_onorm_g, g_onorm, m_hgrn_onorm_g, v_hgrn_onorm_g), (rel_bias, g_rel, m_rel_bias, v_rel_bias),
              (hgrn_lb, g_hgrn_lb, m_hgrn_lb, v_hgrn_lb), (final_g, g_final_g, m_final_g, v_final_g)]
    cat = [jnp.concatenate([rows_of(s[k]) for s in smalls], 0) for k in range(4)]
    cat = [jnp.pad(a, ((0, 16 - a.shape[0]), (0, 0))) for a in cat]
    _, sd, sm, sv = _adamw_call(*cat, "adamw_small")

    def unpack(packed):
        res, r = [], 0
        for s in smalls:
            n = -(-s[0].size // D)
            res.append(packed[r:r + n].reshape(-1)[:s[0].size].reshape(s[0].shape))
            r += n
        return res

    d_small, m_small, v_small = unpack(sd), unpack(sm), unpack(sv)

    sc_all = c_all * jax.nn.sigmoid(c_all)
    dmod_all = packs[:, 0:3].reshape(8, 3 * D)
    dm_s = lax.dynamic_slice(dmod_all, (0, chip * 768), (8, 768))
    g_w_ada, d_w_ada, nm_w_ada, nv_w_ada = _ada_update_call(sc_all.T, dm_s, w_ada[0], m_w_ada[0], v_w_ada[0])

    big = []
    for w, g, m, v, n in [(w_in, g_win, m_w_in, v_w_in, "w_in"), (w_branch_a, g_wa, m_w_branch_a, v_w_branch_a, "w_a"),
                          (w_branch_b, g_wb, m_w_branch_b, v_w_branch_b, "w_b"), (w_out, g_wo, m_w_out, v_w_out, "w_o")]:
        big.append(_adamw_call(w[0], g, m[0], v[0], "adamw_" + n))

    e = lambda a: a[None]
    grads = [e(g_w_ada), g_b_ada, g_norm_g, e(big[0][0]), g_onorm, e(big[1][0]), e(big[2][0]), e(big[3][0]),
             g_rel, g_hgrn_lb, g_final_g]
    deltas = [e(d_w_ada), d_small[0], d_small[1], e(big[0][1]), d_small[2], e(big[1][1]), e(big[2][1]), e(big[3][1]),
              d_small[3], d_small[4], d_small[5]]
    new_m = [e(nm_w_ada), m_small[0], m_small[1], e(big[0][2]), m_small[2], e(big[1][2]), e(big[2][2]), e(big[3][2]),
             m_small[3], m_small[4], m_small[5]]
    new_v = [e(nv_w_ada), v_small[0], v_small[1], e(big[0][3]), v_small[2], e(big[1][3]), e(big[2][3]), e(big[3][3]),
             v_small[3], v_small[4], v_small[5]]
    return (loss, grad_x[None], *grads, *deltas, *new_m, *new_v)
```

```python
import math

import numpy as np
import jax
import jax.numpy as jnp
from jax import lax
from jax.experimental import pallas as pl
from jax.experimental.pallas import tpu as pltpu

D = 1024
AW = 512
NH = 8
HE = 64
HK = 128
NPROJ = 8192
ABLK = 128
PATTERNS = (1, 4, 16)
NBUCKETS = 32
MAXDIST = 2048
NEG = -1e30
EPS = 1e-6
CH = 64
LR, B1, B2, AEPS, WD, STEP = 0.001, 0.9, 0.999, 1e-08, 0.01, 10

F32 = jnp.float32
BF16 = jnp.bfloat16
MESH = pl.DeviceIdType.MESH
VMEM_LIMIT = 56 * 1024 * 1024


def _cp(**kw):
    return pltpu.CompilerParams(vmem_limit_bytes=VMEM_LIMIT, **kw)


def _sig(x):
    return 0.5 * jnp.tanh(0.5 * x) + 0.5


def _nt(a, b):
    return lax.dot_general(a, b, (((1,), (1,)), ((), ())), preferred_element_type=F32)


def _tn(a, b):
    return lax.dot_general(a, b, (((0,), (0,)), ((), ())), preferred_element_type=F32)


def _nn(a, b):
    return jnp.dot(a, b, preferred_element_type=F32)


def _split2(x):
    h = x.astype(BF16)
    return h, (x - h.astype(F32)).astype(BF16)


def _exact_mm(tri_bf16, x):
    h, l = _split2(x)
    return _nn(tri_bf16, h) + _nn(tri_bf16, l)


def _exact_mm_r(x, ones_bf16):
    h, l = _split2(x)
    return _nn(h, ones_bf16) + _nn(l, ones_bf16)


def _proj_own_call(x, avec, w_own, sc):
    S = x.shape[0]
    tm, tn = 512, 2048

    def body(s_ref, x_ref, a_ref, w_ref, o_ref, h_ref, ht_ref):
        xv = x_ref[...]
        r = lax.rsqrt(jnp.mean(xv * xv, axis=-1, keepdims=True) + EPS)
        hv = xv * r * a_ref[0:1, :] + a_ref[1:2, :]
        hb = hv.astype(BF16)
        h_ref[...] = hb
        ht_ref[...] = hv.T.astype(BF16)
        o_ref[...] = _nn(hb, w_ref[...])

    return pl.pallas_call(
        body, name="in_proj_own",
        grid_spec=pltpu.PrefetchScalarGridSpec(
            num_scalar_prefetch=1, grid=(S // tm,),
            in_specs=[pl.BlockSpec((tm, D), lambda i, s: (i, 0)), pl.BlockSpec((8, D), lambda i, s: (0, 0)),
                      pl.BlockSpec((D, tn), lambda i, s: (0, 0))],
            out_specs=[pl.BlockSpec((tm, tn), lambda i, s: (i, s[1])), pl.BlockSpec((tm, D), lambda i, s: (i, 0)),
                       pl.BlockSpec((D, tm), lambda i, s: (0, i))]),
        out_shape=[jax.ShapeDtypeStruct((S, NPROJ), F32), jax.ShapeDtypeStruct((S, D), BF16),
                   jax.ShapeDtypeStruct((D, S), BF16)],
        compiler_params=_cp(),
    )(sc, x, avec, w_own)


def _proj_rest_call(h, w_in, proj, sc):
    S = h.shape[0]
    tm, tn = 512, 2048

    def body(s_ref, h_ref, w_ref, p_ref, o_ref):
        o_ref[...] = _nn(h_ref[...], w_ref[...])

    col = lambda j, s: (s[1] + 1 + j) % 4
    return pl.pallas_call(
        body, name="in_proj_rest",
        grid_spec=pltpu.PrefetchScalarGridSpec(
            num_scalar_prefetch=1, grid=(3, S // tm),
            in_specs=[pl.BlockSpec((tm, D), lambda j, i, s: (i, 0)),
                      pl.BlockSpec((D, tn), lambda j, i, s: (0, col(j, s))),
                      pl.BlockSpec(memory_space=pl.ANY)],
            out_specs=pl.BlockSpec((tm, tn), lambda j, i, s: (i, col(j, s)))),
        out_shape=jax.ShapeDtypeStruct((S, NPROJ), F32), input_output_aliases={3: 0}, compiler_params=_cp(),
    )(sc, h, w_in, proj)


def _t5_bucket_np(dist):
    max_exact = NBUCKETS // 2
    n = dist.astype(np.float32)
    large = max_exact + (np.log(np.maximum(n, np.float32(1.0)) / np.float32(max_exact))
                         / np.float32(math.log(MAXDIST / max_exact))
                         * np.float32(NBUCKETS - max_exact)).astype(np.int32)
    large = np.minimum(large, NBUCKETS - 1)
    return np.where(dist < max_exact, dist, large)


def _band_bucket(d):
    qi = np.arange(ABLK)[:, None]
    kj = np.arange(2 * ABLK)[None, :]
    delta = qi + ABLK - kj
    band = (delta >= 0) & (delta <= ABLK)
    bucket = _t5_bucket_np(np.clip(delta, 0, None) * d)
    return band, bucket


def _bias_tiles(rel_bias):
    tiles = []
    for d in PATTERNS:
        band, bucket = _band_bucket(d)
        onehot = (jnp.asarray(bucket, jnp.int32)[None] == jnp.arange(NBUCKETS, dtype=jnp.int32)[:, None, None])
        bias = jnp.einsum("bqk,bh->hqk", onehot.astype(F32), rel_bias, precision=lax.Precision.HIGHEST)
        tiles.append(jnp.where(jnp.asarray(band)[None], bias, NEG))
    return jnp.stack(tiles, 0)


ATT = 2048
HP = 2 * HE
LOG2E = 1.4426950408889634
LN2 = 0.6931471805599453
QSCALE2 = (HE ** -0.5) * LOG2E
AGRP = 8
AGRP_B = 4


def _attn_blocks():
    out = []
    for p, d in enumerate(PATTERNS):
        for r in range(d):
            for n in range(ATT // (d * ABLK)):
                out.append((p, d, r, n))
    return out


def _attn_fwd_call(proj, biases):
    S = proj.shape[0]
    nt = S // ATT

    def body(q_ref, k_ref, v_ref, z_ref, b_ref, a_ref, l_ref, oa_ref, kc, vc, op, lp):
        i = pl.program_id(1)

        @pl.when(i == 0)
        def _():
            kc[0:ATT] = jnp.zeros((ATT, HP), F32)
            vc[0:ATT] = jnp.zeros((ATT, HP), F32)

        @pl.when(i > 0)
        def _():
            kc[0:ATT] = kc[ATT:2 * ATT]
            vc[0:ATT] = vc[ATT:2 * ATT]

        kc[ATT:2 * ATT] = k_ref[...]
        vc[ATT:2 * ATT] = v_ref[...]
        col = lax.broadcasted_iota(jnp.int32, (ABLK, 2 * ABLK), 1)
        dead = jnp.logical_and(i == 0, col < ABLK)
        blocks = _attn_blocks()
        hs = (slice(0, HE), slice(HE, 2 * HE))
        for g0 in range(0, len(blocks), AGRP):
            grp = blocks[g0:g0 + AGRP]
            qrows = [pl.ds(n * ABLK * d + r, ABLK, stride=d) for p, d, r, n in grp]
            krows = [pl.ds(ATT + (n - 1) * ABLK * d + r, 2 * ABLK, stride=d) for p, d, r, n in grp]
            qs = [(q_ref[qr, :] * QSCALE2).astype(BF16) for qr in qrows]
            ks = [kc[kr, :].astype(BF16) for kr in krows]
            vs = [vc[kr, :].astype(BF16) for kr in krows]
            ss = [[_nt(qs[b][:, sl], ks[b][:, sl]) + b_ref[grp[b][0], e] for e, sl in enumerate(hs)]
                  for b in range(len(grp))]
            ss = [[jnp.where(dead, NEG, s) if grp[b][3] == 0 else s for s in ss[b]] for b in range(len(grp))]
            mxs = [[jnp.max(s, axis=-1, keepdims=True) for s in sb] for sb in ss]
            pes = [[jnp.exp2(s - mx) for s, mx in zip(sb, mb)] for sb, mb in zip(ss, mxs)]
            dens = [[jnp.sum(pe, axis=-1, keepdims=True) for pe in pb] for pb in pes]
            pvs = [[_nn(pe.astype(BF16), vs[b][:, sl]) for pe, sl in zip(pes[b], hs)] for b in range(len(grp))]
            for b in range(len(grp)):
                p, d, r, n = grp[b]
                prow = pl.ds(p * ATT + n * ABLK * d + r, ABLK, stride=d)
                lp[prow, :] = jnp.concatenate(
                    [jnp.broadcast_to(mx + jnp.log2(dn), (ABLK, HE)) for mx, dn in zip(mxs[b], dens[b])], axis=1)
                op[prow, :] = jnp.concatenate([pv / dn for pv, dn in zip(pvs[b], dens[b])], axis=1)
        rt = 256
        for t in range(ATT // rt):
            rows = slice(t * rt, (t + 1) * rt)
            pr = [slice(p * ATT + t * rt, p * ATT + (t + 1) * rt) for p in range(3)]
            la, lb_, lc = lp[pr[0], :], lp[pr[1], :], lp[pr[2], :]
            m = jnp.maximum(jnp.maximum(la, lb_), lc)
            ea, eb, ec = jnp.exp2(la - m), jnp.exp2(lb_ - m), jnp.exp2(lc - m)
            den = ea + eb + ec
            att = (ea * op[pr[0], :] + eb * op[pr[1], :] + ec * op[pr[2], :]) / den
            a_ref[rows, :] = att
            l_ref[rows, :] = m + jnp.log2(den)
            z = z_ref[rows, :]
            oa_ref[rows, :] = (att * (z * _sig(z))).astype(BF16)

    def pcol(c):
        return pl.BlockSpec((ATT, HP), lambda h, i: (i, c * 4 + h))

    out = pl.BlockSpec((ATT, HP), lambda h, i: (i, h))
    return pl.pallas_call(
        body, name="attn_fwd", grid=(4, nt),
        in_specs=[pcol(0), pcol(1), pcol(2), pcol(3),
                  pl.BlockSpec((3, 2, ABLK, 2 * ABLK), lambda h, i: (0, h, 0, 0))],
        out_specs=[out, out, out],
        out_shape=[jax.ShapeDtypeStruct((S, AW), F32), jax.ShapeDtypeStruct((S, AW), F32),
                   jax.ShapeDtypeStruct((S, AW), BF16)],
        scratch_shapes=[pltpu.VMEM((2 * ATT, HP), F32), pltpu.VMEM((2 * ATT, HP), F32),
                        pltpu.VMEM((3 * ATT, HP), F32), pltpu.VMEM((3 * ATT, HP), F32)],
        compiler_params=_cp(),
    )(proj, proj, proj, proj, biases)


def _attn_bwd_call(proj, dattn, lse, dsum, biases):
    S = proj.shape[0]
    nt = S // ATT

    def body(q_ref, k_ref, v_ref, do_ref, l_ref, ds_ref, b_ref, dq_ref, dk_ref, dv_ref, db_ref,
             kc, vc, dkc, dvc, dqa):
        i = pl.program_id(1)

        @pl.when(i == 0)
        def _():
            kc[ATT:2 * ATT] = jnp.zeros((ATT, HP), F32)
            vc[ATT:2 * ATT] = jnp.zeros((ATT, HP), F32)
            dkc[ATT:2 * ATT] = jnp.zeros((ATT, HP), F32)
            dvc[ATT:2 * ATT] = jnp.zeros((ATT, HP), F32)
            db_ref[...] = jnp.zeros_like(db_ref)

        @pl.when(i < nt)
        def _():
            kc[0:ATT] = kc[ATT:2 * ATT]
            vc[0:ATT] = vc[ATT:2 * ATT]
            dkc[0:ATT] = dkc[ATT:2 * ATT]
            dvc[0:ATT] = dvc[ATT:2 * ATT]
            kc[ATT:2 * ATT] = k_ref[...]
            vc[ATT:2 * ATT] = v_ref[...]
            dkc[ATT:2 * ATT] = jnp.zeros((ATT, HP), F32)
            dvc[ATT:2 * ATT] = jnp.zeros((ATT, HP), F32)
            col = lax.broadcasted_iota(jnp.int32, (ABLK, 2 * ABLK), 1)
            dead = jnp.logical_and(i == 0, col < ABLK)
            blocks = _attn_blocks()
            hs = (slice(0, HE), slice(HE, 2 * HE))
            for g0 in range(0, len(blocks), AGRP_B):
                grp = blocks[g0:g0 + AGRP_B]
                nb_ = range(len(grp))
                qrows = [pl.ds(n * ABLK * d + r, ABLK, stride=d) for p, d, r, n in grp]
                krows = [pl.ds(ATT + (n - 1) * ABLK * d + r, 2 * ABLK, stride=d) for p, d, r, n in grp]
                qs = [(q_ref[qr, :] * QSCALE2).astype(BF16) for qr in qrows]
                ks = [kc[kr, :].astype(BF16) for kr in krows]
                vs = [vc[kr, :].astype(BF16) for kr in krows]
                dos = [do_ref[qr, :].astype(BF16) for qr in qrows]
                lvs = [l_ref[qr, :] for qr in qrows]
                dsvs = [ds_ref[qr, :] for qr in qrows]
                ss = [[_nt(qs[b][:, sl], ks[b][:, sl]) + b_ref[grp[b][0], e] for e, sl in enumerate(hs)] for b in nb_]
                ss = [[jnp.where(dead, NEG, s) if grp[b][3] == 0 else s for s in ss[b]] for b in nb_]
                dps = [[_nt(dos[b][:, sl], vs[b][:, sl]) for sl in hs] for b in nb_]
                pes = [[jnp.exp2(ss[b][e] - lvs[b][:, e * HE:e * HE + 1]) for e in range(2)] for b in nb_]
                dscs = [[pes[b][e] * (dps[b][e] - dsvs[b][:, e * HE:e * HE + 1]) for e in range(2)] for b in nb_]
                for e in range(2):
                    tot = dscs[0][e]
                    for b in range(1, len(grp)):
                        tot = tot + dscs[b][e]
                    db_ref[grp[0][0], e] += tot
                dsbs = [[t.astype(BF16) for t in tb] for tb in dscs]
                dqs = [[_nn(dsbs[b][e], ks[b][:, sl]) * (HE ** -0.5) for e, sl in enumerate(hs)] for b in nb_]
                dks = [[_tn(dsbs[b][e], qs[b][:, sl]) for e, sl in enumerate(hs)] for b in nb_]
                dvs = [[_tn(pes[b][e].astype(BF16), dos[b][:, sl]) for e, sl in enumerate(hs)] for b in nb_]
                for b in nb_:
                    dq = jnp.concatenate(dqs[b], axis=1)
                    if grp[b][0] == 0:
                        dqa[qrows[b], :] = dq
                    else:
                        dqa[qrows[b], :] += dq
                    dkc[krows[b], :] += jnp.concatenate(dks[b], axis=1)
                    dvc[krows[b], :] += jnp.concatenate(dvs[b], axis=1)
            dq_ref[...] = dqa[...].astype(BF16)
            dk_ref[...] = (dkc[0:ATT] * LN2).astype(BF16)
            dv_ref[...] = dvc[0:ATT].astype(BF16)

        @pl.when(i == nt)
        def _():
            dk_ref[...] = (dkc[ATT:2 * ATT] * LN2).astype(BF16)
            dv_ref[...] = dvc[ATT:2 * ATT].astype(BF16)

    def pcol(c):
        return pl.BlockSpec((ATT, HP), lambda h, i: (jnp.minimum(i, nt - 1), c * 4 + h))

    qrow = pl.BlockSpec((ATT, HP), lambda h, i: (jnp.minimum(i, nt - 1), h))
    krow = pl.BlockSpec((ATT, HP), lambda h, i: (jnp.maximum(i - 1, 0), h))
    bspec = pl.BlockSpec((3, 2, ABLK, 2 * ABLK), lambda h, i: (0, h, 0, 0))
    return pl.pallas_call(
        body, name="attn_bwd", grid=(4, nt + 1),
        in_specs=[pcol(0), pcol(1), pcol(2), qrow, qrow, qrow, bspec],
        out_specs=[qrow, krow, krow, bspec],
        out_shape=[jax.ShapeDtypeStruct((S, AW), BF16)] * 3
                  + [jax.ShapeDtypeStruct((3, NH, ABLK, 2 * ABLK), F32)],
        scratch_shapes=[pltpu.VMEM((2 * ATT, HP), F32)] * 4 + [pltpu.VMEM((ATT, HP), F32)],
        compiler_params=_cp(),
    )(proj, proj, proj, dattn, lse, dsum, biases)


HRB = 256


def _tri_masks():
    row = lax.broadcasted_iota(jnp.int32, (CH, CH), 0)
    col = lax.broadcasted_iota(jnp.int32, (CH, CH), 1)
    return row >= col


def _heads():
    return [slice(hh * HK, (hh + 1) * HK) for hh in range(NH)]


def _hgrn_gate_heads(q_ref, f_ref, rows, lbv):
    out = []
    for sl in _heads():
        qraw = q_ref[rows, sl]
        sq = _sig(qraw)
        sf = _sig(f_ref[rows, sl])
        f = lbv[:, sl] + (1.0 - lbv[:, sl]) * sf
        out.append(dict(qraw=qraw, sq=sq, q=qraw * sq, sf=sf, f=f, k=1.0 - f, lg=jnp.log(f)))
    return out


def _hgrn_decay_heads(gh, b):
    bl = b[CH - 1:CH, :]
    bm = b[CH // 2 - 1:CH // 2, :]
    ebm = jnp.exp(bm)
    eblm = jnp.exp(bl - bm)
    ebl = jnp.exp(bl)
    out = []
    for g, sl in zip(gh, _heads()):
        d = b[:, sl] - bm[:, sl]
        e1 = jnp.exp(d)
        e2 = jnp.exp(-d)
        qs = g["q"] * e1
        ks = g["k"] * e2
        qe = qs * ebm[:, sl]
        kd = ks * eblm[:, sl]
        out.append(dict(e1=e1, e2=e2, qe=qe, kd=kd, ebm=ebm[:, sl], eblm=eblm[:, sl], ebl=ebl[:, sl],
                        qsb=qs.astype(BF16), ksb=ks.astype(BF16), qeb=qe.astype(BF16), kdb=kd.astype(BF16)))
    return out


def _hgrn_fwd_call(proj, lb, gn):
    S = proj.shape[0]
    nc = S // CH
    cps = HRB // CH

    def body(q_ref, f_ref, i_ref, z_ref, lb_ref, gn_ref, or_ref, ob_ref, st_ref, st):
        @pl.when(pl.program_id(0) == 0)
        def _():
            st[...] = jnp.zeros_like(st)

        low = _tri_masks()
        tri = low.astype(BF16)
        lbv = lb_ref[...]
        hs = _heads()
        for ci in range(cps):
            rows = slice(ci * CH, (ci + 1) * CH)
            gh = _hgrn_gate_heads(q_ref, f_ref, rows, lbv)
            b = _exact_mm(tri, jnp.concatenate([g["lg"] for g in gh], axis=1))
            dh = _hgrn_decay_heads(gh, b)
            vbs = [i_ref[rows, sl].astype(BF16) for sl in hs]
            st_ref[ci] = st[...]
            s0s = [st[sl, :] for sl in hs]
            as_ = [_nt(d["qsb"], d["ksb"]) for d in dh]
            ois = [_nt(d["qeb"], s0.astype(BF16)) for d, s0 in zip(dh, s0s)]
            sts = [_tn(vb, d["kdb"]) for vb, d in zip(vbs, dh)]
            abs_ = [jnp.where(low, a, 0.0).astype(BF16) for a in as_]
            os_ = [oi + _nn(a, vb) for oi, a, vb in zip(ois, abs_, vbs)]
            for sl, s0, sn, d, o in zip(hs, s0s, sts, dh, os_):
                st[sl, :] = s0 * d["ebl"] + sn
                or_ref[rows, sl] = o
                r = lax.rsqrt(jnp.mean(o * o, axis=-1, keepdims=True) + EPS)
                z = z_ref[rows, sl]
                ob_ref[rows, sl] = (o * r * gn_ref[:, sl] * (z * _sig(z))).astype(BF16)

    def pcol(c):
        return pl.BlockSpec((HRB, D), lambda i: (i, c))

    vec = pl.BlockSpec((1, D), lambda i: (0, 0))
    row = pl.BlockSpec((HRB, D), lambda i: (i, 0))
    return pl.pallas_call(
        body, name="hgrn_fwd", grid=(S // HRB,),
        in_specs=[pcol(2), pcol(3), pcol(4), pcol(5), vec, vec],
        out_specs=[row, row, pl.BlockSpec((cps, NH * HK, HK), lambda i: (i, 0, 0))],
        out_shape=[jax.ShapeDtypeStruct((S, D), F32), jax.ShapeDtypeStruct((S, D), BF16),
                   jax.ShapeDtypeStruct((nc, NH * HK, HK), F32)],
        scratch_shapes=[pltpu.VMEM((NH * HK, HK), F32)],
        compiler_params=_cp(),
    )(proj, proj, proj, proj, lb, gn)


def _hgrn_bwd_call(proj, oraw, dob, states, lb, gn):
    S = proj.shape[0]
    nblk = S // HRB
    cps = HRB // CH

    def body(q_ref, f_ref, i_ref, z_ref, or_ref, dob_ref, st_ref, lb_ref, gn_ref, dh_ref, acc_ref, dst):
        @pl.when(pl.program_id(0) == 0)
        def _():
            dst[...] = jnp.zeros_like(dst)
            acc_ref[...] = jnp.zeros_like(acc_ref)

        low = _tri_masks()
        tri = low.astype(BF16)
        triu = jnp.logical_not(_tri_masks()) | (lax.broadcasted_iota(jnp.int32, (CH, CH), 0)
                                               == lax.broadcasted_iota(jnp.int32, (CH, CH), 1))
        triu = triu.astype(BF16)
        lbv = lb_ref[...]
        hs = _heads()
        for ci in reversed(range(cps)):
            rows = slice(ci * CH, (ci + 1) * CH)
            dobs, dgns = [], []
            for sl in hs:
                o = or_ref[rows, sl]
                z = z_ref[rows, sl]
                sz = _sig(z)
                gnv = gn_ref[:, sl]
                dobv = dob_ref[rows, sl]
                r = lax.rsqrt(jnp.mean(o * o, axis=-1, keepdims=True) + EPS)
                onr = o * r
                don = dobv * (z * sz)
                dh_ref[rows, 3 * D + sl.start:3 * D + sl.stop] = (
                    dobv * (onr * gnv) * (sz * (1.0 + z * (1.0 - sz)))).astype(BF16)
                dgns.append(jnp.sum(don * onr, axis=0, keepdims=True))
                gh_ = don * gnv
                dobs.append((r * (gh_ - onr * jnp.mean(gh_ * onr, axis=-1, keepdims=True))).astype(BF16))
            acc_ref[1:2, :] += jnp.concatenate(dgns, axis=1)

            gh = _hgrn_gate_heads(q_ref, f_ref, rows, lbv)
            b = _exact_mm(tri, jnp.concatenate([g["lg"] for g in gh], axis=1))
            dh = _hgrn_decay_heads(gh, b)
            vbs = [i_ref[rows, sl].astype(BF16) for sl in hs]

            st0s = [st_ref[ci, sl, :] for sl in hs]
            dst1s = [dst[sl, :] for sl in hs]
            dst1bs = [t.astype(BF16) for t in dst1s]
            as_ = [_nt(d["qsb"], d["ksb"]) for d in dh]
            das_ = [_nt(do, vb) for do, vb in zip(dobs, vbs)]
            dqes = [_nn(do, s0.astype(BF16)) for do, s0 in zip(dobs, st0s)]
            dkds = [_nn(vb, d1) for vb, d1 in zip(vbs, dst1bs)]
            dvis = [_nt(d["kdb"], d1) for d, d1 in zip(dh, dst1bs)]
            dsts = [_tn(do, d["qeb"]) for do, d in zip(dobs, dh)]
            abs_ = [jnp.where(low, a, 0.0).astype(BF16) for a in as_]
            dabs_ = [jnp.where(low, a, 0.0).astype(BF16) for a in das_]
            dqss = [_nn(da, d["ksb"]) for da, d in zip(dabs_, dh)]
            dkss = [_tn(da, d["qsb"]) for da, d in zip(dabs_, dh)]
            dvs_ = [_tn(a, do) + dvi for a, do, dvi in zip(abs_, dobs, dvis)]

            dqs_, dks_, dbs_, exs_ = [], [], [], []
            for hh, sl in enumerate(hs):
                d, d1, s0 = dh[hh], dst1s[hh], st0s[hh]
                dqe, dqs, dks, dkd = dqes[hh], dqss[hh], dkss[hh], dkds[hh]
                dst[sl, :] = dsts[hh] + d1 * d["ebl"]
                dh_ref[rows, 2 * D + sl.start:2 * D + sl.stop] = dvs_[hh].astype(BF16)
                dqs_.append((dqe * d["ebm"] + dqs) * d["e1"])
                dks_.append((dks + dkd * d["eblm"]) * d["e2"])
                dkdkd = dkd * d["kd"]
                dbs_.append(dqe * d["qe"] + dqs * d["qsb"].astype(F32) - dks * d["ksb"].astype(F32) - dkdkd)
                exs_.append(jnp.sum(dkdkd, axis=0, keepdims=True)
                            + jnp.sum(d1 * s0, axis=0, keepdims=True) * d["ebl"])
            dg = _exact_mm(triu, jnp.concatenate(dbs_, axis=1)) + jnp.concatenate(exs_, axis=1)

            dlbs = []
            for hh, sl in enumerate(hs):
                g = gh[hh]
                df = dg[:, sl] / g["f"] - dks_[hh]
                sf = g["sf"]
                omsf = 1.0 - sf
                sq = g["sq"]
                dlbs.append(jnp.sum(df * omsf, axis=0, keepdims=True))
                dh_ref[rows, sl] = (dqs_[hh] * (sq * (1.0 + g["qraw"] * (1.0 - sq)))).astype(BF16)
                dh_ref[rows, D + sl.start:D + sl.stop] = (df * (1.0 - lbv[:, sl]) * sf * omsf).astype(BF16)
            acc_ref[0:1, :] += jnp.concatenate(dlbs, axis=1)

    def pcol(c):
        return pl.BlockSpec((HRB, D), lambda i: (nblk - 1 - i, c))

    vec = pl.BlockSpec((1, D), lambda i: (0, 0))
    row = pl.BlockSpec((HRB, D), lambda i: (nblk - 1 - i, 0))
    return pl.pallas_call(
        body, name="hgrn_bwd", grid=(nblk,),
        in_specs=[pcol(2), pcol(3), pcol(4), pcol(5), row, row,
                  pl.BlockSpec((cps, NH * HK, HK), lambda i: (nblk - 1 - i, 0, 0)), vec, vec],
        out_specs=[pl.BlockSpec((HRB, 4 * D), lambda i: (nblk - 1 - i, 0)),
                   pl.BlockSpec((8, D), lambda i: (0, 0))],
        out_shape=[jax.ShapeDtypeStruct((S, 4 * D), BF16), jax.ShapeDtypeStruct((8, D), F32)],
        scratch_shapes=[pltpu.VMEM((NH * HK, HK), F32)],
        compiler_params=_cp(),
    )(proj, proj, proj, proj, oraw, dob, states, lb, gn)


def _merge_call(oa, ob, proj, x, tgt, vecs, attn, wa, wb, wo, hsum):
    S = x.shape[0]
    tm = 256

    def body(oa_ref, ob_ref, ga_ref, gb_ref, x_ref, t_ref, v_ref, at_ref, za_ref, wa_ref, wb_ref, wo_ref, hs_ref,
             y_ref, dx2_ref, du_ref, dya_ref, dyb_ref, dg_ref, dat_ref, dsum_ref, dza_ref, dob_ref, acc_ref, ls_ref):
        @pl.when(pl.program_id(0) == 0)
        def _():
            acc_ref[...] = jnp.zeros_like(acc_ref)
            ls_ref[...] = jnp.zeros_like(ls_ref)

        gate = v_ref[0:1, :]
        fg = v_ref[1:2, :]
        ya = _nn(oa_ref[...], wa_ref[...])
        yb = _nn(ob_ref[...], wb_ref[...])
        sa = _sig(ga_ref[...])
        sb = _sig(gb_ref[...])
        y = (sa * ya + sb * yb).astype(BF16)
        y_ref[...] = y
        u = _nn(y, wo_ref[...])
        x2v = x_ref[...] + gate * u
        r = lax.rsqrt(jnp.mean(x2v * x2v, axis=-1, keepdims=True) + EPS)
        err = x2v * r * fg - t_ref[...]
        ls_ref[...] += jnp.sum(err * err)
        dout = err * (1.0 / D)
        gh = dout * fg
        dx2 = r * gh - x2v * (r * r * r * jnp.mean(gh * x2v, axis=-1, keepdims=True))
        acc_ref[0:1, :] += jnp.sum(dx2 * u, axis=0, keepdims=True)
        acc_ref[1:2, :] += jnp.sum(dout * x2v * r, axis=0, keepdims=True)
        dx2_ref[...] = dx2
        du = (dx2 * gate).astype(BF16)
        du_ref[...] = du
        dy = _nt(du, wo_ref[...])
        dya = (dy * sa).astype(BF16)
        dyb = (dy * sb).astype(BF16)
        dya_ref[...] = dya
        dyb_ref[...] = dyb
        dg_ref[:, 0:D] = (dy * ya * sa * (1.0 - sa)).astype(BF16)
        dg_ref[:, D:2 * D] = (dy * yb * sb * (1.0 - sb)).astype(BF16)
        doa = _nt(dya, wa_ref[...])
        dob_ref[...] = _nt(dyb, wb_ref[...])
        za = za_ref[...]
        sz = _sig(za)
        att = at_ref[...]
        dat = doa * (za * sz)
        dat_ref[...] = dat
        dza_ref[...] = (doa * att * (sz * (1.0 + za * (1.0 - sz)))).astype(BF16)
        dsum_ref[...] = _exact_mm_r(dat * att, hs_ref[...])

    row = pl.BlockSpec((tm, D), lambda i: (i, 0))
    arow = pl.BlockSpec((tm, AW), lambda i: (i, 0))
    full = lambda a: pl.BlockSpec(a.shape, lambda i: (0, 0))
    return pl.pallas_call(
        body, name="merge_fwd_bwd", grid=(S // tm,),
        in_specs=[arow, row, pl.BlockSpec((tm, D), lambda i: (i, 6)), pl.BlockSpec((tm, D), lambda i: (i, 7)),
                  row, row, pl.BlockSpec((8, D), lambda i: (0, 0)), arow, pl.BlockSpec((tm, AW), lambda i: (i, 3)),
                  full(wa), full(wb), full(wo), full(hsum)],
        out_specs=[row, row, row, row, row, pl.BlockSpec((tm, 2 * D), lambda i: (i, 0)),
                   arow, arow, arow, row, pl.BlockSpec((8, D), lambda i: (0, 0)),
                   pl.BlockSpec((8, 128), lambda i: (0, 0))],
        out_shape=[jax.ShapeDtypeStruct((S, D), BF16), jax.ShapeDtypeStruct((S, D), F32),
                   jax.ShapeDtypeStruct((S, D), BF16), jax.ShapeDtypeStruct((S, D), BF16),
                   jax.ShapeDtypeStruct((S, D), BF16), jax.ShapeDtypeStruct((S, 2 * D), BF16),
                   jax.ShapeDtypeStruct((S, AW), F32), jax.ShapeDtypeStruct((S, AW), F32),
                   jax.ShapeDtypeStruct((S, AW), BF16), jax.ShapeDtypeStruct((S, D), F32),
                   jax.ShapeDtypeStruct((8, D), F32), jax.ShapeDtypeStruct((8, 128), F32)],
        compiler_params=_cp(),
    )(oa, ob, proj, proj, x, tgt, vecs, attn, proj, wa, wb, wo, hsum)


def _atb_call(a, b, name):
    S, K = a.shape
    N = b.shape[1]
    tm = min(1024, S)

    def body(a_ref, b_ref, o_ref, ob_ref):
        @pl.when(pl.program_id(0) == 0)
        def _():
            o_ref[...] = jnp.zeros_like(o_ref)

        o_ref[...] += _tn(a_ref[...], b_ref[...])

        @pl.when(pl.program_id(0) == S // tm - 1)
        def _():
            ob_ref[...] = o_ref[...].astype(BF16)

    ospec = pl.BlockSpec((K, N), lambda i: (0, 0))
    return pl.pallas_call(
        body, name=name, grid=(S // tm,),
        in_specs=[pl.BlockSpec((tm, K), lambda i: (i, 0)), pl.BlockSpec((tm, N), lambda i: (i, 0))],
        out_specs=[ospec, ospec],
        out_shape=[jax.ShapeDtypeStruct((K, N), F32), jax.ShapeDtypeStruct((K, N), BF16)], compiler_params=_cp(),
    )(a, b)


def _dwin_call(h_t, dqkvz, d_hgrn, d_gates):
    S = h_t.shape[1]
    tm = 512
    tn = 2048

    def body(h_ref, q_ref, k_ref, v_ref, z_ref, m_ref, g_ref, o_ref, ob_ref):
        j = pl.program_id(0)

        @pl.when(pl.program_id(1) == 0)
        def _():
            o_ref[...] = jnp.zeros_like(o_ref)

        hv = h_ref[...]

        @pl.when(j == 0)
        def _():
            for cidx, r in enumerate((q_ref, k_ref, v_ref, z_ref)):
                o_ref[0, :, cidx * AW:(cidx + 1) * AW] += _nn(hv, r[...])

        @pl.when(jnp.logical_or(j == 1, j == 2))
        def _():
            o_ref[0] += _nn(hv, m_ref[...])

        @pl.when(j == 3)
        def _():
            o_ref[0] += _nn(hv, g_ref[...])

        @pl.when(pl.program_id(1) == S // tm - 1)
        def _():
            ob_ref[...] = o_ref[...].astype(BF16)

    aspec = pl.BlockSpec((tm, AW), lambda j, i: (jnp.where(j == 0, i, 0), 0))
    ospec = pl.BlockSpec((1, D, tn), lambda j, i: (j, 0, 0))
    return pl.pallas_call(
        body, name="dw_in", grid=(4, S // tm),
        in_specs=[pl.BlockSpec((D, tm), lambda j, i: (0, i)), aspec, aspec, aspec, aspec,
                  pl.BlockSpec((tm, tn), lambda j, i: (jnp.where(jnp.logical_or(j == 1, j == 2), i, 0),
                                                       jnp.where(j == 2, 1, 0))),
                  pl.BlockSpec((tm, tn), lambda j, i: (jnp.where(j == 3, i, 0), 0))],
        out_specs=[ospec, ospec],
        out_shape=[jax.ShapeDtypeStruct((4, D, tn), F32), jax.ShapeDtypeStruct((4, D, tn), BF16)],
        compiler_params=_cp(),
    )(h_t, *dqkvz, d_hgrn, d_gates)


def _dh_call(dqkvz, d_hgrn, d_gates, w_in, x, dx2, vecs):
    S = x.shape[0]
    tm = 512

    def body(q_ref, k_ref, v_ref, z_ref, m_ref, g_ref, w_ref, x_ref, dx2_ref, p_ref, gx_ref, acc_ref):
        @pl.when(pl.program_id(0) == 0)
        def _():
            acc_ref[...] = jnp.zeros_like(acc_ref)

        dhv = _nt(q_ref[...], w_ref[:, 0:AW])
        for cidx, r in enumerate((k_ref, v_ref, z_ref)):
            dhv += _nt(r[...], w_ref[:, (cidx + 1) * AW:(cidx + 2) * AW])
        dhv += _nt(m_ref[...], w_ref[:, 4 * AW:4 * AW + 4 * D])
        dhv += _nt(g_ref[...], w_ref[:, 4 * AW + 4 * D:NPROJ])
        xv = x_ref[...]
        r = lax.rsqrt(jnp.mean(xv * xv, axis=-1, keepdims=True) + EPS)
        xn = xv * r
        acc_ref[0:1, :] += jnp.sum(dhv, axis=0, keepdims=True)
        acc_ref[1:2, :] += jnp.sum(dhv * xn * p_ref[1:2, :], axis=0, keepdims=True)
        acc_ref[2:3, :] += jnp.sum(dhv * xn * p_ref[2:3, :], axis=0, keepdims=True)
        dxn = dhv * p_ref[0:1, :]
        gx_ref[...] = dx2_ref[...] + r * dxn - xv * (r * r * r * jnp.mean(dxn * xv, axis=-1, keepdims=True))

    row = pl.BlockSpec((tm, D), lambda i: (i, 0))
    aspec = pl.BlockSpec((tm, AW), lambda i: (i, 0))
    const = lambda shape: pl.BlockSpec(shape, lambda i: (0, 0))
    return pl.pallas_call(
        body, name="dh_gradx", grid=(S // tm,),
        in_specs=[aspec, aspec, aspec, aspec,
                  pl.BlockSpec((tm, 4 * D), lambda i: (i, 0)), pl.BlockSpec((tm, 2 * D), lambda i: (i, 0)),
                  pl.BlockSpec((D, NPROJ), lambda i: (0, 0), pipeline_mode=pl.Buffered(1)),
                  row, row, const((8, D))],
        out_specs=[row, const((8, D))],
        out_shape=[jax.ShapeDtypeStruct((S, D), F32), jax.ShapeDtypeStruct((8, D), F32)],
        compiler_params=_cp(),
    )(*dqkvz, d_hgrn, d_gates, w_in, x, dx2, vecs)


def _adamw_math(w, g, m, v):
    m = B1 * m + (1.0 - B1) * g
    v = B2 * v + (1.0 - B2) * (g * g)
    m_hat = m / (1.0 - B1 ** STEP)
    v_hat = v / (1.0 - B2 ** STEP)
    delta = -LR * (m_hat / (jnp.sqrt(v_hat) + AEPS) + WD * w)
    return delta, m, v


def _adamw_call(w, g, m, v, name):
    R, C = w.shape
    tr = R if R * C * 4 <= (1 << 20) else max(8, (1 << 20) // (C * 4))
    assert R % tr == 0

    def body(w_ref, g_ref, m_ref, v_ref, go_ref, d_ref, nm_ref, nv_ref):
        g = g_ref[...]
        go_ref[...] = g
        d_ref[...], nm_ref[...], nv_ref[...] = _adamw_math(w_ref[...], g, m_ref[...], v_ref[...])

    blk = pl.BlockSpec((tr, C), lambda i: (i, 0))
    return pl.pallas_call(
        body, name=name, grid=(R // tr,), in_specs=[blk] * 4, out_specs=[blk] * 4,
        out_shape=[jax.ShapeDtypeStruct((R, C), F32)] * 4, compiler_params=_cp(),
    )(w, g, m, v)


def _mod_call(c_all, w_ada_s, b_s):
    def body(c_ref, w_ref, b_ref, o_ref):
        cv = c_ref[...]
        sc = cv * _sig(cv)
        o_ref[...] = jnp.dot(sc, w_ref[...], preferred_element_type=F32,
                             precision=lax.Precision.HIGHEST) + b_ref[...]

    return pl.pallas_call(
        body, name="ada_mod", out_shape=jax.ShapeDtypeStruct((8, w_ada_s.shape[1]), F32),
        compiler_params=_cp(),
    )(c_all, w_ada_s, b_s)


def _ada_update_call(sct, dm, w, m, v):
    R, C = w.shape
    tr = 256

    def body(s_ref, d_ref, w_ref, m_ref, v_ref, g_ref, dl_ref, nm_ref, nv_ref):
        g = s_ref[:, 0:1] * d_ref[0:1, :]
        for b in range(1, 8):
            g = g + s_ref[:, b:b + 1] * d_ref[b:b + 1, :]
        g_ref[...] = g
        dl_ref[...], nm_ref[...], nv_ref[...] = _adamw_math(w_ref[...], g, m_ref[...], v_ref[...])

    blk = pl.BlockSpec((tr, C), lambda i: (i, 0))
    return pl.pallas_call(
        body, name="ada_update", grid=(R // tr,),
        in_specs=[pl.BlockSpec((tr, 8), lambda i: (i, 0)), pl.BlockSpec((8, C), lambda i: (0, 0)), blk, blk, blk],
        out_specs=[blk] * 4, out_shape=[jax.ShapeDtypeStruct((R, C), F32)] * 4, compiler_params=_cp(),
    )(sct, dm, w, m, v)


def _sum8_call(packs):
    def body(p_ref, o_ref):
        acc = p_ref[0]
        for k in range(1, 8):
            acc = acc + p_ref[k]
        o_ref[...] = acc

    return pl.pallas_call(
        body, name="sum_small", out_shape=jax.ShapeDtypeStruct(packs.shape[1:], F32), compiler_params=_cp(),
    )(packs)


def _local_step(x, tgt, shift, scale, gate, norm_g, hgrn_onorm_g, rel_bias, lb, final_g, weights_fn, hook=None):
    a = norm_g * (1.0 + scale)
    z6 = jnp.zeros((6, D), F32)
    h, h_t, proj, w_in, wa, wb, wo = weights_fn(x, jnp.concatenate([a, shift, z6], 0))

    biases = _bias_tiles(rel_bias) * LOG2E
    attn, lse, oa = _attn_fwd_call(proj, biases)

    gn = jnp.tile(hgrn_onorm_g, (1, NH))
    oraw, ob, states = _hgrn_fwd_call(proj, lb, gn)

    vecs2 = jnp.concatenate([gate, final_g, z6], 0)
    hsum = jnp.asarray(np.kron(np.eye(NH), np.ones((HE, HE))), BF16)
    y, dx2, du, dya, dyb, d_gates, dattn, dsum, dza, dob, acc2, lsq = _merge_call(
        oa, ob, proj, x, tgt, vecs2, attn, wa, wb, wo, hsum)
    loss = 0.5 * lsq[0, 0] / D
    d_wo, d_wo16 = _atb_call(y, du, "dw_out")
    d_wa, d_wa16 = _atb_call(oa, dya, "dw_branch_a")
    d_wb, d_wb16 = _atb_call(ob, dyb, "dw_branch_b")

    d_hgrn, acch = _hgrn_bwd_call(proj, oraw, dob, states, lb, gn)

    dq, dk, dv, dbs = _attn_bwd_call(proj, dattn, lse, dsum, biases)
    dqkvz = (dq, dk, dv, dza)

    d_win, d_win16 = _dwin_call(h_t, dqkvz, d_hgrn, d_gates)
    tok = hook((d_win, d_wa, d_wb, d_wo), (d_win16, d_wa16, d_wb16, d_wo16)) if hook is not None else 0.0
    one_scale = 1.0 + scale
    grad_x, acc1 = _dh_call(dqkvz, d_hgrn, d_gates, w_in, x, dx2,
                            jnp.concatenate([a + tok, norm_g, one_scale, jnp.zeros((5, D), F32)], 0))

    d_rel = jnp.zeros((NBUCKETS, NH), F32)
    for p, d in enumerate(PATTERNS):
        band, bucket = _band_bucket(d)
        onehot = (bucket[None] == np.arange(NBUCKETS)[:, None, None]) & band[None]
        d_rel = d_rel + jnp.einsum("hqk,bqk->bh", dbs[p], jnp.asarray(onehot, F32),
                                   precision=lax.Precision.HIGHEST)
    d_onorm = jnp.sum(acch[1].reshape(NH, HK), axis=0)

    zrow = jnp.zeros((D,), F32)
    pack = jnp.stack([acc1[0], acc1[1], acc2[0], acc1[2], acc2[1], acch[0],
                      zrow.at[:HK].set(d_onorm), zrow.at[0].set(loss),
                      zrow.at[:NBUCKETS * NH].set(d_rel.reshape(-1))] + [zrow] * 7, 0)
    return grad_x, d_win, d_wa, d_wb, d_wo, pack


def _me():
    return lax.axis_index("x"), lax.axis_index("y"), lax.axis_index("c")


def _peers(x, y):
    return [(1 - x, y), (x, 1 - y), (1 - x, 1 - y)]


def _allgather_small(blk, name):
    m_per, n = blk.shape

    def body(x_ref, out_ref, send_sems, recv_sems, local_sem):
        x, y, c = _me()
        me, sibling = (x, y, c), (x, y, 1 - c)
        chips = _peers(x, y)

        def rows(px, py, pc):
            return out_ref.at[pl.ds((4 * px + 2 * py + pc) * m_per, m_per), :]

        def copy(k, block, to, src=None):
            return pltpu.make_async_remote_copy(
                src_ref=rows(*block) if src is None else src, dst_ref=rows(*block),
                send_sem=send_sems.at[k], recv_sem=recv_sems.at[k], device_id=to, device_id_type=MESH)

        mine = pltpu.make_async_copy(x_ref, rows(*me), local_sem)
        mine.start()
        first = [copy(0, me, sibling, src=x_ref)]
        first += [copy(1 + j, me, (*chip, c), src=x_ref) for j, chip in enumerate(chips)]
        for cp in first:
            cp.start()
        passed = [copy(4 + j, (*chip, c), sibling) for j, chip in enumerate(chips)]
        for j, chip in enumerate(chips):
            copy(1 + j, (*chip, c), me).wait_recv()
            passed[j].start()
        copy(0, sibling, me).wait_recv()
        for j, chip in enumerate(chips):
            copy(4 + j, (*chip, 1 - c), me).wait_recv()
        for cp in first + passed:
            cp.wait_send()
        mine.wait()

    return pl.pallas_call(
        body, name=name, out_shape=jax.ShapeDtypeStruct((8 * m_per, n), blk.dtype),
        in_specs=[pl.BlockSpec(memory_space=pltpu.VMEM)], out_specs=pl.BlockSpec(memory_space=pltpu.VMEM),
        scratch_shapes=[pltpu.SemaphoreType.DMA((7,)), pltpu.SemaphoreType.DMA((7,)), pltpu.SemaphoreType.DMA],
    )(blk)


ANY = pl.BlockSpec(memory_space=pl.ANY)


HBM = pl.BlockSpec(memory_space=pltpu.HBM)
SEM = pl.BlockSpec(memory_space=pltpu.SEMAPHORE)
EFFECT = pltpu.SideEffectType.DATAFLOW_SIDE_EFFECTING


def _w_part(t, ref, j, half):
    if t == 0:
        return ref.at[pl.ds(half * (D // 2), D // 2), pl.ds(j * 2048, 2048)]
    if t == 1:
        return ref.at[pl.ds(half * (AW // 2), AW // 2), pl.ds(j * 256, 256)]
    return ref.at[pl.ds(j * 256 + half * 128, 128), :]


def _w_ici_copies(fulls, send_sems, recv_sems):
    x, y, c = _me()
    outs, ins = [], []
    for t in range(4):
        for k, chip in enumerate(_peers(x, y)):
            mine = _w_part(t, fulls[t], 2 * x + y, c)
            theirs = _w_part(t, fulls[t], 2 * chip[0] + chip[1], c)
            kw = dict(send_sem=send_sems.at[3 * t + k], recv_sem=recv_sems.at[3 * t + k],
                      device_id=(*chip, c), device_id_type=MESH)
            outs.append(pltpu.make_async_remote_copy(src_ref=mine, dst_ref=mine, **kw))
            ins.append(pltpu.make_async_remote_copy(src_ref=theirs, dst_ref=theirs, **kw))
    return outs, ins


def _gather_ici_start(fulls):
    def body(f0, f1, f2, f3, send_sems, recv_sems, t0, t1, t2, t3, token):
        for cp in _w_ici_copies([f0, f1, f2, f3], send_sems, recv_sems)[0]:
            cp.start()
        token[...] = jnp.zeros_like(token)

    res = pl.pallas_call(
        body, name="gather_ici_start",
        out_shape=(pltpu.SemaphoreType.DMA((12,)), pltpu.SemaphoreType.DMA((12,)),
                   *[pltpu.HBM(a.shape, a.dtype) for a in fulls], jax.ShapeDtypeStruct((8, 128), F32)),
        in_specs=[HBM] * 4, out_specs=(SEM, SEM, *[HBM] * 4, pl.BlockSpec(memory_space=pltpu.VMEM)),
        input_output_aliases={i: 2 + i for i in range(4)},
        compiler_params=pltpu.CompilerParams(has_side_effects=EFFECT),
    )(*[pltpu.with_memory_space_constraint(a, pltpu.HBM) for a in fulls])
    return res[0], res[1], list(res[2:6]), res[6]


def _gather_ici_wait(send_sems, recv_sems, thru, after):
    def body(f0, f1, f2, f3, send_sems, recv_sems, after_ref, g0, g1, g2, g3):
        outs, ins = _w_ici_copies([f0, f1, f2, f3], send_sems, recv_sems)
        for cp in outs:
            cp.wait_send()
        for cp in ins:
            cp.wait_recv()

    return pl.pallas_call(
        body, name="gather_ici_wait", out_shape=tuple(pltpu.HBM(a.shape, a.dtype) for a in thru),
        in_specs=[HBM] * 4 + [SEM, SEM, ANY], out_specs=[HBM] * 4,
        input_output_aliases={i: i for i in range(4)},
        compiler_params=pltpu.CompilerParams(has_side_effects=EFFECT),
    )(*thru, send_sems, recv_sems, after)


def _gather_sibling(fulls):
    def body(i0, i1, i2, i3, o0, o1, o2, o3, send_sems, recv_sems):
        x, y, c = _me()
        outs = [o0, o1, o2, o3]
        cps = []
        for t in range(4):
            for k, chip in enumerate(_peers(x, y)):
                blk = _w_part(t, outs[t], 2 * chip[0] + chip[1], c)
                cp = pltpu.make_async_remote_copy(
                    src_ref=blk, dst_ref=blk, send_sem=send_sems.at[t, k], recv_sem=recv_sems.at[t, k],
                    device_id=(x, y, 1 - c), device_id_type=MESH)
                cp.start()
                cps.append(cp)
        for t in range(4):
            for k, chip in enumerate(_peers(x, y)):
                blk = _w_part(t, outs[t], 2 * chip[0] + chip[1], 1 - c)
                pltpu.make_async_remote_copy(
                    src_ref=blk, dst_ref=blk, send_sem=send_sems.at[t, k], recv_sem=recv_sems.at[t, k],
                    device_id=(x, y, 1 - c), device_id_type=MESH).wait_recv()
        for cp in cps:
            cp.wait_send()

    return pl.pallas_call(
        body, name="gather_sibling",
        out_shape=[jax.ShapeDtypeStruct(s, BF16) for s in FULL_W_SHAPES],
        in_specs=[ANY] * 4, out_specs=[ANY] * 4, input_output_aliases={0: 0, 1: 1, 2: 2, 3: 3},
        scratch_shapes=[pltpu.SemaphoreType.DMA((4, 3)), pltpu.SemaphoreType.DMA((4, 3))],
    )(*fulls)


def _half_of(t, ref, half):
    if t == 0:
        return ref.at[:, pl.ds(half * 512, 512), :]
    if t == 1:
        return ref.at[pl.ds(half * 256, 256), :]
    return ref.at[:, pl.ds(half * 512, 512)]


HALF_SHAPES = [(4, 512, 2048), (256, D), (D, 512), (D, 512)]
PIECE_SHAPES = [(512, 2048), (256, 256), (256, 512), (256, 512)]
SHARD_SHAPES = [(D, 2048), (AW, 256), (256, D), (256, D)]


def _chip_piece(t, ref, j):
    if t == 0:
        return ref.at[j]
    if t == 1:
        return ref.at[:, pl.ds(j * 256, 256)]
    return ref.at[pl.ds(j * 256, 256), :]


def _reduce_sibling_send(gs):
    def body(g0, g1, g2, g3, r0, r1, r2, r3, send_sems, recv_sems):
        x, y, c = _me()
        ins, outs = [g0, g1, g2, g3], [r0, r1, r2, r3]
        cps = []
        for t in range(4):
            cp = pltpu.make_async_remote_copy(
                src_ref=_half_of(t, ins[t], 1 - c), dst_ref=outs[t],
                send_sem=send_sems.at[t], recv_sem=recv_sems.at[t], device_id=(x, y, 1 - c), device_id_type=MESH)
            cp.start()
            cps.append(cp)
        for cp in cps:
            cp.wait_recv()
        for cp in cps:
            cp.wait_send()

    return pl.pallas_call(
        body, name="reduce_sibling", out_shape=[jax.ShapeDtypeStruct(s, BF16) for s in HALF_SHAPES],
        in_specs=[ANY] * 4, out_specs=[ANY] * 4,
        scratch_shapes=[pltpu.SemaphoreType.DMA((4,)), pltpu.SemaphoreType.DMA((4,))],
    )(*gs)


def _chip_copies(hs, lands, send_sems, recv_sems):
    x, y, c = _me()
    cps = []
    for t in range(4):
        for k, chip in enumerate(_peers(x, y)):
            pj = 2 * chip[0] + chip[1]
            cps.append(pltpu.make_async_remote_copy(
                src_ref=_chip_piece(t, hs[t], pj), dst_ref=lands[t].at[k],
                send_sem=send_sems.at[3 * t + k], recv_sem=recv_sems.at[3 * t + k],
                device_id=(*chip, c), device_id_type=MESH))
    return cps


def _reduce_chips_start(hs):
    lands = [lax.empty((3,) + s, BF16) for s in PIECE_SHAPES]

    def body(h0, h1, h2, h3, l0, l1, l2, l3, send_sems, recv_sems, t0, t1, t2, t3, t4, t5, t6, t7, token):
        for cp in _chip_copies([h0, h1, h2, h3], [l0, l1, l2, l3], send_sems, recv_sems):
            cp.start()
        token[...] = jnp.zeros_like(token)

    bufs = list(hs) + lands
    res = pl.pallas_call(
        body, name="reduce_chips_start",
        out_shape=(pltpu.SemaphoreType.DMA((12,)), pltpu.SemaphoreType.DMA((12,)),
                   *[pltpu.HBM(a.shape, a.dtype) for a in bufs], jax.ShapeDtypeStruct((8, 128), F32)),
        in_specs=[HBM] * 8, out_specs=(SEM, SEM, *[HBM] * 8, pl.BlockSpec(memory_space=pltpu.VMEM)),
        input_output_aliases={i: 2 + i for i in range(8)},
        compiler_params=pltpu.CompilerParams(has_side_effects=EFFECT),
    )(*[pltpu.with_memory_space_constraint(a, pltpu.HBM) for a in bufs])
    return res[0], res[1], list(res[2:10]), res[10]


def _reduce_chips_wait(send_sems, recv_sems, thru, after):
    def body(h0, h1, h2, h3, l0, l1, l2, l3, send_sems, recv_sems, after_ref, d0, d1, d2, d3, g0, g1, g2, g3):
        cps = _chip_copies([h0, h1, h2, h3], [l0, l1, l2, l3], send_sems, recv_sems)
        for cp in cps:
            cp.wait_send()
        for cp in cps:
            cp.wait_recv()

    res = pl.pallas_call(
        body, name="reduce_chips_wait", out_shape=tuple(pltpu.HBM(a.shape, a.dtype) for a in thru),
        in_specs=[HBM] * 8 + [SEM, SEM, ANY], out_specs=[HBM] * 8,
        input_output_aliases={i: i for i in range(8)},
        compiler_params=pltpu.CompilerParams(has_side_effects=EFFECT),
    )(*thru, send_sems, recv_sems, after)
    return list(res[4:8])


def _share_sibling(shards):
    def body(i0, i1, i2, i3, o0, o1, o2, o3, send_sems, recv_sems):
        x, y, c = _me()
        outs = [o0, o1, o2, o3]

        def half(t, ref, hf):
            if t == 0:
                return ref.at[pl.ds(hf * 512, 512), :]
            if t == 1:
                return ref.at[pl.ds(hf * 256, 256), :]
            return ref.at[:, pl.ds(hf * 512, 512)]

        cps = []
        for t in range(4):
            mine = half(t, outs[t], c)
            cp = pltpu.make_async_remote_copy(
                src_ref=mine, dst_ref=mine, send_sem=send_sems.at[t], recv_sem=recv_sems.at[t],
                device_id=(x, y, 1 - c), device_id_type=MESH)
            cp.start()
            cps.append(cp)
        for t in range(4):
            theirs = half(t, outs[t], 1 - c)
            pltpu.make_async_remote_copy(
                src_ref=theirs, dst_ref=theirs, send_sem=send_sems.at[t],
                recv_sem=recv_sems.at[t], device_id=(x, y, 1 - c), device_id_type=MESH).wait_recv()
        for cp in cps:
            cp.wait_send()

    return pl.pallas_call(
        body, name="share_sibling", out_shape=[jax.ShapeDtypeStruct(s, F32) for s in SHARD_SHAPES],
        in_specs=[ANY] * 4, out_specs=[ANY] * 4, input_output_aliases={0: 0, 1: 1, 2: 2, 3: 3},
        scratch_shapes=[pltpu.SemaphoreType.DMA((4,)), pltpu.SemaphoreType.DMA((4,))],
    )(*shards)


def _half_blockspec(t, idx_pos):
    if t == 0:
        return pl.BlockSpec((1, 512, 2048), lambda i, s: (i, s[idx_pos], 0)), 4
    if t == 1:
        return pl.BlockSpec((256, D), lambda i, s: (s[idx_pos], 0)), 1
    return pl.BlockSpec((256, 512), lambda i, s: (i, s[idx_pos])), 4


def _half_out_blockspec(t):
    if t == 0:
        return pl.BlockSpec((1, 512, 2048), lambda i, s: (i, 0, 0))
    if t == 1:
        return pl.BlockSpec((256, D), lambda i, s: (0, 0))
    return pl.BlockSpec((256, 512), lambda i, s: (i, 0))


def _add_half_call(t, own, recv, sc, name):
    in_blk, steps = _half_blockspec(t, 0)
    out_blk = _half_out_blockspec(t)

    def body(s_ref, a_ref, b_ref, o_ref, ob_ref):
        v = a_ref[...] + b_ref[...].astype(F32)
        o_ref[...] = v
        ob_ref[...] = v.astype(BF16)

    return pl.pallas_call(
        body, name=name,
        grid_spec=pltpu.PrefetchScalarGridSpec(
            num_scalar_prefetch=1, grid=(steps,), in_specs=[in_blk, out_blk], out_specs=[out_blk, out_blk]),
        out_shape=[jax.ShapeDtypeStruct(HALF_SHAPES[t], F32), jax.ShapeDtypeStruct(HALF_SHAPES[t], BF16)],
        compiler_params=_cp(),
    )(sc, own, recv)


def _final_piece_call(t, chipsum, recv3, sc, name):
    ps = PIECE_SHAPES[t]
    if t == 0:
        own_blk = pl.BlockSpec((1,) + ps, lambda i, s: (s[1], 0, 0))
        o_blk = pl.BlockSpec(ps, lambda i, s: (s[0], 0))
    elif t == 1:
        own_blk = pl.BlockSpec(ps, lambda i, s: (0, s[1]))
        o_blk = pl.BlockSpec(ps, lambda i, s: (s[0], 0))
    else:
        own_blk = pl.BlockSpec(ps, lambda i, s: (s[1], 0))
        o_blk = pl.BlockSpec(ps, lambda i, s: (0, s[0]))
    r_blk = pl.BlockSpec((3,) + ps, lambda i, s: (0, 0, 0))

    def body(s_ref, a_ref, r_ref, o_ref):
        a = a_ref[0] if t == 0 else a_ref[...]
        o_ref[...] = ((a + r_ref[0].astype(F32)) + r_ref[1].astype(F32)) + r_ref[2].astype(F32)

    return pl.pallas_call(
        body, name=name,
        grid_spec=pltpu.PrefetchScalarGridSpec(
            num_scalar_prefetch=1, grid=(1,), in_specs=[own_blk, r_blk], out_specs=o_blk),
        out_shape=jax.ShapeDtypeStruct(SHARD_SHAPES[t], F32), compiler_params=_cp(),
    )(sc, chipsum, recv3)


FULL_W_SHAPES = [(D, NPROJ), (AW, D), (D, D), (D, D)]


def _cast_place_call(t, shard, sc, name):
    if t == 0:
        blk, steps = (512, 2048), 2
        in_blk = pl.BlockSpec(blk, lambda i, s: (i, 0))
        o_blk = pl.BlockSpec(blk, lambda i, s: (i, s[1]))
    elif t == 1:
        blk, steps = (AW, 256), 1
        in_blk = pl.BlockSpec(blk, lambda i, s: (0, 0))
        o_blk = pl.BlockSpec(blk, lambda i, s: (0, s[1]))
    else:
        blk, steps = (256, D), 1
        in_blk = pl.BlockSpec(blk, lambda i, s: (0, 0))
        o_blk = pl.BlockSpec(blk, lambda i, s: (s[1], 0))

    def body(s_ref, a_ref, o_ref, own_ref):
        v = a_ref[...].astype(BF16)
        o_ref[...] = v
        own_ref[...] = v

    return pl.pallas_call(
        body, name=name,
        grid_spec=pltpu.PrefetchScalarGridSpec(
            num_scalar_prefetch=1, grid=(steps,), in_specs=[in_blk], out_specs=[o_blk, in_blk]),
        out_shape=[jax.ShapeDtypeStruct(FULL_W_SHAPES[t], BF16), jax.ShapeDtypeStruct(shard.shape, BF16)],
        compiler_params=_cp(),
    )(sc, shard)


def _lower_bound_fn(hgrn_lb):
    return jnp.cumsum(jax.nn.softmax(hgrn_lb.astype(F32), axis=0), axis=0)[0]


def kernel(x, c, w_ada, b_ada, norm_g, w_in, hgrn_onorm_g, w_branch_a, w_branch_b, w_out, rel_bias, hgrn_lb, final_g, loss_target, m_w_ada, m_b_ada, m_norm_g, m_w_in, m_hgrn_onorm_g, m_w_branch_a, m_w_branch_b, m_w_out, m_rel_bias, m_hgrn_lb, m_final_g, v_w_ada, v_b_ada, v_norm_g, v_w_in, v_hgrn_onorm_g, v_w_branch_a, v_w_branch_b, v_w_out, v_rel_bias, v_hgrn_lb, v_final_g):
    ax, ay, ac = _me()
    chip = 2 * ax + ay
    dev = 4 * ax + 2 * ay + ac
    sc_idx = jnp.stack([ac, chip]).astype(jnp.int32)

    c_all = _allgather_small(jnp.pad(c, ((0, 7), (0, 0))), "gather_c").reshape(8, 8, D)[:, 0]
    b_s = lax.dynamic_slice(b_ada, (0, chip * 768), (1, 768))
    mod_part = _mod_call(c_all, w_ada[0], b_s)
    mod_all = _allgather_small(mod_part, "gather_mod").reshape(8, 8, 768)
    mod_mine = lax.dynamic_index_in_dim(mod_all, dev, axis=1, keepdims=False)
    mod = mod_mine[0::2].reshape(1, 3 * D)

    names = ["w_in", "w_a", "w_b", "w_o"]
    shards, mod = lax.optimization_barrier(([w_in[0], w_branch_a[0], w_branch_b[0], w_out[0]], mod))
    placed = [_cast_place_call(t, shards[t], sc_idx, "cast_" + names[t]) for t in range(4)]
    w_send_sems, w_recv_sems, w_thru, w_token = _gather_ici_start([p_[0] for p_ in placed])
    mod = mod + w_token[0, 0]
    rel_bias_t = rel_bias + w_token[0, 0]
    shift, scale, gate = mod[:, :D], mod[:, D:2 * D], mod[:, 2 * D:]

    def weights_fn(xs, avec):
        proj, h, h_t = _proj_own_call(xs, avec, placed[0][1], sc_idx)
        arrived = _gather_ici_wait(w_send_sems, w_recv_sems, w_thru, proj)
        win_f, wa_f, wb_f, wo_f = _gather_sibling(arrived)
        return h, h_t, _proj_rest_call(h, win_f, proj, sc_idx), win_f, wa_f, wb_f, wo_f

    flight = {}

    def start_reduction(own, own16):
        sib = _reduce_sibling_send(own16)
        halves = [_add_half_call(t, own[t], sib[t], sc_idx, "chipsum_" + names[t]) for t in range(4)]
        send_sems, recv_sems, thru, token = _reduce_chips_start([hb for _, hb in halves])
        flight.update(sems=(send_sems, recv_sems), thru=thru, sums=[hf for hf, _ in halves])
        return token[0, 0]

    lb, lb_vjp = jax.vjp(_lower_bound_fn, hgrn_lb)
    grad_x, d_win, d_wa, d_wb, d_wo, pack = _local_step(
        x[0], loss_target[0], shift, scale, gate, norm_g, hgrn_onorm_g, rel_bias_t, lb[None, :],
        final_g[None, :], weights_fn, hook=start_reduction)

    rec = _reduce_chips_wait(*flight["sems"], flight["thru"], pack)
    pieces = [_final_piece_call(t, flight["sums"][t], rec[t], sc_idx, "piece_" + names[t]) for t in range(4)]
    g_win, g_wa, g_wb, g_wo = _share_sibling(pieces)

    packs = _allgather_small(pack, "gather_small").reshape(8, 16, D)
    tot = _sum8_call(packs)
    loss = tot[7, 0]
    g_b_ada = tot[0:3].reshape(1, 3 * D)
    g_norm_g = tot[3:4]
    g_final_g = tot[4]
    (g_hgrn_lb,) = lb_vjp(tot[5])
    g_onorm = tot[6:7, :HK]
    g_rel = tot[8, :NBUCKETS * NH].reshape(NBUCKETS, NH)

    def rows_of(a):
        flat = a.reshape(-1)
        n = -(-flat.shape[0] // D)
        return jnp.pad(flat, (0, n * D - flat.shape[0])).reshape(n, D)

    smalls = [(b_ada, g_b_ada, m_b_ada, v_b_ada), (norm_g, g_norm_g, m_norm_g, v_norm_g),
              (hgrn_onorm_g, g_onorm, m_hgrn_onorm_g, v_hgrn_onorm_g), (rel_bias, g_rel, m_rel_bias, v_rel_bias),
              (hgrn_lb, g_hgrn_lb, m_hgrn_lb, v_hgrn_lb), (final_g, g_final_g, m_final_g, v_final_g)]
    cat = [jnp.concatenate([rows_of(s[k]) for s in smalls], 0) for k in range(4)]
    cat = [jnp.pad(a, ((0, 16 - a.shape[0]), (0, 0))) for a in cat]
    _, sd, sm, sv = _adamw_call(*cat, "adamw_small")

    def unpack(packed):
        res, r = [], 0
        for s in smalls:
            n = -(-s[0].size // D)
            res.append(packed[r:r + n].reshape(-1)[:s[0].size].reshape(s[0].shape))
            r += n
        return res

    d_small, m_small, v_small = unpack(sd), unpack(sm), unpack(sv)

    sc_all = c_all * jax.nn.sigmoid(c_all)
    dmod_all = packs[:, 0:3].reshape(8, 3 * D)
    dm_s = lax.dynamic_slice(dmod_all, (0, chip * 768), (8, 768))
    g_w_ada, d_w_ada, nm_w_ada, nv_w_ada = _ada_update_call(sc_all.T, dm_s, w_ada[0], m_w_ada[0], v_w_ada[0])

    big = []
    for w, g, m, v, n in [(w_in, g_win, m_w_in, v_w_in, "w_in"), (w_branch_a, g_wa, m_w_branch_a, v_w_branch_a, "w_a"),
                          (w_branch_b, g_wb, m_w_branch_b, v_w_branch_b, "w_b"), (w_out, g_wo, m_w_out, v_w_out, "w_o")]:
        big.append(_adamw_call(w[0], g, m[0], v[0], "adamw_" + n))

    e = lambda a: a[None]
    grads = [e(g_w_ada), g_b_ada, g_norm_g, e(big[0][0]), g_onorm, e(big[1][0]), e(big[2][0]), e(big[3][0]),
             g_rel, g_hgrn_lb, g_final_g]
    deltas = [e(d_w_ada), d_small[0], d_small[1], e(big[0][1]), d_small[2], e(big[1][1]), e(big[2][1]), e(big[3][1]),
              d_small[3], d_small[4], d_small[5]]
    new_m = [e(nm_w_ada), m_small[0], m_small[1], e(big[0][2]), m_small[2], e(big[1][2]), e(big[2][2]), e(big[3][2]),
             m_small[3], m_small[4], m_small[5]]
    new_v = [e(nv_w_ada), v_small[0], v_small[1], e(big[0][3]), v_small[2], e(big[1][3]), e(big[2][3]), e(big[3][3]),
             v_small[3], v_small[4], v_small[5]]
    return (loss, grad_x[None], *grads, *deltas, *new_m, *new_v)
```

```python
import math

import numpy as np
import jax
import jax.numpy as jnp
from jax import lax
from jax.experimental import pallas as pl
from jax.experimental.pallas import tpu as pltpu

D = 1024
AW = 512
NH = 8
HE = 64
HK = 128
NPROJ = 8192
ABLK = 128
PATTERNS = (1, 4, 16)
NBUCKETS = 32
MAXDIST = 2048
NEG = -1e30
EPS = 1e-6
CH = 64
LR, B1, B2, AEPS, WD, STEP = 0.001, 0.9, 0.999, 1e-08, 0.01, 10

F32 = jnp.float32
BF16 = jnp.bfloat16
MESH = pl.DeviceIdType.MESH
VMEM_LIMIT = 56 * 1024 * 1024


def _cp(**kw):
    return pltpu.CompilerParams(vmem_limit_bytes=VMEM_LIMIT, **kw)


def _sig(x):
    return 0.5 * jnp.tanh(0.5 * x) + 0.5


def _nt(a, b):
    return lax.dot_general(a, b, (((1,), (1,)), ((), ())), preferred_element_type=F32)


def _tn(a, b):
    return lax.dot_general(a, b, (((0,), (0,)), ((), ())), preferred_element_type=F32)


def _nn(a, b):
    return jnp.dot(a, b, preferred_element_type=F32)


def _split2(x):
    h = x.astype(BF16)
    return h, (x - h.astype(F32)).astype(BF16)


def _exact_mm(tri_bf16, x):
    h, l = _split2(x)
    return _nn(tri_bf16, h) + _nn(tri_bf16, l)


def _exact_mm_r(x, ones_bf16):
    h, l = _split2(x)
    return _nn(h, ones_bf16) + _nn(l, ones_bf16)


def _proj_own_call(x, avec, w_own, sc):
    S = x.shape[0]
    tm, tn = 512, 2048

    def body(s_ref, x_ref, a_ref, w_ref, o_ref, h_ref, ht_ref):
        xv = x_ref[...]
        r = lax.rsqrt(jnp.mean(xv * xv, axis=-1, keepdims=True) + EPS)
        hv = xv * r * a_ref[0:1, :] + a_ref[1:2, :]
        hb = hv.astype(BF16)
        h_ref[...] = hb
        ht_ref[...] = hv.T.astype(BF16)
        o_ref[...] = _nn(hb, w_ref[...])

    return pl.pallas_call(
        body, name="in_proj_own",
        grid_spec=pltpu.PrefetchScalarGridSpec(
            num_scalar_prefetch=1, grid=(S // tm,),
            in_specs=[pl.BlockSpec((tm, D), lambda i, s: (i, 0)), pl.BlockSpec((8, D), lambda i, s: (0, 0)),
                      pl.BlockSpec((D, tn), lambda i, s: (0, 0))],
            out_specs=[pl.BlockSpec((tm, tn), lambda i, s: (i, s[1])), pl.BlockSpec((tm, D), lambda i, s: (i, 0)),
                       pl.BlockSpec((D, tm), lambda i, s: (0, i))]),
        out_shape=[jax.ShapeDtypeStruct((S, NPROJ), F32), jax.ShapeDtypeStruct((S, D), BF16),
                   jax.ShapeDtypeStruct((D, S), BF16)],
        compiler_params=_cp(),
    )(sc, x, avec, w_own)


def _proj_rest_call(h, w_in, proj, sc):
    S = h.shape[0]
    tm, tn = 512, 2048

    def body(s_ref, h_ref, w_ref, p_ref, o_ref):
        o_ref[...] = _nn(h_ref[...], w_ref[...])

    col = lambda j, s: (s[1] + 1 + j) % 4
    return pl.pallas_call(
        body, name="in_proj_rest",
        grid_spec=pltpu.PrefetchScalarGridSpec(
            num_scalar_prefetch=1, grid=(3, S // tm),
            in_specs=[pl.BlockSpec((tm, D), lambda j, i, s: (i, 0)),
                      pl.BlockSpec((D, tn), lambda j, i, s: (0, col(j, s))),
                      pl.BlockSpec(memory_space=pl.ANY)],
            out_specs=pl.BlockSpec((tm, tn), lambda j, i, s: (i, col(j, s)))),
        out_shape=jax.ShapeDtypeStruct((S, NPROJ), F32), input_output_aliases={3: 0}, compiler_params=_cp(),
    )(sc, h, w_in, proj)


def _t5_bucket_np(dist):
    max_exact = NBUCKETS // 2
    n = dist.astype(np.float32)
    large = max_exact + (np.log(np.maximum(n, np.float32(1.0)) / np.float32(max_exact))
                         / np.float32(math.log(MAXDIST / max_exact))
                         * np.float32(NBUCKETS - max_exact)).astype(np.int32)
    large = np.minimum(large, NBUCKETS - 1)
    return np.where(dist < max_exact, dist, large)


def _band_bucket(d):
    qi = np.arange(ABLK)[:, None]
    kj = np.arange(2 * ABLK)[None, :]
    delta = qi + ABLK - kj
    band = (delta >= 0) & (delta <= ABLK)
    bucket = _t5_bucket_np(np.clip(delta, 0, None) * d)
    return band, bucket


def _bias_tiles(rel_bias):
    tiles = []
    for d in PATTERNS:
        band, bucket = _band_bucket(d)
        onehot = (jnp.asarray(bucket, jnp.int32)[None] == jnp.arange(NBUCKETS, dtype=jnp.int32)[:, None, None])
        bias = jnp.einsum("bqk,bh->hqk", onehot.astype(F32), rel_bias, precision=lax.Precision.HIGHEST)
        tiles.append(jnp.where(jnp.asarray(band)[None], bias, NEG))
    return jnp.stack(tiles, 0)


ATT = 2048
HP = 2 * HE
LOG2E = 1.4426950408889634
LN2 = 0.6931471805599453
QSCALE2 = (HE ** -0.5) * LOG2E
AGRP = 8
AGRP_B = 4


def _attn_blocks():
    out = []
    for p, d in enumerate(PATTERNS):
        for r in range(d):
            for n in range(ATT // (d * ABLK)):
                out.append((p, d, r, n))
    return out


def _attn_fwd_call(proj, biases):
    S = proj.shape[0]
    nt = S // ATT

    def body(q_ref, k_ref, v_ref, z_ref, b_ref, a_ref, l_ref, oa_ref, kc, vc, op, lp):
        i = pl.program_id(1)

        @pl.when(i == 0)
        def _():
            kc[0:ATT] = jnp.zeros((ATT, HP), F32)
            vc[0:ATT] = jnp.zeros((ATT, HP), F32)

        @pl.when(i > 0)
        def _():
            kc[0:ATT] = kc[ATT:2 * ATT]
            vc[0:ATT] = vc[ATT:2 * ATT]

        kc[ATT:2 * ATT] = k_ref[...]
        vc[ATT:2 * ATT] = v_ref[...]
        col = lax.broadcasted_iota(jnp.int32, (ABLK, 2 * ABLK), 1)
        dead = jnp.logical_and(i == 0, col < ABLK)
        blocks = _attn_blocks()
        hs = (slice(0, HE), slice(HE, 2 * HE))
        for g0 in range(0, len(blocks), AGRP):
            grp = blocks[g0:g0 + AGRP]
            qrows = [pl.ds(n * ABLK * d + r, ABLK, stride=d) for p, d, r, n in grp]
            krows = [pl.ds(ATT + (n - 1) * ABLK * d + r, 2 * ABLK, stride=d) for p, d, r, n in grp]
            qs = [(q_ref[qr, :] * QSCALE2).astype(BF16) for qr in qrows]
            ks = [kc[kr, :].astype(BF16) for kr in krows]
            vs = [vc[kr, :].astype(BF16) for kr in krows]
            ss = [[_nt(qs[b][:, sl], ks[b][:, sl]) + b_ref[grp[b][0], e] for e, sl in enumerate(hs)]
                  for b in range(len(grp))]
            ss = [[jnp.where(dead, NEG, s) if grp[b][3] == 0 else s for s in ss[b]] for b in range(len(grp))]
            mxs = [[jnp.max(s, axis=-1, keepdims=True) for s in sb] for sb in ss]
            pes = [[jnp.exp2(s - mx) for s, mx in zip(sb, mb)] for sb, mb in zip(ss, mxs)]
            dens = [[jnp.sum(pe, axis=-1, keepdims=True) for pe in pb] for pb in pes]
            pvs = [[_nn(pe.astype(BF16), vs[b][:, sl]) for pe, sl in zip(pes[b], hs)] for b in range(len(grp))]
            for b in range(len(grp)):
                p, d, r, n = grp[b]
                prow = pl.ds(p * ATT + n * ABLK * d + r, ABLK, stride=d)
                lp[prow, :] = jnp.concatenate(
                    [jnp.broadcast_to(mx + jnp.log2(dn), (ABLK, HE)) for mx, dn in zip(mxs[b], dens[b])], axis=1)
                op[prow, :] = jnp.concatenate([pv / dn for pv, dn in zip(pvs[b], dens[b])], axis=1)
        rt = 256
        for t in range(ATT // rt):
            rows = slice(t * rt, (t + 1) * rt)
            pr = [slice(p * ATT + t * rt, p * ATT + (t + 1) * rt) for p in range(3)]
            la, lb_, lc = lp[pr[0], :], lp[pr[1], :], lp[pr[2], :]
            m = jnp.maximum(jnp.maximum(la, lb_), lc)
            ea, eb, ec = jnp.exp2(la - m), jnp.exp2(lb_ - m), jnp.exp2(lc - m)
            den = ea + eb + ec
            att = (ea * op[pr[0], :] + eb * op[pr[1], :] + ec * op[pr[2], :]) / den
            a_ref[rows, :] = att
            l_ref[rows, :] = m + jnp.log2(den)
            z = z_ref[rows, :]
            oa_ref[rows, :] = (att * (z * _sig(z))).astype(BF16)

    def pcol(c):
        return pl.BlockSpec((ATT, HP), lambda h, i: (i, c * 4 + h))

    out = pl.BlockSpec((ATT, HP), lambda h, i: (i, h))
    return pl.pallas_call(
        body, name="attn_fwd", grid=(4, nt),
        in_specs=[pcol(0), pcol(1), pcol(2), pcol(3),
                  pl.BlockSpec((3, 2, ABLK, 2 * ABLK), lambda h, i: (0, h, 0, 0))],
        out_specs=[out, out, out],
        out_shape=[jax.ShapeDtypeStruct((S, AW), F32), jax.ShapeDtypeStruct((S, AW), F32),
                   jax.ShapeDtypeStruct((S, AW), BF16)],
        scratch_shapes=[pltpu.VMEM((2 * ATT, HP), F32), pltpu.VMEM((2 * ATT, HP), F32),
                        pltpu.VMEM((3 * ATT, HP), F32), pltpu.VMEM((3 * ATT, HP), F32)],
        compiler_params=_cp(),
    )(proj, proj, proj, proj, biases)


def _attn_bwd_call(proj, dattn, lse, dsum, biases):
    S = proj.shape[0]
    nt = S // ATT

    def body(q_ref, k_ref, v_ref, do_ref, l_ref, ds_ref, b_ref, dq_ref, dk_ref, dv_ref, db_ref,
             kc, vc, dkc, dvc, dqa):
        i = pl.program_id(1)

        @pl.when(i == 0)
        def _():
            kc[ATT:2 * ATT] = jnp.zeros((ATT, HP), F32)
            vc[ATT:2 * ATT] = jnp.zeros((ATT, HP), F32)
            dkc[ATT:2 * ATT] = jnp.zeros((ATT, HP), F32)
            dvc[ATT:2 * ATT] = jnp.zeros((ATT, HP), F32)
            db_ref[...] = jnp.zeros_like(db_ref)

        @pl.when(i < nt)
        def _():
            kc[0:ATT] = kc[ATT:2 * ATT]
            vc[0:ATT] = vc[ATT:2 * ATT]
            dkc[0:ATT] = dkc[ATT:2 * ATT]
            dvc[0:ATT] = dvc[ATT:2 * ATT]
            kc[ATT:2 * ATT] = k_ref[...]
            vc[ATT:2 * ATT] = v_ref[...]
            dkc[ATT:2 * ATT] = jnp.zeros((ATT, HP), F32)
            dvc[ATT:2 * ATT] = jnp.zeros((ATT, HP), F32)
            col = lax.broadcasted_iota(jnp.int32, (ABLK, 2 * ABLK), 1)
            dead = jnp.logical_and(i == 0, col < ABLK)
            blocks = _attn_blocks()
            hs = (slice(0, HE), slice(HE, 2 * HE))
            for g0 in range(0, len(blocks), AGRP_B):
                grp = blocks[g0:g0 + AGRP_B]
                nb_ = range(len(grp))
                qrows = [pl.ds(n * ABLK * d + r, ABLK, stride=d) for p, d, r, n in grp]
                krows = [pl.ds(ATT + (n - 1) * ABLK * d + r, 2 * ABLK, stride=d) for p, d, r, n in grp]
                qs = [(q_ref[qr, :] * QSCALE2).astype(BF16) for qr in qrows]
                ks = [kc[kr, :].astype(BF16) for kr in krows]
                vs = [vc[kr, :].astype(BF16) for kr in krows]
                dos = [do_ref[qr, :].astype(BF16) for qr in qrows]
                lvs = [l_ref[qr, :] for qr in qrows]
                dsvs = [ds_ref[qr, :] for qr in qrows]
                ss = [[_nt(qs[b][:, sl], ks[b][:, sl]) + b_ref[grp[b][0], e] for e, sl in enumerate(hs)] for b in nb_]
                ss = [[jnp.where(dead, NEG, s) if grp[b][3] == 0 else s for s in ss[b]] for b in nb_]
                dps = [[_nt(dos[b][:, sl], vs[b][:, sl]) for sl in hs] for b in nb_]
                pes = [[jnp.exp2(ss[b][e] - lvs[b][:, e * HE:e * HE + 1]) for e in range(2)] for b in nb_]
                dscs = [[pes[b][e] * (dps[b][e] - dsvs[b][:, e * HE:e * HE + 1]) for e in range(2)] for b in nb_]
                for e in range(2):
                    tot = dscs[0][e]
                    for b in range(1, len(grp)):
                        tot = tot + dscs[b][e]
                    db_ref[grp[0][0], e] += tot
                dsbs = [[t.astype(BF16) for t in tb] for tb in dscs]
                dqs = [[_nn(dsbs[b][e], ks[b][:, sl]) * (HE ** -0.5) for e, sl in enumerate(hs)] for b in nb_]
                dks = [[_tn(dsbs[b][e], qs[b][:, sl]) for e, sl in enumerate(hs)] for b in nb_]
                dvs = [[_tn(pes[b][e].astype(BF16), dos[b][:, sl]) for e, sl in enumerate(hs)] for b in nb_]
                for b in nb_:
                    dq = jnp.concatenate(dqs[b], axis=1)
                    if grp[b][0] == 0:
                        dqa[qrows[b], :] = dq
                    else:
                        dqa[qrows[b], :] += dq
                    dkc[krows[b], :] += jnp.concatenate(dks[b], axis=1)
                    dvc[krows[b], :] += jnp.concatenate(dvs[b], axis=1)
            dq_ref[...] = dqa[...].astype(BF16)
            dk_ref[...] = (dkc[0:ATT] * LN2).astype(BF16)
            dv_ref[...] = dvc[0:ATT].astype(BF16)

        @pl.when(i == nt)
        def _():
            dk_ref[...] = (dkc[ATT:2 * ATT] * LN2).astype(BF16)
            dv_ref[...] = dvc[ATT:2 * ATT].astype(BF16)

    def pcol(c):
        return pl.BlockSpec((ATT, HP), lambda h, i: (jnp.minimum(i, nt - 1), c * 4 + h))

    qrow = pl.BlockSpec((ATT, HP), lambda h, i: (jnp.minimum(i, nt - 1), h))
    krow = pl.BlockSpec((ATT, HP), lambda h, i: (jnp.maximum(i - 1, 0), h))
    bspec = pl.BlockSpec((3, 2, ABLK, 2 * ABLK), lambda h, i: (0, h, 0, 0))
    return pl.pallas_call(
        body, name="attn_bwd", grid=(4, nt + 1),
        in_specs=[pcol(0), pcol(1), pcol(2), qrow, qrow, qrow, bspec],
        out_specs=[qrow, krow, krow, bspec],
        out_shape=[jax.ShapeDtypeStruct((S, AW), BF16)] * 3
                  + [jax.ShapeDtypeStruct((3, NH, ABLK, 2 * ABLK), F32)],
        scratch_shapes=[pltpu.VMEM((2 * ATT, HP), F32)] * 4 + [pltpu.VMEM((ATT, HP), F32)],
        compiler_params=_cp(),
    )(proj, proj, proj, dattn, lse, dsum, biases)


HRB = 256


def _tri_masks():
    row = lax.broadcasted_iota(jnp.int32, (CH, CH), 0)
    col = lax.broadcasted_iota(jnp.int32, (CH, CH), 1)
    return row >= col


def _heads():
    return [slice(hh * HK, (hh + 1) * HK) for hh in range(NH)]


def _hgrn_gate_heads(q_ref, f_ref, rows, lbv):
    out = []
    for sl in _heads():
        qraw = q_ref[rows, sl]
        sq = _sig(qraw)
        sf = _sig(f_ref[rows, sl])
        f = lbv[:, sl] + (1.0 - lbv[:, sl]) * sf
        out.append(dict(qraw=qraw, sq=sq, q=qraw * sq, sf=sf, f=f, k=1.0 - f, lg=jnp.log(f)))
    return out


def _hgrn_decay_heads(gh, b):
    bl = b[CH - 1:CH, :]
    bm = b[CH // 2 - 1:CH // 2, :]
    ebm = jnp.exp(bm)
    eblm = jnp.exp(bl - bm)
    ebl = jnp.exp(bl)
    out = []
    for g, sl in zip(gh, _heads()):
        d = b[:, sl] - bm[:, sl]
        e1 = jnp.exp(d)
        e2 = jnp.exp(-d)
        qs = g["q"] * e1
        ks = g["k"] * e2
        qe = qs * ebm[:, sl]
        kd = ks * eblm[:, sl]
        out.append(dict(e1=e1, e2=e2, qe=qe, kd=kd, ebm=ebm[:, sl], eblm=eblm[:, sl], ebl=ebl[:, sl],
                        qsb=qs.astype(BF16), ksb=ks.astype(BF16), qeb=qe.astype(BF16), kdb=kd.astype(BF16)))
    return out


def _hgrn_fwd_call(proj, lb, gn):
    S = proj.shape[0]
    nc = S // CH
    cps = HRB // CH

    def body(q_ref, f_ref, i_ref, z_ref, lb_ref, gn_ref, or_ref, ob_ref, st_ref, st):
        @pl.when(pl.program_id(0) == 0)
        def _():
            st[...] = jnp.zeros_like(st)

        low = _tri_masks()
        tri = low.astype(BF16)
        lbv = lb_ref[...]
        hs = _heads()
        for ci in range(cps):
            rows = slice(ci * CH, (ci + 1) * CH)
            gh = _hgrn_gate_heads(q_ref, f_ref, rows, lbv)
            b = _exact_mm(tri, jnp.concatenate([g["lg"] for g in gh], axis=1))
            dh = _hgrn_decay_heads(gh, b)
            vbs = [i_ref[rows, sl].astype(BF16) for sl in hs]
            st_ref[ci] = st[...]
            s0s = [st[sl, :] for sl in hs]
            as_ = [_nt(d["qsb"], d["ksb"]) for d in dh]
            ois = [_nt(d["qeb"], s0.astype(BF16)) for d, s0 in zip(dh, s0s)]
            sts = [_tn(vb, d["kdb"]) for vb, d in zip(vbs, dh)]
            abs_ = [jnp.where(low, a, 0.0).astype(BF16) for a in as_]
            os_ = [oi + _nn(a, vb) for oi, a, vb in zip(ois, abs_, vbs)]
            for sl, s0, sn, d, o in zip(hs, s0s, sts, dh, os_):
                st[sl, :] = s0 * d["ebl"] + sn
                or_ref[rows, sl] = o
                r = lax.rsqrt(jnp.mean(o * o, axis=-1, keepdims=True) + EPS)
                z = z_ref[rows, sl]
                ob_ref[rows, sl] = (o * r * gn_ref[:, sl] * (z * _sig(z))).astype(BF16)

    def pcol(c):
        return pl.BlockSpec((HRB, D), lambda i: (i, c))

    vec = pl.BlockSpec((1, D), lambda i: (0, 0))
    row = pl.BlockSpec((HRB, D), lambda i: (i, 0))
    return pl.pallas_call(
        body, name="hgrn_fwd", grid=(S // HRB,),
        in_specs=[pcol(2), pcol(3), pcol(4), pcol(5), vec, vec],
        out_specs=[row, row, pl.BlockSpec((cps, NH * HK, HK), lambda i: (i, 0, 0))],
        out_shape=[jax.ShapeDtypeStruct((S, D), F32), jax.ShapeDtypeStruct((S, D), BF16),
                   jax.ShapeDtypeStruct((nc, NH * HK, HK), F32)],
        scratch_shapes=[pltpu.VMEM((NH * HK, HK), F32)],
        compiler_params=_cp(),
    )(proj, proj, proj, proj, lb, gn)


def _hgrn_bwd_call(proj, oraw, dob, states, lb, gn):
    S = proj.shape[0]
    nblk = S // HRB
    cps = HRB // CH

    def body(q_ref, f_ref, i_ref, z_ref, or_ref, dob_ref, st_ref, lb_ref, gn_ref, dh_ref, acc_ref, dst):
        @pl.when(pl.program_id(0) == 0)
        def _():
            dst[...] = jnp.zeros_like(dst)
            acc_ref[...] = jnp.zeros_like(acc_ref)

        low = _tri_masks()
        tri = low.astype(BF16)
        triu = jnp.logical_not(_tri_masks()) | (lax.broadcasted_iota(jnp.int32, (CH, CH), 0)
                                               == lax.broadcasted_iota(jnp.int32, (CH, CH), 1))
        triu = triu.astype(BF16)
        lbv = lb_ref[...]
        hs = _heads()
        for ci in reversed(range(cps)):
            rows = slice(ci * CH, (ci + 1) * CH)
            dobs, dgns = [], []
            for sl in hs:
                o = or_ref[rows, sl]
                z = z_ref[rows, sl]
                sz = _sig(z)
                gnv = gn_ref[:, sl]
                dobv = dob_ref[rows, sl]
                r = lax.rsqrt(jnp.mean(o * o, axis=-1, keepdims=True) + EPS)
                onr = o * r
                don = dobv * (z * sz)
                dh_ref[rows, 3 * D + sl.start:3 * D + sl.stop] = (
                    dobv * (onr * gnv) * (sz * (1.0 + z * (1.0 - sz)))).astype(BF16)
                dgns.append(jnp.sum(don * onr, axis=0, keepdims=True))
                gh_ = don * gnv
                dobs.append((r * (gh_ - onr * jnp.mean(gh_ * onr, axis=-1, keepdims=True))).astype(BF16))
            acc_ref[1:2, :] += jnp.concatenate(dgns, axis=1)

            gh = _hgrn_gate_heads(q_ref, f_ref, rows, lbv)
            b = _exact_mm(tri, jnp.concatenate([g["lg"] for g in gh], axis=1))
            dh = _hgrn_decay_heads(gh, b)
            vbs = [i_ref[rows, sl].astype(BF16) for sl in hs]

            st0s = [st_ref[ci, sl, :] for sl in hs]
            dst1s = [dst[sl, :] for sl in hs]
            dst1bs = [t.astype(BF16) for t in dst1s]
            as_ = [_nt(d["qsb"], d["ksb"]) for d in dh]
            das_ = [_nt(do, vb) for do, vb in zip(dobs, vbs)]
            dqes = [_nn(do, s0.astype(BF16)) for do, s0 in zip(dobs, st0s)]
            dkds = [_nn(vb, d1) for vb, d1 in zip(vbs, dst1bs)]
            dvis = [_nt(d["kdb"], d1) for d, d1 in zip(dh, dst1bs)]
            dsts = [_tn(do, d["qeb"]) for do, d in zip(dobs, dh)]
            abs_ = [jnp.where(low, a, 0.0).astype(BF16) for a in as_]
            dabs_ = [jnp.where(low, a, 0.0).astype(BF16) for a in das_]
            dqss = [_nn(da, d["ksb"]) for da, d in zip(dabs_, dh)]
            dkss = [_tn(da, d["qsb"]) for da, d in zip(dabs_, dh)]
            dvs_ = [_tn(a, do) + dvi for a, do, dvi in zip(abs_, dobs, dvis)]

            dqs_, dks_, dbs_, exs_ = [], [], [], []
            for hh, sl in enumerate(hs):
                d, d1, s0 = dh[hh], dst1s[hh], st0s[hh]
                dqe, dqs, dks, dkd = dqes[hh], dqss[hh], dkss[hh], dkds[hh]
                dst[sl, :] = dsts[hh] + d1 * d["ebl"]
                dh_ref[rows, 2 * D + sl.start:2 * D + sl.stop] = dvs_[hh].astype(BF16)
                dqs_.append((dqe * d["ebm"] + dqs) * d["e1"])
                dks_.append((dks + dkd * d["eblm"]) * d["e2"])
                dkdkd = dkd * d["kd"]
                dbs_.append(dqe * d["qe"] + dqs * d["qsb"].astype(F32) - dks * d["ksb"].astype(F32) - dkdkd)
                exs_.append(jnp.sum(dkdkd, axis=0, keepdims=True)
                            + jnp.sum(d1 * s0, axis=0, keepdims=True) * d["ebl"])
            dg = _exact_mm(triu, jnp.concatenate(dbs_, axis=1)) + jnp.concatenate(exs_, axis=1)

            dlbs = []
            for hh, sl in enumerate(hs):
                g = gh[hh]
                df = dg[:, sl] / g["f"] - dks_[hh]
                sf = g["sf"]
                omsf = 1.0 - sf
                sq = g["sq"]
                dlbs.append(jnp.sum(df * omsf, axis=0, keepdims=True))
                dh_ref[rows, sl] = (dqs_[hh] * (sq * (1.0 + g["qraw"] * (1.0 - sq)))).astype(BF16)
                dh_ref[rows, D + sl.start:D + sl.stop] = (df * (1.0 - lbv[:, sl]) * sf * omsf).astype(BF16)
            acc_ref[0:1, :] += jnp.concatenate(dlbs, axis=1)

    def pcol(c):
        return pl.BlockSpec((HRB, D), lambda i: (nblk - 1 - i, c))

    vec = pl.BlockSpec((1, D), lambda i: (0, 0))
    row = pl.BlockSpec((HRB, D), lambda i: (nblk - 1 - i, 0))
    return pl.pallas_call(
        body, name="hgrn_bwd", grid=(nblk,),
        in_specs=[pcol(2), pcol(3), pcol(4), pcol(5), row, row,
                  pl.BlockSpec((cps, NH * HK, HK), lambda i: (nblk - 1 - i, 0, 0)), vec, vec],
        out_specs=[pl.BlockSpec((HRB, 4 * D), lambda i: (nblk - 1 - i, 0)),
                   pl.BlockSpec((8, D), lambda i: (0, 0))],
        out_shape=[jax.ShapeDtypeStruct((S, 4 * D), BF16), jax.ShapeDtypeStruct((8, D), F32)],
        scratch_shapes=[pltpu.VMEM((NH * HK, HK), F32)],
        compiler_params=_cp(),
    )(proj, proj, proj, proj, oraw, dob, states, lb, gn)


def _merge_call(oa, ob, proj, x, tgt, vecs, attn, wa, wb, wo, hsum):
    S = x.shape[0]
    tm = 256

    def body(oa_ref, ob_ref, ga_ref, gb_ref, x_ref, t_ref, v_ref, at_ref, za_ref, wa_ref, wb_ref, wo_ref, hs_ref,
             y_ref, dx2_ref, du_ref, dya_ref, dyb_ref, dg_ref, dat_ref, dsum_ref, dza_ref, dob_ref, acc_ref, ls_ref):
        @pl.when(pl.program_id(0) == 0)
        def _():
            acc_ref[...] = jnp.zeros_like(acc_ref)
            ls_ref[...] = jnp.zeros_like(ls_ref)

        gate = v_ref[0:1, :]
        fg = v_ref[1:2, :]
        ya = _nn(oa_ref[...], wa_ref[...])
        yb = _nn(ob_ref[...], wb_ref[...])
        sa = _sig(ga_ref[...])
        sb = _sig(gb_ref[...])
        y = (sa * ya + sb * yb).astype(BF16)
        y_ref[...] = y
        u = _nn(y, wo_ref[...])
        x2v = x_ref[...] + gate * u
        r = lax.rsqrt(jnp.mean(x2v * x2v, axis=-1, keepdims=True) + EPS)
        err = x2v * r * fg - t_ref[...]
        ls_ref[...] += jnp.sum(err * err)
        dout = err * (1.0 / D)
        gh = dout * fg
        dx2 = r * gh - x2v * (r * r * r * jnp.mean(gh * x2v, axis=-1, keepdims=True))
        acc_ref[0:1, :] += jnp.sum(dx2 * u, axis=0, keepdims=True)
        acc_ref[1:2, :] += jnp.sum(dout * x2v * r, axis=0, keepdims=True)
        dx2_ref[...] = dx2
        du = (dx2 * gate).astype(BF16)
        du_ref[...] = du
        dy = _nt(du, wo_ref[...])
        dya = (dy * sa).astype(BF16)
        dyb = (dy * sb).astype(BF16)
        dya_ref[...] = dya
        dyb_ref[...] = dyb
        dg_ref[:, 0:D] = (dy * ya * sa * (1.0 - sa)).astype(BF16)
        dg_ref[:, D:2 * D] = (dy * yb * sb * (1.0 - sb)).astype(BF16)
        doa = _nt(dya, wa_ref[...])
        dob_ref[...] = _nt(dyb, wb_ref[...])
        za = za_ref[...]
        sz = _sig(za)
        att = at_ref[...]
        dat = doa * (za * sz)
        dat_ref[...] = dat
        dza_ref[...] = (doa * att * (sz * (1.0 + za * (1.0 - sz)))).astype(BF16)
        dsum_ref[...] = _exact_mm_r(dat * att, hs_ref[...])

    row = pl.BlockSpec((tm, D), lambda i: (i, 0))
    arow = pl.BlockSpec((tm, AW), lambda i: (i, 0))
    full = lambda a: pl.BlockSpec(a.shape, lambda i: (0, 0))
    return pl.pallas_call(
        body, name="merge_fwd_bwd", grid=(S // tm,),
        in_specs=[arow, row, pl.BlockSpec((tm, D), lambda i: (i, 6)), pl.BlockSpec((tm, D), lambda i: (i, 7)),
                  row, row, pl.BlockSpec((8, D), lambda i: (0, 0)), arow, pl.BlockSpec((tm, AW), lambda i: (i, 3)),
                  full(wa), full(wb), full(wo), full(hsum)],
        out_specs=[row, row, row, row, row, pl.BlockSpec((tm, 2 * D), lambda i: (i, 0)),
                   arow, arow, arow, row, pl.BlockSpec((8, D), lambda i: (0, 0)),
                   pl.BlockSpec((8, 128), lambda i: (0, 0))],
        out_shape=[jax.ShapeDtypeStruct((S, D), BF16), jax.ShapeDtypeStruct((S, D), F32),
                   jax.ShapeDtypeStruct((S, D), BF16), jax.ShapeDtypeStruct((S, D), BF16),
                   jax.ShapeDtypeStruct((S, D), BF16), jax.ShapeDtypeStruct((S, 2 * D), BF16),
                   jax.ShapeDtypeStruct((S, AW), F32), jax.ShapeDtypeStruct((S, AW), F32),
                   jax.ShapeDtypeStruct((S, AW), BF16), jax.ShapeDtypeStruct((S, D), F32),
                   jax.ShapeDtypeStruct((8, D), F32), jax.ShapeDtypeStruct((8, 128), F32)],
        compiler_params=_cp(),
    )(oa, ob, proj, proj, x, tgt, vecs, attn, proj, wa, wb, wo, hsum)


def _atb_call(a, b, name):
    S, K = a.shape
    N = b.shape[1]
    tm = min(1024, S)

    def body(a_ref, b_ref, o_ref, ob_ref):
        @pl.when(pl.program_id(0) == 0)
        def _():
            o_ref[...] = jnp.zeros_like(o_ref)

        o_ref[...] += _tn(a_ref[...], b_ref[...])

        @pl.when(pl.program_id(0) == S // tm - 1)
        def _():
            ob_ref[...] = o_ref[...].astype(BF16)

    ospec = pl.BlockSpec((K, N), lambda i: (0, 0))
    return pl.pallas_call(
        body, name=name, grid=(S // tm,),
        in_specs=[pl.BlockSpec((tm, K), lambda i: (i, 0)), pl.BlockSpec((tm, N), lambda i: (i, 0))],
        out_specs=[ospec, ospec],
        out_shape=[jax.ShapeDtypeStruct((K, N), F32), jax.ShapeDtypeStruct((K, N), BF16)], compiler_params=_cp(),
    )(a, b)


def _dwin_call(h_t, dqkvz, d_hgrn, d_gates):
    S = h_t.shape[1]
    tm = 512
    tn = 2048

    def body(h_ref, q_ref, k_ref, v_ref, z_ref, m_ref, g_ref, o_ref, ob_ref):
        j = pl.program_id(0)

        @pl.when(pl.program_id(1) == 0)
        def _():
            o_ref[...] = jnp.zeros_like(o_ref)

        hv = h_ref[...]

        @pl.when(j == 0)
        def _():
            for cidx, r in enumerate((q_ref, k_ref, v_ref, z_ref)):
                o_ref[0, :, cidx * AW:(cidx + 1) * AW] += _nn(hv, r[...])

        @pl.when(jnp.logical_or(j == 1, j == 2))
        def _():
            o_ref[0] += _nn(hv, m_ref[...])

        @pl.when(j == 3)
        def _():
            o_ref[0] += _nn(hv, g_ref[...])

        @pl.when(pl.program_id(1) == S // tm - 1)
        def _():
            ob_ref[...] = o_ref[...].astype(BF16)

    aspec = pl.BlockSpec((tm, AW), lambda j, i: (jnp.where(j == 0, i, 0), 0))
    ospec = pl.BlockSpec((1, D, tn), lambda j, i: (j, 0, 0))
    return pl.pallas_call(
        body, name="dw_in", grid=(4, S // tm),
        in_specs=[pl.BlockSpec((D, tm), lambda j, i: (0, i)), aspec, aspec, aspec, aspec,
                  pl.BlockSpec((tm, tn), lambda j, i: (jnp.where(jnp.logical_or(j == 1, j == 2), i, 0),
                                                       jnp.where(j == 2, 1, 0))),
                  pl.BlockSpec((tm, tn), lambda j, i: (jnp.where(j == 3, i, 0), 0))],
        out_specs=[ospec, ospec],
        out_shape=[jax.ShapeDtypeStruct((4, D, tn), F32), jax.ShapeDtypeStruct((4, D, tn), BF16)],
        compiler_params=_cp(),
    )(h_t, *dqkvz, d_hgrn, d_gates)


def _dh_call(dqkvz, d_hgrn, d_gates, w_in, x, dx2, vecs):
    S = x.shape[0]
    tm = 512

    def body(q_ref, k_ref, v_ref, z_ref, m_ref, g_ref, w_ref, x_ref, dx2_ref, p_ref, gx_ref, acc_ref):
        @pl.when(pl.program_id(0) == 0)
        def _():
            acc_ref[...] = jnp.zeros_like(acc_ref)

        dhv = _nt(q_ref[...], w_ref[:, 0:AW])
        for cidx, r in enumerate((k_ref, v_ref, z_ref)):
            dhv += _nt(r[...], w_ref[:, (cidx + 1) * AW:(cidx + 2) * AW])
        dhv += _nt(m_ref[...], w_ref[:, 4 * AW:4 * AW + 4 * D])
        dhv += _nt(g_ref[...], w_ref[:, 4 * AW + 4 * D:NPROJ])
        xv = x_ref[...]
        r = lax.rsqrt(jnp.mean(xv * xv, axis=-1, keepdims=True) + EPS)
        xn = xv * r
        acc_ref[0:1, :] += jnp.sum(dhv, axis=0, keepdims=True)
        acc_ref[1:2, :] += jnp.sum(dhv * xn * p_ref[1:2, :], axis=0, keepdims=True)
        acc_ref[2:3, :] += jnp.sum(dhv * xn * p_ref[2:3, :], axis=0, keepdims=True)
        dxn = dhv * p_ref[0:1, :]
        gx_ref[...] = dx2_ref[...] + r * dxn - xv * (r * r * r * jnp.mean(dxn * xv, axis=-1, keepdims=True))

    row = pl.BlockSpec((tm, D), lambda i: (i, 0))
    aspec = pl.BlockSpec((tm, AW), lambda i: (i, 0))
    const = lambda shape: pl.BlockSpec(shape, lambda i: (0, 0))
    return pl.pallas_call(
        body, name="dh_gradx", grid=(S // tm,),
        in_specs=[aspec, aspec, aspec, aspec,
                  pl.BlockSpec((tm, 4 * D), lambda i: (i, 0)), pl.BlockSpec((tm, 2 * D), lambda i: (i, 0)),
                  pl.BlockSpec((D, NPROJ), lambda i: (0, 0), pipeline_mode=pl.Buffered(1)),
                  row, row, const((8, D))],
        out_specs=[row, const((8, D))],
        out_shape=[jax.ShapeDtypeStruct((S, D), F32), jax.ShapeDtypeStruct((8, D), F32)],
        compiler_params=_cp(),
    )(*dqkvz, d_hgrn, d_gates, w_in, x, dx2, vecs)


def _adamw_math(w, g, m, v):
    m = B1 * m + (1.0 - B1) * g
    v = B2 * v + (1.0 - B2) * (g * g)
    m_hat = m / (1.0 - B1 ** STEP)
    v_hat = v / (1.0 - B2 ** STEP)
    delta = -LR * (m_hat / (jnp.sqrt(v_hat) + AEPS) + WD * w)
    return delta, m, v


def _adamw_call(w, g, m, v, name):
    R, C = w.shape
    tr = R if R * C * 4 <= (1 << 20) else max(8, (1 << 20) // (C * 4))
    assert R % tr == 0

    def body(w_ref, g_ref, m_ref, v_ref, go_ref, d_ref, nm_ref, nv_ref):
        g = g_ref[...]
        go_ref[...] = g
        d_ref[...], nm_ref[...], nv_ref[...] = _adamw_math(w_ref[...], g, m_ref[...], v_ref[...])

    blk = pl.BlockSpec((tr, C), lambda i: (i, 0))
    return pl.pallas_call(
        body, name=name, grid=(R // tr,), in_specs=[blk] * 4, out_specs=[blk] * 4,
        out_shape=[jax.ShapeDtypeStruct((R, C), F32)] * 4, compiler_params=_cp(),
    )(w, g, m, v)


def _mod_call(c_all, w_ada_s, b_s):
    def body(c_ref, w_ref, b_ref, o_ref):
        cv = c_ref[...]
        sc = cv * _sig(cv)
        o_ref[...] = jnp.dot(sc, w_ref[...], preferred_element_type=F32,
                             precision=lax.Precision.HIGHEST) + b_ref[...]

    return pl.pallas_call(
        body, name="ada_mod", out_shape=jax.ShapeDtypeStruct((8, w_ada_s.shape[1]), F32),
        compiler_params=_cp(),
    )(c_all, w_ada_s, b_s)


def _ada_update_call(sct, dm, w, m, v):
    R, C = w.shape
    tr = 256

    def body(s_ref, d_ref, w_ref, m_ref, v_ref, g_ref, dl_ref, nm_ref, nv_ref):
        g = s_ref[:, 0:1] * d_ref[0:1, :]
        for b in range(1, 8):
            g = g + s_ref[:, b:b + 1] * d_ref[b:b + 1, :]
        g_ref[...] = g
        dl_ref[...], nm_ref[...], nv_ref[...] = _adamw_math(w_ref[...], g, m_ref[...], v_ref[...])

    blk = pl.BlockSpec((tr, C), lambda i: (i, 0))
    return pl.pallas_call(
        body, name="ada_update", grid=(R // tr,),
        in_specs=[pl.BlockSpec((tr, 8), lambda i: (i, 0)), pl.BlockSpec((8, C), lambda i: (0, 0)), blk, blk, blk],
        out_specs=[blk] * 4, out_shape=[jax.ShapeDtypeStruct((R, C), F32)] * 4, compiler_params=_cp(),
    )(sct, dm, w, m, v)


def _sum8_call(packs):
    def body(p_ref, o_ref):
        acc = p_ref[0]
        for k in range(1, 8):
            acc = acc + p_ref[k]
        o_ref[...] = acc

    return pl.pallas_call(
        body, name="sum_small", out_shape=jax.ShapeDtypeStruct(packs.shape[1:], F32), compiler_params=_cp(),
    )(packs)


def _local_step(x, tgt, shift, scale, gate, norm_g, hgrn_onorm_g, rel_bias, lb, final_g, weights_fn, hook=None):
    a = norm_g * (1.0 + scale)
    z6 = jnp.zeros((6, D), F32)
    h, h_t, proj, w_in, rest_fn = weights_fn(x, jnp.concatenate([a, shift, z6], 0))

    biases = _bias_tiles(rel_bias) * LOG2E
    attn, lse, oa = _attn_fwd_call(proj, biases)

    gn = jnp.tile(hgrn_onorm_g, (1, NH))
    oraw, ob, states = _hgrn_fwd_call(proj, lb, gn)

    wa, wb, wo = rest_fn(ob)
    vecs2 = jnp.concatenate([gate, final_g, z6], 0)
    hsum = jnp.asarray(np.kron(np.eye(NH), np.ones((HE, HE))), BF16)
    y, dx2, du, dya, dyb, d_gates, dattn, dsum, dza, dob, acc2, lsq = _merge_call(
        oa, ob, proj, x, tgt, vecs2, attn, wa, wb, wo, hsum)
    loss = 0.5 * lsq[0, 0] / D
    d_wo, d_wo16 = _atb_call(y, du, "dw_out")
    d_wa, d_wa16 = _atb_call(oa, dya, "dw_branch_a")
    d_wb, d_wb16 = _atb_call(ob, dyb, "dw_branch_b")

    d_hgrn, acch = _hgrn_bwd_call(proj, oraw, dob, states, lb, gn)

    dq, dk, dv, dbs = _attn_bwd_call(proj, dattn, lse, dsum, biases)
    dqkvz = (dq, dk, dv, dza)

    d_win, d_win16 = _dwin_call(h_t, dqkvz, d_hgrn, d_gates)
    tok = hook((d_win, d_wa, d_wb, d_wo), (d_win16, d_wa16, d_wb16, d_wo16)) if hook is not None else 0.0
    one_scale = 1.0 + scale
    grad_x, acc1 = _dh_call(dqkvz, d_hgrn, d_gates, w_in, x, dx2,
                            jnp.concatenate([a + tok, norm_g, one_scale, jnp.zeros((5, D), F32)], 0))

    d_rel = jnp.zeros((NBUCKETS, NH), F32)
    for p, d in enumerate(PATTERNS):
        band, bucket = _band_bucket(d)
        onehot = (bucket[None] == np.arange(NBUCKETS)[:, None, None]) & band[None]
        d_rel = d_rel + jnp.einsum("hqk,bqk->bh", dbs[p], jnp.asarray(onehot, F32),
                                   precision=lax.Precision.HIGHEST)
    d_onorm = jnp.sum(acch[1].reshape(NH, HK), axis=0)

    zrow = jnp.zeros((D,), F32)
    pack = jnp.stack([acc1[0], acc1[1], acc2[0], acc1[2], acc2[1], acch[0],
                      zrow.at[:HK].set(d_onorm), zrow.at[0].set(loss),
                      zrow.at[:NBUCKETS * NH].set(d_rel.reshape(-1))] + [zrow] * 7, 0)
    return grad_x, d_win, d_wa, d_wb, d_wo, pack


def _me():
    return lax.axis_index("x"), lax.axis_index("y"), lax.axis_index("c")


def _peers(x, y):
    return [(1 - x, y), (x, 1 - y), (1 - x, 1 - y)]


def _allgather_small(blk, name):
    m_per, n = blk.shape

    def body(x_ref, out_ref, send_sems, recv_sems, local_sem):
        x, y, c = _me()
        me, sibling = (x, y, c), (x, y, 1 - c)
        chips = _peers(x, y)

        def rows(px, py, pc):
            return out_ref.at[pl.ds((4 * px + 2 * py + pc) * m_per, m_per), :]

        def copy(k, block, to, src=None):
            return pltpu.make_async_remote_copy(
                src_ref=rows(*block) if src is None else src, dst_ref=rows(*block),
                send_sem=send_sems.at[k], recv_sem=recv_sems.at[k], device_id=to, device_id_type=MESH)

        mine = pltpu.make_async_copy(x_ref, rows(*me), local_sem)
        mine.start()
        first = [copy(0, me, sibling, src=x_ref)]
        first += [copy(1 + j, me, (*chip, c), src=x_ref) for j, chip in enumerate(chips)]
        for cp in first:
            cp.start()
        passed = [copy(4 + j, (*chip, c), sibling) for j, chip in enumerate(chips)]
        for j, chip in enumerate(chips):
            copy(1 + j, (*chip, c), me).wait_recv()
            passed[j].start()
        copy(0, sibling, me).wait_recv()
        for j, chip in enumerate(chips):
            copy(4 + j, (*chip, 1 - c), me).wait_recv()
        for cp in first + passed:
            cp.wait_send()
        mine.wait()

    return pl.pallas_call(
        body, name=name, out_shape=jax.ShapeDtypeStruct((8 * m_per, n), blk.dtype),
        in_specs=[pl.BlockSpec(memory_space=pltpu.VMEM)], out_specs=pl.BlockSpec(memory_space=pltpu.VMEM),
        scratch_shapes=[pltpu.SemaphoreType.DMA((7,)), pltpu.SemaphoreType.DMA((7,)), pltpu.SemaphoreType.DMA],
    )(blk)


ANY = pl.BlockSpec(memory_space=pl.ANY)


HBM = pl.BlockSpec(memory_space=pltpu.HBM)
SEM = pl.BlockSpec(memory_space=pltpu.SEMAPHORE)
EFFECT = pltpu.SideEffectType.DATAFLOW_SIDE_EFFECTING


def _w_part(t, ref, j, half):
    if t == 0:
        return ref.at[pl.ds(half * (D // 2), D // 2), pl.ds(j * 2048, 2048)]
    if t == 1:
        return ref.at[pl.ds(half * (AW // 2), AW // 2), pl.ds(j * 256, 256)]
    return ref.at[pl.ds(j * 256 + half * 128, 128), :]


def _w_ici_copies(fulls, send_sems, recv_sems, tensors=(0, 1, 2, 3)):
    x, y, c = _me()
    outs, ins = [], []
    for u, t in enumerate(tensors):
        for k, chip in enumerate(_peers(x, y)):
            mine = _w_part(t, fulls[u], 2 * x + y, c)
            theirs = _w_part(t, fulls[u], 2 * chip[0] + chip[1], c)
            kw = dict(send_sem=send_sems.at[3 * t + k], recv_sem=recv_sems.at[3 * t + k],
                      device_id=(*chip, c), device_id_type=MESH)
            outs.append(pltpu.make_async_remote_copy(src_ref=mine, dst_ref=mine, **kw))
            ins.append(pltpu.make_async_remote_copy(src_ref=theirs, dst_ref=theirs, **kw))
    return outs, ins


def _gather_ici_start(fulls):
    def body(f0, f1, f2, f3, send_sems, recv_sems, t0, t1, t2, t3, token):
        for cp in _w_ici_copies([f0, f1, f2, f3], send_sems, recv_sems)[0]:
            cp.start()
        token[...] = jnp.zeros_like(token)

    res = pl.pallas_call(
        body, name="gather_ici_start",
        out_shape=(pltpu.SemaphoreType.DMA((12,)), pltpu.SemaphoreType.DMA((12,)),
                   *[pltpu.HBM(a.shape, a.dtype) for a in fulls], jax.ShapeDtypeStruct((8, 128), F32)),
        in_specs=[HBM] * 4, out_specs=(SEM, SEM, *[HBM] * 4, pl.BlockSpec(memory_space=pltpu.VMEM)),
        input_output_aliases={i: 2 + i for i in range(4)},
        compiler_params=pltpu.CompilerParams(has_side_effects=EFFECT),
    )(*[pltpu.with_memory_space_constraint(a, pltpu.HBM) for a in fulls])
    return res[0], res[1], list(res[2:6]), res[6]


def _gather_ici_wait(send_sems, recv_sems, bufs, after, tensors, name):
    n = len(tensors)

    def body(*refs):
        outs, ins = _w_ici_copies(refs[0:n], refs[n], refs[n + 1], tensors)
        for cp in outs:
            cp.wait_send()
        for cp in ins:
            cp.wait_recv()

    return pl.pallas_call(
        body, name=name, out_shape=tuple(pltpu.HBM(a.shape, a.dtype) for a in bufs),
        in_specs=[HBM] * n + [SEM, SEM, ANY], out_specs=[HBM] * n,
        input_output_aliases={i: i for i in range(n)},
        compiler_params=pltpu.CompilerParams(has_side_effects=EFFECT),
    )(*bufs, send_sems, recv_sems, after)


def _gather_sibling(bufs, tensors, name):
    n = len(tensors)

    def body(*refs):
        outs, send_sems, recv_sems = refs[n:2 * n], refs[2 * n], refs[2 * n + 1]
        x, y, c = _me()
        cps = []
        for u, t in enumerate(tensors):
            for k, chip in enumerate(_peers(x, y)):
                blk = _w_part(t, outs[u], 2 * chip[0] + chip[1], c)
                cp = pltpu.make_async_remote_copy(
                    src_ref=blk, dst_ref=blk, send_sem=send_sems.at[u, k], recv_sem=recv_sems.at[u, k],
                    device_id=(x, y, 1 - c), device_id_type=MESH)
                cp.start()
                cps.append(cp)
        for u, t in enumerate(tensors):
            for k, chip in enumerate(_peers(x, y)):
                blk = _w_part(t, outs[u], 2 * chip[0] + chip[1], 1 - c)
                pltpu.make_async_remote_copy(
                    src_ref=blk, dst_ref=blk, send_sem=send_sems.at[u, k], recv_sem=recv_sems.at[u, k],
                    device_id=(x, y, 1 - c), device_id_type=MESH).wait_recv()
        for cp in cps:
            cp.wait_send()

    return pl.pallas_call(
        body, name=name,
        out_shape=[jax.ShapeDtypeStruct(FULL_W_SHAPES[t], BF16) for t in tensors],
        in_specs=[ANY] * n, out_specs=[ANY] * n, input_output_aliases={u: u for u in range(n)},
        scratch_shapes=[pltpu.SemaphoreType.DMA((n, 3)), pltpu.SemaphoreType.DMA((n, 3))],
    )(*bufs)


def _half_of(t, ref, half):
    if t == 0:
        return ref.at[:, pl.ds(half * 512, 512), :]
    if t == 1:
        return ref.at[pl.ds(half * 256, 256), :]
    return ref.at[:, pl.ds(half * 512, 512)]


HALF_SHAPES = [(4, 512, 2048), (256, D), (D, 512), (D, 512)]
PIECE_SHAPES = [(512, 2048), (256, 256), (256, 512), (256, 512)]
SHARD_SHAPES = [(D, 2048), (AW, 256), (256, D), (256, D)]


def _chip_piece(t, ref, j):
    if t == 0:
        return ref.at[j]
    if t == 1:
        return ref.at[:, pl.ds(j * 256, 256)]
    return ref.at[pl.ds(j * 256, 256), :]


def _reduce_sibling_send(gs):
    def body(g0, g1, g2, g3, r0, r1, r2, r3, send_sems, recv_sems):
        x, y, c = _me()
        ins, outs = [g0, g1, g2, g3], [r0, r1, r2, r3]
        cps = []
        for t in range(4):
            cp = pltpu.make_async_remote_copy(
                src_ref=_half_of(t, ins[t], 1 - c), dst_ref=outs[t],
                send_sem=send_sems.at[t], recv_sem=recv_sems.at[t], device_id=(x, y, 1 - c), device_id_type=MESH)
            cp.start()
            cps.append(cp)
        for cp in cps:
            cp.wait_recv()
        for cp in cps:
            cp.wait_send()

    return pl.pallas_call(
        body, name="reduce_sibling", out_shape=[jax.ShapeDtypeStruct(s, BF16) for s in HALF_SHAPES],
        in_specs=[ANY] * 4, out_specs=[ANY] * 4,
        scratch_shapes=[pltpu.SemaphoreType.DMA((4,)), pltpu.SemaphoreType.DMA((4,))],
    )(*gs)


def _chip_copies(hs, lands, send_sems, recv_sems):
    x, y, c = _me()
    cps = []
    for t in range(4):
        for k, chip in enumerate(_peers(x, y)):
            pj = 2 * chip[0] + chip[1]
            cps.append(pltpu.make_async_remote_copy(
                src_ref=_chip_piece(t, hs[t], pj), dst_ref=lands[t].at[k],
                send_sem=send_sems.at[3 * t + k], recv_sem=recv_sems.at[3 * t + k],
                device_id=(*chip, c), device_id_type=MESH))
    return cps


def _reduce_chips_start(hs):
    lands = [lax.empty((3,) + s, BF16) for s in PIECE_SHAPES]

    def body(h0, h1, h2, h3, l0, l1, l2, l3, send_sems, recv_sems, t0, t1, t2, t3, t4, t5, t6, t7, token):
        for cp in _chip_copies([h0, h1, h2, h3], [l0, l1, l2, l3], send_sems, recv_sems):
            cp.start()
        token[...] = jnp.zeros_like(token)

    bufs = list(hs) + lands
    res = pl.pallas_call(
        body, name="reduce_chips_start",
        out_shape=(pltpu.SemaphoreType.DMA((12,)), pltpu.SemaphoreType.DMA((12,)),
                   *[pltpu.HBM(a.shape, a.dtype) for a in bufs], jax.ShapeDtypeStruct((8, 128), F32)),
        in_specs=[HBM] * 8, out_specs=(SEM, SEM, *[HBM] * 8, pl.BlockSpec(memory_space=pltpu.VMEM)),
        input_output_aliases={i: 2 + i for i in range(8)},
        compiler_params=pltpu.CompilerParams(has_side_effects=EFFECT),
    )(*[pltpu.with_memory_space_constraint(a, pltpu.HBM) for a in bufs])
    return res[0], res[1], list(res[2:10]), res[10]


def _reduce_chips_wait(send_sems, recv_sems, thru, after):
    def body(h0, h1, h2, h3, l0, l1, l2, l3, send_sems, recv_sems, after_ref, d0, d1, d2, d3, g0, g1, g2, g3):
        cps = _chip_copies([h0, h1, h2, h3], [l0, l1, l2, l3], send_sems, recv_sems)
        for cp in cps:
            cp.wait_send()
        for cp in cps:
            cp.wait_recv()

    res = pl.pallas_call(
        body, name="reduce_chips_wait", out_shape=tuple(pltpu.HBM(a.shape, a.dtype) for a in thru),
        in_specs=[HBM] * 8 + [SEM, SEM, ANY], out_specs=[HBM] * 8,
        input_output_aliases={i: i for i in range(8)},
        compiler_params=pltpu.CompilerParams(has_side_effects=EFFECT),
    )(*thru, send_sems, recv_sems, after)
    return list(res[4:8])


def _share_sibling(shards):
    def body(i0, i1, i2, i3, o0, o1, o2, o3, send_sems, recv_sems):
        x, y, c = _me()
        outs = [o0, o1, o2, o3]

        def half(t, ref, hf):
            if t == 0:
                return ref.at[pl.ds(hf * 512, 512), :]
            if t == 1:
                return ref.at[pl.ds(hf * 256, 256), :]
            return ref.at[:, pl.ds(hf * 512, 512)]

        cps = []
        for t in range(4):
            mine = half(t, outs[t], c)
            cp = pltpu.make_async_remote_copy(
                src_ref=mine, dst_ref=mine, send_sem=send_sems.at[t], recv_sem=recv_sems.at[t],
                device_id=(x, y, 1 - c), device_id_type=MESH)
            cp.start()
            cps.append(cp)
        for t in range(4):
            theirs = half(t, outs[t], 1 - c)
            pltpu.make_async_remote_copy(
                src_ref=theirs, dst_ref=theirs, send_sem=send_sems.at[t],
                recv_sem=recv_sems.at[t], device_id=(x, y, 1 - c), device_id_type=MESH).wait_recv()
        for cp in cps:
            cp.wait_send()

    return pl.pallas_call(
        body, name="share_sibling", out_shape=[jax.ShapeDtypeStruct(s, F32) for s in SHARD_SHAPES],
        in_specs=[ANY] * 4, out_specs=[ANY] * 4, input_output_aliases={0: 0, 1: 1, 2: 2, 3: 3},
        scratch_shapes=[pltpu.SemaphoreType.DMA((4,)), pltpu.SemaphoreType.DMA((4,))],
    )(*shards)


def _half_blockspec(t, idx_pos):
    if t == 0:
        return pl.BlockSpec((1, 512, 2048), lambda i, s: (i, s[idx_pos], 0)), 4
    if t == 1:
        return pl.BlockSpec((256, D), lambda i, s: (s[idx_pos], 0)), 1
    return pl.BlockSpec((256, 512), lambda i, s: (i, s[idx_pos])), 4


def _half_out_blockspec(t):
    if t == 0:
        return pl.BlockSpec((1, 512, 2048), lambda i, s: (i, 0, 0))
    if t == 1:
        return pl.BlockSpec((256, D), lambda i, s: (0, 0))
    return pl.BlockSpec((256, 512), lambda i, s: (i, 0))


def _add_half_call(t, own, recv, sc, name):
    in_blk, steps = _half_blockspec(t, 0)
    out_blk = _half_out_blockspec(t)

    def body(s_ref, a_ref, b_ref, o_ref, ob_ref):
        v = a_ref[...] + b_ref[...].astype(F32)
        o_ref[...] = v
        ob_ref[...] = v.astype(BF16)

    return pl.pallas_call(
        body, name=name,
        grid_spec=pltpu.PrefetchScalarGridSpec(
            num_scalar_prefetch=1, grid=(steps,), in_specs=[in_blk, out_blk], out_specs=[out_blk, out_blk]),
        out_shape=[jax.ShapeDtypeStruct(HALF_SHAPES[t], F32), jax.ShapeDtypeStruct(HALF_SHAPES[t], BF16)],
        compiler_params=_cp(),
    )(sc, own, recv)


def _final_piece_call(t, chipsum, recv3, sc, name):
    ps = PIECE_SHAPES[t]
    if t == 0:
        own_blk = pl.BlockSpec((1,) + ps, lambda i, s: (s[1], 0, 0))
        o_blk = pl.BlockSpec(ps, lambda i, s: (s[0], 0))
    elif t == 1:
        own_blk = pl.BlockSpec(ps, lambda i, s: (0, s[1]))
        o_blk = pl.BlockSpec(ps, lambda i, s: (s[0], 0))
    else:
        own_blk = pl.BlockSpec(ps, lambda i, s: (s[1], 0))
        o_blk = pl.BlockSpec(ps, lambda i, s: (0, s[0]))
    r_blk = pl.BlockSpec((3,) + ps, lambda i, s: (0, 0, 0))

    def body(s_ref, a_ref, r_ref, o_ref):
        a = a_ref[0] if t == 0 else a_ref[...]
        o_ref[...] = ((a + r_ref[0].astype(F32)) + r_ref[1].astype(F32)) + r_ref[2].astype(F32)

    return pl.pallas_call(
        body, name=name,
        grid_spec=pltpu.PrefetchScalarGridSpec(
            num_scalar_prefetch=1, grid=(1,), in_specs=[own_blk, r_blk], out_specs=o_blk),
        out_shape=jax.ShapeDtypeStruct(SHARD_SHAPES[t], F32), compiler_params=_cp(),
    )(sc, chipsum, recv3)


FULL_W_SHAPES = [(D, NPROJ), (AW, D), (D, D), (D, D)]


def _cast_place_call(t, shard, sc, name):
    if t == 0:
        blk, steps = (512, 2048), 2
        in_blk = pl.BlockSpec(blk, lambda i, s: (i, 0))
        o_blk = pl.BlockSpec(blk, lambda i, s: (i, s[1]))
    elif t == 1:
        blk, steps = (AW, 256), 1
        in_blk = pl.BlockSpec(blk, lambda i, s: (0, 0))
        o_blk = pl.BlockSpec(blk, lambda i, s: (0, s[1]))
    else:
        blk, steps = (256, D), 1
        in_blk = pl.BlockSpec(blk, lambda i, s: (0, 0))
        o_blk = pl.BlockSpec(blk, lambda i, s: (s[1], 0))

    def body(s_ref, a_ref, o_ref, own_ref):
        v = a_ref[...].astype(BF16)
        o_ref[...] = v
        own_ref[...] = v

    return pl.pallas_call(
        body, name=name,
        grid_spec=pltpu.PrefetchScalarGridSpec(
            num_scalar_prefetch=1, grid=(steps,), in_specs=[in_blk], out_specs=[o_blk, in_blk]),
        out_shape=[jax.ShapeDtypeStruct(FULL_W_SHAPES[t], BF16), jax.ShapeDtypeStruct(shard.shape, BF16)],
        compiler_params=_cp(),
    )(sc, shard)


def _lower_bound_fn(hgrn_lb):
    return jnp.cumsum(jax.nn.softmax(hgrn_lb.astype(F32), axis=0), axis=0)[0]


def kernel(x, c, w_ada, b_ada, norm_g, w_in, hgrn_onorm_g, w_branch_a, w_branch_b, w_out, rel_bias, hgrn_lb, final_g, loss_target, m_w_ada, m_b_ada, m_norm_g, m_w_in, m_hgrn_onorm_g, m_w_branch_a, m_w_branch_b, m_w_out, m_rel_bias, m_hgrn_lb, m_final_g, v_w_ada, v_b_ada, v_norm_g, v_w_in, v_hgrn_onorm_g, v_w_branch_a, v_w_branch_b, v_w_out, v_rel_bias, v_hgrn_lb, v_final_g):
    ax, ay, ac = _me()
    chip = 2 * ax + ay
    dev = 4 * ax + 2 * ay + ac
    sc_idx = jnp.stack([ac, chip]).astype(jnp.int32)

    c_all = _allgather_small(jnp.pad(c, ((0, 7), (0, 0))), "gather_c").reshape(8, 8, D)[:, 0]
    b_s = lax.dynamic_slice(b_ada, (0, chip * 768), (1, 768))
    mod_part = _mod_call(c_all, w_ada[0], b_s)
    mod_all = _allgather_small(mod_part, "gather_mod").reshape(8, 8, 768)
    mod_mine = lax.dynamic_index_in_dim(mod_all, dev, axis=1, keepdims=False)
    mod = mod_mine[0::2].reshape(1, 3 * D)

    names = ["w_in", "w_a", "w_b", "w_o"]
    shards, mod = lax.optimization_barrier(([w_in[0], w_branch_a[0], w_branch_b[0], w_out[0]], mod))
    placed = [_cast_place_call(t, shards[t], sc_idx, "cast_" + names[t]) for t in range(4)]
    w_send_sems, w_recv_sems, w_thru, w_token = _gather_ici_start([p_[0] for p_ in placed])
    mod = mod + w_token[0, 0]
    rel_bias_t = rel_bias + w_token[0, 0]
    shift, scale, gate = mod[:, :D], mod[:, D:2 * D], mod[:, 2 * D:]

    def weights_fn(xs, avec):
        proj, h, h_t = _proj_own_call(xs, avec, placed[0][1], sc_idx)
        arrived = _gather_ici_wait(w_send_sems, w_recv_sems, w_thru[0:1], proj, (0,), "gather_ici_wait_in")
        (win_f,) = _gather_sibling(arrived, (0,), "gather_sibling_in")

        def rest_fn(after):
            late = _gather_ici_wait(w_send_sems, w_recv_sems, w_thru[1:4], after, (1, 2, 3), "gather_ici_wait_rest")
            return _gather_sibling(late, (1, 2, 3), "gather_sibling_rest")

        return h, h_t, _proj_rest_call(h, win_f, proj, sc_idx), win_f, rest_fn

    flight = {}

    def start_reduction(own, own16):
        sib = _reduce_sibling_send(own16)
        halves = [_add_half_call(t, own[t], sib[t], sc_idx, "chipsum_" + names[t]) for t in range(4)]
        send_sems, recv_sems, thru, token = _reduce_chips_start([hb for _, hb in halves])
        flight.update(sems=(send_sems, recv_sems), thru=thru, sums=[hf for hf, _ in halves])
        return token[0, 0]

    lb, lb_vjp = jax.vjp(_lower_bound_fn, hgrn_lb)
    grad_x, d_win, d_wa, d_wb, d_wo, pack = _local_step(
        x[0], loss_target[0], shift, scale, gate, norm_g, hgrn_onorm_g, rel_bias_t, lb[None, :],
        final_g[None, :], weights_fn, hook=start_reduction)

    rec = _reduce_chips_wait(*flight["sems"], flight["thru"], pack)
    pieces = [_final_piece_call(t, flight["sums"][t], rec[t], sc_idx, "piece_" + names[t]) for t in range(4)]
    g_win, g_wa, g_wb, g_wo = _share_sibling(pieces)

    packs = _allgather_small(pack, "gather_small").reshape(8, 16, D)
    tot = _sum8_call(packs)
    loss = tot[7, 0]
    g_b_ada = tot[0:3].reshape(1, 3 * D)
    g_norm_g = tot[3:4]
    g_final_g = tot[4]
    (g_hgrn_lb,) = lb_vjp(tot[5])
    g_onorm = tot[6:7, :HK]
    g_rel = tot[8, :NBUCKETS * NH].reshape(NBUCKETS, NH)

    def rows_of(a):
        flat = a.reshape(-1)
        n = -(-flat.shape[0] // D)
        return jnp.pad(flat, (0, n * D - flat.shape[0])).reshape(n, D)

    smalls = [(b_ada, g_b_ada, m_b_ada, v_b_ada), (norm_g, g_norm_g, m_norm_g, v_norm_g),
              (hgrn_onorm_g, g_onorm, m_hgrn_onorm_g, v_hgrn_onorm_g), (rel_bias, g_rel, m_rel_bias, v_rel_bias),
              (hgrn_lb, g_hgrn_lb, m_hgrn_lb, v_hgrn_lb), (final_g, g_final_g, m_final_g, v_final_g)]
    cat = [jnp.concatenate([rows_of(s[k]) for s in smalls], 0) for k in range(4)]
    cat = [jnp.pad(a, ((0, 16 - a.shape[0]), (0, 0))) for a in cat]
    _, sd, sm, sv = _adamw_call(*cat, "adamw_small")

    def unpack(packed):
        res, r = [], 0
        for s in smalls:
            n = -(-s[0].size // D)
            res.append(packed[r:r + n].reshape(-1)[:s[0].size].reshape(s[0].shape))
            r += n
        return res

    d_small, m_small, v_small = unpack(sd), unpack(sm), unpack(sv)

    sc_all = c_all * jax.nn.sigmoid(c_all)
    dmod_all = packs[:, 0:3].reshape(8, 3 * D)
    dm_s = lax.dynamic_slice(dmod_all, (0, chip * 768), (8, 768))
    g_w_ada, d_w_ada, nm_w_ada, nv_w_ada = _ada_update_call(sc_all.T, dm_s, w_ada[0], m_w_ada[0], v_w_ada[0])

    big = []
    for w, g, m, v, n in [(w_in, g_win, m_w_in, v_w_in, "w_in"), (w_branch_a, g_wa, m_w_branch_a, v_w_branch_a, "w_a"),
                          (w_branch_b, g_wb, m_w_branch_b, v_w_branch_b, "w_b"), (w_out, g_wo, m_w_out, v_w_out, "w_o")]:
        big.append(_adamw_call(w[0], g, m[0], v[0], "adamw_" + n))

    e = lambda a: a[None]
    grads = [e(g_w_ada), g_b_ada, g_norm_g, e(big[0][0]), g_onorm, e(big[1][0]), e(big[2][0]), e(big[3][0]),
             g_rel, g_hgrn_lb, g_final_g]
    deltas = [e(d_w_ada), d_small[0], d_small[1], e(big[0][1]), d_small[2], e(big[1][1]), e(big[2][1]), e(big[3][1]),
              d_small[3], d_small[4], d_small[5]]
    new_m = [e(nm_w_ada), m_small[0], m_small[1], e(big[0][2]), m_small[2], e(big[1][2]), e(big[2][2]), e(big[3][2]),
             m_small[3], m_small[4], m_small[5]]
    new_v = [e(nv_w_ada), v_small[0], v_small[1], e(big[0][3]), v_small[2], e(big[1][3]), e(big[2][3]), e(big[3][3]),
             v_small[3], v_small[4], v_small[5]]
    return (loss, grad_x[None], *grads, *deltas, *new_m, *new_v)
```

```python
import math

import numpy as np
import jax
import jax.numpy as jnp
from jax import lax
from jax.experimental import pallas as pl
from jax.experimental.pallas import tpu as pltpu

D = 1024
AW = 512
NH = 8
HE = 64
HK = 128
NPROJ = 8192
ABLK = 128
PATTERNS = (1, 4, 16)
NBUCKETS = 32
MAXDIST = 2048
NEG = -1e30
EPS = 1e-6
CH = 64
LR, B1, B2, AEPS, WD, STEP = 0.001, 0.9, 0.999, 1e-08, 0.01, 10

F32 = jnp.float32
BF16 = jnp.bfloat16
MESH = pl.DeviceIdType.MESH
VMEM_LIMIT = 56 * 1024 * 1024


def _cp(**kw):
    return pltpu.CompilerParams(vmem_limit_bytes=VMEM_LIMIT, **kw)


def _sig(x):
    return 0.5 * jnp.tanh(0.5 * x) + 0.5


def _nt(a, b):
    return lax.dot_general(a, b, (((1,), (1,)), ((), ())), preferred_element_type=F32)


def _tn(a, b):
    return lax.dot_general(a, b, (((0,), (0,)), ((), ())), preferred_element_type=F32)


def _nn(a, b):
    return jnp.dot(a, b, preferred_element_type=F32)


def _split2(x):
    h = x.astype(BF16)
    return h, (x - h.astype(F32)).astype(BF16)


def _exact_mm(tri_bf16, x):
    h, l = _split2(x)
    return _nn(tri_bf16, h) + _nn(tri_bf16, l)


def _exact_mm_r(x, ones_bf16):
    h, l = _split2(x)
    return _nn(h, ones_bf16) + _nn(l, ones_bf16)


def _proj_own_call(x, avec, w_own, sc):
    S = x.shape[0]
    tm, tn = 512, 2048

    def body(s_ref, x_ref, a_ref, w_ref, o_ref, h_ref, ht_ref):
        xv = x_ref[...]
        r = lax.rsqrt(jnp.mean(xv * xv, axis=-1, keepdims=True) + EPS)
        hv = xv * r * a_ref[0:1, :] + a_ref[1:2, :]
        hb = hv.astype(BF16)
        h_ref[...] = hb
        ht_ref[...] = hv.T.astype(BF16)
        o_ref[...] = _nn(hb, w_ref[...])

    return pl.pallas_call(
        body, name="in_proj_own",
        grid_spec=pltpu.PrefetchScalarGridSpec(
            num_scalar_prefetch=1, grid=(S // tm,),
            in_specs=[pl.BlockSpec((tm, D), lambda i, s: (i, 0)), pl.BlockSpec((8, D), lambda i, s: (0, 0)),
                      pl.BlockSpec((D, tn), lambda i, s: (0, 0))],
            out_specs=[pl.BlockSpec((tm, tn), lambda i, s: (i, s[1])), pl.BlockSpec((tm, D), lambda i, s: (i, 0)),
                       pl.BlockSpec((D, tm), lambda i, s: (0, i))]),
        out_shape=[jax.ShapeDtypeStruct((S, NPROJ), F32), jax.ShapeDtypeStruct((S, D), BF16),
                   jax.ShapeDtypeStruct((D, S), BF16)],
        compiler_params=_cp(),
    )(sc, x, avec, w_own)


def _proj_rest_call(h, w_in, proj, sc, flips, name):
    S = h.shape[0]
    tm, tn = 512, 2048

    def body(s_ref, h_ref, w_ref, p_ref, o_ref):
        o_ref[...] = _nn(h_ref[...], w_ref[...])

    def col(j, s):
        flip = flips[0]
        for idx in range(1, len(flips)):
            flip = jnp.where(j == idx, flips[idx], flip)
        return jnp.bitwise_xor(s[1], flip)

    return pl.pallas_call(
        body, name=name,
        grid_spec=pltpu.PrefetchScalarGridSpec(
            num_scalar_prefetch=1, grid=(len(flips), S // tm),
            in_specs=[pl.BlockSpec((tm, D), lambda j, i, s: (i, 0)),
                      pl.BlockSpec((D, tn), lambda j, i, s: (0, col(j, s))),
                      pl.BlockSpec(memory_space=pl.ANY)],
            out_specs=pl.BlockSpec((tm, tn), lambda j, i, s: (i, col(j, s)))),
        out_shape=jax.ShapeDtypeStruct((S, NPROJ), F32), input_output_aliases={3: 0}, compiler_params=_cp(),
    )(sc, h, w_in, proj)


def _t5_bucket_np(dist):
    max_exact = NBUCKETS // 2
    n = dist.astype(np.float32)
    large = max_exact + (np.log(np.maximum(n, np.float32(1.0)) / np.float32(max_exact))
                         / np.float32(math.log(MAXDIST / max_exact))
                         * np.float32(NBUCKETS - max_exact)).astype(np.int32)
    large = np.minimum(large, NBUCKETS - 1)
    return np.where(dist < max_exact, dist, large)


def _band_bucket(d):
    qi = np.arange(ABLK)[:, None]
    kj = np.arange(2 * ABLK)[None, :]
    delta = qi + ABLK - kj
    band = (delta >= 0) & (delta <= ABLK)
    bucket = _t5_bucket_np(np.clip(delta, 0, None) * d)
    return band, bucket


def _bias_tiles(rel_bias):
    tiles = []
    for d in PATTERNS:
        band, bucket = _band_bucket(d)
        onehot = (jnp.asarray(bucket, jnp.int32)[None] == jnp.arange(NBUCKETS, dtype=jnp.int32)[:, None, None])
        bias = jnp.einsum("bqk,bh->hqk", onehot.astype(F32), rel_bias, precision=lax.Precision.HIGHEST)
        tiles.append(jnp.where(jnp.asarray(band)[None], bias, NEG))
    return jnp.stack(tiles, 0)


ATT = 2048
HP = 2 * HE
LOG2E = 1.4426950408889634
LN2 = 0.6931471805599453
QSCALE2 = (HE ** -0.5) * LOG2E
AGRP = 8
AGRP_B = 4


def _attn_blocks():
    out = []
    for p, d in enumerate(PATTERNS):
        for r in range(d):
            for n in range(ATT // (d * ABLK)):
                out.append((p, d, r, n))
    return out


def _attn_fwd_call(proj, biases):
    S = proj.shape[0]
    nt = S // ATT

    def body(q_ref, k_ref, v_ref, z_ref, b_ref, a_ref, l_ref, oa_ref, kc, vc, op, lp):
        i = pl.program_id(1)

        @pl.when(i == 0)
        def _():
            kc[0:ATT] = jnp.zeros((ATT, HP), F32)
            vc[0:ATT] = jnp.zeros((ATT, HP), F32)

        @pl.when(i > 0)
        def _():
            kc[0:ATT] = kc[ATT:2 * ATT]
            vc[0:ATT] = vc[ATT:2 * ATT]

        kc[ATT:2 * ATT] = k_ref[...]
        vc[ATT:2 * ATT] = v_ref[...]
        col = lax.broadcasted_iota(jnp.int32, (ABLK, 2 * ABLK), 1)
        dead = jnp.logical_and(i == 0, col < ABLK)
        blocks = _attn_blocks()
        hs = (slice(0, HE), slice(HE, 2 * HE))
        for g0 in range(0, len(blocks), AGRP):
            grp = blocks[g0:g0 + AGRP]
            qrows = [pl.ds(n * ABLK * d + r, ABLK, stride=d) for p, d, r, n in grp]
            krows = [pl.ds(ATT + (n - 1) * ABLK * d + r, 2 * ABLK, stride=d) for p, d, r, n in grp]
            qs = [(q_ref[qr, :] * QSCALE2).astype(BF16) for qr in qrows]
            ks = [kc[kr, :].astype(BF16) for kr in krows]
            vs = [vc[kr, :].astype(BF16) for kr in krows]
            ss = [[_nt(qs[b][:, sl], ks[b][:, sl]) + b_ref[grp[b][0], e] for e, sl in enumerate(hs)]
                  for b in range(len(grp))]
            ss = [[jnp.where(dead, NEG, s) if grp[b][3] == 0 else s for s in ss[b]] for b in range(len(grp))]
            mxs = [[jnp.max(s, axis=-1, keepdims=True) for s in sb] for sb in ss]
            pes = [[jnp.exp2(s - mx) for s, mx in zip(sb, mb)] for sb, mb in zip(ss, mxs)]
            dens = [[jnp.sum(pe, axis=-1, keepdims=True) for pe in pb] for pb in pes]
            pvs = [[_nn(pe.astype(BF16), vs[b][:, sl]) for pe, sl in zip(pes[b], hs)] for b in range(len(grp))]
            for b in range(len(grp)):
                p, d, r, n = grp[b]
                prow = pl.ds(p * ATT + n * ABLK * d + r, ABLK, stride=d)
                lp[prow, :] = jnp.concatenate(
                    [jnp.broadcast_to(mx + jnp.log2(dn), (ABLK, HE)) for mx, dn in zip(mxs[b], dens[b])], axis=1)
                op[prow, :] = jnp.concatenate([pv / dn for pv, dn in zip(pvs[b], dens[b])], axis=1)
        rt = 256
        for t in range(ATT // rt):
            rows = slice(t * rt, (t + 1) * rt)
            pr = [slice(p * ATT + t * rt, p * ATT + (t + 1) * rt) for p in range(3)]
            la, lb_, lc = lp[pr[0], :], lp[pr[1], :], lp[pr[2], :]
            m = jnp.maximum(jnp.maximum(la, lb_), lc)
            ea, eb, ec = jnp.exp2(la - m), jnp.exp2(lb_ - m), jnp.exp2(lc - m)
            den = ea + eb + ec
            att = (ea * op[pr[0], :] + eb * op[pr[1], :] + ec * op[pr[2], :]) / den
            a_ref[rows, :] = att
            l_ref[rows, :] = m + jnp.log2(den)
            z = z_ref[rows, :]
            oa_ref[rows, :] = (att * (z * _sig(z))).astype(BF16)

    def pcol(c):
        return pl.BlockSpec((ATT, HP), lambda h, i: (i, c * 4 + h))

    out = pl.BlockSpec((ATT, HP), lambda h, i: (i, h))
    return pl.pallas_call(
        body, name="attn_fwd", grid=(4, nt),
        in_specs=[pcol(0), pcol(1), pcol(2), pcol(3),
                  pl.BlockSpec((3, 2, ABLK, 2 * ABLK), lambda h, i: (0, h, 0, 0))],
        out_specs=[out, out, out],
        out_shape=[jax.ShapeDtypeStruct((S, AW), F32), jax.ShapeDtypeStruct((S, AW), F32),
                   jax.ShapeDtypeStruct((S, AW), BF16)],
        scratch_shapes=[pltpu.VMEM((2 * ATT, HP), F32), pltpu.VMEM((2 * ATT, HP), F32),
                        pltpu.VMEM((3 * ATT, HP), F32), pltpu.VMEM((3 * ATT, HP), F32)],
        compiler_params=_cp(),
    )(proj, proj, proj, proj, biases)


def _attn_bwd_call(proj, dattn, lse, dsum, biases):
    S = proj.shape[0]
    nt = S // ATT

    def body(q_ref, k_ref, v_ref, do_ref, l_ref, ds_ref, b_ref, dq_ref, dk_ref, dv_ref, db_ref,
             kc, vc, dkc, dvc, dqa):
        i = pl.program_id(1)

        @pl.when(i == 0)
        def _():
            kc[ATT:2 * ATT] = jnp.zeros((ATT, HP), F32)
            vc[ATT:2 * ATT] = jnp.zeros((ATT, HP), F32)
            dkc[ATT:2 * ATT] = jnp.zeros((ATT, HP), F32)
            dvc[ATT:2 * ATT] = jnp.zeros((ATT, HP), F32)
            db_ref[...] = jnp.zeros_like(db_ref)

        @pl.when(i < nt)
        def _():
            kc[0:ATT] = kc[ATT:2 * ATT]
            vc[0:ATT] = vc[ATT:2 * ATT]
            dkc[0:ATT] = dkc[ATT:2 * ATT]
            dvc[0:ATT] = dvc[ATT:2 * ATT]
            kc[ATT:2 * ATT] = k_ref[...]
            vc[ATT:2 * ATT] = v_ref[...]
            dkc[ATT:2 * ATT] = jnp.zeros((ATT, HP), F32)
            dvc[ATT:2 * ATT] = jnp.zeros((ATT, HP), F32)
            col = lax.broadcasted_iota(jnp.int32, (ABLK, 2 * ABLK), 1)
            dead = jnp.logical_and(i == 0, col < ABLK)
            blocks = _attn_blocks()
            hs = (slice(0, HE), slice(HE, 2 * HE))
            for g0 in range(0, len(blocks), AGRP_B):
                grp = blocks[g0:g0 + AGRP_B]
                nb_ = range(len(grp))
                qrows = [pl.ds(n * ABLK * d + r, ABLK, stride=d) for p, d, r, n in grp]
                krows = [pl.ds(ATT + (n - 1) * ABLK * d + r, 2 * ABLK, stride=d) for p, d, r, n in grp]
                qs = [(q_ref[qr, :] * QSCALE2).astype(BF16) for qr in qrows]
                ks = [kc[kr, :].astype(BF16) for kr in krows]
                vs = [vc[kr, :].astype(BF16) for kr in krows]
                dos = [do_ref[qr, :].astype(BF16) for qr in qrows]
                lvs = [l_ref[qr, :] for qr in qrows]
                dsvs = [ds_ref[qr, :] for qr in qrows]
                ss = [[_nt(qs[b][:, sl], ks[b][:, sl]) + b_ref[grp[b][0], e] for e, sl in enumerate(hs)] for b in nb_]
                ss = [[jnp.where(dead, NEG, s) if grp[b][3] == 0 else s for s in ss[b]] for b in nb_]
                dps = [[_nt(dos[b][:, sl], vs[b][:, sl]) for sl in hs] for b in nb_]
                pes = [[jnp.exp2(ss[b][e] - lvs[b][:, e * HE:e * HE + 1]) for e in range(2)] for b in nb_]
                dscs = [[pes[b][e] * (dps[b][e] - dsvs[b][:, e * HE:e * HE + 1]) for e in range(2)] for b in nb_]
                for e in range(2):
                    tot = dscs[0][e]
                    for b in range(1, len(grp)):
                        tot = tot + dscs[b][e]
                    db_ref[grp[0][0], e] += tot
                dsbs = [[t.astype(BF16) for t in tb] for tb in dscs]
                dqs = [[_nn(dsbs[b][e], ks[b][:, sl]) * (HE ** -0.5) for e, sl in enumerate(hs)] for b in nb_]
                dks = [[_tn(dsbs[b][e], qs[b][:, sl]) for e, sl in enumerate(hs)] for b in nb_]
                dvs = [[_tn(pes[b][e].astype(BF16), dos[b][:, sl]) for e, sl in enumerate(hs)] for b in nb_]
                for b in nb_:
                    dq = jnp.concatenate(dqs[b], axis=1)
                    if grp[b][0] == 0:
                        dqa[qrows[b], :] = dq
                    else:
                        dqa[qrows[b], :] += dq
                    dkc[krows[b], :] += jnp.concatenate(dks[b], axis=1)
                    dvc[krows[b], :] += jnp.concatenate(dvs[b], axis=1)
            dq_ref[...] = dqa[...].astype(BF16)
            dk_ref[...] = (dkc[0:ATT] * LN2).astype(BF16)
            dv_ref[...] = dvc[0:ATT].astype(BF16)

        @pl.when(i == nt)
        def _():
            dk_ref[...] = (dkc[ATT:2 * ATT] * LN2).astype(BF16)
            dv_ref[...] = dvc[ATT:2 * ATT].astype(BF16)

    def pcol(c):
        return pl.BlockSpec((ATT, HP), lambda h, i: (jnp.minimum(i, nt - 1), c * 4 + h))

    qrow = pl.BlockSpec((ATT, HP), lambda h, i: (jnp.minimum(i, nt - 1), h))
    krow = pl.BlockSpec((ATT, HP), lambda h, i: (jnp.maximum(i - 1, 0), h))
    bspec = pl.BlockSpec((3, 2, ABLK, 2 * ABLK), lambda h, i: (0, h, 0, 0))
    return pl.pallas_call(
        body, name="attn_bwd", grid=(4, nt + 1),
        in_specs=[pcol(0), pcol(1), pcol(2), qrow, qrow, qrow, bspec],
        out_specs=[qrow, krow, krow, bspec],
        out_shape=[jax.ShapeDtypeStruct((S, AW), BF16)] * 3
                  + [jax.ShapeDtypeStruct((3, NH, ABLK, 2 * ABLK), F32)],
        scratch_shapes=[pltpu.VMEM((2 * ATT, HP), F32)] * 4 + [pltpu.VMEM((ATT, HP), F32)],
        compiler_params=_cp(),
    )(proj, proj, proj, dattn, lse, dsum, biases)


HRB = 256


def _tri_masks():
    row = lax.broadcasted_iota(jnp.int32, (CH, CH), 0)
    col = lax.broadcasted_iota(jnp.int32, (CH, CH), 1)
    return row >= col


def _heads():
    return [slice(hh * HK, (hh + 1) * HK) for hh in range(NH)]


def _hgrn_gate_heads(q_ref, f_ref, rows, lbv):
    out = []
    for sl in _heads():
        qraw = q_ref[rows, sl]
        sq = _sig(qraw)
        sf = _sig(f_ref[rows, sl])
        f = lbv[:, sl] + (1.0 - lbv[:, sl]) * sf
        out.append(dict(qraw=qraw, sq=sq, q=qraw * sq, sf=sf, f=f, k=1.0 - f, lg=jnp.log(f)))
    return out


def _hgrn_decay_heads(gh, b):
    bl = b[CH - 1:CH, :]
    bm = b[CH // 2 - 1:CH // 2, :]
    ebm = jnp.exp(bm)
    eblm = jnp.exp(bl - bm)
    ebl = jnp.exp(bl)
    out = []
    for g, sl in zip(gh, _heads()):
        d = b[:, sl] - bm[:, sl]
        e1 = jnp.exp(d)
        e2 = jnp.exp(-d)
        qs = g["q"] * e1
        ks = g["k"] * e2
        qe = qs * ebm[:, sl]
        kd = ks * eblm[:, sl]
        out.append(dict(e1=e1, e2=e2, qe=qe, kd=kd, ebm=ebm[:, sl], eblm=eblm[:, sl], ebl=ebl[:, sl],
                        qsb=qs.astype(BF16), ksb=ks.astype(BF16), qeb=qe.astype(BF16), kdb=kd.astype(BF16)))
    return out


def _hgrn_fwd_call(proj, lb, gn):
    S = proj.shape[0]
    nc = S // CH
    cps = HRB // CH

    def body(q_ref, f_ref, i_ref, z_ref, lb_ref, gn_ref, or_ref, ob_ref, st_ref, st):
        @pl.when(pl.program_id(0) == 0)
        def _():
            st[...] = jnp.zeros_like(st)

        low = _tri_masks()
        tri = low.astype(BF16)
        lbv = lb_ref[...]
        hs = _heads()
        for ci in range(cps):
            rows = slice(ci * CH, (ci + 1) * CH)
            gh = _hgrn_gate_heads(q_ref, f_ref, rows, lbv)
            b = _exact_mm(tri, jnp.concatenate([g["lg"] for g in gh], axis=1))
            dh = _hgrn_decay_heads(gh, b)
            vbs = [i_ref[rows, sl].astype(BF16) for sl in hs]
            st_ref[ci] = st[...]
            s0s = [st[sl, :] for sl in hs]
            as_ = [_nt(d["qsb"], d["ksb"]) for d in dh]
            ois = [_nt(d["qeb"], s0.astype(BF16)) for d, s0 in zip(dh, s0s)]
            sts = [_tn(vb, d["kdb"]) for vb, d in zip(vbs, dh)]
            abs_ = [jnp.where(low, a, 0.0).astype(BF16) for a in as_]
            os_ = [oi + _nn(a, vb) for oi, a, vb in zip(ois, abs_, vbs)]
            for sl, s0, sn, d, o in zip(hs, s0s, sts, dh, os_):
                st[sl, :] = s0 * d["ebl"] + sn
                or_ref[rows, sl] = o
                r = lax.rsqrt(jnp.mean(o * o, axis=-1, keepdims=True) + EPS)
                z = z_ref[rows, sl]
                ob_ref[rows, sl] = (o * r * gn_ref[:, sl] * (z * _sig(z))).astype(BF16)

    def pcol(c):
        return pl.BlockSpec((HRB, D), lambda i: (i, c))

    vec = pl.BlockSpec((1, D), lambda i: (0, 0))
    row = pl.BlockSpec((HRB, D), lambda i: (i, 0))
    return pl.pallas_call(
        body, name="hgrn_fwd", grid=(S // HRB,),
        in_specs=[pcol(2), pcol(3), pcol(4), pcol(5), vec, vec],
        out_specs=[row, row, pl.BlockSpec((cps, NH * HK, HK), lambda i: (i, 0, 0))],
        out_shape=[jax.ShapeDtypeStruct((S, D), F32), jax.ShapeDtypeStruct((S, D), BF16),
                   jax.ShapeDtypeStruct((nc, NH * HK, HK), F32)],
        scratch_shapes=[pltpu.VMEM((NH * HK, HK), F32)],
        compiler_params=_cp(),
    )(proj, proj, proj, proj, lb, gn)


def _hgrn_bwd_call(proj, oraw, dob, states, lb, gn):
    S = proj.shape[0]
    nblk = S // HRB
    cps = HRB // CH

    def body(q_ref, f_ref, i_ref, z_ref, or_ref, dob_ref, st_ref, lb_ref, gn_ref, dh_ref, acc_ref, dst):
        @pl.when(pl.program_id(0) == 0)
        def _():
            dst[...] = jnp.zeros_like(dst)
            acc_ref[...] = jnp.zeros_like(acc_ref)

        low = _tri_masks()
        tri = low.astype(BF16)
        triu = jnp.logical_not(_tri_masks()) | (lax.broadcasted_iota(jnp.int32, (CH, CH), 0)
                                               == lax.broadcasted_iota(jnp.int32, (CH, CH), 1))
        triu = triu.astype(BF16)
        lbv = lb_ref[...]
        hs = _heads()
        for ci in reversed(range(cps)):
            rows = slice(ci * CH, (ci + 1) * CH)
            dobs, dgns = [], []
            for sl in hs:
                o = or_ref[rows, sl]
                z = z_ref[rows, sl]
                sz = _sig(z)
                gnv = gn_ref[:, sl]
                dobv = dob_ref[rows, sl]
                r = lax.rsqrt(jnp.mean(o * o, axis=-1, keepdims=True) + EPS)
                onr = o * r
                don = dobv * (z * sz)
                dh_ref[rows, 3 * D + sl.start:3 * D + sl.stop] = (
                    dobv * (onr * gnv) * (sz * (1.0 + z * (1.0 - sz)))).astype(BF16)
                dgns.append(jnp.sum(don * onr, axis=0, keepdims=True))
                gh_ = don * gnv
                dobs.append((r * (gh_ - onr * jnp.mean(gh_ * onr, axis=-1, keepdims=True))).astype(BF16))
            acc_ref[1:2, :] += jnp.concatenate(dgns, axis=1)

            gh = _hgrn_gate_heads(q_ref, f_ref, rows, lbv)
            b = _exact_mm(tri, jnp.concatenate([g["lg"] for g in gh], axis=1))
            dh = _hgrn_decay_heads(gh, b)
            vbs = [i_ref[rows, sl].astype(BF16) for sl in hs]

            st0s = [st_ref[ci, sl, :] for sl in hs]
            dst1s = [dst[sl, :] for sl in hs]
            dst1bs = [t.astype(BF16) for t in dst1s]
            as_ = [_nt(d["qsb"], d["ksb"]) for d in dh]
            das_ = [_nt(do, vb) for do, vb in zip(dobs, vbs)]
            dqes = [_nn(do, s0.astype(BF16)) for do, s0 in zip(dobs, st0s)]
            dkds = [_nn(vb, d1) for vb, d1 in zip(vbs, dst1bs)]
            dvis = [_nt(d["kdb"], d1) for d, d1 in zip(dh, dst1bs)]
            dsts = [_tn(do, d["qeb"]) for do, d in zip(dobs, dh)]
            abs_ = [jnp.where(low, a, 0.0).astype(BF16) for a in as_]
            dabs_ = [jnp.where(low, a, 0.0).astype(BF16) for a in das_]
            dqss = [_nn(da, d["ksb"]) for da, d in zip(dabs_, dh)]
            dkss = [_tn(da, d["qsb"]) for da, d in zip(dabs_, dh)]
            dvs_ = [_tn(a, do) + dvi for a, do, dvi in zip(abs_, dobs, dvis)]

            dqs_, dks_, dbs_, exs_ = [], [], [], []
            for hh, sl in enumerate(hs):
                d, d1, s0 = dh[hh], dst1s[hh], st0s[hh]
                dqe, dqs, dks, dkd = dqes[hh], dqss[hh], dkss[hh], dkds[hh]
                dst[sl, :] = dsts[hh] + d1 * d["ebl"]
                dh_ref[rows, 2 * D + sl.start:2 * D + sl.stop] = dvs_[hh].astype(BF16)
                dqs_.append((dqe * d["ebm"] + dqs) * d["e1"])
                dks_.append((dks + dkd * d["eblm"]) * d["e2"])
                dkdkd = dkd * d["kd"]
                dbs_.append(dqe * d["qe"] + dqs * d["qsb"].astype(F32) - dks * d["ksb"].astype(F32) - dkdkd)
                exs_.append(jnp.sum(dkdkd, axis=0, keepdims=True)
                            + jnp.sum(d1 * s0, axis=0, keepdims=True) * d["ebl"])
            dg = _exact_mm(triu, jnp.concatenate(dbs_, axis=1)) + jnp.concatenate(exs_, axis=1)

            dlbs = []
            for hh, sl in enumerate(hs):
                g = gh[hh]
                df = dg[:, sl] / g["f"] - dks_[hh]
                sf = g["sf"]
                omsf = 1.0 - sf
                sq = g["sq"]
                dlbs.append(jnp.sum(df * omsf, axis=0, keepdims=True))
                dh_ref[rows, sl] = (dqs_[hh] * (sq * (1.0 + g["qraw"] * (1.0 - sq)))).astype(BF16)
                dh_ref[rows, D + sl.start:D + sl.stop] = (df * (1.0 - lbv[:, sl]) * sf * omsf).astype(BF16)
            acc_ref[0:1, :] += jnp.concatenate(dlbs, axis=1)

    def pcol(c):
        return pl.BlockSpec((HRB, D), lambda i: (nblk - 1 - i, c))

    vec = pl.BlockSpec((1, D), lambda i: (0, 0))
    row = pl.BlockSpec((HRB, D), lambda i: (nblk - 1 - i, 0))
    return pl.pallas_call(
        body, name="hgrn_bwd", grid=(nblk,),
        in_specs=[pcol(2), pcol(3), pcol(4), pcol(5), row, row,
                  pl.BlockSpec((cps, NH * HK, HK), lambda i: (nblk - 1 - i, 0, 0)), vec, vec],
        out_specs=[pl.BlockSpec((HRB, 4 * D), lambda i: (nblk - 1 - i, 0)),
                   pl.BlockSpec((8, D), lambda i: (0, 0))],
        out_shape=[jax.ShapeDtypeStruct((S, 4 * D), BF16), jax.ShapeDtypeStruct((8, D), F32)],
        scratch_shapes=[pltpu.VMEM((NH * HK, HK), F32)],
        compiler_params=_cp(),
    )(proj, proj, proj, proj, oraw, dob, states, lb, gn)


def _merge_call(oa, ob, proj, x, tgt, vecs, attn, wa, wb, wo, hsum):
    S = x.shape[0]
    tm = 256

    def body(oa_ref, ob_ref, ga_ref, gb_ref, x_ref, t_ref, v_ref, at_ref, za_ref, wa_ref, wb_ref, wo_ref, hs_ref,
             y_ref, dx2_ref, du_ref, dya_ref, dyb_ref, dg_ref, dat_ref, dsum_ref, dza_ref, dob_ref, acc_ref, ls_ref):
        @pl.when(pl.program_id(0) == 0)
        def _():
            acc_ref[...] = jnp.zeros_like(acc_ref)
            ls_ref[...] = jnp.zeros_like(ls_ref)

        gate = v_ref[0:1, :]
        fg = v_ref[1:2, :]
        ya = _nn(oa_ref[...], wa_ref[...])
        yb = _nn(ob_ref[...], wb_ref[...])
        sa = _sig(ga_ref[...])
        sb = _sig(gb_ref[...])
        y = (sa * ya + sb * yb).astype(BF16)
        y_ref[...] = y
        u = _nn(y, wo_ref[...])
        x2v = x_ref[...] + gate * u
        r = lax.rsqrt(jnp.mean(x2v * x2v, axis=-1, keepdims=True) + EPS)
        err = x2v * r * fg - t_ref[...]
        ls_ref[...] += jnp.sum(err * err)
        dout = err * (1.0 / D)
        gh = dout * fg
        dx2 = r * gh - x2v * (r * r * r * jnp.mean(gh * x2v, axis=-1, keepdims=True))
        acc_ref[0:1, :] += jnp.sum(dx2 * u, axis=0, keepdims=True)
        acc_ref[1:2, :] += jnp.sum(dout * x2v * r, axis=0, keepdims=True)
        dx2_ref[...] = dx2
        du = (dx2 * gate).astype(BF16)
        du_ref[...] = du
        dy = _nt(du, wo_ref[...])
        dya = (dy * sa).astype(BF16)
        dyb = (dy * sb).astype(BF16)
        dya_ref[...] = dya
        dyb_ref[...] = dyb
        dg_ref[:, 0:D] = (dy * ya * sa * (1.0 - sa)).astype(BF16)
        dg_ref[:, D:2 * D] = (dy * yb * sb * (1.0 - sb)).astype(BF16)
        doa = _nt(dya, wa_ref[...])
        dob_ref[...] = _nt(dyb, wb_ref[...])
        za = za_ref[...]
        sz = _sig(za)
        att = at_ref[...]
        dat = doa * (za * sz)
        dat_ref[...] = dat
        dza_ref[...] = (doa * att * (sz * (1.0 + za * (1.0 - sz)))).astype(BF16)
        dsum_ref[...] = _exact_mm_r(dat * att, hs_ref[...])

    row = pl.BlockSpec((tm, D), lambda i: (i, 0))
    arow = pl.BlockSpec((tm, AW), lambda i: (i, 0))
    full = lambda a: pl.BlockSpec(a.shape, lambda i: (0, 0))
    return pl.pallas_call(
        body, name="merge_fwd_bwd", grid=(S // tm,),
        in_specs=[arow, row, pl.BlockSpec((tm, D), lambda i: (i, 6)), pl.BlockSpec((tm, D), lambda i: (i, 7)),
                  row, row, pl.BlockSpec((8, D), lambda i: (0, 0)), arow, pl.BlockSpec((tm, AW), lambda i: (i, 3)),
                  full(wa), full(wb), full(wo), full(hsum)],
        out_specs=[row, row, row, row, row, pl.BlockSpec((tm, 2 * D), lambda i: (i, 0)),
                   arow, arow, arow, row, pl.BlockSpec((8, D), lambda i: (0, 0)),
                   pl.BlockSpec((8, 128), lambda i: (0, 0))],
        out_shape=[jax.ShapeDtypeStruct((S, D), BF16), jax.ShapeDtypeStruct((S, D), F32),
                   jax.ShapeDtypeStruct((S, D), BF16), jax.ShapeDtypeStruct((S, D), BF16),
                   jax.ShapeDtypeStruct((S, D), BF16), jax.ShapeDtypeStruct((S, 2 * D), BF16),
                   jax.ShapeDtypeStruct((S, AW), F32), jax.ShapeDtypeStruct((S, AW), F32),
                   jax.ShapeDtypeStruct((S, AW), BF16), jax.ShapeDtypeStruct((S, D), F32),
                   jax.ShapeDtypeStruct((8, D), F32), jax.ShapeDtypeStruct((8, 128), F32)],
        compiler_params=_cp(),
    )(oa, ob, proj, proj, x, tgt, vecs, attn, proj, wa, wb, wo, hsum)


def _atb_call(a, b, name):
    S, K = a.shape
    N = b.shape[1]
    tm = min(1024, S)

    def body(a_ref, b_ref, o_ref, ob_ref):
        @pl.when(pl.program_id(0) == 0)
        def _():
            o_ref[...] = jnp.zeros_like(o_ref)

        o_ref[...] += _tn(a_ref[...], b_ref[...])

        @pl.when(pl.program_id(0) == S // tm - 1)
        def _():
            ob_ref[...] = o_ref[...].astype(BF16)

    ospec = pl.BlockSpec((K, N), lambda i: (0, 0))
    return pl.pallas_call(
        body, name=name, grid=(S // tm,),
        in_specs=[pl.BlockSpec((tm, K), lambda i: (i, 0)), pl.BlockSpec((tm, N), lambda i: (i, 0))],
        out_specs=[ospec, ospec],
        out_shape=[jax.ShapeDtypeStruct((K, N), F32), jax.ShapeDtypeStruct((K, N), BF16)], compiler_params=_cp(),
    )(a, b)


def _dwin_call(h_t, dqkvz, d_hgrn, d_gates):
    S = h_t.shape[1]
    tm = 512
    tn = 2048

    def body(h_ref, q_ref, k_ref, v_ref, z_ref, m_ref, g_ref, o_ref, ob_ref):
        j = pl.program_id(0)

        @pl.when(pl.program_id(1) == 0)
        def _():
            o_ref[...] = jnp.zeros_like(o_ref)

        hv = h_ref[...]

        @pl.when(j == 0)
        def _():
            for cidx, r in enumerate((q_ref, k_ref, v_ref, z_ref)):
                o_ref[0, :, cidx * AW:(cidx + 1) * AW] += _nn(hv, r[...])

        @pl.when(jnp.logical_or(j == 1, j == 2))
        def _():
            o_ref[0] += _nn(hv, m_ref[...])

        @pl.when(j == 3)
        def _():
            o_ref[0] += _nn(hv, g_ref[...])

        @pl.when(pl.program_id(1) == S // tm - 1)
        def _():
            ob_ref[...] = o_ref[...].astype(BF16)

    aspec = pl.BlockSpec((tm, AW), lambda j, i: (jnp.where(j == 0, i, 0), 0))
    ospec = pl.BlockSpec((1, D, tn), lambda j, i: (j, 0, 0))
    return pl.pallas_call(
        body, name="dw_in", grid=(4, S // tm),
        in_specs=[pl.BlockSpec((D, tm), lambda j, i: (0, i)), aspec, aspec, aspec, aspec,
                  pl.BlockSpec((tm, tn), lambda j, i: (jnp.where(jnp.logical_or(j == 1, j == 2), i, 0),
                                                       jnp.where(j == 2, 1, 0))),
                  pl.BlockSpec((tm, tn), lambda j, i: (jnp.where(j == 3, i, 0), 0))],
        out_specs=[ospec, ospec],
        out_shape=[jax.ShapeDtypeStruct((4, D, tn), F32), jax.ShapeDtypeStruct((4, D, tn), BF16)],
        compiler_params=_cp(),
    )(h_t, *dqkvz, d_hgrn, d_gates)


def _dh_call(dqkvz, d_hgrn, d_gates, w_in, x, dx2, vecs):
    S = x.shape[0]
    tm = 512

    def body(q_ref, k_ref, v_ref, z_ref, m_ref, g_ref, w_ref, x_ref, dx2_ref, p_ref, gx_ref, acc_ref):
        @pl.when(pl.program_id(0) == 0)
        def _():
            acc_ref[...] = jnp.zeros_like(acc_ref)

        dhv = _nt(q_ref[...], w_ref[:, 0:AW])
        for cidx, r in enumerate((k_ref, v_ref, z_ref)):
            dhv += _nt(r[...], w_ref[:, (cidx + 1) * AW:(cidx + 2) * AW])
        dhv += _nt(m_ref[...], w_ref[:, 4 * AW:4 * AW + 4 * D])
        dhv += _nt(g_ref[...], w_ref[:, 4 * AW + 4 * D:NPROJ])
        xv = x_ref[...]
        r = lax.rsqrt(jnp.mean(xv * xv, axis=-1, keepdims=True) + EPS)
        xn = xv * r
        acc_ref[0:1, :] += jnp.sum(dhv, axis=0, keepdims=True)
        acc_ref[1:2, :] += jnp.sum(dhv * xn * p_ref[1:2, :], axis=0, keepdims=True)
        acc_ref[2:3, :] += jnp.sum(dhv * xn * p_ref[2:3, :], axis=0, keepdims=True)
        dxn = dhv * p_ref[0:1, :]
        gx_ref[...] = dx2_ref[...] + r * dxn - xv * (r * r * r * jnp.mean(dxn * xv, axis=-1, keepdims=True))

    row = pl.BlockSpec((tm, D), lambda i: (i, 0))
    aspec = pl.BlockSpec((tm, AW), lambda i: (i, 0))
    const = lambda shape: pl.BlockSpec(shape, lambda i: (0, 0))
    return pl.pallas_call(
        body, name="dh_gradx", grid=(S // tm,),
        in_specs=[aspec, aspec, aspec, aspec,
                  pl.BlockSpec((tm, 4 * D), lambda i: (i, 0)), pl.BlockSpec((tm, 2 * D), lambda i: (i, 0)),
                  pl.BlockSpec((D, NPROJ), lambda i: (0, 0), pipeline_mode=pl.Buffered(1)),
                  row, row, const((8, D))],
        out_specs=[row, const((8, D))],
        out_shape=[jax.ShapeDtypeStruct((S, D), F32), jax.ShapeDtypeStruct((8, D), F32)],
        compiler_params=_cp(),
    )(*dqkvz, d_hgrn, d_gates, w_in, x, dx2, vecs)


def _adamw_math(w, g, m, v):
    m = B1 * m + (1.0 - B1) * g
    v = B2 * v + (1.0 - B2) * (g * g)
    m_hat = m / (1.0 - B1 ** STEP)
    v_hat = v / (1.0 - B2 ** STEP)
    delta = -LR * (m_hat / (jnp.sqrt(v_hat) + AEPS) + WD * w)
    return delta, m, v


def _adamw_call(w, g, m, v, name):
    R, C = w.shape
    tr = R if R * C * 4 <= (1 << 20) else max(8, (1 << 20) // (C * 4))
    assert R % tr == 0

    def body(w_ref, g_ref, m_ref, v_ref, go_ref, d_ref, nm_ref, nv_ref):
        g = g_ref[...]
        go_ref[...] = g
        d_ref[...], nm_ref[...], nv_ref[...] = _adamw_math(w_ref[...], g, m_ref[...], v_ref[...])

    blk = pl.BlockSpec((tr, C), lambda i: (i, 0))
    return pl.pallas_call(
        body, name=name, grid=(R // tr,), in_specs=[blk] * 4, out_specs=[blk] * 4,
        out_shape=[jax.ShapeDtypeStruct((R, C), F32)] * 4, compiler_params=_cp(),
    )(w, g, m, v)


def _mod_call(c_all, w_ada_s, b_s):
    def body(c_ref, w_ref, b_ref, o_ref):
        cv = c_ref[...]
        sc = cv * _sig(cv)
        o_ref[...] = jnp.dot(sc, w_ref[...], preferred_element_type=F32,
                             precision=lax.Precision.HIGHEST) + b_ref[...]

    return pl.pallas_call(
        body, name="ada_mod", out_shape=jax.ShapeDtypeStruct((8, w_ada_s.shape[1]), F32),
        compiler_params=_cp(),
    )(c_all, w_ada_s, b_s)


def _ada_update_call(sct, dm, w, m, v):
    R, C = w.shape
    tr = 256

    def body(s_ref, d_ref, w_ref, m_ref, v_ref, g_ref, dl_ref, nm_ref, nv_ref):
        g = s_ref[:, 0:1] * d_ref[0:1, :]
        for b in range(1, 8):
            g = g + s_ref[:, b:b + 1] * d_ref[b:b + 1, :]
        g_ref[...] = g
        dl_ref[...], nm_ref[...], nv_ref[...] = _adamw_math(w_ref[...], g, m_ref[...], v_ref[...])

    blk = pl.BlockSpec((tr, C), lambda i: (i, 0))
    return pl.pallas_call(
        body, name="ada_update", grid=(R // tr,),
        in_specs=[pl.BlockSpec((tr, 8), lambda i: (i, 0)), pl.BlockSpec((8, C), lambda i: (0, 0)), blk, blk, blk],
        out_specs=[blk] * 4, out_shape=[jax.ShapeDtypeStruct((R, C), F32)] * 4, compiler_params=_cp(),
    )(sct, dm, w, m, v)


def _sum8_call(packs):
    def body(p_ref, o_ref):
        acc = p_ref[0]
        for k in range(1, 8):
            acc = acc + p_ref[k]
        o_ref[...] = acc

    return pl.pallas_call(
        body, name="sum_small", out_shape=jax.ShapeDtypeStruct(packs.shape[1:], F32), compiler_params=_cp(),
    )(packs)


def _local_step(x, tgt, shift, scale, gate, norm_g, hgrn_onorm_g, rel_bias, lb, final_g, weights_fn, hook=None):
    a = norm_g * (1.0 + scale)
    z6 = jnp.zeros((6, D), F32)
    h, h_t, proj, w_in, rest_fn = weights_fn(x, jnp.concatenate([a, shift, z6], 0))

    biases = _bias_tiles(rel_bias) * LOG2E
    attn, lse, oa = _attn_fwd_call(proj, biases)

    gn = jnp.tile(hgrn_onorm_g, (1, NH))
    oraw, ob, states = _hgrn_fwd_call(proj, lb, gn)

    wa, wb, wo = rest_fn(ob)
    vecs2 = jnp.concatenate([gate, final_g, z6], 0)
    hsum = jnp.asarray(np.kron(np.eye(NH), np.ones((HE, HE))), BF16)
    y, dx2, du, dya, dyb, d_gates, dattn, dsum, dza, dob, acc2, lsq = _merge_call(
        oa, ob, proj, x, tgt, vecs2, attn, wa, wb, wo, hsum)
    loss = 0.5 * lsq[0, 0] / D
    d_wo, d_wo16 = _atb_call(y, du, "dw_out")
    d_wa, d_wa16 = _atb_call(oa, dya, "dw_branch_a")
    d_wb, d_wb16 = _atb_call(ob, dyb, "dw_branch_b")

    d_hgrn, acch = _hgrn_bwd_call(proj, oraw, dob, states, lb, gn)

    dq, dk, dv, dbs = _attn_bwd_call(proj, dattn, lse, dsum, biases)
    dqkvz = (dq, dk, dv, dza)

    d_win, d_win16 = _dwin_call(h_t, dqkvz, d_hgrn, d_gates)
    tok = hook((d_win, d_wa, d_wb, d_wo), (d_win16, d_wa16, d_wb16, d_wo16)) if hook is not None else 0.0
    one_scale = 1.0 + scale
    grad_x, acc1 = _dh_call(dqkvz, d_hgrn, d_gates, w_in, x, dx2,
                            jnp.concatenate([a + tok, norm_g, one_scale, jnp.zeros((5, D), F32)], 0))

    d_rel = jnp.zeros((NBUCKETS, NH), F32)
    for p, d in enumerate(PATTERNS):
        band, bucket = _band_bucket(d)
        onehot = (bucket[None] == np.arange(NBUCKETS)[:, None, None]) & band[None]
        d_rel = d_rel + jnp.einsum("hqk,bqk->bh", dbs[p], jnp.asarray(onehot, F32),
                                   precision=lax.Precision.HIGHEST)
    d_onorm = jnp.sum(acch[1].reshape(NH, HK), axis=0)

    zrow = jnp.zeros((D,), F32)
    pack = jnp.stack([acc1[0], acc1[1], acc2[0], acc1[2], acc2[1], acch[0],
                      zrow.at[:HK].set(d_onorm), zrow.at[0].set(loss),
                      zrow.at[:NBUCKETS * NH].set(d_rel.reshape(-1))] + [zrow] * 7, 0)
    return grad_x, d_win, d_wa, d_wb, d_wo, pack


def _me():
    return lax.axis_index("x"), lax.axis_index("y"), lax.axis_index("c")


def _peers(x, y):
    return [(1 - x, y), (x, 1 - y), (1 - x, 1 - y)]


def _allgather_small(blk, name):
    m_per, n = blk.shape

    def body(x_ref, out_ref, send_sems, recv_sems, local_sem):
        x, y, c = _me()
        me, sibling = (x, y, c), (x, y, 1 - c)
        chips = _peers(x, y)

        def rows(px, py, pc):
            return out_ref.at[pl.ds((4 * px + 2 * py + pc) * m_per, m_per), :]

        def copy(k, block, to, src=None):
            return pltpu.make_async_remote_copy(
                src_ref=rows(*block) if src is None else src, dst_ref=rows(*block),
                send_sem=send_sems.at[k], recv_sem=recv_sems.at[k], device_id=to, device_id_type=MESH)

        mine = pltpu.make_async_copy(x_ref, rows(*me), local_sem)
        mine.start()
        first = [copy(0, me, sibling, src=x_ref)]
        first += [copy(1 + j, me, (*chip, c), src=x_ref) for j, chip in enumerate(chips)]
        for cp in first:
            cp.start()
        passed = [copy(4 + j, (*chip, c), sibling) for j, chip in enumerate(chips)]
        for j, chip in enumerate(chips):
            copy(1 + j, (*chip, c), me).wait_recv()
            passed[j].start()
        copy(0, sibling, me).wait_recv()
        for j, chip in enumerate(chips):
            copy(4 + j, (*chip, 1 - c), me).wait_recv()
        for cp in first + passed:
            cp.wait_send()
        mine.wait()

    return pl.pallas_call(
        body, name=name, out_shape=jax.ShapeDtypeStruct((8 * m_per, n), blk.dtype),
        in_specs=[pl.BlockSpec(memory_space=pltpu.VMEM)], out_specs=pl.BlockSpec(memory_space=pltpu.VMEM),
        scratch_shapes=[pltpu.SemaphoreType.DMA((7,)), pltpu.SemaphoreType.DMA((7,)), pltpu.SemaphoreType.DMA],
    )(blk)


ANY = pl.BlockSpec(memory_space=pl.ANY)


HBM = pl.BlockSpec(memory_space=pltpu.HBM)
SEM = pl.BlockSpec(memory_space=pltpu.SEMAPHORE)
EFFECT = pltpu.SideEffectType.DATAFLOW_SIDE_EFFECTING


def _w_part(t, ref, j, half):
    if t == 0:
        return ref.at[pl.ds(half * (D // 2), D // 2), pl.ds(j * 2048, 2048)]
    if t == 1:
        return ref.at[pl.ds(half * (AW // 2), AW // 2), pl.ds(j * 256, 256)]
    return ref.at[pl.ds(j * 256 + half * 128, 128), :]


def _w_ici_copies(fulls, send_sems, recv_sems, tensors=(0, 1, 2, 3), peers=(0, 1, 2)):
    x, y, c = _me()
    outs, ins = [], []
    for u, t in enumerate(tensors):
        for k, chip in enumerate(_peers(x, y)):
            if k not in peers:
                continue
            mine = _w_part(t, fulls[u], 2 * x + y, c)
            theirs = _w_part(t, fulls[u], 2 * chip[0] + chip[1], c)
            kw = dict(send_sem=send_sems.at[3 * t + k], recv_sem=recv_sems.at[3 * t + k],
                      device_id=(*chip, c), device_id_type=MESH)
            outs.append(pltpu.make_async_remote_copy(src_ref=mine, dst_ref=mine, **kw))
            ins.append(pltpu.make_async_remote_copy(src_ref=theirs, dst_ref=theirs, **kw))
    return outs, ins


def _gather_ici_start(fulls):
    def body(f0, f1, f2, f3, send_sems, recv_sems, t0, t1, t2, t3, token):
        for cp in _w_ici_copies([f0, f1, f2, f3], send_sems, recv_sems)[0]:
            cp.start()
        token[...] = jnp.zeros_like(token)

    res = pl.pallas_call(
        body, name="gather_ici_start",
        out_shape=(pltpu.SemaphoreType.DMA((12,)), pltpu.SemaphoreType.DMA((12,)),
                   *[pltpu.HBM(a.shape, a.dtype) for a in fulls], jax.ShapeDtypeStruct((8, 128), F32)),
        in_specs=[HBM] * 4, out_specs=(SEM, SEM, *[HBM] * 4, pl.BlockSpec(memory_space=pltpu.VMEM)),
        input_output_aliases={i: 2 + i for i in range(4)},
        compiler_params=pltpu.CompilerParams(has_side_effects=EFFECT),
    )(*[pltpu.with_memory_space_constraint(a, pltpu.HBM) for a in fulls])
    return res[0], res[1], list(res[2:6]), res[6]


def _gather_ici_wait(send_sems, recv_sems, bufs, after, tensors, name, peers=(0, 1, 2)):
    n = len(tensors)

    def body(*refs):
        outs, ins = _w_ici_copies(refs[0:n], refs[n], refs[n + 1], tensors, peers)
        for cp in outs:
            cp.wait_send()
        for cp in ins:
            cp.wait_recv()

    return pl.pallas_call(
        body, name=name, out_shape=tuple(pltpu.HBM(a.shape, a.dtype) for a in bufs),
        in_specs=[HBM] * n + [SEM, SEM, ANY], out_specs=[HBM] * n,
        input_output_aliases={i: i for i in range(n)},
        compiler_params=pltpu.CompilerParams(has_side_effects=EFFECT),
    )(*bufs, send_sems, recv_sems, after)


def _gather_sibling(bufs, tensors, name, peers=(0, 1, 2)):
    n = len(tensors)
    np_ = len(peers)

    def body(*refs):
        outs, send_sems, recv_sems = refs[n:2 * n], refs[2 * n], refs[2 * n + 1]
        x, y, c = _me()
        cps = []
        chips = _peers(x, y)
        for u, t in enumerate(tensors):
            for v, k in enumerate(peers):
                blk = _w_part(t, outs[u], 2 * chips[k][0] + chips[k][1], c)
                cp = pltpu.make_async_remote_copy(
                    src_ref=blk, dst_ref=blk, send_sem=send_sems.at[u, v], recv_sem=recv_sems.at[u, v],
                    device_id=(x, y, 1 - c), device_id_type=MESH)
                cp.start()
                cps.append(cp)
        for u, t in enumerate(tensors):
            for v, k in enumerate(peers):
                blk = _w_part(t, outs[u], 2 * chips[k][0] + chips[k][1], 1 - c)
                pltpu.make_async_remote_copy(
                    src_ref=blk, dst_ref=blk, send_sem=send_sems.at[u, v], recv_sem=recv_sems.at[u, v],
                    device_id=(x, y, 1 - c), device_id_type=MESH).wait_recv()
        for cp in cps:
            cp.wait_send()

    return pl.pallas_call(
        body, name=name,
        out_shape=[jax.ShapeDtypeStruct(FULL_W_SHAPES[t], BF16) for t in tensors],
        in_specs=[ANY] * n, out_specs=[ANY] * n, input_output_aliases={u: u for u in range(n)},
        scratch_shapes=[pltpu.SemaphoreType.DMA((n, np_)), pltpu.SemaphoreType.DMA((n, np_))],
    )(*bufs)


def _half_of(t, ref, half):
    if t == 0:
        return ref.at[:, pl.ds(half * 512, 512), :]
    if t == 1:
        return ref.at[pl.ds(half * 256, 256), :]
    return ref.at[:, pl.ds(half * 512, 512)]


HALF_SHAPES = [(4, 512, 2048), (256, D), (D, 512), (D, 512)]
PIECE_SHAPES = [(512, 2048), (256, 256), (256, 512), (256, 512)]
SHARD_SHAPES = [(D, 2048), (AW, 256), (256, D), (256, D)]


def _chip_piece(t, ref, j):
    if t == 0:
        return ref.at[j]
    if t == 1:
        return ref.at[:, pl.ds(j * 256, 256)]
    return ref.at[pl.ds(j * 256, 256), :]


def _reduce_sibling_send(gs):
    def body(g0, g1, g2, g3, r0, r1, r2, r3, send_sems, recv_sems):
        x, y, c = _me()
        ins, outs = [g0, g1, g2, g3], [r0, r1, r2, r3]
        cps = []
        for t in range(4):
            cp = pltpu.make_async_remote_copy(
                src_ref=_half_of(t, ins[t], 1 - c), dst_ref=outs[t],
                send_sem=send_sems.at[t], recv_sem=recv_sems.at[t], device_id=(x, y, 1 - c), device_id_type=MESH)
            cp.start()
            cps.append(cp)
        for cp in cps:
            cp.wait_recv()
        for cp in cps:
            cp.wait_send()

    return pl.pallas_call(
        body, name="reduce_sibling", out_shape=[jax.ShapeDtypeStruct(s, BF16) for s in HALF_SHAPES],
        in_specs=[ANY] * 4, out_specs=[ANY] * 4,
        scratch_shapes=[pltpu.SemaphoreType.DMA((4,)), pltpu.SemaphoreType.DMA((4,))],
    )(*gs)


def _chip_copies(hs, lands, send_sems, recv_sems):
    x, y, c = _me()
    cps = []
    for t in range(4):
        for k, chip in enumerate(_peers(x, y)):
            pj = 2 * chip[0] + chip[1]
            cps.append(pltpu.make_async_remote_copy(
                src_ref=_chip_piece(t, hs[t], pj), dst_ref=lands[t].at[k],
                send_sem=send_sems.at[3 * t + k], recv_sem=recv_sems.at[3 * t + k],
                device_id=(*chip, c), device_id_type=MESH))
    return cps


def _reduce_chips_start(hs):
    lands = [lax.empty((3,) + s, BF16) for s in PIECE_SHAPES]

    def body(h0, h1, h2, h3, l0, l1, l2, l3, send_sems, recv_sems, t0, t1, t2, t3, t4, t5, t6, t7, token):
        for cp in _chip_copies([h0, h1, h2, h3], [l0, l1, l2, l3], send_sems, recv_sems):
            cp.start()
        token[...] = jnp.zeros_like(token)

    bufs = list(hs) + lands
    res = pl.pallas_call(
        body, name="reduce_chips_start",
        out_shape=(pltpu.SemaphoreType.DMA((12,)), pltpu.SemaphoreType.DMA((12,)),
                   *[pltpu.HBM(a.shape, a.dtype) for a in bufs], jax.ShapeDtypeStruct((8, 128), F32)),
        in_specs=[HBM] * 8, out_specs=(SEM, SEM, *[HBM] * 8, pl.BlockSpec(memory_space=pltpu.VMEM)),
        input_output_aliases={i: 2 + i for i in range(8)},
        compiler_params=pltpu.CompilerParams(has_side_effects=EFFECT),
    )(*[pltpu.with_memory_space_constraint(a, pltpu.HBM) for a in bufs])
    return res[0], res[1], list(res[2:10]), res[10]


def _reduce_chips_wait(send_sems, recv_sems, thru, after):
    def body(h0, h1, h2, h3, l0, l1, l2, l3, send_sems, recv_sems, after_ref, d0, d1, d2, d3, g0, g1, g2, g3):
        cps = _chip_copies([h0, h1, h2, h3], [l0, l1, l2, l3], send_sems, recv_sems)
        for cp in cps:
            cp.wait_send()
        for cp in cps:
            cp.wait_recv()

    res = pl.pallas_call(
        body, name="reduce_chips_wait", out_shape=tuple(pltpu.HBM(a.shape, a.dtype) for a in thru),
        in_specs=[HBM] * 8 + [SEM, SEM, ANY], out_specs=[HBM] * 8,
        input_output_aliases={i: i for i in range(8)},
        compiler_params=pltpu.CompilerParams(has_side_effects=EFFECT),
    )(*thru, send_sems, recv_sems, after)
    return list(res[4:8])


def _share_sibling(shards):
    def body(i0, i1, i2, i3, o0, o1, o2, o3, send_sems, recv_sems):
        x, y, c = _me()
        outs = [o0, o1, o2, o3]

        def half(t, ref, hf):
            if t == 0:
                return ref.at[pl.ds(hf * 512, 512), :]
            if t == 1:
                return ref.at[pl.ds(hf * 256, 256), :]
            return ref.at[:, pl.ds(hf * 512, 512)]

        cps = []
        for t in range(4):
            mine = half(t, outs[t], c)
            cp = pltpu.make_async_remote_copy(
                src_ref=mine, dst_ref=mine, send_sem=send_sems.at[t], recv_sem=recv_sems.at[t],
                device_id=(x, y, 1 - c), device_id_type=MESH)
            cp.start()
            cps.append(cp)
        for t in range(4):
            theirs = half(t, outs[t], 1 - c)
            pltpu.make_async_remote_copy(
                src_ref=theirs, dst_ref=theirs, send_sem=send_sems.at[t],
                recv_sem=recv_sems.at[t], device_id=(x, y, 1 - c), device_id_type=MESH).wait_recv()
        for cp in cps:
            cp.wait_send()

    return pl.pallas_call(
        body, name="share_sibling", out_shape=[jax.ShapeDtypeStruct(s, F32) for s in SHARD_SHAPES],
        in_specs=[ANY] * 4, out_specs=[ANY] * 4, input_output_aliases={0: 0, 1: 1, 2: 2, 3: 3},
        scratch_shapes=[pltpu.SemaphoreType.DMA((4,)), pltpu.SemaphoreType.DMA((4,))],
    )(*shards)


def _half_blockspec(t, idx_pos):
    if t == 0:
        return pl.BlockSpec((1, 512, 2048), lambda i, s: (i, s[idx_pos], 0)), 4
    if t == 1:
        return pl.BlockSpec((256, D), lambda i, s: (s[idx_pos], 0)), 1
    return pl.BlockSpec((256, 512), lambda i, s: (i, s[idx_pos])), 4


def _half_out_blockspec(t):
    if t == 0:
        return pl.BlockSpec((1, 512, 2048), lambda i, s: (i, 0, 0))
    if t == 1:
        return pl.BlockSpec((256, D), lambda i, s: (0, 0))
    return pl.BlockSpec((256, 512), lambda i, s: (i, 0))


def _add_half_call(t, own, recv, sc, name):
    in_blk, steps = _half_blockspec(t, 0)
    out_blk = _half_out_blockspec(t)

    def body(s_ref, a_ref, b_ref, o_ref, ob_ref):
        v = a_ref[...] + b_ref[...].astype(F32)
        o_ref[...] = v
        ob_ref[...] = v.astype(BF16)

    return pl.pallas_call(
        body, name=name,
        grid_spec=pltpu.PrefetchScalarGridSpec(
            num_scalar_prefetch=1, grid=(steps,), in_specs=[in_blk, out_blk], out_specs=[out_blk, out_blk]),
        out_shape=[jax.ShapeDtypeStruct(HALF_SHAPES[t], F32), jax.ShapeDtypeStruct(HALF_SHAPES[t], BF16)],
        compiler_params=_cp(),
    )(sc, own, recv)


def _final_piece_call(t, chipsum, recv3, sc, name):
    ps = PIECE_SHAPES[t]
    if t == 0:
        own_blk = pl.BlockSpec((1,) + ps, lambda i, s: (s[1], 0, 0))
        o_blk = pl.BlockSpec(ps, lambda i, s: (s[0], 0))
    elif t == 1:
        own_blk = pl.BlockSpec(ps, lambda i, s: (0, s[1]))
        o_blk = pl.BlockSpec(ps, lambda i, s: (s[0], 0))
    else:
        own_blk = pl.BlockSpec(ps, lambda i, s: (s[1], 0))
        o_blk = pl.BlockSpec(ps, lambda i, s: (0, s[0]))
    r_blk = pl.BlockSpec((3,) + ps, lambda i, s: (0, 0, 0))

    def body(s_ref, a_ref, r_ref, o_ref):
        a = a_ref[0] if t == 0 else a_ref[...]
        o_ref[...] = ((a + r_ref[0].astype(F32)) + r_ref[1].astype(F32)) + r_ref[2].astype(F32)

    return pl.pallas_call(
        body, name=name,
        grid_spec=pltpu.PrefetchScalarGridSpec(
            num_scalar_prefetch=1, grid=(1,), in_specs=[own_blk, r_blk], out_specs=o_blk),
        out_shape=jax.ShapeDtypeStruct(SHARD_SHAPES[t], F32), compiler_params=_cp(),
    )(sc, chipsum, recv3)


FULL_W_SHAPES = [(D, NPROJ), (AW, D), (D, D), (D, D)]


def _cast_place_call(t, shard, sc, name):
    if t == 0:
        blk, steps = (512, 2048), 2
        in_blk = pl.BlockSpec(blk, lambda i, s: (i, 0))
        o_blk = pl.BlockSpec(blk, lambda i, s: (i, s[1]))
    elif t == 1:
        blk, steps = (AW, 256), 1
        in_blk = pl.BlockSpec(blk, lambda i, s: (0, 0))
        o_blk = pl.BlockSpec(blk, lambda i, s: (0, s[1]))
    else:
        blk, steps = (256, D), 1
        in_blk = pl.BlockSpec(blk, lambda i, s: (0, 0))
        o_blk = pl.BlockSpec(blk, lambda i, s: (s[1], 0))

    def body(s_ref, a_ref, o_ref, own_ref):
        v = a_ref[...].astype(BF16)
        o_ref[...] = v
        own_ref[...] = v

    return pl.pallas_call(
        body, name=name,
        grid_spec=pltpu.PrefetchScalarGridSpec(
            num_scalar_prefetch=1, grid=(steps,), in_specs=[in_blk], out_specs=[o_blk, in_blk]),
        out_shape=[jax.ShapeDtypeStruct(FULL_W_SHAPES[t], BF16), jax.ShapeDtypeStruct(shard.shape, BF16)],
        compiler_params=_cp(),
    )(sc, shard)


def _lower_bound_fn(hgrn_lb):
    return jnp.cumsum(jax.nn.softmax(hgrn_lb.astype(F32), axis=0), axis=0)[0]


def kernel(x, c, w_ada, b_ada, norm_g, w_in, hgrn_onorm_g, w_branch_a, w_branch_b, w_out, rel_bias, hgrn_lb, final_g, loss_target, m_w_ada, m_b_ada, m_norm_g, m_w_in, m_hgrn_onorm_g, m_w_branch_a, m_w_branch_b, m_w_out, m_rel_bias, m_hgrn_lb, m_final_g, v_w_ada, v_b_ada, v_norm_g, v_w_in, v_hgrn_onorm_g, v_w_branch_a, v_w_branch_b, v_w_out, v_rel_bias, v_hgrn_lb, v_final_g):
    ax, ay, ac = _me()
    chip = 2 * ax + ay
    dev = 4 * ax + 2 * ay + ac
    sc_idx = jnp.stack([ac, chip]).astype(jnp.int32)

    c_all = _allgather_small(jnp.pad(c, ((0, 7), (0, 0))), "gather_c").reshape(8, 8, D)[:, 0]
    b_s = lax.dynamic_slice(b_ada, (0, chip * 768), (1, 768))
    mod_part = _mod_call(c_all, w_ada[0], b_s)
    mod_all = _allgather_small(mod_part, "gather_mod").reshape(8, 8, 768)
    mod_mine = lax.dynamic_index_in_dim(mod_all, dev, axis=1, keepdims=False)
    mod = mod_mine[0::2].reshape(1, 3 * D)

    names = ["w_in", "w_a", "w_b", "w_o"]
    shards, mod = lax.optimization_barrier(([w_in[0], w_branch_a[0], w_branch_b[0], w_out[0]], mod))
    placed = [_cast_place_call(t, shards[t], sc_idx, "cast_" + names[t]) for t in range(4)]
    w_send_sems, w_recv_sems, w_thru, w_token = _gather_ici_start([p_[0] for p_ in placed])
    mod = mod + w_token[0, 0]
    rel_bias_t = rel_bias + w_token[0, 0]
    shift, scale, gate = mod[:, :D], mod[:, D:2 * D], mod[:, 2 * D:]

    def weights_fn(xs, avec):
        proj, h, h_t = _proj_own_call(xs, avec, placed[0][1], sc_idx)
        near = _gather_ici_wait(w_send_sems, w_recv_sems, w_thru[0:1], proj, (0,), "gather_ici_wait_xy", (0, 1))
        near = _gather_sibling(near, (0,), "gather_sibling_xy", (0, 1))
        proj = _proj_rest_call(h, near[0], proj, sc_idx, (2, 1), "in_proj_xy")
        far = _gather_ici_wait(w_send_sems, w_recv_sems, near, proj, (0,), "gather_ici_wait_diag", (2,))
        (win_f,) = _gather_sibling(far, (0,), "gather_sibling_diag", (2,))

        def rest_fn(after):
            late = _gather_ici_wait(w_send_sems, w_recv_sems, w_thru[1:4], after, (1, 2, 3), "gather_ici_wait_rest")
            return _gather_sibling(late, (1, 2, 3), "gather_sibling_rest")

        return h, h_t, _proj_rest_call(h, win_f, proj, sc_idx, (3,), "in_proj_diag"), win_f, rest_fn

    flight = {}

    def start_reduction(own, own16):
        sib = _reduce_sibling_send(own16)
        halves = [_add_half_call(t, own[t], sib[t], sc_idx, "chipsum_" + names[t]) for t in range(4)]
        send_sems, recv_sems, thru, token = _reduce_chips_start([hb for _, hb in halves])
        flight.update(sems=(send_sems, recv_sems), thru=thru, sums=[hf for hf, _ in halves])
        return token[0, 0]

    lb, lb_vjp = jax.vjp(_lower_bound_fn, hgrn_lb)
    grad_x, d_win, d_wa, d_wb, d_wo, pack = _local_step(
        x[0], loss_target[0], shift, scale, gate, norm_g, hgrn_onorm_g, rel_bias_t, lb[None, :],
        final_g[None, :], weights_fn, hook=start_reduction)

    rec = _reduce_chips_wait(*flight["sems"], flight["thru"], pack)
    pieces = [_final_piece_call(t, flight["sums"][t], rec[t], sc_idx, "piece_" + names[t]) for t in range(4)]
    g_win, g_wa, g_wb, g_wo = _share_sibling(pieces)

    packs = _allgather_small(pack, "gather_small").reshape(8, 16, D)
    tot = _sum8_call(packs)
    loss = tot[7, 0]
    g_b_ada = tot[0:3].reshape(1, 3 * D)
    g_norm_g = tot[3:4]
    g_final_g = tot[4]
    (g_hgrn_lb,) = lb_vjp(tot[5])
    g_onorm = tot[6:7, :HK]
    g_rel = tot[8, :NBUCKETS * NH].reshape(NBUCKETS, NH)

    def rows_of(a):
        flat = a.reshape(-1)
        n = -(-flat.shape[0] // D)
        return jnp.pad(flat, (0, n * D - flat.shape[0])).reshape(n, D)

    smalls = [(b_ada, g_b_ada, m_b_ada, v_b_ada), (norm_g, g_norm_g, m_norm_g, v_norm_g),
              (hgrn_onorm_g, g_onorm, m_hgrn_onorm_g, v_hgrn_onorm_g), (rel_bias, g_rel, m_rel_bias, v_rel_bias),
              (hgrn_lb, g_hgrn_lb, m_hgrn_lb, v_hgrn_lb), (final_g, g_final_g, m_final_g, v_final_g)]
    cat = [jnp.concatenate([rows_of(s[k]) for s in smalls], 0) for k in range(4)]
    cat = [jnp.pad(a, ((0, 16 - a.shape[0]), (0, 0))) for a in cat]
    _, sd, sm, sv = _adamw_call(*cat, "adamw_small")

    def unpack(packed):
        res, r = [], 0
        for s in smalls:
            n = -(-s[0].size // D)
            res.append(packed[r:r + n].reshape(-1)[:s[0].size].reshape(s[0].shape))
            r += n
        return res

    d_small, m_small, v_small = unpack(sd), unpack(sm), unpack(sv)

    sc_all = c_all * jax.nn.sigmoid(c_all)
    dmod_all = packs[:, 0:3].reshape(8, 3 * D)
    dm_s = lax.dynamic_slice(dmod_all, (0, chip * 768), (8, 768))
    g_w_ada, d_w_ada, nm_w_ada, nv_w_ada = _ada_update_call(sc_all.T, dm_s, w_ada[0], m_w_ada[0], v_w_ada[0])

    big = []
    for w, g, m, v, n in [(w_in, g_win, m_w_in, v_w_in, "w_in"), (w_branch_a, g_wa, m_w_branch_a, v_w_branch_a, "w_a"),
                          (w_branch_b, g_wb, m_w_branch_b, v_w_branch_b, "w_b"), (w_out, g_wo, m_w_out, v_w_out, "w_o")]:
        big.append(_adamw_call(w[0], g, m[0], v[0], "adamw_" + n))

    e = lambda a: a[None]
    grads = [e(g_w_ada), g_b_ada, g_norm_g, e(big[0][0]), g_onorm, e(big[1][0]), e(big[2][0]), e(big[3][0]),
             g_rel, g_hgrn_lb, g_final_g]
    deltas = [e(d_w_ada), d_small[0], d_small[1], e(big[0][1]), d_small[2], e(big[1][1]), e(big[2][1]), e(big[3][1]),
              d_small[3], d_small[4], d_small[5]]
    new_m = [e(nm_w_ada), m_small[0], m_small[1], e(big[0][2]), m_small[2], e(big[1][2]), e(big[2][2]), e(big[3][2]),
             m_small[3], m_small[4], m_small[5]]
    new_v = [e(nv_w_ada), v_small[0], v_small[1], e(big[0][3]), v_small[2], e(big[1][3]), e(big[2][3]), e(big[3][3]),
             v_small[3], v_small[4], v_small[5]]
    return (loss, grad_x[None], *grads, *deltas, *new_m, *new_v)
```

```python
import math

import numpy as np
import jax
import jax.numpy as jnp
from jax import lax
from jax.experimental import pallas as pl
from jax.experimental.pallas import tpu as pltpu

D = 1024
AW = 512
NH = 8
HE = 64
HK = 128
NPROJ = 8192
ABLK = 128
PATTERNS = (1, 4, 16)
NBUCKETS = 32
MAXDIST = 2048
NEG = -1e30
EPS = 1e-6
CH = 64
LR, B1, B2, AEPS, WD, STEP = 0.001, 0.9, 0.999, 1e-08, 0.01, 10

F32 = jnp.float32
BF16 = jnp.bfloat16
MESH = pl.DeviceIdType.MESH
VMEM_LIMIT = 56 * 1024 * 1024


def _cp(**kw):
    return pltpu.CompilerParams(vmem_limit_bytes=VMEM_LIMIT, **kw)


def _sig(x):
    return 0.5 * jnp.tanh(0.5 * x) + 0.5


def _nt(a, b):
    return lax.dot_general(a, b, (((1,), (1,)), ((), ())), preferred_element_type=F32)


def _tn(a, b):
    return lax.dot_general(a, b, (((0,), (0,)), ((), ())), preferred_element_type=F32)


def _nn(a, b):
    return jnp.dot(a, b, preferred_element_type=F32)


def _split2(x):
    h = x.astype(BF16)
    return h, (x - h.astype(F32)).astype(BF16)


def _exact_mm(tri_bf16, x):
    h, l = _split2(x)
    return _nn(tri_bf16, h) + _nn(tri_bf16, l)


def _exact_mm_r(x, ones_bf16):
    h, l = _split2(x)
    return _nn(h, ones_bf16) + _nn(l, ones_bf16)


def _proj_own_call(x, avec, w_own, sc):
    S = x.shape[0]
    tm, tn = 512, 2048

    def body(s_ref, x_ref, a_ref, w_ref, o_ref, h_ref, ht_ref):
        xv = x_ref[...]
        r = lax.rsqrt(jnp.mean(xv * xv, axis=-1, keepdims=True) + EPS)
        hv = xv * r * a_ref[0:1, :] + a_ref[1:2, :]
        hb = hv.astype(BF16)
        h_ref[...] = hb
        ht_ref[...] = hv.T.astype(BF16)
        o_ref[...] = _nn(hb, w_ref[...])

    return pl.pallas_call(
        body, name="in_proj_own",
        grid_spec=pltpu.PrefetchScalarGridSpec(
            num_scalar_prefetch=1, grid=(S // tm,),
            in_specs=[pl.BlockSpec((tm, D), lambda i, s: (i, 0)), pl.BlockSpec((8, D), lambda i, s: (0, 0)),
                      pl.BlockSpec((D, tn), lambda i, s: (0, 0))],
            out_specs=[pl.BlockSpec((tm, tn), lambda i, s: (i, s[1])), pl.BlockSpec((tm, D), lambda i, s: (i, 0)),
                       pl.BlockSpec((D, tm), lambda i, s: (0, i))]),
        out_shape=[jax.ShapeDtypeStruct((S, NPROJ), F32), jax.ShapeDtypeStruct((S, D), BF16),
                   jax.ShapeDtypeStruct((D, S), BF16)],
        compiler_params=_cp(),
    )(sc, x, avec, w_own)


def _proj_rest_call(h, w_in, proj, sc):
    S = h.shape[0]
    tm, tn = 512, 2048

    def body(s_ref, h_ref, w_ref, p_ref, o_ref):
        o_ref[...] = _nn(h_ref[...], w_ref[...])

    col = lambda j, s: (s[1] + 1 + j) % 4
    return pl.pallas_call(
        body, name="in_proj_rest",
        grid_spec=pltpu.PrefetchScalarGridSpec(
            num_scalar_prefetch=1, grid=(3, S // tm),
            in_specs=[pl.BlockSpec((tm, D), lambda j, i, s: (i, 0)),
                      pl.BlockSpec((D, tn), lambda j, i, s: (0, col(j, s))),
                      pl.BlockSpec(memory_space=pl.ANY)],
            out_specs=pl.BlockSpec((tm, tn), lambda j, i, s: (i, col(j, s)))),
        out_shape=jax.ShapeDtypeStruct((S, NPROJ), F32), input_output_aliases={3: 0}, compiler_params=_cp(),
    )(sc, h, w_in, proj)


def _t5_bucket_np(dist):
    max_exact = NBUCKETS // 2
    n = dist.astype(np.float32)
    large = max_exact + (np.log(np.maximum(n, np.float32(1.0)) / np.float32(max_exact))
                         / np.float32(math.log(MAXDIST / max_exact))
                         * np.float32(NBUCKETS - max_exact)).astype(np.int32)
    large = np.minimum(large, NBUCKETS - 1)
    return np.where(dist < max_exact, dist, large)


def _band_bucket(d):
    qi = np.arange(ABLK)[:, None]
    kj = np.arange(2 * ABLK)[None, :]
    delta = qi + ABLK - kj
    band = (delta >= 0) & (delta <= ABLK)
    bucket = _t5_bucket_np(np.clip(delta, 0, None) * d)
    return band, bucket


def _bias_tiles(rel_bias):
    tiles = []
    for d in PATTERNS:
        band, bucket = _band_bucket(d)
        onehot = (jnp.asarray(bucket, jnp.int32)[None] == jnp.arange(NBUCKETS, dtype=jnp.int32)[:, None, None])
        bias = jnp.einsum("bqk,bh->hqk", onehot.astype(F32), rel_bias, precision=lax.Precision.HIGHEST)
        tiles.append(jnp.where(jnp.asarray(band)[None], bias, NEG))
    return jnp.stack(tiles, 0)


ATT = 2048
HP = 2 * HE
LOG2E = 1.4426950408889634
LN2 = 0.6931471805599453
QSCALE2 = (HE ** -0.5) * LOG2E
AGRP = 16
AGRP_B = 4


def _attn_blocks():
    out = []
    for p, d in enumerate(PATTERNS):
        for r in range(d):
            for n in range(ATT // (d * ABLK)):
                out.append((p, d, r, n))
    return out


def _attn_fwd_call(proj, biases):
    S = proj.shape[0]
    nt = S // ATT

    def body(q_ref, k_ref, v_ref, z_ref, b_ref, a_ref, l_ref, oa_ref, kc, vc, op, lp):
        i = pl.program_id(1)

        @pl.when(i == 0)
        def _():
            kc[0:ATT] = jnp.zeros((ATT, HP), F32)
            vc[0:ATT] = jnp.zeros((ATT, HP), F32)

        @pl.when(i > 0)
        def _():
            kc[0:ATT] = kc[ATT:2 * ATT]
            vc[0:ATT] = vc[ATT:2 * ATT]

        kc[ATT:2 * ATT] = k_ref[...]
        vc[ATT:2 * ATT] = v_ref[...]
        col = lax.broadcasted_iota(jnp.int32, (ABLK, 2 * ABLK), 1)
        dead = jnp.logical_and(i == 0, col < ABLK)
        blocks = _attn_blocks()
        hs = (slice(0, HE), slice(HE, 2 * HE))
        for g0 in range(0, len(blocks), AGRP):
            grp = blocks[g0:g0 + AGRP]
            qrows = [pl.ds(n * ABLK * d + r, ABLK, stride=d) for p, d, r, n in grp]
            krows = [pl.ds(ATT + (n - 1) * ABLK * d + r, 2 * ABLK, stride=d) for p, d, r, n in grp]
            qs = [(q_ref[qr, :] * QSCALE2).astype(BF16) for qr in qrows]
            ks = [kc[kr, :].astype(BF16) for kr in krows]
            vs = [vc[kr, :].astype(BF16) for kr in krows]
            ss = [[_nt(qs[b][:, sl], ks[b][:, sl]) + b_ref[grp[b][0], e] for e, sl in enumerate(hs)]
                  for b in range(len(grp))]
            ss = [[jnp.where(dead, NEG, s) if grp[b][3] == 0 else s for s in ss[b]] for b in range(len(grp))]
            mxs = [[jnp.max(s, axis=-1, keepdims=True) for s in sb] for sb in ss]
            pes = [[jnp.exp2(s - mx) for s, mx in zip(sb, mb)] for sb, mb in zip(ss, mxs)]
            dens = [[jnp.sum(pe, axis=-1, keepdims=True) for pe in pb] for pb in pes]
            pvs = [[_nn(pe.astype(BF16), vs[b][:, sl]) for pe, sl in zip(pes[b], hs)] for b in range(len(grp))]
            for b in range(len(grp)):
                p, d, r, n = grp[b]
                prow = pl.ds(p * ATT + n * ABLK * d + r, ABLK, stride=d)
                lp[prow, :] = jnp.concatenate(
                    [jnp.broadcast_to(mx + jnp.log2(dn), (ABLK, HE)) for mx, dn in zip(mxs[b], dens[b])], axis=1)
                op[prow, :] = jnp.concatenate([pv / dn for pv, dn in zip(pvs[b], dens[b])], axis=1)
        rt = 256
        for t in range(ATT // rt):
            rows = slice(t * rt, (t + 1) * rt)
            pr = [slice(p * ATT + t * rt, p * ATT + (t + 1) * rt) for p in range(3)]
            la, lb_, lc = lp[pr[0], :], lp[pr[1], :], lp[pr[2], :]
            m = jnp.maximum(jnp.maximum(la, lb_), lc)
            ea, eb, ec = jnp.exp2(la - m), jnp.exp2(lb_ - m), jnp.exp2(lc - m)
            den = ea + eb + ec
            att = (ea * op[pr[0], :] + eb * op[pr[1], :] + ec * op[pr[2], :]) / den
            a_ref[rows, :] = att
            l_ref[rows, :] = m + jnp.log2(den)
            z = z_ref[rows, :]
            oa_ref[rows, :] = (att * (z * _sig(z))).astype(BF16)

    def pcol(c):
        return pl.BlockSpec((ATT, HP), lambda h, i: (i, c * 4 + h))

    out = pl.BlockSpec((ATT, HP), lambda h, i: (i, h))
    return pl.pallas_call(
        body, name="attn_fwd", grid=(4, nt),
        in_specs=[pcol(0), pcol(1), pcol(2), pcol(3),
                  pl.BlockSpec((3, 2, ABLK, 2 * ABLK), lambda h, i: (0, h, 0, 0))],
        out_specs=[out, out, out],
        out_shape=[jax.ShapeDtypeStruct((S, AW), F32), jax.ShapeDtypeStruct((S, AW), F32),
                   jax.ShapeDtypeStruct((S, AW), BF16)],
        scratch_shapes=[pltpu.VMEM((2 * ATT, HP), F32), pltpu.VMEM((2 * ATT, HP), F32),
                        pltpu.VMEM((3 * ATT, HP), F32), pltpu.VMEM((3 * ATT, HP), F32)],
        compiler_params=_cp(),
    )(proj, proj, proj, proj, biases)


def _attn_bwd_call(proj, dattn, lse, dsum, biases):
    S = proj.shape[0]
    nt = S // ATT

    def body(q_ref, k_ref, v_ref, do_ref, l_ref, ds_ref, b_ref, dq_ref, dk_ref, dv_ref, db_ref,
             kc, vc, dkc, dvc, dqa):
        i = pl.program_id(1)

        @pl.when(i == 0)
        def _():
            kc[ATT:2 * ATT] = jnp.zeros((ATT, HP), F32)
            vc[ATT:2 * ATT] = jnp.zeros((ATT, HP), F32)
            dkc[ATT:2 * ATT] = jnp.zeros((ATT, HP), F32)
            dvc[ATT:2 * ATT] = jnp.zeros((ATT, HP), F32)
            db_ref[...] = jnp.zeros_like(db_ref)

        @pl.when(i < nt)
        def _():
            kc[0:ATT] = kc[ATT:2 * ATT]
            vc[0:ATT] = vc[ATT:2 * ATT]
            dkc[0:ATT] = dkc[ATT:2 * ATT]
            dvc[0:ATT] = dvc[ATT:2 * ATT]
            kc[ATT:2 * ATT] = k_ref[...]
            vc[ATT:2 * ATT] = v_ref[...]
            dkc[ATT:2 * ATT] = jnp.zeros((ATT, HP), F32)
            dvc[ATT:2 * ATT] = jnp.zeros((ATT, HP), F32)
            col = lax.broadcasted_iota(jnp.int32, (ABLK, 2 * ABLK), 1)
            dead = jnp.logical_and(i == 0, col < ABLK)
            blocks = _attn_blocks()
            hs = (slice(0, HE), slice(HE, 2 * HE))
            for g0 in range(0, len(blocks), AGRP_B):
                grp = blocks[g0:g0 + AGRP_B]
                nb_ = range(len(grp))
                qrows = [pl.ds(n * ABLK * d + r, ABLK, stride=d) for p, d, r, n in grp]
                krows = [pl.ds(ATT + (n - 1) * ABLK * d + r, 2 * ABLK, stride=d) for p, d, r, n in grp]
                qs = [(q_ref[qr, :] * QSCALE2).astype(BF16) for qr in qrows]
                ks = [kc[kr, :].astype(BF16) for kr in krows]
                vs = [vc[kr, :].astype(BF16) for kr in krows]
                dos = [do_ref[qr, :].astype(BF16) for qr in qrows]
                lvs = [l_ref[qr, :] for qr in qrows]
                dsvs = [ds_ref[qr, :] for qr in qrows]
                ss = [[_nt(qs[b][:, sl], ks[b][:, sl]) + b_ref[grp[b][0], e] for e, sl in enumerate(hs)] for b in nb_]
                ss = [[jnp.where(dead, NEG, s) if grp[b][3] == 0 else s for s in ss[b]] for b in nb_]
                dps = [[_nt(dos[b][:, sl], vs[b][:, sl]) for sl in hs] for b in nb_]
                pes = [[jnp.exp2(ss[b][e] - lvs[b][:, e * HE:e * HE + 1]) for e in range(2)] for b in nb_]
                dscs = [[pes[b][e] * (dps[b][e] - dsvs[b][:, e * HE:e * HE + 1]) for e in range(2)] for b in nb_]
                for e in range(2):
                    tot = dscs[0][e]
                    for b in range(1, len(grp)):
                        tot = tot + dscs[b][e]
                    db_ref[grp[0][0], e] += tot
                dsbs = [[t.astype(BF16) for t in tb] for tb in dscs]
                dqs = [[_nn(dsbs[b][e], ks[b][:, sl]) * (HE ** -0.5) for e, sl in enumerate(hs)] for b in nb_]
                dks = [[_tn(dsbs[b][e], qs[b][:, sl]) for e, sl in enumerate(hs)] for b in nb_]
                dvs = [[_tn(pes[b][e].astype(BF16), dos[b][:, sl]) for e, sl in enumerate(hs)] for b in nb_]
                for b in nb_:
                    dq = jnp.concatenate(dqs[b], axis=1)
                    if grp[b][0] == 0:
                        dqa[qrows[b], :] = dq
                    else:
                        dqa[qrows[b], :] += dq
                    dkc[krows[b], :] += jnp.concatenate(dks[b], axis=1)
                    dvc[krows[b], :] += jnp.concatenate(dvs[b], axis=1)
            dq_ref[...] = dqa[...].astype(BF16)
            dk_ref[...] = (dkc[0:ATT] * LN2).astype(BF16)
            dv_ref[...] = dvc[0:ATT].astype(BF16)

        @pl.when(i == nt)
        def _():
            dk_ref[...] = (dkc[ATT:2 * ATT] * LN2).astype(BF16)
            dv_ref[...] = dvc[ATT:2 * ATT].astype(BF16)

    def pcol(c):
        return pl.BlockSpec((ATT, HP), lambda h, i: (jnp.minimum(i, nt - 1), c * 4 + h))

    qrow = pl.BlockSpec((ATT, HP), lambda h, i: (jnp.minimum(i, nt - 1), h))
    krow = pl.BlockSpec((ATT, HP), lambda h, i: (jnp.maximum(i - 1, 0), h))
    bspec = pl.BlockSpec((3, 2, ABLK, 2 * ABLK), lambda h, i: (0, h, 0, 0))
    return pl.pallas_call(
        body, name="attn_bwd", grid=(4, nt + 1),
        in_specs=[pcol(0), pcol(1), pcol(2), qrow, qrow, qrow, bspec],
        out_specs=[qrow, krow, krow, bspec],
        out_shape=[jax.ShapeDtypeStruct((S, AW), BF16)] * 3
                  + [jax.ShapeDtypeStruct((3, NH, ABLK, 2 * ABLK), F32)],
        scratch_shapes=[pltpu.VMEM((2 * ATT, HP), F32)] * 4 + [pltpu.VMEM((ATT, HP), F32)],
        compiler_params=_cp(),
    )(proj, proj, proj, dattn, lse, dsum, biases)


HRB = 512


def _tri_masks():
    row = lax.broadcasted_iota(jnp.int32, (CH, CH), 0)
    col = lax.broadcasted_iota(jnp.int32, (CH, CH), 1)
    return row >= col


def _heads():
    return [slice(hh * HK, (hh + 1) * HK) for hh in range(NH)]


def _hgrn_gate_heads(q_ref, f_ref, rows, lbv):
    out = []
    for sl in _heads():
        qraw = q_ref[rows, sl]
        sq = _sig(qraw)
        sf = _sig(f_ref[rows, sl])
        f = lbv[:, sl] + (1.0 - lbv[:, sl]) * sf
        out.append(dict(qraw=qraw, sq=sq, q=qraw * sq, sf=sf, f=f, k=1.0 - f, lg=jnp.log(f)))
    return out


def _hgrn_decay_heads(gh, b):
    bl = b[CH - 1:CH, :]
    bm = b[CH // 2 - 1:CH // 2, :]
    ebm = jnp.exp(bm)
    eblm = jnp.exp(bl - bm)
    ebl = jnp.exp(bl)
    out = []
    for g, sl in zip(gh, _heads()):
        d = b[:, sl] - bm[:, sl]
        e1 = jnp.exp(d)
        e2 = jnp.exp(-d)
        qs = g["q"] * e1
        ks = g["k"] * e2
        qe = qs * ebm[:, sl]
        kd = ks * eblm[:, sl]
        out.append(dict(e1=e1, e2=e2, qe=qe, kd=kd, ebm=ebm[:, sl], eblm=eblm[:, sl], ebl=ebl[:, sl],
                        qsb=qs.astype(BF16), ksb=ks.astype(BF16), qeb=qe.astype(BF16), kdb=kd.astype(BF16)))
    return out


def _hgrn_fwd_call(proj, lb, gn):
    S = proj.shape[0]
    nc = S // CH
    cps = HRB // CH

    def body(q_ref, f_ref, i_ref, z_ref, lb_ref, gn_ref, or_ref, ob_ref, st_ref, st):
        @pl.when(pl.program_id(0) == 0)
        def _():
            st[...] = jnp.zeros_like(st)

        low = _tri_masks()
        tri = low.astype(BF16)
        lbv = lb_ref[...]
        hs = _heads()
        for ci in range(cps):
            rows = slice(ci * CH, (ci + 1) * CH)
            gh = _hgrn_gate_heads(q_ref, f_ref, rows, lbv)
            b = _exact_mm(tri, jnp.concatenate([g["lg"] for g in gh], axis=1))
            dh = _hgrn_decay_heads(gh, b)
            vbs = [i_ref[rows, sl].astype(BF16) for sl in hs]
            st_ref[ci] = st[...]
            s0s = [st[sl, :] for sl in hs]
            as_ = [_nt(d["qsb"], d["ksb"]) for d in dh]
            ois = [_nt(d["qeb"], s0.astype(BF16)) for d, s0 in zip(dh, s0s)]
            sts = [_tn(vb, d["kdb"]) for vb, d in zip(vbs, dh)]
            abs_ = [jnp.where(low, a, 0.0).astype(BF16) for a in as_]
            os_ = [oi + _nn(a, vb) for oi, a, vb in zip(ois, abs_, vbs)]
            for sl, s0, sn, d, o in zip(hs, s0s, sts, dh, os_):
                st[sl, :] = s0 * d["ebl"] + sn
                or_ref[rows, sl] = o
                r = lax.rsqrt(jnp.mean(o * o, axis=-1, keepdims=True) + EPS)
                z = z_ref[rows, sl]
                ob_ref[rows, sl] = (o * r * gn_ref[:, sl] * (z * _sig(z))).astype(BF16)

    def pcol(c):
        return pl.BlockSpec((HRB, D), lambda i: (i, c))

    vec = pl.BlockSpec((1, D), lambda i: (0, 0))
    row = pl.BlockSpec((HRB, D), lambda i: (i, 0))
    return pl.pallas_call(
        body, name="hgrn_fwd", grid=(S // HRB,),
        in_specs=[pcol(2), pcol(3), pcol(4), pcol(5), vec, vec],
        out_specs=[row, row, pl.BlockSpec((cps, NH * HK, HK), lambda i: (i, 0, 0))],
        out_shape=[jax.ShapeDtypeStruct((S, D), F32), jax.ShapeDtypeStruct((S, D), BF16),
                   jax.ShapeDtypeStruct((nc, NH * HK, HK), F32)],
        scratch_shapes=[pltpu.VMEM((NH * HK, HK), F32)],
        compiler_params=_cp(),
    )(proj, proj, proj, proj, lb, gn)


def _hgrn_bwd_call(proj, oraw, dob, states, lb, gn):
    S = proj.shape[0]
    nblk = S // HRB
    cps = HRB // CH

    def body(q_ref, f_ref, i_ref, z_ref, or_ref, dob_ref, st_ref, lb_ref, gn_ref, dh_ref, acc_ref, dst):
        @pl.when(pl.program_id(0) == 0)
        def _():
            dst[...] = jnp.zeros_like(dst)
            acc_ref[...] = jnp.zeros_like(acc_ref)

        low = _tri_masks()
        tri = low.astype(BF16)
        triu = jnp.logical_not(_tri_masks()) | (lax.broadcasted_iota(jnp.int32, (CH, CH), 0)
                                               == lax.broadcasted_iota(jnp.int32, (CH, CH), 1))
        triu = triu.astype(BF16)
        lbv = lb_ref[...]
        hs = _heads()
        for ci in reversed(range(cps)):
            rows = slice(ci * CH, (ci + 1) * CH)
            dobs, dgns = [], []
            for sl in hs:
                o = or_ref[rows, sl]
                z = z_ref[rows, sl]
                sz = _sig(z)
                gnv = gn_ref[:, sl]
                dobv = dob_ref[rows, sl]
                r = lax.rsqrt(jnp.mean(o * o, axis=-1, keepdims=True) + EPS)
                onr = o * r
                don = dobv * (z * sz)
                dh_ref[rows, 3 * D + sl.start:3 * D + sl.stop] = (
                    dobv * (onr * gnv) * (sz * (1.0 + z * (1.0 - sz)))).astype(BF16)
                dgns.append(jnp.sum(don * onr, axis=0, keepdims=True))
                gh_ = don * gnv
                dobs.append((r * (gh_ - onr * jnp.mean(gh_ * onr, axis=-1, keepdims=True))).astype(BF16))
            acc_ref[1:2, :] += jnp.concatenate(dgns, axis=1)

            gh = _hgrn_gate_heads(q_ref, f_ref, rows, lbv)
            b = _exact_mm(tri, jnp.concatenate([g["lg"] for g in gh], axis=1))
            dh = _hgrn_decay_heads(gh, b)
            vbs = [i_ref[rows, sl].astype(BF16) for sl in hs]

            st0s = [st_ref[ci, sl, :] for sl in hs]
            dst1s = [dst[sl, :] for sl in hs]
            dst1bs = [t.astype(BF16) for t in dst1s]
            as_ = [_nt(d["qsb"], d["ksb"]) for d in dh]
            das_ = [_nt(do, vb) for do, vb in zip(dobs, vbs)]
            dqes = [_nn(do, s0.astype(BF16)) for do, s0 in zip(dobs, st0s)]
            dkds = [_nn(vb, d1) for vb, d1 in zip(vbs, dst1bs)]
            dvis = [_nt(d["kdb"], d1) for d, d1 in zip(dh, dst1bs)]
            dsts = [_tn(do, d["qeb"]) for do, d in zip(dobs, dh)]
            abs_ = [jnp.where(low, a, 0.0).astype(BF16) for a in as_]
            dabs_ = [jnp.where(low, a, 0.0).astype(BF16) for a in das_]
            dqss = [_nn(da, d["ksb"]) for da, d in zip(dabs_, dh)]
            dkss = [_tn(da, d["qsb"]) for da, d in zip(dabs_, dh)]
            dvs_ = [_tn(a, do) + dvi for a, do, dvi in zip(abs_, dobs, dvis)]

            dqs_, dks_, dbs_, exs_ = [], [], [], []
            for hh, sl in enumerate(hs):
                d, d1, s0 = dh[hh], dst1s[hh], st0s[hh]
                dqe, dqs, dks, dkd = dqes[hh], dqss[hh], dkss[hh], dkds[hh]
                dst[sl, :] = dsts[hh] + d1 * d["ebl"]
                dh_ref[rows, 2 * D + sl.start:2 * D + sl.stop] = dvs_[hh].astype(BF16)
                dqs_.append((dqe * d["ebm"] + dqs) * d["e1"])
                dks_.append((dks + dkd * d["eblm"]) * d["e2"])
                dkdkd = dkd * d["kd"]
                dbs_.append(dqe * d["qe"] + dqs * d["qsb"].astype(F32) - dks * d["ksb"].astype(F32) - dkdkd)
                exs_.append(jnp.sum(dkdkd, axis=0, keepdims=True)
                            + jnp.sum(d1 * s0, axis=0, keepdims=True) * d["ebl"])
            dg = _exact_mm(triu, jnp.concatenate(dbs_, axis=1)) + jnp.concatenate(exs_, axis=1)

            dlbs = []
            for hh, sl in enumerate(hs):
                g = gh[hh]
                df = dg[:, sl] / g["f"] - dks_[hh]
                sf = g["sf"]
                omsf = 1.0 - sf
                sq = g["sq"]
                dlbs.append(jnp.sum(df * omsf, axis=0, keepdims=True))
                dh_ref[rows, sl] = (dqs_[hh] * (sq * (1.0 + g["qraw"] * (1.0 - sq)))).astype(BF16)
                dh_ref[rows, D + sl.start:D + sl.stop] = (df * (1.0 - lbv[:, sl]) * sf * omsf).astype(BF16)
            acc_ref[0:1, :] += jnp.concatenate(dlbs, axis=1)

    def pcol(c):
        return pl.BlockSpec((HRB, D), lambda i: (nblk - 1 - i, c))

    vec = pl.BlockSpec((1, D), lambda i: (0, 0))
    row = pl.BlockSpec((HRB, D), lambda i: (nblk - 1 - i, 0))
    return pl.pallas_call(
        body, name="hgrn_bwd", grid=(nblk,),
        in_specs=[pcol(2), pcol(3), pcol(4), pcol(5), row, row,
                  pl.BlockSpec((cps, NH * HK, HK), lambda i: (nblk - 1 - i, 0, 0)), vec, vec],
        out_specs=[pl.BlockSpec((HRB, 4 * D), lambda i: (nblk - 1 - i, 0)),
                   pl.BlockSpec((8, D), lambda i: (0, 0))],
        out_shape=[jax.ShapeDtypeStruct((S, 4 * D), BF16), jax.ShapeDtypeStruct((8, D), F32)],
        scratch_shapes=[pltpu.VMEM((NH * HK, HK), F32)],
        compiler_params=_cp(),
    )(proj, proj, proj, proj, oraw, dob, states, lb, gn)


def _merge_call(oa, ob, proj, x, tgt, vecs, attn, wa, wb, wo, hsum):
    S = x.shape[0]
    tm = 256

    def body(oa_ref, ob_ref, ga_ref, gb_ref, x_ref, t_ref, v_ref, at_ref, za_ref, wa_ref, wb_ref, wo_ref, hs_ref,
             y_ref, dx2_ref, du_ref, dya_ref, dyb_ref, dg_ref, dat_ref, dsum_ref, dza_ref, dob_ref, acc_ref, ls_ref):
        @pl.when(pl.program_id(0) == 0)
        def _():
            acc_ref[...] = jnp.zeros_like(acc_ref)
            ls_ref[...] = jnp.zeros_like(ls_ref)

        gate = v_ref[0:1, :]
        fg = v_ref[1:2, :]
        ya = _nn(oa_ref[...], wa_ref[...])
        yb = _nn(ob_ref[...], wb_ref[...])
        sa = _sig(ga_ref[...])
        sb = _sig(gb_ref[...])
        y = (sa * ya + sb * yb).astype(BF16)
        y_ref[...] = y
        u = _nn(y, wo_ref[...])
        x2v = x_ref[...] + gate * u
        r = lax.rsqrt(jnp.mean(x2v * x2v, axis=-1, keepdims=True) + EPS)
        err = x2v * r * fg - t_ref[...]
        ls_ref[...] += jnp.sum(err * err)
        dout = err * (1.0 / D)
        gh = dout * fg
        dx2 = r * gh - x2v * (r * r * r * jnp.mean(gh * x2v, axis=-1, keepdims=True))
        acc_ref[0:1, :] += jnp.sum(dx2 * u, axis=0, keepdims=True)
        acc_ref[1:2, :] += jnp.sum(dout * x2v * r, axis=0, keepdims=True)
        dx2_ref[...] = dx2
        du = (dx2 * gate).astype(BF16)
        du_ref[...] = du
        dy = _nt(du, wo_ref[...])
        dya = (dy * sa).astype(BF16)
        dyb = (dy * sb).astype(BF16)
        dya_ref[...] = dya
        dyb_ref[...] = dyb
        dg_ref[:, 0:D] = (dy * ya * sa * (1.0 - sa)).astype(BF16)
        dg_ref[:, D:2 * D] = (dy * yb * sb * (1.0 - sb)).astype(BF16)
        doa = _nt(dya, wa_ref[...])
        dob_ref[...] = _nt(dyb, wb_ref[...])
        za = za_ref[...]
        sz = _sig(za)
        att = at_ref[...]
        dat = doa * (za * sz)
        dat_ref[...] = dat
        dza_ref[...] = (doa * att * (sz * (1.0 + za * (1.0 - sz)))).astype(BF16)
        dsum_ref[...] = _exact_mm_r(dat * att, hs_ref[...])

    row = pl.BlockSpec((tm, D), lambda i: (i, 0))
    arow = pl.BlockSpec((tm, AW), lambda i: (i, 0))
    full = lambda a: pl.BlockSpec(a.shape, lambda i: (0, 0))
    return pl.pallas_call(
        body, name="merge_fwd_bwd", grid=(S // tm,),
        in_specs=[arow, row, pl.BlockSpec((tm, D), lambda i: (i, 6)), pl.BlockSpec((tm, D), lambda i: (i, 7)),
                  row, row, pl.BlockSpec((8, D), lambda i: (0, 0)), arow, pl.BlockSpec((tm, AW), lambda i: (i, 3)),
                  full(wa), full(wb), full(wo), full(hsum)],
        out_specs=[row, row, row, row, row, pl.BlockSpec((tm, 2 * D), lambda i: (i, 0)),
                   arow, arow, arow, row, pl.BlockSpec((8, D), lambda i: (0, 0)),
                   pl.BlockSpec((8, 128), lambda i: (0, 0))],
        out_shape=[jax.ShapeDtypeStruct((S, D), BF16), jax.ShapeDtypeStruct((S, D), F32),
                   jax.ShapeDtypeStruct((S, D), BF16), jax.ShapeDtypeStruct((S, D), BF16),
                   jax.ShapeDtypeStruct((S, D), BF16), jax.ShapeDtypeStruct((S, 2 * D), BF16),
                   jax.ShapeDtypeStruct((S, AW), F32), jax.ShapeDtypeStruct((S, AW), F32),
                   jax.ShapeDtypeStruct((S, AW), BF16), jax.ShapeDtypeStruct((S, D), F32),
                   jax.ShapeDtypeStruct((8, D), F32), jax.ShapeDtypeStruct((8, 128), F32)],
        compiler_params=_cp(),
    )(oa, ob, proj, proj, x, tgt, vecs, attn, proj, wa, wb, wo, hsum)


def _atb_call(a, b, name):
    S, K = a.shape
    N = b.shape[1]
    tm = min(1024, S)

    def body(a_ref, b_ref, o_ref, ob_ref):
        @pl.when(pl.program_id(0) == 0)
        def _():
            o_ref[...] = jnp.zeros_like(o_ref)

        o_ref[...] += _tn(a_ref[...], b_ref[...])

        @pl.when(pl.program_id(0) == S // tm - 1)
        def _():
            ob_ref[...] = o_ref[...].astype(BF16)

    ospec = pl.BlockSpec((K, N), lambda i: (0, 0))
    return pl.pallas_call(
        body, name=name, grid=(S // tm,),
        in_specs=[pl.BlockSpec((tm, K), lambda i: (i, 0)), pl.BlockSpec((tm, N), lambda i: (i, 0))],
        out_specs=[ospec, ospec],
        out_shape=[jax.ShapeDtypeStruct((K, N), F32), jax.ShapeDtypeStruct((K, N), BF16)], compiler_params=_cp(),
    )(a, b)


def _dwin_call(h_t, dqkvz, d_hgrn, d_gates):
    S = h_t.shape[1]
    tm = 512
    tn = 2048

    def body(h_ref, q_ref, k_ref, v_ref, z_ref, m_ref, g_ref, o_ref, ob_ref):
        j = pl.program_id(0)

        @pl.when(pl.program_id(1) == 0)
        def _():
            o_ref[...] = jnp.zeros_like(o_ref)

        hv = h_ref[...]

        @pl.when(j == 0)
        def _():
            for cidx, r in enumerate((q_ref, k_ref, v_ref, z_ref)):
                o_ref[0, :, cidx * AW:(cidx + 1) * AW] += _nn(hv, r[...])

        @pl.when(jnp.logical_or(j == 1, j == 2))
        def _():
            o_ref[0] += _nn(hv, m_ref[...])

        @pl.when(j == 3)
        def _():
            o_ref[0] += _nn(hv, g_ref[...])

        @pl.when(pl.program_id(1) == S // tm - 1)
        def _():
            ob_ref[...] = o_ref[...].astype(BF16)

    aspec = pl.BlockSpec((tm, AW), lambda j, i: (jnp.where(j == 0, i, 0), 0))
    ospec = pl.BlockSpec((1, D, tn), lambda j, i: (j, 0, 0))
    return pl.pallas_call(
        body, name="dw_in", grid=(4, S // tm),
        in_specs=[pl.BlockSpec((D, tm), lambda j, i: (0, i)), aspec, aspec, aspec, aspec,
                  pl.BlockSpec((tm, tn), lambda j, i: (jnp.where(jnp.logical_or(j == 1, j == 2), i, 0),
                                                       jnp.where(j == 2, 1, 0))),
                  pl.BlockSpec((tm, tn), lambda j, i: (jnp.where(j == 3, i, 0), 0))],
        out_specs=[ospec, ospec],
        out_shape=[jax.ShapeDtypeStruct((4, D, tn), F32), jax.ShapeDtypeStruct((4, D, tn), BF16)],
        compiler_params=_cp(),
    )(h_t, *dqkvz, d_hgrn, d_gates)


def _dh_call(dqkvz, d_hgrn, d_gates, w_in, x, dx2, vecs):
    S = x.shape[0]
    tm = 512

    def body(q_ref, k_ref, v_ref, z_ref, m_ref, g_ref, w_ref, x_ref, dx2_ref, p_ref, gx_ref, acc_ref):
        @pl.when(pl.program_id(0) == 0)
        def _():
            acc_ref[...] = jnp.zeros_like(acc_ref)

        dhv = _nt(q_ref[...], w_ref[:, 0:AW])
        for cidx, r in enumerate((k_ref, v_ref, z_ref)):
            dhv += _nt(r[...], w_ref[:, (cidx + 1) * AW:(cidx + 2) * AW])
        dhv += _nt(m_ref[...], w_ref[:, 4 * AW:4 * AW + 4 * D])
        dhv += _nt(g_ref[...], w_ref[:, 4 * AW + 4 * D:NPROJ])
        xv = x_ref[...]
        r = lax.rsqrt(jnp.mean(xv * xv, axis=-1, keepdims=True) + EPS)
        xn = xv * r
        acc_ref[0:1, :] += jnp.sum(dhv, axis=0, keepdims=True)
        acc_ref[1:2, :] += jnp.sum(dhv * xn * p_ref[1:2, :], axis=0, keepdims=True)
        acc_ref[2:3, :] += jnp.sum(dhv * xn * p_ref[2:3, :], axis=0, keepdims=True)
        dxn = dhv * p_ref[0:1, :]
        gx_ref[...] = dx2_ref[...] + r * dxn - xv * (r * r * r * jnp.mean(dxn * xv, axis=-1, keepdims=True))

    row = pl.BlockSpec((tm, D), lambda i: (i, 0))
    aspec = pl.BlockSpec((tm, AW), lambda i: (i, 0))
    const = lambda shape: pl.BlockSpec(shape, lambda i: (0, 0))
    return pl.pallas_call(
        body, name="dh_gradx", grid=(S // tm,),
        in_specs=[aspec, aspec, aspec, aspec,
                  pl.BlockSpec((tm, 4 * D), lambda i: (i, 0)), pl.BlockSpec((tm, 2 * D), lambda i: (i, 0)),
                  pl.BlockSpec((D, NPROJ), lambda i: (0, 0), pipeline_mode=pl.Buffered(1)),
                  row, row, const((8, D))],
        out_specs=[row, const((8, D))],
        out_shape=[jax.ShapeDtypeStruct((S, D), F32), jax.ShapeDtypeStruct((8, D), F32)],
        compiler_params=_cp(),
    )(*dqkvz, d_hgrn, d_gates, w_in, x, dx2, vecs)


def _adamw_math(w, g, m, v):
    m = B1 * m + (1.0 - B1) * g
    v = B2 * v + (1.0 - B2) * (g * g)
    m_hat = m / (1.0 - B1 ** STEP)
    v_hat = v / (1.0 - B2 ** STEP)
    delta = -LR * (m_hat / (jnp.sqrt(v_hat) + AEPS) + WD * w)
    return delta, m, v


def _adamw_call(w, g, m, v, name):
    R, C = w.shape
    tr = R if R * C * 4 <= (1 << 20) else max(8, (1 << 20) // (C * 4))
    assert R % tr == 0

    def body(w_ref, g_ref, m_ref, v_ref, go_ref, d_ref, nm_ref, nv_ref):
        g = g_ref[...]
        go_ref[...] = g
        d_ref[...], nm_ref[...], nv_ref[...] = _adamw_math(w_ref[...], g, m_ref[...], v_ref[...])

    blk = pl.BlockSpec((tr, C), lambda i: (i, 0))
    return pl.pallas_call(
        body, name=name, grid=(R // tr,), in_specs=[blk] * 4, out_specs=[blk] * 4,
        out_shape=[jax.ShapeDtypeStruct((R, C), F32)] * 4, compiler_params=_cp(),
    )(w, g, m, v)


def _mod_call(c_all, w_ada_s, b_s):
    def body(c_ref, w_ref, b_ref, o_ref):
        cv = c_ref[...]
        sc = cv * _sig(cv)
        o_ref[...] = jnp.dot(sc, w_ref[...], preferred_element_type=F32,
                             precision=lax.Precision.HIGHEST) + b_ref[...]

    return pl.pallas_call(
        body, name="ada_mod", out_shape=jax.ShapeDtypeStruct((8, w_ada_s.shape[1]), F32),
        compiler_params=_cp(),
    )(c_all, w_ada_s, b_s)


def _ada_update_call(sct, dm, w, m, v):
    R, C = w.shape
    tr = 256

    def body(s_ref, d_ref, w_ref, m_ref, v_ref, g_ref, dl_ref, nm_ref, nv_ref):
        g = s_ref[:, 0:1] * d_ref[0:1, :]
        for b in range(1, 8):
            g = g + s_ref[:, b:b + 1] * d_ref[b:b + 1, :]
        g_ref[...] = g
        dl_ref[...], nm_ref[...], nv_ref[...] = _adamw_math(w_ref[...], g, m_ref[...], v_ref[...])

    blk = pl.BlockSpec((tr, C), lambda i: (i, 0))
    return pl.pallas_call(
        body, name="ada_update", grid=(R // tr,),
        in_specs=[pl.BlockSpec((tr, 8), lambda i: (i, 0)), pl.BlockSpec((8, C), lambda i: (0, 0)), blk, blk, blk],
        out_specs=[blk] * 4, out_shape=[jax.ShapeDtypeStruct((R, C), F32)] * 4, compiler_params=_cp(),
    )(sct, dm, w, m, v)


def _sum8_call(packs):
    def body(p_ref, o_ref):
        acc = p_ref[0]
        for k in range(1, 8):
            acc = acc + p_ref[k]
        o_ref[...] = acc

    return pl.pallas_call(
        body, name="sum_small", out_shape=jax.ShapeDtypeStruct(packs.shape[1:], F32), compiler_params=_cp(),
    )(packs)


def _local_step(x, tgt, shift, scale, gate, norm_g, hgrn_onorm_g, rel_bias, lb, final_g, weights_fn, hook=None):
    a = norm_g * (1.0 + scale)
    z6 = jnp.zeros((6, D), F32)
    h, h_t, proj, w_in, rest_fn = weights_fn(x, jnp.concatenate([a, shift, z6], 0))

    biases = _bias_tiles(rel_bias) * LOG2E
    attn, lse, oa = _attn_fwd_call(proj, biases)

    gn = jnp.tile(hgrn_onorm_g, (1, NH))
    oraw, ob, states = _hgrn_fwd_call(proj, lb, gn)

    wa, wb, wo = rest_fn(ob)
    vecs2 = jnp.concatenate([gate, final_g, z6], 0)
    hsum = jnp.asarray(np.kron(np.eye(NH), np.ones((HE, HE))), BF16)
    y, dx2, du, dya, dyb, d_gates, dattn, dsum, dza, dob, acc2, lsq = _merge_call(
        oa, ob, proj, x, tgt, vecs2, attn, wa, wb, wo, hsum)
    loss = 0.5 * lsq[0, 0] / D
    d_wo, d_wo16 = _atb_call(y, du, "dw_out")
    d_wa, d_wa16 = _atb_call(oa, dya, "dw_branch_a")
    d_wb, d_wb16 = _atb_call(ob, dyb, "dw_branch_b")

    d_hgrn, acch = _hgrn_bwd_call(proj, oraw, dob, states, lb, gn)

    dq, dk, dv, dbs = _attn_bwd_call(proj, dattn, lse, dsum, biases)
    dqkvz = (dq, dk, dv, dza)

    d_win, d_win16 = _dwin_call(h_t, dqkvz, d_hgrn, d_gates)
    tok = hook((d_win, d_wa, d_wb, d_wo), (d_win16, d_wa16, d_wb16, d_wo16)) if hook is not None else 0.0
    one_scale = 1.0 + scale
    grad_x, acc1 = _dh_call(dqkvz, d_hgrn, d_gates, w_in, x, dx2,
                            jnp.concatenate([a + tok, norm_g, one_scale, jnp.zeros((5, D), F32)], 0))

    d_rel = jnp.zeros((NBUCKETS, NH), F32)
    for p, d in enumerate(PATTERNS):
        band, bucket = _band_bucket(d)
        onehot = (bucket[None] == np.arange(NBUCKETS)[:, None, None]) & band[None]
        d_rel = d_rel + jnp.einsum("hqk,bqk->bh", dbs[p], jnp.asarray(onehot, F32),
                                   precision=lax.Precision.HIGHEST)
    d_onorm = jnp.sum(acch[1].reshape(NH, HK), axis=0)

    zrow = jnp.zeros((D,), F32)
    pack = jnp.stack([acc1[0], acc1[1], acc2[0], acc1[2], acc2[1], acch[0],
                      zrow.at[:HK].set(d_onorm), zrow.at[0].set(loss),
                      zrow.at[:NBUCKETS * NH].set(d_rel.reshape(-1))] + [zrow] * 7, 0)
    return grad_x, d_win, d_wa, d_wb, d_wo, pack


def _me():
    return lax.axis_index("x"), lax.axis_index("y"), lax.axis_index("c")


def _peers(x, y):
    return [(1 - x, y), (x, 1 - y), (1 - x, 1 - y)]


def _allgather_small(blk, name):
    m_per, n = blk.shape

    def body(x_ref, out_ref, send_sems, recv_sems, local_sem):
        x, y, c = _me()
        me, sibling = (x, y, c), (x, y, 1 - c)
        chips = _peers(x, y)

        def rows(px, py, pc):
            return out_ref.at[pl.ds((4 * px + 2 * py + pc) * m_per, m_per), :]

        def copy(k, block, to, src=None):
            return pltpu.make_async_remote_copy(
                src_ref=rows(*block) if src is None else src, dst_ref=rows(*block),
                send_sem=send_sems.at[k], recv_sem=recv_sems.at[k], device_id=to, device_id_type=MESH)

        mine = pltpu.make_async_copy(x_ref, rows(*me), local_sem)
        mine.start()
        first = [copy(0, me, sibling, src=x_ref)]
        first += [copy(1 + j, me, (*chip, c), src=x_ref) for j, chip in enumerate(chips)]
        for cp in first:
            cp.start()
        passed = [copy(4 + j, (*chip, c), sibling) for j, chip in enumerate(chips)]
        for j, chip in enumerate(chips):
            copy(1 + j, (*chip, c), me).wait_recv()
            passed[j].start()
        copy(0, sibling, me).wait_recv()
        for j, chip in enumerate(chips):
            copy(4 + j, (*chip, 1 - c), me).wait_recv()
        for cp in first + passed:
            cp.wait_send()
        mine.wait()

    return pl.pallas_call(
        body, name=name, out_shape=jax.ShapeDtypeStruct((8 * m_per, n), blk.dtype),
        in_specs=[pl.BlockSpec(memory_space=pltpu.VMEM)], out_specs=pl.BlockSpec(memory_space=pltpu.VMEM),
        scratch_shapes=[pltpu.SemaphoreType.DMA((7,)), pltpu.SemaphoreType.DMA((7,)), pltpu.SemaphoreType.DMA],
    )(blk)


ANY = pl.BlockSpec(memory_space=pl.ANY)


HBM = pl.BlockSpec(memory_space=pltpu.HBM)
SEM = pl.BlockSpec(memory_space=pltpu.SEMAPHORE)
EFFECT = pltpu.SideEffectType.DATAFLOW_SIDE_EFFECTING


def _w_part(t, ref, j, half):
    if t == 0:
        return ref.at[pl.ds(half * (D // 2), D // 2), pl.ds(j * 2048, 2048)]
    if t == 1:
        return ref.at[pl.ds(half * (AW // 2), AW // 2), pl.ds(j * 256, 256)]
    return ref.at[pl.ds(j * 256 + half * 128, 128), :]


def _w_ici_copies(fulls, send_sems, recv_sems, tensors=(0, 1, 2, 3)):
    x, y, c = _me()
    outs, ins = [], []
    for u, t in enumerate(tensors):
        for k, chip in enumerate(_peers(x, y)):
            mine = _w_part(t, fulls[u], 2 * x + y, c)
            theirs = _w_part(t, fulls[u], 2 * chip[0] + chip[1], c)
            kw = dict(send_sem=send_sems.at[3 * t + k], recv_sem=recv_sems.at[3 * t + k],
                      device_id=(*chip, c), device_id_type=MESH)
            outs.append(pltpu.make_async_remote_copy(src_ref=mine, dst_ref=mine, **kw))
            ins.append(pltpu.make_async_remote_copy(src_ref=theirs, dst_ref=theirs, **kw))
    return outs, ins


def _gather_ici_start(fulls):
    def body(f0, f1, f2, f3, send_sems, recv_sems, t0, t1, t2, t3, token):
        for cp in _w_ici_copies([f0, f1, f2, f3], send_sems, recv_sems)[0]:
            cp.start()
        token[...] = jnp.zeros_like(token)

    res = pl.pallas_call(
        body, name="gather_ici_start",
        out_shape=(pltpu.SemaphoreType.DMA((12,)), pltpu.SemaphoreType.DMA((12,)),
                   *[pltpu.HBM(a.shape, a.dtype) for a in fulls], jax.ShapeDtypeStruct((8, 128), F32)),
        in_specs=[HBM] * 4, out_specs=(SEM, SEM, *[HBM] * 4, pl.BlockSpec(memory_space=pltpu.VMEM)),
        input_output_aliases={i: 2 + i for i in range(4)},
        compiler_params=pltpu.CompilerParams(has_side_effects=EFFECT),
    )(*[pltpu.with_memory_space_constraint(a, pltpu.HBM) for a in fulls])
    return res[0], res[1], list(res[2:6]), res[6]


def _gather_ici_wait(send_sems, recv_sems, bufs, after, tensors, name):
    n = len(tensors)

    def body(*refs):
        outs, ins = _w_ici_copies(refs[0:n], refs[n], refs[n + 1], tensors)
        for cp in outs:
            cp.wait_send()
        for cp in ins:
            cp.wait_recv()

    return pl.pallas_call(
        body, name=name, out_shape=tuple(pltpu.HBM(a.shape, a.dtype) for a in bufs),
        in_specs=[HBM] * n + [SEM, SEM, ANY], out_specs=[HBM] * n,
        input_output_aliases={i: i for i in range(n)},
        compiler_params=pltpu.CompilerParams(has_side_effects=EFFECT),
    )(*bufs, send_sems, recv_sems, after)


def _gather_sibling(bufs, tensors, name):
    n = len(tensors)

    def body(*refs):
        outs, send_sems, recv_sems = refs[n:2 * n], refs[2 * n], refs[2 * n + 1]
        x, y, c = _me()
        cps = []
        for u, t in enumerate(tensors):
            for k, chip in enumerate(_peers(x, y)):
                blk = _w_part(t, outs[u], 2 * chip[0] + chip[1], c)
                cp = pltpu.make_async_remote_copy(
                    src_ref=blk, dst_ref=blk, send_sem=send_sems.at[u, k], recv_sem=recv_sems.at[u, k],
                    device_id=(x, y, 1 - c), device_id_type=MESH)
                cp.start()
                cps.append(cp)
        for u, t in enumerate(tensors):
            for k, chip in enumerate(_peers(x, y)):
                blk = _w_part(t, outs[u], 2 * chip[0] + chip[1], 1 - c)
                pltpu.make_async_remote_copy(
                    src_ref=blk, dst_ref=blk, send_sem=send_sems.at[u, k], recv_sem=recv_sems.at[u, k],
                    device_id=(x, y, 1 - c), device_id_type=MESH).wait_recv()
        for cp in cps:
            cp.wait_send()

    return pl.pallas_call(
        body, name=name,
        out_shape=[jax.ShapeDtypeStruct(FULL_W_SHAPES[t], BF16) for t in tensors],
        in_specs=[ANY] * n, out_specs=[ANY] * n, input_output_aliases={u: u for u in range(n)},
        scratch_shapes=[pltpu.SemaphoreType.DMA((n, 3)), pltpu.SemaphoreType.DMA((n, 3))],
    )(*bufs)


def _half_of(t, ref, half):
    if t == 0:
        return ref.at[:, pl.ds(half * 512, 512), :]
    if t == 1:
        return ref.at[pl.ds(half * 256, 256), :]
    return ref.at[:, pl.ds(half * 512, 512)]


HALF_SHAPES = [(4, 512, 2048), (256, D), (D, 512), (D, 512)]
PIECE_SHAPES = [(512, 2048), (256, 256), (256, 512), (256, 512)]
SHARD_SHAPES = [(D, 2048), (AW, 256), (256, D), (256, D)]


def _chip_piece(t, ref, j):
    if t == 0:
        return ref.at[j]
    if t == 1:
        return ref.at[:, pl.ds(j * 256, 256)]
    return ref.at[pl.ds(j * 256, 256), :]


def _reduce_sibling_send(gs):
    def body(g0, g1, g2, g3, r0, r1, r2, r3, send_sems, recv_sems):
        x, y, c = _me()
        ins, outs = [g0, g1, g2, g3], [r0, r1, r2, r3]
        cps = []
        for t in range(4):
            cp = pltpu.make_async_remote_copy(
                src_ref=_half_of(t, ins[t], 1 - c), dst_ref=outs[t],
                send_sem=send_sems.at[t], recv_sem=recv_sems.at[t], device_id=(x, y, 1 - c), device_id_type=MESH)
            cp.start()
            cps.append(cp)
        for cp in cps:
            cp.wait_recv()
        for cp in cps:
            cp.wait_send()

    return pl.pallas_call(
        body, name="reduce_sibling", out_shape=[jax.ShapeDtypeStruct(s, BF16) for s in HALF_SHAPES],
        in_specs=[ANY] * 4, out_specs=[ANY] * 4,
        scratch_shapes=[pltpu.SemaphoreType.DMA((4,)), pltpu.SemaphoreType.DMA((4,))],
    )(*gs)


def _chip_copies(hs, lands, send_sems, recv_sems):
    x, y, c = _me()
    cps = []
    for t in range(4):
        for k, chip in enumerate(_peers(x, y)):
            pj = 2 * chip[0] + chip[1]
            cps.append(pltpu.make_async_remote_copy(
                src_ref=_chip_piece(t, hs[t], pj), dst_ref=lands[t].at[k],
                send_sem=send_sems.at[3 * t + k], recv_sem=recv_sems.at[3 * t + k],
                device_id=(*chip, c), device_id_type=MESH))
    return cps


def _reduce_chips_start(hs):
    lands = [lax.empty((3,) + s, BF16) for s in PIECE_SHAPES]

    def body(h0, h1, h2, h3, l0, l1, l2, l3, send_sems, recv_sems, t0, t1, t2, t3, t4, t5, t6, t7, token):
        for cp in _chip_copies([h0, h1, h2, h3], [l0, l1, l2, l3], send_sems, recv_sems):
            cp.start()
        token[...] = jnp.zeros_like(token)

    bufs = list(hs) + lands
    res = pl.pallas_call(
        body, name="reduce_chips_start",
        out_shape=(pltpu.SemaphoreType.DMA((12,)), pltpu.SemaphoreType.DMA((12,)),
                   *[pltpu.HBM(a.shape, a.dtype) for a in bufs], jax.ShapeDtypeStruct((8, 128), F32)),
        in_specs=[HBM] * 8, out_specs=(SEM, SEM, *[HBM] * 8, pl.BlockSpec(memory_space=pltpu.VMEM)),
        input_output_aliases={i: 2 + i for i in range(8)},
        compiler_params=pltpu.CompilerParams(has_side_effects=EFFECT),
    )(*[pltpu.with_memory_space_constraint(a, pltpu.HBM) for a in bufs])
    return res[0], res[1], list(res[2:10]), res[10]


def _reduce_chips_wait(send_sems, recv_sems, thru, after):
    def body(h0, h1, h2, h3, l0, l1, l2, l3, send_sems, recv_sems, after_ref, d0, d1, d2, d3, g0, g1, g2, g3):
        cps = _chip_copies([h0, h1, h2, h3], [l0, l1, l2, l3], send_sems, recv_sems)
        for cp in cps:
            cp.wait_send()
        for cp in cps:
            cp.wait_recv()

    res = pl.pallas_call(
        body, name="reduce_chips_wait", out_shape=tuple(pltpu.HBM(a.shape, a.dtype) for a in thru),
        in_specs=[HBM] * 8 + [SEM, SEM, ANY], out_specs=[HBM] * 8,
        input_output_aliases={i: i for i in range(8)},
        compiler_params=pltpu.CompilerParams(has_side_effects=EFFECT),
    )(*thru, send_sems, recv_sems, after)
    return list(res[4:8])


def _share_sibling(shards):
    def body(i0, i1, i2, i3, o0, o1, o2, o3, send_sems, recv_sems):
        x, y, c = _me()
        outs = [o0, o1, o2, o3]

        def half(t, ref, hf):
            if t == 0:
                return ref.at[pl.ds(hf * 512, 512), :]
            if t == 1:
                return ref.at[pl.ds(hf * 256, 256), :]
            return ref.at[:, pl.ds(hf * 512, 512)]

        cps = []
        for t in range(4):
            mine = half(t, outs[t], c)
            cp = pltpu.make_async_remote_copy(
                src_ref=mine, dst_ref=mine, send_sem=send_sems.at[t], recv_sem=recv_sems.at[t],
                device_id=(x, y, 1 - c), device_id_type=MESH)
            cp.start()
            cps.append(cp)
        for t in range(4):
            theirs = half(t, outs[t], 1 - c)
            pltpu.make_async_remote_copy(
                src_ref=theirs, dst_ref=theirs, send_sem=send_sems.at[t],
                recv_sem=recv_sems.at[t], device_id=(x, y, 1 - c), device_id_type=MESH).wait_recv()
        for cp in cps:
            cp.wait_send()

    return pl.pallas_call(
        body, name="share_sibling", out_shape=[jax.ShapeDtypeStruct(s, F32) for s in SHARD_SHAPES],
        in_specs=[ANY] * 4, out_specs=[ANY] * 4, input_output_aliases={0: 0, 1: 1, 2: 2, 3: 3},
        scratch_shapes=[pltpu.SemaphoreType.DMA((4,)), pltpu.SemaphoreType.DMA((4,))],
    )(*shards)


def _half_blockspec(t, idx_pos):
    if t == 0:
        return pl.BlockSpec((1, 512, 2048), lambda i, s: (i, s[idx_pos], 0)), 4
    if t == 1:
        return pl.BlockSpec((256, D), lambda i, s: (s[idx_pos], 0)), 1
    return pl.BlockSpec((256, 512), lambda i, s: (i, s[idx_pos])), 4


def _half_out_blockspec(t):
    if t == 0:
        return pl.BlockSpec((1, 512, 2048), lambda i, s: (i, 0, 0))
    if t == 1:
        return pl.BlockSpec((256, D), lambda i, s: (0, 0))
    return pl.BlockSpec((256, 512), lambda i, s: (i, 0))


def _add_half_call(t, own, recv, sc, name):
    in_blk, steps = _half_blockspec(t, 0)
    out_blk = _half_out_blockspec(t)

    def body(s_ref, a_ref, b_ref, o_ref, ob_ref):
        v = a_ref[...] + b_ref[...].astype(F32)
        o_ref[...] = v
        ob_ref[...] = v.astype(BF16)

    return pl.pallas_call(
        body, name=name,
        grid_spec=pltpu.PrefetchScalarGridSpec(
            num_scalar_prefetch=1, grid=(steps,), in_specs=[in_blk, out_blk], out_specs=[out_blk, out_blk]),
        out_shape=[jax.ShapeDtypeStruct(HALF_SHAPES[t], F32), jax.ShapeDtypeStruct(HALF_SHAPES[t], BF16)],
        compiler_params=_cp(),
    )(sc, own, recv)


def _final_piece_call(t, chipsum, recv3, sc, name):
    ps = PIECE_SHAPES[t]
    if t == 0:
        own_blk = pl.BlockSpec((1,) + ps, lambda i, s: (s[1], 0, 0))
        o_blk = pl.BlockSpec(ps, lambda i, s: (s[0], 0))
    elif t == 1:
        own_blk = pl.BlockSpec(ps, lambda i, s: (0, s[1]))
        o_blk = pl.BlockSpec(ps, lambda i, s: (s[0], 0))
    else:
        own_blk = pl.BlockSpec(ps, lambda i, s: (s[1], 0))
        o_blk = pl.BlockSpec(ps, lambda i, s: (0, s[0]))
    r_blk = pl.BlockSpec((3,) + ps, lambda i, s: (0, 0, 0))

    def body(s_ref, a_ref, r_ref, o_ref):
        a = a_ref[0] if t == 0 else a_ref[...]
        o_ref[...] = ((a + r_ref[0].astype(F32)) + r_ref[1].astype(F32)) + r_ref[2].astype(F32)

    return pl.pallas_call(
        body, name=name,
        grid_spec=pltpu.PrefetchScalarGridSpec(
            num_scalar_prefetch=1, grid=(1,), in_specs=[own_blk, r_blk], out_specs=o_blk),
        out_shape=jax.ShapeDtypeStruct(SHARD_SHAPES[t], F32), compiler_params=_cp(),
    )(sc, chipsum, recv3)


FULL_W_SHAPES = [(D, NPROJ), (AW, D), (D, D), (D, D)]


def _cast_place_call(t, shard, sc, name):
    if t == 0:
        blk, steps = (512, 2048), 2
        in_blk = pl.BlockSpec(blk, lambda i, s: (i, 0))
        o_blk = pl.BlockSpec(blk, lambda i, s: (i, s[1]))
    elif t == 1:
        blk, steps = (AW, 256), 1
        in_blk = pl.BlockSpec(blk, lambda i, s: (0, 0))
        o_blk = pl.BlockSpec(blk, lambda i, s: (0, s[1]))
    else:
        blk, steps = (256, D), 1
        in_blk = pl.BlockSpec(blk, lambda i, s: (0, 0))
        o_blk = pl.BlockSpec(blk, lambda i, s: (s[1], 0))

    def body(s_ref, a_ref, o_ref, own_ref):
        v = a_ref[...].astype(BF16)
        o_ref[...] = v
        own_ref[...] = v

    return pl.pallas_call(
        body, name=name,
        grid_spec=pltpu.PrefetchScalarGridSpec(
            num_scalar_prefetch=1, grid=(steps,), in_specs=[in_blk], out_specs=[o_blk, in_blk]),
        out_shape=[jax.ShapeDtypeStruct(FULL_W_SHAPES[t], BF16), jax.ShapeDtypeStruct(shard.shape, BF16)],
        compiler_params=_cp(),
    )(sc, shard)


def _lower_bound_fn(hgrn_lb):
    return jnp.cumsum(jax.nn.softmax(hgrn_lb.astype(F32), axis=0), axis=0)[0]


def kernel(x, c, w_ada, b_ada, norm_g, w_in, hgrn_onorm_g, w_branch_a, w_branch_b, w_out, rel_bias, hgrn_lb, final_g, loss_target, m_w_ada, m_b_ada, m_norm_g, m_w_in, m_hgrn_onorm_g, m_w_branch_a, m_w_branch_b, m_w_out, m_rel_bias, m_hgrn_lb, m_final_g, v_w_ada, v_b_ada, v_norm_g, v_w_in, v_hgrn_onorm_g, v_w_branch_a, v_w_branch_b, v_w_out, v_rel_bias, v_hgrn_lb, v_final_g):
    ax, ay, ac = _me()
    chip = 2 * ax + ay
    dev = 4 * ax + 2 * ay + ac
    sc_idx = jnp.stack([ac, chip]).astype(jnp.int32)

    c_all = _allgather_small(jnp.pad(c, ((0, 7), (0, 0))), "gather_c").reshape(8, 8, D)[:, 0]
    b_s = lax.dynamic_slice(b_ada, (0, chip * 768), (1, 768))
    mod_part = _mod_call(c_all, w_ada[0], b_s)
    mod_all = _allgather_small(mod_part, "gather_mod").reshape(8, 8, 768)
    mod_mine = lax.dynamic_index_in_dim(mod_all, dev, axis=1, keepdims=False)
    mod = mod_mine[0::2].reshape(1, 3 * D)

    names = ["w_in", "w_a", "w_b", "w_o"]
    shards, mod = lax.optimization_barrier(([w_in[0], w_branch_a[0], w_branch_b[0], w_out[0]], mod))
    placed = [_cast_place_call(t, shards[t], sc_idx, "cast_" + names[t]) for t in range(4)]
    w_send_sems, w_recv_sems, w_thru, w_token = _gather_ici_start([p_[0] for p_ in placed])
    mod = mod + w_token[0, 0]
    rel_bias_t = rel_bias + w_token[0, 0]
    shift, scale, gate = mod[:, :D], mod[:, D:2 * D], mod[:, 2 * D:]

    def weights_fn(xs, avec):
        proj, h, h_t = _proj_own_call(xs, avec, placed[0][1], sc_idx)
        arrived = _gather_ici_wait(w_send_sems, w_recv_sems, w_thru[0:1], proj, (0,), "gather_ici_wait_in")
        (win_f,) = _gather_sibling(arrived, (0,), "gather_sibling_in")

        def rest_fn(after):
            late = _gather_ici_wait(w_send_sems, w_recv_sems, w_thru[1:4], after, (1, 2, 3), "gather_ici_wait_rest")
            return _gather_sibling(late, (1, 2, 3), "gather_sibling_rest")

        return h, h_t, _proj_rest_call(h, win_f, proj, sc_idx), win_f, rest_fn

    flight = {}

    def start_reduction(own, own16):
        sib = _reduce_sibling_send(own16)
        halves = [_add_half_call(t, own[t], sib[t], sc_idx, "chipsum_" + names[t]) for t in range(4)]
        send_sems, recv_sems, thru, token = _reduce_chips_start([hb for _, hb in halves])
        flight.update(sems=(send_sems, recv_sems), thru=thru, sums=[hf for hf, _ in halves])
        return token[0, 0]

    lb, lb_vjp = jax.vjp(_lower_bound_fn, hgrn_lb)
    grad_x, d_win, d_wa, d_wb, d_wo, pack = _local_step(
        x[0], loss_target[0], shift, scale, gate, norm_g, hgrn_onorm_g, rel_bias_t, lb[None, :],
        final_g[None, :], weights_fn, hook=start_reduction)

    rec = _reduce_chips_wait(*flight["sems"], flight["thru"], pack)
    pieces = [_final_piece_call(t, flight["sums"][t], rec[t], sc_idx, "piece_" + names[t]) for t in range(4)]
    g_win, g_wa, g_wb, g_wo = _share_sibling(pieces)

    packs = _allgather_small(pack, "gather_small").reshape(8, 16, D)
    tot = _sum8_call(packs)
    loss = tot[7, 0]
    g_b_ada = tot[0:3].reshape(1, 3 * D)
    g_norm_g = tot[3:4]
    g_final_g = tot[4]
    (g_hgrn_lb,) = lb_vjp(tot[5])
    g_onorm = tot[6:7, :HK]
    g_rel = tot[8, :NBUCKETS * NH].reshape(NBUCKETS, NH)

    def rows_of(a):
        flat = a.reshape(-1)
        n = -(-flat.shape[0] // D)
        return jnp.pad(flat, (0, n * D - flat.shape[0])).reshape(n, D)

    smalls = [(b_ada, g_b_ada, m_b_ada, v_b_ada), (norm_g, g_norm_g, m_norm_g, v_norm_g),
              (hgrn_onorm_g, g_onorm, m_hgrn_onorm_g, v_hgrn_onorm_g), (rel_bias, g_rel, m_rel_bias, v_rel_bias),
              (hgrn_lb, g_hgrn_lb, m_hgrn_lb, v_hgrn_lb), (final_g, g_final_g, m_final_g, v_final_g)]
    cat = [jnp.concatenate([rows_of(s[k]) for s in smalls], 0) for k in range(4)]
    cat = [jnp.pad(a, ((0, 16 - a.shape[0]), (0, 0))) for a in cat]
    _, sd, sm, sv = _adamw_call(*cat, "adamw_small")

    def unpack(packed):
        res, r = [], 0
        for s in smalls:
            n = -(-s[0].size // D)
            res.append(packed[r:r + n].reshape(-1)[:s[0].size].reshape(s[0].shape))
            r += n
        return res

    d_small, m_small, v_small = unpack(sd), unpack(sm), unpack(sv)

    sc_all = c_all * jax.nn.sigmoid(c_all)
    dmod_all = packs[:, 0:3].reshape(8, 3 * D)
    dm_s = lax.dynamic_slice(dmod_all, (0, chip * 768), (8, 768))
    g_w_ada, d_w_ada, nm_w_ada, nv_w_ada = _ada_update_call(sc_all.T, dm_s, w_ada[0], m_w_ada[0], v_w_ada[0])

    big = []
    for w, g, m, v, n in [(w_in, g_win, m_w_in, v_w_in, "w_in"), (w_branch_a, g_wa, m_w_branch_a, v_w_branch_a, "w_a"),
                          (w_branch_b, g_wb, m_w_branch_b, v_w_branch_b, "w_b"), (w_out, g_wo, m_w_out, v_w_out, "w_o")]:
        big.append(_adamw_call(w[0], g, m[0], v[0], "adamw_" + n))

    e = lambda a: a[None]
    grads = [e(g_w_ada), g_b_ada, g_norm_g, e(big[0][0]), g_onorm, e(big[1][0]), e(big[2][0]), e(big[3][0]),
             g_rel, g_hgrn_lb, g_final_g]
    deltas = [e(d_w_ada), d_small[0], d_small[1], e(big[0][1]), d_small[2], e(big[1][1]), e(big[2][1]), e(big[3][1]),
              d_small[3], d_small[4], d_small[5]]
    new_m = [e(nm_w_ada), m_small[0], m_small[1], e(big[0][2]), m_small[2], e(big[1][2]), e(big[2][2]), e(big[3][2]),
             m_small[3], m_small[4], m_small[5]]
    new_v = [e(nv_w_ada), v_small[0], v_small[1], e(big[0][3]), v_small[2], e(big[1][3]), e(big[2][3]), e(big[3][3]),
             v_small[3], v_small[4], v_small[5]]
    return (loss, grad_x[None], *grads, *deltas, *new_m, *new_v)
```

```python
import math

import numpy as np
import jax
import jax.numpy as jnp
from jax import lax
from jax.experimental import pallas as pl
from jax.experimental.pallas import tpu as pltpu

D = 1024
AW = 512
NH = 8
HE = 64
HK = 128
NPROJ = 8192
ABLK = 128
PATTERNS = (1, 4, 16)
NBUCKETS = 32
MAXDIST = 2048
NEG = -1e30
EPS = 1e-6
CH = 64
LR, B1, B2, AEPS, WD, STEP = 0.001, 0.9, 0.999, 1e-08, 0.01, 10

F32 = jnp.float32
BF16 = jnp.bfloat16
MESH = pl.DeviceIdType.MESH
VMEM_LIMIT = 56 * 1024 * 1024


def _cp(**kw):
    return pltpu.CompilerParams(vmem_limit_bytes=VMEM_LIMIT, **kw)


def _sig(x):
    return 0.5 * jnp.tanh(0.5 * x) + 0.5


def _nt(a, b):
    return lax.dot_general(a, b, (((1,), (1,)), ((), ())), preferred_element_type=F32)


def _tn(a, b):
    return lax.dot_general(a, b, (((0,), (0,)), ((), ())), preferred_element_type=F32)


def _nn(a, b):
    return jnp.dot(a, b, preferred_element_type=F32)


def _split2(x):
    h = x.astype(BF16)
    return h, (x - h.astype(F32)).astype(BF16)


def _exact_mm(tri_bf16, x):
    h, l = _split2(x)
    return _nn(tri_bf16, h) + _nn(tri_bf16, l)


def _exact_mm_r(x, ones_bf16):
    h, l = _split2(x)
    return _nn(h, ones_bf16) + _nn(l, ones_bf16)


def _proj_own_call(x, avec, w_own, sc):
    S = x.shape[0]
    tm, tn = 512, 2048

    def body(s_ref, x_ref, a_ref, w_ref, o_ref, h_ref, ht_ref):
        xv = x_ref[...]
        r = lax.rsqrt(jnp.mean(xv * xv, axis=-1, keepdims=True) + EPS)
        hv = xv * r * a_ref[0:1, :] + a_ref[1:2, :]
        hb = hv.astype(BF16)
        h_ref[...] = hb
        ht_ref[...] = hv.T.astype(BF16)
        o_ref[...] = _nn(hb, w_ref[...])

    return pl.pallas_call(
        body, name="in_proj_own",
        grid_spec=pltpu.PrefetchScalarGridSpec(
            num_scalar_prefetch=1, grid=(S // tm,),
            in_specs=[pl.BlockSpec((tm, D), lambda i, s: (i, 0)), pl.BlockSpec((8, D), lambda i, s: (0, 0)),
                      pl.BlockSpec((D, tn), lambda i, s: (0, 0))],
            out_specs=[pl.BlockSpec((tm, tn), lambda i, s: (i, s[1])), pl.BlockSpec((tm, D), lambda i, s: (i, 0)),
                       pl.BlockSpec((D, tm), lambda i, s: (0, i))]),
        out_shape=[jax.ShapeDtypeStruct((S, NPROJ), F32), jax.ShapeDtypeStruct((S, D), BF16),
                   jax.ShapeDtypeStruct((D, S), BF16)],
        compiler_params=_cp(),
    )(sc, x, avec, w_own)


def _proj_rest_call(h, w_in, proj, sc):
    S = h.shape[0]
    tm, tn = 512, 2048

    def body(s_ref, h_ref, w_ref, p_ref, o_ref):
        o_ref[...] = _nn(h_ref[...], w_ref[...])

    col = lambda j, s: (s[1] + 1 + j) % 4
    return pl.pallas_call(
        body, name="in_proj_rest",
        grid_spec=pltpu.PrefetchScalarGridSpec(
            num_scalar_prefetch=1, grid=(3, S // tm),
            in_specs=[pl.BlockSpec((tm, D), lambda j, i, s: (i, 0)),
                      pl.BlockSpec((D, tn), lambda j, i, s: (0, col(j, s))),
                      pl.BlockSpec(memory_space=pl.ANY)],
            out_specs=pl.BlockSpec((tm, tn), lambda j, i, s: (i, col(j, s)))),
        out_shape=jax.ShapeDtypeStruct((S, NPROJ), F32), input_output_aliases={3: 0}, compiler_params=_cp(),
    )(sc, h, w_in, proj)


def _t5_bucket_np(dist):
    max_exact = NBUCKETS // 2
    n = dist.astype(np.float32)
    large = max_exact + (np.log(np.maximum(n, np.float32(1.0)) / np.float32(max_exact))
                         / np.float32(math.log(MAXDIST / max_exact))
                         * np.float32(NBUCKETS - max_exact)).astype(np.int32)
    large = np.minimum(large, NBUCKETS - 1)
    return np.where(dist < max_exact, dist, large)


def _band_bucket(d):
    qi = np.arange(ABLK)[:, None]
    kj = np.arange(2 * ABLK)[None, :]
    delta = qi + ABLK - kj
    band = (delta >= 0) & (delta <= ABLK)
    bucket = _t5_bucket_np(np.clip(delta, 0, None) * d)
    return band, bucket


def _bias_tiles(rel_bias):
    tiles = []
    for d in PATTERNS:
        band, bucket = _band_bucket(d)
        onehot = (jnp.asarray(bucket, jnp.int32)[None] == jnp.arange(NBUCKETS, dtype=jnp.int32)[:, None, None])
        bias = jnp.einsum("bqk,bh->hqk", onehot.astype(F32), rel_bias, precision=lax.Precision.HIGHEST)
        tiles.append(jnp.where(jnp.asarray(band)[None], bias, NEG))
    return jnp.stack(tiles, 0)


ATT = 2048
HP = 2 * HE
LOG2E = 1.4426950408889634
LN2 = 0.6931471805599453
QSCALE2 = (HE ** -0.5) * LOG2E
AGRP = 16
AGRP_B = 8


def _attn_blocks():
    out = []
    for p, d in enumerate(PATTERNS):
        for r in range(d):
            for n in range(ATT // (d * ABLK)):
                out.append((p, d, r, n))
    return out


def _attn_fwd_call(proj, biases):
    S = proj.shape[0]
    nt = S // ATT

    def body(q_ref, k_ref, v_ref, z_ref, b_ref, a_ref, l_ref, oa_ref, kc, vc, op, lp):
        i = pl.program_id(1)

        @pl.when(i == 0)
        def _():
            kc[0:ATT] = jnp.zeros((ATT, HP), F32)
            vc[0:ATT] = jnp.zeros((ATT, HP), F32)

        @pl.when(i > 0)
        def _():
            kc[0:ATT] = kc[ATT:2 * ATT]
            vc[0:ATT] = vc[ATT:2 * ATT]

        kc[ATT:2 * ATT] = k_ref[...]
        vc[ATT:2 * ATT] = v_ref[...]
        col = lax.broadcasted_iota(jnp.int32, (ABLK, 2 * ABLK), 1)
        dead = jnp.logical_and(i == 0, col < ABLK)
        blocks = _attn_blocks()
        hs = (slice(0, HE), slice(HE, 2 * HE))
        for g0 in range(0, len(blocks), AGRP):
            grp = blocks[g0:g0 + AGRP]
            qrows = [pl.ds(n * ABLK * d + r, ABLK, stride=d) for p, d, r, n in grp]
            krows = [pl.ds(ATT + (n - 1) * ABLK * d + r, 2 * ABLK, stride=d) for p, d, r, n in grp]
            qs = [(q_ref[qr, :] * QSCALE2).astype(BF16) for qr in qrows]
            ks = [kc[kr, :].astype(BF16) for kr in krows]
            vs = [vc[kr, :].astype(BF16) for kr in krows]
            ss = [[_nt(qs[b][:, sl], ks[b][:, sl]) + b_ref[grp[b][0], e] for e, sl in enumerate(hs)]
                  for b in range(len(grp))]
            ss = [[jnp.where(dead, NEG, s) if grp[b][3] == 0 else s for s in ss[b]] for b in range(len(grp))]
            mxs = [[jnp.max(s, axis=-1, keepdims=True) for s in sb] for sb in ss]
            pes = [[jnp.exp2(s - mx) for s, mx in zip(sb, mb)] for sb, mb in zip(ss, mxs)]
            dens = [[jnp.sum(pe, axis=-1, keepdims=True) for pe in pb] for pb in pes]
            pvs = [[_nn(pe.astype(BF16), vs[b][:, sl]) for pe, sl in zip(pes[b], hs)] for b in range(len(grp))]
            for b in range(len(grp)):
                p, d, r, n = grp[b]
                prow = pl.ds(p * ATT + n * ABLK * d + r, ABLK, stride=d)
                lp[prow, :] = jnp.concatenate(
                    [jnp.broadcast_to(mx + jnp.log2(dn), (ABLK, HE)) for mx, dn in zip(mxs[b], dens[b])], axis=1)
                op[prow, :] = jnp.concatenate([pv / dn for pv, dn in zip(pvs[b], dens[b])], axis=1)
        rt = 256
        for t in range(ATT // rt):
            rows = slice(t * rt, (t + 1) * rt)
            pr = [slice(p * ATT + t * rt, p * ATT + (t + 1) * rt) for p in range(3)]
            la, lb_, lc = lp[pr[0], :], lp[pr[1], :], lp[pr[2], :]
            m = jnp.maximum(jnp.maximum(la, lb_), lc)
            ea, eb, ec = jnp.exp2(la - m), jnp.exp2(lb_ - m), jnp.exp2(lc - m)
            den = ea + eb + ec
            att = (ea * op[pr[0], :] + eb * op[pr[1], :] + ec * op[pr[2], :]) / den
            a_ref[rows, :] = att
            l_ref[rows, :] = m + jnp.log2(den)
            z = z_ref[rows, :]
            oa_ref[rows, :] = (att * (z * _sig(z))).astype(BF16)

    def pcol(c):
        return pl.BlockSpec((ATT, HP), lambda h, i: (i, c * 4 + h))

    out = pl.BlockSpec((ATT, HP), lambda h, i: (i, h))
    return pl.pallas_call(
        body, name="attn_fwd", grid=(4, nt),
        in_specs=[pcol(0), pcol(1), pcol(2), pcol(3),
                  pl.BlockSpec((3, 2, ABLK, 2 * ABLK), lambda h, i: (0, h, 0, 0))],
        out_specs=[out, out, out],
        out_shape=[jax.ShapeDtypeStruct((S, AW), F32), jax.ShapeDtypeStruct((S, AW), F32),
                   jax.ShapeDtypeStruct((S, AW), BF16)],
        scratch_shapes=[pltpu.VMEM((2 * ATT, HP), F32), pltpu.VMEM((2 * ATT, HP), F32),
                        pltpu.VMEM((3 * ATT, HP), F32), pltpu.VMEM((3 * ATT, HP), F32)],
        compiler_params=_cp(),
    )(proj, proj, proj, proj, biases)


def _attn_bwd_call(proj, dattn, lse, dsum, biases):
    S = proj.shape[0]
    nt = S // ATT

    def body(q_ref, k_ref, v_ref, do_ref, l_ref, ds_ref, b_ref, dq_ref, dk_ref, dv_ref, db_ref,
             kc, vc, dkc, dvc, dqa):
        i = pl.program_id(1)

        @pl.when(i == 0)
        def _():
            kc[ATT:2 * ATT] = jnp.zeros((ATT, HP), F32)
            vc[ATT:2 * ATT] = jnp.zeros((ATT, HP), F32)
            dkc[ATT:2 * ATT] = jnp.zeros((ATT, HP), F32)
            dvc[ATT:2 * ATT] = jnp.zeros((ATT, HP), F32)
            db_ref[...] = jnp.zeros_like(db_ref)

        @pl.when(i < nt)
        def _():
            kc[0:ATT] = kc[ATT:2 * ATT]
            vc[0:ATT] = vc[ATT:2 * ATT]
            dkc[0:ATT] = dkc[ATT:2 * ATT]
            dvc[0:ATT] = dvc[ATT:2 * ATT]
            kc[ATT:2 * ATT] = k_ref[...]
            vc[ATT:2 * ATT] = v_ref[...]
            dkc[ATT:2 * ATT] = jnp.zeros((ATT, HP), F32)
            dvc[ATT:2 * ATT] = jnp.zeros((ATT, HP), F32)
            col = lax.broadcasted_iota(jnp.int32, (ABLK, 2 * ABLK), 1)
            dead = jnp.logical_and(i == 0, col < ABLK)
            blocks = _attn_blocks()
            hs = (slice(0, HE), slice(HE, 2 * HE))
            for g0 in range(0, len(blocks), AGRP_B):
                grp = blocks[g0:g0 + AGRP_B]
                nb_ = range(len(grp))
                qrows = [pl.ds(n * ABLK * d + r, ABLK, stride=d) for p, d, r, n in grp]
                krows = [pl.ds(ATT + (n - 1) * ABLK * d + r, 2 * ABLK, stride=d) for p, d, r, n in grp]
                qs = [(q_ref[qr, :] * QSCALE2).astype(BF16) for qr in qrows]
                ks = [kc[kr, :].astype(BF16) for kr in krows]
                vs = [vc[kr, :].astype(BF16) for kr in krows]
                dos = [do_ref[qr, :].astype(BF16) for qr in qrows]
                lvs = [l_ref[qr, :] for qr in qrows]
                dsvs = [ds_ref[qr, :] for qr in qrows]
                ss = [[_nt(qs[b][:, sl], ks[b][:, sl]) + b_ref[grp[b][0], e] for e, sl in enumerate(hs)] for b in nb_]
                ss = [[jnp.where(dead, NEG, s) if grp[b][3] == 0 else s for s in ss[b]] for b in nb_]
                dps = [[_nt(dos[b][:, sl], vs[b][:, sl]) for sl in hs] for b in nb_]
                pes = [[jnp.exp2(ss[b][e] - lvs[b][:, e * HE:e * HE + 1]) for e in range(2)] for b in nb_]
                dscs = [[pes[b][e] * (dps[b][e] - dsvs[b][:, e * HE:e * HE + 1]) for e in range(2)] for b in nb_]
                for e in range(2):
                    tot = dscs[0][e]
                    for b in range(1, len(grp)):
                        tot = tot + dscs[b][e]
                    db_ref[grp[0][0], e] += tot
                dsbs = [[t.astype(BF16) for t in tb] for tb in dscs]
                dqs = [[_nn(dsbs[b][e], ks[b][:, sl]) * (HE ** -0.5) for e, sl in enumerate(hs)] for b in nb_]
                dks = [[_tn(dsbs[b][e], qs[b][:, sl]) for e, sl in enumerate(hs)] for b in nb_]
                dvs = [[_tn(pes[b][e].astype(BF16), dos[b][:, sl]) for e, sl in enumerate(hs)] for b in nb_]
                for b in nb_:
                    dq = jnp.concatenate(dqs[b], axis=1)
                    if grp[b][0] == 0:
                        dqa[qrows[b], :] = dq
                    else:
                        dqa[qrows[b], :] += dq
                    dkc[krows[b], :] += jnp.concatenate(dks[b], axis=1)
                    dvc[krows[b], :] += jnp.concatenate(dvs[b], axis=1)
            dq_ref[...] = dqa[...].astype(BF16)
            dk_ref[...] = (dkc[0:ATT] * LN2).astype(BF16)
            dv_ref[...] = dvc[0:ATT].astype(BF16)

        @pl.when(i == nt)
        def _():
            dk_ref[...] = (dkc[ATT:2 * ATT] * LN2).astype(BF16)
            dv_ref[...] = dvc[ATT:2 * ATT].astype(BF16)

    def pcol(c):
        return pl.BlockSpec((ATT, HP), lambda h, i: (jnp.minimum(i, nt - 1), c * 4 + h))

    qrow = pl.BlockSpec((ATT, HP), lambda h, i: (jnp.minimum(i, nt - 1), h))
    krow = pl.BlockSpec((ATT, HP), lambda h, i: (jnp.maximum(i - 1, 0), h))
    bspec = pl.BlockSpec((3, 2, ABLK, 2 * ABLK), lambda h, i: (0, h, 0, 0))
    return pl.pallas_call(
        body, name="attn_bwd", grid=(4, nt + 1),
        in_specs=[pcol(0), pcol(1), pcol(2), qrow, qrow, qrow, bspec],
        out_specs=[qrow, krow, krow, bspec],
        out_shape=[jax.ShapeDtypeStruct((S, AW), BF16)] * 3
                  + [jax.ShapeDtypeStruct((3, NH, ABLK, 2 * ABLK), F32)],
        scratch_shapes=[pltpu.VMEM((2 * ATT, HP), F32)] * 4 + [pltpu.VMEM((ATT, HP), F32)],
        compiler_params=_cp(),
    )(proj, proj, proj, dattn, lse, dsum, biases)


HRB = 512


def _tri_masks():
    row = lax.broadcasted_iota(jnp.int32, (CH, CH), 0)
    col = lax.broadcasted_iota(jnp.int32, (CH, CH), 1)
    return row >= col


def _heads():
    return [slice(hh * HK, (hh + 1) * HK) for hh in range(NH)]


def _hgrn_gate_heads(q_ref, f_ref, rows, lbv):
    out = []
    for sl in _heads():
        qraw = q_ref[rows, sl]
        sq = _sig(qraw)
        sf = _sig(f_ref[rows, sl])
        f = lbv[:, sl] + (1.0 - lbv[:, sl]) * sf
        out.append(dict(qraw=qraw, sq=sq, q=qraw * sq, sf=sf, f=f, k=1.0 - f, lg=jnp.log(f)))
    return out


def _hgrn_decay_heads(gh, b):
    bl = b[CH - 1:CH, :]
    bm = b[CH // 2 - 1:CH // 2, :]
    ebm = jnp.exp(bm)
    eblm = jnp.exp(bl - bm)
    ebl = jnp.exp(bl)
    out = []
    for g, sl in zip(gh, _heads()):
        d = b[:, sl] - bm[:, sl]
        e1 = jnp.exp(d)
        e2 = jnp.exp(-d)
        qs = g["q"] * e1
        ks = g["k"] * e2
        qe = qs * ebm[:, sl]
        kd = ks * eblm[:, sl]
        out.append(dict(e1=e1, e2=e2, qe=qe, kd=kd, ebm=ebm[:, sl], eblm=eblm[:, sl], ebl=ebl[:, sl],
                        qsb=qs.astype(BF16), ksb=ks.astype(BF16), qeb=qe.astype(BF16), kdb=kd.astype(BF16)))
    return out


def _hgrn_fwd_call(proj, lb, gn):
    S = proj.shape[0]
    nc = S // CH
    cps = HRB // CH

    def body(q_ref, f_ref, i_ref, z_ref, lb_ref, gn_ref, or_ref, ob_ref, st_ref, st):
        @pl.when(pl.program_id(0) == 0)
        def _():
            st[...] = jnp.zeros_like(st)

        low = _tri_masks()
        tri = low.astype(BF16)
        lbv = lb_ref[...]
        hs = _heads()
        for ci in range(cps):
            rows = slice(ci * CH, (ci + 1) * CH)
            gh = _hgrn_gate_heads(q_ref, f_ref, rows, lbv)
            b = _exact_mm(tri, jnp.concatenate([g["lg"] for g in gh], axis=1))
            dh = _hgrn_decay_heads(gh, b)
            vbs = [i_ref[rows, sl].astype(BF16) for sl in hs]
            st_ref[ci] = st[...]
            s0s = [st[sl, :] for sl in hs]
            as_ = [_nt(d["qsb"], d["ksb"]) for d in dh]
            ois = [_nt(d["qeb"], s0.astype(BF16)) for d, s0 in zip(dh, s0s)]
            sts = [_tn(vb, d["kdb"]) for vb, d in zip(vbs, dh)]
            abs_ = [jnp.where(low, a, 0.0).astype(BF16) for a in as_]
            os_ = [oi + _nn(a, vb) for oi, a, vb in zip(ois, abs_, vbs)]
            for sl, s0, sn, d, o in zip(hs, s0s, sts, dh, os_):
                st[sl, :] = s0 * d["ebl"] + sn
                or_ref[rows, sl] = o
                r = lax.rsqrt(jnp.mean(o * o, axis=-1, keepdims=True) + EPS)
                z = z_ref[rows, sl]
                ob_ref[rows, sl] = (o * r * gn_ref[:, sl] * (z * _sig(z))).astype(BF16)

    def pcol(c):
        return pl.BlockSpec((HRB, D), lambda i: (i, c))

    vec = pl.BlockSpec((1, D), lambda i: (0, 0))
    row = pl.BlockSpec((HRB, D), lambda i: (i, 0))
    return pl.pallas_call(
        body, name="hgrn_fwd", grid=(S // HRB,),
        in_specs=[pcol(2), pcol(3), pcol(4), pcol(5), vec, vec],
        out_specs=[row, row, pl.BlockSpec((cps, NH * HK, HK), lambda i: (i, 0, 0))],
        out_shape=[jax.ShapeDtypeStruct((S, D), F32), jax.ShapeDtypeStruct((S, D), BF16),
                   jax.ShapeDtypeStruct((nc, NH * HK, HK), F32)],
        scratch_shapes=[pltpu.VMEM((NH * HK, HK), F32)],
        compiler_params=_cp(),
    )(proj, proj, proj, proj, lb, gn)


def _hgrn_bwd_call(proj, oraw, dob, states, lb, gn):
    S = proj.shape[0]
    nblk = S // HRB
    cps = HRB // CH

    def body(q_ref, f_ref, i_ref, z_ref, or_ref, dob_ref, st_ref, lb_ref, gn_ref, dh_ref, acc_ref, dst):
        @pl.when(pl.program_id(0) == 0)
        def _():
            dst[...] = jnp.zeros_like(dst)
            acc_ref[...] = jnp.zeros_like(acc_ref)

        low = _tri_masks()
        tri = low.astype(BF16)
        triu = jnp.logical_not(_tri_masks()) | (lax.broadcasted_iota(jnp.int32, (CH, CH), 0)
                                               == lax.broadcasted_iota(jnp.int32, (CH, CH), 1))
        triu = triu.astype(BF16)
        lbv = lb_ref[...]
        hs = _heads()
        for ci in reversed(range(cps)):
            rows = slice(ci * CH, (ci + 1) * CH)
            dobs, dgns = [], []
            for sl in hs:
                o = or_ref[rows, sl]
                z = z_ref[rows, sl]
                sz = _sig(z)
                gnv = gn_ref[:, sl]
                dobv = dob_ref[rows, sl]
                r = lax.rsqrt(jnp.mean(o * o, axis=-1, keepdims=True) + EPS)
                onr = o * r
                don = dobv * (z * sz)
                dh_ref[rows, 3 * D + sl.start:3 * D + sl.stop] = (
                    dobv * (onr * gnv) * (sz * (1.0 + z * (1.0 - sz)))).astype(BF16)
                dgns.append(jnp.sum(don * onr, axis=0, keepdims=True))
                gh_ = don * gnv
                dobs.append((r * (gh_ - onr * jnp.mean(gh_ * onr, axis=-1, keepdims=True))).astype(BF16))
            acc_ref[1:2, :] += jnp.concatenate(dgns, axis=1)

            gh = _hgrn_gate_heads(q_ref, f_ref, rows, lbv)
            b = _exact_mm(tri, jnp.concatenate([g["lg"] for g in gh], axis=1))
            dh = _hgrn_decay_heads(gh, b)
            vbs = [i_ref[rows, sl].astype(BF16) for sl in hs]

            st0s = [st_ref[ci, sl, :] for sl in hs]
            dst1s = [dst[sl, :] for sl in hs]
            dst1bs = [t.astype(BF16) for t in dst1s]
            as_ = [_nt(d["qsb"], d["ksb"]) for d in dh]
            das_ = [_nt(do, vb) for do, vb in zip(dobs, vbs)]
            dqes = [_nn(do, s0.astype(BF16)) for do, s0 in zip(dobs, st0s)]
            dkds = [_nn(vb, d1) for vb, d1 in zip(vbs, dst1bs)]
            dvis = [_nt(d["kdb"], d1) for d, d1 in zip(dh, dst1bs)]
            dsts = [_tn(do, d["qeb"]) for do, d in zip(dobs, dh)]
            abs_ = [jnp.where(low, a, 0.0).astype(BF16) for a in as_]
            dabs_ = [jnp.where(low, a, 0.0).astype(BF16) for a in das_]
            dqss = [_nn(da, d["ksb"]) for da, d in zip(dabs_, dh)]
            dkss = [_tn(da, d["qsb"]) for da, d in zip(dabs_, dh)]
            dvs_ = [_tn(a, do) + dvi for a, do, dvi in zip(abs_, dobs, dvis)]

            dqs_, dks_, dbs_, exs_ = [], [], [], []
            for hh, sl in enumerate(hs):
                d, d1, s0 = dh[hh], dst1s[hh], st0s[hh]
                dqe, dqs, dks, dkd = dqes[hh], dqss[hh], dkss[hh], dkds[hh]
                dst[sl, :] = dsts[hh] + d1 * d["ebl"]
                dh_ref[rows, 2 * D + sl.start:2 * D + sl.stop] = dvs_[hh].astype(BF16)
                dqs_.append((dqe * d["ebm"] + dqs) * d["e1"])
                dks_.append((dks + dkd * d["eblm"]) * d["e2"])
                dkdkd = dkd * d["kd"]
                dbs_.append(dqe * d["qe"] + dqs * d["qsb"].astype(F32) - dks * d["ksb"].astype(F32) - dkdkd)
                exs_.append(jnp.sum(dkdkd, axis=0, keepdims=True)
                            + jnp.sum(d1 * s0, axis=0, keepdims=True) * d["ebl"])
            dg = _exact_mm(triu, jnp.concatenate(dbs_, axis=1)) + jnp.concatenate(exs_, axis=1)

            dlbs = []
            for hh, sl in enumerate(hs):
                g = gh[hh]
                df = dg[:, sl] / g["f"] - dks_[hh]
                sf = g["sf"]
                omsf = 1.0 - sf
                sq = g["sq"]
                dlbs.append(jnp.sum(df * omsf, axis=0, keepdims=True))
                dh_ref[rows, sl] = (dqs_[hh] * (sq * (1.0 + g["qraw"] * (1.0 - sq)))).astype(BF16)
                dh_ref[rows, D + sl.start:D + sl.stop] = (df * (1.0 - lbv[:, sl]) * sf * omsf).astype(BF16)
            acc_ref[0:1, :] += jnp.concatenate(dlbs, axis=1)

    def pcol(c):
        return pl.BlockSpec((HRB, D), lambda i: (nblk - 1 - i, c))

    vec = pl.BlockSpec((1, D), lambda i: (0, 0))
    row = pl.BlockSpec((HRB, D), lambda i: (nblk - 1 - i, 0))
    return pl.pallas_call(
        body, name="hgrn_bwd", grid=(nblk,),
        in_specs=[pcol(2), pcol(3), pcol(4), pcol(5), row, row,
                  pl.BlockSpec((cps, NH * HK, HK), lambda i: (nblk - 1 - i, 0, 0)), vec, vec],
        out_specs=[pl.BlockSpec((HRB, 4 * D), lambda i: (nblk - 1 - i, 0)),
                   pl.BlockSpec((8, D), lambda i: (0, 0))],
        out_shape=[jax.ShapeDtypeStruct((S, 4 * D), BF16), jax.ShapeDtypeStruct((8, D), F32)],
        scratch_shapes=[pltpu.VMEM((NH * HK, HK), F32)],
        compiler_params=_cp(),
    )(proj, proj, proj, proj, oraw, dob, states, lb, gn)


def _merge_call(oa, ob, proj, x, tgt, vecs, attn, wa, wb, wo, hsum):
    S = x.shape[0]
    tm = 256

    def body(oa_ref, ob_ref, ga_ref, gb_ref, x_ref, t_ref, v_ref, at_ref, za_ref, wa_ref, wb_ref, wo_ref, hs_ref,
             y_ref, dx2_ref, du_ref, dya_ref, dyb_ref, dg_ref, dat_ref, dsum_ref, dza_ref, dob_ref, acc_ref, ls_ref):
        @pl.when(pl.program_id(0) == 0)
        def _():
            acc_ref[...] = jnp.zeros_like(acc_ref)
            ls_ref[...] = jnp.zeros_like(ls_ref)

        gate = v_ref[0:1, :]
        fg = v_ref[1:2, :]
        ya = _nn(oa_ref[...], wa_ref[...])
        yb = _nn(ob_ref[...], wb_ref[...])
        sa = _sig(ga_ref[...])
        sb = _sig(gb_ref[...])
        y = (sa * ya + sb * yb).astype(BF16)
        y_ref[...] = y
        u = _nn(y, wo_ref[...])
        x2v = x_ref[...] + gate * u
        r = lax.rsqrt(jnp.mean(x2v * x2v, axis=-1, keepdims=True) + EPS)
        err = x2v * r * fg - t_ref[...]
        ls_ref[...] += jnp.sum(err * err)
        dout = err * (1.0 / D)
        gh = dout * fg
        dx2 = r * gh - x2v * (r * r * r * jnp.mean(gh * x2v, axis=-1, keepdims=True))
        acc_ref[0:1, :] += jnp.sum(dx2 * u, axis=0, keepdims=True)
        acc_ref[1:2, :] += jnp.sum(dout * x2v * r, axis=0, keepdims=True)
        dx2_ref[...] = dx2
        du = (dx2 * gate).astype(BF16)
        du_ref[...] = du
        dy = _nt(du, wo_ref[...])
        dya = (dy * sa).astype(BF16)
        dyb = (dy * sb).astype(BF16)
        dya_ref[...] = dya
        dyb_ref[...] = dyb
        dg_ref[:, 0:D] = (dy * ya * sa * (1.0 - sa)).astype(BF16)
        dg_ref[:, D:2 * D] = (dy * yb * sb * (1.0 - sb)).astype(BF16)
        doa = _nt(dya, wa_ref[...])
        dob_ref[...] = _nt(dyb, wb_ref[...])
        za = za_ref[...]
        sz = _sig(za)
        att = at_ref[...]
        dat = doa * (za * sz)
        dat_ref[...] = dat
        dza_ref[...] = (doa * att * (sz * (1.0 + za * (1.0 - sz)))).astype(BF16)
        dsum_ref[...] = _exact_mm_r(dat * att, hs_ref[...])

    row = pl.BlockSpec((tm, D), lambda i: (i, 0))
    arow = pl.BlockSpec((tm, AW), lambda i: (i, 0))
    full = lambda a: pl.BlockSpec(a.shape, lambda i: (0, 0))
    return pl.pallas_call(
        body, name="merge_fwd_bwd", grid=(S // tm,),
        in_specs=[arow, row, pl.BlockSpec((tm, D), lambda i: (i, 6)), pl.BlockSpec((tm, D), lambda i: (i, 7)),
                  row, row, pl.BlockSpec((8, D), lambda i: (0, 0)), arow, pl.BlockSpec((tm, AW), lambda i: (i, 3)),
                  full(wa), full(wb), full(wo), full(hsum)],
        out_specs=[row, row, row, row, row, pl.BlockSpec((tm, 2 * D), lambda i: (i, 0)),
                   arow, arow, arow, row, pl.BlockSpec((8, D), lambda i: (0, 0)),
                   pl.BlockSpec((8, 128), lambda i: (0, 0))],
        out_shape=[jax.ShapeDtypeStruct((S, D), BF16), jax.ShapeDtypeStruct((S, D), F32),
                   jax.ShapeDtypeStruct((S, D), BF16), jax.ShapeDtypeStruct((S, D), BF16),
                   jax.ShapeDtypeStruct((S, D), BF16), jax.ShapeDtypeStruct((S, 2 * D), BF16),
                   jax.ShapeDtypeStruct((S, AW), F32), jax.ShapeDtypeStruct((S, AW), F32),
                   jax.ShapeDtypeStruct((S, AW), BF16), jax.ShapeDtypeStruct((S, D), F32),
                   jax.ShapeDtypeStruct((8, D), F32), jax.ShapeDtypeStruct((8, 128), F32)],
        compiler_params=_cp(),
    )(oa, ob, proj, proj, x, tgt, vecs, attn, proj, wa, wb, wo, hsum)


def _atb_call(a, b, name):
    S, K = a.shape
    N = b.shape[1]
    tm = min(1024, S)

    def body(a_ref, b_ref, o_ref, ob_ref):
        @pl.when(pl.program_id(0) == 0)
        def _():
            o_ref[...] = jnp.zeros_like(o_ref)

        o_ref[...] += _tn(a_ref[...], b_ref[...])

        @pl.when(pl.program_id(0) == S // tm - 1)
        def _():
            ob_ref[...] = o_ref[...].astype(BF16)

    ospec = pl.BlockSpec((K, N), lambda i: (0, 0))
    return pl.pallas_call(
        body, name=name, grid=(S // tm,),
        in_specs=[pl.BlockSpec((tm, K), lambda i: (i, 0)), pl.BlockSpec((tm, N), lambda i: (i, 0))],
        out_specs=[ospec, ospec],
        out_shape=[jax.ShapeDtypeStruct((K, N), F32), jax.ShapeDtypeStruct((K, N), BF16)], compiler_params=_cp(),
    )(a, b)


def _dwin_call(h_t, dqkvz, d_hgrn, d_gates):
    S = h_t.shape[1]
    tm = 512
    tn = 2048

    def body(h_ref, q_ref, k_ref, v_ref, z_ref, m_ref, g_ref, o_ref, ob_ref):
        j = pl.program_id(0)

        @pl.when(pl.program_id(1) == 0)
        def _():
            o_ref[...] = jnp.zeros_like(o_ref)

        hv = h_ref[...]

        @pl.when(j == 0)
        def _():
            for cidx, r in enumerate((q_ref, k_ref, v_ref, z_ref)):
                o_ref[0, :, cidx * AW:(cidx + 1) * AW] += _nn(hv, r[...])

        @pl.when(jnp.logical_or(j == 1, j == 2))
        def _():
            o_ref[0] += _nn(hv, m_ref[...])

        @pl.when(j == 3)
        def _():
            o_ref[0] += _nn(hv, g_ref[...])

        @pl.when(pl.program_id(1) == S // tm - 1)
        def _():
            ob_ref[...] = o_ref[...].astype(BF16)

    aspec = pl.BlockSpec((tm, AW), lambda j, i: (jnp.where(j == 0, i, 0), 0))
    ospec = pl.BlockSpec((1, D, tn), lambda j, i: (j, 0, 0))
    return pl.pallas_call(
        body, name="dw_in", grid=(4, S // tm),
        in_specs=[pl.BlockSpec((D, tm), lambda j, i: (0, i)), aspec, aspec, aspec, aspec,
                  pl.BlockSpec((tm, tn), lambda j, i: (jnp.where(jnp.logical_or(j == 1, j == 2), i, 0),
                                                       jnp.where(j == 2, 1, 0))),
                  pl.BlockSpec((tm, tn), lambda j, i: (jnp.where(j == 3, i, 0), 0))],
        out_specs=[ospec, ospec],
        out_shape=[jax.ShapeDtypeStruct((4, D, tn), F32), jax.ShapeDtypeStruct((4, D, tn), BF16)],
        compiler_params=_cp(),
    )(h_t, *dqkvz, d_hgrn, d_gates)


def _dh_call(dqkvz, d_hgrn, d_gates, w_in, x, dx2, vecs):
    S = x.shape[0]
    tm = 512

    def body(q_ref, k_ref, v_ref, z_ref, m_ref, g_ref, w_ref, x_ref, dx2_ref, p_ref, gx_ref, acc_ref):
        @pl.when(pl.program_id(0) == 0)
        def _():
            acc_ref[...] = jnp.zeros_like(acc_ref)

        dhv = _nt(q_ref[...], w_ref[:, 0:AW])
        for cidx, r in enumerate((k_ref, v_ref, z_ref)):
            dhv += _nt(r[...], w_ref[:, (cidx + 1) * AW:(cidx + 2) * AW])
        dhv += _nt(m_ref[...], w_ref[:, 4 * AW:4 * AW + 4 * D])
        dhv += _nt(g_ref[...], w_ref[:, 4 * AW + 4 * D:NPROJ])
        xv = x_ref[...]
        r = lax.rsqrt(jnp.mean(xv * xv, axis=-1, keepdims=True) + EPS)
        xn = xv * r
        acc_ref[0:1, :] += jnp.sum(dhv, axis=0, keepdims=True)
        acc_ref[1:2, :] += jnp.sum(dhv * xn * p_ref[1:2, :], axis=0, keepdims=True)
        acc_ref[2:3, :] += jnp.sum(dhv * xn * p_ref[2:3, :], axis=0, keepdims=True)
        dxn = dhv * p_ref[0:1, :]
        gx_ref[...] = dx2_ref[...] + r * dxn - xv * (r * r * r * jnp.mean(dxn * xv, axis=-1, keepdims=True))

    row = pl.BlockSpec((tm, D), lambda i: (i, 0))
    aspec = pl.BlockSpec((tm, AW), lambda i: (i, 0))
    const = lambda shape: pl.BlockSpec(shape, lambda i: (0, 0))
    return pl.pallas_call(
        body, name="dh_gradx", grid=(S // tm,),
        in_specs=[aspec, aspec, aspec, aspec,
                  pl.BlockSpec((tm, 4 * D), lambda i: (i, 0)), pl.BlockSpec((tm, 2 * D), lambda i: (i, 0)),
                  pl.BlockSpec((D, NPROJ), lambda i: (0, 0), pipeline_mode=pl.Buffered(1)),
                  row, row, const((8, D))],
        out_specs=[row, const((8, D))],
        out_shape=[jax.ShapeDtypeStruct((S, D), F32), jax.ShapeDtypeStruct((8, D), F32)],
        compiler_params=_cp(),
    )(*dqkvz, d_hgrn, d_gates, w_in, x, dx2, vecs)


def _adamw_math(w, g, m, v):
    m = B1 * m + (1.0 - B1) * g
    v = B2 * v + (1.0 - B2) * (g * g)
    m_hat = m / (1.0 - B1 ** STEP)
    v_hat = v / (1.0 - B2 ** STEP)
    delta = -LR * (m_hat / (jnp.sqrt(v_hat) + AEPS) + WD * w)
    return delta, m, v


def _adamw_call(w, g, m, v, name):
    R, C = w.shape
    tr = R if R * C * 4 <= (1 << 20) else max(8, (1 << 20) // (C * 4))
    assert R % tr == 0

    def body(w_ref, g_ref, m_ref, v_ref, go_ref, d_ref, nm_ref, nv_ref):
        g = g_ref[...]
        go_ref[...] = g
        d_ref[...], nm_ref[...], nv_ref[...] = _adamw_math(w_ref[...], g, m_ref[...], v_ref[...])

    blk = pl.BlockSpec((tr, C), lambda i: (i, 0))
    return pl.pallas_call(
        body, name=name, grid=(R // tr,), in_specs=[blk] * 4, out_specs=[blk] * 4,
        out_shape=[jax.ShapeDtypeStruct((R, C), F32)] * 4, compiler_params=_cp(),
    )(w, g, m, v)


def _mod_call(c_all, w_ada_s, b_s):
    def body(c_ref, w_ref, b_ref, o_ref):
        cv = c_ref[...]
        sc = cv * _sig(cv)
        o_ref[...] = jnp.dot(sc, w_ref[...], preferred_element_type=F32,
                             precision=lax.Precision.HIGHEST) + b_ref[...]

    return pl.pallas_call(
        body, name="ada_mod", out_shape=jax.ShapeDtypeStruct((8, w_ada_s.shape[1]), F32),
        compiler_params=_cp(),
    )(c_all, w_ada_s, b_s)


def _ada_update_call(sct, dm, w, m, v):
    R, C = w.shape
    tr = 256

    def body(s_ref, d_ref, w_ref, m_ref, v_ref, g_ref, dl_ref, nm_ref, nv_ref):
        g = s_ref[:, 0:1] * d_ref[0:1, :]
        for b in range(1, 8):
            g = g + s_ref[:, b:b + 1] * d_ref[b:b + 1, :]
        g_ref[...] = g
        dl_ref[...], nm_ref[...], nv_ref[...] = _adamw_math(w_ref[...], g, m_ref[...], v_ref[...])

    blk = pl.BlockSpec((tr, C), lambda i: (i, 0))
    return pl.pallas_call(
        body, name="ada_update", grid=(R // tr,),
        in_specs=[pl.BlockSpec((tr, 8), lambda i: (i, 0)), pl.BlockSpec((8, C), lambda i: (0, 0)), blk, blk, blk],
        out_specs=[blk] * 4, out_shape=[jax.ShapeDtypeStruct((R, C), F32)] * 4, compiler_params=_cp(),
    )(sct, dm, w, m, v)


def _sum8_call(packs):
    def body(p_ref, o_ref):
        acc = p_ref[0]
        for k in range(1, 8):
            acc = acc + p_ref[k]
        o_ref[...] = acc

    return pl.pallas_call(
        body, name="sum_small", out_shape=jax.ShapeDtypeStruct(packs.shape[1:], F32), compiler_params=_cp(),
    )(packs)


def _local_step(x, tgt, shift, scale, gate, norm_g, hgrn_onorm_g, rel_bias, lb, final_g, weights_fn, hook=None):
    a = norm_g * (1.0 + scale)
    z6 = jnp.zeros((6, D), F32)
    h, h_t, proj, w_in, rest_fn = weights_fn(x, jnp.concatenate([a, shift, z6], 0))

    biases = _bias_tiles(rel_bias) * LOG2E
    attn, lse, oa = _attn_fwd_call(proj, biases)

    gn = jnp.tile(hgrn_onorm_g, (1, NH))
    oraw, ob, states = _hgrn_fwd_call(proj, lb, gn)

    wa, wb, wo = rest_fn(ob)
    vecs2 = jnp.concatenate([gate, final_g, z6], 0)
    hsum = jnp.asarray(np.kron(np.eye(NH), np.ones((HE, HE))), BF16)
    y, dx2, du, dya, dyb, d_gates, dattn, dsum, dza, dob, acc2, lsq = _merge_call(
        oa, ob, proj, x, tgt, vecs2, attn, wa, wb, wo, hsum)
    loss = 0.5 * lsq[0, 0] / D
    d_wo, d_wo16 = _atb_call(y, du, "dw_out")
    d_wa, d_wa16 = _atb_call(oa, dya, "dw_branch_a")
    d_wb, d_wb16 = _atb_call(ob, dyb, "dw_branch_b")

    d_hgrn, acch = _hgrn_bwd_call(proj, oraw, dob, states, lb, gn)

    dq, dk, dv, dbs = _attn_bwd_call(proj, dattn, lse, dsum, biases)
    dqkvz = (dq, dk, dv, dza)

    d_win, d_win16 = _dwin_call(h_t, dqkvz, d_hgrn, d_gates)
    tok = hook((d_win, d_wa, d_wb, d_wo), (d_win16, d_wa16, d_wb16, d_wo16)) if hook is not None else 0.0
    one_scale = 1.0 + scale
    grad_x, acc1 = _dh_call(dqkvz, d_hgrn, d_gates, w_in, x, dx2,
                            jnp.concatenate([a + tok, norm_g, one_scale, jnp.zeros((5, D), F32)], 0))

    d_rel = jnp.zeros((NBUCKETS, NH), F32)
    for p, d in enumerate(PATTERNS):
        band, bucket = _band_bucket(d)
        onehot = (bucket[None] == np.arange(NBUCKETS)[:, None, None]) & band[None]
        d_rel = d_rel + jnp.einsum("hqk,bqk->bh", dbs[p], jnp.asarray(onehot, F32),
                                   precision=lax.Precision.HIGHEST)
    d_onorm = jnp.sum(acch[1].reshape(NH, HK), axis=0)

    zrow = jnp.zeros((D,), F32)
    pack = jnp.stack([acc1[0], acc1[1], acc2[0], acc1[2], acc2[1], acch[0],
                      zrow.at[:HK].set(d_onorm), zrow.at[0].set(loss),
                      zrow.at[:NBUCKETS * NH].set(d_rel.reshape(-1))] + [zrow] * 7, 0)
    return grad_x, d_win, d_wa, d_wb, d_wo, pack


def _me():
    return lax.axis_index("x"), lax.axis_index("y"), lax.axis_index("c")


def _peers(x, y):
    return [(1 - x, y), (x, 1 - y), (1 - x, 1 - y)]


def _allgather_small(blk, name):
    m_per, n = blk.shape

    def body(x_ref, out_ref, send_sems, recv_sems, local_sem):
        x, y, c = _me()
        me, sibling = (x, y, c), (x, y, 1 - c)
        chips = _peers(x, y)

        def rows(px, py, pc):
            return out_ref.at[pl.ds((4 * px + 2 * py + pc) * m_per, m_per), :]

        def copy(k, block, to, src=None):
            return pltpu.make_async_remote_copy(
                src_ref=rows(*block) if src is None else src, dst_ref=rows(*block),
                send_sem=send_sems.at[k], recv_sem=recv_sems.at[k], device_id=to, device_id_type=MESH)

        mine = pltpu.make_async_copy(x_ref, rows(*me), local_sem)
        mine.start()
        first = [copy(0, me, sibling, src=x_ref)]
        first += [copy(1 + j, me, (*chip, c), src=x_ref) for j, chip in enumerate(chips)]
        for cp in first:
            cp.start()
        passed = [copy(4 + j, (*chip, c), sibling) for j, chip in enumerate(chips)]
        for j, chip in enumerate(chips):
            copy(1 + j, (*chip, c), me).wait_recv()
            passed[j].start()
        copy(0, sibling, me).wait_recv()
        for j, chip in enumerate(chips):
            copy(4 + j, (*chip, 1 - c), me).wait_recv()
        for cp in first + passed:
            cp.wait_send()
        mine.wait()

    return pl.pallas_call(
        body, name=name, out_shape=jax.ShapeDtypeStruct((8 * m_per, n), blk.dtype),
        in_specs=[pl.BlockSpec(memory_space=pltpu.VMEM)], out_specs=pl.BlockSpec(memory_space=pltpu.VMEM),
        scratch_shapes=[pltpu.SemaphoreType.DMA((7,)), pltpu.SemaphoreType.DMA((7,)), pltpu.SemaphoreType.DMA],
    )(blk)


ANY = pl.BlockSpec(memory_space=pl.ANY)


HBM = pl.BlockSpec(memory_space=pltpu.HBM)
SEM = pl.BlockSpec(memory_space=pltpu.SEMAPHORE)
EFFECT = pltpu.SideEffectType.DATAFLOW_SIDE_EFFECTING


def _w_part(t, ref, j, half):
    if t == 0:
        return ref.at[pl.ds(half * (D // 2), D // 2), pl.ds(j * 2048, 2048)]
    if t == 1:
        return ref.at[pl.ds(half * (AW // 2), AW // 2), pl.ds(j * 256, 256)]
    return ref.at[pl.ds(j * 256 + half * 128, 128), :]


def _w_ici_copies(fulls, send_sems, recv_sems, tensors=(0, 1, 2, 3)):
    x, y, c = _me()
    outs, ins = [], []
    for u, t in enumerate(tensors):
        for k, chip in enumerate(_peers(x, y)):
            mine = _w_part(t, fulls[u], 2 * x + y, c)
            theirs = _w_part(t, fulls[u], 2 * chip[0] + chip[1], c)
            kw = dict(send_sem=send_sems.at[3 * t + k], recv_sem=recv_sems.at[3 * t + k],
                      device_id=(*chip, c), device_id_type=MESH)
            outs.append(pltpu.make_async_remote_copy(src_ref=mine, dst_ref=mine, **kw))
            ins.append(pltpu.make_async_remote_copy(src_ref=theirs, dst_ref=theirs, **kw))
    return outs, ins


def _gather_ici_start(fulls):
    def body(f0, f1, f2, f3, send_sems, recv_sems, t0, t1, t2, t3, token):
        for cp in _w_ici_copies([f0, f1, f2, f3], send_sems, recv_sems)[0]:
            cp.start()
        token[...] = jnp.zeros_like(token)

    res = pl.pallas_call(
        body, name="gather_ici_start",
        out_shape=(pltpu.SemaphoreType.DMA((12,)), pltpu.SemaphoreType.DMA((12,)),
                   *[pltpu.HBM(a.shape, a.dtype) for a in fulls], jax.ShapeDtypeStruct((8, 128), F32)),
        in_specs=[HBM] * 4, out_specs=(SEM, SEM, *[HBM] * 4, pl.BlockSpec(memory_space=pltpu.VMEM)),
        input_output_aliases={i: 2 + i for i in range(4)},
        compiler_params=pltpu.CompilerParams(has_side_effects=EFFECT),
    )(*[pltpu.with_memory_space_constraint(a, pltpu.HBM) for a in fulls])
    return res[0], res[1], list(res[2:6]), res[6]


def _gather_ici_wait(send_sems, recv_sems, bufs, after, tensors, name):
    n = len(tensors)

    def body(*refs):
        outs, ins = _w_ici_copies(refs[0:n], refs[n], refs[n + 1], tensors)
        for cp in outs:
            cp.wait_send()
        for cp in ins:
            cp.wait_recv()

    return pl.pallas_call(
        body, name=name, out_shape=tuple(pltpu.HBM(a.shape, a.dtype) for a in bufs),
        in_specs=[HBM] * n + [SEM, SEM, ANY], out_specs=[HBM] * n,
        input_output_aliases={i: i for i in range(n)},
        compiler_params=pltpu.CompilerParams(has_side_effects=EFFECT),
    )(*bufs, send_sems, recv_sems, after)


def _gather_sibling(bufs, tensors, name):
    n = len(tensors)

    def body(*refs):
        outs, send_sems, recv_sems = refs[n:2 * n], refs[2 * n], refs[2 * n + 1]
        x, y, c = _me()
        cps = []
        for u, t in enumerate(tensors):
            for k, chip in enumerate(_peers(x, y)):
                blk = _w_part(t, outs[u], 2 * chip[0] + chip[1], c)
                cp = pltpu.make_async_remote_copy(
                    src_ref=blk, dst_ref=blk, send_sem=send_sems.at[u, k], recv_sem=recv_sems.at[u, k],
                    device_id=(x, y, 1 - c), device_id_type=MESH)
                cp.start()
                cps.append(cp)
        for u, t in enumerate(tensors):
            for k, chip in enumerate(_peers(x, y)):
                blk = _w_part(t, outs[u], 2 * chip[0] + chip[1], 1 - c)
                pltpu.make_async_remote_copy(
                    src_ref=blk, dst_ref=blk, send_sem=send_sems.at[u, k], recv_sem=recv_sems.at[u, k],
                    device_id=(x, y, 1 - c), device_id_type=MESH).wait_recv()
        for cp in cps:
            cp.wait_send()

    return pl.pallas_call(
        body, name=name,
        out_shape=[jax.ShapeDtypeStruct(FULL_W_SHAPES[t], BF16) for t in tensors],
        in_specs=[ANY] * n, out_specs=[ANY] * n, input_output_aliases={u: u for u in range(n)},
        scratch_shapes=[pltpu.SemaphoreType.DMA((n, 3)), pltpu.SemaphoreType.DMA((n, 3))],
    )(*bufs)


def _half_of(t, ref, half):
    if t == 0:
        return ref.at[:, pl.ds(half * 512, 512), :]
    if t == 1:
        return ref.at[pl.ds(half * 256, 256), :]
    return ref.at[:, pl.ds(half * 512, 512)]


HALF_SHAPES = [(4, 512, 2048), (256, D), (D, 512), (D, 512)]
PIECE_SHAPES = [(512, 2048), (256, 256), (256, 512), (256, 512)]
SHARD_SHAPES = [(D, 2048), (AW, 256), (256, D), (256, D)]


def _chip_piece(t, ref, j):
    if t == 0:
        return ref.at[j]
    if t == 1:
        return ref.at[:, pl.ds(j * 256, 256)]
    return ref.at[pl.ds(j * 256, 256), :]


def _reduce_sibling_send(gs):
    def body(g0, g1, g2, g3, r0, r1, r2, r3, send_sems, recv_sems):
        x, y, c = _me()
        ins, outs = [g0, g1, g2, g3], [r0, r1, r2, r3]
        cps = []
        for t in range(4):
            cp = pltpu.make_async_remote_copy(
                src_ref=_half_of(t, ins[t], 1 - c), dst_ref=outs[t],
                send_sem=send_sems.at[t], recv_sem=recv_sems.at[t], device_id=(x, y, 1 - c), device_id_type=MESH)
            cp.start()
            cps.append(cp)
        for cp in cps:
            cp.wait_recv()
        for cp in cps:
            cp.wait_send()

    return pl.pallas_call(
        body, name="reduce_sibling", out_shape=[jax.ShapeDtypeStruct(s, BF16) for s in HALF_SHAPES],
        in_specs=[ANY] * 4, out_specs=[ANY] * 4,
        scratch_shapes=[pltpu.SemaphoreType.DMA((4,)), pltpu.SemaphoreType.DMA((4,))],
    )(*gs)


def _chip_copies(hs, lands, send_sems, recv_sems):
    x, y, c = _me()
    cps = []
    for t in range(4):
        for k, chip in enumerate(_peers(x, y)):
            pj = 2 * chip[0] + chip[1]
            cps.append(pltpu.make_async_remote_copy(
                src_ref=_chip_piece(t, hs[t], pj), dst_ref=lands[t].at[k],
                send_sem=send_sems.at[3 * t + k], recv_sem=recv_sems.at[3 * t + k],
                device_id=(*chip, c), device_id_type=MESH))
    return cps


def _reduce_chips_start(hs):
    lands = [lax.empty((3,) + s, BF16) for s in PIECE_SHAPES]

    def body(h0, h1, h2, h3, l0, l1, l2, l3, send_sems, recv_sems, t0, t1, t2, t3, t4, t5, t6, t7, token):
        for cp in _chip_copies([h0, h1, h2, h3], [l0, l1, l2, l3], send_sems, recv_sems):
            cp.start()
        token[...] = jnp.zeros_like(token)

    bufs = list(hs) + lands
    res = pl.pallas_call(
        body, name="reduce_chips_start",
        out_shape=(pltpu.SemaphoreType.DMA((12,)), pltpu.SemaphoreType.DMA((12,)),
                   *[pltpu.HBM(a.shape, a.dtype) for a in bufs], jax.ShapeDtypeStruct((8, 128), F32)),
        in_specs=[HBM] * 8, out_specs=(SEM, SEM, *[HBM] * 8, pl.BlockSpec(memory_space=pltpu.VMEM)),
        input_output_aliases={i: 2 + i for i in range(8)},
        compiler_params=pltpu.CompilerParams(has_side_effects=EFFECT),
    )(*[pltpu.with_memory_space_constraint(a, pltpu.HBM) for a in bufs])
    return res[0], res[1], list(res[2:10]), res[10]


def _reduce_chips_wait(send_sems, recv_sems, thru, after):
    def body(h0, h1, h2, h3, l0, l1, l2, l3, send_sems, recv_sems, after_ref, d0, d1, d2, d3, g0, g1, g2, g3):
        cps = _chip_copies([h0, h1, h2, h3], [l0, l1, l2, l3], send_sems, recv_sems)
        for cp in cps:
            cp.wait_send()
        for cp in cps:
            cp.wait_recv()

    res = pl.pallas_call(
        body, name="reduce_chips_wait", out_shape=tuple(pltpu.HBM(a.shape, a.dtype) for a in thru),
        in_specs=[HBM] * 8 + [SEM, SEM, ANY], out_specs=[HBM] * 8,
        input_output_aliases={i: i for i in range(8)},
        compiler_params=pltpu.CompilerParams(has_side_effects=EFFECT),
    )(*thru, send_sems, recv_sems, after)
    return list(res[4:8])


def _share_sibling(shards):
    def body(i0, i1, i2, i3, o0, o1, o2, o3, send_sems, recv_sems):
        x, y, c = _me()
        outs = [o0, o1, o2, o3]

        def half(t, ref, hf):
            if t == 0:
                return ref.at[pl.ds(hf * 512, 512), :]
            if t == 1:
                return ref.at[pl.ds(hf * 256, 256), :]
            return ref.at[:, pl.ds(hf * 512, 512)]

        cps = []
        for t in range(4):
            mine = half(t, outs[t], c)
            cp = pltpu.make_async_remote_copy(
                src_ref=mine, dst_ref=mine, send_sem=send_sems.at[t], recv_sem=recv_sems.at[t],
                device_id=(x, y, 1 - c), device_id_type=MESH)
            cp.start()
            cps.append(cp)
        for t in range(4):
            theirs = half(t, outs[t], 1 - c)
            pltpu.make_async_remote_copy(
                src_ref=theirs, dst_ref=theirs, send_sem=send_sems.at[t],
                recv_sem=recv_sems.at[t], device_id=(x, y, 1 - c), device_id_type=MESH).wait_recv()
        for cp in cps:
            cp.wait_send()

    return pl.pallas_call(
        body, name="share_sibling", out_shape=[jax.ShapeDtypeStruct(s, F32) for s in SHARD_SHAPES],
        in_specs=[ANY] * 4, out_specs=[ANY] * 4, input_output_aliases={0: 0, 1: 1, 2: 2, 3: 3},
        scratch_shapes=[pltpu.SemaphoreType.DMA((4,)), pltpu.SemaphoreType.DMA((4,))],
    )(*shards)


def _half_blockspec(t, idx_pos):
    if t == 0:
        return pl.BlockSpec((1, 512, 2048), lambda i, s: (i, s[idx_pos], 0)), 4
    if t == 1:
        return pl.BlockSpec((256, D), lambda i, s: (s[idx_pos], 0)), 1
    return pl.BlockSpec((256, 512), lambda i, s: (i, s[idx_pos])), 4


def _half_out_blockspec(t):
    if t == 0:
        return pl.BlockSpec((1, 512, 2048), lambda i, s: (i, 0, 0))
    if t == 1:
        return pl.BlockSpec((256, D), lambda i, s: (0, 0))
    return pl.BlockSpec((256, 512), lambda i, s: (i, 0))


def _add_half_call(t, own, recv, sc, name):
    in_blk, steps = _half_blockspec(t, 0)
    out_blk = _half_out_blockspec(t)

    def body(s_ref, a_ref, b_ref, o_ref, ob_ref):
        v = a_ref[...] + b_ref[...].astype(F32)
        o_ref[...] = v
        ob_ref[...] = v.astype(BF16)

    return pl.pallas_call(
        body, name=name,
        grid_spec=pltpu.PrefetchScalarGridSpec(
            num_scalar_prefetch=1, grid=(steps,), in_specs=[in_blk, out_blk], out_specs=[out_blk, out_blk]),
        out_shape=[jax.ShapeDtypeStruct(HALF_SHAPES[t], F32), jax.ShapeDtypeStruct(HALF_SHAPES[t], BF16)],
        compiler_params=_cp(),
    )(sc, own, recv)


def _final_piece_call(t, chipsum, recv3, sc, name):
    ps = PIECE_SHAPES[t]
    if t == 0:
        own_blk = pl.BlockSpec((1,) + ps, lambda i, s: (s[1], 0, 0))
        o_blk = pl.BlockSpec(ps, lambda i, s: (s[0], 0))
    elif t == 1:
        own_blk = pl.BlockSpec(ps, lambda i, s: (0, s[1]))
        o_blk = pl.BlockSpec(ps, lambda i, s: (s[0], 0))
    else:
        own_blk = pl.BlockSpec(ps, lambda i, s: (s[1], 0))
        o_blk = pl.BlockSpec(ps, lambda i, s: (0, s[0]))
    r_blk = pl.BlockSpec((3,) + ps, lambda i, s: (0, 0, 0))

    def body(s_ref, a_ref, r_ref, o_ref):
        a = a_ref[0] if t == 0 else a_ref[...]
        o_ref[...] = ((a + r_ref[0].astype(F32)) + r_ref[1].astype(F32)) + r_ref[2].astype(F32)

    return pl.pallas_call(
        body, name=name,
        grid_spec=pltpu.PrefetchScalarGridSpec(
            num_scalar_prefetch=1, grid=(1,), in_specs=[own_blk, r_blk], out_specs=o_blk),
        out_shape=jax.ShapeDtypeStruct(SHARD_SHAPES[t], F32), compiler_params=_cp(),
    )(sc, chipsum, recv3)


FULL_W_SHAPES = [(D, NPROJ), (AW, D), (D, D), (D, D)]


def _cast_place_call(t, shard, sc, name):
    if t == 0:
        blk, steps = (512, 2048), 2
        in_blk = pl.BlockSpec(blk, lambda i, s: (i, 0))
        o_blk = pl.BlockSpec(blk, lambda i, s: (i, s[1]))
    elif t == 1:
        blk, steps = (AW, 256), 1
        in_blk = pl.BlockSpec(blk, lambda i, s: (0, 0))
        o_blk = pl.BlockSpec(blk, lambda i, s: (0, s[1]))
    else:
        blk, steps = (256, D), 1
        in_blk = pl.BlockSpec(blk, lambda i, s: (0, 0))
        o_blk = pl.BlockSpec(blk, lambda i, s: (s[1], 0))

    def body(s_ref, a_ref, o_ref, own_ref):
        v = a_ref[...].astype(BF16)
        o_ref[...] = v
        own_ref[...] = v

    return pl.pallas_call(
        body, name=name,
        grid_spec=pltpu.PrefetchScalarGridSpec(
            num_scalar_prefetch=1, grid=(steps,), in_specs=[in_blk], out_specs=[o_blk, in_blk]),
        out_shape=[jax.ShapeDtypeStruct(FULL_W_SHAPES[t], BF16), jax.ShapeDtypeStruct(shard.shape, BF16)],
        compiler_params=_cp(),
    )(sc, shard)


def _lower_bound_fn(hgrn_lb):
    return jnp.cumsum(jax.nn.softmax(hgrn_lb.astype(F32), axis=0), axis=0)[0]


def kernel(x, c, w_ada, b_ada, norm_g, w_in, hgrn_onorm_g, w_branch_a, w_branch_b, w_out, rel_bias, hgrn_lb, final_g, loss_target, m_w_ada, m_b_ada, m_norm_g, m_w_in, m_hgrn_onorm_g, m_w_branch_a, m_w_branch_b, m_w_out, m_rel_bias, m_hgrn_lb, m_final_g, v_w_ada, v_b_ada, v_norm_g, v_w_in, v_hgrn_onorm_g, v_w_branch_a, v_w_branch_b, v_w_out, v_rel_bias, v_hgrn_lb, v_final_g):
    ax, ay, ac = _me()
    chip = 2 * ax + ay
    dev = 4 * ax + 2 * ay + ac
    sc_idx = jnp.stack([ac, chip]).astype(jnp.int32)

    c_all = _allgather_small(jnp.pad(c, ((0, 7), (0, 0))), "gather_c").reshape(8, 8, D)[:, 0]
    b_s = lax.dynamic_slice(b_ada, (0, chip * 768), (1, 768))
    mod_part = _mod_call(c_all, w_ada[0], b_s)
    mod_all = _allgather_small(mod_part, "gather_mod").reshape(8, 8, 768)
    mod_mine = lax.dynamic_index_in_dim(mod_all, dev, axis=1, keepdims=False)
    mod = mod_mine[0::2].reshape(1, 3 * D)

    names = ["w_in", "w_a", "w_b", "w_o"]
    shards, mod = lax.optimization_barrier(([w_in[0], w_branch_a[0], w_branch_b[0], w_out[0]], mod))
    placed = [_cast_place_call(t, shards[t], sc_idx, "cast_" + names[t]) for t in range(4)]
    w_send_sems, w_recv_sems, w_thru, w_token = _gather_ici_start([p_[0] for p_ in placed])
    mod = mod + w_token[0, 0]
    rel_bias_t = rel_bias + w_token[0, 0]
    shift, scale, gate = mod[:, :D], mod[:, D:2 * D], mod[:, 2 * D:]

    def weights_fn(xs, avec):
        proj, h, h_t = _proj_own_call(xs, avec, placed[0][1], sc_idx)
        arrived = _gather_ici_wait(w_send_sems, w_recv_sems, w_thru[0:1], proj, (0,), "gather_ici_wait_in")
        (win_f,) = _gather_sibling(arrived, (0,), "gather_sibling_in")

        def rest_fn(after):
            late = _gather_ici_wait(w_send_sems, w_recv_sems, w_thru[1:4], after, (1, 2, 3), "gather_ici_wait_rest")
            return _gather_sibling(late, (1, 2, 3), "gather_sibling_rest")

        return h, h_t, _proj_rest_call(h, win_f, proj, sc_idx), win_f, rest_fn

    flight = {}

    def start_reduction(own, own16):
        sib = _reduce_sibling_send(own16)
        halves = [_add_half_call(t, own[t], sib[t], sc_idx, "chipsum_" + names[t]) for t in range(4)]
        send_sems, recv_sems, thru, token = _reduce_chips_start([hb for _, hb in halves])
        flight.update(sems=(send_sems, recv_sems), thru=thru, sums=[hf for hf, _ in halves])
        return token[0, 0]

    lb, lb_vjp = jax.vjp(_lower_bound_fn, hgrn_lb)
    grad_x, d_win, d_wa, d_wb, d_wo, pack = _local_step(
        x[0], loss_target[0], shift, scale, gate, norm_g, hgrn_onorm_g, rel_bias_t, lb[None, :],
        final_g[None, :], weights_fn, hook=start_reduction)

    rec = _reduce_chips_wait(*flight["sems"], flight["thru"], pack)
    pieces = [_final_piece_call(t, flight["sums"][t], rec[t], sc_idx, "piece_" + names[t]) for t in range(4)]
    g_win, g_wa, g_wb, g_wo = _share_sibling(pieces)

    packs = _allgather_small(pack, "gather_small").reshape(8, 16, D)
    tot = _sum8_call(packs)
    loss = tot[7, 0]
    g_b_ada = tot[0:3].reshape(1, 3 * D)
    g_norm_g = tot[3:4]
    g_final_g = tot[4]
    (g_hgrn_lb,) = lb_vjp(tot[5])
    g_onorm = tot[6:7, :HK]
    g_rel = tot[8, :NBUCKETS * NH].reshape(NBUCKETS, NH)

    def rows_of(a):
        flat = a.reshape(-1)
        n = -(-flat.shape[0] // D)
        return jnp.pad(flat, (0, n * D - flat.shape[0])).reshape(n, D)

    smalls = [(b_ada, g_b_ada, m_b_ada, v_b_ada), (norm_g, g_norm_g, m_norm_g, v_norm_g),
              (hgrn_onorm_g, g_onorm, m_hgrn_onorm_g, v_hgrn_onorm_g), (rel_bias, g_rel, m_rel_bias, v_rel_bias),
              (hgrn_lb, g_hgrn_lb, m_hgrn_lb, v_hgrn_lb), (final_g, g_final_g, m_final_g, v_final_g)]
    cat = [jnp.concatenate([rows_of(s[k]) for s in smalls], 0) for k in range(4)]
    cat = [jnp.pad(a, ((0, 16 - a.shape[0]), (0, 0))) for a in cat]
    _, sd, sm, sv = _adamw_call(*cat, "adamw_small")

    def unpack(packed):
        res, r = [], 0
        for s in smalls:
            n = -(-s[0].size // D)
            res.append(packed[r:r + n].reshape(-1)[:s[0].size].reshape(s[0].shape))
            r += n
        return res

    d_small, m_small, v_small = unpack(sd), unpack(sm), unpack(sv)

    sc_all = c_all * jax.nn.sigmoid(c_all)
    dmod_all = packs[:, 0:3].reshape(8, 3 * D)
    dm_s = lax.dynamic_slice(dmod_all, (0, chip * 768), (8, 768))
    g_w_ada, d_w_ada, nm_w_ada, nv_w_ada = _ada_update_call(sc_all.T, dm_s, w_ada[0], m_w_ada[0], v_w_ada[0])

    big = []
    for w, g, m, v, n in [(w_in, g_win, m_w_in, v_w_in, "w_in"), (w_branch_a, g_wa, m_w_branch_a, v_w_branch_a, "w_a"),
                          (w_branch_b, g_wb, m_w_branch_b, v_w_branch_b, "w_b"), (w_out, g_wo, m_w_out, v_w_out, "w_o")]:
        big.append(_adamw_call(w[0], g, m[0], v[0], "adamw_" + n))

    e = lambda a: a[None]
    grads = [e(g_w_ada), g_b_ada, g_norm_g, e(big[0][0]), g_onorm, e(big[1][0]), e(big[2][0]), e(big[3][0]),
             g_rel, g_hgrn_lb, g_final_g]
    deltas = [e(d_w_ada), d_small[0], d_small[1], e(big[0][1]), d_small[2], e(big[1][1]), e(big[2][1]), e(big[3][1]),
              d_small[3], d_small[4], d_small[5]]
    new_m = [e(nm_w_ada), m_small[0], m_small[1], e(big[0][2]), m_small[2], e(big[1][2]), e(big[2][2]), e(big[3][2]),
             m_small[3], m_small[4], m_small[5]]
    new_v = [e(nv_w_ada), v_small[0], v_small[1], e(big[0][3]), v_small[2], e(big[1][3]), e(big[2][3]), e(big[3][3]),
             v_small[3], v_small[4], v_small[5]]
    return (loss, grad_x[None], *grads, *deltas, *new_m, *new_v)
```

```python
import math

import numpy as np
import jax
import jax.numpy as jnp
from jax import lax
from jax.experimental import pallas as pl
from jax.experimental.pallas import tpu as pltpu

D = 1024
AW = 512
NH = 8
HE = 64
HK = 128
NPROJ = 8192
ABLK = 128
PATTERNS = (1, 4, 16)
NBUCKETS = 32
MAXDIST = 2048
NEG = -1e30
EPS = 1e-6
CH = 64
LR, B1, B2, AEPS, WD, STEP = 0.001, 0.9, 0.999, 1e-08, 0.01, 10

F32 = jnp.float32
BF16 = jnp.bfloat16
MESH = pl.DeviceIdType.MESH
VMEM_LIMIT = 56 * 1024 * 1024


def _cp(**kw):
    return pltpu.CompilerParams(vmem_limit_bytes=VMEM_LIMIT, **kw)


def _sig(x):
    return 0.5 * jnp.tanh(0.5 * x) + 0.5


def _nt(a, b):
    return lax.dot_general(a, b, (((1,), (1,)), ((), ())), preferred_element_type=F32)


def _tn(a, b):
    return lax.dot_general(a, b, (((0,), (0,)), ((), ())), preferred_element_type=F32)


def _nn(a, b):
    return jnp.dot(a, b, preferred_element_type=F32)


def _split2(x):
    h = x.astype(BF16)
    return h, (x - h.astype(F32)).astype(BF16)


def _exact_mm(tri_bf16, x):
    h, l = _split2(x)
    return _nn(tri_bf16, h) + _nn(tri_bf16, l)


def _exact_mm_r(x, ones_bf16):
    h, l = _split2(x)
    return _nn(h, ones_bf16) + _nn(l, ones_bf16)


def _proj_own_call(x, avec, w_own, sc):
    S = x.shape[0]
    tm, tn = 512, 2048

    def body(s_ref, x_ref, a_ref, w_ref, o_ref, h_ref, ht_ref):
        xv = x_ref[...]
        r = lax.rsqrt(jnp.mean(xv * xv, axis=-1, keepdims=True) + EPS)
        hv = xv * r * a_ref[0:1, :] + a_ref[1:2, :]
        hb = hv.astype(BF16)
        h_ref[...] = hb
        ht_ref[...] = hv.T.astype(BF16)
        o_ref[...] = _nn(hb, w_ref[...])

    return pl.pallas_call(
        body, name="in_proj_own",
        grid_spec=pltpu.PrefetchScalarGridSpec(
            num_scalar_prefetch=1, grid=(S // tm,),
            in_specs=[pl.BlockSpec((tm, D), lambda i, s: (i, 0)), pl.BlockSpec((8, D), lambda i, s: (0, 0)),
                      pl.BlockSpec((D, tn), lambda i, s: (0, 0))],
            out_specs=[pl.BlockSpec((tm, tn), lambda i, s: (i, s[1])), pl.BlockSpec((tm, D), lambda i, s: (i, 0)),
                       pl.BlockSpec((D, tm), lambda i, s: (0, i))]),
        out_shape=[jax.ShapeDtypeStruct((S, NPROJ), F32), jax.ShapeDtypeStruct((S, D), BF16),
                   jax.ShapeDtypeStruct((D, S), BF16)],
        compiler_params=_cp(),
    )(sc, x, avec, w_own)


def _proj_rest_call(h, w_in, proj, sc):
    S = h.shape[0]
    tm, tn = 512, 2048

    def body(s_ref, h_ref, w_ref, p_ref, o_ref):
        o_ref[...] = _nn(h_ref[...], w_ref[...])

    col = lambda j, s: (s[1] + 1 + j) % 4
    return pl.pallas_call(
        body, name="in_proj_rest",
        grid_spec=pltpu.PrefetchScalarGridSpec(
            num_scalar_prefetch=1, grid=(3, S // tm),
            in_specs=[pl.BlockSpec((tm, D), lambda j, i, s: (i, 0)),
                      pl.BlockSpec((D, tn), lambda j, i, s: (0, col(j, s))),
                      pl.BlockSpec(memory_space=pl.ANY)],
            out_specs=pl.BlockSpec((tm, tn), lambda j, i, s: (i, col(j, s)))),
        out_shape=jax.ShapeDtypeStruct((S, NPROJ), F32), input_output_aliases={3: 0}, compiler_params=_cp(),
    )(sc, h, w_in, proj)


def _t5_bucket_np(dist):
    max_exact = NBUCKETS // 2
    n = dist.astype(np.float32)
    large = max_exact + (np.log(np.maximum(n, np.float32(1.0)) / np.float32(max_exact))
                         / np.float32(math.log(MAXDIST / max_exact))
                         * np.float32(NBUCKETS - max_exact)).astype(np.int32)
    large = np.minimum(large, NBUCKETS - 1)
    return np.where(dist < max_exact, dist, large)


def _band_bucket(d):
    qi = np.arange(ABLK)[:, None]
    kj = np.arange(2 * ABLK)[None, :]
    delta = qi + ABLK - kj
    band = (delta >= 0) & (delta <= ABLK)
    bucket = _t5_bucket_np(np.clip(delta, 0, None) * d)
    return band, bucket


def _bias_tiles(rel_bias):
    tiles = []
    for d in PATTERNS:
        band, bucket = _band_bucket(d)
        onehot = (jnp.asarray(bucket, jnp.int32)[None] == jnp.arange(NBUCKETS, dtype=jnp.int32)[:, None, None])
        bias = jnp.einsum("bqk,bh->hqk", onehot.astype(F32), rel_bias, precision=lax.Precision.HIGHEST)
        tiles.append(jnp.where(jnp.asarray(band)[None], bias, NEG))
    return jnp.stack(tiles, 0)


ATT = 2048
HP = 2 * HE
LOG2E = 1.4426950408889634
LN2 = 0.6931471805599453
QSCALE2 = (HE ** -0.5) * LOG2E
AGRP = 16
AGRP_B = 8


def _attn_blocks():
    out = []
    for p, d in enumerate(PATTERNS):
        for r in range(d):
            for n in range(ATT // (d * ABLK)):
                out.append((p, d, r, n))
    return out


def _attn_fwd_call(proj, biases):
    S = proj.shape[0]
    nt = S // ATT

    def body(q_ref, k_ref, v_ref, z_ref, b_ref, a_ref, l_ref, oa_ref, kc, vc, op, lp):
        i = pl.program_id(1)

        @pl.when(i == 0)
        def _():
            kc[0:ATT] = jnp.zeros((ATT, HP), F32)
            vc[0:ATT] = jnp.zeros((ATT, HP), F32)

        @pl.when(i > 0)
        def _():
            kc[0:ATT] = kc[ATT:2 * ATT]
            vc[0:ATT] = vc[ATT:2 * ATT]

        kc[ATT:2 * ATT] = k_ref[...]
        vc[ATT:2 * ATT] = v_ref[...]
        col = lax.broadcasted_iota(jnp.int32, (ABLK, 2 * ABLK), 1)
        dead = jnp.logical_and(i == 0, col < ABLK)
        blocks = _attn_blocks()
        hs = (slice(0, HE), slice(HE, 2 * HE))
        for g0 in range(0, len(blocks), AGRP):
            grp = blocks[g0:g0 + AGRP]
            qrows = [pl.ds(n * ABLK * d + r, ABLK, stride=d) for p, d, r, n in grp]
            krows = [pl.ds(ATT + (n - 1) * ABLK * d + r, 2 * ABLK, stride=d) for p, d, r, n in grp]
            qs = [(q_ref[qr, :] * QSCALE2).astype(BF16) for qr in qrows]
            ks = [kc[kr, :].astype(BF16) for kr in krows]
            vs = [vc[kr, :].astype(BF16) for kr in krows]
            ss = [[_nt(qs[b][:, sl], ks[b][:, sl]) + b_ref[grp[b][0], e] for e, sl in enumerate(hs)]
                  for b in range(len(grp))]
            ss = [[jnp.where(dead, NEG, s) if grp[b][3] == 0 else s for s in ss[b]] for b in range(len(grp))]
            mxs = [[jnp.max(s, axis=-1, keepdims=True) for s in sb] for sb in ss]
            pes = [[jnp.exp2(s - mx) for s, mx in zip(sb, mb)] for sb, mb in zip(ss, mxs)]
            dens = [[jnp.sum(pe, axis=-1, keepdims=True) for pe in pb] for pb in pes]
            pvs = [[_nn(pe.astype(BF16), vs[b][:, sl]) for pe, sl in zip(pes[b], hs)] for b in range(len(grp))]
            for b in range(len(grp)):
                p, d, r, n = grp[b]
                prow = pl.ds(p * ATT + n * ABLK * d + r, ABLK, stride=d)
                lp[prow, :] = jnp.concatenate(
                    [jnp.broadcast_to(mx + jnp.log2(dn), (ABLK, HE)) for mx, dn in zip(mxs[b], dens[b])], axis=1)
                op[prow, :] = jnp.concatenate([pv / dn for pv, dn in zip(pvs[b], dens[b])], axis=1)
        rt = 256
        for t in range(ATT // rt):
            rows = slice(t * rt, (t + 1) * rt)
            pr = [slice(p * ATT + t * rt, p * ATT + (t + 1) * rt) for p in range(3)]
            la, lb_, lc = lp[pr[0], :], lp[pr[1], :], lp[pr[2], :]
            m = jnp.maximum(jnp.maximum(la, lb_), lc)
            ea, eb, ec = jnp.exp2(la - m), jnp.exp2(lb_ - m), jnp.exp2(lc - m)
            den = ea + eb + ec
            att = (ea * op[pr[0], :] + eb * op[pr[1], :] + ec * op[pr[2], :]) / den
            a_ref[rows, :] = att
            l_ref[rows, :] = m + jnp.log2(den)
            z = z_ref[rows, :]
            oa_ref[rows, :] = (att * (z * _sig(z))).astype(BF16)

    def pcol(c):
        return pl.BlockSpec((ATT, HP), lambda h, i: (i, c * 4 + h))

    out = pl.BlockSpec((ATT, HP), lambda h, i: (i, h))
    return pl.pallas_call(
        body, name="attn_fwd", grid=(4, nt),
        in_specs=[pcol(0), pcol(1), pcol(2), pcol(3),
                  pl.BlockSpec((3, 2, ABLK, 2 * ABLK), lambda h, i: (0, h, 0, 0))],
        out_specs=[out, out, out],
        out_shape=[jax.ShapeDtypeStruct((S, AW), F32), jax.ShapeDtypeStruct((S, AW), F32),
                   jax.ShapeDtypeStruct((S, AW), BF16)],
        scratch_shapes=[pltpu.VMEM((2 * ATT, HP), F32), pltpu.VMEM((2 * ATT, HP), F32),
                        pltpu.VMEM((3 * ATT, HP), F32), pltpu.VMEM((3 * ATT, HP), F32)],
        compiler_params=_cp(),
    )(proj, proj, proj, proj, biases)


def _attn_bwd_call(proj, dattn, lse, dsum, biases):
    S = proj.shape[0]
    nt = S // ATT

    def body(q_ref, k_ref, v_ref, do_ref, l_ref, ds_ref, b_ref, dq_ref, dk_ref, dv_ref, db_ref,
             kc, vc, dkc, dvc, dqa):
        i = pl.program_id(1)

        @pl.when(i == 0)
        def _():
            kc[ATT:2 * ATT] = jnp.zeros((ATT, HP), F32)
            vc[ATT:2 * ATT] = jnp.zeros((ATT, HP), F32)
            dkc[ATT:2 * ATT] = jnp.zeros((ATT, HP), F32)
            dvc[ATT:2 * ATT] = jnp.zeros((ATT, HP), F32)
            db_ref[...] = jnp.zeros_like(db_ref)

        @pl.when(i < nt)
        def _():
            kc[0:ATT] = kc[ATT:2 * ATT]
            vc[0:ATT] = vc[ATT:2 * ATT]
            dkc[0:ATT] = dkc[ATT:2 * ATT]
            dvc[0:ATT] = dvc[ATT:2 * ATT]
            kc[ATT:2 * ATT] = k_ref[...]
            vc[ATT:2 * ATT] = v_ref[...]
            dkc[ATT:2 * ATT] = jnp.zeros((ATT, HP), F32)
            dvc[ATT:2 * ATT] = jnp.zeros((ATT, HP), F32)
            col = lax.broadcasted_iota(jnp.int32, (ABLK, 2 * ABLK), 1)
            dead = jnp.logical_and(i == 0, col < ABLK)
            blocks = _attn_blocks()
            hs = (slice(0, HE), slice(HE, 2 * HE))
            for g0 in range(0, len(blocks), AGRP_B):
                grp = blocks[g0:g0 + AGRP_B]
                nb_ = range(len(grp))
                qrows = [pl.ds(n * ABLK * d + r, ABLK, stride=d) for p, d, r, n in grp]
                krows = [pl.ds(ATT + (n - 1) * ABLK * d + r, 2 * ABLK, stride=d) for p, d, r, n in grp]
                qs = [(q_ref[qr, :] * QSCALE2).astype(BF16) for qr in qrows]
                ks = [kc[kr, :].astype(BF16) for kr in krows]
                vs = [vc[kr, :].astype(BF16) for kr in krows]
                dos = [do_ref[qr, :].astype(BF16) for qr in qrows]
                lvs = [l_ref[qr, :] for qr in qrows]
                dsvs = [ds_ref[qr, :] for qr in qrows]
                ss = [[_nt(qs[b][:, sl], ks[b][:, sl]) + b_ref[grp[b][0], e] for e, sl in enumerate(hs)] for b in nb_]
                ss = [[jnp.where(dead, NEG, s) if grp[b][3] == 0 else s for s in ss[b]] for b in nb_]
                dps = [[_nt(dos[b][:, sl], vs[b][:, sl]) for sl in hs] for b in nb_]
                pes = [[jnp.exp2(ss[b][e] - lvs[b][:, e * HE:e * HE + 1]) for e in range(2)] for b in nb_]
                dscs = [[pes[b][e] * (dps[b][e] - dsvs[b][:, e * HE:e * HE + 1]) for e in range(2)] for b in nb_]
                for e in range(2):
                    tot = dscs[0][e]
                    for b in range(1, len(grp)):
                        tot = tot + dscs[b][e]
                    db_ref[grp[0][0], e] += tot
                dsbs = [[t.astype(BF16) for t in tb] for tb in dscs]
                dqs = [[_nn(dsbs[b][e], ks[b][:, sl]) * (HE ** -0.5) for e, sl in enumerate(hs)] for b in nb_]
                dks = [[_tn(dsbs[b][e], qs[b][:, sl]) for e, sl in enumerate(hs)] for b in nb_]
                dvs = [[_tn(pes[b][e].astype(BF16), dos[b][:, sl]) for e, sl in enumerate(hs)] for b in nb_]
                for b in nb_:
                    dq = jnp.concatenate(dqs[b], axis=1)
                    if grp[b][0] == 0:
                        dqa[qrows[b], :] = dq
                    else:
                        dqa[qrows[b], :] += dq
                    dkc[krows[b], :] += jnp.concatenate(dks[b], axis=1)
                    dvc[krows[b], :] += jnp.concatenate(dvs[b], axis=1)
            dq_ref[...] = dqa[...].astype(BF16)
            dk_ref[...] = (dkc[0:ATT] * LN2).astype(BF16)
            dv_ref[...] = dvc[0:ATT].astype(BF16)

        @pl.when(i == nt)
        def _():
            dk_ref[...] = (dkc[ATT:2 * ATT] * LN2).astype(BF16)
            dv_ref[...] = dvc[ATT:2 * ATT].astype(BF16)

    def pcol(c):
        return pl.BlockSpec((ATT, HP), lambda h, i: (jnp.minimum(i, nt - 1), c * 4 + h))

    qrow = pl.BlockSpec((ATT, HP), lambda h, i: (jnp.minimum(i, nt - 1), h))
    krow = pl.BlockSpec((ATT, HP), lambda h, i: (jnp.maximum(i - 1, 0), h))
    bspec = pl.BlockSpec((3, 2, ABLK, 2 * ABLK), lambda h, i: (0, h, 0, 0))
    return pl.pallas_call(
        body, name="attn_bwd", grid=(4, nt + 1),
        in_specs=[pcol(0), pcol(1), pcol(2), qrow, qrow, qrow, bspec],
        out_specs=[qrow, krow, krow, bspec],
        out_shape=[jax.ShapeDtypeStruct((S, AW), BF16)] * 3
                  + [jax.ShapeDtypeStruct((3, NH, ABLK, 2 * ABLK), F32)],
        scratch_shapes=[pltpu.VMEM((2 * ATT, HP), F32)] * 4 + [pltpu.VMEM((ATT, HP), F32)],
        compiler_params=_cp(),
    )(proj, proj, proj, dattn, lse, dsum, biases)


HRB = 512


def _tri_masks():
    row = lax.broadcasted_iota(jnp.int32, (CH, CH), 0)
    col = lax.broadcasted_iota(jnp.int32, (CH, CH), 1)
    return row >= col


def _heads():
    return [slice(hh * HK, (hh + 1) * HK) for hh in range(NH)]


def _hgrn_gate_heads(q_ref, f_ref, rows, lbv):
    out = []
    for sl in _heads():
        qraw = q_ref[rows, sl]
        sq = _sig(qraw)
        sf = _sig(f_ref[rows, sl])
        f = lbv[:, sl] + (1.0 - lbv[:, sl]) * sf
        out.append(dict(qraw=qraw, sq=sq, q=qraw * sq, sf=sf, f=f, k=1.0 - f, lg=jnp.log(f)))
    return out


def _hgrn_decay_heads(gh, b):
    bl = b[CH - 1:CH, :]
    bm = b[CH // 2 - 1:CH // 2, :]
    ebm = jnp.exp(bm)
    eblm = jnp.exp(bl - bm)
    ebl = jnp.exp(bl)
    out = []
    for g, sl in zip(gh, _heads()):
        d = b[:, sl] - bm[:, sl]
        e1 = jnp.exp(d)
        e2 = jnp.exp(-d)
        qs = g["q"] * e1
        ks = g["k"] * e2
        qe = qs * ebm[:, sl]
        kd = ks * eblm[:, sl]
        out.append(dict(e1=e1, e2=e2, qe=qe, kd=kd, ebm=ebm[:, sl], eblm=eblm[:, sl], ebl=ebl[:, sl],
                        qsb=qs.astype(BF16), ksb=ks.astype(BF16), qeb=qe.astype(BF16), kdb=kd.astype(BF16)))
    return out


def _hgrn_fwd_call(proj, lb, gn):
    S = proj.shape[0]
    nc = S // CH
    cps = HRB // CH

    def body(q_ref, f_ref, i_ref, z_ref, lb_ref, gn_ref, or_ref, ob_ref, st_ref, st):
        @pl.when(pl.program_id(0) == 0)
        def _():
            st[...] = jnp.zeros_like(st)

        low = _tri_masks()
        tri = low.astype(BF16)
        lbv = lb_ref[...]
        hs = _heads()
        for ci in range(cps):
            rows = slice(ci * CH, (ci + 1) * CH)
            gh = _hgrn_gate_heads(q_ref, f_ref, rows, lbv)
            b = _exact_mm(tri, jnp.concatenate([g["lg"] for g in gh], axis=1))
            dh = _hgrn_decay_heads(gh, b)
            vbs = [i_ref[rows, sl].astype(BF16) for sl in hs]
            st_ref[ci] = st[...]
            s0s = [st[sl, :] for sl in hs]
            as_ = [_nt(d["qsb"], d["ksb"]) for d in dh]
            ois = [_nt(d["qeb"], s0.astype(BF16)) for d, s0 in zip(dh, s0s)]
            sts = [_tn(vb, d["kdb"]) for vb, d in zip(vbs, dh)]
            abs_ = [jnp.where(low, a, 0.0).astype(BF16) for a in as_]
            os_ = [oi + _nn(a, vb) for oi, a, vb in zip(ois, abs_, vbs)]
            for sl, s0, sn, d, o in zip(hs, s0s, sts, dh, os_):
                st[sl, :] = s0 * d["ebl"] + sn
                or_ref[rows, sl] = o
                r = lax.rsqrt(jnp.mean(o * o, axis=-1, keepdims=True) + EPS)
                z = z_ref[rows, sl]
                ob_ref[rows, sl] = (o * r * gn_ref[:, sl] * (z * _sig(z))).astype(BF16)

    def pcol(c):
        return pl.BlockSpec((HRB, D), lambda i: (i, c))

    vec = pl.BlockSpec((1, D), lambda i: (0, 0))
    row = pl.BlockSpec((HRB, D), lambda i: (i, 0))
    return pl.pallas_call(
        body, name="hgrn_fwd", grid=(S // HRB,),
        in_specs=[pcol(2), pcol(3), pcol(4), pcol(5), vec, vec],
        out_specs=[row, row, pl.BlockSpec((cps, NH * HK, HK), lambda i: (i, 0, 0))],
        out_shape=[jax.ShapeDtypeStruct((S, D), F32), jax.ShapeDtypeStruct((S, D), BF16),
                   jax.ShapeDtypeStruct((nc, NH * HK, HK), F32)],
        scratch_shapes=[pltpu.VMEM((NH * HK, HK), F32)],
        compiler_params=_cp(),
    )(proj, proj, proj, proj, lb, gn)


def _hgrn_bwd_call(proj, oraw, dob, states, lb, gn):
    S = proj.shape[0]
    nblk = S // HRB
    cps = HRB // CH

    def body(q_ref, f_ref, i_ref, z_ref, or_ref, dob_ref, st_ref, lb_ref, gn_ref, dh_ref, acc_ref, dst):
        @pl.when(pl.program_id(0) == 0)
        def _():
            dst[...] = jnp.zeros_like(dst)
            acc_ref[...] = jnp.zeros_like(acc_ref)

        low = _tri_masks()
        tri = low.astype(BF16)
        triu = jnp.logical_not(_tri_masks()) | (lax.broadcasted_iota(jnp.int32, (CH, CH), 0)
                                               == lax.broadcasted_iota(jnp.int32, (CH, CH), 1))
        triu = triu.astype(BF16)
        lbv = lb_ref[...]
        hs = _heads()
        for ci in reversed(range(cps)):
            rows = slice(ci * CH, (ci + 1) * CH)
            dobs, dgns = [], []
            for sl in hs:
                o = or_ref[rows, sl]
                z = z_ref[rows, sl]
                sz = _sig(z)
                gnv = gn_ref[:, sl]
                dobv = dob_ref[rows, sl]
                r = lax.rsqrt(jnp.mean(o * o, axis=-1, keepdims=True) + EPS)
                onr = o * r
                don = dobv * (z * sz)
                dh_ref[rows, 3 * D + sl.start:3 * D + sl.stop] = (
                    dobv * (onr * gnv) * (sz * (1.0 + z * (1.0 - sz)))).astype(BF16)
                dgns.append(jnp.sum(don * onr, axis=0, keepdims=True))
                gh_ = don * gnv
                dobs.append((r * (gh_ - onr * jnp.mean(gh_ * onr, axis=-1, keepdims=True))).astype(BF16))
            acc_ref[1:2, :] += jnp.concatenate(dgns, axis=1)

            gh = _hgrn_gate_heads(q_ref, f_ref, rows, lbv)
            b = _exact_mm(tri, jnp.concatenate([g["lg"] for g in gh], axis=1))
            dh = _hgrn_decay_heads(gh, b)
            vbs = [i_ref[rows, sl].astype(BF16) for sl in hs]

            st0s = [st_ref[ci, sl, :] for sl in hs]
            dst1s = [dst[sl, :] for sl in hs]
            dst1bs = [t.astype(BF16) for t in dst1s]
            as_ = [_nt(d["qsb"], d["ksb"]) for d in dh]
            das_ = [_nt(do, vb) for do, vb in zip(dobs, vbs)]
            dqes = [_nn(do, s0.astype(BF16)) for do, s0 in zip(dobs, st0s)]
            dkds = [_nn(vb, d1) for vb, d1 in zip(vbs, dst1bs)]
            dvis = [_nt(d["kdb"], d1) for d, d1 in zip(dh, dst1bs)]
            dsts = [_tn(do, d["qeb"]) for do, d in zip(dobs, dh)]
            abs_ = [jnp.where(low, a, 0.0).astype(BF16) for a in as_]
            dabs_ = [jnp.where(low, a, 0.0).astype(BF16) for a in das_]
            dqss = [_nn(da, d["ksb"]) for da, d in zip(dabs_, dh)]
            dkss = [_tn(da, d["qsb"]) for da, d in zip(dabs_, dh)]
            dvs_ = [_tn(a, do) + dvi for a, do, dvi in zip(abs_, dobs, dvis)]

            dqs_, dks_, dbs_, exs_ = [], [], [], []
            for hh, sl in enumerate(hs):
                d, d1, s0 = dh[hh], dst1s[hh], st0s[hh]
                dqe, dqs, dks, dkd = dqes[hh], dqss[hh], dkss[hh], dkds[hh]
                dst[sl, :] = dsts[hh] + d1 * d["ebl"]
                dh_ref[rows, 2 * D + sl.start:2 * D + sl.stop] = dvs_[hh].astype(BF16)
                dqs_.append((dqe * d["ebm"] + dqs) * d["e1"])
                dks_.append((dks + dkd * d["eblm"]) * d["e2"])
                dkdkd = dkd * d["kd"]
                dbs_.append(dqe * d["qe"] + dqs * d["qsb"].astype(F32) - dks * d["ksb"].astype(F32) - dkdkd)
                exs_.append(jnp.sum(dkdkd, axis=0, keepdims=True)
                            + jnp.sum(d1 * s0, axis=0, keepdims=True) * d["ebl"])
            dg = _exact_mm(triu, jnp.concatenate(dbs_, axis=1)) + jnp.concatenate(exs_, axis=1)

            dlbs = []
            for hh, sl in enumerate(hs):
                g = gh[hh]
                df = dg[:, sl] / g["f"] - dks_[hh]
                sf = g["sf"]
                omsf = 1.0 - sf
                sq = g["sq"]
                dlbs.append(jnp.sum(df * omsf, axis=0, keepdims=True))
                dh_ref[rows, sl] = (dqs_[hh] * (sq * (1.0 + g["qraw"] * (1.0 - sq)))).astype(BF16)
                dh_ref[rows, D + sl.start:D + sl.stop] = (df * (1.0 - lbv[:, sl]) * sf * omsf).astype(BF16)
            acc_ref[0:1, :] += jnp.concatenate(dlbs, axis=1)

    def pcol(c):
        return pl.BlockSpec((HRB, D), lambda i: (nblk - 1 - i, c))

    vec = pl.BlockSpec((1, D), lambda i: (0, 0))
    row = pl.BlockSpec((HRB, D), lambda i: (nblk - 1 - i, 0))
    return pl.pallas_call(
        body, name="hgrn_bwd", grid=(nblk,),
        in_specs=[pcol(2), pcol(3), pcol(4), pcol(5), row, row,
                  pl.BlockSpec((cps, NH * HK, HK), lambda i: (nblk - 1 - i, 0, 0)), vec, vec],
        out_specs=[pl.BlockSpec((HRB, 4 * D), lambda i: (nblk - 1 - i, 0)),
                   pl.BlockSpec((8, D), lambda i: (0, 0))],
        out_shape=[jax.ShapeDtypeStruct((S, 4 * D), BF16), jax.ShapeDtypeStruct((8, D), F32)],
        scratch_shapes=[pltpu.VMEM((NH * HK, HK), F32)],
        compiler_params=_cp(),
    )(proj, proj, proj, proj, oraw, dob, states, lb, gn)


def _merge_call(oa, ob, proj, x, tgt, vecs, attn, wa, wb, wo, hsum):
    S = x.shape[0]
    tm = 256

    def body(oa_ref, ob_ref, ga_ref, gb_ref, x_ref, t_ref, v_ref, at_ref, za_ref, wa_ref, wb_ref, wo_ref, hs_ref,
             y_ref, dx2_ref, du_ref, dya_ref, dyb_ref, dg_ref, dat_ref, dsum_ref, dza_ref, dob_ref, acc_ref, ls_ref):
        @pl.when(pl.program_id(0) == 0)
        def _():
            acc_ref[...] = jnp.zeros_like(acc_ref)
            ls_ref[...] = jnp.zeros_like(ls_ref)

        gate = v_ref[0:1, :]
        fg = v_ref[1:2, :]
        ya = _nn(oa_ref[...], wa_ref[...])
        yb = _nn(ob_ref[...], wb_ref[...])
        sa = _sig(ga_ref[...])
        sb = _sig(gb_ref[...])
        y = (sa * ya + sb * yb).astype(BF16)
        y_ref[...] = y
        u = _nn(y, wo_ref[...])
        x2v = x_ref[...] + gate * u
        r = lax.rsqrt(jnp.mean(x2v * x2v, axis=-1, keepdims=True) + EPS)
        err = x2v * r * fg - t_ref[...]
        ls_ref[...] += jnp.sum(err * err)
        dout = err * (1.0 / D)
        gh = dout * fg
        dx2 = r * gh - x2v * (r * r * r * jnp.mean(gh * x2v, axis=-1, keepdims=True))
        acc_ref[0:1, :] += jnp.sum(dx2 * u, axis=0, keepdims=True)
        acc_ref[1:2, :] += jnp.sum(dout * x2v * r, axis=0, keepdims=True)
        dx2_ref[...] = dx2
        du = (dx2 * gate).astype(BF16)
        du_ref[...] = du
        dy = _nt(du, wo_ref[...])
        dya = (dy * sa).astype(BF16)
        dyb = (dy * sb).astype(BF16)
        dya_ref[...] = dya
        dyb_ref[...] = dyb
        dg_ref[:, 0:D] = (dy * ya * sa * (1.0 - sa)).astype(BF16)
        dg_ref[:, D:2 * D] = (dy * yb * sb * (1.0 - sb)).astype(BF16)
        doa = _nt(dya, wa_ref[...])
        dob_ref[...] = _nt(dyb, wb_ref[...]).astype(BF16)
        za = za_ref[...]
        sz = _sig(za)
        att = at_ref[...]
        dat = doa * (za * sz)
        dat_ref[...] = dat
        dza_ref[...] = (doa * att * (sz * (1.0 + za * (1.0 - sz)))).astype(BF16)
        dsum_ref[...] = _exact_mm_r(dat * att, hs_ref[...])

    row = pl.BlockSpec((tm, D), lambda i: (i, 0))
    arow = pl.BlockSpec((tm, AW), lambda i: (i, 0))
    full = lambda a: pl.BlockSpec(a.shape, lambda i: (0, 0))
    return pl.pallas_call(
        body, name="merge_fwd_bwd", grid=(S // tm,),
        in_specs=[arow, row, pl.BlockSpec((tm, D), lambda i: (i, 6)), pl.BlockSpec((tm, D), lambda i: (i, 7)),
                  row, row, pl.BlockSpec((8, D), lambda i: (0, 0)), arow, pl.BlockSpec((tm, AW), lambda i: (i, 3)),
                  full(wa), full(wb), full(wo), full(hsum)],
        out_specs=[row, row, row, row, row, pl.BlockSpec((tm, 2 * D), lambda i: (i, 0)),
                   arow, arow, arow, row, pl.BlockSpec((8, D), lambda i: (0, 0)),
                   pl.BlockSpec((8, 128), lambda i: (0, 0))],
        out_shape=[jax.ShapeDtypeStruct((S, D), BF16), jax.ShapeDtypeStruct((S, D), F32),
                   jax.ShapeDtypeStruct((S, D), BF16), jax.ShapeDtypeStruct((S, D), BF16),
                   jax.ShapeDtypeStruct((S, D), BF16), jax.ShapeDtypeStruct((S, 2 * D), BF16),
                   jax.ShapeDtypeStruct((S, AW), F32), jax.ShapeDtypeStruct((S, AW), F32),
                   jax.ShapeDtypeStruct((S, AW), BF16), jax.ShapeDtypeStruct((S, D), BF16),
                   jax.ShapeDtypeStruct((8, D), F32), jax.ShapeDtypeStruct((8, 128), F32)],
        compiler_params=_cp(),
    )(oa, ob, proj, proj, x, tgt, vecs, attn, proj, wa, wb, wo, hsum)


def _atb_call(a, b, name):
    S, K = a.shape
    N = b.shape[1]
    tm = min(1024, S)

    def body(a_ref, b_ref, o_ref, ob_ref):
        @pl.when(pl.program_id(0) == 0)
        def _():
            o_ref[...] = jnp.zeros_like(o_ref)

        o_ref[...] += _tn(a_ref[...], b_ref[...])

        @pl.when(pl.program_id(0) == S // tm - 1)
        def _():
            ob_ref[...] = o_ref[...].astype(BF16)

    ospec = pl.BlockSpec((K, N), lambda i: (0, 0))
    return pl.pallas_call(
        body, name=name, grid=(S // tm,),
        in_specs=[pl.BlockSpec((tm, K), lambda i: (i, 0)), pl.BlockSpec((tm, N), lambda i: (i, 0))],
        out_specs=[ospec, ospec],
        out_shape=[jax.ShapeDtypeStruct((K, N), F32), jax.ShapeDtypeStruct((K, N), BF16)], compiler_params=_cp(),
    )(a, b)


def _dwin_call(h_t, dqkvz, d_hgrn, d_gates):
    S = h_t.shape[1]
    tm = 512
    tn = 2048

    def body(h_ref, q_ref, k_ref, v_ref, z_ref, m_ref, g_ref, o_ref, ob_ref):
        j = pl.program_id(0)

        @pl.when(pl.program_id(1) == 0)
        def _():
            o_ref[...] = jnp.zeros_like(o_ref)

        hv = h_ref[...]

        @pl.when(j == 0)
        def _():
            for cidx, r in enumerate((q_ref, k_ref, v_ref, z_ref)):
                o_ref[0, :, cidx * AW:(cidx + 1) * AW] += _nn(hv, r[...])

        @pl.when(jnp.logical_or(j == 1, j == 2))
        def _():
            o_ref[0] += _nn(hv, m_ref[...])

        @pl.when(j == 3)
        def _():
            o_ref[0] += _nn(hv, g_ref[...])

        @pl.when(pl.program_id(1) == S // tm - 1)
        def _():
            ob_ref[...] = o_ref[...].astype(BF16)

    aspec = pl.BlockSpec((tm, AW), lambda j, i: (jnp.where(j == 0, i, 0), 0))
    ospec = pl.BlockSpec((1, D, tn), lambda j, i: (j, 0, 0))
    return pl.pallas_call(
        body, name="dw_in", grid=(4, S // tm),
        in_specs=[pl.BlockSpec((D, tm), lambda j, i: (0, i)), aspec, aspec, aspec, aspec,
                  pl.BlockSpec((tm, tn), lambda j, i: (jnp.where(jnp.logical_or(j == 1, j == 2), i, 0),
                                                       jnp.where(j == 2, 1, 0))),
                  pl.BlockSpec((tm, tn), lambda j, i: (jnp.where(j == 3, i, 0), 0))],
        out_specs=[ospec, ospec],
        out_shape=[jax.ShapeDtypeStruct((4, D, tn), F32), jax.ShapeDtypeStruct((4, D, tn), BF16)],
        compiler_params=_cp(),
    )(h_t, *dqkvz, d_hgrn, d_gates)


def _dh_call(dqkvz, d_hgrn, d_gates, w_in, x, dx2, vecs):
    S = x.shape[0]
    tm = 512

    def body(q_ref, k_ref, v_ref, z_ref, m_ref, g_ref, w_ref, x_ref, dx2_ref, p_ref, gx_ref, acc_ref):
        @pl.when(pl.program_id(0) == 0)
        def _():
            acc_ref[...] = jnp.zeros_like(acc_ref)

        dhv = _nt(q_ref[...], w_ref[:, 0:AW])
        for cidx, r in enumerate((k_ref, v_ref, z_ref)):
            dhv += _nt(r[...], w_ref[:, (cidx + 1) * AW:(cidx + 2) * AW])
        dhv += _nt(m_ref[...], w_ref[:, 4 * AW:4 * AW + 4 * D])
        dhv += _nt(g_ref[...], w_ref[:, 4 * AW + 4 * D:NPROJ])
        xv = x_ref[...]
        r = lax.rsqrt(jnp.mean(xv * xv, axis=-1, keepdims=True) + EPS)
        xn = xv * r
        acc_ref[0:1, :] += jnp.sum(dhv, axis=0, keepdims=True)
        acc_ref[1:2, :] += jnp.sum(dhv * xn * p_ref[1:2, :], axis=0, keepdims=True)
        acc_ref[2:3, :] += jnp.sum(dhv * xn * p_ref[2:3, :], axis=0, keepdims=True)
        dxn = dhv * p_ref[0:1, :]
        gx_ref[...] = dx2_ref[...] + r * dxn - xv * (r * r * r * jnp.mean(dxn * xv, axis=-1, keepdims=True))

    row = pl.BlockSpec((tm, D), lambda i: (i, 0))
    aspec = pl.BlockSpec((tm, AW), lambda i: (i, 0))
    const = lambda shape: pl.BlockSpec(shape, lambda i: (0, 0))
    return pl.pallas_call(
        body, name="dh_gradx", grid=(S // tm,),
        in_specs=[aspec, aspec, aspec, aspec,
                  pl.BlockSpec((tm, 4 * D), lambda i: (i, 0)), pl.BlockSpec((tm, 2 * D), lambda i: (i, 0)),
                  pl.BlockSpec((D, NPROJ), lambda i: (0, 0), pipeline_mode=pl.Buffered(1)),
                  row, row, const((8, D))],
        out_specs=[row, const((8, D))],
        out_shape=[jax.ShapeDtypeStruct((S, D), F32), jax.ShapeDtypeStruct((8, D), F32)],
        compiler_params=_cp(),
    )(*dqkvz, d_hgrn, d_gates, w_in, x, dx2, vecs)


def _adamw_math(w, g, m, v):
    m = B1 * m + (1.0 - B1) * g
    v = B2 * v + (1.0 - B2) * (g * g)
    m_hat = m / (1.0 - B1 ** STEP)
    v_hat = v / (1.0 - B2 ** STEP)
    delta = -LR * (m_hat / (jnp.sqrt(v_hat) + AEPS) + WD * w)
    return delta, m, v


def _adamw_call(w, g, m, v, name):
    R, C = w.shape
    tr = R if R * C * 4 <= (1 << 20) else max(8, (1 << 20) // (C * 4))
    assert R % tr == 0

    def body(w_ref, g_ref, m_ref, v_ref, go_ref, d_ref, nm_ref, nv_ref):
        g = g_ref[...]
        go_ref[...] = g
        d_ref[...], nm_ref[...], nv_ref[...] = _adamw_math(w_ref[...], g, m_ref[...], v_ref[...])

    blk = pl.BlockSpec((tr, C), lambda i: (i, 0))
    return pl.pallas_call(
        body, name=name, grid=(R // tr,), in_specs=[blk] * 4, out_specs=[blk] * 4,
        out_shape=[jax.ShapeDtypeStruct((R, C), F32)] * 4, compiler_params=_cp(),
    )(w, g, m, v)


def _mod_call(c_all, w_ada_s, b_s):
    def body(c_ref, w_ref, b_ref, o_ref):
        cv = c_ref[...]
        sc = cv * _sig(cv)
        o_ref[...] = jnp.dot(sc, w_ref[...], preferred_element_type=F32,
                             precision=lax.Precision.HIGHEST) + b_ref[...]

    return pl.pallas_call(
        body, name="ada_mod", out_shape=jax.ShapeDtypeStruct((8, w_ada_s.shape[1]), F32),
        compiler_params=_cp(),
    )(c_all, w_ada_s, b_s)


def _ada_update_call(sct, dm, w, m, v):
    R, C = w.shape
    tr = 256

    def body(s_ref, d_ref, w_ref, m_ref, v_ref, g_ref, dl_ref, nm_ref, nv_ref):
        g = s_ref[:, 0:1] * d_ref[0:1, :]
        for b in range(1, 8):
            g = g + s_ref[:, b:b + 1] * d_ref[b:b + 1, :]
        g_ref[...] = g
        dl_ref[...], nm_ref[...], nv_ref[...] = _adamw_math(w_ref[...], g, m_ref[...], v_ref[...])

    blk = pl.BlockSpec((tr, C), lambda i: (i, 0))
    return pl.pallas_call(
        body, name="ada_update", grid=(R // tr,),
        in_specs=[pl.BlockSpec((tr, 8), lambda i: (i, 0)), pl.BlockSpec((8, C), lambda i: (0, 0)), blk, blk, blk],
        out_specs=[blk] * 4, out_shape=[jax.ShapeDtypeStruct((R, C), F32)] * 4, compiler_params=_cp(),
    )(sct, dm, w, m, v)


def _sum8_call(packs):
    def body(p_ref, o_ref):
        acc = p_ref[0]
        for k in range(1, 8):
            acc = acc + p_ref[k]
        o_ref[...] = acc

    return pl.pallas_call(
        body, name="sum_small", out_shape=jax.ShapeDtypeStruct(packs.shape[1:], F32), compiler_params=_cp(),
    )(packs)


def _local_step(x, tgt, shift, scale, gate, norm_g, hgrn_onorm_g, rel_bias, lb, final_g, weights_fn, hook=None):
    a = norm_g * (1.0 + scale)
    z6 = jnp.zeros((6, D), F32)
    h, h_t, proj, w_in, rest_fn = weights_fn(x, jnp.concatenate([a, shift, z6], 0))

    biases = _bias_tiles(rel_bias) * LOG2E
    attn, lse, oa = _attn_fwd_call(proj, biases)

    gn = jnp.tile(hgrn_onorm_g, (1, NH))
    oraw, ob, states = _hgrn_fwd_call(proj, lb, gn)

    wa, wb, wo = rest_fn(ob)
    vecs2 = jnp.concatenate([gate, final_g, z6], 0)
    hsum = jnp.asarray(np.kron(np.eye(NH), np.ones((HE, HE))), BF16)
    y, dx2, du, dya, dyb, d_gates, dattn, dsum, dza, dob, acc2, lsq = _merge_call(
        oa, ob, proj, x, tgt, vecs2, attn, wa, wb, wo, hsum)
    loss = 0.5 * lsq[0, 0] / D
    d_wo, d_wo16 = _atb_call(y, du, "dw_out")
    d_wa, d_wa16 = _atb_call(oa, dya, "dw_branch_a")
    d_wb, d_wb16 = _atb_call(ob, dyb, "dw_branch_b")

    d_hgrn, acch = _hgrn_bwd_call(proj, oraw, dob, states, lb, gn)

    dq, dk, dv, dbs = _attn_bwd_call(proj, dattn, lse, dsum, biases)
    dqkvz = (dq, dk, dv, dza)

    d_win, d_win16 = _dwin_call(h_t, dqkvz, d_hgrn, d_gates)
    tok = hook((d_win, d_wa, d_wb, d_wo), (d_win16, d_wa16, d_wb16, d_wo16)) if hook is not None else 0.0
    one_scale = 1.0 + scale
    grad_x, acc1 = _dh_call(dqkvz, d_hgrn, d_gates, w_in, x, dx2,
                            jnp.concatenate([a + tok, norm_g, one_scale, jnp.zeros((5, D), F32)], 0))

    d_rel = jnp.zeros((NBUCKETS, NH), F32)
    for p, d in enumerate(PATTERNS):
        band, bucket = _band_bucket(d)
        onehot = (bucket[None] == np.arange(NBUCKETS)[:, None, None]) & band[None]
        d_rel = d_rel + jnp.einsum("hqk,bqk->bh", dbs[p], jnp.asarray(onehot, F32),
                                   precision=lax.Precision.HIGHEST)
    d_onorm = jnp.sum(acch[1].reshape(NH, HK), axis=0)

    zrow = jnp.zeros((D,), F32)
    pack = jnp.stack([acc1[0], acc1[1], acc2[0], acc1[2], acc2[1], acch[0],
                      zrow.at[:HK].set(d_onorm), zrow.at[0].set(loss),
                      zrow.at[:NBUCKETS * NH].set(d_rel.reshape(-1))] + [zrow] * 7, 0)
    return grad_x, d_win, d_wa, d_wb, d_wo, pack


def _me():
    return lax.axis_index("x"), lax.axis_index("y"), lax.axis_index("c")


def _peers(x, y):
    return [(1 - x, y), (x, 1 - y), (1 - x, 1 - y)]


def _allgather_small(blk, name):
    m_per, n = blk.shape

    def body(x_ref, out_ref, send_sems, recv_sems, local_sem):
        x, y, c = _me()
        me, sibling = (x, y, c), (x, y, 1 - c)
        chips = _peers(x, y)

        def rows(px, py, pc):
            return out_ref.at[pl.ds((4 * px + 2 * py + pc) * m_per, m_per), :]

        def copy(k, block, to, src=None):
            return pltpu.make_async_remote_copy(
                src_ref=rows(*block) if src is None else src, dst_ref=rows(*block),
                send_sem=send_sems.at[k], recv_sem=recv_sems.at[k], device_id=to, device_id_type=MESH)

        mine = pltpu.make_async_copy(x_ref, rows(*me), local_sem)
        mine.start()
        first = [copy(0, me, sibling, src=x_ref)]
        first += [copy(1 + j, me, (*chip, c), src=x_ref) for j, chip in enumerate(chips)]
        for cp in first:
            cp.start()
        passed = [copy(4 + j, (*chip, c), sibling) for j, chip in enumerate(chips)]
        for j, chip in enumerate(chips):
            copy(1 + j, (*chip, c), me).wait_recv()
            passed[j].start()
        copy(0, sibling, me).wait_recv()
        for j, chip in enumerate(chips):
            copy(4 + j, (*chip, 1 - c), me).wait_recv()
        for cp in first + passed:
            cp.wait_send()
        mine.wait()

    return pl.pallas_call(
        body, name=name, out_shape=jax.ShapeDtypeStruct((8 * m_per, n), blk.dtype),
        in_specs=[pl.BlockSpec(memory_space=pltpu.VMEM)], out_specs=pl.BlockSpec(memory_space=pltpu.VMEM),
        scratch_shapes=[pltpu.SemaphoreType.DMA((7,)), pltpu.SemaphoreType.DMA((7,)), pltpu.SemaphoreType.DMA],
    )(blk)


ANY = pl.BlockSpec(memory_space=pl.ANY)


HBM = pl.BlockSpec(memory_space=pltpu.HBM)
SEM = pl.BlockSpec(memory_space=pltpu.SEMAPHORE)
EFFECT = pltpu.SideEffectType.DATAFLOW_SIDE_EFFECTING


def _w_part(t, ref, j, half):
    if t == 0:
        return ref.at[pl.ds(half * (D // 2), D // 2), pl.ds(j * 2048, 2048)]
    if t == 1:
        return ref.at[pl.ds(half * (AW // 2), AW // 2), pl.ds(j * 256, 256)]
    return ref.at[pl.ds(j * 256 + half * 128, 128), :]


def _w_ici_copies(fulls, send_sems, recv_sems, tensors=(0, 1, 2, 3)):
    x, y, c = _me()
    outs, ins = [], []
    for u, t in enumerate(tensors):
        for k, chip in enumerate(_peers(x, y)):
            mine = _w_part(t, fulls[u], 2 * x + y, c)
            theirs = _w_part(t, fulls[u], 2 * chip[0] + chip[1], c)
            kw = dict(send_sem=send_sems.at[3 * t + k], recv_sem=recv_sems.at[3 * t + k],
                      device_id=(*chip, c), device_id_type=MESH)
            outs.append(pltpu.make_async_remote_copy(src_ref=mine, dst_ref=mine, **kw))
            ins.append(pltpu.make_async_remote_copy(src_ref=theirs, dst_ref=theirs, **kw))
    return outs, ins


def _gather_ici_start(fulls):
    def body(f0, f1, f2, f3, send_sems, recv_sems, t0, t1, t2, t3, token):
        for cp in _w_ici_copies([f0, f1, f2, f3], send_sems, recv_sems)[0]:
            cp.start()
        token[...] = jnp.zeros_like(token)

    res = pl.pallas_call(
        body, name="gather_ici_start",
        out_shape=(pltpu.SemaphoreType.DMA((12,)), pltpu.SemaphoreType.DMA((12,)),
                   *[pltpu.HBM(a.shape, a.dtype) for a in fulls], jax.ShapeDtypeStruct((8, 128), F32)),
        in_specs=[HBM] * 4, out_specs=(SEM, SEM, *[HBM] * 4, pl.BlockSpec(memory_space=pltpu.VMEM)),
        input_output_aliases={i: 2 + i for i in range(4)},
        compiler_params=pltpu.CompilerParams(has_side_effects=EFFECT),
    )(*[pltpu.with_memory_space_constraint(a, pltpu.HBM) for a in fulls])
    return res[0], res[1], list(res[2:6]), res[6]


def _gather_ici_wait(send_sems, recv_sems, bufs, after, tensors, name):
    n = len(tensors)

    def body(*refs):
        outs, ins = _w_ici_copies(refs[0:n], refs[n], refs[n + 1], tensors)
        for cp in outs:
            cp.wait_send()
        for cp in ins:
            cp.wait_recv()

    return pl.pallas_call(
        body, name=name, out_shape=tuple(pltpu.HBM(a.shape, a.dtype) for a in bufs),
        in_specs=[HBM] * n + [SEM, SEM, ANY], out_specs=[HBM] * n,
        input_output_aliases={i: i for i in range(n)},
        compiler_params=pltpu.CompilerParams(has_side_effects=EFFECT),
    )(*bufs, send_sems, recv_sems, after)


def _gather_sibling(bufs, tensors, name):
    n = len(tensors)

    def body(*refs):
        outs, send_sems, recv_sems = refs[n:2 * n], refs[2 * n], refs[2 * n + 1]
        x, y, c = _me()
        cps = []
        for u, t in enumerate(tensors):
            for k, chip in enumerate(_peers(x, y)):
                blk = _w_part(t, outs[u], 2 * chip[0] + chip[1], c)
                cp = pltpu.make_async_remote_copy(
                    src_ref=blk, dst_ref=blk, send_sem=send_sems.at[u, k], recv_sem=recv_sems.at[u, k],
                    device_id=(x, y, 1 - c), device_id_type=MESH)
                cp.start()
                cps.append(cp)
        for u, t in enumerate(tensors):
            for k, chip in enumerate(_peers(x, y)):
                blk = _w_part(t, outs[u], 2 * chip[0] + chip[1], 1 - c)
                pltpu.make_async_remote_copy(
                    src_ref=blk, dst_ref=blk, send_sem=send_sems.at[u, k], recv_sem=recv_sems.at[u, k],
                    device_id=(x, y, 1 - c), device_id_type=MESH).wait_recv()
        for cp in cps:
            cp.wait_send()

    return pl.pallas_call(
        body, name=name,
        out_shape=[jax.ShapeDtypeStruct(FULL_W_SHAPES[t], BF16) for t in tensors],
        in_specs=[ANY] * n, out_specs=[ANY] * n, input_output_aliases={u: u for u in range(n)},
        scratch_shapes=[pltpu.SemaphoreType.DMA((n, 3)), pltpu.SemaphoreType.DMA((n, 3))],
    )(*bufs)


def _half_of(t, ref, half):
    if t == 0:
        return ref.at[:, pl.ds(half * 512, 512), :]
    if t == 1:
        return ref.at[pl.ds(half * 256, 256), :]
    return ref.at[:, pl.ds(half * 512, 512)]


HALF_SHAPES = [(4, 512, 2048), (256, D), (D, 512), (D, 512)]
PIECE_SHAPES = [(512, 2048), (256, 256), (256, 512), (256, 512)]
SHARD_SHAPES = [(D, 2048), (AW, 256), (256, D), (256, D)]


def _chip_piece(t, ref, j):
    if t == 0:
        return ref.at[j]
    if t == 1:
        return ref.at[:, pl.ds(j * 256, 256)]
    return ref.at[pl.ds(j * 256, 256), :]


def _reduce_sibling_send(gs):
    def body(g0, g1, g2, g3, r0, r1, r2, r3, send_sems, recv_sems):
        x, y, c = _me()
        ins, outs = [g0, g1, g2, g3], [r0, r1, r2, r3]
        cps = []
        for t in range(4):
            cp = pltpu.make_async_remote_copy(
                src_ref=_half_of(t, ins[t], 1 - c), dst_ref=outs[t],
                send_sem=send_sems.at[t], recv_sem=recv_sems.at[t], device_id=(x, y, 1 - c), device_id_type=MESH)
            cp.start()
            cps.append(cp)
        for cp in cps:
            cp.wait_recv()
        for cp in cps:
            cp.wait_send()

    return pl.pallas_call(
        body, name="reduce_sibling", out_shape=[jax.ShapeDtypeStruct(s, BF16) for s in HALF_SHAPES],
        in_specs=[ANY] * 4, out_specs=[ANY] * 4,
        scratch_shapes=[pltpu.SemaphoreType.DMA((4,)), pltpu.SemaphoreType.DMA((4,))],
    )(*gs)


def _chip_copies(hs, lands, send_sems, recv_sems):
    x, y, c = _me()
    cps = []
    for t in range(4):
        for k, chip in enumerate(_peers(x, y)):
            pj = 2 * chip[0] + chip[1]
            cps.append(pltpu.make_async_remote_copy(
                src_ref=_chip_piece(t, hs[t], pj), dst_ref=lands[t].at[k],
                send_sem=send_sems.at[3 * t + k], recv_sem=recv_sems.at[3 * t + k],
                device_id=(*chip, c), device_id_type=MESH))
    return cps


def _reduce_chips_start(hs):
    lands = [lax.empty((3,) + s, BF16) for s in PIECE_SHAPES]

    def body(h0, h1, h2, h3, l0, l1, l2, l3, send_sems, recv_sems, t0, t1, t2, t3, t4, t5, t6, t7, token):
        for cp in _chip_copies([h0, h1, h2, h3], [l0, l1, l2, l3], send_sems, recv_sems):
            cp.start()
        token[...] = jnp.zeros_like(token)

    bufs = list(hs) + lands
    res = pl.pallas_call(
        body, name="reduce_chips_start",
        out_shape=(pltpu.SemaphoreType.DMA((12,)), pltpu.SemaphoreType.DMA((12,)),
                   *[pltpu.HBM(a.shape, a.dtype) for a in bufs], jax.ShapeDtypeStruct((8, 128), F32)),
        in_specs=[HBM] * 8, out_specs=(SEM, SEM, *[HBM] * 8, pl.BlockSpec(memory_space=pltpu.VMEM)),
        input_output_aliases={i: 2 + i for i in range(8)},
        compiler_params=pltpu.CompilerParams(has_side_effects=EFFECT),
    )(*[pltpu.with_memory_space_constraint(a, pltpu.HBM) for a in bufs])
    return res[0], res[1], list(res[2:10]), res[10]


def _reduce_chips_wait(send_sems, recv_sems, thru, after):
    def body(h0, h1, h2, h3, l0, l1, l2, l3, send_sems, recv_sems, after_ref, d0, d1, d2, d3, g0, g1, g2, g3):
        cps = _chip_copies([h0, h1, h2, h3], [l0, l1, l2, l3], send_sems, recv_sems)
        for cp in cps:
            cp.wait_send()
        for cp in cps:
            cp.wait_recv()

    res = pl.pallas_call(
        body, name="reduce_chips_wait", out_shape=tuple(pltpu.HBM(a.shape, a.dtype) for a in thru),
        in_specs=[HBM] * 8 + [SEM, SEM, ANY], out_specs=[HBM] * 8,
        input_output_aliases={i: i for i in range(8)},
        compiler_params=pltpu.CompilerParams(has_side_effects=EFFECT),
    )(*thru, send_sems, recv_sems, after)
    return list(res[4:8])


def _share_sibling(shards):
    def body(i0, i1, i2, i3, o0, o1, o2, o3, send_sems, recv_sems):
        x, y, c = _me()
        outs = [o0, o1, o2, o3]

        def half(t, ref, hf):
            if t == 0:
                return ref.at[pl.ds(hf * 512, 512), :]
            if t == 1:
                return ref.at[pl.ds(hf * 256, 256), :]
            return ref.at[:, pl.ds(hf * 512, 512)]

        cps = []
        for t in range(4):
            mine = half(t, outs[t], c)
            cp = pltpu.make_async_remote_copy(
                src_ref=mine, dst_ref=mine, send_sem=send_sems.at[t], recv_sem=recv_sems.at[t],
                device_id=(x, y, 1 - c), device_id_type=MESH)
            cp.start()
            cps.append(cp)
        for t in range(4):
            theirs = half(t, outs[t], 1 - c)
            pltpu.make_async_remote_copy(
                src_ref=theirs, dst_ref=theirs, send_sem=send_sems.at[t],
                recv_sem=recv_sems.at[t], device_id=(x, y, 1 - c), device_id_type=MESH).wait_recv()
        for cp in cps:
            cp.wait_send()

    return pl.pallas_call(
        body, name="share_sibling", out_shape=[jax.ShapeDtypeStruct(s, F32) for s in SHARD_SHAPES],
        in_specs=[ANY] * 4, out_specs=[ANY] * 4, input_output_aliases={0: 0, 1: 1, 2: 2, 3: 3},
        scratch_shapes=[pltpu.SemaphoreType.DMA((4,)), pltpu.SemaphoreType.DMA((4,))],
    )(*shards)


def _half_blockspec(t, idx_pos):
    if t == 0:
        return pl.BlockSpec((1, 512, 2048), lambda i, s: (i, s[idx_pos], 0)), 4
    if t == 1:
        return pl.BlockSpec((256, D), lambda i, s: (s[idx_pos], 0)), 1
    return pl.BlockSpec((256, 512), lambda i, s: (i, s[idx_pos])), 4


def _half_out_blockspec(t):
    if t == 0:
        return pl.BlockSpec((1, 512, 2048), lambda i, s: (i, 0, 0))
    if t == 1:
        return pl.BlockSpec((256, D), lambda i, s: (0, 0))
    return pl.BlockSpec((256, 512), lambda i, s: (i, 0))


def _add_half_call(t, own, recv, sc, name):
    in_blk, steps = _half_blockspec(t, 0)
    out_blk = _half_out_blockspec(t)

    def body(s_ref, a_ref, b_ref, o_ref, ob_ref):
        v = a_ref[...] + b_ref[...].astype(F32)
        o_ref[...] = v
        ob_ref[...] = v.astype(BF16)

    return pl.pallas_call(
        body, name=name,
        grid_spec=pltpu.PrefetchScalarGridSpec(
            num_scalar_prefetch=1, grid=(steps,), in_specs=[in_blk, out_blk], out_specs=[out_blk, out_blk]),
        out_shape=[jax.ShapeDtypeStruct(HALF_SHAPES[t], F32), jax.ShapeDtypeStruct(HALF_SHAPES[t], BF16)],
        compiler_params=_cp(),
    )(sc, own, recv)


def _final_piece_call(t, chipsum, recv3, sc, name):
    ps = PIECE_SHAPES[t]
    if t == 0:
        own_blk = pl.BlockSpec((1,) + ps, lambda i, s: (s[1], 0, 0))
        o_blk = pl.BlockSpec(ps, lambda i, s: (s[0], 0))
    elif t == 1:
        own_blk = pl.BlockSpec(ps, lambda i, s: (0, s[1]))
        o_blk = pl.BlockSpec(ps, lambda i, s: (s[0], 0))
    else:
        own_blk = pl.BlockSpec(ps, lambda i, s: (s[1], 0))
        o_blk = pl.BlockSpec(ps, lambda i, s: (0, s[0]))
    r_blk = pl.BlockSpec((3,) + ps, lambda i, s: (0, 0, 0))

    def body(s_ref, a_ref, r_ref, o_ref):
        a = a_ref[0] if t == 0 else a_ref[...]
        o_ref[...] = ((a + r_ref[0].astype(F32)) + r_ref[1].astype(F32)) + r_ref[2].astype(F32)

    return pl.pallas_call(
        body, name=name,
        grid_spec=pltpu.PrefetchScalarGridSpec(
            num_scalar_prefetch=1, grid=(1,), in_specs=[own_blk, r_blk], out_specs=o_blk),
        out_shape=jax.ShapeDtypeStruct(SHARD_SHAPES[t], F32), compiler_params=_cp(),
    )(sc, chipsum, recv3)


FULL_W_SHAPES = [(D, NPROJ), (AW, D), (D, D), (D, D)]


def _cast_place_call(t, shard, sc, name):
    if t == 0:
        blk, steps = (512, 2048), 2
        in_blk = pl.BlockSpec(blk, lambda i, s: (i, 0))
        o_blk = pl.BlockSpec(blk, lambda i, s: (i, s[1]))
    elif t == 1:
        blk, steps = (AW, 256), 1
        in_blk = pl.BlockSpec(blk, lambda i, s: (0, 0))
        o_blk = pl.BlockSpec(blk, lambda i, s: (0, s[1]))
    else:
        blk, steps = (256, D), 1
        in_blk = pl.BlockSpec(blk, lambda i, s: (0, 0))
        o_blk = pl.BlockSpec(blk, lambda i, s: (s[1], 0))

    def body(s_ref, a_ref, o_ref, own_ref):
        v = a_ref[...].astype(BF16)
        o_ref[...] = v
        own_ref[...] = v

    return pl.pallas_call(
        body, name=name,
        grid_spec=pltpu.PrefetchScalarGridSpec(
            num_scalar_prefetch=1, grid=(steps,), in_specs=[in_blk], out_specs=[o_blk, in_blk]),
        out_shape=[jax.ShapeDtypeStruct(FULL_W_SHAPES[t], BF16), jax.ShapeDtypeStruct(shard.shape, BF16)],
        compiler_params=_cp(),
    )(sc, shard)


def _lower_bound_fn(hgrn_lb):
    return jnp.cumsum(jax.nn.softmax(hgrn_lb.astype(F32), axis=0), axis=0)[0]


def kernel(x, c, w_ada, b_ada, norm_g, w_in, hgrn_onorm_g, w_branch_a, w_branch_b, w_out, rel_bias, hgrn_lb, final_g, loss_target, m_w_ada, m_b_ada, m_norm_g, m_w_in, m_hgrn_onorm_g, m_w_branch_a, m_w_branch_b, m_w_out, m_rel_bias, m_hgrn_lb, m_final_g, v_w_ada, v_b_ada, v_norm_g, v_w_in, v_hgrn_onorm_g, v_w_branch_a, v_w_branch_b, v_w_out, v_rel_bias, v_hgrn_lb, v_final_g):
    ax, ay, ac = _me()
    chip = 2 * ax + ay
    dev = 4 * ax + 2 * ay + ac
    sc_idx = jnp.stack([ac, chip]).astype(jnp.int32)

    c_all = _allgather_small(jnp.pad(c, ((0, 7), (0, 0))), "gather_c").reshape(8, 8, D)[:, 0]
    b_s = lax.dynamic_slice(b_ada, (0, chip * 768), (1, 768))
    mod_part = _mod_call(c_all, w_ada[0], b_s)
    mod_all = _allgather_small(mod_part, "gather_mod").reshape(8, 8, 768)
    mod_mine = lax.dynamic_index_in_dim(mod_all, dev, axis=1, keepdims=False)
    mod = mod_mine[0::2].reshape(1, 3 * D)

    names = ["w_in", "w_a", "w_b", "w_o"]
    shards, mod = lax.optimization_barrier(([w_in[0], w_branch_a[0], w_branch_b[0], w_out[0]], mod))
    placed = [_cast_place_call(t, shards[t], sc_idx, "cast_" + names[t]) for t in range(4)]
    w_send_sems, w_recv_sems, w_thru, w_token = _gather_ici_start([p_[0] for p_ in placed])
    mod = mod + w_token[0, 0]
    rel_bias_t = rel_bias + w_token[0, 0]
    shift, scale, gate = mod[:, :D], mod[:, D:2 * D], mod[:, 2 * D:]

    def weights_fn(xs, avec):
        proj, h, h_t = _proj_own_call(xs, avec, placed[0][1], sc_idx)
        arrived = _gather_ici_wait(w_send_sems, w_recv_sems, w_thru[0:1], proj, (0,), "gather_ici_wait_in")
        (win_f,) = _gather_sibling(arrived, (0,), "gather_sibling_in")

        def rest_fn(after):
            late = _gather_ici_wait(w_send_sems, w_recv_sems, w_thru[1:4], after, (1, 2, 3), "gather_ici_wait_rest")
            return _gather_sibling(late, (1, 2, 3), "gather_sibling_rest")

        return h, h_t, _proj_rest_call(h, win_f, proj, sc_idx), win_f, rest_fn

    flight = {}

    def start_reduction(own, own16):
        sib = _reduce_sibling_send(own16)
        halves = [_add_half_call(t, own[t], sib[t], sc_idx, "chipsum_" + names[t]) for t in range(4)]
        send_sems, recv_sems, thru, token = _reduce_chips_start([hb for _, hb in halves])
        flight.update(sems=(send_sems, recv_sems), thru=thru, sums=[hf for hf, _ in halves])
        return token[0, 0]

    lb, lb_vjp = jax.vjp(_lower_bound_fn, hgrn_lb)
    grad_x, d_win, d_wa, d_wb, d_wo, pack = _local_step(
        x[0], loss_target[0], shift, scale, gate, norm_g, hgrn_onorm_g, rel_bias_t, lb[None, :],
        final_g[None, :], weights_fn, hook=start_reduction)

    rec = _reduce_chips_wait(*flight["sems"], flight["thru"], pack)
    pieces = [_final_piece_call(t, flight["sums"][t], rec[t], sc_idx, "piece_" + names[t]) for t in range(4)]
    g_win, g_wa, g_wb, g_wo = _share_sibling(pieces)

    packs = _allgather_small(pack, "gather_small").reshape(8, 16, D)
    tot = _sum8_call(packs)
    loss = tot[7, 0]
    g_b_ada = tot[0:3].reshape(1, 3 * D)
    g_norm_g = tot[3:4]
    g_final_g = tot[4]
    (g_hgrn_lb,) = lb_vjp(tot[5])
    g_onorm = tot[6:7, :HK]
    g_rel = tot[8, :NBUCKETS * NH].reshape(NBUCKETS, NH)

    def rows_of(a):
        flat = a.reshape(-1)
        n = -(-flat.shape[0] // D)
        return jnp.pad(flat, (0, n * D - flat.shape[0])).reshape(n, D)

    smalls = [(b_ada, g_b_ada, m_b_ada, v_b_ada), (norm_g, g_norm_g, m_norm_g, v_norm_g),
              (hgrn_onorm_g, g_onorm, m_hgrn_onorm_g, v_hgrn_onorm_g), (rel_bias, g_rel, m_rel_bias, v_rel_bias),
              (hgrn_lb, g_hgrn_lb, m_hgrn_lb, v_hgrn_lb), (final_g, g_final_g, m_final_g, v_final_g)]
    cat = [jnp.concatenate([rows_of(s[k]) for s in smalls], 0) for k in range(4)]
    cat = [jnp.pad(a, ((0, 16 - a.shape[0]), (0, 0))) for a in cat]
    _, sd, sm, sv = _adamw_call(*cat, "adamw_small")

    def unpack(packed):
        res, r = [], 0
        for s in smalls:
            n = -(-s[0].size // D)
            res.append(packed[r:r + n].reshape(-1)[:s[0].size].reshape(s[0].shape))
            r += n
        return res

    d_small, m_small, v_small = unpack(sd), unpack(sm), unpack(sv)

    sc_all = c_all * jax.nn.sigmoid(c_all)
    dmod_all = packs[:, 0:3].reshape(8, 3 * D)
    dm_s = lax.dynamic_slice(dmod_all, (0, chip * 768), (8, 768))
    g_w_ada, d_w_ada, nm_w_ada, nv_w_ada = _ada_update_call(sc_all.T, dm_s, w_ada[0], m_w_ada[0], v_w_ada[0])

    big = []
    for w, g, m, v, n in [(w_in, g_win, m_w_in, v_w_in, "w_in"), (w_branch_a, g_wa, m_w_branch_a, v_w_branch_a, "w_a"),
                          (w_branch_b, g_wb, m_w_branch_b, v_w_branch_b, "w_b"), (w_out, g_wo, m_w_out, v_w_out, "w_o")]:
        big.append(_adamw_call(w[0], g, m[0], v[0], "adamw_" + n))

    e = lambda a: a[None]
    grads = [e(g_w_ada), g_b_ada, g_norm_g, e(big[0][0]), g_onorm, e(big[1][0]), e(big[2][0]), e(big[3][0]),
             g_rel, g_hgrn_lb, g_final_g]
    deltas = [e(d_w_ada), d_small[0], d_small[1], e(big[0][1]), d_small[2], e(big[1][1]), e(big[2][1]), e(big[3][1]),
              d_small[3], d_small[4], d_small[5]]
    new_m = [e(nm_w_ada), m_small[0], m_small[1], e(big[0][2]), m_small[2], e(big[1][2]), e(big[2][2]), e(big[3][2]),
             m_small[3], m_small[4], m_small[5]]
    new_v = [e(nv_w_ada), v_small[0], v_small[1], e(big[0][3]), v_small[2], e(big[1][3]), e(big[2][3]), e(big[3][3]),
             v_small[3], v_small[4], v_small[5]]
    return (loss, grad_x[None], *grads, *deltas, *new_m, *new_v)
```

```python
import math

import numpy as np
import jax
import jax.numpy as jnp
from jax import lax
from jax.experimental import pallas as pl
from jax.experimental.pallas import tpu as pltpu

D = 1024
AW = 512
NH = 8
HE = 64
HK = 128
NPROJ = 8192
ABLK = 128
PATTERNS = (1, 4, 16)
NBUCKETS = 32
MAXDIST = 2048
NEG = -1e30
EPS = 1e-6
CH = 64
LR, B1, B2, AEPS, WD, STEP = 0.001, 0.9, 0.999, 1e-08, 0.01, 10

F32 = jnp.float32
BF16 = jnp.bfloat16
MESH = pl.DeviceIdType.MESH
VMEM_LIMIT = 56 * 1024 * 1024


def _cp(**kw):
    return pltpu.CompilerParams(vmem_limit_bytes=VMEM_LIMIT, **kw)


def _sig(x):
    return 0.5 * jnp.tanh(0.5 * x) + 0.5


def _nt(a, b):
    return lax.dot_general(a, b, (((1,), (1,)), ((), ())), preferred_element_type=F32)


def _tn(a, b):
    return lax.dot_general(a, b, (((0,), (0,)), ((), ())), preferred_element_type=F32)


def _nn(a, b):
    return jnp.dot(a, b, preferred_element_type=F32)


def _split2(x):
    h = x.astype(BF16)
    return h, (x - h.astype(F32)).astype(BF16)


def _exact_mm(tri_bf16, x):
    h, l = _split2(x)
    return _nn(tri_bf16, h) + _nn(tri_bf16, l)


def _exact_mm_r(x, ones_bf16):
    h, l = _split2(x)
    return _nn(h, ones_bf16) + _nn(l, ones_bf16)


def _proj_own_call(x, avec, w_own, sc):
    S = x.shape[0]
    tm, tn = 512, 2048

    def body(s_ref, x_ref, a_ref, w_ref, o_ref, h_ref, ht_ref):
        xv = x_ref[...]
        r = lax.rsqrt(jnp.mean(xv * xv, axis=-1, keepdims=True) + EPS)
        hv = xv * r * a_ref[0:1, :] + a_ref[1:2, :]
        hb = hv.astype(BF16)
        h_ref[...] = hb
        ht_ref[...] = hv.T.astype(BF16)
        o_ref[...] = _nn(hb, w_ref[...])

    return pl.pallas_call(
        body, name="in_proj_own",
        grid_spec=pltpu.PrefetchScalarGridSpec(
            num_scalar_prefetch=1, grid=(S // tm,),
            in_specs=[pl.BlockSpec((tm, D), lambda i, s: (i, 0)), pl.BlockSpec((8, D), lambda i, s: (0, 0)),
                      pl.BlockSpec((D, tn), lambda i, s: (0, 0))],
            out_specs=[pl.BlockSpec((tm, tn), lambda i, s: (i, s[1])), pl.BlockSpec((tm, D), lambda i, s: (i, 0)),
                       pl.BlockSpec((D, tm), lambda i, s: (0, i))]),
        out_shape=[jax.ShapeDtypeStruct((S, NPROJ), F32), jax.ShapeDtypeStruct((S, D), BF16),
                   jax.ShapeDtypeStruct((D, S), BF16)],
        compiler_params=_cp(),
    )(sc, x, avec, w_own)


def _proj_rest_call(h, w_in, proj, sc):
    S = h.shape[0]
    tm, tn = 512, 2048

    def body(s_ref, h_ref, w_ref, p_ref, o_ref):
        o_ref[...] = _nn(h_ref[...], w_ref[...])

    col = lambda j, s: (s[1] + 1 + j) % 4
    return pl.pallas_call(
        body, name="in_proj_rest",
        grid_spec=pltpu.PrefetchScalarGridSpec(
            num_scalar_prefetch=1, grid=(3, S // tm),
            in_specs=[pl.BlockSpec((tm, D), lambda j, i, s: (i, 0)),
                      pl.BlockSpec((D, tn), lambda j, i, s: (0, col(j, s))),
                      pl.BlockSpec(memory_space=pl.ANY)],
            out_specs=pl.BlockSpec((tm, tn), lambda j, i, s: (i, col(j, s)))),
        out_shape=jax.ShapeDtypeStruct((S, NPROJ), F32), input_output_aliases={3: 0}, compiler_params=_cp(),
    )(sc, h, w_in, proj)


def _t5_bucket_np(dist):
    max_exact = NBUCKETS // 2
    n = dist.astype(np.float32)
    large = max_exact + (np.log(np.maximum(n, np.float32(1.0)) / np.float32(max_exact))
                         / np.float32(math.log(MAXDIST / max_exact))
                         * np.float32(NBUCKETS - max_exact)).astype(np.int32)
    large = np.minimum(large, NBUCKETS - 1)
    return np.where(dist < max_exact, dist, large)


def _band_bucket(d):
    qi = np.arange(ABLK)[:, None]
    kj = np.arange(2 * ABLK)[None, :]
    delta = qi + ABLK - kj
    band = (delta >= 0) & (delta <= ABLK)
    bucket = _t5_bucket_np(np.clip(delta, 0, None) * d)
    return band, bucket


def _bias_tiles(rel_bias):
    tiles = []
    for d in PATTERNS:
        band, bucket = _band_bucket(d)
        onehot = (jnp.asarray(bucket, jnp.int32)[None] == jnp.arange(NBUCKETS, dtype=jnp.int32)[:, None, None])
        bias = jnp.einsum("bqk,bh->hqk", onehot.astype(F32), rel_bias, precision=lax.Precision.HIGHEST)
        tiles.append(jnp.where(jnp.asarray(band)[None], bias, NEG))
    return jnp.stack(tiles, 0)


ATT = 2048
HP = 2 * HE
LOG2E = 1.4426950408889634
LN2 = 0.6931471805599453
QSCALE2 = (HE ** -0.5) * LOG2E
AGRP = 16
AGRP_B = 8


def _attn_blocks():
    out = []
    for p, d in enumerate(PATTERNS):
        for r in range(d):
            for n in range(ATT // (d * ABLK)):
                out.append((p, d, r, n))
    return out


def _attn_fwd_call(proj, biases):
    S = proj.shape[0]
    nt = S // ATT

    def body(q_ref, k_ref, v_ref, z_ref, b_ref, a_ref, l_ref, oa_ref, kc, vc, op, lp):
        i = pl.program_id(1)

        @pl.when(i == 0)
        def _():
            kc[0:ATT] = jnp.zeros((ATT, HP), F32)
            vc[0:ATT] = jnp.zeros((ATT, HP), F32)

        @pl.when(i > 0)
        def _():
            kc[0:ATT] = kc[ATT:2 * ATT]
            vc[0:ATT] = vc[ATT:2 * ATT]

        kc[ATT:2 * ATT] = k_ref[...]
        vc[ATT:2 * ATT] = v_ref[...]
        col = lax.broadcasted_iota(jnp.int32, (ABLK, 2 * ABLK), 1)
        dead = jnp.logical_and(i == 0, col < ABLK)
        blocks = _attn_blocks()
        hs = (slice(0, HE), slice(HE, 2 * HE))
        for g0 in range(0, len(blocks), AGRP):
            grp = blocks[g0:g0 + AGRP]
            qrows = [pl.ds(n * ABLK * d + r, ABLK, stride=d) for p, d, r, n in grp]
            krows = [pl.ds(ATT + (n - 1) * ABLK * d + r, 2 * ABLK, stride=d) for p, d, r, n in grp]
            qs = [(q_ref[qr, :] * QSCALE2).astype(BF16) for qr in qrows]
            ks = [kc[kr, :].astype(BF16) for kr in krows]
            vs = [vc[kr, :].astype(BF16) for kr in krows]
            ss = [[_nt(qs[b][:, sl], ks[b][:, sl]) + b_ref[grp[b][0], e] for e, sl in enumerate(hs)]
                  for b in range(len(grp))]
            ss = [[jnp.where(dead, NEG, s) if grp[b][3] == 0 else s for s in ss[b]] for b in range(len(grp))]
            mxs = [[jnp.max(s, axis=-1, keepdims=True) for s in sb] for sb in ss]
            pes = [[jnp.exp2(s - mx) for s, mx in zip(sb, mb)] for sb, mb in zip(ss, mxs)]
            dens = [[jnp.sum(pe, axis=-1, keepdims=True) for pe in pb] for pb in pes]
            pvs = [[_nn(pe.astype(BF16), vs[b][:, sl]) for pe, sl in zip(pes[b], hs)] for b in range(len(grp))]
            for b in range(len(grp)):
                p, d, r, n = grp[b]
                prow = pl.ds(p * ATT + n * ABLK * d + r, ABLK, stride=d)
                lp[prow, :] = jnp.concatenate(
                    [jnp.broadcast_to(mx + jnp.log2(dn), (ABLK, HE)) for mx, dn in zip(mxs[b], dens[b])], axis=1)
                op[prow, :] = jnp.concatenate([pv / dn for pv, dn in zip(pvs[b], dens[b])], axis=1)
        rt = 256
        for t in range(ATT // rt):
            rows = slice(t * rt, (t + 1) * rt)
            pr = [slice(p * ATT + t * rt, p * ATT + (t + 1) * rt) for p in range(3)]
            la, lb_, lc = lp[pr[0], :], lp[pr[1], :], lp[pr[2], :]
            m = jnp.maximum(jnp.maximum(la, lb_), lc)
            ea, eb, ec = jnp.exp2(la - m), jnp.exp2(lb_ - m), jnp.exp2(lc - m)
            den = ea + eb + ec
            att = (ea * op[pr[0], :] + eb * op[pr[1], :] + ec * op[pr[2], :]) / den
            a_ref[rows, :] = att
            l_ref[rows, :] = m + jnp.log2(den)
            z = z_ref[rows, :]
            oa_ref[rows, :] = (att * (z * _sig(z))).astype(BF16)

    def pcol(c):
        return pl.BlockSpec((ATT, HP), lambda h, i: (i, c * 4 + h))

    out = pl.BlockSpec((ATT, HP), lambda h, i: (i, h))
    return pl.pallas_call(
        body, name="attn_fwd", grid=(4, nt),
        in_specs=[pcol(0), pcol(1), pcol(2), pcol(3),
                  pl.BlockSpec((3, 2, ABLK, 2 * ABLK), lambda h, i: (0, h, 0, 0))],
        out_specs=[out, out, out],
        out_shape=[jax.ShapeDtypeStruct((S, AW), F32), jax.ShapeDtypeStruct((S, AW), F32),
                   jax.ShapeDtypeStruct((S, AW), BF16)],
        scratch_shapes=[pltpu.VMEM((2 * ATT, HP), F32), pltpu.VMEM((2 * ATT, HP), F32),
                        pltpu.VMEM((3 * ATT, HP), F32), pltpu.VMEM((3 * ATT, HP), F32)],
        compiler_params=_cp(),
    )(proj, proj, proj, proj, biases)


def _attn_bwd_call(proj, dattn, lse, dsum, biases):
    S = proj.shape[0]
    nt = S // ATT

    def body(q_ref, k_ref, v_ref, do_ref, l_ref, ds_ref, b_ref, dq_ref, dk_ref, dv_ref, db_ref,
             kc, vc, dkc, dvc, dqa):
        i = pl.program_id(1)

        @pl.when(i == 0)
        def _():
            kc[ATT:2 * ATT] = jnp.zeros((ATT, HP), F32)
            vc[ATT:2 * ATT] = jnp.zeros((ATT, HP), F32)
            dkc[ATT:2 * ATT] = jnp.zeros((ATT, HP), F32)
            dvc[ATT:2 * ATT] = jnp.zeros((ATT, HP), F32)
            db_ref[...] = jnp.zeros_like(db_ref)

        @pl.when(i < nt)
        def _():
            kc[0:ATT] = kc[ATT:2 * ATT]
            vc[0:ATT] = vc[ATT:2 * ATT]
            dkc[0:ATT] = dkc[ATT:2 * ATT]
            dvc[0:ATT] = dvc[ATT:2 * ATT]
            kc[ATT:2 * ATT] = k_ref[...]
            vc[ATT:2 * ATT] = v_ref[...]
            dkc[ATT:2 * ATT] = jnp.zeros((ATT, HP), F32)
            dvc[ATT:2 * ATT] = jnp.zeros((ATT, HP), F32)
            col = lax.broadcasted_iota(jnp.int32, (ABLK, 2 * ABLK), 1)
            dead = jnp.logical_and(i == 0, col < ABLK)
            blocks = _attn_blocks()
            hs = (slice(0, HE), slice(HE, 2 * HE))
            for g0 in range(0, len(blocks), AGRP_B):
                grp = blocks[g0:g0 + AGRP_B]
                nb_ = range(len(grp))
                qrows = [pl.ds(n * ABLK * d + r, ABLK, stride=d) for p, d, r, n in grp]
                krows = [pl.ds(ATT + (n - 1) * ABLK * d + r, 2 * ABLK, stride=d) for p, d, r, n in grp]
                qs = [(q_ref[qr, :] * QSCALE2).astype(BF16) for qr in qrows]
                ks = [kc[kr, :].astype(BF16) for kr in krows]
                vs = [vc[kr, :].astype(BF16) for kr in krows]
                dos = [do_ref[qr, :].astype(BF16) for qr in qrows]
                lvs = [l_ref[qr, :] for qr in qrows]
                dsvs = [ds_ref[qr, :] for qr in qrows]
                ss = [[_nt(qs[b][:, sl], ks[b][:, sl]) + b_ref[grp[b][0], e] for e, sl in enumerate(hs)] for b in nb_]
                ss = [[jnp.where(dead, NEG, s) if grp[b][3] == 0 else s for s in ss[b]] for b in nb_]
                dps = [[_nt(dos[b][:, sl], vs[b][:, sl]) for sl in hs] for b in nb_]
                pes = [[jnp.exp2(ss[b][e] - lvs[b][:, e * HE:e * HE + 1]) for e in range(2)] for b in nb_]
                dscs = [[pes[b][e] * (dps[b][e] - dsvs[b][:, e * HE:e * HE + 1]) for e in range(2)] for b in nb_]
                for e in range(2):
                    tot = dscs[0][e]
                    for b in range(1, len(grp)):
                        tot = tot + dscs[b][e]
                    db_ref[grp[0][0], e] += tot
                dsbs = [[t.astype(BF16) for t in tb] for tb in dscs]
                dqs = [[_nn(dsbs[b][e], ks[b][:, sl]) * (HE ** -0.5) for e, sl in enumerate(hs)] for b in nb_]
                dks = [[_tn(dsbs[b][e], qs[b][:, sl]) for e, sl in enumerate(hs)] for b in nb_]
                dvs = [[_tn(pes[b][e].astype(BF16), dos[b][:, sl]) for e, sl in enumerate(hs)] for b in nb_]
                for b in nb_:
                    dq = jnp.concatenate(dqs[b], axis=1)
                    if grp[b][0] == 0:
                        dqa[qrows[b], :] = dq
                    else:
                        dqa[qrows[b], :] += dq
                    dkc[krows[b], :] += jnp.concatenate(dks[b], axis=1)
                    dvc[krows[b], :] += jnp.concatenate(dvs[b], axis=1)
            dq_ref[...] = dqa[...].astype(BF16)
            dk_ref[...] = (dkc[0:ATT] * LN2).astype(BF16)
            dv_ref[...] = dvc[0:ATT].astype(BF16)

        @pl.when(i == nt)
        def _():
            dk_ref[...] = (dkc[ATT:2 * ATT] * LN2).astype(BF16)
            dv_ref[...] = dvc[ATT:2 * ATT].astype(BF16)

    def pcol(c):
        return pl.BlockSpec((ATT, HP), lambda h, i: (jnp.minimum(i, nt - 1), c * 4 + h))

    qrow = pl.BlockSpec((ATT, HP), lambda h, i: (jnp.minimum(i, nt - 1), h))
    krow = pl.BlockSpec((ATT, HP), lambda h, i: (jnp.maximum(i - 1, 0), h))
    bspec = pl.BlockSpec((3, 2, ABLK, 2 * ABLK), lambda h, i: (0, h, 0, 0))
    return pl.pallas_call(
        body, name="attn_bwd", grid=(4, nt + 1),
        in_specs=[pcol(0), pcol(1), pcol(2), qrow, qrow, qrow, bspec],
        out_specs=[qrow, krow, krow, bspec],
        out_shape=[jax.ShapeDtypeStruct((S, AW), BF16)] * 3
                  + [jax.ShapeDtypeStruct((3, NH, ABLK, 2 * ABLK), F32)],
        scratch_shapes=[pltpu.VMEM((2 * ATT, HP), F32)] * 4 + [pltpu.VMEM((ATT, HP), F32)],
        compiler_params=_cp(),
    )(proj, proj, proj, dattn, lse, dsum, biases)


HRB = 512


def _tri_masks():
    row = lax.broadcasted_iota(jnp.int32, (CH, CH), 0)
    col = lax.broadcasted_iota(jnp.int32, (CH, CH), 1)
    return row >= col


def _heads():
    return [slice(hh * HK, (hh + 1) * HK) for hh in range(NH)]


def _hgrn_gate_heads(q_ref, f_ref, rows, lbv):
    out = []
    for sl in _heads():
        qraw = q_ref[rows, sl]
        sq = _sig(qraw)
        sf = _sig(f_ref[rows, sl])
        f = lbv[:, sl] + (1.0 - lbv[:, sl]) * sf
        out.append(dict(qraw=qraw, sq=sq, q=qraw * sq, sf=sf, f=f, k=1.0 - f, lg=jnp.log(f)))
    return out


def _hgrn_decay_heads(gh, b):
    bl = b[CH - 1:CH, :]
    bm = b[CH // 2 - 1:CH // 2, :]
    ebm = jnp.exp(bm)
    eblm = jnp.exp(bl - bm)
    ebl = jnp.exp(bl)
    out = []
    for g, sl in zip(gh, _heads()):
        d = b[:, sl] - bm[:, sl]
        e1 = jnp.exp(d)
        e2 = jnp.exp(-d)
        qs = g["q"] * e1
        ks = g["k"] * e2
        qe = qs * ebm[:, sl]
        kd = ks * eblm[:, sl]
        out.append(dict(e1=e1, e2=e2, qe=qe, kd=kd, ebm=ebm[:, sl], eblm=eblm[:, sl], ebl=ebl[:, sl],
                        qsb=qs.astype(BF16), ksb=ks.astype(BF16), qeb=qe.astype(BF16), kdb=kd.astype(BF16)))
    return out


def _hgrn_fwd_call(proj, lb, gn):
    S = proj.shape[0]
    nc = S // CH
    cps = HRB // CH

    def body(q_ref, f_ref, i_ref, z_ref, lb_ref, gn_ref, or_ref, ob_ref, st_ref, st):
        @pl.when(pl.program_id(0) == 0)
        def _():
            st[...] = jnp.zeros_like(st)

        low = _tri_masks()
        tri = low.astype(BF16)
        lbv = lb_ref[...]
        hs = _heads()
        for ci in range(cps):
            rows = slice(ci * CH, (ci + 1) * CH)
            gh = _hgrn_gate_heads(q_ref, f_ref, rows, lbv)
            b = _exact_mm(tri, jnp.concatenate([g["lg"] for g in gh], axis=1))
            dh = _hgrn_decay_heads(gh, b)
            vbs = [i_ref[rows, sl].astype(BF16) for sl in hs]
            st_ref[ci] = st[...]
            s0s = [st[sl, :] for sl in hs]
            as_ = [_nt(d["qsb"], d["ksb"]) for d in dh]
            ois = [_nt(d["qeb"], s0.astype(BF16)) for d, s0 in zip(dh, s0s)]
            sts = [_tn(vb, d["kdb"]) for vb, d in zip(vbs, dh)]
            abs_ = [jnp.where(low, a, 0.0).astype(BF16) for a in as_]
            os_ = [oi + _nn(a, vb) for oi, a, vb in zip(ois, abs_, vbs)]
            for sl, s0, sn, d, o in zip(hs, s0s, sts, dh, os_):
                st[sl, :] = s0 * d["ebl"] + sn
                or_ref[rows, sl] = o
                r = lax.rsqrt(jnp.mean(o * o, axis=-1, keepdims=True) + EPS)
                z = z_ref[rows, sl]
                ob_ref[rows, sl] = (o * r * gn_ref[:, sl] * (z * _sig(z))).astype(BF16)

    def pcol(c):
        return pl.BlockSpec((HRB, D), lambda i: (i, c))

    vec = pl.BlockSpec((1, D), lambda i: (0, 0))
    row = pl.BlockSpec((HRB, D), lambda i: (i, 0))
    return pl.pallas_call(
        body, name="hgrn_fwd", grid=(S // HRB,),
        in_specs=[pcol(2), pcol(3), pcol(4), pcol(5), vec, vec],
        out_specs=[row, row, pl.BlockSpec((cps, NH * HK, HK), lambda i: (i, 0, 0))],
        out_shape=[jax.ShapeDtypeStruct((S, D), F32), jax.ShapeDtypeStruct((S, D), BF16),
                   jax.ShapeDtypeStruct((nc, NH * HK, HK), F32)],
        scratch_shapes=[pltpu.VMEM((NH * HK, HK), F32)],
        compiler_params=_cp(),
    )(proj, proj, proj, proj, lb, gn)


def _hgrn_bwd_call(proj, oraw, dob, states, lb, gn):
    S = proj.shape[0]
    nblk = S // HRB
    cps = HRB // CH

    def body(q_ref, f_ref, i_ref, z_ref, or_ref, dob_ref, st_ref, lb_ref, gn_ref, dh_ref, acc_ref, dst):
        @pl.when(pl.program_id(0) == 0)
        def _():
            dst[...] = jnp.zeros_like(dst)
            acc_ref[...] = jnp.zeros_like(acc_ref)

        low = _tri_masks()
        tri = low.astype(BF16)
        triu = jnp.logical_not(_tri_masks()) | (lax.broadcasted_iota(jnp.int32, (CH, CH), 0)
                                               == lax.broadcasted_iota(jnp.int32, (CH, CH), 1))
        triu = triu.astype(BF16)
        lbv = lb_ref[...]
        hs = _heads()
        for ci in reversed(range(cps)):
            rows = slice(ci * CH, (ci + 1) * CH)
            dobs, dgns = [], []
            for sl in hs:
                o = or_ref[rows, sl]
                z = z_ref[rows, sl]
                sz = _sig(z)
                gnv = gn_ref[:, sl]
                dobv = dob_ref[rows, sl]
                r = lax.rsqrt(jnp.mean(o * o, axis=-1, keepdims=True) + EPS)
                onr = o * r
                don = dobv * (z * sz)
                dh_ref[rows, 3 * D + sl.start:3 * D + sl.stop] = (
                    dobv * (onr * gnv) * (sz * (1.0 + z * (1.0 - sz)))).astype(BF16)
                dgns.append(jnp.sum(don * onr, axis=0, keepdims=True))
                gh_ = don * gnv
                dobs.append((r * (gh_ - onr * jnp.mean(gh_ * onr, axis=-1, keepdims=True))).astype(BF16))
            acc_ref[1:2, :] += jnp.concatenate(dgns, axis=1)

            gh = _hgrn_gate_heads(q_ref, f_ref, rows, lbv)
            b = _exact_mm(tri, jnp.concatenate([g["lg"] for g in gh], axis=1))
            dh = _hgrn_decay_heads(gh, b)
            vbs = [i_ref[rows, sl].astype(BF16) for sl in hs]

            st0s = [st_ref[ci, sl, :] for sl in hs]
            dst1s = [dst[sl, :] for sl in hs]
            dst1bs = [t.astype(BF16) for t in dst1s]
            as_ = [_nt(d["qsb"], d["ksb"]) for d in dh]
            das_ = [_nt(do, vb) for do, vb in zip(dobs, vbs)]
            dqes = [_nn(do, s0.astype(BF16)) for do, s0 in zip(dobs, st0s)]
            dkds = [_nn(vb, d1) for vb, d1 in zip(vbs, dst1bs)]
            dvis = [_nt(d["kdb"], d1) for d, d1 in zip(dh, dst1bs)]
            dsts = [_tn(do, d["qeb"]) for do, d in zip(dobs, dh)]
            abs_ = [jnp.where(low, a, 0.0).astype(BF16) for a in as_]
            dabs_ = [jnp.where(low, a, 0.0).astype(BF16) for a in das_]
            dqss = [_nn(da, d["ksb"]) for da, d in zip(dabs_, dh)]
            dkss = [_tn(da, d["qsb"]) for da, d in zip(dabs_, dh)]
            dvs_ = [_tn(a, do) + dvi for a, do, dvi in zip(abs_, dobs, dvis)]

            dqs_, dks_, dbs_, exs_ = [], [], [], []
            for hh, sl in enumerate(hs):
                d, d1, s0 = dh[hh], dst1s[hh], st0s[hh]
                dqe, dqs, dks, dkd = dqes[hh], dqss[hh], dkss[hh], dkds[hh]
                dst[sl, :] = dsts[hh] + d1 * d["ebl"]
                dh_ref[rows, 2 * D + sl.start:2 * D + sl.stop] = dvs_[hh].astype(BF16)
                dqs_.append((dqe * d["ebm"] + dqs) * d["e1"])
                dks_.append((dks + dkd * d["eblm"]) * d["e2"])
                dkdkd = dkd * d["kd"]
                dbs_.append(dqe * d["qe"] + dqs * d["qsb"].astype(F32) - dks * d["ksb"].astype(F32) - dkdkd)
                exs_.append(jnp.sum(dkdkd, axis=0, keepdims=True)
                            + jnp.sum(d1 * s0, axis=0, keepdims=True) * d["ebl"])
            dg = _exact_mm(triu, jnp.concatenate(dbs_, axis=1)) + jnp.concatenate(exs_, axis=1)

            dlbs = []
            for hh, sl in enumerate(hs):
                g = gh[hh]
                df = dg[:, sl] / g["f"] - dks_[hh]
                sf = g["sf"]
                omsf = 1.0 - sf
                sq = g["sq"]
                dlbs.append(jnp.sum(df * omsf, axis=0, keepdims=True))
                dh_ref[rows, sl] = (dqs_[hh] * (sq * (1.0 + g["qraw"] * (1.0 - sq)))).astype(BF16)
                dh_ref[rows, D + sl.start:D + sl.stop] = (df * (1.0 - lbv[:, sl]) * sf * omsf).astype(BF16)
            acc_ref[0:1, :] += jnp.concatenate(dlbs, axis=1)

    def pcol(c):
        return pl.BlockSpec((HRB, D), lambda i: (nblk - 1 - i, c))

    vec = pl.BlockSpec((1, D), lambda i: (0, 0))
    row = pl.BlockSpec((HRB, D), lambda i: (nblk - 1 - i, 0))
    return pl.pallas_call(
        body, name="hgrn_bwd", grid=(nblk,),
        in_specs=[pcol(2), pcol(3), pcol(4), pcol(5), row, row,
                  pl.BlockSpec((cps, NH * HK, HK), lambda i: (nblk - 1 - i, 0, 0)), vec, vec],
        out_specs=[pl.BlockSpec((HRB, 4 * D), lambda i: (nblk - 1 - i, 0)),
                   pl.BlockSpec((8, D), lambda i: (0, 0))],
        out_shape=[jax.ShapeDtypeStruct((S, 4 * D), BF16), jax.ShapeDtypeStruct((8, D), F32)],
        scratch_shapes=[pltpu.VMEM((NH * HK, HK), F32)],
        compiler_params=_cp(),
    )(proj, proj, proj, proj, oraw, dob, states, lb, gn)


def _merge_call(oa, ob, proj, x, tgt, vecs, attn, wa, wb, wo, hsum):
    S = x.shape[0]
    tm = 256

    def body(oa_ref, ob_ref, ga_ref, gb_ref, x_ref, t_ref, v_ref, at_ref, za_ref, wa_ref, wb_ref, wo_ref, hs_ref,
             y_ref, dx2_ref, du_ref, dya_ref, dyb_ref, dg_ref, dat_ref, dsum_ref, dza_ref, dob_ref, acc_ref, ls_ref):
        @pl.when(pl.program_id(0) == 0)
        def _():
            acc_ref[...] = jnp.zeros_like(acc_ref)
            ls_ref[...] = jnp.zeros_like(ls_ref)

        gate = v_ref[0:1, :]
        fg = v_ref[1:2, :]
        ya = _nn(oa_ref[...], wa_ref[...])
        yb = _nn(ob_ref[...], wb_ref[...])
        sa = _sig(ga_ref[...])
        sb = _sig(gb_ref[...])
        y = (sa * ya + sb * yb).astype(BF16)
        y_ref[...] = y
        u = _nn(y, wo_ref[...])
        x2v = x_ref[...] + gate * u
        r = lax.rsqrt(jnp.mean(x2v * x2v, axis=-1, keepdims=True) + EPS)
        err = x2v * r * fg - t_ref[...]
        ls_ref[...] += jnp.sum(err * err)
        dout = err * (1.0 / D)
        gh = dout * fg
        dx2 = r * gh - x2v * (r * r * r * jnp.mean(gh * x2v, axis=-1, keepdims=True))
        acc_ref[0:1, :] += jnp.sum(dx2 * u, axis=0, keepdims=True)
        acc_ref[1:2, :] += jnp.sum(dout * x2v * r, axis=0, keepdims=True)
        dx2_ref[...] = dx2
        du = (dx2 * gate).astype(BF16)
        du_ref[...] = du
        dy = _nt(du, wo_ref[...])
        dya = (dy * sa).astype(BF16)
        dyb = (dy * sb).astype(BF16)
        dya_ref[...] = dya
        dyb_ref[...] = dyb
        dg_ref[:, 0:D] = (dy * ya * sa * (1.0 - sa)).astype(BF16)
        dg_ref[:, D:2 * D] = (dy * yb * sb * (1.0 - sb)).astype(BF16)
        doa = _nt(dya, wa_ref[...])
        dob_ref[...] = _nt(dyb, wb_ref[...])
        za = za_ref[...]
        sz = _sig(za)
        att = at_ref[...]
        dat = doa * (za * sz)
        dat_ref[...] = dat
        dza_ref[...] = (doa * att * (sz * (1.0 + za * (1.0 - sz)))).astype(BF16)
        dsum_ref[...] = _exact_mm_r(dat * att, hs_ref[...])

    row = pl.BlockSpec((tm, D), lambda i: (i, 0))
    arow = pl.BlockSpec((tm, AW), lambda i: (i, 0))
    full = lambda a: pl.BlockSpec(a.shape, lambda i: (0, 0))
    return pl.pallas_call(
        body, name="merge_fwd_bwd", grid=(S // tm,),
        in_specs=[arow, row, pl.BlockSpec((tm, D), lambda i: (i, 6)), pl.BlockSpec((tm, D), lambda i: (i, 7)),
                  row, row, pl.BlockSpec((8, D), lambda i: (0, 0)), arow, pl.BlockSpec((tm, AW), lambda i: (i, 3)),
                  full(wa), full(wb), full(wo), full(hsum)],
        out_specs=[row, row, row, row, row, pl.BlockSpec((tm, 2 * D), lambda i: (i, 0)),
                   arow, arow, arow, row, pl.BlockSpec((8, D), lambda i: (0, 0)),
                   pl.BlockSpec((8, 128), lambda i: (0, 0))],
        out_shape=[jax.ShapeDtypeStruct((S, D), BF16), jax.ShapeDtypeStruct((S, D), F32),
                   jax.ShapeDtypeStruct((S, D), BF16), jax.ShapeDtypeStruct((S, D), BF16),
                   jax.ShapeDtypeStruct((S, D), BF16), jax.ShapeDtypeStruct((S, 2 * D), BF16),
                   jax.ShapeDtypeStruct((S, AW), F32), jax.ShapeDtypeStruct((S, AW), F32),
                   jax.ShapeDtypeStruct((S, AW), BF16), jax.ShapeDtypeStruct((S, D), F32),
                   jax.ShapeDtypeStruct((8, D), F32), jax.ShapeDtypeStruct((8, 128), F32)],
        compiler_params=_cp(),
    )(oa, ob, proj, proj, x, tgt, vecs, attn, proj, wa, wb, wo, hsum)


def _atb_call(a, b, name):
    S, K = a.shape
    N = b.shape[1]
    tm = min(1024, S)

    def body(a_ref, b_ref, o_ref, ob_ref):
        @pl.when(pl.program_id(0) == 0)
        def _():
            o_ref[...] = jnp.zeros_like(o_ref)

        o_ref[...] += _tn(a_ref[...], b_ref[...])

        @pl.when(pl.program_id(0) == S // tm - 1)
        def _():
            ob_ref[...] = o_ref[...].astype(BF16)

    ospec = pl.BlockSpec((K, N), lambda i: (0, 0))
    return pl.pallas_call(
        body, name=name, grid=(S // tm,),
        in_specs=[pl.BlockSpec((tm, K), lambda i: (i, 0)), pl.BlockSpec((tm, N), lambda i: (i, 0))],
        out_specs=[ospec, ospec],
        out_shape=[jax.ShapeDtypeStruct((K, N), F32), jax.ShapeDtypeStruct((K, N), BF16)], compiler_params=_cp(),
    )(a, b)


def _atb3_call(pairs):
    S = pairs[0][0].shape[0]
    tm = 512
    n = len(pairs)
    shapes = [(a.shape[1], b.shape[1]) for a, b in pairs]

    def body(*refs):
        ins, outs = refs[:2 * n], refs[2 * n:]

        @pl.when(pl.program_id(0) == 0)
        def _():
            for u in range(n):
                outs[2 * u][...] = jnp.zeros_like(outs[2 * u])

        for u in range(n):
            outs[2 * u][...] += _tn(ins[2 * u][...], ins[2 * u + 1][...])

        @pl.when(pl.program_id(0) == S // tm - 1)
        def _():
            for u in range(n):
                outs[2 * u + 1][...] = outs[2 * u][...].astype(BF16)

    in_specs, out_specs, out_shape = [], [], []
    for (a, b), (K, N) in zip(pairs, shapes):
        in_specs += [pl.BlockSpec((tm, K), lambda i: (i, 0)), pl.BlockSpec((tm, N), lambda i: (i, 0))]
        out_specs += [pl.BlockSpec((K, N), lambda i: (0, 0))] * 2
        out_shape += [jax.ShapeDtypeStruct((K, N), F32), jax.ShapeDtypeStruct((K, N), BF16)]
    res = pl.pallas_call(
        body, name="dw_small3", grid=(S // tm,), in_specs=in_specs, out_specs=out_specs, out_shape=out_shape,
        compiler_params=_cp(),
    )(*[t for p in pairs for t in p])
    return [(res[2 * u], res[2 * u + 1]) for u in range(n)]


def _dwin_call(h_t, dqkvz, d_hgrn, d_gates):
    S = h_t.shape[1]
    tm = 512
    tn = 2048

    def body(h_ref, q_ref, k_ref, v_ref, z_ref, m_ref, g_ref, o_ref, ob_ref):
        j = pl.program_id(0)

        @pl.when(pl.program_id(1) == 0)
        def _():
            o_ref[...] = jnp.zeros_like(o_ref)

        hv = h_ref[...]

        @pl.when(j == 0)
        def _():
            for cidx, r in enumerate((q_ref, k_ref, v_ref, z_ref)):
                o_ref[0, :, cidx * AW:(cidx + 1) * AW] += _nn(hv, r[...])

        @pl.when(jnp.logical_or(j == 1, j == 2))
        def _():
            o_ref[0] += _nn(hv, m_ref[...])

        @pl.when(j == 3)
        def _():
            o_ref[0] += _nn(hv, g_ref[...])

        @pl.when(pl.program_id(1) == S // tm - 1)
        def _():
            ob_ref[...] = o_ref[...].astype(BF16)

    aspec = pl.BlockSpec((tm, AW), lambda j, i: (jnp.where(j == 0, i, 0), 0))
    ospec = pl.BlockSpec((1, D, tn), lambda j, i: (j, 0, 0))
    return pl.pallas_call(
        body, name="dw_in", grid=(4, S // tm),
        in_specs=[pl.BlockSpec((D, tm), lambda j, i: (0, i)), aspec, aspec, aspec, aspec,
                  pl.BlockSpec((tm, tn), lambda j, i: (jnp.where(jnp.logical_or(j == 1, j == 2), i, 0),
                                                       jnp.where(j == 2, 1, 0))),
                  pl.BlockSpec((tm, tn), lambda j, i: (jnp.where(j == 3, i, 0), 0))],
        out_specs=[ospec, ospec],
        out_shape=[jax.ShapeDtypeStruct((4, D, tn), F32), jax.ShapeDtypeStruct((4, D, tn), BF16)],
        compiler_params=_cp(),
    )(h_t, *dqkvz, d_hgrn, d_gates)


def _dh_call(dqkvz, d_hgrn, d_gates, w_in, x, dx2, vecs):
    S = x.shape[0]
    tm = 512

    def body(q_ref, k_ref, v_ref, z_ref, m_ref, g_ref, w_ref, x_ref, dx2_ref, p_ref, gx_ref, acc_ref):
        @pl.when(pl.program_id(0) == 0)
        def _():
            acc_ref[...] = jnp.zeros_like(acc_ref)

        dhv = _nt(q_ref[...], w_ref[:, 0:AW])
        for cidx, r in enumerate((k_ref, v_ref, z_ref)):
            dhv += _nt(r[...], w_ref[:, (cidx + 1) * AW:(cidx + 2) * AW])
        dhv += _nt(m_ref[...], w_ref[:, 4 * AW:4 * AW + 4 * D])
        dhv += _nt(g_ref[...], w_ref[:, 4 * AW + 4 * D:NPROJ])
        xv = x_ref[...]
        r = lax.rsqrt(jnp.mean(xv * xv, axis=-1, keepdims=True) + EPS)
        xn = xv * r
        acc_ref[0:1, :] += jnp.sum(dhv, axis=0, keepdims=True)
        acc_ref[1:2, :] += jnp.sum(dhv * xn * p_ref[1:2, :], axis=0, keepdims=True)
        acc_ref[2:3, :] += jnp.sum(dhv * xn * p_ref[2:3, :], axis=0, keepdims=True)
        dxn = dhv * p_ref[0:1, :]
        gx_ref[...] = dx2_ref[...] + r * dxn - xv * (r * r * r * jnp.mean(dxn * xv, axis=-1, keepdims=True))

    row = pl.BlockSpec((tm, D), lambda i: (i, 0))
    aspec = pl.BlockSpec((tm, AW), lambda i: (i, 0))
    const = lambda shape: pl.BlockSpec(shape, lambda i: (0, 0))
    return pl.pallas_call(
        body, name="dh_gradx", grid=(S // tm,),
        in_specs=[aspec, aspec, aspec, aspec,
                  pl.BlockSpec((tm, 4 * D), lambda i: (i, 0)), pl.BlockSpec((tm, 2 * D), lambda i: (i, 0)),
                  pl.BlockSpec((D, NPROJ), lambda i: (0, 0), pipeline_mode=pl.Buffered(1)),
                  row, row, const((8, D))],
        out_specs=[row, const((8, D))],
        out_shape=[jax.ShapeDtypeStruct((S, D), F32), jax.ShapeDtypeStruct((8, D), F32)],
        compiler_params=_cp(),
    )(*dqkvz, d_hgrn, d_gates, w_in, x, dx2, vecs)


def _adamw_math(w, g, m, v):
    m = B1 * m + (1.0 - B1) * g
    v = B2 * v + (1.0 - B2) * (g * g)
    m_hat = m / (1.0 - B1 ** STEP)
    v_hat = v / (1.0 - B2 ** STEP)
    delta = -LR * (m_hat / (jnp.sqrt(v_hat) + AEPS) + WD * w)
    return delta, m, v


def _adamw_call(w, g, m, v, name):
    R, C = w.shape
    tr = R if R * C * 4 <= (1 << 20) else max(8, (1 << 20) // (C * 4))
    assert R % tr == 0

    def body(w_ref, g_ref, m_ref, v_ref, go_ref, d_ref, nm_ref, nv_ref):
        g = g_ref[...]
        go_ref[...] = g
        d_ref[...], nm_ref[...], nv_ref[...] = _adamw_math(w_ref[...], g, m_ref[...], v_ref[...])

    blk = pl.BlockSpec((tr, C), lambda i: (i, 0))
    return pl.pallas_call(
        body, name=name, grid=(R // tr,), in_specs=[blk] * 4, out_specs=[blk] * 4,
        out_shape=[jax.ShapeDtypeStruct((R, C), F32)] * 4, compiler_params=_cp(),
    )(w, g, m, v)


def _mod_call(c_all, w_ada_s, b_s):
    def body(c_ref, w_ref, b_ref, o_ref):
        cv = c_ref[...]
        sc = cv * _sig(cv)
        o_ref[...] = jnp.dot(sc, w_ref[...], preferred_element_type=F32,
                             precision=lax.Precision.HIGHEST) + b_ref[...]

    return pl.pallas_call(
        body, name="ada_mod", out_shape=jax.ShapeDtypeStruct((8, w_ada_s.shape[1]), F32),
        compiler_params=_cp(),
    )(c_all, w_ada_s, b_s)


def _ada_update_call(sct, dm, w, m, v):
    R, C = w.shape
    tr = 256

    def body(s_ref, d_ref, w_ref, m_ref, v_ref, g_ref, dl_ref, nm_ref, nv_ref):
        g = s_ref[:, 0:1] * d_ref[0:1, :]
        for b in range(1, 8):
            g = g + s_ref[:, b:b + 1] * d_ref[b:b + 1, :]
        g_ref[...] = g
        dl_ref[...], nm_ref[...], nv_ref[...] = _adamw_math(w_ref[...], g, m_ref[...], v_ref[...])

    blk = pl.BlockSpec((tr, C), lambda i: (i, 0))
    return pl.pallas_call(
        body, name="ada_update", grid=(R // tr,),
        in_specs=[pl.BlockSpec((tr, 8), lambda i: (i, 0)), pl.BlockSpec((8, C), lambda i: (0, 0)), blk, blk, blk],
        out_specs=[blk] * 4, out_shape=[jax.ShapeDtypeStruct((R, C), F32)] * 4, compiler_params=_cp(),
    )(sct, dm, w, m, v)


def _sum8_call(packs):
    def body(p_ref, o_ref):
        acc = p_ref[0]
        for k in range(1, 8):
            acc = acc + p_ref[k]
        o_ref[...] = acc

    return pl.pallas_call(
        body, name="sum_small", out_shape=jax.ShapeDtypeStruct(packs.shape[1:], F32), compiler_params=_cp(),
    )(packs)


def _local_step(x, tgt, shift, scale, gate, norm_g, hgrn_onorm_g, rel_bias, lb, final_g, weights_fn, hook=None):
    a = norm_g * (1.0 + scale)
    z6 = jnp.zeros((6, D), F32)
    h, h_t, proj, w_in, rest_fn = weights_fn(x, jnp.concatenate([a, shift, z6], 0))

    biases = _bias_tiles(rel_bias) * LOG2E
    attn, lse, oa = _attn_fwd_call(proj, biases)

    gn = jnp.tile(hgrn_onorm_g, (1, NH))
    oraw, ob, states = _hgrn_fwd_call(proj, lb, gn)

    wa, wb, wo = rest_fn(ob)
    vecs2 = jnp.concatenate([gate, final_g, z6], 0)
    hsum = jnp.asarray(np.kron(np.eye(NH), np.ones((HE, HE))), BF16)
    y, dx2, du, dya, dyb, d_gates, dattn, dsum, dza, dob, acc2, lsq = _merge_call(
        oa, ob, proj, x, tgt, vecs2, attn, wa, wb, wo, hsum)
    loss = 0.5 * lsq[0, 0] / D
    (d_wo, d_wo16), (d_wa, d_wa16), (d_wb, d_wb16) = _atb3_call([(y, du), (oa, dya), (ob, dyb)])

    d_hgrn, acch = _hgrn_bwd_call(proj, oraw, dob, states, lb, gn)

    dq, dk, dv, dbs = _attn_bwd_call(proj, dattn, lse, dsum, biases)
    dqkvz = (dq, dk, dv, dza)

    d_win, d_win16 = _dwin_call(h_t, dqkvz, d_hgrn, d_gates)
    tok = hook((d_win, d_wa, d_wb, d_wo), (d_win16, d_wa16, d_wb16, d_wo16)) if hook is not None else 0.0
    one_scale = 1.0 + scale
    grad_x, acc1 = _dh_call(dqkvz, d_hgrn, d_gates, w_in, x, dx2,
                            jnp.concatenate([a + tok, norm_g, one_scale, jnp.zeros((5, D), F32)], 0))

    d_rel = jnp.zeros((NBUCKETS, NH), F32)
    for p, d in enumerate(PATTERNS):
        band, bucket = _band_bucket(d)
        onehot = (bucket[None] == np.arange(NBUCKETS)[:, None, None]) & band[None]
        d_rel = d_rel + jnp.einsum("hqk,bqk->bh", dbs[p], jnp.asarray(onehot, F32),
                                   precision=lax.Precision.HIGHEST)
    d_onorm = jnp.sum(acch[1].reshape(NH, HK), axis=0)

    zrow = jnp.zeros((D,), F32)
    pack = jnp.stack([acc1[0], acc1[1], acc2[0], acc1[2], acc2[1], acch[0],
                      zrow.at[:HK].set(d_onorm), zrow.at[0].set(loss),
                      zrow.at[:NBUCKETS * NH].set(d_rel.reshape(-1))] + [zrow] * 7, 0)
    return grad_x, d_win, d_wa, d_wb, d_wo, pack


def _me():
    return lax.axis_index("x"), lax.axis_index("y"), lax.axis_index("c")


def _peers(x, y):
    return [(1 - x, y), (x, 1 - y), (1 - x, 1 - y)]


def _allgather_small(blk, name):
    m_per, n = blk.shape

    def body(x_ref, out_ref, send_sems, recv_sems, local_sem):
        x, y, c = _me()
        me, sibling = (x, y, c), (x, y, 1 - c)
        chips = _peers(x, y)

        def rows(px, py, pc):
            return out_ref.at[pl.ds((4 * px + 2 * py + pc) * m_per, m_per), :]

        def copy(k, block, to, src=None):
            return pltpu.make_async_remote_copy(
                src_ref=rows(*block) if src is None else src, dst_ref=rows(*block),
                send_sem=send_sems.at[k], recv_sem=recv_sems.at[k], device_id=to, device_id_type=MESH)

        mine = pltpu.make_async_copy(x_ref, rows(*me), local_sem)
        mine.start()
        first = [copy(0, me, sibling, src=x_ref)]
        first += [copy(1 + j, me, (*chip, c), src=x_ref) for j, chip in enumerate(chips)]
        for cp in first:
            cp.start()
        passed = [copy(4 + j, (*chip, c), sibling) for j, chip in enumerate(chips)]
        for j, chip in enumerate(chips):
            copy(1 + j, (*chip, c), me).wait_recv()
            passed[j].start()
        copy(0, sibling, me).wait_recv()
        for j, chip in enumerate(chips):
            copy(4 + j, (*chip, 1 - c), me).wait_recv()
        for cp in first + passed:
            cp.wait_send()
        mine.wait()

    return pl.pallas_call(
        body, name=name, out_shape=jax.ShapeDtypeStruct((8 * m_per, n), blk.dtype),
        in_specs=[pl.BlockSpec(memory_space=pltpu.VMEM)], out_specs=pl.BlockSpec(memory_space=pltpu.VMEM),
        scratch_shapes=[pltpu.SemaphoreType.DMA((7,)), pltpu.SemaphoreType.DMA((7,)), pltpu.SemaphoreType.DMA],
    )(blk)


ANY = pl.BlockSpec(memory_space=pl.ANY)


HBM = pl.BlockSpec(memory_space=pltpu.HBM)
SEM = pl.BlockSpec(memory_space=pltpu.SEMAPHORE)
EFFECT = pltpu.SideEffectType.DATAFLOW_SIDE_EFFECTING


def _w_part(t, ref, j, half):
    if t == 0:
        return ref.at[pl.ds(half * (D // 2), D // 2), pl.ds(j * 2048, 2048)]
    if t == 1:
        return ref.at[pl.ds(half * (AW // 2), AW // 2), pl.ds(j * 256, 256)]
    return ref.at[pl.ds(j * 256 + half * 128, 128), :]


def _w_ici_copies(fulls, send_sems, recv_sems, tensors=(0, 1, 2, 3)):
    x, y, c = _me()
    outs, ins = [], []
    for u, t in enumerate(tensors):
        for k, chip in enumerate(_peers(x, y)):
            mine = _w_part(t, fulls[u], 2 * x + y, c)
            theirs = _w_part(t, fulls[u], 2 * chip[0] + chip[1], c)
            kw = dict(send_sem=send_sems.at[3 * t + k], recv_sem=recv_sems.at[3 * t + k],
                      device_id=(*chip, c), device_id_type=MESH)
            outs.append(pltpu.make_async_remote_copy(src_ref=mine, dst_ref=mine, **kw))
            ins.append(pltpu.make_async_remote_copy(src_ref=theirs, dst_ref=theirs, **kw))
    return outs, ins


def _gather_ici_start(fulls):
    def body(f0, f1, f2, f3, send_sems, recv_sems, t0, t1, t2, t3, token):
        for cp in _w_ici_copies([f0, f1, f2, f3], send_sems, recv_sems)[0]:
            cp.start()
        token[...] = jnp.zeros_like(token)

    res = pl.pallas_call(
        body, name="gather_ici_start",
        out_shape=(pltpu.SemaphoreType.DMA((12,)), pltpu.SemaphoreType.DMA((12,)),
                   *[pltpu.HBM(a.shape, a.dtype) for a in fulls], jax.ShapeDtypeStruct((8, 128), F32)),
        in_specs=[HBM] * 4, out_specs=(SEM, SEM, *[HBM] * 4, pl.BlockSpec(memory_space=pltpu.VMEM)),
        input_output_aliases={i: 2 + i for i in range(4)},
        compiler_params=pltpu.CompilerParams(has_side_effects=EFFECT),
    )(*[pltpu.with_memory_space_constraint(a, pltpu.HBM) for a in fulls])
    return res[0], res[1], list(res[2:6]), res[6]


def _gather_ici_wait(send_sems, recv_sems, bufs, after, tensors, name):
    n = len(tensors)

    def body(*refs):
        outs, ins = _w_ici_copies(refs[0:n], refs[n], refs[n + 1], tensors)
        for cp in outs:
            cp.wait_send()
        for cp in ins:
            cp.wait_recv()

    return pl.pallas_call(
        body, name=name, out_shape=tuple(pltpu.HBM(a.shape, a.dtype) for a in bufs),
        in_specs=[HBM] * n + [SEM, SEM, ANY], out_specs=[HBM] * n,
        input_output_aliases={i: i for i in range(n)},
        compiler_params=pltpu.CompilerParams(has_side_effects=EFFECT),
    )(*bufs, send_sems, recv_sems, after)


def _gather_sibling(bufs, tensors, name):
    n = len(tensors)

    def body(*refs):
        outs, send_sems, recv_sems = refs[n:2 * n], refs[2 * n], refs[2 * n + 1]
        x, y, c = _me()
        cps = []
        for u, t in enumerate(tensors):
            for k, chip in enumerate(_peers(x, y)):
                blk = _w_part(t, outs[u], 2 * chip[0] + chip[1], c)
                cp = pltpu.make_async_remote_copy(
                    src_ref=blk, dst_ref=blk, send_sem=send_sems.at[u, k], recv_sem=recv_sems.at[u, k],
                    device_id=(x, y, 1 - c), device_id_type=MESH)
                cp.start()
                cps.append(cp)
        for u, t in enumerate(tensors):
            for k, chip in enumerate(_peers(x, y)):
                blk = _w_part(t, outs[u], 2 * chip[0] + chip[1], 1 - c)
                pltpu.make_async_remote_copy(
                    src_ref=blk, dst_ref=blk, send_sem=send_sems.at[u, k], recv_sem=recv_sems.at[u, k],
                    device_id=(x, y, 1 - c), device_id_type=MESH).wait_recv()
        for cp in cps:
            cp.wait_send()

    return pl.pallas_call(
        body, name=name,
        out_shape=[jax.ShapeDtypeStruct(FULL_W_SHAPES[t], BF16) for t in tensors],
        in_specs=[ANY] * n, out_specs=[ANY] * n, input_output_aliases={u: u for u in range(n)},
        scratch_shapes=[pltpu.SemaphoreType.DMA((n, 3)), pltpu.SemaphoreType.DMA((n, 3))],
    )(*bufs)


def _half_of(t, ref, half):
    if t == 0:
        return ref.at[:, pl.ds(half * 512, 512), :]
    if t == 1:
        return ref.at[pl.ds(half * 256, 256), :]
    return ref.at[:, pl.ds(half * 512, 512)]


HALF_SHAPES = [(4, 512, 2048), (256, D), (D, 512), (D, 512)]
PIECE_SHAPES = [(512, 2048), (256, 256), (256, 512), (256, 512)]
SHARD_SHAPES = [(D, 2048), (AW, 256), (256, D), (256, D)]


def _chip_piece(t, ref, j):
    if t == 0:
        return ref.at[j]
    if t == 1:
        return ref.at[:, pl.ds(j * 256, 256)]
    return ref.at[pl.ds(j * 256, 256), :]


def _reduce_sibling_send(gs):
    def body(g0, g1, g2, g3, r0, r1, r2, r3, send_sems, recv_sems):
        x, y, c = _me()
        ins, outs = [g0, g1, g2, g3], [r0, r1, r2, r3]
        cps = []
        for t in range(4):
            cp = pltpu.make_async_remote_copy(
                src_ref=_half_of(t, ins[t], 1 - c), dst_ref=outs[t],
                send_sem=send_sems.at[t], recv_sem=recv_sems.at[t], device_id=(x, y, 1 - c), device_id_type=MESH)
            cp.start()
            cps.append(cp)
        for cp in cps:
            cp.wait_recv()
        for cp in cps:
            cp.wait_send()

    return pl.pallas_call(
        body, name="reduce_sibling", out_shape=[jax.ShapeDtypeStruct(s, BF16) for s in HALF_SHAPES],
        in_specs=[ANY] * 4, out_specs=[ANY] * 4,
        scratch_shapes=[pltpu.SemaphoreType.DMA((4,)), pltpu.SemaphoreType.DMA((4,))],
    )(*gs)


def _chip_copies(hs, lands, send_sems, recv_sems):
    x, y, c = _me()
    cps = []
    for t in range(4):
        for k, chip in enumerate(_peers(x, y)):
            pj = 2 * chip[0] + chip[1]
            cps.append(pltpu.make_async_remote_copy(
                src_ref=_chip_piece(t, hs[t], pj), dst_ref=lands[t].at[k],
                send_sem=send_sems.at[3 * t + k], recv_sem=recv_sems.at[3 * t + k],
                device_id=(*chip, c), device_id_type=MESH))
    return cps


def _reduce_chips_start(hs):
    lands = [lax.empty((3,) + s, BF16) for s in PIECE_SHAPES]

    def body(h0, h1, h2, h3, l0, l1, l2, l3, send_sems, recv_sems, t0, t1, t2, t3, t4, t5, t6, t7, token):
        for cp in _chip_copies([h0, h1, h2, h3], [l0, l1, l2, l3], send_sems, recv_sems):
            cp.start()
        token[...] = jnp.zeros_like(token)

    bufs = list(hs) + lands
    res = pl.pallas_call(
        body, name="reduce_chips_start",
        out_shape=(pltpu.SemaphoreType.DMA((12,)), pltpu.SemaphoreType.DMA((12,)),
                   *[pltpu.HBM(a.shape, a.dtype) for a in bufs], jax.ShapeDtypeStruct((8, 128), F32)),
        in_specs=[HBM] * 8, out_specs=(SEM, SEM, *[HBM] * 8, pl.BlockSpec(memory_space=pltpu.VMEM)),
        input_output_aliases={i: 2 + i for i in range(8)},
        compiler_params=pltpu.CompilerParams(has_side_effects=EFFECT),
    )(*[pltpu.with_memory_space_constraint(a, pltpu.HBM) for a in bufs])
    return res[0], res[1], list(res[2:10]), res[10]


def _reduce_chips_wait(send_sems, recv_sems, thru, after):
    def body(h0, h1, h2, h3, l0, l1, l2, l3, send_sems, recv_sems, after_ref, d0, d1, d2, d3, g0, g1, g2, g3):
        cps = _chip_copies([h0, h1, h2, h3], [l0, l1, l2, l3], send_sems, recv_sems)
        for cp in cps:
            cp.wait_send()
        for cp in cps:
            cp.wait_recv()

    res = pl.pallas_call(
        body, name="reduce_chips_wait", out_shape=tuple(pltpu.HBM(a.shape, a.dtype) for a in thru),
        in_specs=[HBM] * 8 + [SEM, SEM, ANY], out_specs=[HBM] * 8,
        input_output_aliases={i: i for i in range(8)},
        compiler_params=pltpu.CompilerParams(has_side_effects=EFFECT),
    )(*thru, send_sems, recv_sems, after)
    return list(res[4:8])


def _share_sibling(shards):
    def body(i0, i1, i2, i3, o0, o1, o2, o3, send_sems, recv_sems):
        x, y, c = _me()
        outs = [o0, o1, o2, o3]

        def half(t, ref, hf):
            if t == 0:
                return ref.at[pl.ds(hf * 512, 512), :]
            if t == 1:
                return ref.at[pl.ds(hf * 256, 256), :]
            return ref.at[:, pl.ds(hf * 512, 512)]

        cps = []
        for t in range(4):
            mine = half(t, outs[t], c)
            cp = pltpu.make_async_remote_copy(
                src_ref=mine, dst_ref=mine, send_sem=send_sems.at[t], recv_sem=recv_sems.at[t],
                device_id=(x, y, 1 - c), device_id_type=MESH)
            cp.start()
            cps.append(cp)
        for t in range(4):
            theirs = half(t, outs[t], 1 - c)
            pltpu.make_async_remote_copy(
                src_ref=theirs, dst_ref=theirs, send_sem=send_sems.at[t],
                recv_sem=recv_sems.at[t], device_id=(x, y, 1 - c), device_id_type=MESH).wait_recv()
        for cp in cps:
            cp.wait_send()

    return pl.pallas_call(
        body, name="share_sibling", out_shape=[jax.ShapeDtypeStruct(s, F32) for s in SHARD_SHAPES],
        in_specs=[ANY] * 4, out_specs=[ANY] * 4, input_output_aliases={0: 0, 1: 1, 2: 2, 3: 3},
        scratch_shapes=[pltpu.SemaphoreType.DMA((4,)), pltpu.SemaphoreType.DMA((4,))],
    )(*shards)


def _half_blockspec(t, idx_pos):
    if t == 0:
        return pl.BlockSpec((1, 512, 2048), lambda i, s: (i, s[idx_pos], 0)), 4
    if t == 1:
        return pl.BlockSpec((256, D), lambda i, s: (s[idx_pos], 0)), 1
    return pl.BlockSpec((256, 512), lambda i, s: (i, s[idx_pos])), 4


def _half_out_blockspec(t):
    if t == 0:
        return pl.BlockSpec((1, 512, 2048), lambda i, s: (i, 0, 0))
    if t == 1:
        return pl.BlockSpec((256, D), lambda i, s: (0, 0))
    return pl.BlockSpec((256, 512), lambda i, s: (i, 0))


def _add_half_call(t, own, recv, sc, name):
    in_blk, steps = _half_blockspec(t, 0)
    out_blk = _half_out_blockspec(t)

    def body(s_ref, a_ref, b_ref, o_ref, ob_ref):
        v = a_ref[...] + b_ref[...].astype(F32)
        o_ref[...] = v
        ob_ref[...] = v.astype(BF16)

    return pl.pallas_call(
        body, name=name,
        grid_spec=pltpu.PrefetchScalarGridSpec(
            num_scalar_prefetch=1, grid=(steps,), in_specs=[in_blk, out_blk], out_specs=[out_blk, out_blk]),
        out_shape=[jax.ShapeDtypeStruct(HALF_SHAPES[t], F32), jax.ShapeDtypeStruct(HALF_SHAPES[t], BF16)],
        compiler_params=_cp(),
    )(sc, own, recv)


def _final_piece_call(t, chipsum, recv3, sc, name):
    ps = PIECE_SHAPES[t]
    if t == 0:
        own_blk = pl.BlockSpec((1,) + ps, lambda i, s: (s[1], 0, 0))
        o_blk = pl.BlockSpec(ps, lambda i, s: (s[0], 0))
    elif t == 1:
        own_blk = pl.BlockSpec(ps, lambda i, s: (0, s[1]))
        o_blk = pl.BlockSpec(ps, lambda i, s: (s[0], 0))
    else:
        own_blk = pl.BlockSpec(ps, lambda i, s: (s[1], 0))
        o_blk = pl.BlockSpec(ps, lambda i, s: (0, s[0]))
    r_blk = pl.BlockSpec((3,) + ps, lambda i, s: (0, 0, 0))

    def body(s_ref, a_ref, r_ref, o_ref):
        a = a_ref[0] if t == 0 else a_ref[...]
        o_ref[...] = ((a + r_ref[0].astype(F32)) + r_ref[1].astype(F32)) + r_ref[2].astype(F32)

    return pl.pallas_call(
        body, name=name,
        grid_spec=pltpu.PrefetchScalarGridSpec(
            num_scalar_prefetch=1, grid=(1,), in_specs=[own_blk, r_blk], out_specs=o_blk),
        out_shape=jax.ShapeDtypeStruct(SHARD_SHAPES[t], F32), compiler_params=_cp(),
    )(sc, chipsum, recv3)


FULL_W_SHAPES = [(D, NPROJ), (AW, D), (D, D), (D, D)]


def _cast_place_call(t, shard, sc, name):
    if t == 0:
        blk, steps = (512, 2048), 2
        in_blk = pl.BlockSpec(blk, lambda i, s: (i, 0))
        o_blk = pl.BlockSpec(blk, lambda i, s: (i, s[1]))
    elif t == 1:
        blk, steps = (AW, 256), 1
        in_blk = pl.BlockSpec(blk, lambda i, s: (0, 0))
        o_blk = pl.BlockSpec(blk, lambda i, s: (0, s[1]))
    else:
        blk, steps = (256, D), 1
        in_blk = pl.BlockSpec(blk, lambda i, s: (0, 0))
        o_blk = pl.BlockSpec(blk, lambda i, s: (s[1], 0))

    def body(s_ref, a_ref, o_ref, own_ref):
        v = a_ref[...].astype(BF16)
        o_ref[...] = v
        own_ref[...] = v

    return pl.pallas_call(
        body, name=name,
        grid_spec=pltpu.PrefetchScalarGridSpec(
            num_scalar_prefetch=1, grid=(steps,), in_specs=[in_blk], out_specs=[o_blk, in_blk]),
        out_shape=[jax.ShapeDtypeStruct(FULL_W_SHAPES[t], BF16), jax.ShapeDtypeStruct(shard.shape, BF16)],
        compiler_params=_cp(),
    )(sc, shard)


def _lower_bound_fn(hgrn_lb):
    return jnp.cumsum(jax.nn.softmax(hgrn_lb.astype(F32), axis=0), axis=0)[0]


def kernel(x, c, w_ada, b_ada, norm_g, w_in, hgrn_onorm_g, w_branch_a, w_branch_b, w_out, rel_bias, hgrn_lb, final_g, loss_target, m_w_ada, m_b_ada, m_norm_g, m_w_in, m_hgrn_onorm_g, m_w_branch_a, m_w_branch_b, m_w_out, m_rel_bias, m_hgrn_lb, m_final_g, v_w_ada, v_b_ada, v_norm_g, v_w_in, v_hgrn_onorm_g, v_w_branch_a, v_w_branch_b, v_w_out, v_rel_bias, v_hgrn_lb, v_final_g):
    ax, ay, ac = _me()
    chip = 2 * ax + ay
    dev = 4 * ax + 2 * ay + ac
    sc_idx = jnp.stack([ac, chip]).astype(jnp.int32)

    c_all = _allgather_small(jnp.pad(c, ((0, 7), (0, 0))), "gather_c").reshape(8, 8, D)[:, 0]
    b_s = lax.dynamic_slice(b_ada, (0, chip * 768), (1, 768))
    mod_part = _mod_call(c_all, w_ada[0], b_s)
    mod_all = _allgather_small(mod_part, "gather_mod").reshape(8, 8, 768)
    mod_mine = lax.dynamic_index_in_dim(mod_all, dev, axis=1, keepdims=False)
    mod = mod_mine[0::2].reshape(1, 3 * D)

    names = ["w_in", "w_a", "w_b", "w_o"]
    shards, mod = lax.optimization_barrier(([w_in[0], w_branch_a[0], w_branch_b[0], w_out[0]], mod))
    placed = [_cast_place_call(t, shards[t], sc_idx, "cast_" + names[t]) for t in range(4)]
    w_send_sems, w_recv_sems, w_thru, w_token = _gather_ici_start([p_[0] for p_ in placed])
    mod = mod + w_token[0, 0]
    rel_bias_t = rel_bias + w_token[0, 0]
    shift, scale, gate = mod[:, :D], mod[:, D:2 * D], mod[:, 2 * D:]

    def weights_fn(xs, avec):
        proj, h, h_t = _proj_own_call(xs, avec, placed[0][1], sc_idx)
        arrived = _gather_ici_wait(w_send_sems, w_recv_sems, w_thru[0:1], proj, (0,), "gather_ici_wait_in")
        (win_f,) = _gather_sibling(arrived, (0,), "gather_sibling_in")

        def rest_fn(after):
            late = _gather_ici_wait(w_send_sems, w_recv_sems, w_thru[1:4], after, (1, 2, 3), "gather_ici_wait_rest")
            return _gather_sibling(late, (1, 2, 3), "gather_sibling_rest")

        return h, h_t, _proj_rest_call(h, win_f, proj, sc_idx), win_f, rest_fn

    flight = {}

    def start_reduction(own, own16):
        sib = _reduce_sibling_send(own16)
        halves = [_add_half_call(t, own[t], sib[t], sc_idx, "chipsum_" + names[t]) for t in range(4)]
        send_sems, recv_sems, thru, token = _reduce_chips_start([hb for _, hb in halves])
        flight.update(sems=(send_sems, recv_sems), thru=thru, sums=[hf for hf, _ in halves])
        return token[0, 0]

    lb, lb_vjp = jax.vjp(_lower_bound_fn, hgrn_lb)
    grad_x, d_win, d_wa, d_wb, d_wo, pack = _local_step(
        x[0], loss_target[0], shift, scale, gate, norm_g, hgrn_onorm_g, rel_bias_t, lb[None, :],
        final_g[None, :], weights_fn, hook=start_reduction)

    rec = _reduce_chips_wait(*flight["sems"], flight["thru"], pack)
    pieces = [_final_piece_call(t, flight["sums"][t], rec[t], sc_idx, "piece_" + names[t]) for t in range(4)]
    g_win, g_wa, g_wb, g_wo = _share_sibling(pieces)

    packs = _allgather_small(pack, "gather_small").reshape(8, 16, D)
    tot = _sum8_call(packs)
    loss = tot[7, 0]
    g_b_ada = tot[0:3].reshape(1, 3 * D)
    g_norm_g = tot[3:4]
    g_final_g = tot[4]
    (g_hgrn_lb,) = lb_vjp(tot[5])
    g_onorm = tot[6:7, :HK]
    g_rel = tot[8, :NBUCKETS * NH].reshape(NBUCKETS, NH)

    def rows_of(a):
        flat = a.reshape(-1)
        n = -(-flat.shape[0] // D)
        return jnp.pad(flat, (0, n * D - flat.shape[0])).reshape(n, D)

    smalls = [(b_ada, g_b_ada, m_b_ada, v_b_ada), (norm_g, g_norm_g, m_norm_g, v_norm_g),
              (hgrn_onorm_g, g_onorm, m_hgrn_onorm_g, v_hgrn_onorm_g), (rel_bias, g_rel, m_rel_bias, v_rel_bias),
              (hgrn_lb, g_hgrn_lb, m_hgrn_lb, v_hgrn_lb), (final_g, g_final_g, m_final_g, v_final_g)]
    cat = [jnp.concatenate([rows_of(s[k]) for s in smalls], 0) for k in range(4)]
    cat = [jnp.pad(a, ((0, 16 - a.shape[0]), (0, 0))) for a in cat]
    _, sd, sm, sv = _adamw_call(*cat, "adamw_small")

    def unpack(packed):
        res, r = [], 0
        for s in smalls:
            n = -(-s[0].size // D)
            res.append(packed[r:r + n].reshape(-1)[:s[0].size].reshape(s[0].shape))
            r += n
        return res

    d_small, m_small, v_small = unpack(sd), unpack(sm), unpack(sv)

    sc_all = c_all * jax.nn.sigmoid(c_all)
    dmod_all = packs[:, 0:3].reshape(8, 3 * D)
    dm_s = lax.dynamic_slice(dmod_all, (0, chip * 768), (8, 768))
    g_w_ada, d_w_ada, nm_w_ada, nv_w_ada = _ada_update_call(sc_all.T, dm_s, w_ada[0], m_w_ada[0], v_w_ada[0])

    big = []
    for w, g, m, v, n in [(w_in, g_win, m_w_in, v_w_in, "w_in"), (w_branch_a, g_wa, m_w_branch_a, v_w_branch_a, "w_a"),
                          (w_branch_b, g_wb, m_w_branch_b, v_w_branch_b, "w_b"), (w_out, g_wo, m_w_out, v_w_out, "w_o")]:
        big.append(_adamw_call(w[0], g, m[0], v[0], "adamw_" + n))

    e = lambda a: a[None]
    grads = [e(g_w_ada), g_b_ada, g_norm_g, e(big[0][0]), g_onorm, e(big[1][0]), e(big[2][0]), e(big[3][0]),
             g_rel, g_hgrn_lb, g_final_g]
    deltas = [e(d_w_ada), d_small[0], d_small[1], e(big[0][1]), d_small[2], e(big[1][1]), e(big[2][1]), e(big[3][1]),
              d_small[3], d_small[4], d_small[5]]
    new_m = [e(nm_w_ada), m_small[0], m_small[1], e(big[0][2]), m_small[2], e(big[1][2]), e(big[2][2]), e(big[3][2]),
             m_small[3], m_small[4], m_small[5]]
    new_v = [e(nv_w_ada), v_small[0], v_small[1], e(big[0][3]), v_small[2], e(big[1][3]), e(big[2][3]), e(big[3][3]),
             v_small[3], v_small[4], v_small[5]]
    return (loss, grad_x[None], *grads, *deltas, *new_m, *new_v)
```

```python
import math

import numpy as np
import jax
import jax.numpy as jnp
from jax import lax
from jax.experimental import pallas as pl
from jax.experimental.pallas import tpu as pltpu

D = 1024
AW = 512
NH = 8
HE = 64
HK = 128
NPROJ = 8192
ABLK = 128
PATTERNS = (1, 4, 16)
NBUCKETS = 32
MAXDIST = 2048
NEG = -1e30
EPS = 1e-6
CH = 64
LR, B1, B2, AEPS, WD, STEP = 0.001, 0.9, 0.999, 1e-08, 0.01, 10

F32 = jnp.float32
BF16 = jnp.bfloat16
MESH = pl.DeviceIdType.MESH
VMEM_LIMIT = 56 * 1024 * 1024


def _cp(**kw):
    return pltpu.CompilerParams(vmem_limit_bytes=VMEM_LIMIT, **kw)


def _sig(x):
    return 0.5 * jnp.tanh(0.5 * x) + 0.5


def _nt(a, b):
    return lax.dot_general(a, b, (((1,), (1,)), ((), ())), preferred_element_type=F32)


def _tn(a, b):
    return lax.dot_general(a, b, (((0,), (0,)), ((), ())), preferred_element_type=F32)


def _nn(a, b):
    return jnp.dot(a, b, preferred_element_type=F32)


def _split2(x):
    h = x.astype(BF16)
    return h, (x - h.astype(F32)).astype(BF16)


def _exact_mm(tri_bf16, x):
    h, l = _split2(x)
    return _nn(tri_bf16, h) + _nn(tri_bf16, l)


def _exact_mm_r(x, ones_bf16):
    h, l = _split2(x)
    return _nn(h, ones_bf16) + _nn(l, ones_bf16)


def _proj_own_call(x, avec, w_own, sc):
    S = x.shape[0]
    tm, tn = 512, 2048

    def body(s_ref, x_ref, a_ref, w_ref, o_ref, h_ref, ht_ref):
        xv = x_ref[...]
        r = lax.rsqrt(jnp.mean(xv * xv, axis=-1, keepdims=True) + EPS)
        hv = xv * r * a_ref[0:1, :] + a_ref[1:2, :]
        hb = hv.astype(BF16)
        h_ref[...] = hb
        ht_ref[...] = hv.T.astype(BF16)
        o_ref[...] = _nn(hb, w_ref[...])

    return pl.pallas_call(
        body, name="in_proj_own",
        grid_spec=pltpu.PrefetchScalarGridSpec(
            num_scalar_prefetch=1, grid=(S // tm,),
            in_specs=[pl.BlockSpec((tm, D), lambda i, s: (i, 0)), pl.BlockSpec((8, D), lambda i, s: (0, 0)),
                      pl.BlockSpec((D, tn), lambda i, s: (0, 0))],
            out_specs=[pl.BlockSpec((tm, tn), lambda i, s: (i, s[1])), pl.BlockSpec((tm, D), lambda i, s: (i, 0)),
                       pl.BlockSpec((D, tm), lambda i, s: (0, i))]),
        out_shape=[jax.ShapeDtypeStruct((S, NPROJ), F32), jax.ShapeDtypeStruct((S, D), BF16),
                   jax.ShapeDtypeStruct((D, S), BF16)],
        compiler_params=_cp(),
    )(sc, x, avec, w_own)


def _proj_rest_call(h, w_in, proj, sc):
    S = h.shape[0]
    tm, tn = 512, 2048

    def body(s_ref, h_ref, w_ref, p_ref, o_ref):
        o_ref[...] = _nn(h_ref[...], w_ref[...])

    col = lambda j, s: (s[1] + 1 + j) % 4
    return pl.pallas_call(
        body, name="in_proj_rest",
        grid_spec=pltpu.PrefetchScalarGridSpec(
            num_scalar_prefetch=1, grid=(3, S // tm),
            in_specs=[pl.BlockSpec((tm, D), lambda j, i, s: (i, 0)),
                      pl.BlockSpec((D, tn), lambda j, i, s: (0, col(j, s))),
                      pl.BlockSpec(memory_space=pl.ANY)],
            out_specs=pl.BlockSpec((tm, tn), lambda j, i, s: (i, col(j, s)))),
        out_shape=jax.ShapeDtypeStruct((S, NPROJ), F32), input_output_aliases={3: 0}, compiler_params=_cp(),
    )(sc, h, w_in, proj)


def _t5_bucket_np(dist):
    max_exact = NBUCKETS // 2
    n = dist.astype(np.float32)
    large = max_exact + (np.log(np.maximum(n, np.float32(1.0)) / np.float32(max_exact))
                         / np.float32(math.log(MAXDIST / max_exact))
                         * np.float32(NBUCKETS - max_exact)).astype(np.int32)
    large = np.minimum(large, NBUCKETS - 1)
    return np.where(dist < max_exact, dist, large)


def _band_bucket(d):
    qi = np.arange(ABLK)[:, None]
    kj = np.arange(2 * ABLK)[None, :]
    delta = qi + ABLK - kj
    band = (delta >= 0) & (delta <= ABLK)
    bucket = _t5_bucket_np(np.clip(delta, 0, None) * d)
    return band, bucket


def _bias_tiles(rel_bias):
    tiles = []
    for d in PATTERNS:
        band, bucket = _band_bucket(d)
        onehot = (jnp.asarray(bucket, jnp.int32)[None] == jnp.arange(NBUCKETS, dtype=jnp.int32)[:, None, None])
        bias = jnp.einsum("bqk,bh->hqk", onehot.astype(F32), rel_bias, precision=lax.Precision.HIGHEST)
        tiles.append(jnp.where(jnp.asarray(band)[None], bias, NEG))
    return jnp.stack(tiles, 0)


ATT = 2048
HP = 2 * HE
LOG2E = 1.4426950408889634
LN2 = 0.6931471805599453
QSCALE2 = (HE ** -0.5) * LOG2E
AGRP = 16
AGRP_B = 8


def _attn_blocks():
    out = []
    for p, d in enumerate(PATTERNS):
        for r in range(d):
            for n in range(ATT // (d * ABLK)):
                out.append((p, d, r, n))
    return out


def _attn_fwd_call(proj, biases):
    S = proj.shape[0]
    nt = S // ATT

    def body(q_ref, k_ref, v_ref, z_ref, b_ref, a_ref, l_ref, oa_ref, kc, vc, op, lp):
        i = pl.program_id(1)

        @pl.when(i == 0)
        def _():
            kc[0:ATT] = jnp.zeros((ATT, HP), F32)
            vc[0:ATT] = jnp.zeros((ATT, HP), F32)

        @pl.when(i > 0)
        def _():
            kc[0:ATT] = kc[ATT:2 * ATT]
            vc[0:ATT] = vc[ATT:2 * ATT]

        kc[ATT:2 * ATT] = k_ref[...]
        vc[ATT:2 * ATT] = v_ref[...]
        col = lax.broadcasted_iota(jnp.int32, (ABLK, 2 * ABLK), 1)
        dead = jnp.logical_and(i == 0, col < ABLK)
        blocks = _attn_blocks()
        hs = (slice(0, HE), slice(HE, 2 * HE))
        for g0 in range(0, len(blocks), AGRP):
            grp = blocks[g0:g0 + AGRP]
            qrows = [pl.ds(n * ABLK * d + r, ABLK, stride=d) for p, d, r, n in grp]
            krows = [pl.ds(ATT + (n - 1) * ABLK * d + r, 2 * ABLK, stride=d) for p, d, r, n in grp]
            qs = [(q_ref[qr, :] * QSCALE2).astype(BF16) for qr in qrows]
            ks = [kc[kr, :].astype(BF16) for kr in krows]
            vs = [vc[kr, :].astype(BF16) for kr in krows]
            ss = [[_nt(qs[b][:, sl], ks[b][:, sl]) + b_ref[grp[b][0], e] for e, sl in enumerate(hs)]
                  for b in range(len(grp))]
            ss = [[jnp.where(dead, NEG, s) if grp[b][3] == 0 else s for s in ss[b]] for b in range(len(grp))]
            mxs = [[jnp.max(s, axis=-1, keepdims=True) for s in sb] for sb in ss]
            pes = [[jnp.exp2(s - mx) for s, mx in zip(sb, mb)] for sb, mb in zip(ss, mxs)]
            dens = [[jnp.sum(pe, axis=-1, keepdims=True) for pe in pb] for pb in pes]
            pvs = [[_nn(pe.astype(BF16), vs[b][:, sl]) for pe, sl in zip(pes[b], hs)] for b in range(len(grp))]
            for b in range(len(grp)):
                p, d, r, n = grp[b]
                prow = pl.ds(p * ATT + n * ABLK * d + r, ABLK, stride=d)
                lp[prow, :] = jnp.concatenate(
                    [jnp.broadcast_to(mx + jnp.log2(dn), (ABLK, HE)) for mx, dn in zip(mxs[b], dens[b])], axis=1)
                op[prow, :] = jnp.concatenate([pv / dn for pv, dn in zip(pvs[b], dens[b])], axis=1)
        rt = 256
        for t in range(ATT // rt):
            rows = slice(t * rt, (t + 1) * rt)
            pr = [slice(p * ATT + t * rt, p * ATT + (t + 1) * rt) for p in range(3)]
            la, lb_, lc = lp[pr[0], :], lp[pr[1], :], lp[pr[2], :]
            m = jnp.maximum(jnp.maximum(la, lb_), lc)
            ea, eb, ec = jnp.exp2(la - m), jnp.exp2(lb_ - m), jnp.exp2(lc - m)
            den = ea + eb + ec
            att = (ea * op[pr[0], :] + eb * op[pr[1], :] + ec * op[pr[2], :]) / den
            a_ref[rows, :] = att
            l_ref[rows, :] = m + jnp.log2(den)
            z = z_ref[rows, :]
            oa_ref[rows, :] = (att * (z * _sig(z))).astype(BF16)

    def pcol(c):
        return pl.BlockSpec((ATT, HP), lambda h, i: (i, c * 4 + h))

    out = pl.BlockSpec((ATT, HP), lambda h, i: (i, h))
    return pl.pallas_call(
        body, name="attn_fwd", grid=(4, nt),
        in_specs=[pcol(0), pcol(1), pcol(2), pcol(3),
                  pl.BlockSpec((3, 2, ABLK, 2 * ABLK), lambda h, i: (0, h, 0, 0))],
        out_specs=[out, out, out],
        out_shape=[jax.ShapeDtypeStruct((S, AW), F32), jax.ShapeDtypeStruct((S, AW), F32),
                   jax.ShapeDtypeStruct((S, AW), BF16)],
        scratch_shapes=[pltpu.VMEM((2 * ATT, HP), F32), pltpu.VMEM((2 * ATT, HP), F32),
                        pltpu.VMEM((3 * ATT, HP), F32), pltpu.VMEM((3 * ATT, HP), F32)],
        compiler_params=_cp(),
    )(proj, proj, proj, proj, biases)


def _attn_bwd_call(proj, dattn, lse, dsum, biases):
    S = proj.shape[0]
    nt = S // ATT

    def body(q_ref, k_ref, v_ref, do_ref, l_ref, ds_ref, b_ref, dq_ref, dk_ref, dv_ref, db_ref,
             kc, vc, dkc, dvc, dqa):
        i = pl.program_id(1)

        @pl.when(i == 0)
        def _():
            kc[ATT:2 * ATT] = jnp.zeros((ATT, HP), F32)
            vc[ATT:2 * ATT] = jnp.zeros((ATT, HP), F32)
            dkc[ATT:2 * ATT] = jnp.zeros((ATT, HP), F32)
            dvc[ATT:2 * ATT] = jnp.zeros((ATT, HP), F32)
            db_ref[...] = jnp.zeros_like(db_ref)

        @pl.when(i < nt)
        def _():
            kc[0:ATT] = kc[ATT:2 * ATT]
            vc[0:ATT] = vc[ATT:2 * ATT]
            dkc[0:ATT] = dkc[ATT:2 * ATT]
            dvc[0:ATT] = dvc[ATT:2 * ATT]
            kc[ATT:2 * ATT] = k_ref[...]
            vc[ATT:2 * ATT] = v_ref[...]
            dkc[ATT:2 * ATT] = jnp.zeros((ATT, HP), F32)
            dvc[ATT:2 * ATT] = jnp.zeros((ATT, HP), F32)
            col = lax.broadcasted_iota(jnp.int32, (ABLK, 2 * ABLK), 1)
            dead = jnp.logical_and(i == 0, col < ABLK)
            blocks = _attn_blocks()
            hs = (slice(0, HE), slice(HE, 2 * HE))
            for g0 in range(0, len(blocks), AGRP_B):
                grp = blocks[g0:g0 + AGRP_B]
                nb_ = range(len(grp))
                qrows = [pl.ds(n * ABLK * d + r, ABLK, stride=d) for p, d, r, n in grp]
                krows = [pl.ds(ATT + (n - 1) * ABLK * d + r, 2 * ABLK, stride=d) for p, d, r, n in grp]
                qs = [(q_ref[qr, :] * QSCALE2).astype(BF16) for qr in qrows]
                ks = [kc[kr, :].astype(BF16) for kr in krows]
                vs = [vc[kr, :].astype(BF16) for kr in krows]
                dos = [do_ref[qr, :].astype(BF16) for qr in qrows]
                lvs = [l_ref[qr, :] for qr in qrows]
                dsvs = [ds_ref[qr, :] for qr in qrows]
                ss = [[_nt(qs[b][:, sl], ks[b][:, sl]) + b_ref[grp[b][0], e] for e, sl in enumerate(hs)] for b in nb_]
                ss = [[jnp.where(dead, NEG, s) if grp[b][3] == 0 else s for s in ss[b]] for b in nb_]
                dps = [[_nt(dos[b][:, sl], vs[b][:, sl]) for sl in hs] for b in nb_]
                pes = [[jnp.exp2(ss[b][e] - lvs[b][:, e * HE:e * HE + 1]) for e in range(2)] for b in nb_]
                dscs = [[pes[b][e] * (dps[b][e] - dsvs[b][:, e * HE:e * HE + 1]) for e in range(2)] for b in nb_]
                for e in range(2):
                    tot = dscs[0][e]
                    for b in range(1, len(grp)):
                        tot = tot + dscs[b][e]
                    db_ref[grp[0][0], e] += tot
                dsbs = [[t.astype(BF16) for t in tb] for tb in dscs]
                dqs = [[_nn(dsbs[b][e], ks[b][:, sl]) * (HE ** -0.5) for e, sl in enumerate(hs)] for b in nb_]
                dks = [[_tn(dsbs[b][e], qs[b][:, sl]) for e, sl in enumerate(hs)] for b in nb_]
                dvs = [[_tn(pes[b][e].astype(BF16), dos[b][:, sl]) for e, sl in enumerate(hs)] for b in nb_]
                for b in nb_:
                    dq = jnp.concatenate(dqs[b], axis=1)
                    if grp[b][0] == 0:
                        dqa[qrows[b], :] = dq
                    else:
                        dqa[qrows[b], :] += dq
                    dkc[krows[b], :] += jnp.concatenate(dks[b], axis=1)
                    dvc[krows[b], :] += jnp.concatenate(dvs[b], axis=1)
            dq_ref[...] = dqa[...].astype(BF16)
            dk_ref[...] = (dkc[0:ATT] * LN2).astype(BF16)
            dv_ref[...] = dvc[0:ATT].astype(BF16)

        @pl.when(i == nt)
        def _():
            dk_ref[...] = (dkc[ATT:2 * ATT] * LN2).astype(BF16)
            dv_ref[...] = dvc[ATT:2 * ATT].astype(BF16)

    def pcol(c):
        return pl.BlockSpec((ATT, HP), lambda h, i: (jnp.minimum(i, nt - 1), c * 4 + h))

    qrow = pl.BlockSpec((ATT, HP), lambda h, i: (jnp.minimum(i, nt - 1), h))
    krow = pl.BlockSpec((ATT, HP), lambda h, i: (jnp.maximum(i - 1, 0), h))
    bspec = pl.BlockSpec((3, 2, ABLK, 2 * ABLK), lambda h, i: (0, h, 0, 0))
    return pl.pallas_call(
        body, name="attn_bwd", grid=(4, nt + 1),
        in_specs=[pcol(0), pcol(1), pcol(2), qrow, qrow, qrow, bspec],
        out_specs=[qrow, krow, krow, bspec],
        out_shape=[jax.ShapeDtypeStruct((S, AW), BF16)] * 3
                  + [jax.ShapeDtypeStruct((3, NH, ABLK, 2 * ABLK), F32)],
        scratch_shapes=[pltpu.VMEM((2 * ATT, HP), F32)] * 4 + [pltpu.VMEM((ATT, HP), F32)],
        compiler_params=_cp(),
    )(proj, proj, proj, dattn, lse, dsum, biases)


HRB = 512


def _tri_masks():
    row = lax.broadcasted_iota(jnp.int32, (CH, CH), 0)
    col = lax.broadcasted_iota(jnp.int32, (CH, CH), 1)
    return row >= col


def _heads():
    return [slice(hh * HK, (hh + 1) * HK) for hh in range(NH)]


def _hgrn_gate_heads(q_ref, f_ref, rows, lbv):
    out = []
    for sl in _heads():
        qraw = q_ref[rows, sl]
        sq = _sig(qraw)
        sf = _sig(f_ref[rows, sl])
        f = lbv[:, sl] + (1.0 - lbv[:, sl]) * sf
        out.append(dict(qraw=qraw, sq=sq, q=qraw * sq, sf=sf, f=f, k=1.0 - f, lg=jnp.log(f)))
    return out


def _hgrn_decay_heads(gh, b):
    bl = b[CH - 1:CH, :]
    bm = b[CH // 2 - 1:CH // 2, :]
    ebm = jnp.exp(bm)
    eblm = jnp.exp(bl - bm)
    ebl = jnp.exp(bl)
    out = []
    for g, sl in zip(gh, _heads()):
        d = b[:, sl] - bm[:, sl]
        e1 = jnp.exp(d)
        e2 = jnp.exp(-d)
        qs = g["q"] * e1
        ks = g["k"] * e2
        qe = qs * ebm[:, sl]
        kd = ks * eblm[:, sl]
        out.append(dict(e1=e1, e2=e2, qe=qe, kd=kd, ebm=ebm[:, sl], eblm=eblm[:, sl], ebl=ebl[:, sl],
                        qsb=qs.astype(BF16), ksb=ks.astype(BF16), qeb=qe.astype(BF16), kdb=kd.astype(BF16)))
    return out


def _hgrn_fwd_call(proj, lb, gn):
    S = proj.shape[0]
    nc = S // CH
    cps = HRB // CH

    def body(q_ref, f_ref, i_ref, z_ref, lb_ref, gn_ref, or_ref, ob_ref, st_ref, st):
        @pl.when(pl.program_id(0) == 0)
        def _():
            st[...] = jnp.zeros_like(st)

        low = _tri_masks()
        tri = low.astype(BF16)
        lbv = lb_ref[...]
        hs = _heads()
        for ci in range(cps):
            rows = slice(ci * CH, (ci + 1) * CH)
            gh = _hgrn_gate_heads(q_ref, f_ref, rows, lbv)
            b = _exact_mm(tri, jnp.concatenate([g["lg"] for g in gh], axis=1))
            dh = _hgrn_decay_heads(gh, b)
            vbs = [i_ref[rows, sl].astype(BF16) for sl in hs]
            st_ref[ci] = st[...]
            s0s = [st[sl, :] for sl in hs]
            as_ = [_nt(d["qsb"], d["ksb"]) for d in dh]
            ois = [_nt(d["qeb"], s0.astype(BF16)) for d, s0 in zip(dh, s0s)]
            sts = [_tn(vb, d["kdb"]) for vb, d in zip(vbs, dh)]
            abs_ = [jnp.where(low, a, 0.0).astype(BF16) for a in as_]
            os_ = [oi + _nn(a, vb) for oi, a, vb in zip(ois, abs_, vbs)]
            for sl, s0, sn, d, o in zip(hs, s0s, sts, dh, os_):
                st[sl, :] = s0 * d["ebl"] + sn
                or_ref[rows, sl] = o
                r = lax.rsqrt(jnp.mean(o * o, axis=-1, keepdims=True) + EPS)
                z = z_ref[rows, sl]
                ob_ref[rows, sl] = (o * r * gn_ref[:, sl] * (z * _sig(z))).astype(BF16)

    def pcol(c):
        return pl.BlockSpec((HRB, D), lambda i: (i, c))

    vec = pl.BlockSpec((1, D), lambda i: (0, 0))
    row = pl.BlockSpec((HRB, D), lambda i: (i, 0))
    return pl.pallas_call(
        body, name="hgrn_fwd", grid=(S // HRB,),
        in_specs=[pcol(2), pcol(3), pcol(4), pcol(5), vec, vec],
        out_specs=[row, row, pl.BlockSpec((cps, NH * HK, HK), lambda i: (i, 0, 0))],
        out_shape=[jax.ShapeDtypeStruct((S, D), F32), jax.ShapeDtypeStruct((S, D), BF16),
                   jax.ShapeDtypeStruct((nc, NH * HK, HK), F32)],
        scratch_shapes=[pltpu.VMEM((NH * HK, HK), F32)],
        compiler_params=_cp(),
    )(proj, proj, proj, proj, lb, gn)


def _hgrn_bwd_call(proj, oraw, dob, states, lb, gn):
    S = proj.shape[0]
    nblk = S // HRB
    cps = HRB // CH

    def body(q_ref, f_ref, i_ref, z_ref, or_ref, dob_ref, st_ref, lb_ref, gn_ref, dh_ref, acc_ref, dst):
        @pl.when(pl.program_id(0) == 0)
        def _():
            dst[...] = jnp.zeros_like(dst)
            acc_ref[...] = jnp.zeros_like(acc_ref)

        low = _tri_masks()
        tri = low.astype(BF16)
        triu = jnp.logical_not(_tri_masks()) | (lax.broadcasted_iota(jnp.int32, (CH, CH), 0)
                                               == lax.broadcasted_iota(jnp.int32, (CH, CH), 1))
        triu = triu.astype(BF16)
        lbv = lb_ref[...]
        hs = _heads()
        for ci in reversed(range(cps)):
            rows = slice(ci * CH, (ci + 1) * CH)
            dobs, dgns = [], []
            for sl in hs:
                o = or_ref[rows, sl]
                z = z_ref[rows, sl]
                sz = _sig(z)
                gnv = gn_ref[:, sl]
                dobv = dob_ref[rows, sl]
                r = lax.rsqrt(jnp.mean(o * o, axis=-1, keepdims=True) + EPS)
                onr = o * r
                don = dobv * (z * sz)
                dh_ref[rows, 3 * D + sl.start:3 * D + sl.stop] = (
                    dobv * (onr * gnv) * (sz * (1.0 + z * (1.0 - sz)))).astype(BF16)
                dgns.append(jnp.sum(don * onr, axis=0, keepdims=True))
                gh_ = don * gnv
                dobs.append((r * (gh_ - onr * jnp.mean(gh_ * onr, axis=-1, keepdims=True))).astype(BF16))
            acc_ref[1:2, :] += jnp.concatenate(dgns, axis=1)

            gh = _hgrn_gate_heads(q_ref, f_ref, rows, lbv)
            b = _exact_mm(tri, jnp.concatenate([g["lg"] for g in gh], axis=1))
            dh = _hgrn_decay_heads(gh, b)
            vbs = [i_ref[rows, sl].astype(BF16) for sl in hs]

            st0s = [st_ref[ci, sl, :] for sl in hs]
            dst1s = [dst[sl, :] for sl in hs]
            dst1bs = [t.astype(BF16) for t in dst1s]
            as_ = [_nt(d["qsb"], d["ksb"]) for d in dh]
            das_ = [_nt(do, vb) for do, vb in zip(dobs, vbs)]
            dqes = [_nn(do, s0.astype(BF16)) for do, s0 in zip(dobs, st0s)]
            dkds = [_nn(vb, d1) for vb, d1 in zip(vbs, dst1bs)]
            dvis = [_nt(d["kdb"], d1) for d, d1 in zip(dh, dst1bs)]
            dsts = [_tn(do, d["qeb"]) for do, d in zip(dobs, dh)]
            abs_ = [jnp.where(low, a, 0.0).astype(BF16) for a in as_]
            dabs_ = [jnp.where(low, a, 0.0).astype(BF16) for a in das_]
            dqss = [_nn(da, d["ksb"]) for da, d in zip(dabs_, dh)]
            dkss = [_tn(da, d["qsb"]) for da, d in zip(dabs_, dh)]
            dvs_ = [_tn(a, do) + dvi for a, do, dvi in zip(abs_, dobs, dvis)]

            dqs_, dks_, dbs_, exs_ = [], [], [], []
            for hh, sl in enumerate(hs):
                d, d1, s0 = dh[hh], dst1s[hh], st0s[hh]
                dqe, dqs, dks, dkd = dqes[hh], dqss[hh], dkss[hh], dkds[hh]
                dst[sl, :] = dsts[hh] + d1 * d["ebl"]
                dh_ref[rows, 2 * D + sl.start:2 * D + sl.stop] = dvs_[hh].astype(BF16)
                dqs_.append((dqe * d["ebm"] + dqs) * d["e1"])
                dks_.append((dks + dkd * d["eblm"]) * d["e2"])
                dkdkd = dkd * d["kd"]
                dbs_.append(dqe * d["qe"] + dqs * d["qsb"].astype(F32) - dks * d["ksb"].astype(F32) - dkdkd)
                exs_.append(jnp.sum(dkdkd, axis=0, keepdims=True)
                            + jnp.sum(d1 * s0, axis=0, keepdims=True) * d["ebl"])
            dg = _exact_mm(triu, jnp.concatenate(dbs_, axis=1)) + jnp.concatenate(exs_, axis=1)

            dlbs = []
            for hh, sl in enumerate(hs):
                g = gh[hh]
                df = dg[:, sl] / g["f"] - dks_[hh]
                sf = g["sf"]
                omsf = 1.0 - sf
                sq = g["sq"]
                dlbs.append(jnp.sum(df * omsf, axis=0, keepdims=True))
                dh_ref[rows, sl] = (dqs_[hh] * (sq * (1.0 + g["qraw"] * (1.0 - sq)))).astype(BF16)
                dh_ref[rows, D + sl.start:D + sl.stop] = (df * (1.0 - lbv[:, sl]) * sf * omsf).astype(BF16)
            acc_ref[0:1, :] += jnp.concatenate(dlbs, axis=1)

    def pcol(c):
        return pl.BlockSpec((HRB, D), lambda i: (nblk - 1 - i, c))

    vec = pl.BlockSpec((1, D), lambda i: (0, 0))
    row = pl.BlockSpec((HRB, D), lambda i: (nblk - 1 - i, 0))
    return pl.pallas_call(
        body, name="hgrn_bwd", grid=(nblk,),
        in_specs=[pcol(2), pcol(3), pcol(4), pcol(5), row, row,
                  pl.BlockSpec((cps, NH * HK, HK), lambda i: (nblk - 1 - i, 0, 0)), vec, vec],
        out_specs=[pl.BlockSpec((HRB, 4 * D), lambda i: (nblk - 1 - i, 0)),
                   pl.BlockSpec((8, D), lambda i: (0, 0))],
        out_shape=[jax.ShapeDtypeStruct((S, 4 * D), BF16), jax.ShapeDtypeStruct((8, D), F32)],
        scratch_shapes=[pltpu.VMEM((NH * HK, HK), F32)],
        compiler_params=_cp(),
    )(proj, proj, proj, proj, oraw, dob, states, lb, gn)


def _merge_call(oa, ob, proj, x, tgt, vecs, attn, wa, wb, wo, hsum):
    S = x.shape[0]
    tm = 256

    def body(oa_ref, ob_ref, ga_ref, gb_ref, x_ref, t_ref, v_ref, at_ref, za_ref, wa_ref, wb_ref, wo_ref, hs_ref,
             y_ref, dx2_ref, du_ref, dya_ref, dyb_ref, dg_ref, dat_ref, dsum_ref, dza_ref, dob_ref, acc_ref, ls_ref):
        @pl.when(pl.program_id(0) == 0)
        def _():
            acc_ref[...] = jnp.zeros_like(acc_ref)
            ls_ref[...] = jnp.zeros_like(ls_ref)

        gate = v_ref[0:1, :]
        fg = v_ref[1:2, :]
        ya = _nn(oa_ref[...], wa_ref[...])
        yb = _nn(ob_ref[...], wb_ref[...])
        sa = _sig(ga_ref[...])
        sb = _sig(gb_ref[...])
        y = (sa * ya + sb * yb).astype(BF16)
        y_ref[...] = y
        u = _nn(y, wo_ref[...])
        x2v = x_ref[...] + gate * u
        r = lax.rsqrt(jnp.mean(x2v * x2v, axis=-1, keepdims=True) + EPS)
        err = x2v * r * fg - t_ref[...]
        ls_ref[...] += jnp.sum(err * err)
        dout = err * (1.0 / D)
        gh = dout * fg
        dx2 = r * gh - x2v * (r * r * r * jnp.mean(gh * x2v, axis=-1, keepdims=True))
        acc_ref[0:1, :] += jnp.sum(dx2 * u, axis=0, keepdims=True)
        acc_ref[1:2, :] += jnp.sum(dout * x2v * r, axis=0, keepdims=True)
        dx2_ref[...] = dx2
        du = (dx2 * gate).astype(BF16)
        du_ref[...] = du
        dy = _nt(du, wo_ref[...])
        dya = (dy * sa).astype(BF16)
        dyb = (dy * sb).astype(BF16)
        dya_ref[...] = dya
        dyb_ref[...] = dyb
        dg_ref[:, 0:D] = (dy * ya * sa * (1.0 - sa)).astype(BF16)
        dg_ref[:, D:2 * D] = (dy * yb * sb * (1.0 - sb)).astype(BF16)
        doa = _nt(dya, wa_ref[...])
        dob_ref[...] = _nt(dyb, wb_ref[...])
        za = za_ref[...]
        sz = _sig(za)
        att = at_ref[...]
        dat = doa * (za * sz)
        dat_ref[...] = dat
        dza_ref[...] = (doa * att * (sz * (1.0 + za * (1.0 - sz)))).astype(BF16)
        dsum_ref[...] = _exact_mm_r(dat * att, hs_ref[...])

    row = pl.BlockSpec((tm, D), lambda i: (i, 0))
    arow = pl.BlockSpec((tm, AW), lambda i: (i, 0))
    full = lambda a: pl.BlockSpec(a.shape, lambda i: (0, 0))
    return pl.pallas_call(
        body, name="merge_fwd_bwd", grid=(S // tm,),
        in_specs=[arow, row, pl.BlockSpec((tm, D), lambda i: (i, 6)), pl.BlockSpec((tm, D), lambda i: (i, 7)),
                  row, row, pl.BlockSpec((8, D), lambda i: (0, 0)), arow, pl.BlockSpec((tm, AW), lambda i: (i, 3)),
                  full(wa), full(wb), full(wo), full(hsum)],
        out_specs=[row, row, row, row, row, pl.BlockSpec((tm, 2 * D), lambda i: (i, 0)),
                   arow, arow, arow, row, pl.BlockSpec((8, D), lambda i: (0, 0)),
                   pl.BlockSpec((8, 128), lambda i: (0, 0))],
        out_shape=[jax.ShapeDtypeStruct((S, D), BF16), jax.ShapeDtypeStruct((S, D), F32),
                   jax.ShapeDtypeStruct((S, D), BF16), jax.ShapeDtypeStruct((S, D), BF16),
                   jax.ShapeDtypeStruct((S, D), BF16), jax.ShapeDtypeStruct((S, 2 * D), BF16),
                   jax.ShapeDtypeStruct((S, AW), F32), jax.ShapeDtypeStruct((S, AW), F32),
                   jax.ShapeDtypeStruct((S, AW), BF16), jax.ShapeDtypeStruct((S, D), F32),
                   jax.ShapeDtypeStruct((8, D), F32), jax.ShapeDtypeStruct((8, 128), F32)],
        compiler_params=_cp(),
    )(oa, ob, proj, proj, x, tgt, vecs, attn, proj, wa, wb, wo, hsum)


def _atb_call(a, b, name):
    S, K = a.shape
    N = b.shape[1]
    tm = min(1024, S)

    def body(a_ref, b_ref, o_ref, ob_ref):
        @pl.when(pl.program_id(0) == 0)
        def _():
            o_ref[...] = jnp.zeros_like(o_ref)

        o_ref[...] += _tn(a_ref[...], b_ref[...])

        @pl.when(pl.program_id(0) == S // tm - 1)
        def _():
            ob_ref[...] = o_ref[...].astype(BF16)

    ospec = pl.BlockSpec((K, N), lambda i: (0, 0))
    return pl.pallas_call(
        body, name=name, grid=(S // tm,),
        in_specs=[pl.BlockSpec((tm, K), lambda i: (i, 0)), pl.BlockSpec((tm, N), lambda i: (i, 0))],
        out_specs=[ospec, ospec],
        out_shape=[jax.ShapeDtypeStruct((K, N), F32), jax.ShapeDtypeStruct((K, N), BF16)], compiler_params=_cp(),
    )(a, b)


def _atb3_call(pairs):
    S = pairs[0][0].shape[0]
    tm = 512
    n = len(pairs)
    shapes = [(a.shape[1], b.shape[1]) for a, b in pairs]

    def body(*refs):
        ins, outs = refs[:2 * n], refs[2 * n:]

        @pl.when(pl.program_id(0) == 0)
        def _():
            for u in range(n):
                outs[2 * u][...] = jnp.zeros_like(outs[2 * u])

        for u in range(n):
            outs[2 * u][...] += _tn(ins[2 * u][...], ins[2 * u + 1][...])

        @pl.when(pl.program_id(0) == S // tm - 1)
        def _():
            for u in range(n):
                outs[2 * u + 1][...] = outs[2 * u][...].astype(BF16)

    in_specs, out_specs, out_shape = [], [], []
    for (a, b), (K, N) in zip(pairs, shapes):
        in_specs += [pl.BlockSpec((tm, K), lambda i: (i, 0)), pl.BlockSpec((tm, N), lambda i: (i, 0))]
        out_specs += [pl.BlockSpec((K, N), lambda i: (0, 0))] * 2
        out_shape += [jax.ShapeDtypeStruct((K, N), F32), jax.ShapeDtypeStruct((K, N), BF16)]
    res = pl.pallas_call(
        body, name="dw_small3", grid=(S // tm,), in_specs=in_specs, out_specs=out_specs, out_shape=out_shape,
        compiler_params=_cp(),
    )(*[t for p in pairs for t in p])
    return [(res[2 * u], res[2 * u + 1]) for u in range(n)]


def _dwin_call(h_t, dqkvz, d_hgrn, d_gates):
    S = h_t.shape[1]
    tm = 512
    tn = 2048

    def body(h_ref, q_ref, k_ref, v_ref, z_ref, m_ref, g_ref, o_ref, ob_ref):
        j = pl.program_id(0)

        @pl.when(pl.program_id(1) == 0)
        def _():
            o_ref[...] = jnp.zeros_like(o_ref)

        hv = h_ref[...]

        @pl.when(j == 0)
        def _():
            for cidx, r in enumerate((q_ref, k_ref, v_ref, z_ref)):
                o_ref[0, :, cidx * AW:(cidx + 1) * AW] += _nn(hv, r[...])

        @pl.when(jnp.logical_or(j == 1, j == 2))
        def _():
            o_ref[0] += _nn(hv, m_ref[...])

        @pl.when(j == 3)
        def _():
            o_ref[0] += _nn(hv, g_ref[...])

        @pl.when(pl.program_id(1) == S // tm - 1)
        def _():
            ob_ref[...] = o_ref[...].astype(BF16)

    aspec = pl.BlockSpec((tm, AW), lambda j, i: (jnp.where(j == 0, i, 0), 0))
    ospec = pl.BlockSpec((1, D, tn), lambda j, i: (j, 0, 0))
    return pl.pallas_call(
        body, name="dw_in", grid=(4, S // tm),
        in_specs=[pl.BlockSpec((D, tm), lambda j, i: (0, i)), aspec, aspec, aspec, aspec,
                  pl.BlockSpec((tm, tn), lambda j, i: (jnp.where(jnp.logical_or(j == 1, j == 2), i, 0),
                                                       jnp.where(j == 2, 1, 0))),
                  pl.BlockSpec((tm, tn), lambda j, i: (jnp.where(j == 3, i, 0), 0))],
        out_specs=[ospec, ospec],
        out_shape=[jax.ShapeDtypeStruct((4, D, tn), F32), jax.ShapeDtypeStruct((4, D, tn), BF16)],
        compiler_params=_cp(),
    )(h_t, *dqkvz, d_hgrn, d_gates)


def _dh_call(dqkvz, d_hgrn, d_gates, w_in, x, dx2, vecs):
    S = x.shape[0]
    tm = 512

    def body(q_ref, k_ref, v_ref, z_ref, m_ref, g_ref, w_ref, x_ref, dx2_ref, p_ref, gx_ref, acc_ref):
        @pl.when(pl.program_id(0) == 0)
        def _():
            acc_ref[...] = jnp.zeros_like(acc_ref)

        dhv = _nt(q_ref[...], w_ref[:, 0:AW])
        for cidx, r in enumerate((k_ref, v_ref, z_ref)):
            dhv += _nt(r[...], w_ref[:, (cidx + 1) * AW:(cidx + 2) * AW])
        dhv += _nt(m_ref[...], w_ref[:, 4 * AW:4 * AW + 4 * D])
        dhv += _nt(g_ref[...], w_ref[:, 4 * AW + 4 * D:NPROJ])
        xv = x_ref[...]
        r = lax.rsqrt(jnp.mean(xv * xv, axis=-1, keepdims=True) + EPS)
        xn = xv * r
        acc_ref[0:1, :] += jnp.sum(dhv, axis=0, keepdims=True)
        acc_ref[1:2, :] += jnp.sum(dhv * xn * p_ref[1:2, :], axis=0, keepdims=True)
        acc_ref[2:3, :] += jnp.sum(dhv * xn * p_ref[2:3, :], axis=0, keepdims=True)
        dxn = dhv * p_ref[0:1, :]
        gx_ref[...] = dx2_ref[...] + r * dxn - xv * (r * r * r * jnp.mean(dxn * xv, axis=-1, keepdims=True))

    row = pl.BlockSpec((tm, D), lambda i: (i, 0))
    aspec = pl.BlockSpec((tm, AW), lambda i: (i, 0))
    const = lambda shape: pl.BlockSpec(shape, lambda i: (0, 0))
    return pl.pallas_call(
        body, name="dh_gradx", grid=(S // tm,),
        in_specs=[aspec, aspec, aspec, aspec,
                  pl.BlockSpec((tm, 4 * D), lambda i: (i, 0)), pl.BlockSpec((tm, 2 * D), lambda i: (i, 0)),
                  pl.BlockSpec((D, NPROJ), lambda i: (0, 0), pipeline_mode=pl.Buffered(1)),
                  row, row, const((8, D))],
        out_specs=[row, const((8, D))],
        out_shape=[jax.ShapeDtypeStruct((S, D), F32), jax.ShapeDtypeStruct((8, D), F32)],
        compiler_params=_cp(),
    )(*dqkvz, d_hgrn, d_gates, w_in, x, dx2, vecs)


def _adamw_math(w, g, m, v):
    m = B1 * m + (1.0 - B1) * g
    v = B2 * v + (1.0 - B2) * (g * g)
    m_hat = m / (1.0 - B1 ** STEP)
    v_hat = v / (1.0 - B2 ** STEP)
    delta = -LR * (m_hat / (jnp.sqrt(v_hat) + AEPS) + WD * w)
    return delta, m, v


def _adamw_call(w, g, m, v, name):
    R, C = w.shape
    tr = R if R * C * 4 <= (1 << 20) else max(8, (1 << 20) // (C * 4))
    assert R % tr == 0

    def body(w_ref, g_ref, m_ref, v_ref, go_ref, d_ref, nm_ref, nv_ref):
        g = g_ref[...]
        go_ref[...] = g
        d_ref[...], nm_ref[...], nv_ref[...] = _adamw_math(w_ref[...], g, m_ref[...], v_ref[...])

    blk = pl.BlockSpec((tr, C), lambda i: (i, 0))
    return pl.pallas_call(
        body, name=name, grid=(R // tr,), in_specs=[blk] * 4, out_specs=[blk] * 4,
        out_shape=[jax.ShapeDtypeStruct((R, C), F32)] * 4, compiler_params=_cp(),
    )(w, g, m, v)


def _adamw_many_call(items, name):
    n = len(items)

    def body(*refs):
        ins, outs = refs[:4 * n], refs[4 * n:]
        for u in range(n):
            w_ref, g_ref, m_ref, v_ref = ins[4 * u:4 * u + 4]
            g = g_ref[...]
            outs[4 * u][...] = g
            outs[4 * u + 1][...], outs[4 * u + 2][...], outs[4 * u + 3][...] = _adamw_math(
                w_ref[...], g, m_ref[...], v_ref[...])

    flat = [t for it in items for t in it]
    out_shape = [jax.ShapeDtypeStruct(it[0].shape, F32) for it in items for _ in range(4)]
    res = pl.pallas_call(body, name=name, out_shape=out_shape, compiler_params=_cp())(*flat)
    return [tuple(res[4 * u:4 * u + 4]) for u in range(n)]


def _mod_call(c_all, w_ada_s, b_s):
    def body(c_ref, w_ref, b_ref, o_ref):
        cv = c_ref[...]
        sc = cv * _sig(cv)
        o_ref[...] = jnp.dot(sc, w_ref[...], preferred_element_type=F32,
                             precision=lax.Precision.HIGHEST) + b_ref[...]

    return pl.pallas_call(
        body, name="ada_mod", out_shape=jax.ShapeDtypeStruct((8, w_ada_s.shape[1]), F32),
        compiler_params=_cp(),
    )(c_all, w_ada_s, b_s)


def _ada_update_call(sct, dm, w, m, v):
    R, C = w.shape
    tr = 256

    def body(s_ref, d_ref, w_ref, m_ref, v_ref, g_ref, dl_ref, nm_ref, nv_ref):
        g = s_ref[:, 0:1] * d_ref[0:1, :]
        for b in range(1, 8):
            g = g + s_ref[:, b:b + 1] * d_ref[b:b + 1, :]
        g_ref[...] = g
        dl_ref[...], nm_ref[...], nv_ref[...] = _adamw_math(w_ref[...], g, m_ref[...], v_ref[...])

    blk = pl.BlockSpec((tr, C), lambda i: (i, 0))
    return pl.pallas_call(
        body, name="ada_update", grid=(R // tr,),
        in_specs=[pl.BlockSpec((tr, 8), lambda i: (i, 0)), pl.BlockSpec((8, C), lambda i: (0, 0)), blk, blk, blk],
        out_specs=[blk] * 4, out_shape=[jax.ShapeDtypeStruct((R, C), F32)] * 4, compiler_params=_cp(),
    )(sct, dm, w, m, v)


def _sum8_call(packs):
    def body(p_ref, o_ref):
        acc = p_ref[0]
        for k in range(1, 8):
            acc = acc + p_ref[k]
        o_ref[...] = acc

    return pl.pallas_call(
        body, name="sum_small", out_shape=jax.ShapeDtypeStruct(packs.shape[1:], F32), compiler_params=_cp(),
    )(packs)


def _local_step(x, tgt, shift, scale, gate, norm_g, hgrn_onorm_g, rel_bias, lb, final_g, weights_fn, hook=None):
    a = norm_g * (1.0 + scale)
    z6 = jnp.zeros((6, D), F32)
    h, h_t, proj, w_in, rest_fn = weights_fn(x, jnp.concatenate([a, shift, z6], 0))

    biases = _bias_tiles(rel_bias) * LOG2E
    attn, lse, oa = _attn_fwd_call(proj, biases)

    gn = jnp.tile(hgrn_onorm_g, (1, NH))
    oraw, ob, states = _hgrn_fwd_call(proj, lb, gn)

    wa, wb, wo = rest_fn(ob)
    vecs2 = jnp.concatenate([gate, final_g, z6], 0)
    hsum = jnp.asarray(np.kron(np.eye(NH), np.ones((HE, HE))), BF16)
    y, dx2, du, dya, dyb, d_gates, dattn, dsum, dza, dob, acc2, lsq = _merge_call(
        oa, ob, proj, x, tgt, vecs2, attn, wa, wb, wo, hsum)
    loss = 0.5 * lsq[0, 0] / D
    (d_wo, d_wo16), (d_wa, d_wa16), (d_wb, d_wb16) = _atb3_call([(y, du), (oa, dya), (ob, dyb)])

    d_hgrn, acch = _hgrn_bwd_call(proj, oraw, dob, states, lb, gn)

    dq, dk, dv, dbs = _attn_bwd_call(proj, dattn, lse, dsum, biases)
    dqkvz = (dq, dk, dv, dza)

    d_win, d_win16 = _dwin_call(h_t, dqkvz, d_hgrn, d_gates)
    tok = hook((d_win, d_wa, d_wb, d_wo), (d_win16, d_wa16, d_wb16, d_wo16)) if hook is not None else 0.0
    one_scale = 1.0 + scale
    grad_x, acc1 = _dh_call(dqkvz, d_hgrn, d_gates, w_in, x, dx2,
                            jnp.concatenate([a + tok, norm_g, one_scale, jnp.zeros((5, D), F32)], 0))

    d_rel = jnp.zeros((NBUCKETS, NH), F32)
    for p, d in enumerate(PATTERNS):
        band, bucket = _band_bucket(d)
        onehot = (bucket[None] == np.arange(NBUCKETS)[:, None, None]) & band[None]
        d_rel = d_rel + jnp.einsum("hqk,bqk->bh", dbs[p], jnp.asarray(onehot, F32),
                                   precision=lax.Precision.HIGHEST)
    d_onorm = jnp.sum(acch[1].reshape(NH, HK), axis=0)

    zrow = jnp.zeros((D,), F32)
    pack = jnp.stack([acc1[0], acc1[1], acc2[0], acc1[2], acc2[1], acch[0],
                      zrow.at[:HK].set(d_onorm), zrow.at[0].set(loss),
                      zrow.at[:NBUCKETS * NH].set(d_rel.reshape(-1))] + [zrow] * 7, 0)
    return grad_x, d_win, d_wa, d_wb, d_wo, pack


def _me():
    return lax.axis_index("x"), lax.axis_index("y"), lax.axis_index("c")


def _peers(x, y):
    return [(1 - x, y), (x, 1 - y), (1 - x, 1 - y)]


def _allgather_small(blk, name):
    m_per, n = blk.shape

    def body(x_ref, out_ref, send_sems, recv_sems, local_sem):
        x, y, c = _me()
        me, sibling = (x, y, c), (x, y, 1 - c)
        chips = _peers(x, y)

        def rows(px, py, pc):
            return out_ref.at[pl.ds((4 * px + 2 * py + pc) * m_per, m_per), :]

        def copy(k, block, to, src=None):
            return pltpu.make_async_remote_copy(
                src_ref=rows(*block) if src is None else src, dst_ref=rows(*block),
                send_sem=send_sems.at[k], recv_sem=recv_sems.at[k], device_id=to, device_id_type=MESH)

        mine = pltpu.make_async_copy(x_ref, rows(*me), local_sem)
        mine.start()
        first = [copy(0, me, sibling, src=x_ref)]
        first += [copy(1 + j, me, (*chip, c), src=x_ref) for j, chip in enumerate(chips)]
        for cp in first:
            cp.start()
        passed = [copy(4 + j, (*chip, c), sibling) for j, chip in enumerate(chips)]
        for j, chip in enumerate(chips):
            copy(1 + j, (*chip, c), me).wait_recv()
            passed[j].start()
        copy(0, sibling, me).wait_recv()
        for j, chip in enumerate(chips):
            copy(4 + j, (*chip, 1 - c), me).wait_recv()
        for cp in first + passed:
            cp.wait_send()
        mine.wait()

    return pl.pallas_call(
        body, name=name, out_shape=jax.ShapeDtypeStruct((8 * m_per, n), blk.dtype),
        in_specs=[pl.BlockSpec(memory_space=pltpu.VMEM)], out_specs=pl.BlockSpec(memory_space=pltpu.VMEM),
        scratch_shapes=[pltpu.SemaphoreType.DMA((7,)), pltpu.SemaphoreType.DMA((7,)), pltpu.SemaphoreType.DMA],
    )(blk)


ANY = pl.BlockSpec(memory_space=pl.ANY)


HBM = pl.BlockSpec(memory_space=pltpu.HBM)
SEM = pl.BlockSpec(memory_space=pltpu.SEMAPHORE)
EFFECT = pltpu.SideEffectType.DATAFLOW_SIDE_EFFECTING


def _w_part(t, ref, j, half):
    if t == 0:
        return ref.at[pl.ds(half * (D // 2), D // 2), pl.ds(j * 2048, 2048)]
    if t == 1:
        return ref.at[pl.ds(half * (AW // 2), AW // 2), pl.ds(j * 256, 256)]
    return ref.at[pl.ds(j * 256 + half * 128, 128), :]


def _w_ici_copies(fulls, send_sems, recv_sems, tensors=(0, 1, 2, 3)):
    x, y, c = _me()
    outs, ins = [], []
    for u, t in enumerate(tensors):
        for k, chip in enumerate(_peers(x, y)):
            mine = _w_part(t, fulls[u], 2 * x + y, c)
            theirs = _w_part(t, fulls[u], 2 * chip[0] + chip[1], c)
            kw = dict(send_sem=send_sems.at[3 * t + k], recv_sem=recv_sems.at[3 * t + k],
                      device_id=(*chip, c), device_id_type=MESH)
            outs.append(pltpu.make_async_remote_copy(src_ref=mine, dst_ref=mine, **kw))
            ins.append(pltpu.make_async_remote_copy(src_ref=theirs, dst_ref=theirs, **kw))
    return outs, ins


def _gather_ici_start(fulls):
    def body(f0, f1, f2, f3, send_sems, recv_sems, t0, t1, t2, t3, token):
        for cp in _w_ici_copies([f0, f1, f2, f3], send_sems, recv_sems)[0]:
            cp.start()
        token[...] = jnp.zeros_like(token)

    res = pl.pallas_call(
        body, name="gather_ici_start",
        out_shape=(pltpu.SemaphoreType.DMA((12,)), pltpu.SemaphoreType.DMA((12,)),
                   *[pltpu.HBM(a.shape, a.dtype) for a in fulls], jax.ShapeDtypeStruct((8, 128), F32)),
        in_specs=[HBM] * 4, out_specs=(SEM, SEM, *[HBM] * 4, pl.BlockSpec(memory_space=pltpu.VMEM)),
        input_output_aliases={i: 2 + i for i in range(4)},
        compiler_params=pltpu.CompilerParams(has_side_effects=EFFECT),
    )(*[pltpu.with_memory_space_constraint(a, pltpu.HBM) for a in fulls])
    return res[0], res[1], list(res[2:6]), res[6]


def _gather_ici_wait(send_sems, recv_sems, bufs, after, tensors, name):
    n = len(tensors)

    def body(*refs):
        outs, ins = _w_ici_copies(refs[0:n], refs[n], refs[n + 1], tensors)
        for cp in outs:
            cp.wait_send()
        for cp in ins:
            cp.wait_recv()

    return pl.pallas_call(
        body, name=name, out_shape=tuple(pltpu.HBM(a.shape, a.dtype) for a in bufs),
        in_specs=[HBM] * n + [SEM, SEM, ANY], out_specs=[HBM] * n,
        input_output_aliases={i: i for i in range(n)},
        compiler_params=pltpu.CompilerParams(has_side_effects=EFFECT),
    )(*bufs, send_sems, recv_sems, after)


def _gather_sibling(bufs, tensors, name):
    n = len(tensors)

    def body(*refs):
        outs, send_sems, recv_sems = refs[n:2 * n], refs[2 * n], refs[2 * n + 1]
        x, y, c = _me()
        cps = []
        for u, t in enumerate(tensors):
            for k, chip in enumerate(_peers(x, y)):
                blk = _w_part(t, outs[u], 2 * chip[0] + chip[1], c)
                cp = pltpu.make_async_remote_copy(
                    src_ref=blk, dst_ref=blk, send_sem=send_sems.at[u, k], recv_sem=recv_sems.at[u, k],
                    device_id=(x, y, 1 - c), device_id_type=MESH)
                cp.start()
                cps.append(cp)
        for u, t in enumerate(tensors):
            for k, chip in enumerate(_peers(x, y)):
                blk = _w_part(t, outs[u], 2 * chip[0] + chip[1], 1 - c)
                pltpu.make_async_remote_copy(
                    src_ref=blk, dst_ref=blk, send_sem=send_sems.at[u, k], recv_sem=recv_sems.at[u, k],
                    device_id=(x, y, 1 - c), device_id_type=MESH).wait_recv()
        for cp in cps:
            cp.wait_send()

    return pl.pallas_call(
        body, name=name,
        out_shape=[jax.ShapeDtypeStruct(FULL_W_SHAPES[t], BF16) for t in tensors],
        in_specs=[ANY] * n, out_specs=[ANY] * n, input_output_aliases={u: u for u in range(n)},
        scratch_shapes=[pltpu.SemaphoreType.DMA((n, 3)), pltpu.SemaphoreType.DMA((n, 3))],
    )(*bufs)


def _half_of(t, ref, half):
    if t == 0:
        return ref.at[:, pl.ds(half * 512, 512), :]
    if t == 1:
        return ref.at[pl.ds(half * 256, 256), :]
    return ref.at[:, pl.ds(half * 512, 512)]


HALF_SHAPES = [(4, 512, 2048), (256, D), (D, 512), (D, 512)]
PIECE_SHAPES = [(512, 2048), (256, 256), (256, 512), (256, 512)]
SHARD_SHAPES = [(D, 2048), (AW, 256), (256, D), (256, D)]


def _chip_piece(t, ref, j):
    if t == 0:
        return ref.at[j]
    if t == 1:
        return ref.at[:, pl.ds(j * 256, 256)]
    return ref.at[pl.ds(j * 256, 256), :]


def _reduce_sibling_send(gs):
    def body(g0, g1, g2, g3, r0, r1, r2, r3, send_sems, recv_sems):
        x, y, c = _me()
        ins, outs = [g0, g1, g2, g3], [r0, r1, r2, r3]
        cps = []
        for t in range(4):
            cp = pltpu.make_async_remote_copy(
                src_ref=_half_of(t, ins[t], 1 - c), dst_ref=outs[t],
                send_sem=send_sems.at[t], recv_sem=recv_sems.at[t], device_id=(x, y, 1 - c), device_id_type=MESH)
            cp.start()
            cps.append(cp)
        for cp in cps:
            cp.wait_recv()
        for cp in cps:
            cp.wait_send()

    return pl.pallas_call(
        body, name="reduce_sibling", out_shape=[jax.ShapeDtypeStruct(s, BF16) for s in HALF_SHAPES],
        in_specs=[ANY] * 4, out_specs=[ANY] * 4,
        scratch_shapes=[pltpu.SemaphoreType.DMA((4,)), pltpu.SemaphoreType.DMA((4,))],
    )(*gs)


def _chip_copies(hs, lands, send_sems, recv_sems):
    x, y, c = _me()
    cps = []
    for t in range(4):
        for k, chip in enumerate(_peers(x, y)):
            pj = 2 * chip[0] + chip[1]
            cps.append(pltpu.make_async_remote_copy(
                src_ref=_chip_piece(t, hs[t], pj), dst_ref=lands[t].at[k],
                send_sem=send_sems.at[3 * t + k], recv_sem=recv_sems.at[3 * t + k],
                device_id=(*chip, c), device_id_type=MESH))
    return cps


def _reduce_chips_start(hs):
    lands = [lax.empty((3,) + s, BF16) for s in PIECE_SHAPES]

    def body(h0, h1, h2, h3, l0, l1, l2, l3, send_sems, recv_sems, t0, t1, t2, t3, t4, t5, t6, t7, token):
        for cp in _chip_copies([h0, h1, h2, h3], [l0, l1, l2, l3], send_sems, recv_sems):
            cp.start()
        token[...] = jnp.zeros_like(token)

    bufs = list(hs) + lands
    res = pl.pallas_call(
        body, name="reduce_chips_start",
        out_shape=(pltpu.SemaphoreType.DMA((12,)), pltpu.SemaphoreType.DMA((12,)),
                   *[pltpu.HBM(a.shape, a.dtype) for a in bufs], jax.ShapeDtypeStruct((8, 128), F32)),
        in_specs=[HBM] * 8, out_specs=(SEM, SEM, *[HBM] * 8, pl.BlockSpec(memory_space=pltpu.VMEM)),
        input_output_aliases={i: 2 + i for i in range(8)},
        compiler_params=pltpu.CompilerParams(has_side_effects=EFFECT),
    )(*[pltpu.with_memory_space_constraint(a, pltpu.HBM) for a in bufs])
    return res[0], res[1], list(res[2:10]), res[10]


def _reduce_chips_wait(send_sems, recv_sems, thru, after):
    def body(h0, h1, h2, h3, l0, l1, l2, l3, send_sems, recv_sems, after_ref, d0, d1, d2, d3, g0, g1, g2, g3):
        cps = _chip_copies([h0, h1, h2, h3], [l0, l1, l2, l3], send_sems, recv_sems)
        for cp in cps:
            cp.wait_send()
        for cp in cps:
            cp.wait_recv()

    res = pl.pallas_call(
        body, name="reduce_chips_wait", out_shape=tuple(pltpu.HBM(a.shape, a.dtype) for a in thru),
        in_specs=[HBM] * 8 + [SEM, SEM, ANY], out_specs=[HBM] * 8,
        input_output_aliases={i: i for i in range(8)},
        compiler_params=pltpu.CompilerParams(has_side_effects=EFFECT),
    )(*thru, send_sems, recv_sems, after)
    return list(res[4:8])


def _share_sibling(shards):
    def body(i0, i1, i2, i3, o0, o1, o2, o3, send_sems, recv_sems):
        x, y, c = _me()
        outs = [o0, o1, o2, o3]

        def half(t, ref, hf):
            if t == 0:
                return ref.at[pl.ds(hf * 512, 512), :]
            if t == 1:
                return ref.at[pl.ds(hf * 256, 256), :]
            return ref.at[:, pl.ds(hf * 512, 512)]

        cps = []
        for t in range(4):
            mine = half(t, outs[t], c)
            cp = pltpu.make_async_remote_copy(
                src_ref=mine, dst_ref=mine, send_sem=send_sems.at[t], recv_sem=recv_sems.at[t],
                device_id=(x, y, 1 - c), device_id_type=MESH)
            cp.start()
            cps.append(cp)
        for t in range(4):
            theirs = half(t, outs[t], 1 - c)
            pltpu.make_async_remote_copy(
                src_ref=theirs, dst_ref=theirs, send_sem=send_sems.at[t],
                recv_sem=recv_sems.at[t], device_id=(x, y, 1 - c), device_id_type=MESH).wait_recv()
        for cp in cps:
            cp.wait_send()

    return pl.pallas_call(
        body, name="share_sibling", out_shape=[jax.ShapeDtypeStruct(s, F32) for s in SHARD_SHAPES],
        in_specs=[ANY] * 4, out_specs=[ANY] * 4, input_output_aliases={0: 0, 1: 1, 2: 2, 3: 3},
        scratch_shapes=[pltpu.SemaphoreType.DMA((4,)), pltpu.SemaphoreType.DMA((4,))],
    )(*shards)


def _half_blockspec(t, idx_pos):
    if t == 0:
        return pl.BlockSpec((1, 512, 2048), lambda i, s: (i, s[idx_pos], 0)), 4
    if t == 1:
        return pl.BlockSpec((256, D), lambda i, s: (s[idx_pos], 0)), 1
    return pl.BlockSpec((256, 512), lambda i, s: (i, s[idx_pos])), 4


def _half_out_blockspec(t):
    if t == 0:
        return pl.BlockSpec((1, 512, 2048), lambda i, s: (i, 0, 0))
    if t == 1:
        return pl.BlockSpec((256, D), lambda i, s: (0, 0))
    return pl.BlockSpec((256, 512), lambda i, s: (i, 0))


def _add_half_call(t, own, recv, sc, name):
    in_blk, steps = _half_blockspec(t, 0)
    out_blk = _half_out_blockspec(t)

    def body(s_ref, a_ref, b_ref, o_ref, ob_ref):
        v = a_ref[...] + b_ref[...].astype(F32)
        o_ref[...] = v
        ob_ref[...] = v.astype(BF16)

    return pl.pallas_call(
        body, name=name,
        grid_spec=pltpu.PrefetchScalarGridSpec(
            num_scalar_prefetch=1, grid=(steps,), in_specs=[in_blk, out_blk], out_specs=[out_blk, out_blk]),
        out_shape=[jax.ShapeDtypeStruct(HALF_SHAPES[t], F32), jax.ShapeDtypeStruct(HALF_SHAPES[t], BF16)],
        compiler_params=_cp(),
    )(sc, own, recv)


def _final_piece_call(t, chipsum, recv3, sc, name):
    ps = PIECE_SHAPES[t]
    if t == 0:
        own_blk = pl.BlockSpec((1,) + ps, lambda i, s: (s[1], 0, 0))
        o_blk = pl.BlockSpec(ps, lambda i, s: (s[0], 0))
    elif t == 1:
        own_blk = pl.BlockSpec(ps, lambda i, s: (0, s[1]))
        o_blk = pl.BlockSpec(ps, lambda i, s: (s[0], 0))
    else:
        own_blk = pl.BlockSpec(ps, lambda i, s: (s[1], 0))
        o_blk = pl.BlockSpec(ps, lambda i, s: (0, s[0]))
    r_blk = pl.BlockSpec((3,) + ps, lambda i, s: (0, 0, 0))

    def body(s_ref, a_ref, r_ref, o_ref):
        a = a_ref[0] if t == 0 else a_ref[...]
        o_ref[...] = ((a + r_ref[0].astype(F32)) + r_ref[1].astype(F32)) + r_ref[2].astype(F32)

    return pl.pallas_call(
        body, name=name,
        grid_spec=pltpu.PrefetchScalarGridSpec(
            num_scalar_prefetch=1, grid=(1,), in_specs=[own_blk, r_blk], out_specs=o_blk),
        out_shape=jax.ShapeDtypeStruct(SHARD_SHAPES[t], F32), compiler_params=_cp(),
    )(sc, chipsum, recv3)


FULL_W_SHAPES = [(D, NPROJ), (AW, D), (D, D), (D, D)]


def _cast_place_call(t, shard, sc, name):
    if t == 0:
        blk, steps = (512, 2048), 2
        in_blk = pl.BlockSpec(blk, lambda i, s: (i, 0))
        o_blk = pl.BlockSpec(blk, lambda i, s: (i, s[1]))
    elif t == 1:
        blk, steps = (AW, 256), 1
        in_blk = pl.BlockSpec(blk, lambda i, s: (0, 0))
        o_blk = pl.BlockSpec(blk, lambda i, s: (0, s[1]))
    else:
        blk, steps = (256, D), 1
        in_blk = pl.BlockSpec(blk, lambda i, s: (0, 0))
        o_blk = pl.BlockSpec(blk, lambda i, s: (s[1], 0))

    def body(s_ref, a_ref, o_ref, own_ref):
        v = a_ref[...].astype(BF16)
        o_ref[...] = v
        own_ref[...] = v

    return pl.pallas_call(
        body, name=name,
        grid_spec=pltpu.PrefetchScalarGridSpec(
            num_scalar_prefetch=1, grid=(steps,), in_specs=[in_blk], out_specs=[o_blk, in_blk]),
        out_shape=[jax.ShapeDtypeStruct(FULL_W_SHAPES[t], BF16), jax.ShapeDtypeStruct(shard.shape, BF16)],
        compiler_params=_cp(),
    )(sc, shard)


def _lower_bound_fn(hgrn_lb):
    return jnp.cumsum(jax.nn.softmax(hgrn_lb.astype(F32), axis=0), axis=0)[0]


def kernel(x, c, w_ada, b_ada, norm_g, w_in, hgrn_onorm_g, w_branch_a, w_branch_b, w_out, rel_bias, hgrn_lb, final_g, loss_target, m_w_ada, m_b_ada, m_norm_g, m_w_in, m_hgrn_onorm_g, m_w_branch_a, m_w_branch_b, m_w_out, m_rel_bias, m_hgrn_lb, m_final_g, v_w_ada, v_b_ada, v_norm_g, v_w_in, v_hgrn_onorm_g, v_w_branch_a, v_w_branch_b, v_w_out, v_rel_bias, v_hgrn_lb, v_final_g):
    ax, ay, ac = _me()
    chip = 2 * ax + ay
    dev = 4 * ax + 2 * ay + ac
    sc_idx = jnp.stack([ac, chip]).astype(jnp.int32)

    c_all = _allgather_small(jnp.pad(c, ((0, 7), (0, 0))), "gather_c").reshape(8, 8, D)[:, 0]
    b_s = lax.dynamic_slice(b_ada, (0, chip * 768), (1, 768))
    mod_part = _mod_call(c_all, w_ada[0], b_s)
    mod_all = _allgather_small(mod_part, "gather_mod").reshape(8, 8, 768)
    mod_mine = lax.dynamic_index_in_dim(mod_all, dev, axis=1, keepdims=False)
    mod = mod_mine[0::2].reshape(1, 3 * D)

    names = ["w_in", "w_a", "w_b", "w_o"]
    shards, mod = lax.optimization_barrier(([w_in[0], w_branch_a[0], w_branch_b[0], w_out[0]], mod))
    placed = [_cast_place_call(t, shards[t], sc_idx, "cast_" + names[t]) for t in range(4)]
    w_send_sems, w_recv_sems, w_thru, w_token = _gather_ici_start([p_[0] for p_ in placed])
    mod = mod + w_token[0, 0]
    rel_bias_t = rel_bias + w_token[0, 0]
    shift, scale, gate = mod[:, :D], mod[:, D:2 * D], mod[:, 2 * D:]

    def weights_fn(xs, avec):
        proj, h, h_t = _proj_own_call(xs, avec, placed[0][1], sc_idx)
        arrived = _gather_ici_wait(w_send_sems, w_recv_sems, w_thru[0:1], proj, (0,), "gather_ici_wait_in")
        (win_f,) = _gather_sibling(arrived, (0,), "gather_sibling_in")

        def rest_fn(after):
            late = _gather_ici_wait(w_send_sems, w_recv_sems, w_thru[1:4], after, (1, 2, 3), "gather_ici_wait_rest")
            return _gather_sibling(late, (1, 2, 3), "gather_sibling_rest")

        return h, h_t, _proj_rest_call(h, win_f, proj, sc_idx), win_f, rest_fn

    flight = {}

    def start_reduction(own, own16):
        sib = _reduce_sibling_send(own16)
        halves = [_add_half_call(t, own[t], sib[t], sc_idx, "chipsum_" + names[t]) for t in range(4)]
        send_sems, recv_sems, thru, token = _reduce_chips_start([hb for _, hb in halves])
        flight.update(sems=(send_sems, recv_sems), thru=thru, sums=[hf for hf, _ in halves])
        return token[0, 0]

    lb, lb_vjp = jax.vjp(_lower_bound_fn, hgrn_lb)
    grad_x, d_win, d_wa, d_wb, d_wo, pack = _local_step(
        x[0], loss_target[0], shift, scale, gate, norm_g, hgrn_onorm_g, rel_bias_t, lb[None, :],
        final_g[None, :], weights_fn, hook=start_reduction)

    rec = _reduce_chips_wait(*flight["sems"], flight["thru"], pack)
    pieces = [_final_piece_call(t, flight["sums"][t], rec[t], sc_idx, "piece_" + names[t]) for t in range(4)]
    g_win, g_wa, g_wb, g_wo = _share_sibling(pieces)

    packs = _allgather_small(pack, "gather_small").reshape(8, 16, D)
    tot = _sum8_call(packs)
    loss = tot[7, 0]
    g_b_ada = tot[0:3].reshape(1, 3 * D)
    g_norm_g = tot[3:4]
    g_final_g = tot[4]
    (g_hgrn_lb,) = lb_vjp(tot[5])
    g_onorm = tot[6:7, :HK]
    g_rel = tot[8, :NBUCKETS * NH].reshape(NBUCKETS, NH)

    def rows_of(a):
        flat = a.reshape(-1)
        n = -(-flat.shape[0] // D)
        return jnp.pad(flat, (0, n * D - flat.shape[0])).reshape(n, D)

    smalls = [(b_ada, g_b_ada, m_b_ada, v_b_ada), (norm_g, g_norm_g, m_norm_g, v_norm_g),
              (hgrn_onorm_g, g_onorm, m_hgrn_onorm_g, v_hgrn_onorm_g), (rel_bias, g_rel, m_rel_bias, v_rel_bias),
              (hgrn_lb, g_hgrn_lb, m_hgrn_lb, v_hgrn_lb), (final_g, g_final_g, m_final_g, v_final_g)]
    cat = [jnp.concatenate([rows_of(s[k]) for s in smalls], 0) for k in range(4)]
    cat = [jnp.pad(a, ((0, 16 - a.shape[0]), (0, 0))) for a in cat]
    _, sd, sm, sv = _adamw_call(*cat, "adamw_small")

    def unpack(packed):
        res, r = [], 0
        for s in smalls:
            n = -(-s[0].size // D)
            res.append(packed[r:r + n].reshape(-1)[:s[0].size].reshape(s[0].shape))
            r += n
        return res

    d_small, m_small, v_small = unpack(sd), unpack(sm), unpack(sv)

    sc_all = c_all * jax.nn.sigmoid(c_all)
    dmod_all = packs[:, 0:3].reshape(8, 3 * D)
    dm_s = lax.dynamic_slice(dmod_all, (0, chip * 768), (8, 768))
    g_w_ada, d_w_ada, nm_w_ada, nv_w_ada = _ada_update_call(sc_all.T, dm_s, w_ada[0], m_w_ada[0], v_w_ada[0])

    big = [_adamw_call(w_in[0], g_win, m_w_in[0], v_w_in[0], "adamw_w_in")]
    big += _adamw_many_call([(w_branch_a[0], g_wa, m_w_branch_a[0], v_w_branch_a[0]),
                             (w_branch_b[0], g_wb, m_w_branch_b[0], v_w_branch_b[0]),
                             (w_out[0], g_wo, m_w_out[0], v_w_out[0])], "adamw_branch3")

    e = lambda a: a[None]
    grads = [e(g_w_ada), g_b_ada, g_norm_g, e(big[0][0]), g_onorm, e(big[1][0]), e(big[2][0]), e(big[3][0]),
             g_rel, g_hgrn_lb, g_final_g]
    deltas = [e(d_w_ada), d_small[0], d_small[1], e(big[0][1]), d_small[2], e(big[1][1]), e(big[2][1]), e(big[3][1]),
              d_small[3], d_small[4], d_small[5]]
    new_m = [e(nm_w_ada), m_small[0], m_small[1], e(big[0][2]), m_small[2], e(big[1][2]), e(big[2][2]), e(big[3][2]),
             m_small[3], m_small[4], m_small[5]]
    new_v = [e(nv_w_ada), v_small[0], v_small[1], e(big[0][3]), v_small[2], e(big[1][3]), e(big[2][3]), e(big[3][3]),
             v_small[3], v_small[4], v_small[5]]
    return (loss, grad_x[None], *grads, *deltas, *new_m, *new_v)
```
